```python
import math
import jax, jax.numpy as jnp
from jax import lax
import numpy as np

D_MODEL = 1024
BATCH = 8
SEQ = 8192
DEPTH = 4

CHUNK = 64
N_MIXERS = 3
EPS = 1e-6
GLA_HEADS = 4
GLA_DK = D_MODEL // (2 * GLA_HEADS)
GLA_DV = D_MODEL // GLA_HEADS
GLA_RANK = 16
GLA_TAU = 16.0
SSD_DINNER = 2 * D_MODEL
SSD_HEADDIM = 64
SSD_HEADS = SSD_DINNER // SSD_HEADDIM
SSD_GROUPS = 8
SSD_HPG = SSD_HEADS // SSD_GROUPS
SSD_DSTATE = 128
SSD_CONV = 4
S5_GROUP = 16
S5_GROUPS = D_MODEL // S5_GROUP
S5_STATE = 64
FFN_HIDDEN = ((-(-8 * D_MODEL // 3)) + 255) // 256 * 256

kernel_name = 'chunk_causal_hybrid_gla_ssd_s5'


def _layers_of(mixer):
    return len(range(mixer, DEPTH, N_MIXERS))


def rmsnorm(x, g):
    xf = x.astype(jnp.float32)
    y = xf * lax.rsqrt(jnp.mean(xf * xf, axis=-1, keepdims=True) + EPS)
    return (y * g.astype(jnp.float32)).astype(x.dtype)


def causal_depthwise_conv(x, w, b):
    k = w.shape[0]
    out = lax.conv_general_dilated(
        x, w[:, None, :].astype(x.dtype), window_strides=(1,), padding=[(k - 1, 0)],
        dimension_numbers=('NWC', 'WIO', 'NWC'), feature_group_count=x.shape[-1])
    return out + b.astype(x.dtype)


def gla_mixer(h, w_in, w_a2, b_a, norm_g, w_out):
    bsz, seq, _ = h.shape
    nc = seq // CHUNK
    f32 = jnp.float32
    qk = GLA_HEADS * GLA_DK
    vd = GLA_HEADS * GLA_DV
    proj = h @ w_in
    q, k, v, r, a_low = jnp.split(proj, [qk, 2 * qk, 2 * qk + vd, 2 * qk + 2 * vd], axis=-1)
    log_a = jax.nn.log_sigmoid((a_low @ w_a2 + b_a).astype(f32)) / GLA_TAU

    def chunks(t, d):
        return t.reshape(bsz, nc, CHUNK, GLA_HEADS, d).astype(f32)

    q = chunks(q, GLA_DK) * (GLA_DK ** -0.5)
    k = chunks(k, GLA_DK)
    v = chunks(v, GLA_DV)
    lc = jnp.cumsum(chunks(log_a, GLA_DK), axis=2)
    lend = lc[:, :, -1:]
    q_fwd = q * jnp.exp(lc)
    k_fwd = k * jnp.exp(-lc)
    q_bwd = q * jnp.exp(-lc)
    k_bwd = k * jnp.exp(lc)
    s_past = jnp.einsum('bclhd,bcshd->bchls', q_fwd, k_fwd)
    s_future = jnp.einsum('bclhd,bcshd->bchls', q_bwd, k_bwd)
    past_mask = jnp.tril(jnp.ones((CHUNK, CHUNK), dtype=bool))
    scores = jnp.where(past_mask, s_past, s_future)
    o = jnp.einsum('bchls,bcshv->bclhv', scores, v)
    g_chunk = jnp.exp(lend[:, :, 0])
    d_state = jnp.einsum('bcshd,bcshv->bchdv', k * jnp.exp(lend - lc), v)

    def step(s, inp):
        g, ds = inp
        return g[..., None] * s + ds, s

    s0 = jnp.zeros((bsz, GLA_HEADS, GLA_DK, GLA_DV), f32)
    _, s_prev = lax.scan(step, s0, (jnp.moveaxis(g_chunk, 1, 0), jnp.moveaxis(d_state, 1, 0)))
    s_prev = jnp.moveaxis(s_prev, 0, 1)
    o = o + jnp.einsum('bclhd,bchdv->bclhv', q_fwd, s_prev)
    o = rmsnorm(o.reshape(bsz, seq, GLA_HEADS, GLA_DV), norm_g.reshape(GLA_HEADS, GLA_DV))
    o = o.reshape(bsz, seq, vd) * jax.nn.silu(r.astype(f32))
    return (o @ w_out).astype(h.dtype)


def ssd_mixer(h, w_in, conv_w, conv_b, dt_bias, a_log, d_skip, norm_g, w_out):
    bsz, seq, _ = h.shape
    nc = seq // CHUNK
    f32 = jnp.float32
    gn = SSD_GROUPS * SSD_DSTATE
    proj = h @ w_in
    z, xbc, dt = jnp.split(proj, [SSD_DINNER, 2 * SSD_DINNER + 2 * gn], axis=-1)
    xbc = jax.nn.silu(causal_depthwise_conv(xbc, conv_w, conv_b))
    xs, bm, cm = jnp.split(xbc, [SSD_DINNER, SSD_DINNER + gn], axis=-1)
    dt = jax.nn.softplus(dt.astype(f32) + dt_bias.astype(f32))
    da = dt * (-jnp.exp(a_log.astype(f32)))
    xs = xs.reshape(bsz, nc, CHUNK, SSD_GROUPS, SSD_HPG, SSD_HEADDIM).astype(f32)
    bm = bm.reshape(bsz, nc, CHUNK, SSD_GROUPS, SSD_DSTATE).astype(f32)
    cm = cm.reshape(bsz, nc, CHUNK, SSD_GROUPS, SSD_DSTATE).astype(f32)
    dt = dt.reshape(bsz, nc, CHUNK, SSD_GROUPS, SSD_HPG)
    cum = jnp.cumsum(da.reshape(bsz, nc, CHUNK, SSD_GROUPS, SSD_HPG), axis=2)
    cb = jnp.einsum('bclgn,bcsgn->bcgls', cm, bm)
    cum_h = jnp.moveaxis(cum, 2, -1)
    decay = jnp.exp(-jnp.abs(cum_h[..., :, None] - cum_h[..., None, :]))
    dt_h = jnp.moveaxis(dt, 2, -1)
    mix = cb[:, :, :, None] * decay * dt_h[..., None, :]
    y = jnp.einsum('bcgjls,bcsgjp->bclgjp', mix, xs)
    cum_end = cum[:, :, -1]
    xw = xs * (dt * jnp.exp(cum_end[:, :, None] - cum))[..., None]
    states = jnp.einsum('bcsgn,bcsgjp->bcgjpn', bm, xw)

    def step(hs, inp):
        a, s = inp
        return a[..., None, None] * hs + s, hs

    h0 = jnp.zeros((bsz, SSD_GROUPS, SSD_HPG, SSD_HEADDIM, SSD_DSTATE), f32)
    _, h_prev = lax.scan(step, h0, (jnp.moveaxis(jnp.exp(cum_end), 1, 0), jnp.moveaxis(states, 1, 0)))
    h_prev = jnp.moveaxis(h_prev, 0, 1)
    y = y + jnp.einsum('bclgn,bcgjpn->bclgjp', cm, h_prev) * jnp.exp(cum)[..., None]
    y = y + d_skip.astype(f32).reshape(SSD_GROUPS, SSD_HPG)[:, :, None] * xs
    y = y.reshape(bsz, seq, SSD_DINNER) * jax.nn.silu(z.astype(f32))
    gsz = SSD_DINNER // SSD_GROUPS
    y = rmsnorm(y.reshape(bsz, seq, SSD_GROUPS, gsz), norm_g.reshape(SSD_GROUPS, gsz))
    return (y.reshape(bsz, seq, SSD_DINNER) @ w_out).astype(h.dtype)


def _complex_linear_combine(e1, e2):
    a1r, a1i, b1r, b1i = e1
    a2r, a2i, b2r, b2i = e2
    ar = a2r * a1r - a2i * a1i
    ai = a2r * a1i + a2i * a1r
    br = a2r * b1r - a2i * b1i + b2r
    bi = a2r * b1i + a2i * b1r + b2i
    return ar, ai, br, bi


def s5_mixer(h, log_dt, a_re, a_im, b_re, b_im, c_re, c_im, d_skip, w_glu):
    bsz, seq, _ = h.shape
    f32 = jnp.float32
    u = h.reshape(bsz, seq, S5_GROUPS, S5_GROUP).astype(f32)
    step = jnp.exp(log_dt.astype(f32))[:, None]
    a_re = a_re.astype(f32)
    a_im = a_im.astype(f32)
    mag = jnp.exp(step * a_re)
    abar_re = mag * jnp.cos(step * a_im)
    abar_im = mag * jnp.sin(step * a_im)
    den = a_re * a_re + a_im * a_im
    num_re = abar_re - 1.0
    num_im = abar_im
    f_re = (num_re * a_re + num_im * a_im) / den
    f_im = (num_im * a_re - num_re * a_im) / den
    b_re = b_re.astype(f32)
    b_im = b_im.astype(f32)
    bb_re = f_re[..., None] * b_re - f_im[..., None] * b_im
    bb_im = f_re[..., None] * b_im + f_im[..., None] * b_re
    bu_re = jnp.einsum('gpc,blgc->blgp', bb_re, u)
    bu_im = jnp.einsum('gpc,blgc->blgp', bb_im, u)
    a_seq_re = jnp.broadcast_to(abar_re, bu_re.shape)
    a_seq_im = jnp.broadcast_to(abar_im, bu_im.shape)
    _, _, x_re, x_im = lax.associative_scan(
        _complex_linear_combine, (a_seq_re, a_seq_im, bu_re, bu_im), axis=1)
    y = (jnp.einsum('gcp,blgp->blgc', c_re.astype(f32), x_re)
         - jnp.einsum('gcp,blgp->blgc', c_im.astype(f32), x_im))
    y = y + d_skip.astype(f32).reshape(S5_GROUPS, S5_GROUP) * u
    y = jax.nn.gelu(y.reshape(bsz, seq, D_MODEL))
    val, gate = jnp.split(y @ w_glu, 2, axis=-1)
    return (val * jax.nn.sigmoid(gate)).astype(h.dtype)


def swiglu_ffn(h, w_gu, w_down):
    g, u = jnp.split(h @ w_gu, 2, axis=-1)
    return (jax.nn.silu(g) * u) @ w_down


def _fwd_setup_inputs(seed: int = 0) -> dict:
    key = jax.random.key(seed)
    ks = iter(jax.random.split(key, 48))
    f32 = jnp.float32

    def nrm(shape, scale):
        return jax.random.normal(next(ks), shape, f32) * scale

    n_gla, n_ssd, n_s5 = _layers_of(0), _layers_of(1), _layers_of(2)
    qk = GLA_HEADS * GLA_DK
    vd = GLA_HEADS * GLA_DV
    gla_in = 2 * qk + 2 * vd + GLA_RANK
    gn = SSD_GROUPS * SSD_DSTATE
    ssd_conv_dim = SSD_DINNER + 2 * gn
    ssd_in = SSD_DINNER + ssd_conv_dim + SSD_HEADS

    x = jax.random.normal(next(ks), (BATCH, SEQ, D_MODEL), f32)
    norm_mix_g = 1.0 + nrm((DEPTH, D_MODEL), 0.01)
    norm_ffn_g = 1.0 + nrm((DEPTH, D_MODEL), 0.01)
    gla_w_in = nrm((n_gla, D_MODEL, gla_in), D_MODEL ** -0.5)
    gla_w_a2 = nrm((n_gla, GLA_RANK, qk), GLA_RANK ** -0.5)
    gla_b_a = nrm((n_gla, qk), 0.1)
    gla_norm_g = 1.0 + nrm((n_gla, vd), 0.01)
    gla_w_out = nrm((n_gla, vd, D_MODEL), vd ** -0.5)
    ssd_w_in = nrm((n_ssd, D_MODEL, ssd_in), D_MODEL ** -0.5)
    ssd_conv_w = nrm((n_ssd, SSD_CONV, ssd_conv_dim), SSD_CONV ** -0.5)
    ssd_conv_b = nrm((n_ssd, ssd_conv_dim), 0.02)
    dt0 = jnp.exp(jax.random.uniform(next(ks), (n_ssd, SSD_HEADS), f32, math.log(1e-3), math.log(1e-1)))
    ssd_dt_bias = dt0 + jnp.log(-jnp.expm1(-dt0))
    ssd_a_log = jnp.log(jax.random.uniform(next(ks), (n_ssd, SSD_HEADS), f32, 1.0, 16.0))
    ssd_d = 1.0 + nrm((n_ssd, SSD_HEADS), 0.1)
    ssd_norm_g = 1.0 + nrm((n_ssd, SSD_DINNER), 0.01)
    ssd_w_out = nrm((n_ssd, SSD_DINNER, D_MODEL), SSD_DINNER ** -0.5)
    s5_log_dt = jax.random.uniform(next(ks), (n_s5, S5_GROUPS), f32, math.log(1e-3), math.log(1e-1))
    n_idx = jnp.arange(S5_STATE, dtype=f32)
    s5_a_re = -0.5 + nrm((n_s5, S5_GROUPS, S5_STATE), 0.01)
    s5_a_im = jnp.pi * n_idx + nrm((n_s5, S5_GROUPS, S5_STATE), 0.01)
    bscale = (2.0 * S5_GROUP) ** -0.5
    s5_b_re = nrm((n_s5, S5_GROUPS, S5_STATE, S5_GROUP), bscale)
    s5_b_im = nrm((n_s5, S5_GROUPS, S5_STATE, S5_GROUP), bscale)
    cscale = (2.0 * S5_STATE) ** -0.5
    s5_c_re = nrm((n_s5, S5_GROUPS, S5_GROUP, S5_STATE), cscale)
    s5_c_im = nrm((n_s5, S5_GROUPS, S5_GROUP, S5_STATE), cscale)
    s5_d = 1.0 + nrm((n_s5, D_MODEL), 0.1)
    s5_w_glu = nrm((n_s5, D_MODEL, 2 * D_MODEL), D_MODEL ** -0.5)
    ffn_w_gu = nrm((DEPTH, D_MODEL, 2 * FFN_HIDDEN), D_MODEL ** -0.5)
    ffn_w_down = nrm((DEPTH, FFN_HIDDEN, D_MODEL), FFN_HIDDEN ** -0.5)
    final_norm_g = 1.0 + nrm((D_MODEL,), 0.01)
    return {
        'x': x, 'norm_mix_g': norm_mix_g, 'norm_ffn_g': norm_ffn_g,
        'gla_w_in': gla_w_in, 'gla_w_a2': gla_w_a2, 'gla_b_a': gla_b_a,
        'gla_norm_g': gla_norm_g, 'gla_w_out': gla_w_out,
        'ssd_w_in': ssd_w_in, 'ssd_conv_w': ssd_conv_w, 'ssd_conv_b': ssd_conv_b,
        'ssd_dt_bias': ssd_dt_bias, 'ssd_a_log': ssd_a_log, 'ssd_d': ssd_d,
        'ssd_norm_g': ssd_norm_g, 'ssd_w_out': ssd_w_out,
        's5_log_dt': s5_log_dt, 's5_a_re': s5_a_re, 's5_a_im': s5_a_im,
        's5_b_re': s5_b_re, 's5_b_im': s5_b_im, 's5_c_re': s5_c_re, 's5_c_im': s5_c_im,
        's5_d': s5_d, 's5_w_glu': s5_w_glu,
        'ffn_w_gu': ffn_w_gu, 'ffn_w_down': ffn_w_down, 'final_norm_g': final_norm_g,
    }


def _fwd_reference(x, norm_mix_g, norm_ffn_g,
              gla_w_in, gla_w_a2, gla_b_a, gla_norm_g, gla_w_out,
              ssd_w_in, ssd_conv_w, ssd_conv_b, ssd_dt_bias, ssd_a_log, ssd_d, ssd_norm_g, ssd_w_out,
              s5_log_dt, s5_a_re, s5_a_im, s5_b_re, s5_b_im, s5_c_re, s5_c_im, s5_d, s5_w_glu,
              ffn_w_gu, ffn_w_down, final_norm_g):
    h = x
    for i in range(DEPTH):
        mixer, j = i % N_MIXERS, i // N_MIXERS
        hn = rmsnorm(h, norm_mix_g[i])
        if mixer == 0:
            y = gla_mixer(hn, gla_w_in[j], gla_w_a2[j], gla_b_a[j], gla_norm_g[j], gla_w_out[j])
        elif mixer == 1:
            y = ssd_mixer(hn, ssd_w_in[j], ssd_conv_w[j], ssd_conv_b[j], ssd_dt_bias[j],
                          ssd_a_log[j], ssd_d[j], ssd_norm_g[j], ssd_w_out[j])
        else:
            y = s5_mixer(hn, s5_log_dt[j], s5_a_re[j], s5_a_im[j], s5_b_re[j], s5_b_im[j],
                         s5_c_re[j], s5_c_im[j], s5_d[j], s5_w_glu[j])
        h = h + y.astype(h.dtype)
        h = h + swiglu_ffn(rmsnorm(h, norm_ffn_g[i]), ffn_w_gu[i], ffn_w_down[i]).astype(h.dtype)
    return rmsnorm(h, final_norm_g)


import jax as _jax
import jax.numpy as _jnp

TWIN_FORMAT = 'train_step'
FWD_PARAMS = ['x', 'norm_mix_g', 'norm_ffn_g', 'gla_w_in', 'gla_w_a2', 'gla_b_a', 'gla_norm_g', 'gla_w_out', 'ssd_w_in', 'ssd_conv_w', 'ssd_conv_b', 'ssd_dt_bias', 'ssd_a_log', 'ssd_d', 'ssd_norm_g', 'ssd_w_out', 's5_log_dt', 's5_a_re', 's5_a_im', 's5_b_re', 's5_b_im', 's5_c_re', 's5_c_im', 's5_d', 's5_w_glu', 'ffn_w_gu', 'ffn_w_down', 'final_norm_g']
TWIN_WEIGHTS = ['norm_mix_g', 'norm_ffn_g', 'gla_w_in', 'gla_w_a2', 'gla_b_a', 'gla_norm_g', 'gla_w_out', 'ssd_w_in', 'ssd_conv_w', 'ssd_conv_b', 'ssd_dt_bias', 'ssd_a_log', 'ssd_d', 'ssd_norm_g', 'ssd_w_out', 's5_log_dt', 's5_a_re', 's5_a_im', 's5_b_re', 's5_b_im', 's5_c_re', 's5_c_im', 's5_d', 's5_w_glu', 'ffn_w_gu', 'ffn_w_down', 'final_norm_g']
TWIN_DIFF_INPUT = 'x'
TWIN_INPUTS = ['x', 'norm_mix_g', 'norm_ffn_g', 'gla_w_in', 'gla_w_a2', 'gla_b_a', 'gla_norm_g', 'gla_w_out', 'ssd_w_in', 'ssd_conv_w', 'ssd_conv_b', 'ssd_dt_bias', 'ssd_a_log', 'ssd_d', 'ssd_norm_g', 'ssd_w_out', 's5_log_dt', 's5_a_re', 's5_a_im', 's5_b_re', 's5_b_im', 's5_c_re', 's5_c_im', 's5_d', 's5_w_glu', 'ffn_w_gu', 'ffn_w_down', 'final_norm_g', 'loss_target', 'm_norm_mix_g', 'm_norm_ffn_g', 'm_gla_w_in', 'm_gla_w_a2', 'm_gla_b_a', 'm_gla_norm_g', 'm_gla_w_out', 'm_ssd_w_in', 'm_ssd_conv_w', 'm_ssd_conv_b', 'm_ssd_dt_bias', 'm_ssd_a_log', 'm_ssd_d', 'm_ssd_norm_g', 'm_ssd_w_out', 'm_s5_log_dt', 'm_s5_a_re', 'm_s5_a_im', 'm_s5_b_re', 'm_s5_b_im', 'm_s5_c_re', 'm_s5_c_im', 'm_s5_d', 'm_s5_w_glu', 'm_ffn_w_gu', 'm_ffn_w_down', 'm_final_norm_g', 'v_norm_mix_g', 'v_norm_ffn_g', 'v_gla_w_in', 'v_gla_w_a2', 'v_gla_b_a', 'v_gla_norm_g', 'v_gla_w_out', 'v_ssd_w_in', 'v_ssd_conv_w', 'v_ssd_conv_b', 'v_ssd_dt_bias', 'v_ssd_a_log', 'v_ssd_d', 'v_ssd_norm_g', 'v_ssd_w_out', 'v_s5_log_dt', 'v_s5_a_re', 'v_s5_a_im', 'v_s5_b_re', 'v_s5_b_im', 'v_s5_c_re', 'v_s5_c_im', 'v_s5_d', 'v_s5_w_glu', 'v_ffn_w_gu', 'v_ffn_w_down', 'v_final_norm_g']
TWIN_OUTPUTS = ['loss', 'grad_x', 'grad_norm_mix_g', 'grad_norm_ffn_g', 'grad_gla_w_in', 'grad_gla_w_a2', 'grad_gla_b_a', 'grad_gla_norm_g', 'grad_gla_w_out', 'grad_ssd_w_in', 'grad_ssd_conv_w', 'grad_ssd_conv_b', 'grad_ssd_dt_bias', 'grad_ssd_a_log', 'grad_ssd_d', 'grad_ssd_norm_g', 'grad_ssd_w_out', 'grad_s5_log_dt', 'grad_s5_a_re', 'grad_s5_a_im', 'grad_s5_b_re', 'grad_s5_b_im', 'grad_s5_c_re', 'grad_s5_c_im', 'grad_s5_d', 'grad_s5_w_glu', 'grad_ffn_w_gu', 'grad_ffn_w_down', 'grad_final_norm_g', 'delta_norm_mix_g', 'delta_norm_ffn_g', 'delta_gla_w_in', 'delta_gla_w_a2', 'delta_gla_b_a', 'delta_gla_norm_g', 'delta_gla_w_out', 'delta_ssd_w_in', 'delta_ssd_conv_w', 'delta_ssd_conv_b', 'delta_ssd_dt_bias', 'delta_ssd_a_log', 'delta_ssd_d', 'delta_ssd_norm_g', 'delta_ssd_w_out', 'delta_s5_log_dt', 'delta_s5_a_re', 'delta_s5_a_im', 'delta_s5_b_re', 'delta_s5_b_im', 'delta_s5_c_re', 'delta_s5_c_im', 'delta_s5_d', 'delta_s5_w_glu', 'delta_ffn_w_gu', 'delta_ffn_w_down', 'delta_final_norm_g', 'new_m_norm_mix_g', 'new_m_norm_ffn_g', 'new_m_gla_w_in', 'new_m_gla_w_a2', 'new_m_gla_b_a', 'new_m_gla_norm_g', 'new_m_gla_w_out', 'new_m_ssd_w_in', 'new_m_ssd_conv_w', 'new_m_ssd_conv_b', 'new_m_ssd_dt_bias', 'new_m_ssd_a_log', 'new_m_ssd_d', 'new_m_ssd_norm_g', 'new_m_ssd_w_out', 'new_m_s5_log_dt', 'new_m_s5_a_re', 'new_m_s5_a_im', 'new_m_s5_b_re', 'new_m_s5_b_im', 'new_m_s5_c_re', 'new_m_s5_c_im', 'new_m_s5_d', 'new_m_s5_w_glu', 'new_m_ffn_w_gu', 'new_m_ffn_w_down', 'new_m_final_norm_g', 'new_v_norm_mix_g', 'new_v_norm_ffn_g', 'new_v_gla_w_in', 'new_v_gla_w_a2', 'new_v_gla_b_a', 'new_v_gla_norm_g', 'new_v_gla_w_out', 'new_v_ssd_w_in', 'new_v_ssd_conv_w', 'new_v_ssd_conv_b', 'new_v_ssd_dt_bias', 'new_v_ssd_a_log', 'new_v_ssd_d', 'new_v_ssd_norm_g', 'new_v_ssd_w_out', 'new_v_s5_log_dt', 'new_v_s5_a_re', 'new_v_s5_a_im', 'new_v_s5_b_re', 'new_v_s5_b_im', 'new_v_s5_c_re', 'new_v_s5_c_im', 'new_v_s5_d', 'new_v_s5_w_glu', 'new_v_ffn_w_gu', 'new_v_ffn_w_down', 'new_v_final_norm_g']
TWIN_LEAF_KINDS = {'loss': 'loss', 'grad_x': 'grad_x', 'grad_norm_mix_g': 'grad_w', 'grad_norm_ffn_g': 'grad_w', 'grad_gla_w_in': 'grad_w', 'grad_gla_w_a2': 'grad_w', 'grad_gla_b_a': 'grad_w', 'grad_gla_norm_g': 'grad_w', 'grad_gla_w_out': 'grad_w', 'grad_ssd_w_in': 'grad_w', 'grad_ssd_conv_w': 'grad_w', 'grad_ssd_conv_b': 'grad_w', 'grad_ssd_dt_bias': 'grad_w', 'grad_ssd_a_log': 'grad_w', 'grad_ssd_d': 'grad_w', 'grad_ssd_norm_g': 'grad_w', 'grad_ssd_w_out': 'grad_w', 'grad_s5_log_dt': 'grad_w', 'grad_s5_a_re': 'grad_w', 'grad_s5_a_im': 'grad_w', 'grad_s5_b_re': 'grad_w', 'grad_s5_b_im': 'grad_w', 'grad_s5_c_re': 'grad_w', 'grad_s5_c_im': 'grad_w', 'grad_s5_d': 'grad_w', 'grad_s5_w_glu': 'grad_w', 'grad_ffn_w_gu': 'grad_w', 'grad_ffn_w_down': 'grad_w', 'grad_final_norm_g': 'grad_w', 'delta_norm_mix_g': 'delta_w', 'delta_norm_ffn_g': 'delta_w', 'delta_gla_w_in': 'delta_w', 'delta_gla_w_a2': 'delta_w', 'delta_gla_b_a': 'delta_w', 'delta_gla_norm_g': 'delta_w', 'delta_gla_w_out': 'delta_w', 'delta_ssd_w_in': 'delta_w', 'delta_ssd_conv_w': 'delta_w', 'delta_ssd_conv_b': 'delta_w', 'delta_ssd_dt_bias': 'delta_w', 'delta_ssd_a_log': 'delta_w', 'delta_ssd_d': 'delta_w', 'delta_ssd_norm_g': 'delta_w', 'delta_ssd_w_out': 'delta_w', 'delta_s5_log_dt': 'delta_w', 'delta_s5_a_re': 'delta_w', 'delta_s5_a_im': 'delta_w', 'delta_s5_b_re': 'delta_w', 'delta_s5_b_im': 'delta_w', 'delta_s5_c_re': 'delta_w', 'delta_s5_c_im': 'delta_w', 'delta_s5_d': 'delta_w', 'delta_s5_w_glu': 'delta_w', 'delta_ffn_w_gu': 'delta_w', 'delta_ffn_w_down': 'delta_w', 'delta_final_norm_g': 'delta_w', 'new_m_norm_mix_g': 'new_m', 'new_m_norm_ffn_g': 'new_m', 'new_m_gla_w_in': 'new_m', 'new_m_gla_w_a2': 'new_m', 'new_m_gla_b_a': 'new_m', 'new_m_gla_norm_g': 'new_m', 'new_m_gla_w_out': 'new_m', 'new_m_ssd_w_in': 'new_m', 'new_m_ssd_conv_w': 'new_m', 'new_m_ssd_conv_b': 'new_m', 'new_m_ssd_dt_bias': 'new_m', 'new_m_ssd_a_log': 'new_m', 'new_m_ssd_d': 'new_m', 'new_m_ssd_norm_g': 'new_m', 'new_m_ssd_w_out': 'new_m', 'new_m_s5_log_dt': 'new_m', 'new_m_s5_a_re': 'new_m', 'new_m_s5_a_im': 'new_m', 'new_m_s5_b_re': 'new_m', 'new_m_s5_b_im': 'new_m', 'new_m_s5_c_re': 'new_m', 'new_m_s5_c_im': 'new_m', 'new_m_s5_d': 'new_m', 'new_m_s5_w_glu': 'new_m', 'new_m_ffn_w_gu': 'new_m', 'new_m_ffn_w_down': 'new_m', 'new_m_final_norm_g': 'new_m', 'new_v_norm_mix_g': 'new_v', 'new_v_norm_ffn_g': 'new_v', 'new_v_gla_w_in': 'new_v', 'new_v_gla_w_a2': 'new_v', 'new_v_gla_b_a': 'new_v', 'new_v_gla_norm_g': 'new_v', 'new_v_gla_w_out': 'new_v', 'new_v_ssd_w_in': 'new_v', 'new_v_ssd_conv_w': 'new_v', 'new_v_ssd_conv_b': 'new_v', 'new_v_ssd_dt_bias': 'new_v', 'new_v_ssd_a_log': 'new_v', 'new_v_ssd_d': 'new_v', 'new_v_ssd_norm_g': 'new_v', 'new_v_ssd_w_out': 'new_v', 'new_v_s5_log_dt': 'new_v', 'new_v_s5_a_re': 'new_v', 'new_v_s5_a_im': 'new_v', 'new_v_s5_b_re': 'new_v', 'new_v_s5_b_im': 'new_v', 'new_v_s5_c_re': 'new_v', 'new_v_s5_c_im': 'new_v', 'new_v_s5_d': 'new_v', 'new_v_s5_w_glu': 'new_v', 'new_v_ffn_w_gu': 'new_v', 'new_v_ffn_w_down': 'new_v', 'new_v_final_norm_g': 'new_v'}


def _forward(args):
    return _fwd_reference(*[args[k] for k in FWD_PARAMS])


def _output_shape():
    def fwd():
        inp = _fwd_setup_inputs(0)
        return _fwd_reference(*[inp[k] for k in FWD_PARAMS])
    out = _jax.eval_shape(fwd)
    return out.shape, out.dtype

N_MICROBATCH = 1
ADAM_LR = 0.001
ADAM_B1 = 0.9
ADAM_B2 = 0.999
ADAM_EPS = 1e-08
ADAM_WD = 0.01
ADAM_STEP = 10
PER_EXAMPLE_BATCH_AXIS = {'x': 0, 'loss_target': 0}
SHARED_INPUTS = []
_WEIGHT_DTYPES = {'norm_mix_g': _jnp.float32, 'norm_ffn_g': _jnp.float32, 'gla_w_in': _jnp.float32, 'gla_w_a2': _jnp.float32, 'gla_b_a': _jnp.float32, 'gla_norm_g': _jnp.float32, 'gla_w_out': _jnp.float32, 'ssd_w_in': _jnp.float32, 'ssd_conv_w': _jnp.float32, 'ssd_conv_b': _jnp.float32, 'ssd_dt_bias': _jnp.float32, 'ssd_a_log': _jnp.float32, 'ssd_d': _jnp.float32, 'ssd_norm_g': _jnp.float32, 'ssd_w_out': _jnp.float32, 's5_log_dt': _jnp.float32, 's5_a_re': _jnp.float32, 's5_a_im': _jnp.float32, 's5_b_re': _jnp.float32, 's5_b_im': _jnp.float32, 's5_c_re': _jnp.float32, 's5_c_im': _jnp.float32, 's5_d': _jnp.float32, 's5_w_glu': _jnp.float32, 'ffn_w_gu': _jnp.float32, 'ffn_w_down': _jnp.float32, 'final_norm_g': _jnp.float32}
MOMENT_SCALE = {'norm_mix_g': 2.971948e-01, 'norm_ffn_g': 1.764178e-01, 'gla_w_in': 1.976275e-01, 'gla_w_a2': 2.575029e-02, 'gla_b_a': 9.539219e-02, 'gla_norm_g': 1.693281e-01, 'gla_w_out': 1.681456e-01, 'ssd_w_in': 1.209950e-01, 'ssd_conv_w': 1.048717e-01, 'ssd_conv_b': 1.595578e-01, 'ssd_dt_bias': 2.983265e-01, 'ssd_a_log': 3.576598e-01, 'ssd_d': 6.123354e-01, 'ssd_norm_g': 1.512275e-01, 'ssd_w_out': 2.028818e-01, 's5_log_dt': 2.119011e+00, 's5_a_re': 5.146876e-03, 's5_a_im': 4.300007e-03, 's5_b_re': 2.595136e-03, 's5_b_im': 2.542821e-03, 's5_c_re': 5.116817e-03, 's5_c_im': 4.942638e-03, 's5_d': 7.931844e-02, 's5_w_glu': 4.574362e-02, 'ffn_w_gu': 7.367875e-02, 'ffn_w_down': 1.201962e-01, 'final_norm_g': 6.396264e+01}


def _to_microbatches(a, axis):
    t = _jnp.moveaxis(a, axis, 0)
    t = t.reshape((N_MICROBATCH, t.shape[0] // N_MICROBATCH) + t.shape[1:])
    return _jnp.moveaxis(t, 1, axis + 1)


def setup_inputs(seed: int = 0) -> dict:
    inp = _fwd_setup_inputs(seed)
    key = _jax.random.fold_in(_jax.random.key(seed), 7919)
    shape, _ = _output_shape()
    out = dict(inp)
    out["loss_target"] = _jax.random.normal(_jax.random.fold_in(key, 0), shape, _jnp.float32)
    for i, name in enumerate(TWIN_WEIGHTS):
        w = inp[name].astype(_jnp.float32)
        if MOMENT_SCALE is None:
            s = _jnp.sqrt(_jnp.mean(_jnp.square(w)) + 1e-30)
        else:
            s = MOMENT_SCALE[name]
        km, kv = _jax.random.split(_jax.random.fold_in(key, i + 1))
        out[name] = w
        out["m_" + name] = s * _jax.random.normal(km, w.shape, _jnp.float32)
        out["v_" + name] = (s * s) * _jax.random.uniform(kv, w.shape, _jnp.float32, 0.5, 1.5)
    if N_MICROBATCH > 1:
        for name, axis in PER_EXAMPLE_BATCH_AXIS.items():
            out[name] = _to_microbatches(out[name], axis)
    return {'x': out['x'], 'norm_mix_g': out['norm_mix_g'], 'norm_ffn_g': out['norm_ffn_g'], 'gla_w_in': out['gla_w_in'], 'gla_w_a2': out['gla_w_a2'], 'gla_b_a': out['gla_b_a'], 'gla_norm_g': out['gla_norm_g'], 'gla_w_out': out['gla_w_out'], 'ssd_w_in': out['ssd_w_in'], 'ssd_conv_w': out['ssd_conv_w'], 'ssd_conv_b': out['ssd_conv_b'], 'ssd_dt_bias': out['ssd_dt_bias'], 'ssd_a_log': out['ssd_a_log'], 'ssd_d': out['ssd_d'], 'ssd_norm_g': out['ssd_norm_g'], 'ssd_w_out': out['ssd_w_out'], 's5_log_dt': out['s5_log_dt'], 's5_a_re': out['s5_a_re'], 's5_a_im': out['s5_a_im'], 's5_b_re': out['s5_b_re'], 's5_b_im': out['s5_b_im'], 's5_c_re': out['s5_c_re'], 's5_c_im': out['s5_c_im'], 's5_d': out['s5_d'], 's5_w_glu': out['s5_w_glu'], 'ffn_w_gu': out['ffn_w_gu'], 'ffn_w_down': out['ffn_w_down'], 'final_norm_g': out['final_norm_g'], 'loss_target': out['loss_target'], 'm_norm_mix_g': out['m_norm_mix_g'], 'm_norm_ffn_g': out['m_norm_ffn_g'], 'm_gla_w_in': out['m_gla_w_in'], 'm_gla_w_a2': out['m_gla_w_a2'], 'm_gla_b_a': out['m_gla_b_a'], 'm_gla_norm_g': out['m_gla_norm_g'], 'm_gla_w_out': out['m_gla_w_out'], 'm_ssd_w_in': out['m_ssd_w_in'], 'm_ssd_conv_w': out['m_ssd_conv_w'], 'm_ssd_conv_b': out['m_ssd_conv_b'], 'm_ssd_dt_bias': out['m_ssd_dt_bias'], 'm_ssd_a_log': out['m_ssd_a_log'], 'm_ssd_d': out['m_ssd_d'], 'm_ssd_norm_g': out['m_ssd_norm_g'], 'm_ssd_w_out': out['m_ssd_w_out'], 'm_s5_log_dt': out['m_s5_log_dt'], 'm_s5_a_re': out['m_s5_a_re'], 'm_s5_a_im': out['m_s5_a_im'], 'm_s5_b_re': out['m_s5_b_re'], 'm_s5_b_im': out['m_s5_b_im'], 'm_s5_c_re': out['m_s5_c_re'], 'm_s5_c_im': out['m_s5_c_im'], 'm_s5_d': out['m_s5_d'], 'm_s5_w_glu': out['m_s5_w_glu'], 'm_ffn_w_gu': out['m_ffn_w_gu'], 'm_ffn_w_down': out['m_ffn_w_down'], 'm_final_norm_g': out['m_final_norm_g'], 'v_norm_mix_g': out['v_norm_mix_g'], 'v_norm_ffn_g': out['v_norm_ffn_g'], 'v_gla_w_in': out['v_gla_w_in'], 'v_gla_w_a2': out['v_gla_w_a2'], 'v_gla_b_a': out['v_gla_b_a'], 'v_gla_norm_g': out['v_gla_norm_g'], 'v_gla_w_out': out['v_gla_w_out'], 'v_ssd_w_in': out['v_ssd_w_in'], 'v_ssd_conv_w': out['v_ssd_conv_w'], 'v_ssd_conv_b': out['v_ssd_conv_b'], 'v_ssd_dt_bias': out['v_ssd_dt_bias'], 'v_ssd_a_log': out['v_ssd_a_log'], 'v_ssd_d': out['v_ssd_d'], 'v_ssd_norm_g': out['v_ssd_norm_g'], 'v_ssd_w_out': out['v_ssd_w_out'], 'v_s5_log_dt': out['v_s5_log_dt'], 'v_s5_a_re': out['v_s5_a_re'], 'v_s5_a_im': out['v_s5_a_im'], 'v_s5_b_re': out['v_s5_b_re'], 'v_s5_b_im': out['v_s5_b_im'], 'v_s5_c_re': out['v_s5_c_re'], 'v_s5_c_im': out['v_s5_c_im'], 'v_s5_d': out['v_s5_d'], 'v_s5_w_glu': out['v_s5_w_glu'], 'v_ffn_w_gu': out['v_ffn_w_gu'], 'v_ffn_w_down': out['v_ffn_w_down'], 'v_final_norm_g': out['v_final_norm_g']}


def _loss(weights, diff, rest, loss_target):
    with _jax.named_scope("forward"):
        args = {**rest, TWIN_DIFF_INPUT: diff, **{k: w.astype(_WEIGHT_DTYPES[k]) for k, w in weights.items()}}
        y = _forward(args)
    with _jax.named_scope("loss_head"):
        err = _jnp.square(y.astype(_jnp.float32) - loss_target)
        return 0.5 * _jnp.sum(_jnp.mean(err, axis=-1)) if err.ndim else 0.5 * err


def _adamw(w, g, m, v):
    m = ADAM_B1 * m + (1.0 - ADAM_B1) * g
    v = ADAM_B2 * v + (1.0 - ADAM_B2) * _jnp.square(g)
    m_hat = m / (1.0 - ADAM_B1 ** ADAM_STEP)
    v_hat = v / (1.0 - ADAM_B2 ** ADAM_STEP)
    delta = -ADAM_LR * (m_hat / (_jnp.sqrt(v_hat) + ADAM_EPS) + ADAM_WD * w)
    return delta, m, v


def reference(x, norm_mix_g, norm_ffn_g, gla_w_in, gla_w_a2, gla_b_a, gla_norm_g, gla_w_out, ssd_w_in, ssd_conv_w, ssd_conv_b, ssd_dt_bias, ssd_a_log, ssd_d, ssd_norm_g, ssd_w_out, s5_log_dt, s5_a_re, s5_a_im, s5_b_re, s5_b_im, s5_c_re, s5_c_im, s5_d, s5_w_glu, ffn_w_gu, ffn_w_down, final_norm_g, loss_target, m_norm_mix_g, m_norm_ffn_g, m_gla_w_in, m_gla_w_a2, m_gla_b_a, m_gla_norm_g, m_gla_w_out, m_ssd_w_in, m_ssd_conv_w, m_ssd_conv_b, m_ssd_dt_bias, m_ssd_a_log, m_ssd_d, m_ssd_norm_g, m_ssd_w_out, m_s5_log_dt, m_s5_a_re, m_s5_a_im, m_s5_b_re, m_s5_b_im, m_s5_c_re, m_s5_c_im, m_s5_d, m_s5_w_glu, m_ffn_w_gu, m_ffn_w_down, m_final_norm_g, v_norm_mix_g, v_norm_ffn_g, v_gla_w_in, v_gla_w_a2, v_gla_b_a, v_gla_norm_g, v_gla_w_out, v_ssd_w_in, v_ssd_conv_w, v_ssd_conv_b, v_ssd_dt_bias, v_ssd_a_log, v_ssd_d, v_ssd_norm_g, v_ssd_w_out, v_s5_log_dt, v_s5_a_re, v_s5_a_im, v_s5_b_re, v_s5_b_im, v_s5_c_re, v_s5_c_im, v_s5_d, v_s5_w_glu, v_ffn_w_gu, v_ffn_w_down, v_final_norm_g):
    given = dict(x=x, norm_mix_g=norm_mix_g, norm_ffn_g=norm_ffn_g, gla_w_in=gla_w_in, gla_w_a2=gla_w_a2, gla_b_a=gla_b_a, gla_norm_g=gla_norm_g, gla_w_out=gla_w_out, ssd_w_in=ssd_w_in, ssd_conv_w=ssd_conv_w, ssd_conv_b=ssd_conv_b, ssd_dt_bias=ssd_dt_bias, ssd_a_log=ssd_a_log, ssd_d=ssd_d, ssd_norm_g=ssd_norm_g, ssd_w_out=ssd_w_out, s5_log_dt=s5_log_dt, s5_a_re=s5_a_re, s5_a_im=s5_a_im, s5_b_re=s5_b_re, s5_b_im=s5_b_im, s5_c_re=s5_c_re, s5_c_im=s5_c_im, s5_d=s5_d, s5_w_glu=s5_w_glu, ffn_w_gu=ffn_w_gu, ffn_w_down=ffn_w_down, final_norm_g=final_norm_g, loss_target=loss_target, m_norm_mix_g=m_norm_mix_g, m_norm_ffn_g=m_norm_ffn_g, m_gla_w_in=m_gla_w_in, m_gla_w_a2=m_gla_w_a2, m_gla_b_a=m_gla_b_a, m_gla_norm_g=m_gla_norm_g, m_gla_w_out=m_gla_w_out, m_ssd_w_in=m_ssd_w_in, m_ssd_conv_w=m_ssd_conv_w, m_ssd_conv_b=m_ssd_conv_b, m_ssd_dt_bias=m_ssd_dt_bias, m_ssd_a_log=m_ssd_a_log, m_ssd_d=m_ssd_d, m_ssd_norm_g=m_ssd_norm_g, m_ssd_w_out=m_ssd_w_out, m_s5_log_dt=m_s5_log_dt, m_s5_a_re=m_s5_a_re, m_s5_a_im=m_s5_a_im, m_s5_b_re=m_s5_b_re, m_s5_b_im=m_s5_b_im, m_s5_c_re=m_s5_c_re, m_s5_c_im=m_s5_c_im, m_s5_d=m_s5_d, m_s5_w_glu=m_s5_w_glu, m_ffn_w_gu=m_ffn_w_gu, m_ffn_w_down=m_ffn_w_down, m_final_norm_g=m_final_norm_g, v_norm_mix_g=v_norm_mix_g, v_norm_ffn_g=v_norm_ffn_g, v_gla_w_in=v_gla_w_in, v_gla_w_a2=v_gla_w_a2, v_gla_b_a=v_gla_b_a, v_gla_norm_g=v_gla_norm_g, v_gla_w_out=v_gla_w_out, v_ssd_w_in=v_ssd_w_in, v_ssd_conv_w=v_ssd_conv_w, v_ssd_conv_b=v_ssd_conv_b, v_ssd_dt_bias=v_ssd_dt_bias, v_ssd_a_log=v_ssd_a_log, v_ssd_d=v_ssd_d, v_ssd_norm_g=v_ssd_norm_g, v_ssd_w_out=v_ssd_w_out, v_s5_log_dt=v_s5_log_dt, v_s5_a_re=v_s5_a_re, v_s5_a_im=v_s5_a_im, v_s5_b_re=v_s5_b_re, v_s5_b_im=v_s5_b_im, v_s5_c_re=v_s5_c_re, v_s5_c_im=v_s5_c_im, v_s5_d=v_s5_d, v_s5_w_glu=v_s5_w_glu, v_ffn_w_gu=v_ffn_w_gu, v_ffn_w_down=v_ffn_w_down, v_final_norm_g=v_final_norm_g)
    weights = {n: given[n] for n in TWIN_WEIGHTS}
    shared = {n: given[n] for n in SHARED_INPUTS}
    per_example = {n: given[n] for n in ['x']}
    grad_fn = _jax.value_and_grad(_loss, argnums=(0, 1))

    def one_microbatch(ex, loss_target):
        ex = dict(ex)
        diff = ex.pop(TWIN_DIFF_INPUT)
        return grad_fn(weights, diff, {**shared, **ex}, loss_target)

    if N_MICROBATCH == 1:
        loss, (grad_w, grad_x) = one_microbatch(per_example, given["loss_target"])
    else:
        def body(carry, xs):
            loss_sum, grad_sum = carry
            l_k, (gw_k, gx_k) = one_microbatch(xs[0], xs[1])
            with _jax.named_scope("update"):
                return (loss_sum + l_k, _jax.tree.map(_jnp.add, grad_sum, gw_k)), gx_k

        init = (_jnp.zeros((), _jnp.float32), _jax.tree.map(_jnp.zeros_like, weights))
        (loss, grad_w), grad_x = _jax.lax.scan(body, init, (per_example, given["loss_target"]))
    with _jax.named_scope("update"):
        delta_w, new_m, new_v = {}, {}, {}
        for n in TWIN_WEIGHTS:
            delta_w[n], new_m[n], new_v[n] = _adamw(weights[n], grad_w[n], given["m_" + n], given["v_" + n])
    return (loss, grad_x, *[grad_w[n] for n in TWIN_WEIGHTS], *[delta_w[n] for n in TWIN_WEIGHTS],
            *[new_m[n] for n in TWIN_WEIGHTS], *[new_v[n] for n in TWIN_WEIGHTS])
```

```python
import functools
import math

import jax
import jax.numpy as jnp
import numpy as np
from jax import lax
from jax.experimental import pallas as pl
from jax.experimental.pallas import tpu as pltpu

F32 = jnp.float32
BF16 = jnp.bfloat16
_MXU_DTYPE = jnp.bfloat16

N_DEV = 8
D_MODEL = 1024
DEPTH = 4
CHUNK = 64
EPS = 1e-6
GLA_HEADS, GLA_DK, GLA_DV, GLA_RANK, GLA_TAU = 4, 128, 256, 16, 16.0
GLA_QK = GLA_HEADS * GLA_DK
GLA_VD = GLA_HEADS * GLA_DV
LANES = 128
GLA_IN = 2 * GLA_QK + 2 * GLA_VD + GLA_RANK
GLA_PROJ = 2 * GLA_QK + 2 * GLA_VD + LANES
SSD_DINNER, SSD_HEADDIM, SSD_HEADS, SSD_GROUPS, SSD_HPG, SSD_DSTATE, SSD_CONV = 2048, 64, 32, 8, 4, 128, 4
SSD_GN = SSD_GROUPS * SSD_DSTATE
SSD_GW = SSD_HPG * SSD_HEADDIM
SSD_XBC = SSD_DINNER + 2 * SSD_GN
SSD_IN = SSD_DINNER + SSD_XBC + SSD_HEADS
SSD_PROJ = SSD_DINNER + SSD_XBC + LANES
S5_GROUP, S5_GROUPS, S5_STATE = 16, 64, 64
S5_CHUNK = 64
FFN_HIDDEN = 2816
ADAM_LR, ADAM_B1, ADAM_B2, ADAM_EPS, ADAM_WD, ADAM_STEP = 0.001, 0.9, 0.999, 1e-08, 0.01, 10
VMEM_LIMIT = 48 * 1024 * 1024
FLAT_COLS = 1024
FLAT_ROWS_ALIGN = 64


def _tile(n, cap, unit):
    if n <= cap:
        return n
    best = None
    for t in range(unit, cap + 1, unit):
        if n % t == 0:
            best = t
    assert best is not None, (n, cap, unit)
    return best


def _matmul(a, b, *, ta=False, tb=False, out_dtype=F32, name):
    batched = a.ndim == 3
    if ta:
        k_dim, m_dim = a.shape[-2:]
    else:
        m_dim, k_dim = a.shape[-2:]
    if tb:
        n_dim, kb = b.shape[-2:]
    else:
        kb, n_dim = b.shape[-2:]
    assert kb == k_dim, (a.shape, b.shape, ta, tb)
    tm = _tile(m_dim, 1024, LANES if ta else 8)
    tn = _tile(n_dim, 1408, LANES)
    tk = _tile(k_dim, 1024, LANES)
    nk = k_dim // tk
    ca, cb = (0 if ta else 1), (1 if tb else 0)

    def body(a_ref, b_ref, o_ref, *scratch):
        part = lax.dot_general(a_ref[...].astype(_MXU_DTYPE), b_ref[...].astype(_MXU_DTYPE),
                               (((ca,), (cb,)), ((), ())), preferred_element_type=F32)
        if nk == 1:
            o_ref[...] = part.astype(o_ref.dtype)
            return
        acc_ref, = scratch
        k = pl.program_id(3 if batched else 2)

        @pl.when(k == 0)
        def _():
            acc_ref[...] = part

        @pl.when(k > 0)
        def _():
            acc_ref[...] += part

        @pl.when(k == nk - 1)
        def _():
            o_ref[...] = acc_ref[...].astype(o_ref.dtype)

    lead = (None,) if batched else ()

    def spec(shape, fn):
        if batched:
            return pl.BlockSpec(lead + shape, lambda g, i, j, k: (g,) + fn(i, j, k))
        return pl.BlockSpec(shape, fn)

    a_spec = spec((tk, tm), lambda i, j, k: (k, i)) if ta else spec((tm, tk), lambda i, j, k: (i, k))
    b_spec = spec((tn, tk), lambda i, j, k: (j, k)) if tb else spec((tk, tn), lambda i, j, k: (k, j))
    o_spec = spec((tm, tn), lambda i, j, k: (i, j))
    grid = (m_dim // tm, n_dim // tn, nk)
    sem = ("parallel", "parallel", "arbitrary")
    out_shape = (m_dim, n_dim)
    if batched:
        grid = (a.shape[0],) + grid
        sem = ("parallel",) + sem
        out_shape = (a.shape[0],) + out_shape
    return pl.pallas_call(
        body, name=name, grid=grid, in_specs=[a_spec, b_spec], out_specs=o_spec,
        out_shape=jax.ShapeDtypeStruct(out_shape, out_dtype),
        scratch_shapes=[pltpu.VMEM((tm, tn), F32)] if nk > 1 else [],
        compiler_params=pltpu.CompilerParams(dimension_semantics=sem, vmem_limit_bytes=VMEM_LIMIT),
    )(a, b)


def _dot(a, b, ca=1, cb=0, exact=False):
    if exact:
        return lax.dot_general(a, b, (((ca,), (cb,)), ((), ())), precision=lax.Precision.HIGHEST,
                               preferred_element_type=F32)
    return lax.dot_general(a.astype(_MXU_DTYPE), b.astype(_MXU_DTYPE), (((ca,), (cb,)), ((), ())),
                           preferred_element_type=F32)


def _tri(n):
    return lax.broadcasted_iota(jnp.int32, (n, n), 0) >= lax.broadcasted_iota(jnp.int32, (n, n), 1)


def _log_sigmoid(x):
    return jnp.minimum(x, 0.0) - jnp.log(1.0 + jnp.exp(-jnp.abs(x)))


def _softplus(x):
    return jnp.maximum(x, 0.0) + jnp.log(1.0 + jnp.exp(-jnp.abs(x)))


def _silu(x):
    return x / (1.0 + jnp.exp(-x))


def _full_spec(shape):
    return pl.BlockSpec(shape, lambda c: (0,) * len(shape))


_SEQ_PARAMS = pltpu.CompilerParams(dimension_semantics=("arbitrary",), vmem_limit_bytes=VMEM_LIMIT)


def _gla_chunk(proj, st, w_a2, b_a, norm_g):
    t = proj.shape[0]
    q = proj[:, 0:GLA_QK] * (GLA_DK ** -0.5)
    k = proj[:, GLA_QK:2 * GLA_QK]
    v = proj[:, 2 * GLA_QK:2 * GLA_QK + GLA_VD]
    r = proj[:, 2 * GLA_QK + GLA_VD:2 * GLA_QK + 2 * GLA_VD]
    a_low = proj[:, 2 * GLA_QK + 2 * GLA_VD:]
    log_a = _log_sigmoid(_dot(a_low, w_a2) + b_a) * (1.0 / GLA_TAU)
    past = _tri(t)
    lc = _dot(past.astype(F32), log_a, exact=True)
    lend = lc[t - 1:t, :]
    e_pos = jnp.exp(lc)
    e_neg = jnp.exp(-lc)
    q_fwd, k_fwd, q_bwd, k_bwd = q * e_pos, k * e_neg, q * e_neg, k * e_pos
    kd = k * jnp.exp(lend - lc)
    g = jnp.exp(lend)
    outs, new_st = [], []
    for h in range(GLA_HEADS):
        sk = slice(h * GLA_DK, (h + 1) * GLA_DK)
        sv = slice(h * GLA_DV, (h + 1) * GLA_DV)
        s_past = _dot(q_fwd[:, sk], k_fwd[:, sk], 1, 1)
        s_future = _dot(q_bwd[:, sk], k_bwd[:, sk], 1, 1)
        scores = jnp.where(past, s_past, s_future)
        o = _dot(scores, v[:, sv]) + _dot(q_fwd[:, sk], st[h], 1, 1)
        new_st.append(st[h] * g[:, sk] + _dot(v[:, sv], kd[:, sk], 0, 0))
        o = o * lax.rsqrt(jnp.mean(o * o, axis=-1, keepdims=True) + EPS) * norm_g[:, sv]
        outs.append(o)
    return jnp.concatenate(outs, axis=1) * _silu(r), tuple(new_st)


_GLA_STATE = (GLA_HEADS, GLA_DV, GLA_DK)


def _gla_core_fwd(proj, w_a2, b_a, norm_g):
    seq = proj.shape[0]
    nc = seq // CHUNK

    def body(proj_ref, wa_ref, ba_ref, ng_ref, o_ref, sprev_ref, st_ref):
        @pl.when(pl.program_id(0) == 0)
        def _():
            st_ref[...] = jnp.zeros_like(st_ref)

        st = tuple(st_ref[h] for h in range(GLA_HEADS))
        for h in range(GLA_HEADS):
            sprev_ref[0, h] = st[h]
        out, new_st = _gla_chunk(proj_ref[...], st, wa_ref[...], ba_ref[...], ng_ref[...])
        o_ref[...] = out
        for h in range(GLA_HEADS):
            st_ref[h] = new_st[h]

    return pl.pallas_call(
        body, name="gla_core_fwd", grid=(nc,),
        in_specs=[pl.BlockSpec((CHUNK, GLA_PROJ), lambda c: (c, 0)), _full_spec(w_a2.shape), _full_spec(b_a.shape),
                  _full_spec(norm_g.shape)],
        out_specs=[pl.BlockSpec((CHUNK, GLA_VD), lambda c: (c, 0)), pl.BlockSpec((1,) + _GLA_STATE, lambda c: (c, 0, 0, 0))],
        out_shape=[jax.ShapeDtypeStruct((seq, GLA_VD), F32), jax.ShapeDtypeStruct((nc,) + _GLA_STATE, F32)],
        scratch_shapes=[pltpu.VMEM(_GLA_STATE, F32)],
        compiler_params=_SEQ_PARAMS,
    )(proj, w_a2, b_a, norm_g)


def _gla_core_bwd(proj, sprev, d_out, w_a2, b_a, norm_g):
    seq = proj.shape[0]
    nc = seq // CHUNK

    def body(proj_ref, sprev_ref, do_ref, wa_ref, ba_ref, ng_ref, dproj_ref, dwa_ref, dba_ref, dng_ref, dst_ref):
        @pl.when(pl.program_id(0) == 0)
        def _():
            dst_ref[...] = jnp.zeros_like(dst_ref)
            dwa_ref[...] = jnp.zeros_like(dwa_ref)
            dba_ref[...] = jnp.zeros_like(dba_ref)
            dng_ref[...] = jnp.zeros_like(dng_ref)

        st = tuple(sprev_ref[0, h] for h in range(GLA_HEADS))
        _, vjp = jax.vjp(_gla_chunk, proj_ref[...], st, wa_ref[...], ba_ref[...], ng_ref[...])
        d_next = tuple(dst_ref[h] for h in range(GLA_HEADS))
        d_proj, d_st, d_wa, d_ba, d_ng = vjp((do_ref[...], d_next))
        dproj_ref[...] = d_proj
        for h in range(GLA_HEADS):
            dst_ref[h] = d_st[h]
        dwa_ref[...] += d_wa
        dba_ref[...] += d_ba
        dng_ref[...] += d_ng

    rev = lambda c: (nc - 1 - c, 0)
    return pl.pallas_call(
        body, name="gla_core_bwd", grid=(nc,),
        in_specs=[pl.BlockSpec((CHUNK, GLA_PROJ), rev), pl.BlockSpec((1,) + _GLA_STATE, lambda c: (nc - 1 - c, 0, 0, 0)),
                  pl.BlockSpec((CHUNK, GLA_VD), rev), _full_spec(w_a2.shape), _full_spec(b_a.shape), _full_spec(norm_g.shape)],
        out_specs=[pl.BlockSpec((CHUNK, GLA_PROJ), rev), _full_spec(w_a2.shape), _full_spec(b_a.shape), _full_spec(norm_g.shape)],
        out_shape=[jax.ShapeDtypeStruct((seq, GLA_PROJ), F32), jax.ShapeDtypeStruct(w_a2.shape, F32),
                   jax.ShapeDtypeStruct(b_a.shape, F32), jax.ShapeDtypeStruct(norm_g.shape, F32)],
        scratch_shapes=[pltpu.VMEM(_GLA_STATE, F32)],
        compiler_params=_SEQ_PARAMS,
    )(proj, sprev, d_out, w_a2, b_a, norm_g)


def _ssd_chunk(z, xbc, dt_raw, hs, dt_bias, a_log, d_skip, norm_g):
    t = z.shape[0]
    xs = xbc[:, :SSD_DINNER]
    bm = xbc[:, SSD_DINNER:SSD_DINNER + SSD_GN]
    cm = xbc[:, SSD_DINNER + SSD_GN:]
    dt = _softplus(dt_raw + dt_bias)
    da = dt * (-jnp.exp(a_log))
    tri = _tri(t).astype(F32)
    eye = (lax.broadcasted_iota(jnp.int32, (t, t), 0) == lax.broadcasted_iota(jnp.int32, (t, t), 1)).astype(F32)
    cum = _dot(tri, da, exact=True)
    cum_t = _dot(da, tri, 0, 1, exact=True)
    dt_t = _dot(dt, eye, 0, 0, exact=True)
    cum_end = cum[t - 1:t, :]
    w_state = dt * jnp.exp(cum_end - cum)
    e_cum = jnp.exp(cum)
    g_end = jnp.exp(cum_end)
    ys, new_hs = [], []
    for g in range(SSD_GROUPS):
        bm_g = bm[:, g * SSD_DSTATE:(g + 1) * SSD_DSTATE]
        cm_g = cm[:, g * SSD_DSTATE:(g + 1) * SSD_DSTATE]
        cb = _dot(cm_g, bm_g, 1, 1)
        y_heads, h_heads = [], []
        for j in range(SSD_HPG):
            h = g * SSD_HPG + j
            xs_h = xs[:, h * SSD_HEADDIM:(h + 1) * SSD_HEADDIM]
            decay = jnp.exp(-jnp.abs(cum[:, h:h + 1] - cum_t[h:h + 1, :]))
            y = _dot(cb * decay * dt_t[h:h + 1, :], xs_h)
            hp = hs[g][j * SSD_HEADDIM:(j + 1) * SSD_HEADDIM, :]
            y = y + _dot(cm_g, hp, 1, 1) * e_cum[:, h:h + 1]
            y = y + d_skip[:, h:h + 1] * xs_h
            h_heads.append(g_end[:, h:h + 1] * hp + _dot(xs_h * w_state[:, h:h + 1], bm_g, 0, 0))
            y_heads.append(y)
        cols = slice(g * SSD_GW, (g + 1) * SSD_GW)
        yg = jnp.concatenate(y_heads, axis=1) * _silu(z[:, cols])
        ys.append(yg * lax.rsqrt(jnp.mean(yg * yg, axis=-1, keepdims=True) + EPS) * norm_g[:, cols])
        new_hs.append(jnp.concatenate(h_heads, axis=0))
    return jnp.concatenate(ys, axis=1), tuple(new_hs)


_SSD_STATE = (SSD_GROUPS, SSD_GW, SSD_DSTATE)
_SSD_DT_BLOCK = (SSD_DINNER + SSD_XBC) // LANES


def _ssd_core_fwd(proj, xbc, dt_bias, a_log, d_skip, norm_g):
    seq = proj.shape[0]
    nc = seq // CHUNK

    def body(z_ref, xbc_ref, dt_ref, db_ref, al_ref, ds_ref, ng_ref, o_ref, hprev_ref, hs_ref):
        @pl.when(pl.program_id(0) == 0)
        def _():
            hs_ref[...] = jnp.zeros_like(hs_ref)

        hs = tuple(hs_ref[g] for g in range(SSD_GROUPS))
        for g in range(SSD_GROUPS):
            hprev_ref[0, g] = hs[g]
        out, new_hs = _ssd_chunk(z_ref[...], xbc_ref[...], dt_ref[...], hs, db_ref[...], al_ref[...], ds_ref[...], ng_ref[...])
        o_ref[...] = out
        for g in range(SSD_GROUPS):
            hs_ref[g] = new_hs[g]

    return pl.pallas_call(
        body, name="ssd_core_fwd", grid=(nc,),
        in_specs=[pl.BlockSpec((CHUNK, SSD_DINNER), lambda c: (c, 0)), pl.BlockSpec((CHUNK, SSD_XBC), lambda c: (c, 0)),
                  pl.BlockSpec((CHUNK, LANES), lambda c: (c, _SSD_DT_BLOCK)),
                  _full_spec(dt_bias.shape), _full_spec(a_log.shape), _full_spec(d_skip.shape), _full_spec(norm_g.shape)],
        out_specs=[pl.BlockSpec((CHUNK, SSD_DINNER), lambda c: (c, 0)), pl.BlockSpec((1,) + _SSD_STATE, lambda c: (c, 0, 0, 0))],
        out_shape=[jax.ShapeDtypeStruct((seq, SSD_DINNER), F32), jax.ShapeDtypeStruct((nc,) + _SSD_STATE, F32)],
        scratch_shapes=[pltpu.VMEM(_SSD_STATE, F32)],
        compiler_params=_SEQ_PARAMS,
    )(proj, xbc, proj, dt_bias, a_log, d_skip, norm_g)


def _ssd_core_bwd(proj, xbc, hprev, d_out, dt_bias, a_log, d_skip, norm_g):
    seq = proj.shape[0]
    nc = seq // CHUNK

    def body(z_ref, xbc_ref, dt_ref, hprev_ref, do_ref, db_ref, al_ref, ds_ref, ng_ref,
             dz_ref, dxbc_ref, ddt_ref, ddb_ref, dal_ref, dds_ref, dng_ref, dhs_ref):
        @pl.when(pl.program_id(0) == 0)
        def _():
            dhs_ref[...] = jnp.zeros_like(dhs_ref)
            ddb_ref[...] = jnp.zeros_like(ddb_ref)
            dal_ref[...] = jnp.zeros_like(dal_ref)
            dds_ref[...] = jnp.zeros_like(dds_ref)
            dng_ref[...] = jnp.zeros_like(dng_ref)

        hs = tuple(hprev_ref[0, g] for g in range(SSD_GROUPS))
        _, vjp = jax.vjp(_ssd_chunk, z_ref[...], xbc_ref[...], dt_ref[...], hs, db_ref[...], al_ref[...], ds_ref[...], ng_ref[...])
        d_next = tuple(dhs_ref[g] for g in range(SSD_GROUPS))
        d_z, d_xbc, d_dt, d_hs, d_db, d_al, d_ds, d_ng = vjp((do_ref[...], d_next))
        dz_ref[...] = d_z
        dxbc_ref[...] = d_xbc
        ddt_ref[...] = d_dt
        for g in range(SSD_GROUPS):
            dhs_ref[g] = d_hs[g]
        ddb_ref[...] += d_db
        dal_ref[...] += d_al
        dds_ref[...] += d_ds
        dng_ref[...] += d_ng

    rev = lambda c: (nc - 1 - c, 0)
    vec = [_full_spec(dt_bias.shape), _full_spec(a_log.shape), _full_spec(d_skip.shape), _full_spec(norm_g.shape)]
    return pl.pallas_call(
        body, name="ssd_core_bwd", grid=(nc,),
        in_specs=[pl.BlockSpec((CHUNK, SSD_DINNER), rev), pl.BlockSpec((CHUNK, SSD_XBC), rev),
                  pl.BlockSpec((CHUNK, LANES), lambda c: (nc - 1 - c, _SSD_DT_BLOCK)),
                  pl.BlockSpec((1,) + _SSD_STATE, lambda c: (nc - 1 - c, 0, 0, 0)),
                  pl.BlockSpec((CHUNK, SSD_DINNER), rev)] + vec,
        out_specs=[pl.BlockSpec((CHUNK, SSD_DINNER), rev), pl.BlockSpec((CHUNK, SSD_XBC), rev),
                   pl.BlockSpec((CHUNK, LANES), rev)] + vec,
        out_shape=[jax.ShapeDtypeStruct((seq, SSD_DINNER), F32), jax.ShapeDtypeStruct((seq, SSD_XBC), F32),
                   jax.ShapeDtypeStruct((seq, LANES), F32),
                   jax.ShapeDtypeStruct(dt_bias.shape, F32), jax.ShapeDtypeStruct(a_log.shape, F32),
                   jax.ShapeDtypeStruct(d_skip.shape, F32), jax.ShapeDtypeStruct(norm_g.shape, F32)],
        scratch_shapes=[pltpu.VMEM(_SSD_STATE, F32)],
        compiler_params=_SEQ_PARAMS,
    )(proj, xbc, proj, hprev, d_out, dt_bias, a_log, d_skip, norm_g)


def _s5_boundary_scan(z_re, z_im, lam_re, lam_im):
    n_chunks = z_re.shape[0]

    def body(zr_ref, zi_ref, lr_ref, li_ref, xr_ref, xi_ref):
        lr, li = lr_ref[...], li_ref[...]

        def step(n, carry):
            xr, xi = carry
            xr_ref[n] = xr
            xi_ref[n] = xi
            return lr * xr - li * xi + zr_ref[n], lr * xi + li * xr + zi_ref[n]

        zero = jnp.zeros(lr.shape, F32)
        lax.fori_loop(0, n_chunks, step, (zero, zero))

    shape = jax.ShapeDtypeStruct(z_re.shape, F32)
    return pl.pallas_call(body, name="s5_boundary_scan", out_shape=[shape, shape],
                          compiler_params=pltpu.CompilerParams(vmem_limit_bytes=VMEM_LIMIT))(z_re, z_im, lam_re, lam_im)


_FLIPS = [(kx, ky, kc) for kx in (0, 1) for ky in (0, 1) for kc in (0, 1)][1:]


def _mesh_position():
    return lax.axis_index("x"), lax.axis_index("y"), lax.axis_index("c")


def _peer(pos, flip):
    return tuple((1 - p) if f else p for p, f in zip(pos, flip))


def _index(pos):
    return 4 * pos[0] + 2 * pos[1] + pos[2]


_ANY = pl.BlockSpec(memory_space=pl.ANY)


def _all_gather(x, name, axis=0):
    lead = (slice(None),) * axis

    def body(x_ref, out_ref, send_sems, recv_sems, local_sem):
        me = _mesh_position()
        slot = lambda pos: out_ref.at[lead + (_index(pos),)]
        mine = pltpu.make_async_copy(x_ref, slot(me), local_sem)
        mine.start()
        copies = []
        for k, flip in enumerate(_FLIPS):
            peer = _peer(me, flip)
            cp = pltpu.make_async_remote_copy(src_ref=x_ref, dst_ref=slot(me), send_sem=send_sems.at[k],
                                              recv_sem=recv_sems.at[k], device_id=peer, device_id_type=pl.DeviceIdType.MESH)
            cp.start()
            copies.append(cp)
        for k, flip in enumerate(_FLIPS):
            peer = _peer(me, flip)
            pltpu.make_async_remote_copy(src_ref=x_ref, dst_ref=slot(peer), send_sem=send_sems.at[k],
                                         recv_sem=recv_sems.at[k], device_id=peer,
                                         device_id_type=pl.DeviceIdType.MESH).wait_recv()
        for cp in copies:
            cp.wait_send()
        mine.wait()

    return pl.pallas_call(
        body, name=name, in_specs=[_ANY], out_specs=_ANY,
        out_shape=jax.ShapeDtypeStruct(x.shape[:axis] + (N_DEV,) + x.shape[axis:], x.dtype),
        scratch_shapes=[pltpu.SemaphoreType.DMA((N_DEV - 1,)), pltpu.SemaphoreType.DMA((N_DEV - 1,)), pltpu.SemaphoreType.DMA],
        compiler_params=pltpu.CompilerParams(has_side_effects=True),
    )(x)


def _exchange(x, name):
    def body(x_ref, out_ref, send_sems, recv_sems, local_sem):
        me = _mesh_position()
        mine = pltpu.make_async_copy(x_ref.at[_index(me)], out_ref.at[_index(me)], local_sem)
        mine.start()
        copies = []
        for k, flip in enumerate(_FLIPS):
            peer = _peer(me, flip)
            cp = pltpu.make_async_remote_copy(src_ref=x_ref.at[_index(peer)], dst_ref=out_ref.at[_index(me)],
                                              send_sem=send_sems.at[k], recv_sem=recv_sems.at[k], device_id=peer,
                                              device_id_type=pl.DeviceIdType.MESH)
            cp.start()
            copies.append(cp)
        for k, flip in enumerate(_FLIPS):
            peer = _peer(me, flip)
            pltpu.make_async_remote_copy(src_ref=x_ref.at[_index(peer)], dst_ref=out_ref.at[_index(peer)],
                                         send_sem=send_sems.at[k], recv_sem=recv_sems.at[k], device_id=peer,
                                         device_id_type=pl.DeviceIdType.MESH).wait_recv()
        for cp in copies:
            cp.wait_send()
        mine.wait()

    return pl.pallas_call(
        body, name=name, in_specs=[_ANY], out_specs=_ANY,
        out_shape=jax.ShapeDtypeStruct(x.shape, x.dtype),
        scratch_shapes=[pltpu.SemaphoreType.DMA((N_DEV - 1,)), pltpu.SemaphoreType.DMA((N_DEV - 1,)), pltpu.SemaphoreType.DMA],
        compiler_params=pltpu.CompilerParams(has_side_effects=True),
    )(x)


def _adamw(parts, w, m, v, name):
    n_parts = parts.shape[0]
    layers, rows, cols = w.shape
    tr = _tile(rows, 256, 8)

    def body(p_ref, w_ref, m_ref, v_ref, g_ref, d_ref, mo_ref, vo_ref):
        g = p_ref[0].astype(F32)
        for s in range(1, n_parts):
            g = g + p_ref[s].astype(F32)
        m_new = ADAM_B1 * m_ref[...] + (1.0 - ADAM_B1) * g
        v_new = ADAM_B2 * v_ref[...] + (1.0 - ADAM_B2) * (g * g)
        m_hat = m_new / (1.0 - ADAM_B1 ** ADAM_STEP)
        v_hat = v_new / (1.0 - ADAM_B2 ** ADAM_STEP)
        g_ref[...] = g
        d_ref[...] = -ADAM_LR * (m_hat / (jnp.sqrt(v_hat) + ADAM_EPS) + ADAM_WD * w_ref[...])
        mo_ref[...] = m_new
        vo_ref[...] = v_new

    blk = pl.BlockSpec((None, tr, cols), lambda l, i: (l, i, 0))
    shape = jax.ShapeDtypeStruct(w.shape, F32)
    return pl.pallas_call(
        body, name=name, grid=(layers, rows // tr),
        in_specs=[pl.BlockSpec((n_parts, None, tr, cols), lambda l, i: (0, l, i, 0)), blk, blk, blk],
        out_specs=[blk, blk, blk, blk], out_shape=[shape, shape, shape, shape],
        compiler_params=pltpu.CompilerParams(dimension_semantics=("parallel", "parallel"), vmem_limit_bytes=VMEM_LIMIT),
    )(parts, w, m, v)


def _sum_parts(parts, name):
    _, rows, cols = parts.shape
    tr = _tile(rows, 256, 8)

    def body(p_ref, o_ref):
        total = p_ref[0]
        for s in range(1, N_DEV):
            total = total + p_ref[s]
        o_ref[...] = total

    return pl.pallas_call(
        body, name=name, grid=(rows // tr,),
        in_specs=[pl.BlockSpec((N_DEV, tr, cols), lambda i: (0, i, 0))], out_specs=pl.BlockSpec((tr, cols), lambda i: (i, 0)),
        out_shape=jax.ShapeDtypeStruct((rows, cols), parts.dtype),
        compiler_params=pltpu.CompilerParams(dimension_semantics=("parallel",), vmem_limit_bytes=VMEM_LIMIT),
    )(parts)


def _join_cols(blocks, n_out, name):
    _, layers, rows, n = blocks.shape
    tr = _tile(rows, 256, 16)

    def body(x_ref, o_ref):
        for d in range(N_DEV):
            o_ref[:, d * n:(d + 1) * n] = x_ref[d]
        if n_out > N_DEV * n:
            o_ref[:, N_DEV * n:] = jnp.zeros((tr, n_out - N_DEV * n), o_ref.dtype)

    return pl.pallas_call(
        body, name=name, grid=(layers, rows // tr),
        in_specs=[pl.BlockSpec((N_DEV, None, tr, n), lambda l, i: (0, l, i, 0))],
        out_specs=pl.BlockSpec((None, tr, n_out), lambda l, i: (l, i, 0)),
        out_shape=jax.ShapeDtypeStruct((layers, rows, n_out), blocks.dtype),
        compiler_params=pltpu.CompilerParams(dimension_semantics=("parallel", "parallel"), vmem_limit_bytes=VMEM_LIMIT),
    )(blocks)


def _split_cols(full, n, name):
    rows = full.shape[0]
    tr = _tile(rows, 256, 16)

    def body(x_ref, o_ref):
        for d in range(N_DEV):
            o_ref[d] = x_ref[:, d * n:(d + 1) * n]

    return pl.pallas_call(
        body, name=name, grid=(rows // tr,),
        in_specs=[pl.BlockSpec((tr, full.shape[1]), lambda i: (i, 0))],
        out_specs=pl.BlockSpec((N_DEV, tr, n), lambda i: (0, i, 0)),
        out_shape=jax.ShapeDtypeStruct((N_DEV, rows, n), full.dtype),
        compiler_params=pltpu.CompilerParams(dimension_semantics=("parallel",), vmem_limit_bytes=VMEM_LIMIT),
    )(full)


def _pack(arrays):
    flat = jnp.concatenate([a.reshape(-1) for a in arrays])
    unit = FLAT_COLS * FLAT_ROWS_ALIGN
    padded = -(-flat.shape[0] // unit) * unit
    return jnp.pad(flat, (0, padded - flat.shape[0])).reshape(-1, FLAT_COLS)


def _unpack(flat, shapes, lead=()):
    flat = flat.reshape(lead + (-1,))
    out, off = [], 0
    for shape in shapes:
        n = math.prod(shape)
        out.append(flat[..., off:off + n].reshape(lead + tuple(shape)))
        off += n
    return out


def _join(blocks, axis):
    moved = jnp.moveaxis(blocks, 0, axis)
    shape = list(moved.shape)
    shape[axis:axis + 2] = [shape[axis] * shape[axis + 1]]
    return moved.reshape(shape)


def _own_shard(full, axis, position):
    n = full.shape[axis] // N_DEV
    return lax.dynamic_slice_in_dim(full, position * n, n, axis)


def _rmsnorm(x, g):
    return x * lax.rsqrt(jnp.mean(x * x, axis=-1, keepdims=True) + EPS) * g


def _swiglu(gu):
    return jax.nn.silu(gu[:, :FFN_HIDDEN]) * gu[:, FFN_HIDDEN:]


def _ssd_conv(xbc, conv_w, conv_b):
    seq = xbc.shape[0]
    padded = jnp.pad(xbc, ((SSD_CONV - 1, 0), (0, 0)))
    out = conv_b
    for k in range(SSD_CONV):
        out = out + padded[k:k + seq] * conv_w[k]
    return jax.nn.silu(out)


def _loss_head(h, g, target):
    err = _rmsnorm(h, g) - target
    return 0.5 * jnp.sum(jnp.mean(err * err, axis=-1))


def _s5_operators(log_dt, a_re, a_im, b_re, b_im, c_re, c_im):
    t = S5_CHUNK
    hi = lax.Precision.HIGHEST
    step = jnp.exp(log_dt)[:, None]
    mag = jnp.exp(step * a_re)
    abar_re = mag * jnp.cos(step * a_im)
    abar_im = mag * jnp.sin(step * a_im)
    den = a_re * a_re + a_im * a_im
    f_re = ((abar_re - 1.0) * a_re + abar_im * a_im) / den
    f_im = (abar_im * a_re - (abar_re - 1.0) * a_im) / den
    bb_re = f_re[..., None] * b_re - f_im[..., None] * b_im
    bb_im = f_re[..., None] * b_im + f_im[..., None] * b_re
    j = jnp.arange(t + 1, dtype=F32)[:, None, None]
    pmag = jnp.exp(j * (step * a_re))
    pw_re = pmag * jnp.cos(j * (step * a_im))
    pw_im = pmag * jnp.sin(j * (step * a_im))
    cl_re = c_re[None] * pw_re[:t, :, None, :] - c_im[None] * pw_im[:t, :, None, :]
    cl_im = c_re[None] * pw_im[:t, :, None, :] + c_im[None] * pw_re[:t, :, None, :]
    kern = (jnp.einsum('jgcp,gpk->jgck', cl_re, bb_re, precision=hi)
            - jnp.einsum('jgcp,gpk->jgck', cl_im, bb_im, precision=hi))
    rp_re, rp_im = pw_re[:t][::-1], pw_im[:t][::-1]
    wz_re = rp_re[:, :, :, None] * bb_re[None] - rp_im[:, :, :, None] * bb_im[None]
    wz_im = rp_re[:, :, :, None] * bb_im[None] + rp_im[:, :, :, None] * bb_re[None]
    w_z = jnp.concatenate([wz_re, wz_im], axis=2).transpose(1, 0, 3, 2).reshape(S5_GROUPS, t * S5_GROUP, 2 * S5_STATE)
    cy_re = c_re[None] * pw_re[1:, :, None, :] - c_im[None] * pw_im[1:, :, None, :]
    cy_im = c_re[None] * pw_im[1:, :, None, :] + c_im[None] * pw_re[1:, :, None, :]
    w_y = jnp.concatenate([cy_re, -cy_im], axis=3).transpose(1, 3, 0, 2).reshape(S5_GROUPS, 2 * S5_STATE, t * S5_GROUP)
    return kern, w_z, w_y, pw_re[t], pw_im[t]


def _s5_lag_selector():
    t = S5_CHUNK
    lag = jnp.arange(t)[:, None] - jnp.arange(t)[None, :]
    return (lag[:, :, None] == jnp.arange(t)[None, None, :]).astype(F32).reshape(t * t, t)


def _s5_toeplitz(kern, tag):
    t = S5_CHUNK
    sel = _s5_lag_selector()
    flat = _matmul(sel, kern.reshape(t, -1), out_dtype=BF16, name=tag + "_toeplitz")
    toep = flat.reshape(t, t, S5_GROUPS, S5_GROUP, S5_GROUP).transpose(2, 1, 4, 0, 3)
    toep = toep.reshape(S5_GROUPS, t * S5_GROUP, t * S5_GROUP)

    def backward(d_toep):
        d_flat = d_toep.reshape(S5_GROUPS, t, S5_GROUP, t, S5_GROUP).transpose(3, 1, 0, 4, 2).reshape(t * t, -1)
        return _matmul(sel, d_flat, ta=True, name=tag + "_toeplitz_dw").reshape(kern.shape)

    return toep, backward


def _s5_gate(y, u, d_skip):
    return jax.nn.gelu(y + d_skip * u)


def _glu(vg):
    return vg[:, :D_MODEL] * jax.nn.sigmoid(vg[:, D_MODEL:])


def _linear(x, w, tag):
    y = _matmul(x, w, name=tag + "_fwd")

    def backward(dy):
        return (_matmul(dy, w, tb=True, name=tag + "_dx"),
                _matmul(x, dy, ta=True, out_dtype=BF16, name=tag + "_dw"))

    return y, backward


def _gla_mixer(hn, p, tag):
    w_in = p["w_in"]
    w_a2 = jnp.pad(p["w_a2"], ((0, LANES - GLA_RANK), (0, 0)))
    b_a, norm_g = p["b_a"][None], p["norm_g"][None]
    proj, lin_in = _linear(hn, w_in, tag + "_in")
    o, sprev = _gla_core_fwd(proj, w_a2, b_a, norm_g)
    y, lin_out = _linear(o, p["w_out"], tag + "_out")

    def backward(dy):
        d_o, d_wout = lin_out(dy)
        d_proj, d_wa, d_ba, d_ng = _gla_core_bwd(proj, sprev, d_o, w_a2, b_a, norm_g)
        d_hn, d_win = lin_in(d_proj)
        return d_hn, dict(w_in=d_win, w_a2=d_wa[:GLA_RANK], b_a=d_ba[0], norm_g=d_ng[0], w_out=d_wout)

    return y, backward


def _ssd_mixer(hn, p, tag):
    pad = lambda a: jnp.pad(a[None], ((0, 0), (0, LANES - SSD_HEADS)))
    w_in = p["w_in"]
    dt_bias, a_log, d_skip, norm_g = pad(p["dt_bias"]), pad(p["a_log"]), pad(p["d"]), p["norm_g"][None]
    proj, lin_in = _linear(hn, w_in, tag + "_in")
    xbc, conv_vjp = jax.vjp(_ssd_conv, proj[:, SSD_DINNER:SSD_DINNER + SSD_XBC], p["conv_w"], p["conv_b"])
    o, hprev = _ssd_core_fwd(proj, xbc, dt_bias, a_log, d_skip, norm_g)
    y, lin_out = _linear(o, p["w_out"], tag + "_out")

    def backward(dy):
        d_o, d_wout = lin_out(dy)
        d_z, d_xbc, d_dt, d_db, d_al, d_ds, d_ng = _ssd_core_bwd(proj, xbc, hprev, d_o, dt_bias, a_log, d_skip, norm_g)
        d_pre, d_cw, d_cb = conv_vjp(d_xbc)
        d_hn, d_win = lin_in(jnp.concatenate([d_z, d_pre, d_dt], axis=1))
        return d_hn, dict(w_in=d_win, conv_w=d_cw, conv_b=d_cb, dt_bias=d_db[0, :SSD_HEADS],
                          a_log=d_al[0, :SSD_HEADS], d=d_ds[0, :SSD_HEADS], norm_g=d_ng[0], w_out=d_wout)

    return y, backward


def _s5_mixer(hn, p, tag):
    seq = hn.shape[0]
    t, n_chunks = S5_CHUNK, hn.shape[0] // S5_CHUNK
    names = ("log_dt", "a_re", "a_im", "b_re", "b_im", "c_re", "c_im")
    (kern, w_z, w_y, lam_re, lam_im), ops_vjp = jax.vjp(_s5_operators, *[p[k] for k in names])
    toep, toep_bwd = _s5_toeplitz(kern, tag)
    to_groups = lambda a: a.reshape(n_chunks, t, S5_GROUPS, S5_GROUP).transpose(2, 0, 1, 3).reshape(S5_GROUPS, n_chunks, t * S5_GROUP)
    from_groups = lambda a: a.reshape(S5_GROUPS, n_chunks, t, S5_GROUP).transpose(1, 2, 0, 3).reshape(seq, D_MODEL)
    ug = to_groups(hn)
    z = _matmul(ug, w_z, name=tag + "_z")
    z_re, z_im = z[..., :S5_STATE].transpose(1, 0, 2), z[..., S5_STATE:].transpose(1, 0, 2)
    x_re, x_im = _s5_boundary_scan(z_re, z_im, lam_re, lam_im)
    xprev = jnp.concatenate([x_re, x_im], axis=2).transpose(1, 0, 2)
    yg = _matmul(ug, toep, name=tag + "_intra") + _matmul(xprev, w_y, name=tag + "_inter")
    act, gate_vjp = jax.vjp(_s5_gate, from_groups(yg), hn, p["d"])
    vg, lin_glu = _linear(act, p["w_glu"], tag + "_glu")
    out, glu_vjp = jax.vjp(_glu, vg)

    def backward(dy):
        d_vg, = glu_vjp(dy)
        d_act, d_wglu = lin_glu(d_vg)
        d_y, d_hn, d_d = gate_vjp(d_act)
        d_yg = to_groups(d_y)
        d_ug = _matmul(d_yg, toep, tb=True, name=tag + "_intra_dx")
        d_toep = _matmul(ug, d_yg, ta=True, out_dtype=BF16, name=tag + "_intra_dw")
        d_xprev = _matmul(d_yg, w_y, tb=True, name=tag + "_inter_dx").transpose(1, 0, 2)
        d_wy = _matmul(xprev, d_yg, ta=True, name=tag + "_inter_dw")
        dz_re, dz_im = _s5_boundary_scan(d_xprev[::-1, :, :S5_STATE], d_xprev[::-1, :, S5_STATE:], lam_re, -lam_im)
        dz_re, dz_im = dz_re[::-1], dz_im[::-1]
        d_lam_re = jnp.sum(x_re * dz_re + x_im * dz_im, axis=0)
        d_lam_im = jnp.sum(x_re * dz_im - x_im * dz_re, axis=0)
        d_z = jnp.concatenate([dz_re, dz_im], axis=2).transpose(1, 0, 2)
        d_ug = d_ug + _matmul(d_z, w_z, tb=True, name=tag + "_z_dx")
        d_wz = _matmul(ug, d_z, ta=True, name=tag + "_z_dw")
        grads = dict(zip(names, ops_vjp((toep_bwd(d_toep), d_wz, d_wy, d_lam_re, d_lam_im))))
        grads.update(d=d_d, w_glu=d_wglu)
        return d_hn + from_groups(d_ug), grads

    return out, backward


_BIG = [("gla_w_in", 2), ("gla_w_out", 1), ("ssd_w_in", 2), ("ssd_w_out", 1), ("s5_w_glu", 2), ("ffn_w_gu", 2),
        ("ffn_w_down", 1)]
_PADDED_COLS = {"gla_w_in": GLA_PROJ, "ssd_w_in": SSD_PROJ}
_SMALL = [("gla_w_a2", 2), ("gla_b_a", 1), ("gla_norm_g", 1), ("ssd_conv_w", 2), ("s5_d", 1)]
_REPLICATED = ["norm_mix_g", "norm_ffn_g", "ssd_conv_b", "ssd_dt_bias", "ssd_a_log", "ssd_d", "ssd_norm_g", "s5_log_dt",
               "s5_a_re", "s5_a_im", "s5_b_re", "s5_b_im", "s5_c_re", "s5_c_im", "final_norm_g"]
_WEIGHTS = ['norm_mix_g', 'norm_ffn_g', 'gla_w_in', 'gla_w_a2', 'gla_b_a', 'gla_norm_g', 'gla_w_out', 'ssd_w_in',
            'ssd_conv_w', 'ssd_conv_b', 'ssd_dt_bias', 'ssd_a_log', 'ssd_d', 'ssd_norm_g', 'ssd_w_out', 's5_log_dt',
            's5_a_re', 's5_a_im', 's5_b_re', 's5_b_im', 's5_c_re', 's5_c_im', 's5_d', 's5_w_glu', 'ffn_w_gu', 'ffn_w_down',
            'final_norm_g']


def _gather_matrix(shard, name, axis):
    shard = shard.astype(BF16)
    if axis == 1:
        blocks = _all_gather(shard, "gather_" + name, axis=1)
        return blocks.reshape(shard.shape[0], N_DEV * shard.shape[1], shard.shape[2])
    blocks = _all_gather(shard, "gather_" + name)
    return _join_cols(blocks, _PADDED_COLS.get(name, N_DEV * shard.shape[2]), "join_" + name)


def _gather_small(local):
    shapes = [local[n].shape for n, _ in _SMALL]
    blocks = _all_gather(_pack([local[n] for n, _ in _SMALL]), "gather_vectors")
    parts = _unpack(blocks, shapes, lead=(N_DEV,))
    return {n: _join(part, axis) for (n, axis), part in zip(_SMALL, parts)}


def _forward_backward(x, target, w):
    gla = lambda j: dict(w_in=w["gla_w_in"][j], w_a2=w["gla_w_a2"][j], b_a=w["gla_b_a"][j], norm_g=w["gla_norm_g"][j],
                         w_out=w["gla_w_out"][j])
    ssd = lambda j: dict(w_in=w["ssd_w_in"][j], conv_w=w["ssd_conv_w"][j], conv_b=w["ssd_conv_b"][j],
                         dt_bias=w["ssd_dt_bias"][j], a_log=w["ssd_a_log"][j], d=w["ssd_d"][j], norm_g=w["ssd_norm_g"][j],
                         w_out=w["ssd_w_out"][j])
    s5 = lambda j: dict(log_dt=w["s5_log_dt"][j], a_re=w["s5_a_re"][j], a_im=w["s5_a_im"][j], b_re=w["s5_b_re"][j],
                        b_im=w["s5_b_im"][j], c_re=w["s5_c_re"][j], c_im=w["s5_c_im"][j], d=w["s5_d"][j], w_glu=w["s5_w_glu"][j])
    mixers = [("gla", _gla_mixer, gla), ("ssd", _ssd_mixer, ssd), ("s5", _s5_mixer, s5)]
    h = x
    tape = []
    for i in range(DEPTH):
        kind, mixer, params = mixers[i % 3]
        j = i // 3
        hn, norm1_vjp = jax.vjp(_rmsnorm, h, w["norm_mix_g"][i])
        y, mixer_bwd = mixer(hn, params(j), f"l{i}_{kind}")
        h_mid = h + y
        hn2, norm2_vjp = jax.vjp(_rmsnorm, h_mid, w["norm_ffn_g"][i])
        gu, gu_bwd = _linear(hn2, w["ffn_w_gu"][i], f"l{i}_ffn_gu")
        act, act_vjp = jax.vjp(_swiglu, gu)
        f, down_bwd = _linear(act, w["ffn_w_down"][i], f"l{i}_ffn_down")
        h = h_mid + f
        tape.append((kind, j, norm1_vjp, mixer_bwd, norm2_vjp, gu_bwd, act_vjp, down_bwd))
    loss, head_vjp = jax.vjp(_loss_head, h, w["final_norm_g"], target)
    d_h, d_final_g, _ = head_vjp(jnp.ones((), F32))

    grads = {n: [None] * w[n].shape[0] for n in _WEIGHTS if n != "final_norm_g"}
    grads["final_norm_g"] = d_final_g
    for i in reversed(range(DEPTH)):
        kind, j, norm1_vjp, mixer_bwd, norm2_vjp, gu_bwd, act_vjp, down_bwd = tape[i]
        d_act, grads["ffn_w_down"][i] = down_bwd(d_h)
        d_gu, = act_vjp(d_act)
        d_hn2, grads["ffn_w_gu"][i] = gu_bwd(d_gu)
        d_mid, grads["norm_ffn_g"][i] = norm2_vjp(d_hn2)
        d_mid = d_mid + d_h
        d_hn, mixer_grads = mixer_bwd(d_mid)
        for k, g in mixer_grads.items():
            grads[f"{kind}_{k}"][j] = g
        d_in, grads["norm_mix_g"][i] = norm1_vjp(d_hn)
        d_h = d_in + d_mid
    return loss, d_h, grads


def kernel(x, norm_mix_g, norm_ffn_g, gla_w_in, gla_w_a2, gla_b_a, gla_norm_g, gla_w_out, ssd_w_in, ssd_conv_w, ssd_conv_b, ssd_dt_bias, ssd_a_log, ssd_d, ssd_norm_g, ssd_w_out, s5_log_dt, s5_a_re, s5_a_im, s5_b_re, s5_b_im, s5_c_re, s5_c_im, s5_d, s5_w_glu, ffn_w_gu, ffn_w_down, final_norm_g, loss_target, m_norm_mix_g, m_norm_ffn_g, m_gla_w_in, m_gla_w_a2, m_gla_b_a, m_gla_norm_g, m_gla_w_out, m_ssd_w_in, m_ssd_conv_w, m_ssd_conv_b, m_ssd_dt_bias, m_ssd_a_log, m_ssd_d, m_ssd_norm_g, m_ssd_w_out, m_s5_log_dt, m_s5_a_re, m_s5_a_im, m_s5_b_re, m_s5_b_im, m_s5_c_re, m_s5_c_im, m_s5_d, m_s5_w_glu, m_ffn_w_gu, m_ffn_w_down, m_final_norm_g, v_norm_mix_g, v_norm_ffn_g, v_gla_w_in, v_gla_w_a2, v_gla_b_a, v_gla_norm_g, v_gla_w_out, v_ssd_w_in, v_ssd_conv_w, v_ssd_conv_b, v_ssd_dt_bias, v_ssd_a_log, v_ssd_d, v_ssd_norm_g, v_ssd_w_out, v_s5_log_dt, v_s5_a_re, v_s5_a_im, v_s5_b_re, v_s5_b_im, v_s5_c_re, v_s5_c_im, v_s5_d, v_s5_w_glu, v_ffn_w_gu, v_ffn_w_down, v_final_norm_g):
    args = locals()
    local = {n: args[n] for n in _WEIGHTS}
    moment_m = {n: args["m_" + n] for n in _WEIGHTS}
    moment_v = {n: args["v_" + n] for n in _WEIGHTS}

    full = {n: local[n] for n in _REPLICATED}
    full.update({n: _gather_matrix(local[n], n, axis) for n, axis in _BIG})
    full.update(_gather_small(local))

    loss, d_x, grads = _forward_backward(x[0], loss_target[0], full)
    loss = lax.psum(loss, ("x", "y", "c"))
    kinds = ("grad", "delta", "new_m", "new_v")
    out = {}

    for n, axis in _BIG:
        received = []
        for layer, g in enumerate(grads[n]):
            tag = f"{n}_{layer}"
            blocks = _split_cols(g, local[n].shape[2], "split_" + tag) if axis == 2 else g.reshape((N_DEV,) + local[n].shape[1:])
            received.append(_exchange(blocks, "exchange_" + tag))
        results = _adamw(jnp.stack(received, axis=1), local[n], moment_m[n], moment_v[n], "adamw_" + n)
        out.update({f"{kind}_{n}": a for kind, a in zip(kinds, results)})

    small = [n for n, _ in _SMALL] + _REPLICATED
    stacked = lambda n: grads[n] if n == "final_norm_g" else jnp.stack(grads[n])
    parts = _all_gather(_pack([stacked(n) for n in small]), "gather_small_gradients")
    summed = _unpack(_sum_parts(parts, "sum_small_gradients"), [stacked(n).shape for n in small])
    position = _index(_mesh_position())
    mine = [_own_shard(g, axis, position) for g, (_, axis) in zip(summed, _SMALL)] + summed[len(_SMALL):]
    shapes = [local[n].shape for n in small]
    pk = lambda arrays: _pack(arrays)[None]
    results = _adamw(pk(mine)[None], pk([local[n] for n in small]), pk([moment_m[n] for n in small]),
                     pk([moment_v[n] for n in small]), "adamw_small")
    for kind, flat in zip(kinds, results):
        out.update({f"{kind}_{n}": a for n, a in zip(small, _unpack(flat[0], shapes))})

    return (loss, d_x[None], *[out[f"{kind}_{n}"] for kind in ("grad", "delta", "new_m", "new_v") for n in _WEIGHTS])
```

```python
import functools
import math

import jax
import jax.numpy as jnp
import numpy as np
from jax import lax
from jax.experimental import pallas as pl
from jax.experimental.pallas import tpu as pltpu

F32 = jnp.float32
BF16 = jnp.bfloat16
_MXU_DTYPE = jnp.bfloat16

N_DEV = 8
D_MODEL = 1024
DEPTH = 4
CHUNK = 64
STEP_CHUNKS = 1
STEP = CHUNK * STEP_CHUNKS
EPS = 1e-6
GLA_HEADS, GLA_DK, GLA_DV, GLA_RANK, GLA_TAU = 4, 128, 256, 16, 16.0
GLA_QK = GLA_HEADS * GLA_DK
GLA_VD = GLA_HEADS * GLA_DV
LANES = 128
GLA_IN = 2 * GLA_QK + 2 * GLA_VD + GLA_RANK
GLA_PROJ = 2 * GLA_QK + 2 * GLA_VD + LANES
SSD_DINNER, SSD_HEADDIM, SSD_HEADS, SSD_GROUPS, SSD_HPG, SSD_DSTATE, SSD_CONV = 2048, 64, 32, 8, 4, 128, 4
SSD_GN = SSD_GROUPS * SSD_DSTATE
SSD_GW = SSD_HPG * SSD_HEADDIM
SSD_XBC = SSD_DINNER + 2 * SSD_GN
SSD_IN = SSD_DINNER + SSD_XBC + SSD_HEADS
SSD_PROJ = SSD_DINNER + SSD_XBC + LANES
S5_GROUP, S5_GROUPS, S5_STATE = 16, 64, 64
S5_CHUNK = 32
FFN_HIDDEN = 2816
ADAM_LR, ADAM_B1, ADAM_B2, ADAM_EPS, ADAM_WD, ADAM_STEP = 0.001, 0.9, 0.999, 1e-08, 0.01, 10
VMEM_LIMIT = 48 * 1024 * 1024
FLAT_COLS = 1024
FLAT_ROWS_ALIGN = 64


def _tile(n, cap, unit):
    if n <= cap:
        return n
    best = None
    for t in range(unit, cap + 1, unit):
        if n % t == 0:
            best = t
    assert best is not None, (n, cap, unit)
    return best


def _matmul(a, b, *, ta=False, tb=False, out_dtype=F32, name):
    batched = a.ndim == 3
    if ta:
        k_dim, m_dim = a.shape[-2:]
    else:
        m_dim, k_dim = a.shape[-2:]
    if tb:
        n_dim, kb = b.shape[-2:]
    else:
        kb, n_dim = b.shape[-2:]
    assert kb == k_dim, (a.shape, b.shape, ta, tb)
    tm = _tile(m_dim, 1024, LANES if ta else 8)
    tn = _tile(n_dim, 1408, LANES)
    tk = _tile(k_dim, 1024, LANES)
    nk = k_dim // tk
    ca, cb = (0 if ta else 1), (1 if tb else 0)

    def body(a_ref, b_ref, o_ref, *scratch):
        part = lax.dot_general(a_ref[...].astype(_MXU_DTYPE), b_ref[...].astype(_MXU_DTYPE),
                               (((ca,), (cb,)), ((), ())), preferred_element_type=F32)
        if nk == 1:
            o_ref[...] = part.astype(o_ref.dtype)
            return
        acc_ref, = scratch
        k = pl.program_id(3 if batched else 2)

        @pl.when(k == 0)
        def _():
            acc_ref[...] = part

        @pl.when(k > 0)
        def _():
            acc_ref[...] += part

        @pl.when(k == nk - 1)
        def _():
            o_ref[...] = acc_ref[...].astype(o_ref.dtype)

    lead = (None,) if batched else ()

    def spec(shape, fn):
        if batched:
            return pl.BlockSpec(lead + shape, lambda g, i, j, k: (g,) + fn(i, j, k))
        return pl.BlockSpec(shape, fn)

    a_spec = spec((tk, tm), lambda i, j, k: (k, i)) if ta else spec((tm, tk), lambda i, j, k: (i, k))
    b_spec = spec((tn, tk), lambda i, j, k: (j, k)) if tb else spec((tk, tn), lambda i, j, k: (k, j))
    o_spec = spec((tm, tn), lambda i, j, k: (i, j))
    grid = (m_dim // tm, n_dim // tn, nk)
    sem = ("parallel", "parallel", "arbitrary")
    out_shape = (m_dim, n_dim)
    if batched:
        grid = (a.shape[0],) + grid
        sem = ("parallel",) + sem
        out_shape = (a.shape[0],) + out_shape
    return pl.pallas_call(
        body, name=name, grid=grid, in_specs=[a_spec, b_spec], out_specs=o_spec,
        out_shape=jax.ShapeDtypeStruct(out_shape, out_dtype),
        scratch_shapes=[pltpu.VMEM((tm, tn), F32)] if nk > 1 else [],
        compiler_params=pltpu.CompilerParams(dimension_semantics=sem, vmem_limit_bytes=VMEM_LIMIT),
    )(a, b)


def _dot(a, b, ca=1, cb=0, exact=False):
    if exact:
        return lax.dot_general(a, b, (((ca,), (cb,)), ((), ())), precision=lax.Precision.HIGHEST,
                               preferred_element_type=F32)
    return lax.dot_general(a.astype(_MXU_DTYPE), b.astype(_MXU_DTYPE), (((ca,), (cb,)), ((), ())),
                           preferred_element_type=F32)


def _tri(n):
    return lax.broadcasted_iota(jnp.int32, (n, n), 0) >= lax.broadcasted_iota(jnp.int32, (n, n), 1)


def _log_sigmoid(x):
    return jnp.minimum(x, 0.0) - jnp.log(1.0 + jnp.exp(-jnp.abs(x)))


def _softplus(x):
    return jnp.maximum(x, 0.0) + jnp.log(1.0 + jnp.exp(-jnp.abs(x)))


def _silu(x):
    return x / (1.0 + jnp.exp(-x))


def _full_spec(shape):
    return pl.BlockSpec(shape, lambda c: (0,) * len(shape))


_SEQ_PARAMS = pltpu.CompilerParams(dimension_semantics=("arbitrary",), vmem_limit_bytes=VMEM_LIMIT)


def _gla_chunk(proj, st, w_a2, b_a, norm_g):
    t = proj.shape[0]
    q = proj[:, 0:GLA_QK] * (GLA_DK ** -0.5)
    k = proj[:, GLA_QK:2 * GLA_QK]
    v = proj[:, 2 * GLA_QK:2 * GLA_QK + GLA_VD]
    r = proj[:, 2 * GLA_QK + GLA_VD:2 * GLA_QK + 2 * GLA_VD]
    a_low = proj[:, 2 * GLA_QK + 2 * GLA_VD:]
    log_a = _log_sigmoid(_dot(a_low, w_a2) + b_a) * (1.0 / GLA_TAU)
    past = _tri(t)
    lc = _dot(past.astype(F32), log_a, exact=True)
    lend = lc[t - 1:t, :]
    e_pos = jnp.exp(lc)
    e_neg = jnp.exp(-lc)
    q_fwd, k_fwd, q_bwd, k_bwd = q * e_pos, k * e_neg, q * e_neg, k * e_pos
    kd = k * jnp.exp(lend - lc)
    g = jnp.exp(lend)
    outs, new_st = [], []
    for h in range(GLA_HEADS):
        sk = slice(h * GLA_DK, (h + 1) * GLA_DK)
        sv = slice(h * GLA_DV, (h + 1) * GLA_DV)
        s_past = _dot(q_fwd[:, sk], k_fwd[:, sk], 1, 1)
        s_future = _dot(q_bwd[:, sk], k_bwd[:, sk], 1, 1)
        scores = jnp.where(past, s_past, s_future)
        o = _dot(scores, v[:, sv]) + _dot(q_fwd[:, sk], st[h], 1, 1)
        new_st.append(st[h] * g[:, sk] + _dot(v[:, sv], kd[:, sk], 0, 0))
        o = o * lax.rsqrt(jnp.mean(o * o, axis=-1, keepdims=True) + EPS) * norm_g[:, sv]
        outs.append(o)
    return jnp.concatenate(outs, axis=1) * _silu(r), tuple(new_st)


_GLA_STATE = (GLA_HEADS, GLA_DV, GLA_DK)


def _gla_step(proj, st, w_a2, b_a, norm_g):
    outs = []
    for s in range(STEP_CHUNKS):
        out, st = _gla_chunk(proj[s * CHUNK:(s + 1) * CHUNK], st, w_a2, b_a, norm_g)
        outs.append(out)
    return jnp.concatenate(outs, axis=0), st


def _gla_core_fwd(proj, w_a2, b_a, norm_g):
    seq = proj.shape[0]
    nc = seq // STEP

    def body(proj_ref, wa_ref, ba_ref, ng_ref, o_ref, sprev_ref, st_ref):
        @pl.when(pl.program_id(0) == 0)
        def _():
            st_ref[...] = jnp.zeros_like(st_ref)

        st = tuple(st_ref[h] for h in range(GLA_HEADS))
        for h in range(GLA_HEADS):
            sprev_ref[0, h] = st[h]
        out, new_st = _gla_step(proj_ref[...], st, wa_ref[...], ba_ref[...], ng_ref[...])
        o_ref[...] = out
        for h in range(GLA_HEADS):
            st_ref[h] = new_st[h]

    return pl.pallas_call(
        body, name="gla_core_fwd", grid=(nc,),
        in_specs=[pl.BlockSpec((STEP,GLA_PROJ), lambda c: (c, 0)), _full_spec(w_a2.shape), _full_spec(b_a.shape),
                  _full_spec(norm_g.shape)],
        out_specs=[pl.BlockSpec((STEP,GLA_VD), lambda c: (c, 0)), pl.BlockSpec((1,) + _GLA_STATE, lambda c: (c, 0, 0, 0))],
        out_shape=[jax.ShapeDtypeStruct((seq, GLA_VD), F32), jax.ShapeDtypeStruct((nc,) + _GLA_STATE, F32)],
        scratch_shapes=[pltpu.VMEM(_GLA_STATE, F32)],
        compiler_params=_SEQ_PARAMS,
    )(proj, w_a2, b_a, norm_g)


def _gla_core_bwd(proj, sprev, d_out, w_a2, b_a, norm_g):
    seq = proj.shape[0]
    nc = seq // STEP

    def body(proj_ref, sprev_ref, do_ref, wa_ref, ba_ref, ng_ref, dproj_ref, dwa_ref, dba_ref, dng_ref, dst_ref):
        @pl.when(pl.program_id(0) == 0)
        def _():
            dst_ref[...] = jnp.zeros_like(dst_ref)
            dwa_ref[...] = jnp.zeros_like(dwa_ref)
            dba_ref[...] = jnp.zeros_like(dba_ref)
            dng_ref[...] = jnp.zeros_like(dng_ref)

        st = tuple(sprev_ref[0, h] for h in range(GLA_HEADS))
        _, vjp = jax.vjp(_gla_step, proj_ref[...], st, wa_ref[...], ba_ref[...], ng_ref[...])
        d_next = tuple(dst_ref[h] for h in range(GLA_HEADS))
        d_proj, d_st, d_wa, d_ba, d_ng = vjp((do_ref[...], d_next))
        dproj_ref[...] = d_proj
        for h in range(GLA_HEADS):
            dst_ref[h] = d_st[h]
        dwa_ref[...] += d_wa
        dba_ref[...] += d_ba
        dng_ref[...] += d_ng

    rev = lambda c: (nc - 1 - c, 0)
    return pl.pallas_call(
        body, name="gla_core_bwd", grid=(nc,),
        in_specs=[pl.BlockSpec((STEP,GLA_PROJ), rev), pl.BlockSpec((1,) + _GLA_STATE, lambda c: (nc - 1 - c, 0, 0, 0)),
                  pl.BlockSpec((STEP,GLA_VD), rev), _full_spec(w_a2.shape), _full_spec(b_a.shape), _full_spec(norm_g.shape)],
        out_specs=[pl.BlockSpec((STEP,GLA_PROJ), rev), _full_spec(w_a2.shape), _full_spec(b_a.shape), _full_spec(norm_g.shape)],
        out_shape=[jax.ShapeDtypeStruct((seq, GLA_PROJ), F32), jax.ShapeDtypeStruct(w_a2.shape, F32),
                   jax.ShapeDtypeStruct(b_a.shape, F32), jax.ShapeDtypeStruct(norm_g.shape, F32)],
        scratch_shapes=[pltpu.VMEM(_GLA_STATE, F32)],
        compiler_params=_SEQ_PARAMS,
    )(proj, sprev, d_out, w_a2, b_a, norm_g)


def _ssd_chunk(z, xbc, dt_raw, hs, dt_bias, a_log, d_skip, norm_g):
    t = z.shape[0]
    xs = xbc[:, :SSD_DINNER]
    bm = xbc[:, SSD_DINNER:SSD_DINNER + SSD_GN]
    cm = xbc[:, SSD_DINNER + SSD_GN:]
    dt = _softplus(dt_raw + dt_bias)
    da = dt * (-jnp.exp(a_log))
    tri = _tri(t).astype(F32)
    eye = (lax.broadcasted_iota(jnp.int32, (t, t), 0) == lax.broadcasted_iota(jnp.int32, (t, t), 1)).astype(F32)
    cum = _dot(tri, da, exact=True)
    cum_t = _dot(da, tri, 0, 1, exact=True)
    dt_t = _dot(dt, eye, 0, 0, exact=True)
    cum_end = cum[t - 1:t, :]
    w_state = dt * jnp.exp(cum_end - cum)
    e_cum = jnp.exp(cum)
    g_end = jnp.exp(cum_end)
    head_of = lambda axis: lax.shift_right_logical(lax.broadcasted_iota(jnp.int32, (SSD_GW, SSD_GW), axis),
                                                   jnp.int32(SSD_HEADDIM.bit_length() - 1))
    same_head = head_of(0) == head_of(1)
    ys, new_hs = [], []
    for g in range(SSD_GROUPS):
        heads = range(g * SSD_HPG, (g + 1) * SSD_HPG)
        cols = slice(g * SSD_GW, (g + 1) * SSD_GW)

        def spread(a):
            return jnp.concatenate([jnp.broadcast_to(a[:, h:h + 1], (a.shape[0], SSD_HEADDIM)) for h in heads], axis=1)

        def row(a_t):
            return jnp.concatenate([a_t[h:h + 1, :] for h in heads], axis=1)

        bm_g = bm[:, g * SSD_DSTATE:(g + 1) * SSD_DSTATE]
        cm_g = cm[:, g * SSD_DSTATE:(g + 1) * SSD_DSTATE]
        xs_g = xs[:, cols]
        cb = _dot(cm_g, jnp.concatenate([bm_g] * SSD_HPG, axis=0), 1, 1)
        mix = cb * jnp.exp(-jnp.abs(spread(cum) - row(cum_t))) * row(dt_t)
        x_diag = jnp.where(same_head, jnp.concatenate([xs_g] * SSD_HPG, axis=0), 0.0)
        y = _dot(mix, x_diag)
        y = y + _dot(cm_g, hs[g], 1, 1) * spread(e_cum)
        y = y + spread(d_skip) * xs_g
        states = _dot(xs_g * spread(w_state), bm_g, 0, 0)
        decayed = jnp.concatenate([g_end[:, h:h + 1] * hs[g][j * SSD_HEADDIM:(j + 1) * SSD_HEADDIM, :]
                                   for j, h in enumerate(heads)], axis=0)
        new_hs.append(decayed + states)
        yg = y * _silu(z[:, cols])
        ys.append(yg * lax.rsqrt(jnp.mean(yg * yg, axis=-1, keepdims=True) + EPS) * norm_g[:, cols])
    return jnp.concatenate(ys, axis=1), tuple(new_hs)


_SSD_STATE = (SSD_GROUPS, SSD_GW, SSD_DSTATE)


def _ssd_step(z, xbc, dt_raw, hs, dt_bias, a_log, d_skip, norm_g):
    outs = []
    for s in range(STEP_CHUNKS):
        rows = slice(s * CHUNK, (s + 1) * CHUNK)
        out, hs = _ssd_chunk(z[rows], xbc[rows], dt_raw[rows], hs, dt_bias, a_log, d_skip, norm_g)
        outs.append(out)
    return jnp.concatenate(outs, axis=0), hs
_SSD_DT_BLOCK = (SSD_DINNER + SSD_XBC) // LANES


def _ssd_core_fwd(proj, xbc, dt_bias, a_log, d_skip, norm_g):
    seq = proj.shape[0]
    nc = seq // STEP

    def body(z_ref, xbc_ref, dt_ref, db_ref, al_ref, ds_ref, ng_ref, o_ref, hprev_ref, hs_ref):
        @pl.when(pl.program_id(0) == 0)
        def _():
            hs_ref[...] = jnp.zeros_like(hs_ref)

        hs = tuple(hs_ref[g] for g in range(SSD_GROUPS))
        for g in range(SSD_GROUPS):
            hprev_ref[0, g] = hs[g]
        out, new_hs = _ssd_step(z_ref[...], xbc_ref[...], dt_ref[...], hs, db_ref[...], al_ref[...], ds_ref[...], ng_ref[...])
        o_ref[...] = out
        for g in range(SSD_GROUPS):
            hs_ref[g] = new_hs[g]

    return pl.pallas_call(
        body, name="ssd_core_fwd", grid=(nc,),
        in_specs=[pl.BlockSpec((STEP,SSD_DINNER), lambda c: (c, 0)), pl.BlockSpec((STEP,SSD_XBC), lambda c: (c, 0)),
                  pl.BlockSpec((STEP,LANES), lambda c: (c, _SSD_DT_BLOCK)),
                  _full_spec(dt_bias.shape), _full_spec(a_log.shape), _full_spec(d_skip.shape), _full_spec(norm_g.shape)],
        out_specs=[pl.BlockSpec((STEP,SSD_DINNER), lambda c: (c, 0)), pl.BlockSpec((1,) + _SSD_STATE, lambda c: (c, 0, 0, 0))],
        out_shape=[jax.ShapeDtypeStruct((seq, SSD_DINNER), F32), jax.ShapeDtypeStruct((nc,) + _SSD_STATE, F32)],
        scratch_shapes=[pltpu.VMEM(_SSD_STATE, F32)],
        compiler_params=_SEQ_PARAMS,
    )(proj, xbc, proj, dt_bias, a_log, d_skip, norm_g)


def _ssd_core_bwd(proj, xbc, hprev, d_out, dt_bias, a_log, d_skip, norm_g):
    seq = proj.shape[0]
    nc = seq // STEP

    def body(z_ref, xbc_ref, dt_ref, hprev_ref, do_ref, db_ref, al_ref, ds_ref, ng_ref,
             dz_ref, dxbc_ref, ddt_ref, ddb_ref, dal_ref, dds_ref, dng_ref, dhs_ref):
        @pl.when(pl.program_id(0) == 0)
        def _():
            dhs_ref[...] = jnp.zeros_like(dhs_ref)
            ddb_ref[...] = jnp.zeros_like(ddb_ref)
            dal_ref[...] = jnp.zeros_like(dal_ref)
            dds_ref[...] = jnp.zeros_like(dds_ref)
            dng_ref[...] = jnp.zeros_like(dng_ref)

        hs = tuple(hprev_ref[0, g] for g in range(SSD_GROUPS))
        _, vjp = jax.vjp(_ssd_step, z_ref[...], xbc_ref[...], dt_ref[...], hs, db_ref[...], al_ref[...], ds_ref[...], ng_ref[...])
        d_next = tuple(dhs_ref[g] for g in range(SSD_GROUPS))
        d_z, d_xbc, d_dt, d_hs, d_db, d_al, d_ds, d_ng = vjp((do_ref[...], d_next))
        dz_ref[...] = d_z
        dxbc_ref[...] = d_xbc
        ddt_ref[...] = d_dt
        for g in range(SSD_GROUPS):
            dhs_ref[g] = d_hs[g]
        ddb_ref[...] += d_db
        dal_ref[...] += d_al
        dds_ref[...] += d_ds
        dng_ref[...] += d_ng

    rev = lambda c: (nc - 1 - c, 0)
    vec = [_full_spec(dt_bias.shape), _full_spec(a_log.shape), _full_spec(d_skip.shape), _full_spec(norm_g.shape)]
    return pl.pallas_call(
        body, name="ssd_core_bwd", grid=(nc,),
        in_specs=[pl.BlockSpec((STEP,SSD_DINNER), rev), pl.BlockSpec((STEP,SSD_XBC), rev),
                  pl.BlockSpec((STEP,LANES), lambda c: (nc - 1 - c, _SSD_DT_BLOCK)),
                  pl.BlockSpec((1,) + _SSD_STATE, lambda c: (nc - 1 - c, 0, 0, 0)),
                  pl.BlockSpec((STEP,SSD_DINNER), rev)] + vec,
        out_specs=[pl.BlockSpec((STEP,SSD_DINNER), rev), pl.BlockSpec((STEP,SSD_XBC), rev),
                   pl.BlockSpec((STEP,LANES), rev)] + vec,
        out_shape=[jax.ShapeDtypeStruct((seq, SSD_DINNER), F32), jax.ShapeDtypeStruct((seq, SSD_XBC), F32),
                   jax.ShapeDtypeStruct((seq, LANES), F32),
                   jax.ShapeDtypeStruct(dt_bias.shape, F32), jax.ShapeDtypeStruct(a_log.shape, F32),
                   jax.ShapeDtypeStruct(d_skip.shape, F32), jax.ShapeDtypeStruct(norm_g.shape, F32)],
        scratch_shapes=[pltpu.VMEM(_SSD_STATE, F32)],
        compiler_params=_SEQ_PARAMS,
    )(proj, xbc, proj, hprev, d_out, dt_bias, a_log, d_skip, norm_g)


def _s5_boundary_scan(z_re, z_im, lam_re, lam_im):
    n_chunks = z_re.shape[0]

    def body(zr_ref, zi_ref, lr_ref, li_ref, xr_ref, xi_ref):
        lr, li = lr_ref[...], li_ref[...]

        def step(n, carry):
            xr, xi = carry
            xr_ref[n] = xr
            xi_ref[n] = xi
            return lr * xr - li * xi + zr_ref[n], lr * xi + li * xr + zi_ref[n]

        zero = jnp.zeros(lr.shape, F32)
        lax.fori_loop(0, n_chunks, step, (zero, zero))

    shape = jax.ShapeDtypeStruct(z_re.shape, F32)
    return pl.pallas_call(body, name="s5_boundary_scan", out_shape=[shape, shape],
                          compiler_params=pltpu.CompilerParams(vmem_limit_bytes=VMEM_LIMIT))(z_re, z_im, lam_re, lam_im)


_FLIPS = [(kx, ky, kc) for kx in (0, 1) for ky in (0, 1) for kc in (0, 1)][1:]


def _mesh_position():
    return lax.axis_index("x"), lax.axis_index("y"), lax.axis_index("c")


def _peer(pos, flip):
    return tuple((1 - p) if f else p for p, f in zip(pos, flip))


def _index(pos):
    return 4 * pos[0] + 2 * pos[1] + pos[2]


_ANY = pl.BlockSpec(memory_space=pl.ANY)


def _all_gather(x, name, axis=0):
    lead = (slice(None),) * axis

    def body(x_ref, out_ref, send_sems, recv_sems, local_sem):
        me = _mesh_position()
        slot = lambda pos: out_ref.at[lead + (_index(pos),)]
        mine = pltpu.make_async_copy(x_ref, slot(me), local_sem)
        mine.start()
        copies = []
        for k, flip in enumerate(_FLIPS):
            peer = _peer(me, flip)
            cp = pltpu.make_async_remote_copy(src_ref=x_ref, dst_ref=slot(me), send_sem=send_sems.at[k],
                                              recv_sem=recv_sems.at[k], device_id=peer, device_id_type=pl.DeviceIdType.MESH)
            cp.start()
            copies.append(cp)
        for k, flip in enumerate(_FLIPS):
            peer = _peer(me, flip)
            pltpu.make_async_remote_copy(src_ref=x_ref, dst_ref=slot(peer), send_sem=send_sems.at[k],
                                         recv_sem=recv_sems.at[k], device_id=peer,
                                         device_id_type=pl.DeviceIdType.MESH).wait_recv()
        for cp in copies:
            cp.wait_send()
        mine.wait()

    return pl.pallas_call(
        body, name=name, in_specs=[_ANY], out_specs=_ANY,
        out_shape=jax.ShapeDtypeStruct(x.shape[:axis] + (N_DEV,) + x.shape[axis:], x.dtype),
        scratch_shapes=[pltpu.SemaphoreType.DMA((N_DEV - 1,)), pltpu.SemaphoreType.DMA((N_DEV - 1,)), pltpu.SemaphoreType.DMA],
        compiler_params=pltpu.CompilerParams(has_side_effects=True),
    )(x)


def _exchange(x, name):
    def body(x_ref, out_ref, send_sems, recv_sems, local_sem):
        me = _mesh_position()
        mine = pltpu.make_async_copy(x_ref.at[_index(me)], out_ref.at[_index(me)], local_sem)
        mine.start()
        copies = []
        for k, flip in enumerate(_FLIPS):
            peer = _peer(me, flip)
            cp = pltpu.make_async_remote_copy(src_ref=x_ref.at[_index(peer)], dst_ref=out_ref.at[_index(me)],
                                              send_sem=send_sems.at[k], recv_sem=recv_sems.at[k], device_id=peer,
                                              device_id_type=pl.DeviceIdType.MESH)
            cp.start()
            copies.append(cp)
        for k, flip in enumerate(_FLIPS):
            peer = _peer(me, flip)
            pltpu.make_async_remote_copy(src_ref=x_ref.at[_index(peer)], dst_ref=out_ref.at[_index(peer)],
                                         send_sem=send_sems.at[k], recv_sem=recv_sems.at[k], device_id=peer,
                                         device_id_type=pl.DeviceIdType.MESH).wait_recv()
        for cp in copies:
            cp.wait_send()
        mine.wait()

    return pl.pallas_call(
        body, name=name, in_specs=[_ANY], out_specs=_ANY,
        out_shape=jax.ShapeDtypeStruct(x.shape, x.dtype),
        scratch_shapes=[pltpu.SemaphoreType.DMA((N_DEV - 1,)), pltpu.SemaphoreType.DMA((N_DEV - 1,)), pltpu.SemaphoreType.DMA],
        compiler_params=pltpu.CompilerParams(has_side_effects=True),
    )(x)


def _adamw(parts, w, m, v, name):
    n_parts = parts.shape[0]
    layers, rows, cols = w.shape
    tr = _tile(rows, 256, 8)

    def body(p_ref, w_ref, m_ref, v_ref, g_ref, d_ref, mo_ref, vo_ref):
        g = p_ref[0].astype(F32)
        for s in range(1, n_parts):
            g = g + p_ref[s].astype(F32)
        m_new = ADAM_B1 * m_ref[...] + (1.0 - ADAM_B1) * g
        v_new = ADAM_B2 * v_ref[...] + (1.0 - ADAM_B2) * (g * g)
        m_hat = m_new / (1.0 - ADAM_B1 ** ADAM_STEP)
        v_hat = v_new / (1.0 - ADAM_B2 ** ADAM_STEP)
        g_ref[...] = g
        d_ref[...] = -ADAM_LR * (m_hat / (jnp.sqrt(v_hat) + ADAM_EPS) + ADAM_WD * w_ref[...])
        mo_ref[...] = m_new
        vo_ref[...] = v_new

    blk = pl.BlockSpec((None, tr, cols), lambda l, i: (l, i, 0))
    shape = jax.ShapeDtypeStruct(w.shape, F32)
    return pl.pallas_call(
        body, name=name, grid=(layers, rows // tr),
        in_specs=[pl.BlockSpec((n_parts, None, tr, cols), lambda l, i: (0, l, i, 0)), blk, blk, blk],
        out_specs=[blk, blk, blk, blk], out_shape=[shape, shape, shape, shape],
        compiler_params=pltpu.CompilerParams(dimension_semantics=("parallel", "parallel"), vmem_limit_bytes=VMEM_LIMIT),
    )(parts, w, m, v)


def _sum_parts(parts, name):
    _, rows, cols = parts.shape
    tr = _tile(rows, 256, 8)

    def body(p_ref, o_ref):
        total = p_ref[0]
        for s in range(1, N_DEV):
            total = total + p_ref[s]
        o_ref[...] = total

    return pl.pallas_call(
        body, name=name, grid=(rows // tr,),
        in_specs=[pl.BlockSpec((N_DEV, tr, cols), lambda i: (0, i, 0))], out_specs=pl.BlockSpec((tr, cols), lambda i: (i, 0)),
        out_shape=jax.ShapeDtypeStruct((rows, cols), parts.dtype),
        compiler_params=pltpu.CompilerParams(dimension_semantics=("parallel",), vmem_limit_bytes=VMEM_LIMIT),
    )(parts)


def _join_cols(blocks, n_out, name):
    _, layers, rows, n = blocks.shape
    tr = _tile(rows, 256, 16)

    def body(x_ref, o_ref):
        for d in range(N_DEV):
            o_ref[:, d * n:(d + 1) * n] = x_ref[d]
        if n_out > N_DEV * n:
            o_ref[:, N_DEV * n:] = jnp.zeros((tr, n_out - N_DEV * n), o_ref.dtype)

    return pl.pallas_call(
        body, name=name, grid=(layers, rows // tr),
        in_specs=[pl.BlockSpec((N_DEV, None, tr, n), lambda l, i: (0, l, i, 0))],
        out_specs=pl.BlockSpec((None, tr, n_out), lambda l, i: (l, i, 0)),
        out_shape=jax.ShapeDtypeStruct((layers, rows, n_out), blocks.dtype),
        compiler_params=pltpu.CompilerParams(dimension_semantics=("parallel", "parallel"), vmem_limit_bytes=VMEM_LIMIT),
    )(blocks)


def _split_cols(full, n, name):
    rows = full.shape[0]
    tr = _tile(rows, 256, 16)

    def body(x_ref, o_ref):
        for d in range(N_DEV):
            o_ref[d] = x_ref[:, d * n:(d + 1) * n]

    return pl.pallas_call(
        body, name=name, grid=(rows // tr,),
        in_specs=[pl.BlockSpec((tr, full.shape[1]), lambda i: (i, 0))],
        out_specs=pl.BlockSpec((N_DEV, tr, n), lambda i: (0, i, 0)),
        out_shape=jax.ShapeDtypeStruct((N_DEV, rows, n), full.dtype),
        compiler_params=pltpu.CompilerParams(dimension_semantics=("parallel",), vmem_limit_bytes=VMEM_LIMIT),
    )(full)


def _pack(arrays):
    flat = jnp.concatenate([a.reshape(-1) for a in arrays])
    unit = FLAT_COLS * FLAT_ROWS_ALIGN
    padded = -(-flat.shape[0] // unit) * unit
    return jnp.pad(flat, (0, padded - flat.shape[0])).reshape(-1, FLAT_COLS)


def _unpack(flat, shapes, lead=()):
    flat = flat.reshape(lead + (-1,))
    out, off = [], 0
    for shape in shapes:
        n = math.prod(shape)
        out.append(flat[..., off:off + n].reshape(lead + tuple(shape)))
        off += n
    return out


def _join(blocks, axis):
    moved = jnp.moveaxis(blocks, 0, axis)
    shape = list(moved.shape)
    shape[axis:axis + 2] = [shape[axis] * shape[axis + 1]]
    return moved.reshape(shape)


def _own_shard(full, axis, position):
    n = full.shape[axis] // N_DEV
    return lax.dynamic_slice_in_dim(full, position * n, n, axis)


def _rmsnorm(x, g):
    return x * lax.rsqrt(jnp.mean(x * x, axis=-1, keepdims=True) + EPS) * g


def _swiglu(gu):
    gu = gu.astype(F32)
    return (jax.nn.silu(gu[:, :FFN_HIDDEN]) * gu[:, FFN_HIDDEN:]).astype(BF16)


def _ssd_conv(xbc, conv_w, conv_b):
    seq = xbc.shape[0]
    padded = jnp.pad(xbc, ((SSD_CONV - 1, 0), (0, 0)))
    out = conv_b
    for k in range(SSD_CONV):
        out = out + padded[k:k + seq] * conv_w[k]
    return jax.nn.silu(out)


def _loss_head(h, g, target):
    err = _rmsnorm(h, g) - target
    return 0.5 * jnp.sum(jnp.mean(err * err, axis=-1))


def _s5_operators(log_dt, a_re, a_im, b_re, b_im, c_re, c_im):
    t = S5_CHUNK
    hi = lax.Precision.HIGHEST
    step = jnp.exp(log_dt)[:, None]
    mag = jnp.exp(step * a_re)
    abar_re = mag * jnp.cos(step * a_im)
    abar_im = mag * jnp.sin(step * a_im)
    den = a_re * a_re + a_im * a_im
    f_re = ((abar_re - 1.0) * a_re + abar_im * a_im) / den
    f_im = (abar_im * a_re - (abar_re - 1.0) * a_im) / den
    bb_re = f_re[..., None] * b_re - f_im[..., None] * b_im
    bb_im = f_re[..., None] * b_im + f_im[..., None] * b_re
    j = jnp.arange(t + 1, dtype=F32)[:, None, None]
    pmag = jnp.exp(j * (step * a_re))
    pw_re = pmag * jnp.cos(j * (step * a_im))
    pw_im = pmag * jnp.sin(j * (step * a_im))
    cl_re = c_re[None] * pw_re[:t, :, None, :] - c_im[None] * pw_im[:t, :, None, :]
    cl_im = c_re[None] * pw_im[:t, :, None, :] + c_im[None] * pw_re[:t, :, None, :]
    kern = (jnp.einsum('jgcp,gpk->jgck', cl_re, bb_re, precision=hi)
            - jnp.einsum('jgcp,gpk->jgck', cl_im, bb_im, precision=hi))
    rp_re, rp_im = pw_re[:t][::-1], pw_im[:t][::-1]
    wz_re = rp_re[:, :, :, None] * bb_re[None] - rp_im[:, :, :, None] * bb_im[None]
    wz_im = rp_re[:, :, :, None] * bb_im[None] + rp_im[:, :, :, None] * bb_re[None]
    w_z = jnp.concatenate([wz_re, wz_im], axis=2).transpose(1, 0, 3, 2).reshape(S5_GROUPS, t * S5_GROUP, 2 * S5_STATE)
    cy_re = c_re[None] * pw_re[1:, :, None, :] - c_im[None] * pw_im[1:, :, None, :]
    cy_im = c_re[None] * pw_im[1:, :, None, :] + c_im[None] * pw_re[1:, :, None, :]
    w_y = jnp.concatenate([cy_re, -cy_im], axis=3).transpose(1, 3, 0, 2).reshape(S5_GROUPS, 2 * S5_STATE, t * S5_GROUP)
    return kern, w_z, w_y, pw_re[t], pw_im[t]


def _s5_lag_selector():
    t = S5_CHUNK
    lag = jnp.arange(t)[:, None] - jnp.arange(t)[None, :]
    return (lag[:, :, None] == jnp.arange(t)[None, None, :]).astype(F32).reshape(t * t, t)


def _s5_toeplitz(kern, tag):
    t = S5_CHUNK
    sel = _s5_lag_selector()
    flat = _matmul(sel, kern.reshape(t, -1), out_dtype=BF16, name=tag + "_toeplitz")
    toep = flat.reshape(t, t, S5_GROUPS, S5_GROUP, S5_GROUP).transpose(2, 1, 4, 0, 3)
    toep = toep.reshape(S5_GROUPS, t * S5_GROUP, t * S5_GROUP)

    def backward(d_toep):
        d_flat = d_toep.reshape(S5_GROUPS, t, S5_GROUP, t, S5_GROUP).transpose(3, 1, 0, 4, 2).reshape(t * t, -1)
        return _matmul(sel, d_flat, ta=True, name=tag + "_toeplitz_dw").reshape(kern.shape)

    return toep, backward


def _s5_gate(y, u, d_skip):
    return jax.nn.gelu(y + d_skip * u)


def _glu(vg):
    return vg[:, :D_MODEL] * jax.nn.sigmoid(vg[:, D_MODEL:])


def _linear(x, w, tag, out_dtype=F32, dx_dtype=F32):
    y = _matmul(x, w, out_dtype=out_dtype, name=tag + "_fwd")

    def backward(dy):
        return (_matmul(dy, w, tb=True, out_dtype=dx_dtype, name=tag + "_dx"),
                _matmul(x, dy, ta=True, out_dtype=BF16, name=tag + "_dw"))

    return y, backward


def _gla_mixer(hn, p, tag):
    w_in = p["w_in"]
    w_a2 = jnp.pad(p["w_a2"], ((0, LANES - GLA_RANK), (0, 0)))
    b_a, norm_g = p["b_a"][None], p["norm_g"][None]
    proj, lin_in = _linear(hn, w_in, tag + "_in")
    o, sprev = _gla_core_fwd(proj, w_a2, b_a, norm_g)
    y, lin_out = _linear(o, p["w_out"], tag + "_out")

    def backward(dy):
        d_o, d_wout = lin_out(dy)
        d_proj, d_wa, d_ba, d_ng = _gla_core_bwd(proj, sprev, d_o, w_a2, b_a, norm_g)
        d_hn, d_win = lin_in(d_proj)
        return d_hn, dict(w_in=d_win, w_a2=d_wa[:GLA_RANK], b_a=d_ba[0], norm_g=d_ng[0], w_out=d_wout)

    return y, backward


def _ssd_mixer(hn, p, tag):
    pad = lambda a: jnp.pad(a[None], ((0, 0), (0, LANES - SSD_HEADS)))
    w_in = p["w_in"]
    dt_bias, a_log, d_skip, norm_g = pad(p["dt_bias"]), pad(p["a_log"]), pad(p["d"]), p["norm_g"][None]
    proj, lin_in = _linear(hn, w_in, tag + "_in")
    xbc, conv_vjp = jax.vjp(_ssd_conv, proj[:, SSD_DINNER:SSD_DINNER + SSD_XBC], p["conv_w"], p["conv_b"])
    o, hprev = _ssd_core_fwd(proj, xbc, dt_bias, a_log, d_skip, norm_g)
    y, lin_out = _linear(o, p["w_out"], tag + "_out")

    def backward(dy):
        d_o, d_wout = lin_out(dy)
        d_z, d_xbc, d_dt, d_db, d_al, d_ds, d_ng = _ssd_core_bwd(proj, xbc, hprev, d_o, dt_bias, a_log, d_skip, norm_g)
        d_pre, d_cw, d_cb = conv_vjp(d_xbc)
        d_hn, d_win = lin_in(jnp.concatenate([d_z, d_pre, d_dt], axis=1))
        return d_hn, dict(w_in=d_win, conv_w=d_cw, conv_b=d_cb, dt_bias=d_db[0, :SSD_HEADS],
                          a_log=d_al[0, :SSD_HEADS], d=d_ds[0, :SSD_HEADS], norm_g=d_ng[0], w_out=d_wout)

    return y, backward


def _s5_mixer(hn, p, tag):
    seq = hn.shape[0]
    t, n_chunks = S5_CHUNK, hn.shape[0] // S5_CHUNK
    names = ("log_dt", "a_re", "a_im", "b_re", "b_im", "c_re", "c_im")
    (kern, w_z, w_y, lam_re, lam_im), ops_vjp = jax.vjp(_s5_operators, *[p[k] for k in names])
    toep, toep_bwd = _s5_toeplitz(kern, tag)
    to_groups = lambda a: a.reshape(n_chunks, t, S5_GROUPS, S5_GROUP).transpose(2, 0, 1, 3).reshape(S5_GROUPS, n_chunks, t * S5_GROUP)
    from_groups = lambda a: a.reshape(S5_GROUPS, n_chunks, t, S5_GROUP).transpose(1, 2, 0, 3).reshape(seq, D_MODEL)
    ug = to_groups(hn)
    z = _matmul(ug, w_z, name=tag + "_z")
    z_re, z_im = z[..., :S5_STATE].transpose(1, 0, 2), z[..., S5_STATE:].transpose(1, 0, 2)
    x_re, x_im = _s5_boundary_scan(z_re, z_im, lam_re, lam_im)
    xprev = jnp.concatenate([x_re, x_im], axis=2).transpose(1, 0, 2)
    yg = _matmul(ug, toep, name=tag + "_intra") + _matmul(xprev, w_y, name=tag + "_inter")
    act, gate_vjp = jax.vjp(_s5_gate, from_groups(yg), hn, p["d"])
    vg, lin_glu = _linear(act, p["w_glu"], tag + "_glu")
    out, glu_vjp = jax.vjp(_glu, vg)

    def backward(dy):
        d_vg, = glu_vjp(dy)
        d_act, d_wglu = lin_glu(d_vg)
        d_y, d_hn, d_d = gate_vjp(d_act)
        d_yg = to_groups(d_y)
        d_ug = _matmul(d_yg, toep, tb=True, name=tag + "_intra_dx")
        d_toep = _matmul(ug, d_yg, ta=True, out_dtype=BF16, name=tag + "_intra_dw")
        d_xprev = _matmul(d_yg, w_y, tb=True, name=tag + "_inter_dx").transpose(1, 0, 2)
        d_wy = _matmul(xprev, d_yg, ta=True, name=tag + "_inter_dw")
        dz_re, dz_im = _s5_boundary_scan(d_xprev[::-1, :, :S5_STATE], d_xprev[::-1, :, S5_STATE:], lam_re, -lam_im)
        dz_re, dz_im = dz_re[::-1], dz_im[::-1]
        d_lam_re = jnp.sum(x_re * dz_re + x_im * dz_im, axis=0)
        d_lam_im = jnp.sum(x_re * dz_im - x_im * dz_re, axis=0)
        d_z = jnp.concatenate([dz_re, dz_im], axis=2).transpose(1, 0, 2)
        d_ug = d_ug + _matmul(d_z, w_z, tb=True, name=tag + "_z_dx")
        d_wz = _matmul(ug, d_z, ta=True, name=tag + "_z_dw")
        grads = dict(zip(names, ops_vjp((toep_bwd(d_toep), d_wz, d_wy, d_lam_re, d_lam_im))))
        grads.update(d=d_d, w_glu=d_wglu)
        return d_hn + from_groups(d_ug), grads

    return out, backward


_BIG = [("gla_w_in", 2), ("gla_w_out", 1), ("ssd_w_in", 2), ("ssd_w_out", 1), ("s5_w_glu", 2), ("ffn_w_gu", 2),
        ("ffn_w_down", 1)]
_PADDED_COLS = {"gla_w_in": GLA_PROJ, "ssd_w_in": SSD_PROJ}
_SMALL = [("gla_w_a2", 2), ("gla_b_a", 1), ("gla_norm_g", 1), ("ssd_conv_w", 2), ("s5_d", 1)]
_REPLICATED = ["norm_mix_g", "norm_ffn_g", "ssd_conv_b", "ssd_dt_bias", "ssd_a_log", "ssd_d", "ssd_norm_g", "s5_log_dt",
               "s5_a_re", "s5_a_im", "s5_b_re", "s5_b_im", "s5_c_re", "s5_c_im", "final_norm_g"]
_WEIGHTS = ['norm_mix_g', 'norm_ffn_g', 'gla_w_in', 'gla_w_a2', 'gla_b_a', 'gla_norm_g', 'gla_w_out', 'ssd_w_in',
            'ssd_conv_w', 'ssd_conv_b', 'ssd_dt_bias', 'ssd_a_log', 'ssd_d', 'ssd_norm_g', 'ssd_w_out', 's5_log_dt',
            's5_a_re', 's5_a_im', 's5_b_re', 's5_b_im', 's5_c_re', 's5_c_im', 's5_d', 's5_w_glu', 'ffn_w_gu', 'ffn_w_down',
            'final_norm_g']


def _gather_matrix(shard, name, axis):
    shard = shard.astype(BF16)
    if axis == 1:
        blocks = _all_gather(shard, "gather_" + name, axis=1)
        return blocks.reshape(shard.shape[0], N_DEV * shard.shape[1], shard.shape[2])
    blocks = _all_gather(shard, "gather_" + name)
    return _join_cols(blocks, _PADDED_COLS.get(name, N_DEV * shard.shape[2]), "join_" + name)


def _gather_small(local):
    shapes = [local[n].shape for n, _ in _SMALL]
    blocks = _all_gather(_pack([local[n] for n, _ in _SMALL]), "gather_vectors")
    parts = _unpack(blocks, shapes, lead=(N_DEV,))
    return {n: _join(part, axis) for (n, axis), part in zip(_SMALL, parts)}


def _forward_backward(x, target, w):
    gla = lambda j: dict(w_in=w["gla_w_in"][j], w_a2=w["gla_w_a2"][j], b_a=w["gla_b_a"][j], norm_g=w["gla_norm_g"][j],
                         w_out=w["gla_w_out"][j])
    ssd = lambda j: dict(w_in=w["ssd_w_in"][j], conv_w=w["ssd_conv_w"][j], conv_b=w["ssd_conv_b"][j],
                         dt_bias=w["ssd_dt_bias"][j], a_log=w["ssd_a_log"][j], d=w["ssd_d"][j], norm_g=w["ssd_norm_g"][j],
                         w_out=w["ssd_w_out"][j])
    s5 = lambda j: dict(log_dt=w["s5_log_dt"][j], a_re=w["s5_a_re"][j], a_im=w["s5_a_im"][j], b_re=w["s5_b_re"][j],
                        b_im=w["s5_b_im"][j], c_re=w["s5_c_re"][j], c_im=w["s5_c_im"][j], d=w["s5_d"][j], w_glu=w["s5_w_glu"][j])
    mixers = [("gla", _gla_mixer, gla), ("ssd", _ssd_mixer, ssd), ("s5", _s5_mixer, s5)]
    h = x
    tape = []
    for i in range(DEPTH):
        kind, mixer, params = mixers[i % 3]
        j = i // 3
        hn, norm1_vjp = jax.vjp(_rmsnorm, h, w["norm_mix_g"][i])
        y, mixer_bwd = mixer(hn.astype(BF16), params(j), f"l{i}_{kind}")
        h_mid = h + y
        hn2, norm2_vjp = jax.vjp(_rmsnorm, h_mid, w["norm_ffn_g"][i])
        gu, gu_bwd = _linear(hn2.astype(BF16), w["ffn_w_gu"][i], f"l{i}_ffn_gu", out_dtype=BF16)
        act, act_vjp = jax.vjp(_swiglu, gu)
        f, down_bwd = _linear(act, w["ffn_w_down"][i], f"l{i}_ffn_down", dx_dtype=BF16)
        h = h_mid + f
        tape.append((kind, j, norm1_vjp, mixer_bwd, norm2_vjp, gu_bwd, act_vjp, down_bwd))
    loss, head_vjp = jax.vjp(_loss_head, h, w["final_norm_g"], target)
    d_h, d_final_g, _ = head_vjp(jnp.ones((), F32))

    grads = {n: [None] * w[n].shape[0] for n in _WEIGHTS if n != "final_norm_g"}
    grads["final_norm_g"] = d_final_g
    for i in reversed(range(DEPTH)):
        kind, j, norm1_vjp, mixer_bwd, norm2_vjp, gu_bwd, act_vjp, down_bwd = tape[i]
        d_act, grads["ffn_w_down"][i] = down_bwd(d_h)
        d_gu, = act_vjp(d_act)
        d_hn2, grads["ffn_w_gu"][i] = gu_bwd(d_gu)
        d_mid, grads["norm_ffn_g"][i] = norm2_vjp(d_hn2)
        d_mid = d_mid + d_h
        d_hn, mixer_grads = mixer_bwd(d_mid)
        for k, g in mixer_grads.items():
            grads[f"{kind}_{k}"][j] = g
        d_in, grads["norm_mix_g"][i] = norm1_vjp(d_hn)
        d_h = d_in + d_mid
    return loss, d_h, grads


def kernel(x, norm_mix_g, norm_ffn_g, gla_w_in, gla_w_a2, gla_b_a, gla_norm_g, gla_w_out, ssd_w_in, ssd_conv_w, ssd_conv_b, ssd_dt_bias, ssd_a_log, ssd_d, ssd_norm_g, ssd_w_out, s5_log_dt, s5_a_re, s5_a_im, s5_b_re, s5_b_im, s5_c_re, s5_c_im, s5_d, s5_w_glu, ffn_w_gu, ffn_w_down, final_norm_g, loss_target, m_norm_mix_g, m_norm_ffn_g, m_gla_w_in, m_gla_w_a2, m_gla_b_a, m_gla_norm_g, m_gla_w_out, m_ssd_w_in, m_ssd_conv_w, m_ssd_conv_b, m_ssd_dt_bias, m_ssd_a_log, m_ssd_d, m_ssd_norm_g, m_ssd_w_out, m_s5_log_dt, m_s5_a_re, m_s5_a_im, m_s5_b_re, m_s5_b_im, m_s5_c_re, m_s5_c_im, m_s5_d, m_s5_w_glu, m_ffn_w_gu, m_ffn_w_down, m_final_norm_g, v_norm_mix_g, v_norm_ffn_g, v_gla_w_in, v_gla_w_a2, v_gla_b_a, v_gla_norm_g, v_gla_w_out, v_ssd_w_in, v_ssd_conv_w, v_ssd_conv_b, v_ssd_dt_bias, v_ssd_a_log, v_ssd_d, v_ssd_norm_g, v_ssd_w_out, v_s5_log_dt, v_s5_a_re, v_s5_a_im, v_s5_b_re, v_s5_b_im, v_s5_c_re, v_s5_c_im, v_s5_d, v_s5_w_glu, v_ffn_w_gu, v_ffn_w_down, v_final_norm_g):
    args = locals()
    local = {n: args[n] for n in _WEIGHTS}
    moment_m = {n: args["m_" + n] for n in _WEIGHTS}
    moment_v = {n: args["v_" + n] for n in _WEIGHTS}

    full = {n: local[n] for n in _REPLICATED}
    full.update({n: _gather_matrix(local[n], n, axis) for n, axis in _BIG})
    full.update(_gather_small(local))

    loss, d_x, grads = _forward_backward(x[0], loss_target[0], full)
    loss = lax.psum(loss, ("x", "y", "c"))
    kinds = ("grad", "delta", "new_m", "new_v")
    out = {}

    for n, axis in _BIG:
        received = []
        for layer, g in enumerate(grads[n]):
            tag = f"{n}_{layer}"
            blocks = _split_cols(g, local[n].shape[2], "split_" + tag) if axis == 2 else g.reshape((N_DEV,) + local[n].shape[1:])
            received.append(_exchange(blocks, "exchange_" + tag))
        results = _adamw(jnp.stack(received, axis=1), local[n], moment_m[n], moment_v[n], "adamw_" + n)
        out.update({f"{kind}_{n}": a for kind, a in zip(kinds, results)})

    small = [n for n, _ in _SMALL] + _REPLICATED
    stacked = lambda n: grads[n] if n == "final_norm_g" else jnp.stack(grads[n])
    parts = _all_gather(_pack([stacked(n) for n in small]), "gather_small_gradients")
    summed = _unpack(_sum_parts(parts, "sum_small_gradients"), [stacked(n).shape for n in small])
    position = _index(_mesh_position())
    mine = [_own_shard(g, axis, position) for g, (_, axis) in zip(summed, _SMALL)] + summed[len(_SMALL):]
    shapes = [local[n].shape for n in small]
    pk = lambda arrays: _pack(arrays)[None]
    results = _adamw(pk(mine)[None], pk([local[n] for n in small]), pk([moment_m[n] for n in small]),
                     pk([moment_v[n] for n in small]), "adamw_small")
    for kind, flat in zip(kinds, results):
        out.update({f"{kind}_{n}": a for n, a in zip(small, _unpack(flat[0], shapes))})

    return (loss, d_x[None], *[out[f"{kind}_{n}"] for kind in ("grad", "delta", "new_m", "new_v") for n in _WEIGHTS])
```

```python
import functools
import math

import jax
import jax.numpy as jnp
import numpy as np
from jax import lax
from jax.experimental import pallas as pl
from jax.experimental.pallas import tpu as pltpu

F32 = jnp.float32
BF16 = jnp.bfloat16
_MXU_DTYPE = jnp.bfloat16

N_DEV = 8
D_MODEL = 1024
DEPTH = 4
CHUNK = 64
STEP_CHUNKS = 1
STEP = CHUNK * STEP_CHUNKS
EPS = 1e-6
GLA_HEADS, GLA_DK, GLA_DV, GLA_RANK, GLA_TAU = 4, 128, 256, 16, 16.0
GLA_QK = GLA_HEADS * GLA_DK
GLA_VD = GLA_HEADS * GLA_DV
LANES = 128
GLA_IN = 2 * GLA_QK + 2 * GLA_VD + GLA_RANK
GLA_PROJ = 2 * GLA_QK + 2 * GLA_VD + LANES
SSD_DINNER, SSD_HEADDIM, SSD_HEADS, SSD_GROUPS, SSD_HPG, SSD_DSTATE, SSD_CONV = 2048, 64, 32, 8, 4, 128, 4
SSD_GN = SSD_GROUPS * SSD_DSTATE
SSD_GW = SSD_HPG * SSD_HEADDIM
SSD_XBC = SSD_DINNER + 2 * SSD_GN
SSD_IN = SSD_DINNER + SSD_XBC + SSD_HEADS
SSD_PROJ = SSD_DINNER + SSD_XBC + LANES
S5_GROUP, S5_GROUPS, S5_STATE = 16, 64, 64
S5_CHUNK = 32
FFN_HIDDEN = 2816
ADAM_LR, ADAM_B1, ADAM_B2, ADAM_EPS, ADAM_WD, ADAM_STEP = 0.001, 0.9, 0.999, 1e-08, 0.01, 10
VMEM_LIMIT = 48 * 1024 * 1024
FLAT_COLS = 1024
FLAT_ROWS_ALIGN = 64


def _tile(n, cap, unit):
    if n <= cap:
        return n
    best = None
    for t in range(unit, cap + 1, unit):
        if n % t == 0:
            best = t
    assert best is not None, (n, cap, unit)
    return best


def _divisors(n, unit):
    return sorted({t for t in range(unit, n + 1, unit) if n % t == 0} | {n})


_MXU_FLOPS, _HBM_BYTES, _ACC_BYTES, _STEP_SECONDS = 1.1e15, 3e12, 1.1e13, 3.5e-7
_MXU_ROWS = 256
_TILE_VMEM_BUDGET = 36 * 1024 * 1024


def _pick_tiles(m, n, k, a_bytes, b_bytes, o_bytes, m_unit):
    best = None
    for tm in _divisors(m, m_unit):
        for tn in _divisors(n, LANES):
            for tk in _divisors(k, LANES):
                nk = k // tk
                vmem = 2 * tm * tk * a_bytes + 2 * tk * tn * b_bytes + 2 * tm * tn * o_bytes + (nk > 1) * tm * tn * 4
                if vmem > _TILE_VMEM_BUDGET or tm > 2048 or tn > 2048:
                    continue
                a_reads = n // tn if nk > 1 else 1
                b_reads = 1 if (nk == 1 and n == tn) else m // tm
                traffic = m * k * a_bytes * a_reads + k * n * b_bytes * b_reads + m * n * o_bytes
                mxu = 2.0 * m * n * k / _MXU_FLOPS * (1.0 + _MXU_ROWS / tm)
                cost = (max(mxu, traffic / _HBM_BYTES) + (nk > 1) * nk * m * n * 8 / _ACC_BYTES
                        + (m // tm) * (n // tn) * nk * _STEP_SECONDS)
                if best is None or cost < best[0]:
                    best = (cost, tm, tn, tk)
    assert best is not None, (m, n, k)
    return best[1:]


def _matmul(a, b, *, ta=False, tb=False, out_dtype=F32, name, cargo=()):
    batched = a.ndim == 3
    if ta:
        k_dim, m_dim = a.shape[-2:]
    else:
        m_dim, k_dim = a.shape[-2:]
    if tb:
        n_dim, kb = b.shape[-2:]
    else:
        kb, n_dim = b.shape[-2:]
    assert kb == k_dim, (a.shape, b.shape, ta, tb)
    tm, tn, tk = _pick_tiles(m_dim, n_dim, k_dim, a.dtype.itemsize, b.dtype.itemsize, jnp.dtype(out_dtype).itemsize,
                             LANES if ta else 16)
    nk = k_dim // tk
    ca, cb = (0 if ta else 1), (1 if tb else 0)
    grid = (m_dim // tm, n_dim // tn, nk)
    if batched:
        grid = (a.shape[0],) + grid
    n_cargo = len(cargo)

    def body(*refs):
        a_ref, b_ref = refs[:2]
        cargo_in = refs[2:2 + n_cargo]
        o_ref = refs[2 + n_cargo]
        cargo_out = refs[3 + n_cargo:3 + 2 * n_cargo]
        scratch = refs[3 + 2 * n_cargo:]
        acc = scratch[:1] if nk > 1 else ()
        sems = scratch[len(acc):]
        ids = [pl.program_id(d) for d in range(len(grid))]
        first = functools.reduce(jnp.logical_and, [i == 0 for i in ids])
        last = functools.reduce(jnp.logical_and, [i == g - 1 for i, g in zip(ids, grid)])
        moves = lambda: [_moves(kind, x_ref, y_ref, *sems[3 * c:3 * c + 3])
                         for c, ((kind, _), x_ref, y_ref) in enumerate(zip(cargo, cargo_in, cargo_out))]
        if n_cargo:
            @pl.when(first)
            def _():
                for mv in moves():
                    _start(mv)

        part = lax.dot_general(a_ref[...].astype(_MXU_DTYPE), b_ref[...].astype(_MXU_DTYPE),
                               (((ca,), (cb,)), ((), ())), preferred_element_type=F32)
        if nk == 1:
            o_ref[...] = part.astype(o_ref.dtype)
        else:
            acc_ref, = acc
            k = ids[-1]

            @pl.when(k == 0)
            def _():
                acc_ref[...] = part

            @pl.when(k > 0)
            def _():
                acc_ref[...] += part

            @pl.when(k == nk - 1)
            def _():
                o_ref[...] = acc_ref[...].astype(o_ref.dtype)

        if n_cargo:
            @pl.when(last)
            def _():
                for mv in moves():
                    _finish(mv)

    lead = (None,) if batched else ()

    def spec(shape, fn):
        if batched:
            return pl.BlockSpec(lead + shape, lambda g, i, j, k: (g,) + fn(i, j, k))
        return pl.BlockSpec(shape, fn)

    a_spec = spec((tk, tm), lambda i, j, k: (k, i)) if ta else spec((tm, tk), lambda i, j, k: (i, k))
    b_spec = spec((tn, tk), lambda i, j, k: (j, k)) if tb else spec((tk, tn), lambda i, j, k: (k, j))
    o_spec = spec((tm, tn), lambda i, j, k: (i, j))
    sem = ("parallel",) * (len(grid) - 1) + ("arbitrary",) if not n_cargo else ("arbitrary",) * len(grid)
    out_shape = ((a.shape[0],) if batched else ()) + (m_dim, n_dim)
    scratch = [pltpu.VMEM((tm, tn), F32)] if nk > 1 else []
    for _ in cargo:
        scratch += [pltpu.SemaphoreType.DMA((N_DEV - 1,)), pltpu.SemaphoreType.DMA((N_DEV - 1,)), pltpu.SemaphoreType.DMA]
    results = pl.pallas_call(
        body, name=name, grid=grid, in_specs=[a_spec, b_spec] + [_ANY] * n_cargo, out_specs=[o_spec] + [_ANY] * n_cargo,
        out_shape=[jax.ShapeDtypeStruct(out_shape, out_dtype)] + [_moved_shape(kind, x) for kind, x in cargo],
        scratch_shapes=scratch,
        compiler_params=pltpu.CompilerParams(dimension_semantics=sem, vmem_limit_bytes=VMEM_LIMIT),
    )(a, b, *[x for _, x in cargo])
    return (results[0], list(results[1:])) if n_cargo else results[0]


def _dot(a, b, ca=1, cb=0, exact=False):
    if exact:
        return lax.dot_general(a, b, (((ca,), (cb,)), ((), ())), precision=lax.Precision.HIGHEST,
                               preferred_element_type=F32)
    return lax.dot_general(a.astype(_MXU_DTYPE), b.astype(_MXU_DTYPE), (((ca,), (cb,)), ((), ())),
                           preferred_element_type=F32)


def _tri(n):
    return lax.broadcasted_iota(jnp.int32, (n, n), 0) >= lax.broadcasted_iota(jnp.int32, (n, n), 1)


def _log_sigmoid(x):
    return jnp.minimum(x, 0.0) - jnp.log(1.0 + jnp.exp(-jnp.abs(x)))


def _softplus(x):
    return jnp.maximum(x, 0.0) + jnp.log(1.0 + jnp.exp(-jnp.abs(x)))


def _silu(x):
    return x / (1.0 + jnp.exp(-x))


def _full_spec(shape):
    return pl.BlockSpec(shape, lambda c: (0,) * len(shape))


_SEQ_PARAMS = pltpu.CompilerParams(dimension_semantics=("arbitrary",), vmem_limit_bytes=VMEM_LIMIT)


def _gla_chunk(proj, st, w_a2, b_a, norm_g):
    t = proj.shape[0]
    q = proj[:, 0:GLA_QK] * (GLA_DK ** -0.5)
    k = proj[:, GLA_QK:2 * GLA_QK]
    v = proj[:, 2 * GLA_QK:2 * GLA_QK + GLA_VD]
    r = proj[:, 2 * GLA_QK + GLA_VD:2 * GLA_QK + 2 * GLA_VD]
    a_low = proj[:, 2 * GLA_QK + 2 * GLA_VD:]
    log_a = _log_sigmoid(_dot(a_low, w_a2) + b_a) * (1.0 / GLA_TAU)
    past = _tri(t)
    lc = _dot(past.astype(F32), log_a, exact=True)
    lend = lc[t - 1:t, :]
    e_pos = jnp.exp(lc)
    e_neg = jnp.exp(-lc)
    q_fwd, k_fwd, q_bwd, k_bwd = q * e_pos, k * e_neg, q * e_neg, k * e_pos
    kd = k * jnp.exp(lend - lc)
    g = jnp.exp(lend)
    outs, new_st = [], []
    for h in range(GLA_HEADS):
        sk = slice(h * GLA_DK, (h + 1) * GLA_DK)
        sv = slice(h * GLA_DV, (h + 1) * GLA_DV)
        s_past = _dot(q_fwd[:, sk], k_fwd[:, sk], 1, 1)
        s_future = _dot(q_bwd[:, sk], k_bwd[:, sk], 1, 1)
        scores = jnp.where(past, s_past, s_future)
        o = _dot(scores, v[:, sv]) + _dot(q_fwd[:, sk], st[h], 1, 1)
        new_st.append(st[h] * g[:, sk] + _dot(v[:, sv], kd[:, sk], 0, 0))
        o = o * lax.rsqrt(jnp.mean(o * o, axis=-1, keepdims=True) + EPS) * norm_g[:, sv]
        outs.append(o)
    return jnp.concatenate(outs, axis=1) * _silu(r), tuple(new_st)


_GLA_STATE = (GLA_HEADS, GLA_DV, GLA_DK)


def _gla_step(proj, st, w_a2, b_a, norm_g):
    outs = []
    for s in range(STEP_CHUNKS):
        out, st = _gla_chunk(proj[s * CHUNK:(s + 1) * CHUNK], st, w_a2, b_a, norm_g)
        outs.append(out)
    return jnp.concatenate(outs, axis=0), st


def _gla_core_fwd(proj, w_a2, b_a, norm_g):
    seq = proj.shape[0]
    nc = seq // STEP

    def body(proj_ref, wa_ref, ba_ref, ng_ref, o_ref, sprev_ref, st_ref):
        @pl.when(pl.program_id(0) == 0)
        def _():
            st_ref[...] = jnp.zeros_like(st_ref)

        st = tuple(st_ref[h] for h in range(GLA_HEADS))
        for h in range(GLA_HEADS):
            sprev_ref[0, h] = st[h]
        out, new_st = _gla_step(proj_ref[...], st, wa_ref[...], ba_ref[...], ng_ref[...])
        o_ref[...] = out
        for h in range(GLA_HEADS):
            st_ref[h] = new_st[h]

    return pl.pallas_call(
        body, name="gla_core_fwd", grid=(nc,),
        in_specs=[pl.BlockSpec((STEP,GLA_PROJ), lambda c: (c, 0)), _full_spec(w_a2.shape), _full_spec(b_a.shape),
                  _full_spec(norm_g.shape)],
        out_specs=[pl.BlockSpec((STEP,GLA_VD), lambda c: (c, 0)), pl.BlockSpec((1,) + _GLA_STATE, lambda c: (c, 0, 0, 0))],
        out_shape=[jax.ShapeDtypeStruct((seq, GLA_VD), F32), jax.ShapeDtypeStruct((nc,) + _GLA_STATE, F32)],
        scratch_shapes=[pltpu.VMEM(_GLA_STATE, F32)],
        compiler_params=_SEQ_PARAMS,
    )(proj, w_a2, b_a, norm_g)


def _gla_core_bwd(proj, sprev, d_out, w_a2, b_a, norm_g):
    seq = proj.shape[0]
    nc = seq // STEP

    def body(proj_ref, sprev_ref, do_ref, wa_ref, ba_ref, ng_ref, dproj_ref, dwa_ref, dba_ref, dng_ref, dst_ref):
        @pl.when(pl.program_id(0) == 0)
        def _():
            dst_ref[...] = jnp.zeros_like(dst_ref)
            dwa_ref[...] = jnp.zeros_like(dwa_ref)
            dba_ref[...] = jnp.zeros_like(dba_ref)
            dng_ref[...] = jnp.zeros_like(dng_ref)

        st = tuple(sprev_ref[0, h] for h in range(GLA_HEADS))
        _, vjp = jax.vjp(_gla_step, proj_ref[...], st, wa_ref[...], ba_ref[...], ng_ref[...])
        d_next = tuple(dst_ref[h] for h in range(GLA_HEADS))
        d_proj, d_st, d_wa, d_ba, d_ng = vjp((do_ref[...], d_next))
        dproj_ref[...] = d_proj
        for h in range(GLA_HEADS):
            dst_ref[h] = d_st[h]
        dwa_ref[...] += d_wa
        dba_ref[...] += d_ba
        dng_ref[...] += d_ng

    rev = lambda c: (nc - 1 - c, 0)
    return pl.pallas_call(
        body, name="gla_core_bwd", grid=(nc,),
        in_specs=[pl.BlockSpec((STEP,GLA_PROJ), rev), pl.BlockSpec((1,) + _GLA_STATE, lambda c: (nc - 1 - c, 0, 0, 0)),
                  pl.BlockSpec((STEP,GLA_VD), rev), _full_spec(w_a2.shape), _full_spec(b_a.shape), _full_spec(norm_g.shape)],
        out_specs=[pl.BlockSpec((STEP,GLA_PROJ), rev), _full_spec(w_a2.shape), _full_spec(b_a.shape), _full_spec(norm_g.shape)],
        out_shape=[jax.ShapeDtypeStruct((seq, GLA_PROJ), F32), jax.ShapeDtypeStruct(w_a2.shape, F32),
                   jax.ShapeDtypeStruct(b_a.shape, F32), jax.ShapeDtypeStruct(norm_g.shape, F32)],
        scratch_shapes=[pltpu.VMEM(_GLA_STATE, F32)],
        compiler_params=_SEQ_PARAMS,
    )(proj, sprev, d_out, w_a2, b_a, norm_g)


def _ssd_chunk(z, xbc, dt_raw, hs, dt_bias, a_log, d_skip, norm_g):
    t = z.shape[0]
    xs = xbc[:, :SSD_DINNER]
    bm = xbc[:, SSD_DINNER:SSD_DINNER + SSD_GN]
    cm = xbc[:, SSD_DINNER + SSD_GN:]
    dt = _softplus(dt_raw + dt_bias)
    da = dt * (-jnp.exp(a_log))
    tri = _tri(t).astype(F32)
    eye = (lax.broadcasted_iota(jnp.int32, (t, t), 0) == lax.broadcasted_iota(jnp.int32, (t, t), 1)).astype(F32)
    cum = _dot(tri, da, exact=True)
    cum_t = _dot(da, tri, 0, 1, exact=True)
    dt_t = _dot(dt, eye, 0, 0, exact=True)
    cum_end = cum[t - 1:t, :]
    w_state = dt * jnp.exp(cum_end - cum)
    e_cum = jnp.exp(cum)
    g_end = jnp.exp(cum_end)
    head_of = lambda axis: lax.shift_right_logical(lax.broadcasted_iota(jnp.int32, (SSD_GW, SSD_GW), axis),
                                                   jnp.int32(SSD_HEADDIM.bit_length() - 1))
    same_head = head_of(0) == head_of(1)
    ys, new_hs = [], []
    for g in range(SSD_GROUPS):
        heads = range(g * SSD_HPG, (g + 1) * SSD_HPG)
        cols = slice(g * SSD_GW, (g + 1) * SSD_GW)

        def spread(a):
            return jnp.concatenate([jnp.broadcast_to(a[:, h:h + 1], (a.shape[0], SSD_HEADDIM)) for h in heads], axis=1)

        def row(a_t):
            return jnp.concatenate([a_t[h:h + 1, :] for h in heads], axis=1)

        bm_g = bm[:, g * SSD_DSTATE:(g + 1) * SSD_DSTATE]
        cm_g = cm[:, g * SSD_DSTATE:(g + 1) * SSD_DSTATE]
        xs_g = xs[:, cols]
        cb = _dot(cm_g, jnp.concatenate([bm_g] * SSD_HPG, axis=0), 1, 1)
        mix = cb * jnp.exp(-jnp.abs(spread(cum) - row(cum_t))) * row(dt_t)
        x_diag = jnp.where(same_head, jnp.concatenate([xs_g] * SSD_HPG, axis=0), 0.0)
        y = _dot(mix, x_diag)
        y = y + _dot(cm_g, hs[g], 1, 1) * spread(e_cum)
        y = y + spread(d_skip) * xs_g
        states = _dot(xs_g * spread(w_state), bm_g, 0, 0)
        decayed = jnp.concatenate([g_end[:, h:h + 1] * hs[g][j * SSD_HEADDIM:(j + 1) * SSD_HEADDIM, :]
                                   for j, h in enumerate(heads)], axis=0)
        new_hs.append(decayed + states)
        yg = y * _silu(z[:, cols])
        ys.append(yg * lax.rsqrt(jnp.mean(yg * yg, axis=-1, keepdims=True) + EPS) * norm_g[:, cols])
    return jnp.concatenate(ys, axis=1), tuple(new_hs)


_SSD_STATE = (SSD_GROUPS, SSD_GW, SSD_DSTATE)


def _ssd_step(z, xbc, dt_raw, hs, dt_bias, a_log, d_skip, norm_g):
    outs = []
    for s in range(STEP_CHUNKS):
        rows = slice(s * CHUNK, (s + 1) * CHUNK)
        out, hs = _ssd_chunk(z[rows], xbc[rows], dt_raw[rows], hs, dt_bias, a_log, d_skip, norm_g)
        outs.append(out)
    return jnp.concatenate(outs, axis=0), hs
_SSD_DT_BLOCK = (SSD_DINNER + SSD_XBC) // LANES


def _ssd_core_fwd(proj, xbc, dt_bias, a_log, d_skip, norm_g):
    seq = proj.shape[0]
    nc = seq // STEP

    def body(z_ref, xbc_ref, dt_ref, db_ref, al_ref, ds_ref, ng_ref, o_ref, hprev_ref, hs_ref):
        @pl.when(pl.program_id(0) == 0)
        def _():
            hs_ref[...] = jnp.zeros_like(hs_ref)

        hs = tuple(hs_ref[g] for g in range(SSD_GROUPS))
        for g in range(SSD_GROUPS):
            hprev_ref[0, g] = hs[g]
        out, new_hs = _ssd_step(z_ref[...], xbc_ref[...], dt_ref[...], hs, db_ref[...], al_ref[...], ds_ref[...], ng_ref[...])
        o_ref[...] = out
        for g in range(SSD_GROUPS):
            hs_ref[g] = new_hs[g]

    return pl.pallas_call(
        body, name="ssd_core_fwd", grid=(nc,),
        in_specs=[pl.BlockSpec((STEP,SSD_DINNER), lambda c: (c, 0)), pl.BlockSpec((STEP,SSD_XBC), lambda c: (c, 0)),
                  pl.BlockSpec((STEP,LANES), lambda c: (c, _SSD_DT_BLOCK)),
                  _full_spec(dt_bias.shape), _full_spec(a_log.shape), _full_spec(d_skip.shape), _full_spec(norm_g.shape)],
        out_specs=[pl.BlockSpec((STEP,SSD_DINNER), lambda c: (c, 0)), pl.BlockSpec((1,) + _SSD_STATE, lambda c: (c, 0, 0, 0))],
        out_shape=[jax.ShapeDtypeStruct((seq, SSD_DINNER), F32), jax.ShapeDtypeStruct((nc,) + _SSD_STATE, F32)],
        scratch_shapes=[pltpu.VMEM(_SSD_STATE, F32)],
        compiler_params=_SEQ_PARAMS,
    )(proj, xbc, proj, dt_bias, a_log, d_skip, norm_g)


def _ssd_core_bwd(proj, xbc, hprev, d_out, dt_bias, a_log, d_skip, norm_g):
    seq = proj.shape[0]
    nc = seq // STEP

    def body(z_ref, xbc_ref, dt_ref, hprev_ref, do_ref, db_ref, al_ref, ds_ref, ng_ref,
             dz_ref, dxbc_ref, ddt_ref, ddb_ref, dal_ref, dds_ref, dng_ref, dhs_ref):
        @pl.when(pl.program_id(0) == 0)
        def _():
            dhs_ref[...] = jnp.zeros_like(dhs_ref)
            ddb_ref[...] = jnp.zeros_like(ddb_ref)
            dal_ref[...] = jnp.zeros_like(dal_ref)
            dds_ref[...] = jnp.zeros_like(dds_ref)
            dng_ref[...] = jnp.zeros_like(dng_ref)

        hs = tuple(hprev_ref[0, g] for g in range(SSD_GROUPS))
        _, vjp = jax.vjp(_ssd_step, z_ref[...], xbc_ref[...], dt_ref[...], hs, db_ref[...], al_ref[...], ds_ref[...], ng_ref[...])
        d_next = tuple(dhs_ref[g] for g in range(SSD_GROUPS))
        d_z, d_xbc, d_dt, d_hs, d_db, d_al, d_ds, d_ng = vjp((do_ref[...], d_next))
        dz_ref[...] = d_z
        dxbc_ref[...] = d_xbc
        ddt_ref[...] = d_dt
        for g in range(SSD_GROUPS):
            dhs_ref[g] = d_hs[g]
        ddb_ref[...] += d_db
        dal_ref[...] += d_al
        dds_ref[...] += d_ds
        dng_ref[...] += d_ng

    rev = lambda c: (nc - 1 - c, 0)
    vec = [_full_spec(dt_bias.shape), _full_spec(a_log.shape), _full_spec(d_skip.shape), _full_spec(norm_g.shape)]
    return pl.pallas_call(
        body, name="ssd_core_bwd", grid=(nc,),
        in_specs=[pl.BlockSpec((STEP,SSD_DINNER), rev), pl.BlockSpec((STEP,SSD_XBC), rev),
                  pl.BlockSpec((STEP,LANES), lambda c: (nc - 1 - c, _SSD_DT_BLOCK)),
                  pl.BlockSpec((1,) + _SSD_STATE, lambda c: (nc - 1 - c, 0, 0, 0)),
                  pl.BlockSpec((STEP,SSD_DINNER), rev)] + vec,
        out_specs=[pl.BlockSpec((STEP,SSD_DINNER), rev), pl.BlockSpec((STEP,SSD_XBC), rev),
                   pl.BlockSpec((STEP,LANES), rev)] + vec,
        out_shape=[jax.ShapeDtypeStruct((seq, SSD_DINNER), F32), jax.ShapeDtypeStruct((seq, SSD_XBC), F32),
                   jax.ShapeDtypeStruct((seq, LANES), F32),
                   jax.ShapeDtypeStruct(dt_bias.shape, F32), jax.ShapeDtypeStruct(a_log.shape, F32),
                   jax.ShapeDtypeStruct(d_skip.shape, F32), jax.ShapeDtypeStruct(norm_g.shape, F32)],
        scratch_shapes=[pltpu.VMEM(_SSD_STATE, F32)],
        compiler_params=_SEQ_PARAMS,
    )(proj, xbc, proj, hprev, d_out, dt_bias, a_log, d_skip, norm_g)


def _s5_boundary_scan(z_re, z_im, lam_re, lam_im):
    n_chunks = z_re.shape[0]

    def body(zr_ref, zi_ref, lr_ref, li_ref, xr_ref, xi_ref):
        lr, li = lr_ref[...], li_ref[...]

        def step(n, carry):
            xr, xi = carry
            xr_ref[n] = xr
            xi_ref[n] = xi
            return lr * xr - li * xi + zr_ref[n], lr * xi + li * xr + zi_ref[n]

        zero = jnp.zeros(lr.shape, F32)
        lax.fori_loop(0, n_chunks, step, (zero, zero))

    shape = jax.ShapeDtypeStruct(z_re.shape, F32)
    return pl.pallas_call(body, name="s5_boundary_scan", out_shape=[shape, shape],
                          compiler_params=pltpu.CompilerParams(vmem_limit_bytes=VMEM_LIMIT))(z_re, z_im, lam_re, lam_im)


_FLIPS = [(kx, ky, kc) for kx in (0, 1) for ky in (0, 1) for kc in (0, 1)][1:]


def _mesh_position():
    return lax.axis_index("x"), lax.axis_index("y"), lax.axis_index("c")


def _peer(pos, flip):
    return tuple((1 - p) if f else p for p, f in zip(pos, flip))


def _index(pos):
    return 4 * pos[0] + 2 * pos[1] + pos[2]


_ANY = pl.BlockSpec(memory_space=pl.ANY)


def _moved_shape(kind, x):
    return jax.ShapeDtypeStruct(((N_DEV,) + x.shape) if kind == "gather" else x.shape, x.dtype)


def _moves(kind, x_ref, out_ref, send_sems, recv_sems, local_sem):
    me = _mesh_position()
    source = (lambda pos: x_ref) if kind == "gather" else (lambda pos: x_ref.at[_index(pos)])
    local = pltpu.make_async_copy(source(me), out_ref.at[_index(me)], local_sem)
    outgoing, incoming = [], []
    for k, flip in enumerate(_FLIPS):
        peer = _peer(me, flip)
        copy = lambda slot: pltpu.make_async_remote_copy(
            src_ref=source(peer), dst_ref=out_ref.at[_index(slot)], send_sem=send_sems.at[k], recv_sem=recv_sems.at[k],
            device_id=peer, device_id_type=pl.DeviceIdType.MESH)
        outgoing.append(copy(me))
        incoming.append(copy(peer))
    return local, outgoing, incoming


def _start(moves):
    local, outgoing, _ = moves
    local.start()
    for cp in outgoing:
        cp.start()


def _finish(moves):
    local, outgoing, incoming = moves
    for cp in incoming:
        cp.wait_recv()
    for cp in outgoing:
        cp.wait_send()
    local.wait()


def _collective(kind, x, name):
    def body(x_ref, out_ref, send_sems, recv_sems, local_sem):
        moves = _moves(kind, x_ref, out_ref, send_sems, recv_sems, local_sem)
        _start(moves)
        _finish(moves)

    return pl.pallas_call(
        body, name=name, in_specs=[_ANY], out_specs=_ANY, out_shape=_moved_shape(kind, x),
        scratch_shapes=[pltpu.SemaphoreType.DMA((N_DEV - 1,)), pltpu.SemaphoreType.DMA((N_DEV - 1,)), pltpu.SemaphoreType.DMA],
        compiler_params=pltpu.CompilerParams(has_side_effects=True),
    )(x)


def _adamw(parts, w, m, v, name):
    n_parts = parts.shape[0]
    layers, rows, cols = w.shape
    tr = _tile(rows, 256, 8)

    def body(p_ref, w_ref, m_ref, v_ref, g_ref, d_ref, mo_ref, vo_ref):
        g = p_ref[0].astype(F32)
        for s in range(1, n_parts):
            g = g + p_ref[s].astype(F32)
        m_new = ADAM_B1 * m_ref[...] + (1.0 - ADAM_B1) * g
        v_new = ADAM_B2 * v_ref[...] + (1.0 - ADAM_B2) * (g * g)
        m_hat = m_new / (1.0 - ADAM_B1 ** ADAM_STEP)
        v_hat = v_new / (1.0 - ADAM_B2 ** ADAM_STEP)
        g_ref[...] = g
        d_ref[...] = -ADAM_LR * (m_hat / (jnp.sqrt(v_hat) + ADAM_EPS) + ADAM_WD * w_ref[...])
        mo_ref[...] = m_new
        vo_ref[...] = v_new

    blk = pl.BlockSpec((None, tr, cols), lambda l, i: (l, i, 0))
    shape = jax.ShapeDtypeStruct(w.shape, F32)
    return pl.pallas_call(
        body, name=name, grid=(layers, rows // tr),
        in_specs=[pl.BlockSpec((n_parts, None, tr, cols), lambda l, i: (0, l, i, 0)), blk, blk, blk],
        out_specs=[blk, blk, blk, blk], out_shape=[shape, shape, shape, shape],
        compiler_params=pltpu.CompilerParams(dimension_semantics=("parallel", "parallel"), vmem_limit_bytes=VMEM_LIMIT),
    )(parts, w, m, v)


def _sum_parts(parts, name):
    _, rows, cols = parts.shape
    tr = _tile(rows, 256, 8)

    def body(p_ref, o_ref):
        total = p_ref[0]
        for s in range(1, N_DEV):
            total = total + p_ref[s]
        o_ref[...] = total

    return pl.pallas_call(
        body, name=name, grid=(rows // tr,),
        in_specs=[pl.BlockSpec((N_DEV, tr, cols), lambda i: (0, i, 0))], out_specs=pl.BlockSpec((tr, cols), lambda i: (i, 0)),
        out_shape=jax.ShapeDtypeStruct((rows, cols), parts.dtype),
        compiler_params=pltpu.CompilerParams(dimension_semantics=("parallel",), vmem_limit_bytes=VMEM_LIMIT),
    )(parts)


def _join_cols(blocks, n_out, name):
    _, layers, rows, n = blocks.shape
    tr = _tile(rows, 256, 16)

    def body(x_ref, o_ref):
        for d in range(N_DEV):
            o_ref[:, d * n:(d + 1) * n] = x_ref[d]
        if n_out > N_DEV * n:
            o_ref[:, N_DEV * n:] = jnp.zeros((tr, n_out - N_DEV * n), o_ref.dtype)

    return pl.pallas_call(
        body, name=name, grid=(layers, rows // tr),
        in_specs=[pl.BlockSpec((N_DEV, None, tr, n), lambda l, i: (0, l, i, 0))],
        out_specs=pl.BlockSpec((None, tr, n_out), lambda l, i: (l, i, 0)),
        out_shape=jax.ShapeDtypeStruct((layers, rows, n_out), blocks.dtype),
        compiler_params=pltpu.CompilerParams(dimension_semantics=("parallel", "parallel"), vmem_limit_bytes=VMEM_LIMIT),
    )(blocks)


def _split_cols(full, n, name):
    rows = full.shape[0]
    tr = _tile(rows, 256, 16)

    def body(x_ref, o_ref):
        for d in range(N_DEV):
            o_ref[d] = x_ref[:, d * n:(d + 1) * n]

    return pl.pallas_call(
        body, name=name, grid=(rows // tr,),
        in_specs=[pl.BlockSpec((tr, full.shape[1]), lambda i: (i, 0))],
        out_specs=pl.BlockSpec((N_DEV, tr, n), lambda i: (0, i, 0)),
        out_shape=jax.ShapeDtypeStruct((N_DEV, rows, n), full.dtype),
        compiler_params=pltpu.CompilerParams(dimension_semantics=("parallel",), vmem_limit_bytes=VMEM_LIMIT),
    )(full)


def _pack(arrays):
    flat = jnp.concatenate([a.reshape(-1) for a in arrays])
    unit = FLAT_COLS * FLAT_ROWS_ALIGN
    padded = -(-flat.shape[0] // unit) * unit
    return jnp.pad(flat, (0, padded - flat.shape[0])).reshape(-1, FLAT_COLS)


def _unpack(flat, shapes, lead=()):
    flat = flat.reshape(lead + (-1,))
    out, off = [], 0
    for shape in shapes:
        n = math.prod(shape)
        out.append(flat[..., off:off + n].reshape(lead + tuple(shape)))
        off += n
    return out


def _join(blocks, axis):
    moved = jnp.moveaxis(blocks, 0, axis)
    shape = list(moved.shape)
    shape[axis:axis + 2] = [shape[axis] * shape[axis + 1]]
    return moved.reshape(shape)


def _own_shard(full, axis, position):
    n = full.shape[axis] // N_DEV
    return lax.dynamic_slice_in_dim(full, position * n, n, axis)


def _rmsnorm(x, g):
    return x * lax.rsqrt(jnp.mean(x * x, axis=-1, keepdims=True) + EPS) * g


def _swiglu(gu):
    gu = gu.astype(F32)
    return (jax.nn.silu(gu[:, :FFN_HIDDEN]) * gu[:, FFN_HIDDEN:]).astype(BF16)


def _ssd_conv(xbc, conv_w, conv_b):
    seq = xbc.shape[0]
    padded = jnp.pad(xbc, ((SSD_CONV - 1, 0), (0, 0)))
    out = conv_b
    for k in range(SSD_CONV):
        out = out + padded[k:k + seq] * conv_w[k]
    return jax.nn.silu(out)


def _loss_head(h, g, target):
    err = _rmsnorm(h, g) - target
    return 0.5 * jnp.sum(jnp.mean(err * err, axis=-1))


def _s5_operators(log_dt, a_re, a_im, b_re, b_im, c_re, c_im):
    t = S5_CHUNK
    hi = lax.Precision.HIGHEST
    step = jnp.exp(log_dt)[:, None]
    mag = jnp.exp(step * a_re)
    abar_re = mag * jnp.cos(step * a_im)
    abar_im = mag * jnp.sin(step * a_im)
    den = a_re * a_re + a_im * a_im
    f_re = ((abar_re - 1.0) * a_re + abar_im * a_im) / den
    f_im = (abar_im * a_re - (abar_re - 1.0) * a_im) / den
    bb_re = f_re[..., None] * b_re - f_im[..., None] * b_im
    bb_im = f_re[..., None] * b_im + f_im[..., None] * b_re
    j = jnp.arange(t + 1, dtype=F32)[:, None, None]
    pmag = jnp.exp(j * (step * a_re))
    pw_re = pmag * jnp.cos(j * (step * a_im))
    pw_im = pmag * jnp.sin(j * (step * a_im))
    cl_re = c_re[None] * pw_re[:t, :, None, :] - c_im[None] * pw_im[:t, :, None, :]
    cl_im = c_re[None] * pw_im[:t, :, None, :] + c_im[None] * pw_re[:t, :, None, :]
    kern = (jnp.einsum('jgcp,gpk->jgck', cl_re, bb_re, precision=hi)
            - jnp.einsum('jgcp,gpk->jgck', cl_im, bb_im, precision=hi))
    rp_re, rp_im = pw_re[:t][::-1], pw_im[:t][::-1]
    wz_re = rp_re[:, :, :, None] * bb_re[None] - rp_im[:, :, :, None] * bb_im[None]
    wz_im = rp_re[:, :, :, None] * bb_im[None] + rp_im[:, :, :, None] * bb_re[None]
    w_z = jnp.concatenate([wz_re, wz_im], axis=2).transpose(1, 0, 3, 2).reshape(S5_GROUPS, t * S5_GROUP, 2 * S5_STATE)
    cy_re = c_re[None] * pw_re[1:, :, None, :] - c_im[None] * pw_im[1:, :, None, :]
    cy_im = c_re[None] * pw_im[1:, :, None, :] + c_im[None] * pw_re[1:, :, None, :]
    w_y = jnp.concatenate([cy_re, -cy_im], axis=3).transpose(1, 3, 0, 2).reshape(S5_GROUPS, 2 * S5_STATE, t * S5_GROUP)
    return kern, w_z, w_y, pw_re[t], pw_im[t]


def _s5_lag_selector():
    t = S5_CHUNK
    lag = jnp.arange(t)[:, None] - jnp.arange(t)[None, :]
    return (lag[:, :, None] == jnp.arange(t)[None, None, :]).astype(F32).reshape(t * t, t)


def _s5_toeplitz(kern, tag):
    t = S5_CHUNK
    sel = _s5_lag_selector()
    flat = _matmul(sel, kern.reshape(t, -1), out_dtype=BF16, name=tag + "_toeplitz")
    toep = flat.reshape(t, t, S5_GROUPS, S5_GROUP, S5_GROUP).transpose(2, 1, 4, 0, 3)
    toep = toep.reshape(S5_GROUPS, t * S5_GROUP, t * S5_GROUP)

    def backward(d_toep):
        d_flat = d_toep.reshape(S5_GROUPS, t, S5_GROUP, t, S5_GROUP).transpose(3, 1, 0, 4, 2).reshape(t * t, -1)
        return _matmul(sel, d_flat, ta=True, name=tag + "_toeplitz_dw").reshape(kern.shape)

    return toep, backward


def _s5_gate(y, u, d_skip):
    return jax.nn.gelu(y + d_skip * u)


def _glu(vg):
    return vg[:, :D_MODEL] * jax.nn.sigmoid(vg[:, D_MODEL:])


_BIG = [("gla_w_in", 2), ("gla_w_out", 1), ("ssd_w_in", 2), ("ssd_w_out", 1), ("s5_w_glu", 2), ("ffn_w_gu", 2),
        ("ffn_w_down", 1)]
_PADDED_COLS = {"gla_w_in": GLA_PROJ, "ssd_w_in": SSD_PROJ}


class _Traffic:
    RIDERS = 2

    def __init__(self, shards, plan):
        self.shards, self.plan = shards, plan
        self.position = 0
        self.queue = []
        self.weights, self.received = {}, {}
        self.standalone = 0

    def _request(self, key):
        self.queue.append(("gather", self.shards[key], lambda blocks: self.weights.__setitem__(key, self._assemble(key, blocks))))

    @staticmethod
    def _assemble(key, blocks):
        name, layer = key
        if dict(_BIG)[name] == 1:
            return blocks.reshape((N_DEV * blocks.shape[1], blocks.shape[2]))
        n_out = _PADDED_COLS.get(name, N_DEV * blocks.shape[2])
        return _join_cols(blocks[:, None], n_out, f"join_{name}_{layer}")[0]

    def take(self, key):
        if self.position == 0:
            self._request(self.plan[0])
            self.flush()
        assert key == self.plan[self.position], (key, self.plan[self.position])
        self.position += 1
        if self.position < len(self.plan):
            self._request(self.plan[self.position])
        return self.weights[key]

    def matmul(self, a, b, **kw):
        riders, self.queue = self.queue[:self.RIDERS], self.queue[self.RIDERS:]
        if not riders:
            return _matmul(a, b, **kw)
        out, moved = _matmul(a, b, cargo=[(kind, x) for kind, x, _ in riders], **kw)
        for (_, _, deliver), y in zip(riders, moved):
            deliver(y)
        return out

    def send_gradient(self, key, dw):
        name, layer = key
        shard = self.shards[key]
        if dict(_BIG)[name] == 1:
            blocks = dw.reshape((N_DEV,) + shard.shape)
        else:
            blocks = _split_cols(dw, shard.shape[1], f"split_{name}_{layer}")
        self.queue.append(("exchange", blocks, lambda parts: self.received.__setitem__(key, parts)))

    def flush(self):
        for kind, x, deliver in self.queue:
            deliver(_collective(kind, x, f"{kind}_alone_{self.standalone}"))
            self.standalone += 1
        self.queue = []


def _linear(x, w, tag, out_dtype=F32, dx_dtype=F32):
    traffic, key = w
    weight = traffic.take(key)
    y = traffic.matmul(x, weight, out_dtype=out_dtype, name=tag + "_fwd")

    def backward(dy):
        dx = traffic.matmul(dy, weight, tb=True, out_dtype=dx_dtype, name=tag + "_dx")
        traffic.send_gradient(key, traffic.matmul(x, dy, ta=True, out_dtype=BF16, name=tag + "_dw"))
        return dx

    return y, backward


def _gla_mixer(hn, p, tag):
    w_a2 = jnp.pad(p["w_a2"], ((0, LANES - GLA_RANK), (0, 0)))
    b_a, norm_g = p["b_a"][None], p["norm_g"][None]
    proj, lin_in = _linear(hn, p["w_in"], tag + "_in")
    o, sprev = _gla_core_fwd(proj, w_a2, b_a, norm_g)
    y, lin_out = _linear(o, p["w_out"], tag + "_out")

    def backward(dy):
        d_proj, d_wa, d_ba, d_ng = _gla_core_bwd(proj, sprev, lin_out(dy), w_a2, b_a, norm_g)
        return lin_in(d_proj), dict(w_a2=d_wa[:GLA_RANK], b_a=d_ba[0], norm_g=d_ng[0])

    return y, backward


def _ssd_mixer(hn, p, tag):
    pad = lambda a: jnp.pad(a[None], ((0, 0), (0, LANES - SSD_HEADS)))
    dt_bias, a_log, d_skip, norm_g = pad(p["dt_bias"]), pad(p["a_log"]), pad(p["d"]), p["norm_g"][None]
    proj, lin_in = _linear(hn, p["w_in"], tag + "_in")
    xbc, conv_vjp = jax.vjp(_ssd_conv, proj[:, SSD_DINNER:SSD_DINNER + SSD_XBC], p["conv_w"], p["conv_b"])
    o, hprev = _ssd_core_fwd(proj, xbc, dt_bias, a_log, d_skip, norm_g)
    y, lin_out = _linear(o, p["w_out"], tag + "_out")

    def backward(dy):
        d_z, d_xbc, d_dt, d_db, d_al, d_ds, d_ng = _ssd_core_bwd(proj, xbc, hprev, lin_out(dy), dt_bias, a_log, d_skip, norm_g)
        d_pre, d_cw, d_cb = conv_vjp(d_xbc)
        d_hn = lin_in(jnp.concatenate([d_z, d_pre, d_dt], axis=1))
        return d_hn, dict(conv_w=d_cw, conv_b=d_cb, dt_bias=d_db[0, :SSD_HEADS], a_log=d_al[0, :SSD_HEADS],
                          d=d_ds[0, :SSD_HEADS], norm_g=d_ng[0])

    return y, backward


def _s5_mixer(hn, p, tag):
    seq = hn.shape[0]
    t, n_chunks = S5_CHUNK, hn.shape[0] // S5_CHUNK
    names = ("log_dt", "a_re", "a_im", "b_re", "b_im", "c_re", "c_im")
    (kern, w_z, w_y, lam_re, lam_im), ops_vjp = jax.vjp(_s5_operators, *[p[k] for k in names])
    toep, toep_bwd = _s5_toeplitz(kern, tag)
    to_groups = lambda a: a.reshape(n_chunks, t, S5_GROUPS, S5_GROUP).transpose(2, 0, 1, 3).reshape(S5_GROUPS, n_chunks, t * S5_GROUP)
    from_groups = lambda a: a.reshape(S5_GROUPS, n_chunks, t, S5_GROUP).transpose(1, 2, 0, 3).reshape(seq, D_MODEL)
    ug = to_groups(hn)
    z = _matmul(ug, w_z, name=tag + "_z")
    z_re, z_im = z[..., :S5_STATE].transpose(1, 0, 2), z[..., S5_STATE:].transpose(1, 0, 2)
    x_re, x_im = _s5_boundary_scan(z_re, z_im, lam_re, lam_im)
    xprev = jnp.concatenate([x_re, x_im], axis=2).transpose(1, 0, 2)
    yg = _matmul(ug, toep, name=tag + "_intra") + _matmul(xprev, w_y, name=tag + "_inter")
    act, gate_vjp = jax.vjp(_s5_gate, from_groups(yg), hn, p["d"])
    vg, lin_glu = _linear(act, p["w_glu"], tag + "_glu")
    out, glu_vjp = jax.vjp(_glu, vg)

    def backward(dy):
        d_vg, = glu_vjp(dy)
        d_act = lin_glu(d_vg)
        d_y, d_hn, d_d = gate_vjp(d_act)
        d_yg = to_groups(d_y)
        d_ug = _matmul(d_yg, toep, tb=True, name=tag + "_intra_dx")
        d_toep = _matmul(ug, d_yg, ta=True, out_dtype=BF16, name=tag + "_intra_dw")
        d_xprev = _matmul(d_yg, w_y, tb=True, name=tag + "_inter_dx").transpose(1, 0, 2)
        d_wy = _matmul(xprev, d_yg, ta=True, name=tag + "_inter_dw")
        dz_re, dz_im = _s5_boundary_scan(d_xprev[::-1, :, :S5_STATE], d_xprev[::-1, :, S5_STATE:], lam_re, -lam_im)
        dz_re, dz_im = dz_re[::-1], dz_im[::-1]
        d_lam_re = jnp.sum(x_re * dz_re + x_im * dz_im, axis=0)
        d_lam_im = jnp.sum(x_re * dz_im - x_im * dz_re, axis=0)
        d_z = jnp.concatenate([dz_re, dz_im], axis=2).transpose(1, 0, 2)
        d_ug = d_ug + _matmul(d_z, w_z, tb=True, name=tag + "_z_dx")
        d_wz = _matmul(ug, d_z, ta=True, name=tag + "_z_dw")
        grads = dict(zip(names, ops_vjp((toep_bwd(d_toep), d_wz, d_wy, d_lam_re, d_lam_im))))
        grads.update(d=d_d)
        return d_hn + from_groups(d_ug), grads

    return out, backward


_SMALL =[("gla_w_a2", 2), ("gla_b_a", 1), ("gla_norm_g", 1), ("ssd_conv_w", 2), ("s5_d", 1)]
_REPLICATED = ["norm_mix_g", "norm_ffn_g", "ssd_conv_b", "ssd_dt_bias", "ssd_a_log", "ssd_d", "ssd_norm_g", "s5_log_dt",
               "s5_a_re", "s5_a_im", "s5_b_re", "s5_b_im", "s5_c_re", "s5_c_im", "final_norm_g"]
_WEIGHTS = ['norm_mix_g', 'norm_ffn_g', 'gla_w_in', 'gla_w_a2', 'gla_b_a', 'gla_norm_g', 'gla_w_out', 'ssd_w_in',
            'ssd_conv_w', 'ssd_conv_b', 'ssd_dt_bias', 'ssd_a_log', 'ssd_d', 'ssd_norm_g', 'ssd_w_out', 's5_log_dt',
            's5_a_re', 's5_a_im', 's5_b_re', 's5_b_im', 's5_c_re', 's5_c_im', 's5_d', 's5_w_glu', 'ffn_w_gu', 'ffn_w_down',
            'final_norm_g']


def _gather_small(local):
    shapes = [local[n].shape for n, _ in _SMALL]
    blocks = _collective("gather", _pack([local[n] for n, _ in _SMALL]), "gather_vectors")
    parts = _unpack(blocks, shapes, lead=(N_DEV,))
    return {n: _join(part, axis) for (n, axis), part in zip(_SMALL, parts)}


def _forward_plan():
    plan = []
    for i in range(DEPTH):
        j = i // 3
        plan += [[("gla_w_in", j), ("gla_w_out", j)], [("ssd_w_in", j), ("ssd_w_out", j)], [("s5_w_glu", j)]][i % 3]
        plan += [("ffn_w_gu", i), ("ffn_w_down", i)]
    return plan


def _forward_backward(x, target, w, traffic):
    big = lambda name, j: (traffic, (name, j))
    gla = lambda j: dict(w_in=big("gla_w_in", j), w_a2=w["gla_w_a2"][j], b_a=w["gla_b_a"][j], norm_g=w["gla_norm_g"][j],
                         w_out=big("gla_w_out", j))
    ssd = lambda j: dict(w_in=big("ssd_w_in", j), conv_w=w["ssd_conv_w"][j], conv_b=w["ssd_conv_b"][j],
                         dt_bias=w["ssd_dt_bias"][j], a_log=w["ssd_a_log"][j], d=w["ssd_d"][j], norm_g=w["ssd_norm_g"][j],
                         w_out=big("ssd_w_out", j))
    s5 = lambda j: dict(log_dt=w["s5_log_dt"][j], a_re=w["s5_a_re"][j], a_im=w["s5_a_im"][j], b_re=w["s5_b_re"][j],
                        b_im=w["s5_b_im"][j], c_re=w["s5_c_re"][j], c_im=w["s5_c_im"][j], d=w["s5_d"][j],
                        w_glu=big("s5_w_glu", j))
    mixers = [("gla", _gla_mixer, gla), ("ssd", _ssd_mixer, ssd), ("s5", _s5_mixer, s5)]
    h = x
    tape = []
    for i in range(DEPTH):
        kind, mixer, params = mixers[i % 3]
        j = i // 3
        hn, norm1_vjp = jax.vjp(_rmsnorm, h, w["norm_mix_g"][i])
        y, mixer_bwd = mixer(hn.astype(BF16), params(j), f"l{i}_{kind}")
        h_mid = h + y
        hn2, norm2_vjp = jax.vjp(_rmsnorm, h_mid, w["norm_ffn_g"][i])
        gu, gu_bwd = _linear(hn2.astype(BF16), big("ffn_w_gu", i), f"l{i}_ffn_gu", out_dtype=BF16)
        act, act_vjp = jax.vjp(_swiglu, gu)
        f, down_bwd = _linear(act, big("ffn_w_down", i), f"l{i}_ffn_down", dx_dtype=BF16)
        h = h_mid + f
        tape.append((kind, j, norm1_vjp, mixer_bwd, norm2_vjp, gu_bwd, act_vjp, down_bwd))
    loss, head_vjp = jax.vjp(_loss_head, h, w["final_norm_g"], target)
    d_h, d_final_g, _ = head_vjp(jnp.ones((), F32))

    grads = {n: [None] * w[n].shape[0] for n in w if n != "final_norm_g"}
    grads["final_norm_g"] = d_final_g
    for i in reversed(range(DEPTH)):
        kind, j, norm1_vjp, mixer_bwd, norm2_vjp, gu_bwd, act_vjp, down_bwd = tape[i]
        d_gu, = act_vjp(down_bwd(d_h))
        d_mid, grads["norm_ffn_g"][i] = norm2_vjp(gu_bwd(d_gu))
        d_mid = d_mid + d_h
        d_hn, mixer_grads = mixer_bwd(d_mid)
        for k, g in mixer_grads.items():
            grads[f"{kind}_{k}"][j] = g
        d_in, grads["norm_mix_g"][i] = norm1_vjp(d_hn)
        d_h = d_in + d_mid
    return loss, d_h, grads


def kernel(x, norm_mix_g, norm_ffn_g, gla_w_in, gla_w_a2, gla_b_a, gla_norm_g, gla_w_out, ssd_w_in, ssd_conv_w, ssd_conv_b, ssd_dt_bias, ssd_a_log, ssd_d, ssd_norm_g, ssd_w_out, s5_log_dt, s5_a_re, s5_a_im, s5_b_re, s5_b_im, s5_c_re, s5_c_im, s5_d, s5_w_glu, ffn_w_gu, ffn_w_down, final_norm_g, loss_target, m_norm_mix_g, m_norm_ffn_g, m_gla_w_in, m_gla_w_a2, m_gla_b_a, m_gla_norm_g, m_gla_w_out, m_ssd_w_in, m_ssd_conv_w, m_ssd_conv_b, m_ssd_dt_bias, m_ssd_a_log, m_ssd_d, m_ssd_norm_g, m_ssd_w_out, m_s5_log_dt, m_s5_a_re, m_s5_a_im, m_s5_b_re, m_s5_b_im, m_s5_c_re, m_s5_c_im, m_s5_d, m_s5_w_glu, m_ffn_w_gu, m_ffn_w_down, m_final_norm_g, v_norm_mix_g, v_norm_ffn_g, v_gla_w_in, v_gla_w_a2, v_gla_b_a, v_gla_norm_g, v_gla_w_out, v_ssd_w_in, v_ssd_conv_w, v_ssd_conv_b, v_ssd_dt_bias, v_ssd_a_log, v_ssd_d, v_ssd_norm_g, v_ssd_w_out, v_s5_log_dt, v_s5_a_re, v_s5_a_im, v_s5_b_re, v_s5_b_im, v_s5_c_re, v_s5_c_im, v_s5_d, v_s5_w_glu, v_ffn_w_gu, v_ffn_w_down, v_final_norm_g):
    args = locals()
    local = {n: args[n] for n in _WEIGHTS}
    moment_m = {n: args["m_" + n] for n in _WEIGHTS}
    moment_v = {n: args["v_" + n] for n in _WEIGHTS}

    shards = {(n, layer): local[n][layer].astype(BF16) for n, _ in _BIG for layer in range(local[n].shape[0])}
    traffic = _Traffic(shards, _forward_plan())
    full = {n: local[n] for n in _REPLICATED}
    full.update(_gather_small(local))

    loss, d_x, grads = _forward_backward(x[0], loss_target[0], full, traffic)
    traffic.flush()
    loss = lax.psum(loss, ("x", "y", "c"))
    kinds = ("grad", "delta", "new_m", "new_v")
    out = {}

    for n, _ in _BIG:
        parts = jnp.stack([traffic.received[(n, layer)] for layer in range(local[n].shape[0])], axis=1)
        results = _adamw(parts, local[n], moment_m[n], moment_v[n], "adamw_" + n)
        out.update({f"{kind}_{n}": a for kind, a in zip(kinds, results)})

    small = [n for n, _ in _SMALL] + _REPLICATED
    stacked = lambda n: grads[n] if n == "final_norm_g" else jnp.stack(grads[n])
    parts = _collective("gather", _pack([stacked(n) for n in small]), "gather_small_gradients")
    summed = _unpack(_sum_parts(parts, "sum_small_gradients"), [stacked(n).shape for n in small])
    position = _index(_mesh_position())
    mine = [_own_shard(g, axis, position) for g, (_, axis) in zip(summed, _SMALL)] + summed[len(_SMALL):]
    shapes = [local[n].shape for n in small]
    pk = lambda arrays: _pack(arrays)[None]
    results = _adamw(pk(mine)[None], pk([local[n] for n in small]), pk([moment_m[n] for n in small]),
                     pk([moment_v[n] for n in small]), "adamw_small")
    for kind, flat in zip(kinds, results):
        out.update({f"{kind}_{n}": a for n, a in zip(small, _unpack(flat[0], shapes))})

    return (loss, d_x[None], *[out[f"{kind}_{n}"] for kind in ("grad", "delta", "new_m", "new_v") for n in _WEIGHTS])
```

```python
import functools
import math

import jax
import jax.numpy as jnp
import numpy as np
from jax import lax
from jax.experimental import pallas as pl
from jax.experimental.pallas import tpu as pltpu

F32 = jnp.float32
BF16 = jnp.bfloat16
_MXU_DTYPE = jnp.bfloat16

N_DEV = 8
D_MODEL = 1024
DEPTH = 4
CHUNK = 64
STEP_CHUNKS = 1
STEP = CHUNK * STEP_CHUNKS
EPS = 1e-6
GLA_HEADS, GLA_DK, GLA_DV, GLA_RANK, GLA_TAU = 4, 128, 256, 16, 16.0
GLA_QK = GLA_HEADS * GLA_DK
GLA_VD = GLA_HEADS * GLA_DV
LANES = 128
GLA_IN = 2 * GLA_QK + 2 * GLA_VD + GLA_RANK
GLA_PROJ = 2 * GLA_QK + 2 * GLA_VD + LANES
SSD_DINNER, SSD_HEADDIM, SSD_HEADS, SSD_GROUPS, SSD_HPG, SSD_DSTATE, SSD_CONV = 2048, 64, 32, 8, 4, 128, 4
SSD_GN = SSD_GROUPS * SSD_DSTATE
SSD_GW = SSD_HPG * SSD_HEADDIM
SSD_XBC = SSD_DINNER + 2 * SSD_GN
SSD_IN = SSD_DINNER + SSD_XBC + SSD_HEADS
SSD_PROJ = SSD_DINNER + SSD_XBC + LANES
S5_GROUP, S5_GROUPS, S5_STATE = 16, 64, 64
S5_CHUNK = 32
FFN_HIDDEN = 2816
ADAM_LR, ADAM_B1, ADAM_B2, ADAM_EPS, ADAM_WD, ADAM_STEP = 0.001, 0.9, 0.999, 1e-08, 0.01, 10
VMEM_LIMIT = 48 * 1024 * 1024
FLAT_COLS = 1024
FLAT_ROWS_ALIGN = 64


def _tile(n, cap, unit):
    if n <= cap:
        return n
    best = None
    for t in range(unit, cap + 1, unit):
        if n % t == 0:
            best = t
    assert best is not None, (n, cap, unit)
    return best


def _divisors(n, unit):
    return sorted({t for t in range(unit, n + 1, unit) if n % t == 0} | {n})


_MXU_FLOPS, _HBM_BYTES, _ACC_BYTES, _STEP_SECONDS = 1.1e15, 3e12, 1.1e13, 3.5e-7
_MXU_ROWS = 256
_TILE_VMEM_BUDGET = 36 * 1024 * 1024


def _pick_tiles(m, n, k, a_bytes, b_bytes, o_bytes, m_unit):
    best = None
    for tm in _divisors(m, m_unit):
        for tn in _divisors(n, LANES):
            for tk in _divisors(k, LANES):
                nk = k // tk
                vmem = 2 * tm * tk * a_bytes + 2 * tk * tn * b_bytes + 2 * tm * tn * o_bytes + (nk > 1) * tm * tn * 4
                if vmem > _TILE_VMEM_BUDGET or tm > 2048 or tn > 2048:
                    continue
                a_reads = n // tn if nk > 1 else 1
                b_reads = 1 if (nk == 1 and n == tn) else m // tm
                traffic = m * k * a_bytes * a_reads + k * n * b_bytes * b_reads + m * n * o_bytes
                mxu = 2.0 * m * n * k / _MXU_FLOPS * (1.0 + _MXU_ROWS / tm)
                cost = (max(mxu, traffic / _HBM_BYTES) + (nk > 1) * nk * m * n * 8 / _ACC_BYTES
                        + (m // tm) * (n // tn) * nk * _STEP_SECONDS)
                if best is None or cost < best[0]:
                    best = (cost, tm, tn, tk)
    assert best is not None, (m, n, k)
    return best[1:]


def _matmul(a, b, *, ta=False, tb=False, out_dtype=F32, name, cargo=()):
    batched = a.ndim == 3
    if ta:
        k_dim, m_dim = a.shape[-2:]
    else:
        m_dim, k_dim = a.shape[-2:]
    if tb:
        n_dim, kb = b.shape[-2:]
    else:
        kb, n_dim = b.shape[-2:]
    assert kb == k_dim, (a.shape, b.shape, ta, tb)
    tm, tn, tk = _pick_tiles(m_dim, n_dim, k_dim, a.dtype.itemsize, b.dtype.itemsize, jnp.dtype(out_dtype).itemsize,
                             LANES if ta else 16)
    nk = k_dim // tk
    ca, cb = (0 if ta else 1), (1 if tb else 0)
    grid = (m_dim // tm, n_dim // tn, nk)
    if batched:
        grid = (a.shape[0],) + grid
    n_cargo = len(cargo)

    def body(*refs):
        a_ref, b_ref = refs[:2]
        cargo_in = refs[2:2 + n_cargo]
        o_ref = refs[2 + n_cargo]
        cargo_out = refs[3 + n_cargo:3 + 2 * n_cargo]
        scratch = refs[3 + 2 * n_cargo:]
        acc = scratch[:1] if nk > 1 else ()
        sems = scratch[len(acc):]
        ids = [pl.program_id(d) for d in range(len(grid))]
        first = functools.reduce(jnp.logical_and, [i == 0 for i in ids])
        last = functools.reduce(jnp.logical_and, [i == g - 1 for i, g in zip(ids, grid)])
        moves = lambda: [_moves(kind, x_ref, y_ref, *sems[3 * c:3 * c + 3])
                         for c, ((kind, _), x_ref, y_ref) in enumerate(zip(cargo, cargo_in, cargo_out))]
        if n_cargo:
            @pl.when(first)
            def _():
                for mv in moves():
                    _start(mv)

        part = lax.dot_general(a_ref[...].astype(_MXU_DTYPE), b_ref[...].astype(_MXU_DTYPE),
                               (((ca,), (cb,)), ((), ())), preferred_element_type=F32)
        if nk == 1:
            o_ref[...] = part.astype(o_ref.dtype)
        else:
            acc_ref, = acc
            k = ids[-1]

            @pl.when(k == 0)
            def _():
                acc_ref[...] = part

            @pl.when(k > 0)
            def _():
                acc_ref[...] += part

            @pl.when(k == nk - 1)
            def _():
                o_ref[...] = acc_ref[...].astype(o_ref.dtype)

        if n_cargo:
            @pl.when(last)
            def _():
                for mv in moves():
                    _finish(mv)

    lead = (None,) if batched else ()

    def spec(shape, fn):
        if batched:
            return pl.BlockSpec(lead + shape, lambda g, i, j, k: (g,) + fn(i, j, k))
        return pl.BlockSpec(shape, fn)

    a_spec = spec((tk, tm), lambda i, j, k: (k, i)) if ta else spec((tm, tk), lambda i, j, k: (i, k))
    b_spec = spec((tn, tk), lambda i, j, k: (j, k)) if tb else spec((tk, tn), lambda i, j, k: (k, j))
    o_spec = spec((tm, tn), lambda i, j, k: (i, j))
    sem = ("parallel",) * (len(grid) - 1) + ("arbitrary",) if not n_cargo else ("arbitrary",) * len(grid)
    out_shape = ((a.shape[0],) if batched else ()) + (m_dim, n_dim)
    scratch = [pltpu.VMEM((tm, tn), F32)] if nk > 1 else []
    for _ in cargo:
        scratch += [pltpu.SemaphoreType.DMA((N_DEV - 1,)), pltpu.SemaphoreType.DMA((N_DEV - 1,)), pltpu.SemaphoreType.DMA]
    results = pl.pallas_call(
        body, name=name, grid=grid, in_specs=[a_spec, b_spec] + [_ANY] * n_cargo, out_specs=[o_spec] + [_ANY] * n_cargo,
        out_shape=[jax.ShapeDtypeStruct(out_shape, out_dtype)] + [_moved_shape(kind, x) for kind, x in cargo],
        scratch_shapes=scratch,
        compiler_params=pltpu.CompilerParams(dimension_semantics=sem, vmem_limit_bytes=VMEM_LIMIT),
    )(a, b, *[x for _, x in cargo])
    return (results[0], list(results[1:])) if n_cargo else results[0]


def _dot(a, b, ca=1, cb=0, exact=False):
    if exact:
        return lax.dot_general(a, b, (((ca,), (cb,)), ((), ())), precision=lax.Precision.HIGHEST,
                               preferred_element_type=F32)
    return lax.dot_general(a.astype(_MXU_DTYPE), b.astype(_MXU_DTYPE), (((ca,), (cb,)), ((), ())),
                           preferred_element_type=F32)


def _tri(n):
    return lax.broadcasted_iota(jnp.int32, (n, n), 0) >= lax.broadcasted_iota(jnp.int32, (n, n), 1)


def _log_sigmoid(x):
    return jnp.minimum(x, 0.0) - jnp.log(1.0 + jnp.exp(-jnp.abs(x)))


def _softplus(x):
    return jnp.maximum(x, 0.0) + jnp.log(1.0 + jnp.exp(-jnp.abs(x)))


def _silu(x):
    return x / (1.0 + jnp.exp(-x))


def _full_spec(shape):
    return pl.BlockSpec(shape, lambda c: (0,) * len(shape))


_SEQ_PARAMS = pltpu.CompilerParams(dimension_semantics=("arbitrary",), vmem_limit_bytes=VMEM_LIMIT)


def _gla_chunk(proj, st, w_a2, b_a, norm_g):
    t = proj.shape[0]
    q = proj[:, 0:GLA_QK] * (GLA_DK ** -0.5)
    k = proj[:, GLA_QK:2 * GLA_QK]
    v = proj[:, 2 * GLA_QK:2 * GLA_QK + GLA_VD]
    r = proj[:, 2 * GLA_QK + GLA_VD:2 * GLA_QK + 2 * GLA_VD]
    a_low = proj[:, 2 * GLA_QK + 2 * GLA_VD:]
    log_a = _log_sigmoid(_dot(a_low, w_a2) + b_a) * (1.0 / GLA_TAU)
    past = _tri(t)
    lc = _dot(past.astype(F32), log_a, exact=True)
    lend = lc[t - 1:t, :]
    e_pos = jnp.exp(lc)
    e_neg = jnp.exp(-lc)
    q_fwd, k_fwd, q_bwd, k_bwd = q * e_pos, k * e_neg, q * e_neg, k * e_pos
    kd = k * jnp.exp(lend - lc)
    g = jnp.exp(lend)
    outs, new_st = [], []
    for h in range(GLA_HEADS):
        sk = slice(h * GLA_DK, (h + 1) * GLA_DK)
        sv = slice(h * GLA_DV, (h + 1) * GLA_DV)
        s_past = _dot(q_fwd[:, sk], k_fwd[:, sk], 1, 1)
        s_future = _dot(q_bwd[:, sk], k_bwd[:, sk], 1, 1)
        scores = jnp.where(past, s_past, s_future)
        o = _dot(scores, v[:, sv]) + _dot(q_fwd[:, sk], st[h], 1, 1)
        new_st.append(st[h] * g[:, sk] + _dot(v[:, sv], kd[:, sk], 0, 0))
        o = o * lax.rsqrt(jnp.mean(o * o, axis=-1, keepdims=True) + EPS) * norm_g[:, sv]
        outs.append(o)
    return jnp.concatenate(outs, axis=1) * _silu(r), tuple(new_st)


_GLA_STATE = (GLA_HEADS, GLA_DV, GLA_DK)


def _gla_step(proj, st, w_a2, b_a, norm_g):
    outs = []
    for s in range(STEP_CHUNKS):
        out, st = _gla_chunk(proj[s * CHUNK:(s + 1) * CHUNK], st, w_a2, b_a, norm_g)
        outs.append(out)
    return jnp.concatenate(outs, axis=0), st


def _gla_core_fwd(proj, w_a2, b_a, norm_g):
    seq = proj.shape[0]
    nc = seq // STEP

    def body(proj_ref, wa_ref, ba_ref, ng_ref, o_ref, sprev_ref, st_ref):
        @pl.when(pl.program_id(0) == 0)
        def _():
            st_ref[...] = jnp.zeros_like(st_ref)

        st = tuple(st_ref[h] for h in range(GLA_HEADS))
        for h in range(GLA_HEADS):
            sprev_ref[0, h] = st[h]
        out, new_st = _gla_step(proj_ref[...], st, wa_ref[...], ba_ref[...], ng_ref[...])
        o_ref[...] = out
        for h in range(GLA_HEADS):
            st_ref[h] = new_st[h]

    return pl.pallas_call(
        body, name="gla_core_fwd", grid=(nc,),
        in_specs=[pl.BlockSpec((STEP,GLA_PROJ), lambda c: (c, 0)), _full_spec(w_a2.shape), _full_spec(b_a.shape),
                  _full_spec(norm_g.shape)],
        out_specs=[pl.BlockSpec((STEP,GLA_VD), lambda c: (c, 0)), pl.BlockSpec((1,) + _GLA_STATE, lambda c: (c, 0, 0, 0))],
        out_shape=[jax.ShapeDtypeStruct((seq, GLA_VD), F32), jax.ShapeDtypeStruct((nc,) + _GLA_STATE, F32)],
        scratch_shapes=[pltpu.VMEM(_GLA_STATE, F32)],
        compiler_params=_SEQ_PARAMS,
    )(proj, w_a2, b_a, norm_g)


def _gla_core_bwd(proj, sprev, d_out, w_a2, b_a, norm_g):
    seq = proj.shape[0]
    nc = seq // STEP

    def body(proj_ref, sprev_ref, do_ref, wa_ref, ba_ref, ng_ref, dproj_ref, dwa_ref, dba_ref, dng_ref, dst_ref):
        @pl.when(pl.program_id(0) == 0)
        def _():
            dst_ref[...] = jnp.zeros_like(dst_ref)
            dwa_ref[...] = jnp.zeros_like(dwa_ref)
            dba_ref[...] = jnp.zeros_like(dba_ref)
            dng_ref[...] = jnp.zeros_like(dng_ref)

        st = tuple(sprev_ref[0, h] for h in range(GLA_HEADS))
        _, vjp = jax.vjp(_gla_step, proj_ref[...], st, wa_ref[...], ba_ref[...], ng_ref[...])
        d_next = tuple(dst_ref[h] for h in range(GLA_HEADS))
        d_proj, d_st, d_wa, d_ba, d_ng = vjp((do_ref[...], d_next))
        dproj_ref[...] = d_proj
        for h in range(GLA_HEADS):
            dst_ref[h] = d_st[h]
        dwa_ref[...] += d_wa
        dba_ref[...] += d_ba
        dng_ref[...] += d_ng

    rev = lambda c: (nc - 1 - c, 0)
    return pl.pallas_call(
        body, name="gla_core_bwd", grid=(nc,),
        in_specs=[pl.BlockSpec((STEP,GLA_PROJ), rev), pl.BlockSpec((1,) + _GLA_STATE, lambda c: (nc - 1 - c, 0, 0, 0)),
                  pl.BlockSpec((STEP,GLA_VD), rev), _full_spec(w_a2.shape), _full_spec(b_a.shape), _full_spec(norm_g.shape)],
        out_specs=[pl.BlockSpec((STEP,GLA_PROJ), rev), _full_spec(w_a2.shape), _full_spec(b_a.shape), _full_spec(norm_g.shape)],
        out_shape=[jax.ShapeDtypeStruct((seq, GLA_PROJ), F32), jax.ShapeDtypeStruct(w_a2.shape, F32),
                   jax.ShapeDtypeStruct(b_a.shape, F32), jax.ShapeDtypeStruct(norm_g.shape, F32)],
        scratch_shapes=[pltpu.VMEM(_GLA_STATE, F32)],
        compiler_params=_SEQ_PARAMS,
    )(proj, sprev, d_out, w_a2, b_a, norm_g)


def _ssd_chunk(z, xbc, dt_raw, hs, dt_bias, a_log, d_skip, norm_g):
    t = z.shape[0]
    xs = xbc[:, :SSD_DINNER]
    bm = xbc[:, SSD_DINNER:SSD_DINNER + SSD_GN]
    cm = xbc[:, SSD_DINNER + SSD_GN:]
    dt = _softplus(dt_raw + dt_bias)
    da = dt * (-jnp.exp(a_log))
    tri = _tri(t).astype(F32)
    eye = (lax.broadcasted_iota(jnp.int32, (t, t), 0) == lax.broadcasted_iota(jnp.int32, (t, t), 1)).astype(F32)
    cum = _dot(tri, da, exact=True)
    cum_t = _dot(da, tri, 0, 1, exact=True)
    dt_t = _dot(dt, eye, 0, 0, exact=True)
    cum_end = cum[t - 1:t, :]
    w_state = dt * jnp.exp(cum_end - cum)
    e_cum = jnp.exp(cum)
    g_end = jnp.exp(cum_end)
    head_of = lambda axis: lax.shift_right_logical(lax.broadcasted_iota(jnp.int32, (SSD_GW, SSD_GW), axis),
                                                   jnp.int32(SSD_HEADDIM.bit_length() - 1))
    same_head = head_of(0) == head_of(1)
    ys, new_hs = [], []
    for g in range(SSD_GROUPS):
        heads = range(g * SSD_HPG, (g + 1) * SSD_HPG)
        cols = slice(g * SSD_GW, (g + 1) * SSD_GW)

        def spread(a):
            return jnp.concatenate([jnp.broadcast_to(a[:, h:h + 1], (a.shape[0], SSD_HEADDIM)) for h in heads], axis=1)

        def row(a_t):
            return jnp.concatenate([a_t[h:h + 1, :] for h in heads], axis=1)

        bm_g = bm[:, g * SSD_DSTATE:(g + 1) * SSD_DSTATE]
        cm_g = cm[:, g * SSD_DSTATE:(g + 1) * SSD_DSTATE]
        xs_g = xs[:, cols]
        cb = _dot(cm_g, jnp.concatenate([bm_g] * SSD_HPG, axis=0), 1, 1)
        mix = cb * jnp.exp(-jnp.abs(spread(cum) - row(cum_t))) * row(dt_t)
        x_diag = jnp.where(same_head, jnp.concatenate([xs_g] * SSD_HPG, axis=0), 0.0)
        y = _dot(mix, x_diag)
        y = y + _dot(cm_g, hs[g], 1, 1) * spread(e_cum)
        y = y + spread(d_skip) * xs_g
        states = _dot(xs_g * spread(w_state), bm_g, 0, 0)
        decayed = jnp.concatenate([g_end[:, h:h + 1] * hs[g][j * SSD_HEADDIM:(j + 1) * SSD_HEADDIM, :]
                                   for j, h in enumerate(heads)], axis=0)
        new_hs.append(decayed + states)
        yg = y * _silu(z[:, cols])
        ys.append(yg * lax.rsqrt(jnp.mean(yg * yg, axis=-1, keepdims=True) + EPS) * norm_g[:, cols])
    return jnp.concatenate(ys, axis=1), tuple(new_hs)


_SSD_STATE = (SSD_GROUPS, SSD_GW, SSD_DSTATE)


def _ssd_step(z, xbc, dt_raw, hs, dt_bias, a_log, d_skip, norm_g):
    outs = []
    for s in range(STEP_CHUNKS):
        rows = slice(s * CHUNK, (s + 1) * CHUNK)
        out, hs = _ssd_chunk(z[rows], xbc[rows], dt_raw[rows], hs, dt_bias, a_log, d_skip, norm_g)
        outs.append(out)
    return jnp.concatenate(outs, axis=0), hs
_SSD_DT_BLOCK = (SSD_DINNER + SSD_XBC) // LANES


def _ssd_core_fwd(proj, xbc, dt_bias, a_log, d_skip, norm_g):
    seq = proj.shape[0]
    nc = seq // STEP

    def body(z_ref, xbc_ref, dt_ref, db_ref, al_ref, ds_ref, ng_ref, o_ref, hprev_ref, hs_ref):
        @pl.when(pl.program_id(0) == 0)
        def _():
            hs_ref[...] = jnp.zeros_like(hs_ref)

        hs = tuple(hs_ref[g] for g in range(SSD_GROUPS))
        for g in range(SSD_GROUPS):
            hprev_ref[0, g] = hs[g]
        out, new_hs = _ssd_step(z_ref[...], xbc_ref[...], dt_ref[...], hs, db_ref[...], al_ref[...], ds_ref[...], ng_ref[...])
        o_ref[...] = out
        for g in range(SSD_GROUPS):
            hs_ref[g] = new_hs[g]

    return pl.pallas_call(
        body, name="ssd_core_fwd", grid=(nc,),
        in_specs=[pl.BlockSpec((STEP,SSD_DINNER), lambda c: (c, 0)), pl.BlockSpec((STEP,SSD_XBC), lambda c: (c, 0)),
                  pl.BlockSpec((STEP,LANES), lambda c: (c, _SSD_DT_BLOCK)),
                  _full_spec(dt_bias.shape), _full_spec(a_log.shape), _full_spec(d_skip.shape), _full_spec(norm_g.shape)],
        out_specs=[pl.BlockSpec((STEP,SSD_DINNER), lambda c: (c, 0)), pl.BlockSpec((1,) + _SSD_STATE, lambda c: (c, 0, 0, 0))],
        out_shape=[jax.ShapeDtypeStruct((seq, SSD_DINNER), F32), jax.ShapeDtypeStruct((nc,) + _SSD_STATE, F32)],
        scratch_shapes=[pltpu.VMEM(_SSD_STATE, F32)],
        compiler_params=_SEQ_PARAMS,
    )(proj, xbc, proj, dt_bias, a_log, d_skip, norm_g)


def _ssd_core_bwd(proj, xbc, hprev, d_out, dt_bias, a_log, d_skip, norm_g):
    seq = proj.shape[0]
    nc = seq // STEP

    def body(z_ref, xbc_ref, dt_ref, hprev_ref, do_ref, db_ref, al_ref, ds_ref, ng_ref,
             dz_ref, dxbc_ref, ddt_ref, ddb_ref, dal_ref, dds_ref, dng_ref, dhs_ref):
        @pl.when(pl.program_id(0) == 0)
        def _():
            dhs_ref[...] = jnp.zeros_like(dhs_ref)
            ddb_ref[...] = jnp.zeros_like(ddb_ref)
            dal_ref[...] = jnp.zeros_like(dal_ref)
            dds_ref[...] = jnp.zeros_like(dds_ref)
            dng_ref[...] = jnp.zeros_like(dng_ref)

        hs = tuple(hprev_ref[0, g] for g in range(SSD_GROUPS))
        _, vjp = jax.vjp(_ssd_step, z_ref[...], xbc_ref[...], dt_ref[...], hs, db_ref[...], al_ref[...], ds_ref[...], ng_ref[...])
        d_next = tuple(dhs_ref[g] for g in range(SSD_GROUPS))
        d_z, d_xbc, d_dt, d_hs, d_db, d_al, d_ds, d_ng = vjp((do_ref[...], d_next))
        dz_ref[...] = d_z
        dxbc_ref[...] = d_xbc
        ddt_ref[...] = d_dt
        for g in range(SSD_GROUPS):
            dhs_ref[g] = d_hs[g]
        ddb_ref[...] += d_db
        dal_ref[...] += d_al
        dds_ref[...] += d_ds
        dng_ref[...] += d_ng

    rev = lambda c: (nc - 1 - c, 0)
    vec = [_full_spec(dt_bias.shape), _full_spec(a_log.shape), _full_spec(d_skip.shape), _full_spec(norm_g.shape)]
    return pl.pallas_call(
        body, name="ssd_core_bwd", grid=(nc,),
        in_specs=[pl.BlockSpec((STEP,SSD_DINNER), rev), pl.BlockSpec((STEP,SSD_XBC), rev),
                  pl.BlockSpec((STEP,LANES), lambda c: (nc - 1 - c, _SSD_DT_BLOCK)),
                  pl.BlockSpec((1,) + _SSD_STATE, lambda c: (nc - 1 - c, 0, 0, 0)),
                  pl.BlockSpec((STEP,SSD_DINNER), rev)] + vec,
        out_specs=[pl.BlockSpec((STEP,SSD_DINNER), rev), pl.BlockSpec((STEP,SSD_XBC), rev),
                   pl.BlockSpec((STEP,LANES), rev)] + vec,
        out_shape=[jax.ShapeDtypeStruct((seq, SSD_DINNER), F32), jax.ShapeDtypeStruct((seq, SSD_XBC), F32),
                   jax.ShapeDtypeStruct((seq, LANES), F32),
                   jax.ShapeDtypeStruct(dt_bias.shape, F32), jax.ShapeDtypeStruct(a_log.shape, F32),
                   jax.ShapeDtypeStruct(d_skip.shape, F32), jax.ShapeDtypeStruct(norm_g.shape, F32)],
        scratch_shapes=[pltpu.VMEM(_SSD_STATE, F32)],
        compiler_params=_SEQ_PARAMS,
    )(proj, xbc, proj, hprev, d_out, dt_bias, a_log, d_skip, norm_g)


def _s5_boundary_scan(z_re, z_im, lam_re, lam_im):
    n_chunks = z_re.shape[0]

    def body(zr_ref, zi_ref, lr_ref, li_ref, xr_ref, xi_ref):
        lr, li = lr_ref[...], li_ref[...]

        def step(n, carry):
            xr, xi = carry
            xr_ref[n] = xr
            xi_ref[n] = xi
            return lr * xr - li * xi + zr_ref[n], lr * xi + li * xr + zi_ref[n]

        zero = jnp.zeros(lr.shape, F32)
        lax.fori_loop(0, n_chunks, step, (zero, zero))

    shape = jax.ShapeDtypeStruct(z_re.shape, F32)
    return pl.pallas_call(body, name="s5_boundary_scan", out_shape=[shape, shape],
                          compiler_params=pltpu.CompilerParams(vmem_limit_bytes=VMEM_LIMIT))(z_re, z_im, lam_re, lam_im)


_FLIPS = [(kx, ky, kc) for kx in (0, 1) for ky in (0, 1) for kc in (0, 1)][1:]


def _mesh_position():
    return lax.axis_index("x"), lax.axis_index("y"), lax.axis_index("c")


def _peer(pos, flip):
    return tuple((1 - p) if f else p for p, f in zip(pos, flip))


def _index(pos):
    return 4 * pos[0] + 2 * pos[1] + pos[2]


_ANY = pl.BlockSpec(memory_space=pl.ANY)


def _moved_shape(kind, x):
    return jax.ShapeDtypeStruct(((N_DEV,) + x.shape) if kind == "gather" else x.shape, x.dtype)


def _moves(kind, x_ref, out_ref, send_sems, recv_sems, local_sem):
    me = _mesh_position()
    source = (lambda pos: x_ref) if kind == "gather" else (lambda pos: x_ref.at[_index(pos)])
    local = pltpu.make_async_copy(source(me), out_ref.at[_index(me)], local_sem)
    outgoing, incoming = [], []
    for k, flip in enumerate(_FLIPS):
        peer = _peer(me, flip)
        copy = lambda slot: pltpu.make_async_remote_copy(
            src_ref=source(peer), dst_ref=out_ref.at[_index(slot)], send_sem=send_sems.at[k], recv_sem=recv_sems.at[k],
            device_id=peer, device_id_type=pl.DeviceIdType.MESH)
        outgoing.append(copy(me))
        incoming.append(copy(peer))
    return local, outgoing, incoming


def _start(moves):
    local, outgoing, _ = moves
    local.start()
    for cp in outgoing:
        cp.start()


def _finish(moves):
    local, outgoing, incoming = moves
    for cp in incoming:
        cp.wait_recv()
    for cp in outgoing:
        cp.wait_send()
    local.wait()


def _collective(kind, x, name):
    def body(x_ref, out_ref, send_sems, recv_sems, local_sem):
        moves = _moves(kind, x_ref, out_ref, send_sems, recv_sems, local_sem)
        _start(moves)
        _finish(moves)

    return pl.pallas_call(
        body, name=name, in_specs=[_ANY], out_specs=_ANY, out_shape=_moved_shape(kind, x),
        scratch_shapes=[pltpu.SemaphoreType.DMA((N_DEV - 1,)), pltpu.SemaphoreType.DMA((N_DEV - 1,)), pltpu.SemaphoreType.DMA],
        compiler_params=pltpu.CompilerParams(has_side_effects=True),
    )(x)


def _adamw(parts, w, m, v, name):
    n_parts = parts.shape[0]
    layers, rows, cols = w.shape
    tr = _tile(rows, 256, 8)

    def body(p_ref, w_ref, m_ref, v_ref, g_ref, d_ref, mo_ref, vo_ref):
        g = p_ref[0].astype(F32)
        for s in range(1, n_parts):
            g = g + p_ref[s].astype(F32)
        m_new = ADAM_B1 * m_ref[...] + (1.0 - ADAM_B1) * g
        v_new = ADAM_B2 * v_ref[...] + (1.0 - ADAM_B2) * (g * g)
        m_hat = m_new / (1.0 - ADAM_B1 ** ADAM_STEP)
        v_hat = v_new / (1.0 - ADAM_B2 ** ADAM_STEP)
        g_ref[...] = g
        d_ref[...] = -ADAM_LR * (m_hat / (jnp.sqrt(v_hat) + ADAM_EPS) + ADAM_WD * w_ref[...])
        mo_ref[...] = m_new
        vo_ref[...] = v_new

    blk = pl.BlockSpec((None, tr, cols), lambda l, i: (l, i, 0))
    shape = jax.ShapeDtypeStruct(w.shape, F32)
    return pl.pallas_call(
        body, name=name, grid=(layers, rows // tr),
        in_specs=[pl.BlockSpec((n_parts, None, tr, cols), lambda l, i: (0, l, i, 0)), blk, blk, blk],
        out_specs=[blk, blk, blk, blk], out_shape=[shape, shape, shape, shape],
        compiler_params=pltpu.CompilerParams(dimension_semantics=("parallel", "parallel"), vmem_limit_bytes=VMEM_LIMIT),
    )(parts, w, m, v)


def _sum_parts(parts, name):
    _, rows, cols = parts.shape
    tr = _tile(rows, 256, 8)

    def body(p_ref, o_ref):
        total = p_ref[0]
        for s in range(1, N_DEV):
            total = total + p_ref[s]
        o_ref[...] = total

    return pl.pallas_call(
        body, name=name, grid=(rows // tr,),
        in_specs=[pl.BlockSpec((N_DEV, tr, cols), lambda i: (0, i, 0))], out_specs=pl.BlockSpec((tr, cols), lambda i: (i, 0)),
        out_shape=jax.ShapeDtypeStruct((rows, cols), parts.dtype),
        compiler_params=pltpu.CompilerParams(dimension_semantics=("parallel",), vmem_limit_bytes=VMEM_LIMIT),
    )(parts)


def _row_tile(rows):
    return _tile(rows, 512, 16)


def _row_spec(rows, cols, block=0):
    return pl.BlockSpec((_row_tile(rows), cols), lambda i: (i, block))


def _rows_params(accumulates):
    return pltpu.CompilerParams(dimension_semantics=("arbitrary" if accumulates else "parallel",),
                                vmem_limit_bytes=VMEM_LIMIT)


def _swiglu_fwd(gu, name):
    rows = gu.shape[0]

    def body(g_ref, u_ref, o_ref):
        o_ref[...] = (_silu(g_ref[...].astype(F32)) * u_ref[...].astype(F32)).astype(o_ref.dtype)

    return pl.pallas_call(
        body, name=name, grid=(rows // _row_tile(rows),),
        in_specs=[_row_spec(rows, FFN_HIDDEN, 0), _row_spec(rows, FFN_HIDDEN, 1)], out_specs=_row_spec(rows, FFN_HIDDEN),
        out_shape=jax.ShapeDtypeStruct((rows, FFN_HIDDEN), BF16), compiler_params=_rows_params(False))(gu, gu)


def _swiglu_bwd(gu, d_act, name):
    rows = gu.shape[0]

    def body(g_ref, u_ref, d_ref, o_ref):
        g, u, d = g_ref[...].astype(F32), u_ref[...].astype(F32), d_ref[...].astype(F32)
        sig = 1.0 / (1.0 + jnp.exp(-g))
        o_ref[:, :FFN_HIDDEN] = (d * u * sig * (1.0 + g * (1.0 - sig))).astype(o_ref.dtype)
        o_ref[:, FFN_HIDDEN:] = (d * g * sig).astype(o_ref.dtype)

    return pl.pallas_call(
        body, name=name, grid=(rows // _row_tile(rows),),
        in_specs=[_row_spec(rows, FFN_HIDDEN, 0), _row_spec(rows, FFN_HIDDEN, 1), _row_spec(rows, FFN_HIDDEN)],
        out_specs=_row_spec(rows, 2 * FFN_HIDDEN),
        out_shape=jax.ShapeDtypeStruct((rows, 2 * FFN_HIDDEN), BF16), compiler_params=_rows_params(False))(gu, gu, d_act)


def _add_norm_fwd(h, y, gain, name):
    rows = h.shape[0]

    def body(*refs):
        if y is None:
            h_ref, g_ref, n_ref = refs
            x = h_ref[...]
        else:
            h_ref, y_ref, g_ref, s_ref, n_ref = refs
            x = h_ref[...] + y_ref[...]
            s_ref[...] = x
        n_ref[...] = (x * lax.rsqrt(jnp.mean(x * x, axis=-1, keepdims=True) + EPS) * g_ref[...]).astype(n_ref.dtype)

    row = _row_spec(rows, D_MODEL)
    ins = [h] if y is None else [h, y]
    out_shape = [jax.ShapeDtypeStruct((rows, D_MODEL), BF16)]
    if y is not None:
        out_shape = [jax.ShapeDtypeStruct((rows, D_MODEL), F32)] + out_shape
    res = pl.pallas_call(
        body, name=name, grid=(rows // _row_tile(rows),),
        in_specs=[row] * len(ins) + [_full_spec((1, D_MODEL))], out_specs=[row] * len(out_shape), out_shape=out_shape,
        compiler_params=_rows_params(False))(*ins, gain[None])
    return (h, res[0]) if y is None else (res[0], res[1])


def _norm_bwd(x, gain, d_n, d_skip, name):
    rows = x.shape[0]

    def body(x_ref, g_ref, dn_ref, ds_ref, dx_ref, dg_ref):
        @pl.when(pl.program_id(0) == 0)
        def _():
            dg_ref[...] = jnp.zeros_like(dg_ref)

        x, dn = x_ref[...], dn_ref[...].astype(F32)
        r = lax.rsqrt(jnp.mean(x * x, axis=-1, keepdims=True) + EPS)
        gd = g_ref[...] * dn
        dx_ref[...] = r * gd - x * (r * r * r) * jnp.mean(x * gd, axis=-1, keepdims=True) + ds_ref[...]
        dg_ref[...] += jnp.sum(x * r * dn, axis=0, keepdims=True)

    row = _row_spec(rows, D_MODEL)
    dx, dg = pl.pallas_call(
        body, name=name, grid=(rows // _row_tile(rows),),
        in_specs=[row, _full_spec((1, D_MODEL)), row, row], out_specs=[row, _full_spec((1, D_MODEL))],
        out_shape=[jax.ShapeDtypeStruct((rows, D_MODEL), F32), jax.ShapeDtypeStruct((1, D_MODEL), F32)],
        compiler_params=_rows_params(True))(x, gain[None], d_n, d_skip)
    return dx, dg[0]


def _loss_head(h, gain, target, name):
    rows = h.shape[0]

    def body(x_ref, g_ref, t_ref, loss_ref, dx_ref, dg_ref):
        @pl.when(pl.program_id(0) == 0)
        def _():
            loss_ref[...] = jnp.zeros_like(loss_ref)
            dg_ref[...] = jnp.zeros_like(dg_ref)

        x = x_ref[...]
        r = lax.rsqrt(jnp.mean(x * x, axis=-1, keepdims=True) + EPS)
        err = x * r * g_ref[...] - t_ref[...]
        loss_ref[...] += 0.5 * jnp.sum(jnp.mean(err * err, axis=-1, keepdims=True), axis=0, keepdims=True)
        dy = err * (1.0 / D_MODEL)
        gd = g_ref[...] * dy
        dx_ref[...] = r * gd - x * (r * r * r) * jnp.mean(x * gd, axis=-1, keepdims=True)
        dg_ref[...] += jnp.sum(x * r * dy, axis=0, keepdims=True)

    row = _row_spec(rows, D_MODEL)
    loss, dx, dg = pl.pallas_call(
        body, name=name, grid=(rows // _row_tile(rows),),
        in_specs=[row, _full_spec((1, D_MODEL)), row], out_specs=[_full_spec((1, 1)), row, _full_spec((1, D_MODEL))],
        out_shape=[jax.ShapeDtypeStruct((1, 1), F32), jax.ShapeDtypeStruct((rows, D_MODEL), F32),
                   jax.ShapeDtypeStruct((1, D_MODEL), F32)],
        compiler_params=_rows_params(True))(h, gain[None], target)
    return loss[0, 0], dx, dg[0]


def _join_cols(blocks, n_out, name):
    _, layers, rows, n = blocks.shape
    tr = _tile(rows, 256, 16)

    def body(x_ref, o_ref):
        for d in range(N_DEV):
            o_ref[:, d * n:(d + 1) * n] = x_ref[d]
        if n_out > N_DEV * n:
            o_ref[:, N_DEV * n:] = jnp.zeros((tr, n_out - N_DEV * n), o_ref.dtype)

    return pl.pallas_call(
        body, name=name, grid=(layers, rows // tr),
        in_specs=[pl.BlockSpec((N_DEV, None, tr, n), lambda l, i: (0, l, i, 0))],
        out_specs=pl.BlockSpec((None, tr, n_out), lambda l, i: (l, i, 0)),
        out_shape=jax.ShapeDtypeStruct((layers, rows, n_out), blocks.dtype),
        compiler_params=pltpu.CompilerParams(dimension_semantics=("parallel", "parallel"), vmem_limit_bytes=VMEM_LIMIT),
    )(blocks)


def _split_cols(full, n, name):
    rows = full.shape[0]
    tr = _tile(rows, 256, 16)

    def body(x_ref, o_ref):
        for d in range(N_DEV):
            o_ref[d] = x_ref[:, d * n:(d + 1) * n]

    return pl.pallas_call(
        body, name=name, grid=(rows // tr,),
        in_specs=[pl.BlockSpec((tr, full.shape[1]), lambda i: (i, 0))],
        out_specs=pl.BlockSpec((N_DEV, tr, n), lambda i: (0, i, 0)),
        out_shape=jax.ShapeDtypeStruct((N_DEV, rows, n), full.dtype),
        compiler_params=pltpu.CompilerParams(dimension_semantics=("parallel",), vmem_limit_bytes=VMEM_LIMIT),
    )(full)


def _pack(arrays):
    flat = jnp.concatenate([a.reshape(-1) for a in arrays])
    unit = FLAT_COLS * FLAT_ROWS_ALIGN
    padded = -(-flat.shape[0] // unit) * unit
    return jnp.pad(flat, (0, padded - flat.shape[0])).reshape(-1, FLAT_COLS)


def _unpack(flat, shapes, lead=()):
    flat = flat.reshape(lead + (-1,))
    out, off = [], 0
    for shape in shapes:
        n = math.prod(shape)
        out.append(flat[..., off:off + n].reshape(lead + tuple(shape)))
        off += n
    return out


def _join(blocks, axis):
    moved = jnp.moveaxis(blocks, 0, axis)
    shape = list(moved.shape)
    shape[axis:axis + 2] = [shape[axis] * shape[axis + 1]]
    return moved.reshape(shape)


def _own_shard(full, axis, position):
    n = full.shape[axis] // N_DEV
    return lax.dynamic_slice_in_dim(full, position * n, n, axis)


def _ssd_conv(xbc, conv_w, conv_b):
    seq = xbc.shape[0]
    padded = jnp.pad(xbc, ((SSD_CONV - 1, 0), (0, 0)))
    out = conv_b
    for k in range(SSD_CONV):
        out = out + padded[k:k + seq] * conv_w[k]
    return jax.nn.silu(out)


def _s5_operators(log_dt, a_re, a_im, b_re, b_im, c_re, c_im):
    t = S5_CHUNK
    hi = lax.Precision.HIGHEST
    step = jnp.exp(log_dt)[:, None]
    mag = jnp.exp(step * a_re)
    abar_re = mag * jnp.cos(step * a_im)
    abar_im = mag * jnp.sin(step * a_im)
    den = a_re * a_re + a_im * a_im
    f_re = ((abar_re - 1.0) * a_re + abar_im * a_im) / den
    f_im = (abar_im * a_re - (abar_re - 1.0) * a_im) / den
    bb_re = f_re[..., None] * b_re - f_im[..., None] * b_im
    bb_im = f_re[..., None] * b_im + f_im[..., None] * b_re
    j = jnp.arange(t + 1, dtype=F32)[:, None, None]
    pmag = jnp.exp(j * (step * a_re))
    pw_re = pmag * jnp.cos(j * (step * a_im))
    pw_im = pmag * jnp.sin(j * (step * a_im))
    cl_re = c_re[None] * pw_re[:t, :, None, :] - c_im[None] * pw_im[:t, :, None, :]
    cl_im = c_re[None] * pw_im[:t, :, None, :] + c_im[None] * pw_re[:t, :, None, :]
    kern = (jnp.einsum('jgcp,gpk->jgck', cl_re, bb_re, precision=hi)
            - jnp.einsum('jgcp,gpk->jgck', cl_im, bb_im, precision=hi))
    rp_re, rp_im = pw_re[:t][::-1], pw_im[:t][::-1]
    wz_re = rp_re[:, :, :, None] * bb_re[None] - rp_im[:, :, :, None] * bb_im[None]
    wz_im = rp_re[:, :, :, None] * bb_im[None] + rp_im[:, :, :, None] * bb_re[None]
    w_z = jnp.concatenate([wz_re, wz_im], axis=2).transpose(1, 0, 3, 2).reshape(S5_GROUPS, t * S5_GROUP, 2 * S5_STATE)
    cy_re = c_re[None] * pw_re[1:, :, None, :] - c_im[None] * pw_im[1:, :, None, :]
    cy_im = c_re[None] * pw_im[1:, :, None, :] + c_im[None] * pw_re[1:, :, None, :]
    w_y = jnp.concatenate([cy_re, -cy_im], axis=3).transpose(1, 3, 0, 2).reshape(S5_GROUPS, 2 * S5_STATE, t * S5_GROUP)
    return kern, w_z, w_y, pw_re[t], pw_im[t]


def _s5_lag_selector():
    t = S5_CHUNK
    lag = jnp.arange(t)[:, None] - jnp.arange(t)[None, :]
    return (lag[:, :, None] == jnp.arange(t)[None, None, :]).astype(F32).reshape(t * t, t)


def _s5_toeplitz(kern, tag):
    t = S5_CHUNK
    sel = _s5_lag_selector()
    flat = _matmul(sel, kern.reshape(t, -1), out_dtype=BF16, name=tag + "_toeplitz")
    toep = flat.reshape(t, t, S5_GROUPS, S5_GROUP, S5_GROUP).transpose(2, 1, 4, 0, 3)
    toep = toep.reshape(S5_GROUPS, t * S5_GROUP, t * S5_GROUP)

    def backward(d_toep):
        d_flat = d_toep.reshape(S5_GROUPS, t, S5_GROUP, t, S5_GROUP).transpose(3, 1, 0, 4, 2).reshape(t * t, -1)
        return _matmul(sel, d_flat, ta=True, name=tag + "_toeplitz_dw").reshape(kern.shape)

    return toep, backward


def _s5_gate(y, u, d_skip):
    return jax.nn.gelu(y + d_skip * u)


def _glu(vg):
    return vg[:, :D_MODEL] * jax.nn.sigmoid(vg[:, D_MODEL:])


_BIG = [("gla_w_in", 2), ("gla_w_out", 1), ("ssd_w_in", 2), ("ssd_w_out", 1), ("s5_w_glu", 2), ("ffn_w_gu", 2),
        ("ffn_w_down", 1)]
_PADDED_COLS = {"gla_w_in": GLA_PROJ, "ssd_w_in": SSD_PROJ}


class _Traffic:
    RIDERS = 2

    def __init__(self, shards, plan):
        self.shards, self.plan = shards, plan
        self.position = 0
        self.queue = []
        self.weights, self.received = {}, {}
        self.standalone = 0

    def _request(self, key):
        self.queue.append(("gather", self.shards[key], lambda blocks: self.weights.__setitem__(key, self._assemble(key, blocks))))

    @staticmethod
    def _assemble(key, blocks):
        name, layer = key
        if dict(_BIG)[name] == 1:
            return blocks.reshape((N_DEV * blocks.shape[1], blocks.shape[2]))
        n_out = _PADDED_COLS.get(name, N_DEV * blocks.shape[2])
        return _join_cols(blocks[:, None], n_out, f"join_{name}_{layer}")[0]

    def take(self, key):
        if self.position == 0:
            self._request(self.plan[0])
            self.flush()
        assert key == self.plan[self.position], (key, self.plan[self.position])
        self.position += 1
        if self.position < len(self.plan):
            self._request(self.plan[self.position])
        return self.weights[key]

    def matmul(self, a, b, **kw):
        riders, self.queue = self.queue[:self.RIDERS], self.queue[self.RIDERS:]
        if not riders:
            return _matmul(a, b, **kw)
        out, moved = _matmul(a, b, cargo=[(kind, x) for kind, x, _ in riders], **kw)
        for (_, _, deliver), y in zip(riders, moved):
            deliver(y)
        return out

    def send_gradient(self, key, dw):
        name, layer = key
        shard = self.shards[key]
        if dict(_BIG)[name] == 1:
            blocks = dw.reshape((N_DEV,) + shard.shape)
        else:
            blocks = _split_cols(dw, shard.shape[1], f"split_{name}_{layer}")
        self.queue.append(("exchange", blocks, lambda parts: self.received.__setitem__(key, parts)))

    def flush(self):
        for kind, x, deliver in self.queue:
            deliver(_collective(kind, x, f"{kind}_alone_{self.standalone}"))
            self.standalone += 1
        self.queue = []


def _linear(x, w, tag, out_dtype=F32, dx_dtype=F32):
    traffic, key = w
    weight = traffic.take(key)
    y = traffic.matmul(x, weight, out_dtype=out_dtype, name=tag + "_fwd")

    def backward(dy):
        dx = traffic.matmul(dy, weight, tb=True, out_dtype=dx_dtype, name=tag + "_dx")
        traffic.send_gradient(key, traffic.matmul(x, dy, ta=True, out_dtype=BF16, name=tag + "_dw"))
        return dx

    return y, backward


def _gla_mixer(hn, p, tag):
    w_a2 = jnp.pad(p["w_a2"], ((0, LANES - GLA_RANK), (0, 0)))
    b_a, norm_g = p["b_a"][None], p["norm_g"][None]
    proj, lin_in = _linear(hn, p["w_in"], tag + "_in")
    o, sprev = _gla_core_fwd(proj, w_a2, b_a, norm_g)
    y, lin_out = _linear(o, p["w_out"], tag + "_out")

    def backward(dy):
        d_proj, d_wa, d_ba, d_ng = _gla_core_bwd(proj, sprev, lin_out(dy), w_a2, b_a, norm_g)
        return lin_in(d_proj), dict(w_a2=d_wa[:GLA_RANK], b_a=d_ba[0], norm_g=d_ng[0])

    return y, backward


def _ssd_mixer(hn, p, tag):
    pad = lambda a: jnp.pad(a[None], ((0, 0), (0, LANES - SSD_HEADS)))
    dt_bias, a_log, d_skip, norm_g = pad(p["dt_bias"]), pad(p["a_log"]), pad(p["d"]), p["norm_g"][None]
    proj, lin_in = _linear(hn, p["w_in"], tag + "_in")
    xbc, conv_vjp = jax.vjp(_ssd_conv, proj[:, SSD_DINNER:SSD_DINNER + SSD_XBC], p["conv_w"], p["conv_b"])
    o, hprev = _ssd_core_fwd(proj, xbc, dt_bias, a_log, d_skip, norm_g)
    y, lin_out = _linear(o, p["w_out"], tag + "_out")

    def backward(dy):
        d_z, d_xbc, d_dt, d_db, d_al, d_ds, d_ng = _ssd_core_bwd(proj, xbc, hprev, lin_out(dy), dt_bias, a_log, d_skip, norm_g)
        d_pre, d_cw, d_cb = conv_vjp(d_xbc)
        d_hn = lin_in(jnp.concatenate([d_z, d_pre, d_dt], axis=1))
        return d_hn, dict(conv_w=d_cw, conv_b=d_cb, dt_bias=d_db[0, :SSD_HEADS], a_log=d_al[0, :SSD_HEADS],
                          d=d_ds[0, :SSD_HEADS], norm_g=d_ng[0])

    return y, backward


def _s5_mixer(hn, p, tag):
    seq = hn.shape[0]
    t, n_chunks = S5_CHUNK, hn.shape[0] // S5_CHUNK
    names = ("log_dt", "a_re", "a_im", "b_re", "b_im", "c_re", "c_im")
    (kern, w_z, w_y, lam_re, lam_im), ops_vjp = jax.vjp(_s5_operators, *[p[k] for k in names])
    toep, toep_bwd = _s5_toeplitz(kern, tag)
    to_groups = lambda a: a.reshape(n_chunks, t, S5_GROUPS, S5_GROUP).transpose(2, 0, 1, 3).reshape(S5_GROUPS, n_chunks, t * S5_GROUP)
    from_groups = lambda a: a.reshape(S5_GROUPS, n_chunks, t, S5_GROUP).transpose(1, 2, 0, 3).reshape(seq, D_MODEL)
    ug = to_groups(hn)
    z = _matmul(ug, w_z, name=tag + "_z")
    z_re, z_im = z[..., :S5_STATE].transpose(1, 0, 2), z[..., S5_STATE:].transpose(1, 0, 2)
    x_re, x_im = _s5_boundary_scan(z_re, z_im, lam_re, lam_im)
    xprev = jnp.concatenate([x_re, x_im], axis=2).transpose(1, 0, 2)
    yg = _matmul(ug, toep, name=tag + "_intra") + _matmul(xprev, w_y, name=tag + "_inter")
    act, gate_vjp = jax.vjp(_s5_gate, from_groups(yg), hn, p["d"])
    vg, lin_glu = _linear(act, p["w_glu"], tag + "_glu")
    out, glu_vjp = jax.vjp(_glu, vg)

    def backward(dy):
        d_vg, = glu_vjp(dy)
        d_act = lin_glu(d_vg)
        d_y, d_hn, d_d = gate_vjp(d_act)
        d_yg = to_groups(d_y)
        d_ug = _matmul(d_yg, toep, tb=True, name=tag + "_intra_dx")
        d_toep = _matmul(ug, d_yg, ta=True, out_dtype=BF16, name=tag + "_intra_dw")
        d_xprev = _matmul(d_yg, w_y, tb=True, name=tag + "_inter_dx").transpose(1, 0, 2)
        d_wy = _matmul(xprev, d_yg, ta=True, name=tag + "_inter_dw")
        dz_re, dz_im = _s5_boundary_scan(d_xprev[::-1, :, :S5_STATE], d_xprev[::-1, :, S5_STATE:], lam_re, -lam_im)
        dz_re, dz_im = dz_re[::-1], dz_im[::-1]
        d_lam_re = jnp.sum(x_re * dz_re + x_im * dz_im, axis=0)
        d_lam_im = jnp.sum(x_re * dz_im - x_im * dz_re, axis=0)
        d_z = jnp.concatenate([dz_re, dz_im], axis=2).transpose(1, 0, 2)
        d_ug = d_ug + _matmul(d_z, w_z, tb=True, name=tag + "_z_dx")
        d_wz = _matmul(ug, d_z, ta=True, name=tag + "_z_dw")
        grads = dict(zip(names, ops_vjp((toep_bwd(d_toep), d_wz, d_wy, d_lam_re, d_lam_im))))
        grads.update(d=d_d)
        return d_hn + from_groups(d_ug), grads

    return out, backward


_SMALL =[("gla_w_a2", 2), ("gla_b_a", 1), ("gla_norm_g", 1), ("ssd_conv_w", 2), ("s5_d", 1)]
_REPLICATED = ["norm_mix_g", "norm_ffn_g", "ssd_conv_b", "ssd_dt_bias", "ssd_a_log", "ssd_d", "ssd_norm_g", "s5_log_dt",
               "s5_a_re", "s5_a_im", "s5_b_re", "s5_b_im", "s5_c_re", "s5_c_im", "final_norm_g"]
_WEIGHTS = ['norm_mix_g', 'norm_ffn_g', 'gla_w_in', 'gla_w_a2', 'gla_b_a', 'gla_norm_g', 'gla_w_out', 'ssd_w_in',
            'ssd_conv_w', 'ssd_conv_b', 'ssd_dt_bias', 'ssd_a_log', 'ssd_d', 'ssd_norm_g', 'ssd_w_out', 's5_log_dt',
            's5_a_re', 's5_a_im', 's5_b_re', 's5_b_im', 's5_c_re', 's5_c_im', 's5_d', 's5_w_glu', 'ffn_w_gu', 'ffn_w_down',
            'final_norm_g']


def _gather_small(local):
    shapes = [local[n].shape for n, _ in _SMALL]
    blocks = _collective("gather", _pack([local[n] for n, _ in _SMALL]), "gather_vectors")
    parts = _unpack(blocks, shapes, lead=(N_DEV,))
    return {n: _join(part, axis) for (n, axis), part in zip(_SMALL, parts)}


def _forward_plan():
    plan = []
    for i in range(DEPTH):
        j = i // 3
        plan += [[("gla_w_in", j), ("gla_w_out", j)], [("ssd_w_in", j), ("ssd_w_out", j)], [("s5_w_glu", j)]][i % 3]
        plan += [("ffn_w_gu", i), ("ffn_w_down", i)]
    return plan


def _forward_backward(x, target, w, traffic):
    big = lambda name, j: (traffic, (name, j))
    gla = lambda j: dict(w_in=big("gla_w_in", j), w_a2=w["gla_w_a2"][j], b_a=w["gla_b_a"][j], norm_g=w["gla_norm_g"][j],
                         w_out=big("gla_w_out", j))
    ssd = lambda j: dict(w_in=big("ssd_w_in", j), conv_w=w["ssd_conv_w"][j], conv_b=w["ssd_conv_b"][j],
                         dt_bias=w["ssd_dt_bias"][j], a_log=w["ssd_a_log"][j], d=w["ssd_d"][j], norm_g=w["ssd_norm_g"][j],
                         w_out=big("ssd_w_out", j))
    s5 = lambda j: dict(log_dt=w["s5_log_dt"][j], a_re=w["s5_a_re"][j], a_im=w["s5_a_im"][j], b_re=w["s5_b_re"][j],
                        b_im=w["s5_b_im"][j], c_re=w["s5_c_re"][j], c_im=w["s5_c_im"][j], d=w["s5_d"][j],
                        w_glu=big("s5_w_glu", j))
    mixers = [("gla", _gla_mixer, gla), ("ssd", _ssd_mixer, ssd), ("s5", _s5_mixer, s5)]
    base, delta = x, None
    tape = []
    for i in range(DEPTH):
        kind, mixer, params = mixers[i % 3]
        j = i // 3
        h, hn = _add_norm_fwd(base, delta, w["norm_mix_g"][i], f"l{i}_norm_mix")
        y, mixer_bwd = mixer(hn, params(j), f"l{i}_{kind}")
        h_mid, hn2 = _add_norm_fwd(h, y, w["norm_ffn_g"][i], f"l{i}_norm_ffn")
        gu, gu_bwd = _linear(hn2, big("ffn_w_gu", i), f"l{i}_ffn_gu", out_dtype=BF16)
        act = _swiglu_fwd(gu, f"l{i}_swiglu")
        delta, down_bwd = _linear(act, big("ffn_w_down", i), f"l{i}_ffn_down", dx_dtype=BF16)
        base = h_mid
        tape.append((kind, j, h, mixer_bwd, h_mid, gu_bwd, gu, down_bwd))
    loss, d_h, d_final_g = _loss_head(base + delta, w["final_norm_g"], target, "loss_head")

    grads = {n: [None] * w[n].shape[0] for n in w if n != "final_norm_g"}
    grads["final_norm_g"] = d_final_g
    for i in reversed(range(DEPTH)):
        kind, j, h, mixer_bwd, h_mid, gu_bwd, gu, down_bwd = tape[i]
        d_gu = _swiglu_bwd(gu, down_bwd(d_h), f"l{i}_swiglu_bwd")
        d_mid, grads["norm_ffn_g"][i] = _norm_bwd(h_mid, w["norm_ffn_g"][i], gu_bwd(d_gu), d_h, f"l{i}_norm_ffn_bwd")
        d_hn, mixer_grads = mixer_bwd(d_mid)
        for k, g in mixer_grads.items():
            grads[f"{kind}_{k}"][j] = g
        d_h, grads["norm_mix_g"][i] = _norm_bwd(h, w["norm_mix_g"][i], d_hn, d_mid, f"l{i}_norm_mix_bwd")
    return loss, d_h, grads


def kernel(x, norm_mix_g, norm_ffn_g, gla_w_in, gla_w_a2, gla_b_a, gla_norm_g, gla_w_out, ssd_w_in, ssd_conv_w, ssd_conv_b, ssd_dt_bias, ssd_a_log, ssd_d, ssd_norm_g, ssd_w_out, s5_log_dt, s5_a_re, s5_a_im, s5_b_re, s5_b_im, s5_c_re, s5_c_im, s5_d, s5_w_glu, ffn_w_gu, ffn_w_down, final_norm_g, loss_target, m_norm_mix_g, m_norm_ffn_g, m_gla_w_in, m_gla_w_a2, m_gla_b_a, m_gla_norm_g, m_gla_w_out, m_ssd_w_in, m_ssd_conv_w, m_ssd_conv_b, m_ssd_dt_bias, m_ssd_a_log, m_ssd_d, m_ssd_norm_g, m_ssd_w_out, m_s5_log_dt, m_s5_a_re, m_s5_a_im, m_s5_b_re, m_s5_b_im, m_s5_c_re, m_s5_c_im, m_s5_d, m_s5_w_glu, m_ffn_w_gu, m_ffn_w_down, m_final_norm_g, v_norm_mix_g, v_norm_ffn_g, v_gla_w_in, v_gla_w_a2, v_gla_b_a, v_gla_norm_g, v_gla_w_out, v_ssd_w_in, v_ssd_conv_w, v_ssd_conv_b, v_ssd_dt_bias, v_ssd_a_log, v_ssd_d, v_ssd_norm_g, v_ssd_w_out, v_s5_log_dt, v_s5_a_re, v_s5_a_im, v_s5_b_re, v_s5_b_im, v_s5_c_re, v_s5_c_im, v_s5_d, v_s5_w_glu, v_ffn_w_gu, v_ffn_w_down, v_final_norm_g):
    args = locals()
    local = {n: args[n] for n in _WEIGHTS}
    moment_m = {n: args["m_" + n] for n in _WEIGHTS}
    moment_v = {n: args["v_" + n] for n in _WEIGHTS}

    shards = {(n, layer): local[n][layer].astype(BF16) for n, _ in _BIG for layer in range(local[n].shape[0])}
    traffic = _Traffic(shards, _forward_plan())
    full = {n: local[n] for n in _REPLICATED}
    full.update(_gather_small(local))

    loss, d_x, grads = _forward_backward(x[0], loss_target[0], full, traffic)
    traffic.flush()
    loss = lax.psum(loss, ("x", "y", "c"))
    kinds = ("grad", "delta", "new_m", "new_v")
    out = {}

    for n, _ in _BIG:
        parts = jnp.stack([traffic.received[(n, layer)] for layer in range(local[n].shape[0])], axis=1)
        results = _adamw(parts, local[n], moment_m[n], moment_v[n], "adamw_" + n)
        out.update({f"{kind}_{n}": a for kind, a in zip(kinds, results)})

    small = [n for n, _ in _SMALL] + _REPLICATED
    stacked = lambda n: grads[n] if n == "final_norm_g" else jnp.stack(grads[n])
    parts = _collective("gather", _pack([stacked(n) for n in small]), "gather_small_gradients")
    summed = _unpack(_sum_parts(parts, "sum_small_gradients"), [stacked(n).shape for n in small])
    position = _index(_mesh_position())
    mine = [_own_shard(g, axis, position) for g, (_, axis) in zip(summed, _SMALL)] + summed[len(_SMALL):]
    shapes = [local[n].shape for n in small]
    pk = lambda arrays: _pack(arrays)[None]
    results = _adamw(pk(mine)[None], pk([local[n] for n in small]), pk([moment_m[n] for n in small]),
                     pk([moment_v[n] for n in small]), "adamw_small")
    for kind, flat in zip(kinds, results):
        out.update({f"{kind}_{n}": a for n, a in zip(small, _unpack(flat[0], shapes))})

    return (loss, d_x[None], *[out[f"{kind}_{n}"] for kind in ("grad", "delta", "new_m", "new_v") for n in _WEIGHTS])
```

```python
import functools
import math

import jax
import jax.numpy as jnp
import numpy as np
from jax import lax
from jax.experimental import pallas as pl
from jax.experimental.pallas import tpu as pltpu

F32 = jnp.float32
BF16 = jnp.bfloat16
_MXU_DTYPE = jnp.bfloat16

N_DEV = 8
D_MODEL = 1024
DEPTH = 4
CHUNK = 64
STEP_CHUNKS = 1
STEP = CHUNK * STEP_CHUNKS
EPS = 1e-6
GLA_HEADS, GLA_DK, GLA_DV, GLA_RANK, GLA_TAU = 4, 128, 256, 16, 16.0
GLA_QK = GLA_HEADS * GLA_DK
GLA_VD = GLA_HEADS * GLA_DV
LANES = 128
GLA_IN = 2 * GLA_QK + 2 * GLA_VD + GLA_RANK
GLA_PROJ = 2 * GLA_QK + 2 * GLA_VD + LANES
SSD_DINNER, SSD_HEADDIM, SSD_HEADS, SSD_GROUPS, SSD_HPG, SSD_DSTATE, SSD_CONV = 2048, 64, 32, 8, 4, 128, 4
SSD_GN = SSD_GROUPS * SSD_DSTATE
SSD_GW = SSD_HPG * SSD_HEADDIM
SSD_XBC = SSD_DINNER + 2 * SSD_GN
SSD_IN = SSD_DINNER + SSD_XBC + SSD_HEADS
SSD_PROJ = SSD_DINNER + SSD_XBC + LANES
S5_GROUP, S5_GROUPS, S5_STATE = 16, 64, 64
S5_CHUNK = 16
FFN_HIDDEN = 2816
ADAM_LR, ADAM_B1, ADAM_B2, ADAM_EPS, ADAM_WD, ADAM_STEP = 0.001, 0.9, 0.999, 1e-08, 0.01, 10
VMEM_LIMIT = 48 * 1024 * 1024
FLAT_COLS = 1024
FLAT_ROWS_ALIGN = 64


def _tile(n, cap, unit):
    if n <= cap:
        return n
    best = None
    for t in range(unit, cap + 1, unit):
        if n % t == 0:
            best = t
    assert best is not None, (n, cap, unit)
    return best


def _divisors(n, unit):
    return sorted({t for t in range(unit, n + 1, unit) if n % t == 0} | {n})


_MXU_FLOPS, _HBM_BYTES, _ACC_BYTES, _STEP_SECONDS = 1.1e15, 3e12, 1.1e13, 3.5e-7
_MXU_ROWS = 256
_TILE_VMEM_BUDGET = 36 * 1024 * 1024


def _pick_tiles(m, n, k, a_bytes, b_bytes, o_bytes, m_unit):
    best = None
    for tm in _divisors(m, m_unit):
        for tn in _divisors(n, LANES):
            for tk in _divisors(k, LANES):
                nk = k // tk
                vmem = 2 * tm * tk * a_bytes + 2 * tk * tn * b_bytes + 2 * tm * tn * o_bytes + (nk > 1) * tm * tn * 4
                if vmem > _TILE_VMEM_BUDGET or tm > 2048 or tn > 2048:
                    continue
                a_reads = n // tn if nk > 1 else 1
                b_reads = 1 if (nk == 1 and n == tn) else m // tm
                traffic = m * k * a_bytes * a_reads + k * n * b_bytes * b_reads + m * n * o_bytes
                mxu = 2.0 * m * n * k / _MXU_FLOPS * (1.0 + _MXU_ROWS / tm)
                cost = (max(mxu, traffic / _HBM_BYTES) + (nk > 1) * nk * m * n * 8 / _ACC_BYTES
                        + (m // tm) * (n // tn) * nk * _STEP_SECONDS)
                if best is None or cost < best[0]:
                    best = (cost, tm, tn, tk)
    assert best is not None, (m, n, k)
    return best[1:]


def _cargo_call(body, cargo, *, name, grid, in_specs, out_specs, out_shape, scratch_shapes, semantics):
    params = lambda sem: pltpu.CompilerParams(dimension_semantics=sem, vmem_limit_bytes=VMEM_LIMIT)
    if not cargo:
        return pl.pallas_call(body, name=name, grid=grid, in_specs=in_specs, out_specs=out_specs, out_shape=out_shape,
                              scratch_shapes=scratch_shapes, compiler_params=params(semantics))
    n_in, n_out, n_scratch, n_cargo = len(in_specs), len(out_specs), len(scratch_shapes), len(cargo)

    def loaded(*refs):
        ins, cargo_in, rest = refs[:n_in], refs[n_in:n_in + n_cargo], refs[n_in + n_cargo:]
        outs, cargo_out, rest = rest[:n_out], rest[n_out:n_out + n_cargo], rest[n_out + n_cargo:]
        scratch, sems = rest[:n_scratch], rest[n_scratch:]
        ids = [pl.program_id(d) for d in range(len(grid))]
        first = functools.reduce(jnp.logical_and, [i == 0 for i in ids])
        last = functools.reduce(jnp.logical_and, [i == g - 1 for i, g in zip(ids, grid)])
        moves = lambda: [_moves(kind, x_ref, y_ref, *sems[3 * c:3 * c + 3])
                         for c, ((kind, _), x_ref, y_ref) in enumerate(zip(cargo, cargo_in, cargo_out))]

        @pl.when(first)
        def _():
            for mv in moves():
                _start(mv)

        body(*ins, *outs, *scratch)

        @pl.when(last)
        def _():
            for mv in moves():
                _finish(mv)

    sems = [pltpu.SemaphoreType.DMA((N_DEV - 1,)), pltpu.SemaphoreType.DMA((N_DEV - 1,)), pltpu.SemaphoreType.DMA] * n_cargo
    call = pl.pallas_call(
        loaded, name=name, grid=grid, in_specs=list(in_specs) + [_ANY] * n_cargo,
        out_specs=list(out_specs) + [_ANY] * n_cargo,
        out_shape=list(out_shape) + [_moved_shape(kind, x) for kind, x in cargo],
        scratch_shapes=list(scratch_shapes) + sems, compiler_params=params(("arbitrary",) * len(grid)))

    def run(*args):
        results = call(*args, *[x for _, x in cargo])
        return list(results[:n_out]), list(results[n_out:])

    return run


def _matmul(a, b, *, ta=False, tb=False, out_dtype=F32, name, cargo=()):
    batched = a.ndim == 3
    if ta:
        k_dim, m_dim = a.shape[-2:]
    else:
        m_dim, k_dim = a.shape[-2:]
    if tb:
        n_dim, kb = b.shape[-2:]
    else:
        kb, n_dim = b.shape[-2:]
    assert kb == k_dim, (a.shape, b.shape, ta, tb)
    tm, tn, tk = _pick_tiles(m_dim, n_dim, k_dim, a.dtype.itemsize, b.dtype.itemsize, jnp.dtype(out_dtype).itemsize,
                             LANES if ta else 16)
    nk = k_dim // tk
    ca, cb = (0 if ta else 1), (1 if tb else 0)
    grid = (m_dim // tm, n_dim // tn, nk)
    if batched:
        grid = (a.shape[0],) + grid

    def body(a_ref, b_ref, o_ref, *acc):
        part = lax.dot_general(a_ref[...].astype(_MXU_DTYPE), b_ref[...].astype(_MXU_DTYPE),
                               (((ca,), (cb,)), ((), ())), preferred_element_type=F32)
        if nk == 1:
            o_ref[...] = part.astype(o_ref.dtype)
            return
        acc_ref, = acc
        k = pl.program_id(len(grid) - 1)

        @pl.when(k == 0)
        def _():
            acc_ref[...] = part

        @pl.when(k > 0)
        def _():
            acc_ref[...] += part

        @pl.when(k == nk - 1)
        def _():
            o_ref[...] = acc_ref[...].astype(o_ref.dtype)

    lead = (None,) if batched else ()

    def spec(shape, fn):
        if batched:
            return pl.BlockSpec(lead + shape, lambda g, i, j, k: (g,) + fn(i, j, k))
        return pl.BlockSpec(shape, fn)

    a_spec = spec((tk, tm), lambda i, j, k: (k, i)) if ta else spec((tm, tk), lambda i, j, k: (i, k))
    b_spec = spec((tn, tk), lambda i, j, k: (j, k)) if tb else spec((tk, tn), lambda i, j, k: (k, j))
    o_spec = spec((tm, tn), lambda i, j, k: (i, j))
    out_shape = ((a.shape[0],) if batched else ()) + (m_dim, n_dim)
    call = _cargo_call(
        body, cargo, name=name, grid=grid, in_specs=[a_spec, b_spec], out_specs=[o_spec],
        out_shape=[jax.ShapeDtypeStruct(out_shape, out_dtype)], scratch_shapes=[pltpu.VMEM((tm, tn), F32)] if nk > 1 else [],
        semantics=("parallel",) * (len(grid) - 1) + ("arbitrary",))
    if not cargo:
        return call(a, b)[0]
    results, moved = call(a, b)
    return results[0], moved


def _dot(a, b, ca=1, cb=0, exact=False):
    if exact:
        return lax.dot_general(a, b, (((ca,), (cb,)), ((), ())), precision=lax.Precision.HIGHEST,
                               preferred_element_type=F32)
    return lax.dot_general(a.astype(_MXU_DTYPE), b.astype(_MXU_DTYPE), (((ca,), (cb,)), ((), ())),
                           preferred_element_type=F32)


def _tri(n):
    return lax.broadcasted_iota(jnp.int32, (n, n), 0) >= lax.broadcasted_iota(jnp.int32, (n, n), 1)


def _log_sigmoid(x):
    return jnp.minimum(x, 0.0) - jnp.log(1.0 + jnp.exp(-jnp.abs(x)))


def _softplus(x):
    return jnp.maximum(x, 0.0) + jnp.log(1.0 + jnp.exp(-jnp.abs(x)))


def _silu(x):
    return x / (1.0 + jnp.exp(-x))


def _full_spec(shape):
    return pl.BlockSpec(shape, lambda c: (0,) * len(shape))


def _gla_chunk(proj, st, w_a2, b_a, norm_g):
    t = proj.shape[0]
    q = proj[:, 0:GLA_QK] * (GLA_DK ** -0.5)
    k = proj[:, GLA_QK:2 * GLA_QK]
    v = proj[:, 2 * GLA_QK:2 * GLA_QK + GLA_VD]
    r = proj[:, 2 * GLA_QK + GLA_VD:2 * GLA_QK + 2 * GLA_VD]
    a_low = proj[:, 2 * GLA_QK + 2 * GLA_VD:]
    log_a = _log_sigmoid(_dot(a_low, w_a2) + b_a) * (1.0 / GLA_TAU)
    past = _tri(t)
    lc = _dot(past.astype(F32), log_a, exact=True)
    lend = lc[t - 1:t, :]
    e_pos = jnp.exp(lc)
    e_neg = jnp.exp(-lc)
    q_fwd, k_fwd, q_bwd, k_bwd = q * e_pos, k * e_neg, q * e_neg, k * e_pos
    kd = k * jnp.exp(lend - lc)
    g = jnp.exp(lend)
    outs, new_st = [], []
    for h in range(GLA_HEADS):
        sk = slice(h * GLA_DK, (h + 1) * GLA_DK)
        sv = slice(h * GLA_DV, (h + 1) * GLA_DV)
        s_past = _dot(q_fwd[:, sk], k_fwd[:, sk], 1, 1)
        s_future = _dot(q_bwd[:, sk], k_bwd[:, sk], 1, 1)
        scores = jnp.where(past, s_past, s_future)
        o = _dot(scores, v[:, sv]) + _dot(q_fwd[:, sk], st[h], 1, 1)
        new_st.append(st[h] * g[:, sk] + _dot(v[:, sv], kd[:, sk], 0, 0))
        o = o * lax.rsqrt(jnp.mean(o * o, axis=-1, keepdims=True) + EPS) * norm_g[:, sv]
        outs.append(o)
    return jnp.concatenate(outs, axis=1) * _silu(r), tuple(new_st)


_GLA_STATE = (GLA_HEADS, GLA_DV, GLA_DK)


def _gla_step(proj, st, w_a2, b_a, norm_g):
    outs = []
    for s in range(STEP_CHUNKS):
        out, st = _gla_chunk(proj[s * CHUNK:(s + 1) * CHUNK], st, w_a2, b_a, norm_g)
        outs.append(out)
    return jnp.concatenate(outs, axis=0), st


def _gla_core_fwd(proj, w_a2, b_a, norm_g, cargo=()):
    seq = proj.shape[0]
    nc = seq // STEP

    def body(proj_ref, wa_ref, ba_ref, ng_ref, o_ref, sprev_ref, st_ref):
        @pl.when(pl.program_id(0) == 0)
        def _():
            st_ref[...] = jnp.zeros_like(st_ref)

        st = tuple(st_ref[h] for h in range(GLA_HEADS))
        for h in range(GLA_HEADS):
            sprev_ref[0, h] = st[h]
        out, new_st = _gla_step(proj_ref[...], st, wa_ref[...], ba_ref[...], ng_ref[...])
        o_ref[...] = out
        for h in range(GLA_HEADS):
            st_ref[h] = new_st[h]

    return _cargo_call(
        body, cargo, name="gla_core_fwd", grid=(nc,),
        in_specs=[pl.BlockSpec((STEP,GLA_PROJ), lambda c: (c, 0)), _full_spec(w_a2.shape), _full_spec(b_a.shape),
                  _full_spec(norm_g.shape)],
        out_specs=[pl.BlockSpec((STEP,GLA_VD), lambda c: (c, 0)), pl.BlockSpec((1,) + _GLA_STATE, lambda c: (c, 0, 0, 0))],
        out_shape=[jax.ShapeDtypeStruct((seq, GLA_VD), F32), jax.ShapeDtypeStruct((nc,) + _GLA_STATE, F32)],
        scratch_shapes=[pltpu.VMEM(_GLA_STATE, F32)],
        semantics=("arbitrary",),
    )(proj, w_a2, b_a, norm_g)


def _gla_core_bwd(proj, sprev, d_out, w_a2, b_a, norm_g, cargo=()):
    seq = proj.shape[0]
    nc = seq // STEP

    def body(proj_ref, sprev_ref, do_ref, wa_ref, ba_ref, ng_ref, dproj_ref, dwa_ref, dba_ref, dng_ref, dst_ref):
        @pl.when(pl.program_id(0) == 0)
        def _():
            dst_ref[...] = jnp.zeros_like(dst_ref)
            dwa_ref[...] = jnp.zeros_like(dwa_ref)
            dba_ref[...] = jnp.zeros_like(dba_ref)
            dng_ref[...] = jnp.zeros_like(dng_ref)

        st = tuple(sprev_ref[0, h] for h in range(GLA_HEADS))
        _, vjp = jax.vjp(_gla_step, proj_ref[...], st, wa_ref[...], ba_ref[...], ng_ref[...])
        d_next = tuple(dst_ref[h] for h in range(GLA_HEADS))
        d_proj, d_st, d_wa, d_ba, d_ng = vjp((do_ref[...], d_next))
        dproj_ref[...] = d_proj
        for h in range(GLA_HEADS):
            dst_ref[h] = d_st[h]
        dwa_ref[...] += d_wa
        dba_ref[...] += d_ba
        dng_ref[...] += d_ng

    rev = lambda c: (nc - 1 - c, 0)
    return _cargo_call(
        body, cargo, name="gla_core_bwd", grid=(nc,),
        in_specs=[pl.BlockSpec((STEP,GLA_PROJ), rev), pl.BlockSpec((1,) + _GLA_STATE, lambda c: (nc - 1 - c, 0, 0, 0)),
                  pl.BlockSpec((STEP,GLA_VD), rev), _full_spec(w_a2.shape), _full_spec(b_a.shape), _full_spec(norm_g.shape)],
        out_specs=[pl.BlockSpec((STEP,GLA_PROJ), rev), _full_spec(w_a2.shape), _full_spec(b_a.shape), _full_spec(norm_g.shape)],
        out_shape=[jax.ShapeDtypeStruct((seq, GLA_PROJ), F32), jax.ShapeDtypeStruct(w_a2.shape, F32),
                   jax.ShapeDtypeStruct(b_a.shape, F32), jax.ShapeDtypeStruct(norm_g.shape, F32)],
        scratch_shapes=[pltpu.VMEM(_GLA_STATE, F32)],
        semantics=("arbitrary",),
    )(proj, sprev, d_out, w_a2, b_a, norm_g)


def _ssd_chunk(z, xbc, dt_raw, hs, dt_bias, a_log, d_skip, norm_g):
    t = z.shape[0]
    xs = xbc[:, :SSD_DINNER]
    bm = xbc[:, SSD_DINNER:SSD_DINNER + SSD_GN]
    cm = xbc[:, SSD_DINNER + SSD_GN:]
    dt = _softplus(dt_raw + dt_bias)
    da = dt * (-jnp.exp(a_log))
    tri = _tri(t).astype(F32)
    eye = (lax.broadcasted_iota(jnp.int32, (t, t), 0) == lax.broadcasted_iota(jnp.int32, (t, t), 1)).astype(F32)
    cum = _dot(tri, da, exact=True)
    cum_t = _dot(da, tri, 0, 1, exact=True)
    dt_t = _dot(dt, eye, 0, 0, exact=True)
    cum_end = cum[t - 1:t, :]
    w_state = dt * jnp.exp(cum_end - cum)
    e_cum = jnp.exp(cum)
    g_end = jnp.exp(cum_end)
    head_of = lambda axis: lax.shift_right_logical(lax.broadcasted_iota(jnp.int32, (SSD_GW, SSD_GW), axis),
                                                   jnp.int32(SSD_HEADDIM.bit_length() - 1))
    same_head = head_of(0) == head_of(1)
    ys, new_hs = [], []
    for g in range(SSD_GROUPS):
        heads = range(g * SSD_HPG, (g + 1) * SSD_HPG)
        cols = slice(g * SSD_GW, (g + 1) * SSD_GW)

        def spread(a):
            return jnp.concatenate([jnp.broadcast_to(a[:, h:h + 1], (a.shape[0], SSD_HEADDIM)) for h in heads], axis=1)

        def row(a_t):
            return jnp.concatenate([a_t[h:h + 1, :] for h in heads], axis=1)

        bm_g = bm[:, g * SSD_DSTATE:(g + 1) * SSD_DSTATE]
        cm_g = cm[:, g * SSD_DSTATE:(g + 1) * SSD_DSTATE]
        xs_g = xs[:, cols]
        cb = _dot(cm_g, jnp.concatenate([bm_g] * SSD_HPG, axis=0), 1, 1)
        mix = cb * jnp.exp(-jnp.abs(spread(cum) - row(cum_t))) * row(dt_t)
        x_diag = jnp.where(same_head, jnp.concatenate([xs_g] * SSD_HPG, axis=0), 0.0)
        y = _dot(mix, x_diag)
        y = y + _dot(cm_g, hs[g], 1, 1) * spread(e_cum)
        y = y + spread(d_skip) * xs_g
        states = _dot(xs_g * spread(w_state), bm_g, 0, 0)
        decayed = jnp.concatenate([g_end[:, h:h + 1] * hs[g][j * SSD_HEADDIM:(j + 1) * SSD_HEADDIM, :]
                                   for j, h in enumerate(heads)], axis=0)
        new_hs.append(decayed + states)
        yg = y * _silu(z[:, cols])
        ys.append(yg * lax.rsqrt(jnp.mean(yg * yg, axis=-1, keepdims=True) + EPS) * norm_g[:, cols])
    return jnp.concatenate(ys, axis=1), tuple(new_hs)


_SSD_STATE = (SSD_GROUPS, SSD_GW, SSD_DSTATE)


def _ssd_step(z, xbc, dt_raw, hs, dt_bias, a_log, d_skip, norm_g):
    outs = []
    for s in range(STEP_CHUNKS):
        rows = slice(s * CHUNK, (s + 1) * CHUNK)
        out, hs = _ssd_chunk(z[rows], xbc[rows], dt_raw[rows], hs, dt_bias, a_log, d_skip, norm_g)
        outs.append(out)
    return jnp.concatenate(outs, axis=0), hs
_SSD_DT_BLOCK = (SSD_DINNER + SSD_XBC) // LANES


def _ssd_core_fwd(proj, xbc, dt_bias, a_log, d_skip, norm_g, cargo=()):
    seq = proj.shape[0]
    nc = seq // STEP

    def body(z_ref, xbc_ref, dt_ref, db_ref, al_ref, ds_ref, ng_ref, o_ref, hprev_ref, hs_ref):
        @pl.when(pl.program_id(0) == 0)
        def _():
            hs_ref[...] = jnp.zeros_like(hs_ref)

        hs = tuple(hs_ref[g] for g in range(SSD_GROUPS))
        for g in range(SSD_GROUPS):
            hprev_ref[0, g] = hs[g]
        out, new_hs = _ssd_step(z_ref[...], xbc_ref[...], dt_ref[...], hs, db_ref[...], al_ref[...], ds_ref[...], ng_ref[...])
        o_ref[...] = out
        for g in range(SSD_GROUPS):
            hs_ref[g] = new_hs[g]

    return _cargo_call(
        body, cargo, name="ssd_core_fwd", grid=(nc,),
        in_specs=[pl.BlockSpec((STEP,SSD_DINNER), lambda c: (c, 0)), pl.BlockSpec((STEP,SSD_XBC), lambda c: (c, 0)),
                  pl.BlockSpec((STEP,LANES), lambda c: (c, _SSD_DT_BLOCK)),
                  _full_spec(dt_bias.shape), _full_spec(a_log.shape), _full_spec(d_skip.shape), _full_spec(norm_g.shape)],
        out_specs=[pl.BlockSpec((STEP,SSD_DINNER), lambda c: (c, 0)), pl.BlockSpec((1,) + _SSD_STATE, lambda c: (c, 0, 0, 0))],
        out_shape=[jax.ShapeDtypeStruct((seq, SSD_DINNER), F32), jax.ShapeDtypeStruct((nc,) + _SSD_STATE, F32)],
        scratch_shapes=[pltpu.VMEM(_SSD_STATE, F32)],
        semantics=("arbitrary",),
    )(proj, xbc, proj, dt_bias, a_log, d_skip, norm_g)


def _ssd_core_bwd(proj, xbc, hprev, d_out, dt_bias, a_log, d_skip, norm_g, cargo=()):
    seq = proj.shape[0]
    nc = seq // STEP

    def body(z_ref, xbc_ref, dt_ref, hprev_ref, do_ref, db_ref, al_ref, ds_ref, ng_ref,
             dz_ref, dxbc_ref, ddt_ref, ddb_ref, dal_ref, dds_ref, dng_ref, dhs_ref):
        @pl.when(pl.program_id(0) == 0)
        def _():
            dhs_ref[...] = jnp.zeros_like(dhs_ref)
            ddb_ref[...] = jnp.zeros_like(ddb_ref)
            dal_ref[...] = jnp.zeros_like(dal_ref)
            dds_ref[...] = jnp.zeros_like(dds_ref)
            dng_ref[...] = jnp.zeros_like(dng_ref)

        hs = tuple(hprev_ref[0, g] for g in range(SSD_GROUPS))
        _, vjp = jax.vjp(_ssd_step, z_ref[...], xbc_ref[...], dt_ref[...], hs, db_ref[...], al_ref[...], ds_ref[...], ng_ref[...])
        d_next = tuple(dhs_ref[g] for g in range(SSD_GROUPS))
        d_z, d_xbc, d_dt, d_hs, d_db, d_al, d_ds, d_ng = vjp((do_ref[...], d_next))
        dz_ref[...] = d_z
        dxbc_ref[...] = d_xbc
        ddt_ref[...] = d_dt
        for g in range(SSD_GROUPS):
            dhs_ref[g] = d_hs[g]
        ddb_ref[...] += d_db
        dal_ref[...] += d_al
        dds_ref[...] += d_ds
        dng_ref[...] += d_ng

    rev = lambda c: (nc - 1 - c, 0)
    vec = [_full_spec(dt_bias.shape), _full_spec(a_log.shape), _full_spec(d_skip.shape), _full_spec(norm_g.shape)]
    return _cargo_call(
        body, cargo, name="ssd_core_bwd", grid=(nc,),
        in_specs=[pl.BlockSpec((STEP,SSD_DINNER), rev), pl.BlockSpec((STEP,SSD_XBC), rev),
                  pl.BlockSpec((STEP,LANES), lambda c: (nc - 1 - c, _SSD_DT_BLOCK)),
                  pl.BlockSpec((1,) + _SSD_STATE, lambda c: (nc - 1 - c, 0, 0, 0)),
                  pl.BlockSpec((STEP,SSD_DINNER), rev)] + vec,
        out_specs=[pl.BlockSpec((STEP,SSD_DINNER), rev), pl.BlockSpec((STEP,SSD_XBC), rev),
                   pl.BlockSpec((STEP,LANES), rev)] + vec,
        out_shape=[jax.ShapeDtypeStruct((seq, SSD_DINNER), F32), jax.ShapeDtypeStruct((seq, SSD_XBC), F32),
                   jax.ShapeDtypeStruct((seq, LANES), F32),
                   jax.ShapeDtypeStruct(dt_bias.shape, F32), jax.ShapeDtypeStruct(a_log.shape, F32),
                   jax.ShapeDtypeStruct(d_skip.shape, F32), jax.ShapeDtypeStruct(norm_g.shape, F32)],
        scratch_shapes=[pltpu.VMEM(_SSD_STATE, F32)],
        semantics=("arbitrary",),
    )(proj, xbc, proj, hprev, d_out, dt_bias, a_log, d_skip, norm_g)


def _s5_boundary_scan(z, lam_re, lam_im, name, reverse=False):
    n_chunks, groups, width = z.shape
    tn = _tile(n_chunks, 128, 1)
    blocks = n_chunks // tn
    lam_a = jnp.concatenate([lam_re, lam_re], axis=1)
    lam_b = jnp.concatenate([-lam_im, lam_im], axis=1)

    def body(z_ref, a_ref, b_ref, x_ref, carry_ref):
        @pl.when(pl.program_id(0) == 0)
        def _():
            carry_ref[...] = jnp.zeros_like(carry_ref)

        a, b = a_ref[...], b_ref[...]

        def step(i, x):
            n = tn - 1 - i if reverse else i
            x_ref[n] = x
            return a * x + b * pltpu.roll(x, width // 2, 1) + z_ref[n]

        carry_ref[...] = lax.fori_loop(0, tn, step, carry_ref[...])

    block = pl.BlockSpec((tn, groups, width), (lambda i: (blocks - 1 - i, 0, 0)) if reverse else (lambda i: (i, 0, 0)))
    return pl.pallas_call(
        body, name=name, grid=(blocks,), in_specs=[block, _full_spec((groups, width)), _full_spec((groups, width))],
        out_specs=block, out_shape=jax.ShapeDtypeStruct(z.shape, F32), scratch_shapes=[pltpu.VMEM((groups, width), F32)],
        compiler_params=pltpu.CompilerParams(dimension_semantics=("arbitrary",), vmem_limit_bytes=VMEM_LIMIT),
    )(z, lam_a, lam_b)


_FLIPS = [(kx, ky, kc) for kx in (0, 1) for ky in (0, 1) for kc in (0, 1)][1:]


def _mesh_position():
    return lax.axis_index("x"), lax.axis_index("y"), lax.axis_index("c")


def _peer(pos, flip):
    return tuple((1 - p) if f else p for p, f in zip(pos, flip))


def _index(pos):
    return 4 * pos[0] + 2 * pos[1] + pos[2]


_ANY = pl.BlockSpec(memory_space=pl.ANY)


def _moved_shape(kind, x):
    return jax.ShapeDtypeStruct(((N_DEV,) + x.shape) if kind == "gather" else x.shape, x.dtype)


def _moves(kind, x_ref, out_ref, send_sems, recv_sems, local_sem):
    me = _mesh_position()
    source = (lambda pos: x_ref) if kind == "gather" else (lambda pos: x_ref.at[_index(pos)])
    local = pltpu.make_async_copy(source(me), out_ref.at[_index(me)], local_sem)
    outgoing, incoming = [], []
    for k, flip in enumerate(_FLIPS):
        peer = _peer(me, flip)
        copy = lambda slot: pltpu.make_async_remote_copy(
            src_ref=source(peer), dst_ref=out_ref.at[_index(slot)], send_sem=send_sems.at[k], recv_sem=recv_sems.at[k],
            device_id=peer, device_id_type=pl.DeviceIdType.MESH)
        outgoing.append(copy(me))
        incoming.append(copy(peer))
    return local, outgoing, incoming


def _start(moves):
    local, outgoing, _ = moves
    local.start()
    for cp in outgoing:
        cp.start()


def _finish(moves):
    local, outgoing, incoming = moves
    for cp in incoming:
        cp.wait_recv()
    for cp in outgoing:
        cp.wait_send()
    local.wait()


def _collective(kind, x, name):
    def body(x_ref, out_ref, send_sems, recv_sems, local_sem):
        moves = _moves(kind, x_ref, out_ref, send_sems, recv_sems, local_sem)
        _start(moves)
        _finish(moves)

    return pl.pallas_call(
        body, name=name, in_specs=[_ANY], out_specs=_ANY, out_shape=_moved_shape(kind, x),
        scratch_shapes=[pltpu.SemaphoreType.DMA((N_DEV - 1,)), pltpu.SemaphoreType.DMA((N_DEV - 1,)), pltpu.SemaphoreType.DMA],
        compiler_params=pltpu.CompilerParams(has_side_effects=True),
    )(x)


def _adamw(parts, w, m, v, name):
    n_parts = parts.shape[0]
    layers, rows, cols = w.shape
    tr = _tile(rows, 256, 8)

    def body(p_ref, w_ref, m_ref, v_ref, g_ref, d_ref, mo_ref, vo_ref):
        g = p_ref[0].astype(F32)
        for s in range(1, n_parts):
            g = g + p_ref[s].astype(F32)
        m_new = ADAM_B1 * m_ref[...] + (1.0 - ADAM_B1) * g
        v_new = ADAM_B2 * v_ref[...] + (1.0 - ADAM_B2) * (g * g)
        m_hat = m_new / (1.0 - ADAM_B1 ** ADAM_STEP)
        v_hat = v_new / (1.0 - ADAM_B2 ** ADAM_STEP)
        g_ref[...] = g
        d_ref[...] = -ADAM_LR * (m_hat / (jnp.sqrt(v_hat) + ADAM_EPS) + ADAM_WD * w_ref[...])
        mo_ref[...] = m_new
        vo_ref[...] = v_new

    blk = pl.BlockSpec((None, tr, cols), lambda l, i: (l, i, 0))
    shape = jax.ShapeDtypeStruct(w.shape, F32)
    return pl.pallas_call(
        body, name=name, grid=(layers, rows // tr),
        in_specs=[pl.BlockSpec((n_parts, None, tr, cols), lambda l, i: (0, l, i, 0)), blk, blk, blk],
        out_specs=[blk, blk, blk, blk], out_shape=[shape, shape, shape, shape],
        compiler_params=pltpu.CompilerParams(dimension_semantics=("parallel", "parallel"), vmem_limit_bytes=VMEM_LIMIT),
    )(parts, w, m, v)


def _sum_parts(parts, name):
    _, rows, cols = parts.shape
    tr = _tile(rows, 256, 8)

    def body(p_ref, o_ref):
        total = p_ref[0]
        for s in range(1, N_DEV):
            total = total + p_ref[s]
        o_ref[...] = total

    return pl.pallas_call(
        body, name=name, grid=(rows // tr,),
        in_specs=[pl.BlockSpec((N_DEV, tr, cols), lambda i: (0, i, 0))], out_specs=pl.BlockSpec((tr, cols), lambda i: (i, 0)),
        out_shape=jax.ShapeDtypeStruct((rows, cols), parts.dtype),
        compiler_params=pltpu.CompilerParams(dimension_semantics=("parallel",), vmem_limit_bytes=VMEM_LIMIT),
    )(parts)


def _row_tile(rows):
    return _tile(rows, 512, 16)


def _row_spec(rows, cols, block=0):
    return pl.BlockSpec((_row_tile(rows), cols), lambda i: (i, block))


def _rows_params(accumulates):
    return pltpu.CompilerParams(dimension_semantics=("arbitrary" if accumulates else "parallel",),
                                vmem_limit_bytes=VMEM_LIMIT)


def _swiglu_fwd(gu, name):
    rows = gu.shape[0]

    def body(g_ref, u_ref, o_ref):
        o_ref[...] = (_silu(g_ref[...].astype(F32)) * u_ref[...].astype(F32)).astype(o_ref.dtype)

    return pl.pallas_call(
        body, name=name, grid=(rows // _row_tile(rows),),
        in_specs=[_row_spec(rows, FFN_HIDDEN, 0), _row_spec(rows, FFN_HIDDEN, 1)], out_specs=_row_spec(rows, FFN_HIDDEN),
        out_shape=jax.ShapeDtypeStruct((rows, FFN_HIDDEN), BF16), compiler_params=_rows_params(False))(gu, gu)


def _swiglu_bwd(gu, d_act, name):
    rows = gu.shape[0]

    def body(g_ref, u_ref, d_ref, o_ref):
        g, u, d = g_ref[...].astype(F32), u_ref[...].astype(F32), d_ref[...].astype(F32)
        sig = 1.0 / (1.0 + jnp.exp(-g))
        o_ref[:, :FFN_HIDDEN] = (d * u * sig * (1.0 + g * (1.0 - sig))).astype(o_ref.dtype)
        o_ref[:, FFN_HIDDEN:] = (d * g * sig).astype(o_ref.dtype)

    return pl.pallas_call(
        body, name=name, grid=(rows // _row_tile(rows),),
        in_specs=[_row_spec(rows, FFN_HIDDEN, 0), _row_spec(rows, FFN_HIDDEN, 1), _row_spec(rows, FFN_HIDDEN)],
        out_specs=_row_spec(rows, 2 * FFN_HIDDEN),
        out_shape=jax.ShapeDtypeStruct((rows, 2 * FFN_HIDDEN), BF16), compiler_params=_rows_params(False))(gu, gu, d_act)


def _add_norm_fwd(h, y, gain, name):
    rows = h.shape[0]

    def body(*refs):
        if y is None:
            h_ref, g_ref, n_ref = refs
            x = h_ref[...]
        else:
            h_ref, y_ref, g_ref, s_ref, n_ref = refs
            x = h_ref[...] + y_ref[...]
            s_ref[...] = x
        n_ref[...] = (x * lax.rsqrt(jnp.mean(x * x, axis=-1, keepdims=True) + EPS) * g_ref[...]).astype(n_ref.dtype)

    row = _row_spec(rows, D_MODEL)
    ins = [h] if y is None else [h, y]
    out_shape = [jax.ShapeDtypeStruct((rows, D_MODEL), BF16)]
    if y is not None:
        out_shape = [jax.ShapeDtypeStruct((rows, D_MODEL), F32)] + out_shape
    res = pl.pallas_call(
        body, name=name, grid=(rows // _row_tile(rows),),
        in_specs=[row] * len(ins) + [_full_spec((1, D_MODEL))], out_specs=[row] * len(out_shape), out_shape=out_shape,
        compiler_params=_rows_params(False))(*ins, gain[None])
    return (h, res[0]) if y is None else (res[0], res[1])


def _norm_bwd(x, gain, d_n, d_skip, name):
    rows = x.shape[0]

    def body(x_ref, g_ref, dn_ref, ds_ref, dx_ref, dg_ref):
        @pl.when(pl.program_id(0) == 0)
        def _():
            dg_ref[...] = jnp.zeros_like(dg_ref)

        x, dn = x_ref[...], dn_ref[...].astype(F32)
        r = lax.rsqrt(jnp.mean(x * x, axis=-1, keepdims=True) + EPS)
        gd = g_ref[...] * dn
        dx_ref[...] = r * gd - x * (r * r * r) * jnp.mean(x * gd, axis=-1, keepdims=True) + ds_ref[...]
        dg_ref[...] += jnp.sum(x * r * dn, axis=0, keepdims=True)

    row = _row_spec(rows, D_MODEL)
    dx, dg = pl.pallas_call(
        body, name=name, grid=(rows // _row_tile(rows),),
        in_specs=[row, _full_spec((1, D_MODEL)), row, row], out_specs=[row, _full_spec((1, D_MODEL))],
        out_shape=[jax.ShapeDtypeStruct((rows, D_MODEL), F32), jax.ShapeDtypeStruct((1, D_MODEL), F32)],
        compiler_params=_rows_params(True))(x, gain[None], d_n, d_skip)
    return dx, dg[0]


def _loss_head(h, gain, target, name):
    rows = h.shape[0]

    def body(x_ref, g_ref, t_ref, loss_ref, dx_ref, dg_ref):
        @pl.when(pl.program_id(0) == 0)
        def _():
            loss_ref[...] = jnp.zeros_like(loss_ref)
            dg_ref[...] = jnp.zeros_like(dg_ref)

        x = x_ref[...]
        r = lax.rsqrt(jnp.mean(x * x, axis=-1, keepdims=True) + EPS)
        err = x * r * g_ref[...] - t_ref[...]
        loss_ref[...] += 0.5 * jnp.sum(jnp.mean(err * err, axis=-1, keepdims=True), axis=0, keepdims=True)
        dy = err * (1.0 / D_MODEL)
        gd = g_ref[...] * dy
        dx_ref[...] = r * gd - x * (r * r * r) * jnp.mean(x * gd, axis=-1, keepdims=True)
        dg_ref[...] += jnp.sum(x * r * dy, axis=0, keepdims=True)

    row = _row_spec(rows, D_MODEL)
    loss, dx, dg = pl.pallas_call(
        body, name=name, grid=(rows // _row_tile(rows),),
        in_specs=[row, _full_spec((1, D_MODEL)), row], out_specs=[_full_spec((1, 1)), row, _full_spec((1, D_MODEL))],
        out_shape=[jax.ShapeDtypeStruct((1, 1), F32), jax.ShapeDtypeStruct((rows, D_MODEL), F32),
                   jax.ShapeDtypeStruct((1, D_MODEL), F32)],
        compiler_params=_rows_params(True))(h, gain[None], target)
    return loss[0, 0], dx, dg[0]


def _join_cols(blocks, n_out, name):
    _, layers, rows, n = blocks.shape
    tr = _tile(rows, 256, 16)

    def body(x_ref, o_ref):
        for d in range(N_DEV):
            o_ref[:, d * n:(d + 1) * n] = x_ref[d]
        if n_out > N_DEV * n:
            o_ref[:, N_DEV * n:] = jnp.zeros((tr, n_out - N_DEV * n), o_ref.dtype)

    return pl.pallas_call(
        body, name=name, grid=(layers, rows // tr),
        in_specs=[pl.BlockSpec((N_DEV, None, tr, n), lambda l, i: (0, l, i, 0))],
        out_specs=pl.BlockSpec((None, tr, n_out), lambda l, i: (l, i, 0)),
        out_shape=jax.ShapeDtypeStruct((layers, rows, n_out), blocks.dtype),
        compiler_params=pltpu.CompilerParams(dimension_semantics=("parallel", "parallel"), vmem_limit_bytes=VMEM_LIMIT),
    )(blocks)


def _split_cols(full, n, name):
    rows = full.shape[0]
    tr = _tile(rows, 256, 16)

    def body(x_ref, o_ref):
        for d in range(N_DEV):
            o_ref[d] = x_ref[:, d * n:(d + 1) * n]

    return pl.pallas_call(
        body, name=name, grid=(rows // tr,),
        in_specs=[pl.BlockSpec((tr, full.shape[1]), lambda i: (i, 0))],
        out_specs=pl.BlockSpec((N_DEV, tr, n), lambda i: (0, i, 0)),
        out_shape=jax.ShapeDtypeStruct((N_DEV, rows, n), full.dtype),
        compiler_params=pltpu.CompilerParams(dimension_semantics=("parallel",), vmem_limit_bytes=VMEM_LIMIT),
    )(full)


def _pack(arrays):
    flat = jnp.concatenate([a.reshape(-1) for a in arrays])
    unit = FLAT_COLS * FLAT_ROWS_ALIGN
    padded = -(-flat.shape[0] // unit) * unit
    return jnp.pad(flat, (0, padded - flat.shape[0])).reshape(-1, FLAT_COLS)


def _unpack(flat, shapes, lead=()):
    flat = flat.reshape(lead + (-1,))
    out, off = [], 0
    for shape in shapes:
        n = math.prod(shape)
        out.append(flat[..., off:off + n].reshape(lead + tuple(shape)))
        off += n
    return out


def _join(blocks, axis):
    moved = jnp.moveaxis(blocks, 0, axis)
    shape = list(moved.shape)
    shape[axis:axis + 2] = [shape[axis] * shape[axis + 1]]
    return moved.reshape(shape)


def _own_shard(full, axis, position):
    n = full.shape[axis] // N_DEV
    return lax.dynamic_slice_in_dim(full, position * n, n, axis)


def _ssd_conv(xbc, conv_w, conv_b):
    seq = xbc.shape[0]
    padded = jnp.pad(xbc, ((SSD_CONV - 1, 0), (0, 0)))
    out = conv_b
    for k in range(SSD_CONV):
        out = out + padded[k:k + seq] * conv_w[k]
    return jax.nn.silu(out)


def _s5_operators(log_dt, a_re, a_im, b_re, b_im, c_re, c_im):
    t = S5_CHUNK
    hi = lax.Precision.HIGHEST
    step = jnp.exp(log_dt)[:, None]
    mag = jnp.exp(step * a_re)
    abar_re = mag * jnp.cos(step * a_im)
    abar_im = mag * jnp.sin(step * a_im)
    den = a_re * a_re + a_im * a_im
    f_re = ((abar_re - 1.0) * a_re + abar_im * a_im) / den
    f_im = (abar_im * a_re - (abar_re - 1.0) * a_im) / den
    bb_re = f_re[..., None] * b_re - f_im[..., None] * b_im
    bb_im = f_re[..., None] * b_im + f_im[..., None] * b_re
    j = jnp.arange(t + 1, dtype=F32)[:, None, None]
    pmag = jnp.exp(j * (step * a_re))
    pw_re = pmag * jnp.cos(j * (step * a_im))
    pw_im = pmag * jnp.sin(j * (step * a_im))
    cl_re = c_re[None] * pw_re[:t, :, None, :] - c_im[None] * pw_im[:t, :, None, :]
    cl_im = c_re[None] * pw_im[:t, :, None, :] + c_im[None] * pw_re[:t, :, None, :]
    kern = (jnp.einsum('jgcp,gpk->jgck', cl_re, bb_re, precision=hi)
            - jnp.einsum('jgcp,gpk->jgck', cl_im, bb_im, precision=hi))
    rp_re, rp_im = pw_re[:t][::-1], pw_im[:t][::-1]
    wz_re = rp_re[:, :, :, None] * bb_re[None] - rp_im[:, :, :, None] * bb_im[None]
    wz_im = rp_re[:, :, :, None] * bb_im[None] + rp_im[:, :, :, None] * bb_re[None]
    w_z = jnp.concatenate([wz_re, wz_im], axis=2).transpose(1, 0, 3, 2).reshape(S5_GROUPS, t * S5_GROUP, 2 * S5_STATE)
    cy_re = c_re[None] * pw_re[1:, :, None, :] - c_im[None] * pw_im[1:, :, None, :]
    cy_im = c_re[None] * pw_im[1:, :, None, :] + c_im[None] * pw_re[1:, :, None, :]
    w_y = jnp.concatenate([cy_re, -cy_im], axis=3).transpose(1, 3, 0, 2).reshape(S5_GROUPS, 2 * S5_STATE, t * S5_GROUP)
    return kern, w_z, w_y, pw_re[t], pw_im[t]


def _s5_lag_selector():
    t = S5_CHUNK
    lag = jnp.arange(t)[:, None] - jnp.arange(t)[None, :]
    return (lag[:, :, None] == jnp.arange(t)[None, None, :]).astype(F32).reshape(t * t, t)


def _s5_toeplitz(kern, tag):
    t = S5_CHUNK
    sel = _s5_lag_selector()
    flat = _matmul(sel, kern.reshape(t, -1), out_dtype=BF16, name=tag + "_toeplitz")
    toep = flat.reshape(t, t, S5_GROUPS, S5_GROUP, S5_GROUP).transpose(2, 1, 4, 0, 3)
    toep = toep.reshape(S5_GROUPS, t * S5_GROUP, t * S5_GROUP)

    def backward(d_toep):
        d_flat = d_toep.reshape(S5_GROUPS, t, S5_GROUP, t, S5_GROUP).transpose(3, 1, 0, 4, 2).reshape(t * t, -1)
        return _matmul(sel, d_flat, ta=True, name=tag + "_toeplitz_dw").reshape(kern.shape)

    return toep, backward


def _s5_gate(y, u, d_skip):
    return jax.nn.gelu(y + d_skip * u)


def _glu(vg):
    return vg[:, :D_MODEL] * jax.nn.sigmoid(vg[:, D_MODEL:])


_BIG = [("gla_w_in", 2), ("gla_w_out", 1), ("ssd_w_in", 2), ("ssd_w_out", 1), ("s5_w_glu", 2), ("ffn_w_gu", 2),
        ("ffn_w_down", 1)]
_PADDED_COLS = {"gla_w_in": GLA_PROJ, "ssd_w_in": SSD_PROJ}


class _Traffic:
    LINK_BYTES_PER_SECOND = 7.0e10
    MATMUL_FLOPS = 7.0e14

    def __init__(self, shards, plan):
        self.shards, self.plan = shards, plan
        self.position = 0
        self.queue = []
        self.weights, self.received = {}, {}
        self.standalone = 0
        for key in plan:
            self._request(key)

    def _request(self, key):
        shard = self.shards[key]
        seconds = (N_DEV - 1) * shard.size * shard.dtype.itemsize / self.LINK_BYTES_PER_SECOND
        self.queue.append(("gather", shard, seconds, key,
                           lambda blocks: self.weights.__setitem__(key, self._assemble(key, blocks))))

    @staticmethod
    def _assemble(key, blocks):
        name, layer = key
        if dict(_BIG)[name] == 1:
            return blocks.reshape((N_DEV * blocks.shape[1], blocks.shape[2]))
        n_out = _PADDED_COLS.get(name, N_DEV * blocks.shape[2])
        return _join_cols(blocks[:, None], n_out, f"join_{name}_{layer}")[0]

    def take(self, key):
        assert key == self.plan[self.position], (key, self.plan[self.position])
        self.position += 1
        while key not in self.weights:
            self._alone(self.queue.pop(0))
        return self.weights[key]

    def run(self, seconds, call, more_carriers_follow=False):
        riders, waiting, left = [], [], seconds
        for item in self.queue:
            if item[2] <= left:
                riders.append(item)
                left -= item[2]
            else:
                waiting.append(item)
        due = self.plan[self.position] if self.position < len(self.plan) else None
        if not riders and not more_carriers_follow and waiting and waiting[0][3] is not None and waiting[0][3] == due:
            riders.append(waiting.pop(0))
        self.queue = waiting
        if not riders:
            return call(())
        results, moved = call([(kind, x) for kind, x, *_ in riders])
        for item, y in zip(riders, moved):
            item[4](y)
        return results

    def matmul(self, a, b, more_carriers_follow=False, **kw):
        m, n = (a.shape[-1] if kw.get("ta") else a.shape[-2]), (b.shape[-2] if kw.get("tb") else b.shape[-1])
        k = a.shape[-2] if kw.get("ta") else a.shape[-1]
        return self.run(2.0 * m * n * k / self.MATMUL_FLOPS, lambda cargo: _matmul(a, b, cargo=cargo, **kw),
                        more_carriers_follow)

    def send_gradient(self, key, dw):
        name, layer = key
        shard = self.shards[key]
        if dict(_BIG)[name] == 1:
            blocks = dw.reshape((N_DEV,) + shard.shape)
        else:
            blocks = _split_cols(dw, shard.shape[1], f"split_{name}_{layer}")
        seconds = (N_DEV - 1) * shard.size * shard.dtype.itemsize / self.LINK_BYTES_PER_SECOND
        self.queue.append(("exchange", blocks, seconds, None, lambda parts: self.received.__setitem__(key, parts)))

    def _alone(self, item):
        kind, x, _, _, deliver = item
        deliver(_collective(kind, x, f"{kind}_alone_{self.standalone}"))
        self.standalone += 1

    def flush(self):
        for item in self.queue:
            self._alone(item)
        self.queue = []


_GLA_FWD_SECONDS, _GLA_BWD_SECONDS, _SSD_FWD_SECONDS, _SSD_BWD_SECONDS = 1.25e-6, 3.4e-6, 3.5e-6, 14e-6


def _linear(x, w, tag, out_dtype=F32, dx_dtype=F32, more_carriers_follow=False):
    traffic, key = w
    weight = traffic.take(key)
    y = traffic.matmul(x, weight, more_carriers_follow, out_dtype=out_dtype, name=tag + "_fwd")

    def backward(dy):
        dx = traffic.matmul(dy, weight, tb=True, out_dtype=dx_dtype, name=tag + "_dx")
        traffic.send_gradient(key, traffic.matmul(x, dy, ta=True, out_dtype=BF16, name=tag + "_dw"))
        return dx

    return y, backward


def _gla_mixer(hn, p, tag):
    traffic, chunks = p["w_in"][0], hn.shape[0] // CHUNK
    w_a2 = jnp.pad(p["w_a2"], ((0, LANES - GLA_RANK), (0, 0)))
    b_a, norm_g = p["b_a"][None], p["norm_g"][None]
    proj, lin_in = _linear(hn, p["w_in"], tag + "_in", more_carriers_follow=True)
    o, sprev = traffic.run(chunks * _GLA_FWD_SECONDS, lambda cargo: _gla_core_fwd(proj, w_a2, b_a, norm_g, cargo))
    y, lin_out = _linear(o, p["w_out"], tag + "_out")

    def backward(dy):
        d_o = lin_out(dy)
        d_proj, d_wa, d_ba, d_ng = traffic.run(chunks * _GLA_BWD_SECONDS,
                                               lambda cargo: _gla_core_bwd(proj, sprev, d_o, w_a2, b_a, norm_g, cargo))
        return lin_in(d_proj), dict(w_a2=d_wa[:GLA_RANK], b_a=d_ba[0], norm_g=d_ng[0])

    return y, backward


def _ssd_mixer(hn, p, tag):
    pad = lambda a: jnp.pad(a[None], ((0, 0), (0, LANES - SSD_HEADS)))
    dt_bias, a_log, d_skip, norm_g = pad(p["dt_bias"]), pad(p["a_log"]), pad(p["d"]), p["norm_g"][None]
    traffic, chunks = p["w_in"][0], hn.shape[0] // CHUNK
    proj, lin_in = _linear(hn, p["w_in"], tag + "_in", more_carriers_follow=True)
    xbc, conv_vjp = jax.vjp(_ssd_conv, proj[:, SSD_DINNER:SSD_DINNER + SSD_XBC], p["conv_w"], p["conv_b"])
    o, hprev = traffic.run(chunks * _SSD_FWD_SECONDS,
                           lambda cargo: _ssd_core_fwd(proj, xbc, dt_bias, a_log, d_skip, norm_g, cargo))
    y, lin_out = _linear(o, p["w_out"], tag + "_out")

    def backward(dy):
        d_o = lin_out(dy)
        d_z, d_xbc, d_dt, d_db, d_al, d_ds, d_ng = traffic.run(
            chunks * _SSD_BWD_SECONDS, lambda cargo: _ssd_core_bwd(proj, xbc, hprev, d_o, dt_bias, a_log, d_skip, norm_g, cargo))
        d_pre, d_cw, d_cb = conv_vjp(d_xbc)
        d_hn = lin_in(jnp.concatenate([d_z, d_pre, d_dt], axis=1))
        return d_hn, dict(conv_w=d_cw, conv_b=d_cb, dt_bias=d_db[0, :SSD_HEADS], a_log=d_al[0, :SSD_HEADS],
                          d=d_ds[0, :SSD_HEADS], norm_g=d_ng[0])

    return y, backward


def _s5_mixer(hn, p, tag):
    seq = hn.shape[0]
    t, n_chunks = S5_CHUNK, hn.shape[0] // S5_CHUNK
    names = ("log_dt", "a_re", "a_im", "b_re", "b_im", "c_re", "c_im")
    (kern, w_z, w_y, lam_re, lam_im), ops_vjp = jax.vjp(_s5_operators, *[p[k] for k in names])
    toep, toep_bwd = _s5_toeplitz(kern, tag)
    to_groups = lambda a: a.reshape(n_chunks, t, S5_GROUPS, S5_GROUP).transpose(2, 0, 1, 3).reshape(S5_GROUPS, n_chunks, t * S5_GROUP)
    from_groups = lambda a: a.reshape(S5_GROUPS, n_chunks, t, S5_GROUP).transpose(1, 2, 0, 3).reshape(seq, D_MODEL)
    ug = to_groups(hn)
    z = _matmul(ug, w_z, name=tag + "_z")
    x_before = _s5_boundary_scan(z.transpose(1, 0, 2), lam_re, lam_im, tag + "_scan")
    xprev = x_before.transpose(1, 0, 2)
    yg = _matmul(ug, toep, name=tag + "_intra") + _matmul(xprev, w_y, name=tag + "_inter")
    act, gate_vjp = jax.vjp(_s5_gate, from_groups(yg), hn, p["d"])
    vg, lin_glu = _linear(act, p["w_glu"], tag + "_glu")
    out, glu_vjp = jax.vjp(_glu, vg)

    def backward(dy):
        d_vg, = glu_vjp(dy)
        d_act = lin_glu(d_vg)
        d_y, d_hn, d_d = gate_vjp(d_act)
        d_yg = to_groups(d_y)
        d_ug = _matmul(d_yg, toep, tb=True, name=tag + "_intra_dx")
        d_toep = _matmul(ug, d_yg, ta=True, out_dtype=BF16, name=tag + "_intra_dw")
        d_xprev = _matmul(d_yg, w_y, tb=True, name=tag + "_inter_dx").transpose(1, 0, 2)
        d_wy = _matmul(xprev, d_yg, ta=True, name=tag + "_inter_dw")
        dz = _s5_boundary_scan(d_xprev, lam_re, -lam_im, tag + "_scan_bwd", reverse=True)
        x_re, x_im, dz_re, dz_im = (x_before[..., :S5_STATE], x_before[..., S5_STATE:], dz[..., :S5_STATE],
                                    dz[..., S5_STATE:])
        d_lam_re = jnp.sum(x_re * dz_re + x_im * dz_im, axis=0)
        d_lam_im = jnp.sum(x_re * dz_im - x_im * dz_re, axis=0)
        d_z = dz.transpose(1, 0, 2)
        d_ug = d_ug + _matmul(d_z, w_z, tb=True, name=tag + "_z_dx")
        d_wz = _matmul(ug, d_z, ta=True, name=tag + "_z_dw")
        grads = dict(zip(names, ops_vjp((toep_bwd(d_toep), d_wz, d_wy, d_lam_re, d_lam_im))))
        grads.update(d=d_d)
        return d_hn + from_groups(d_ug), grads

    return out, backward


_SMALL =[("gla_w_a2", 2), ("gla_b_a", 1), ("gla_norm_g", 1), ("ssd_conv_w", 2), ("s5_d", 1)]
_REPLICATED = ["norm_mix_g", "norm_ffn_g", "ssd_conv_b", "ssd_dt_bias", "ssd_a_log", "ssd_d", "ssd_norm_g", "s5_log_dt",
               "s5_a_re", "s5_a_im", "s5_b_re", "s5_b_im", "s5_c_re", "s5_c_im", "final_norm_g"]
_WEIGHTS = ['norm_mix_g', 'norm_ffn_g', 'gla_w_in', 'gla_w_a2', 'gla_b_a', 'gla_norm_g', 'gla_w_out', 'ssd_w_in',
            'ssd_conv_w', 'ssd_conv_b', 'ssd_dt_bias', 'ssd_a_log', 'ssd_d', 'ssd_norm_g', 'ssd_w_out', 's5_log_dt',
            's5_a_re', 's5_a_im', 's5_b_re', 's5_b_im', 's5_c_re', 's5_c_im', 's5_d', 's5_w_glu', 'ffn_w_gu', 'ffn_w_down',
            'final_norm_g']


def _gather_small(local):
    shapes = [local[n].shape for n, _ in _SMALL]
    blocks = _collective("gather", _pack([local[n] for n, _ in _SMALL]), "gather_vectors")
    parts = _unpack(blocks, shapes, lead=(N_DEV,))
    return {n: _join(part, axis) for (n, axis), part in zip(_SMALL, parts)}


def _forward_plan():
    plan = []
    for i in range(DEPTH):
        j = i // 3
        plan += [[("gla_w_in", j), ("gla_w_out", j)], [("ssd_w_in", j), ("ssd_w_out", j)], [("s5_w_glu", j)]][i % 3]
        plan += [("ffn_w_gu", i), ("ffn_w_down", i)]
    return plan


def _forward_backward(x, target, w, traffic):
    big = lambda name, j: (traffic, (name, j))
    gla = lambda j: dict(w_in=big("gla_w_in", j), w_a2=w["gla_w_a2"][j], b_a=w["gla_b_a"][j], norm_g=w["gla_norm_g"][j],
                         w_out=big("gla_w_out", j))
    ssd = lambda j: dict(w_in=big("ssd_w_in", j), conv_w=w["ssd_conv_w"][j], conv_b=w["ssd_conv_b"][j],
                         dt_bias=w["ssd_dt_bias"][j], a_log=w["ssd_a_log"][j], d=w["ssd_d"][j], norm_g=w["ssd_norm_g"][j],
                         w_out=big("ssd_w_out", j))
    s5 = lambda j: dict(log_dt=w["s5_log_dt"][j], a_re=w["s5_a_re"][j], a_im=w["s5_a_im"][j], b_re=w["s5_b_re"][j],
                        b_im=w["s5_b_im"][j], c_re=w["s5_c_re"][j], c_im=w["s5_c_im"][j], d=w["s5_d"][j],
                        w_glu=big("s5_w_glu", j))
    mixers = [("gla", _gla_mixer, gla), ("ssd", _ssd_mixer, ssd), ("s5", _s5_mixer, s5)]
    base, delta = x, None
    tape = []
    for i in range(DEPTH):
        kind, mixer, params = mixers[i % 3]
        j = i // 3
        h, hn = _add_norm_fwd(base, delta, w["norm_mix_g"][i], f"l{i}_norm_mix")
        y, mixer_bwd = mixer(hn, params(j), f"l{i}_{kind}")
        h_mid, hn2 = _add_norm_fwd(h, y, w["norm_ffn_g"][i], f"l{i}_norm_ffn")
        gu, gu_bwd = _linear(hn2, big("ffn_w_gu", i), f"l{i}_ffn_gu", out_dtype=BF16)
        act = _swiglu_fwd(gu, f"l{i}_swiglu")
        delta, down_bwd = _linear(act, big("ffn_w_down", i), f"l{i}_ffn_down", dx_dtype=BF16)
        base = h_mid
        tape.append((kind, j, h, mixer_bwd, h_mid, gu_bwd, gu, down_bwd))
    loss, d_h, d_final_g = _loss_head(base + delta, w["final_norm_g"], target, "loss_head")

    grads = {n: [None] * w[n].shape[0] for n in w if n != "final_norm_g"}
    grads["final_norm_g"] = d_final_g
    for i in reversed(range(DEPTH)):
        kind, j, h, mixer_bwd, h_mid, gu_bwd, gu, down_bwd = tape[i]
        d_gu = _swiglu_bwd(gu, down_bwd(d_h), f"l{i}_swiglu_bwd")
        d_mid, grads["norm_ffn_g"][i] = _norm_bwd(h_mid, w["norm_ffn_g"][i], gu_bwd(d_gu), d_h, f"l{i}_norm_ffn_bwd")
        d_hn, mixer_grads = mixer_bwd(d_mid)
        for k, g in mixer_grads.items():
            grads[f"{kind}_{k}"][j] = g
        d_h, grads["norm_mix_g"][i] = _norm_bwd(h, w["norm_mix_g"][i], d_hn, d_mid, f"l{i}_norm_mix_bwd")
    return loss, d_h, grads


def kernel(x, norm_mix_g, norm_ffn_g, gla_w_in, gla_w_a2, gla_b_a, gla_norm_g, gla_w_out, ssd_w_in, ssd_conv_w, ssd_conv_b, ssd_dt_bias, ssd_a_log, ssd_d, ssd_norm_g, ssd_w_out, s5_log_dt, s5_a_re, s5_a_im, s5_b_re, s5_b_im, s5_c_re, s5_c_im, s5_d, s5_w_glu, ffn_w_gu, ffn_w_down, final_norm_g, loss_target, m_norm_mix_g, m_norm_ffn_g, m_gla_w_in, m_gla_w_a2, m_gla_b_a, m_gla_norm_g, m_gla_w_out, m_ssd_w_in, m_ssd_conv_w, m_ssd_conv_b, m_ssd_dt_bias, m_ssd_a_log, m_ssd_d, m_ssd_norm_g, m_ssd_w_out, m_s5_log_dt, m_s5_a_re, m_s5_a_im, m_s5_b_re, m_s5_b_im, m_s5_c_re, m_s5_c_im, m_s5_d, m_s5_w_glu, m_ffn_w_gu, m_ffn_w_down, m_final_norm_g, v_norm_mix_g, v_norm_ffn_g, v_gla_w_in, v_gla_w_a2, v_gla_b_a, v_gla_norm_g, v_gla_w_out, v_ssd_w_in, v_ssd_conv_w, v_ssd_conv_b, v_ssd_dt_bias, v_ssd_a_log, v_ssd_d, v_ssd_norm_g, v_ssd_w_out, v_s5_log_dt, v_s5_a_re, v_s5_a_im, v_s5_b_re, v_s5_b_im, v_s5_c_re, v_s5_c_im, v_s5_d, v_s5_w_glu, v_ffn_w_gu, v_ffn_w_down, v_final_norm_g):
    args = locals()
    local = {n: args[n] for n in _WEIGHTS}
    moment_m = {n: args["m_" + n] for n in _WEIGHTS}
    moment_v = {n: args["v_" + n] for n in _WEIGHTS}

    shards = {(n, layer): local[n][layer].astype(BF16) for n, _ in _BIG for layer in range(local[n].shape[0])}
    traffic = _Traffic(shards, _forward_plan())
    full = {n: local[n] for n in _REPLICATED}
    full.update(_gather_small(local))

    loss, d_x, grads = _forward_backward(x[0], loss_target[0], full, traffic)
    traffic.flush()
    loss = lax.psum(loss, ("x", "y", "c"))
    kinds = ("grad", "delta", "new_m", "new_v")
    out = {}

    for n, _ in _BIG:
        parts = jnp.stack([traffic.received[(n, layer)] for layer in range(local[n].shape[0])], axis=1)
        results = _adamw(parts, local[n], moment_m[n], moment_v[n], "adamw_" + n)
        out.update({f"{kind}_{n}": a for kind, a in zip(kinds, results)})

    small = [n for n, _ in _SMALL] + _REPLICATED
    stacked = lambda n: grads[n] if n == "final_norm_g" else jnp.stack(grads[n])
    parts = _collective("gather", _pack([stacked(n) for n in small]), "gather_small_gradients")
    summed = _unpack(_sum_parts(parts, "sum_small_gradients"), [stacked(n).shape for n in small])
    position = _index(_mesh_position())
    mine = [_own_shard(g, axis, position) for g, (_, axis) in zip(summed, _SMALL)] + summed[len(_SMALL):]
    shapes = [local[n].shape for n in small]
    pk = lambda arrays: _pack(arrays)[None]
    results = _adamw(pk(mine)[None], pk([local[n] for n in small]), pk([moment_m[n] for n in small]),
                     pk([moment_v[n] for n in small]), "adamw_small")
    for kind, flat in zip(kinds, results):
        out.update({f"{kind}_{n}": a for n, a in zip(small, _unpack(flat[0], shapes))})

    return (loss, d_x[None], *[out[f"{kind}_{n}"] for kind in ("grad", "delta", "new_m", "new_v") for n in _WEIGHTS])
```

```python
import functools
import math

import jax
import jax.numpy as jnp
import numpy as np
from jax import lax
from jax.experimental import pallas as pl
from jax.experimental.pallas import tpu as pltpu

F32 = jnp.float32
BF16 = jnp.bfloat16
_MXU_DTYPE = jnp.bfloat16

N_DEV = 8
D_MODEL = 1024
DEPTH = 4
CHUNK = 64
STEP_CHUNKS = 1
STEP = CHUNK * STEP_CHUNKS
EPS = 1e-6
GLA_HEADS, GLA_DK, GLA_DV, GLA_RANK, GLA_TAU = 4, 128, 256, 16, 16.0
GLA_QK = GLA_HEADS * GLA_DK
GLA_VD = GLA_HEADS * GLA_DV
LANES = 128
GLA_IN = 2 * GLA_QK + 2 * GLA_VD + GLA_RANK
GLA_PROJ = 2 * GLA_QK + 2 * GLA_VD + LANES
SSD_DINNER, SSD_HEADDIM, SSD_HEADS, SSD_GROUPS, SSD_HPG, SSD_DSTATE, SSD_CONV = 2048, 64, 32, 8, 4, 128, 4
SSD_GN = SSD_GROUPS * SSD_DSTATE
SSD_GW = SSD_HPG * SSD_HEADDIM
SSD_XBC = SSD_DINNER + 2 * SSD_GN
SSD_IN = SSD_DINNER + SSD_XBC + SSD_HEADS
SSD_PROJ = SSD_DINNER + SSD_XBC + LANES
S5_GROUP, S5_GROUPS, S5_STATE = 16, 64, 64
S5_CHUNK = 16
FFN_HIDDEN = 2816
ADAM_LR, ADAM_B1, ADAM_B2, ADAM_EPS, ADAM_WD, ADAM_STEP = 0.001, 0.9, 0.999, 1e-08, 0.01, 10
VMEM_LIMIT = 48 * 1024 * 1024
FLAT_COLS = 1024
FLAT_ROWS_ALIGN = 64


def _tile(n, cap, unit):
    if n <= cap:
        return n
    best = None
    for t in range(unit, cap + 1, unit):
        if n % t == 0:
            best = t
    assert best is not None, (n, cap, unit)
    return best


def _divisors(n, unit):
    return sorted({t for t in range(unit, n + 1, unit) if n % t == 0} | {n})


_MXU_FLOPS, _HBM_BYTES, _ACC_BYTES, _STEP_SECONDS = 1.1e15, 3e12, 1.1e13, 3.5e-7
_MXU_ROWS = 256
_TILE_VMEM_BUDGET = 36 * 1024 * 1024


def _pick_tiles(m, n, k, a_bytes, b_bytes, o_bytes, m_unit):
    best = None
    for tm in _divisors(m, m_unit):
        for tn in _divisors(n, LANES):
            for tk in _divisors(k, LANES):
                nk = k // tk
                vmem = 2 * tm * tk * a_bytes + 2 * tk * tn * b_bytes + 2 * tm * tn * o_bytes + (nk > 1) * tm * tn * 4
                if vmem > _TILE_VMEM_BUDGET or tm > 2048 or tn > 2048:
                    continue
                a_reads = n // tn if nk > 1 else 1
                b_reads = 1 if (nk == 1 and n == tn) else m // tm
                traffic = m * k * a_bytes * a_reads + k * n * b_bytes * b_reads + m * n * o_bytes
                mxu = 2.0 * m * n * k / _MXU_FLOPS * (1.0 + _MXU_ROWS / tm)
                cost = (max(mxu, traffic / _HBM_BYTES) + (nk > 1) * nk * m * n * 8 / _ACC_BYTES
                        + (m // tm) * (n // tn) * nk * _STEP_SECONDS)
                if best is None or cost < best[0]:
                    best = (cost, tm, tn, tk)
    assert best is not None, (m, n, k)
    return best[1:]


def _cargo_call(body, cargo, *, name, grid, in_specs, out_specs, out_shape, scratch_shapes, semantics):
    params = lambda sem: pltpu.CompilerParams(dimension_semantics=sem, vmem_limit_bytes=VMEM_LIMIT)
    if not cargo:
        return pl.pallas_call(body, name=name, grid=grid, in_specs=in_specs, out_specs=out_specs, out_shape=out_shape,
                              scratch_shapes=scratch_shapes, compiler_params=params(semantics))
    n_in, n_out, n_scratch, n_cargo = len(in_specs), len(out_specs), len(scratch_shapes), len(cargo)

    def loaded(*refs):
        ins, cargo_in, rest = refs[:n_in], refs[n_in:n_in + n_cargo], refs[n_in + n_cargo:]
        outs, cargo_out, rest = rest[:n_out], rest[n_out:n_out + n_cargo], rest[n_out + n_cargo:]
        scratch, sems = rest[:n_scratch], rest[n_scratch:]
        ids = [pl.program_id(d) for d in range(len(grid))]
        first = functools.reduce(jnp.logical_and, [i == 0 for i in ids])
        last = functools.reduce(jnp.logical_and, [i == g - 1 for i, g in zip(ids, grid)])
        moves = lambda: [_moves(kind, x_ref, y_ref, *sems[3 * c:3 * c + 3])
                         for c, ((kind, _), x_ref, y_ref) in enumerate(zip(cargo, cargo_in, cargo_out))]

        @pl.when(first)
        def _():
            for mv in moves():
                _start(mv)

        body(*ins, *outs, *scratch)

        @pl.when(last)
        def _():
            for mv in moves():
                _finish(mv)

    sems = [pltpu.SemaphoreType.DMA((N_DEV - 1,)), pltpu.SemaphoreType.DMA((N_DEV - 1,)), pltpu.SemaphoreType.DMA] * n_cargo
    call = pl.pallas_call(
        loaded, name=name, grid=grid, in_specs=list(in_specs) + [_ANY] * n_cargo,
        out_specs=list(out_specs) + [_ANY] * n_cargo,
        out_shape=list(out_shape) + [_moved_shape(kind, x) for kind, x in cargo],
        scratch_shapes=list(scratch_shapes) + sems, compiler_params=params(("arbitrary",) * len(grid)))

    def run(*args):
        results = call(*args, *[x for _, x in cargo])
        return list(results[:n_out]), list(results[n_out:])

    return run


def _matmul(a, b, *, ta=False, tb=False, out_dtype=F32, name, cargo=()):
    batched = a.ndim == 3
    if ta:
        k_dim, m_dim = a.shape[-2:]
    else:
        m_dim, k_dim = a.shape[-2:]
    if tb:
        n_dim, kb = b.shape[-2:]
    else:
        kb, n_dim = b.shape[-2:]
    assert kb == k_dim, (a.shape, b.shape, ta, tb)
    tm, tn, tk = _pick_tiles(m_dim, n_dim, k_dim, a.dtype.itemsize, b.dtype.itemsize, jnp.dtype(out_dtype).itemsize,
                             LANES if ta else 16)
    nk = k_dim // tk
    ca, cb = (0 if ta else 1), (1 if tb else 0)
    grid = (m_dim // tm, n_dim // tn, nk)
    if batched:
        grid = (a.shape[0],) + grid

    def body(a_ref, b_ref, o_ref, *acc):
        part = lax.dot_general(a_ref[...].astype(_MXU_DTYPE), b_ref[...].astype(_MXU_DTYPE),
                               (((ca,), (cb,)), ((), ())), preferred_element_type=F32)
        if nk == 1:
            o_ref[...] = part.astype(o_ref.dtype)
            return
        acc_ref, = acc
        k = pl.program_id(len(grid) - 1)

        @pl.when(k == 0)
        def _():
            acc_ref[...] = part

        @pl.when(k > 0)
        def _():
            acc_ref[...] += part

        @pl.when(k == nk - 1)
        def _():
            o_ref[...] = acc_ref[...].astype(o_ref.dtype)

    lead = (None,) if batched else ()

    def spec(shape, fn):
        if batched:
            return pl.BlockSpec(lead + shape, lambda g, i, j, k: (g,) + fn(i, j, k))
        return pl.BlockSpec(shape, fn)

    a_spec = spec((tk, tm), lambda i, j, k: (k, i)) if ta else spec((tm, tk), lambda i, j, k: (i, k))
    b_spec = spec((tn, tk), lambda i, j, k: (j, k)) if tb else spec((tk, tn), lambda i, j, k: (k, j))
    o_spec = spec((tm, tn), lambda i, j, k: (i, j))
    out_shape = ((a.shape[0],) if batched else ()) + (m_dim, n_dim)
    call = _cargo_call(
        body, cargo, name=name, grid=grid, in_specs=[a_spec, b_spec], out_specs=[o_spec],
        out_shape=[jax.ShapeDtypeStruct(out_shape, out_dtype)], scratch_shapes=[pltpu.VMEM((tm, tn), F32)] if nk > 1 else [],
        semantics=("parallel",) * (len(grid) - 1) + ("arbitrary",))
    if not cargo:
        return call(a, b)[0]
    results, moved = call(a, b)
    return results[0], moved


def _dot(a, b, ca=1, cb=0, exact=False):
    if exact:
        return lax.dot_general(a, b, (((ca,), (cb,)), ((), ())), precision=lax.Precision.HIGHEST,
                               preferred_element_type=F32)
    return lax.dot_general(a.astype(_MXU_DTYPE), b.astype(_MXU_DTYPE), (((ca,), (cb,)), ((), ())),
                           preferred_element_type=F32)


def _tri(n):
    return lax.broadcasted_iota(jnp.int32, (n, n), 0) >= lax.broadcasted_iota(jnp.int32, (n, n), 1)


def _log_sigmoid(x):
    return jnp.minimum(x, 0.0) - jnp.log(1.0 + jnp.exp(-jnp.abs(x)))


def _softplus(x):
    return jnp.maximum(x, 0.0) + jnp.log(1.0 + jnp.exp(-jnp.abs(x)))


def _silu(x):
    return x / (1.0 + jnp.exp(-x))


def _full_spec(shape):
    return pl.BlockSpec(shape, lambda c: (0,) * len(shape))


def _gla_chunk(proj, st, w_a2, b_a, norm_g):
    t = proj.shape[0]
    q = proj[:, 0:GLA_QK] * (GLA_DK ** -0.5)
    k = proj[:, GLA_QK:2 * GLA_QK]
    v = proj[:, 2 * GLA_QK:2 * GLA_QK + GLA_VD]
    r = proj[:, 2 * GLA_QK + GLA_VD:2 * GLA_QK + 2 * GLA_VD]
    a_low = proj[:, 2 * GLA_QK + 2 * GLA_VD:]
    log_a = _log_sigmoid(_dot(a_low, w_a2) + b_a) * (1.0 / GLA_TAU)
    past = _tri(t)
    lc = _dot(past.astype(F32), log_a, exact=True)
    lend = lc[t - 1:t, :]
    e_pos = jnp.exp(lc)
    e_neg = jnp.exp(-lc)
    q_fwd, k_fwd, q_bwd, k_bwd = q * e_pos, k * e_neg, q * e_neg, k * e_pos
    kd = k * jnp.exp(lend - lc)
    g = jnp.exp(lend)
    outs, new_st = [], []
    for h in range(GLA_HEADS):
        sk = slice(h * GLA_DK, (h + 1) * GLA_DK)
        sv = slice(h * GLA_DV, (h + 1) * GLA_DV)
        s_past = _dot(q_fwd[:, sk], k_fwd[:, sk], 1, 1)
        s_future = _dot(q_bwd[:, sk], k_bwd[:, sk], 1, 1)
        scores = jnp.where(past, s_past, s_future)
        o = _dot(scores, v[:, sv]) + _dot(q_fwd[:, sk], st[h], 1, 1)
        new_st.append(st[h] * g[:, sk] + _dot(v[:, sv], kd[:, sk], 0, 0))
        o = o * lax.rsqrt(jnp.mean(o * o, axis=-1, keepdims=True) + EPS) * norm_g[:, sv]
        outs.append(o)
    return jnp.concatenate(outs, axis=1) * _silu(r), tuple(new_st)


_GLA_STATE = (GLA_HEADS, GLA_DV, GLA_DK)


def _gla_step(proj, st, w_a2, b_a, norm_g):
    outs = []
    for s in range(STEP_CHUNKS):
        out, st = _gla_chunk(proj[s * CHUNK:(s + 1) * CHUNK], st, w_a2, b_a, norm_g)
        outs.append(out)
    return jnp.concatenate(outs, axis=0), st


def _gla_core_fwd(proj, w_a2, b_a, norm_g, cargo=()):
    seq = proj.shape[0]
    nc = seq // STEP

    def body(proj_ref, wa_ref, ba_ref, ng_ref, o_ref, sprev_ref, st_ref):
        @pl.when(pl.program_id(0) == 0)
        def _():
            st_ref[...] = jnp.zeros_like(st_ref)

        st = tuple(st_ref[h] for h in range(GLA_HEADS))
        for h in range(GLA_HEADS):
            sprev_ref[0, h] = st[h]
        out, new_st = _gla_step(proj_ref[...], st, wa_ref[...], ba_ref[...], ng_ref[...])
        o_ref[...] = out
        for h in range(GLA_HEADS):
            st_ref[h] = new_st[h]

    return _cargo_call(
        body, cargo, name="gla_core_fwd", grid=(nc,),
        in_specs=[pl.BlockSpec((STEP,GLA_PROJ), lambda c: (c, 0)), _full_spec(w_a2.shape), _full_spec(b_a.shape),
                  _full_spec(norm_g.shape)],
        out_specs=[pl.BlockSpec((STEP,GLA_VD), lambda c: (c, 0)), pl.BlockSpec((1,) + _GLA_STATE, lambda c: (c, 0, 0, 0))],
        out_shape=[jax.ShapeDtypeStruct((seq, GLA_VD), F32), jax.ShapeDtypeStruct((nc,) + _GLA_STATE, F32)],
        scratch_shapes=[pltpu.VMEM(_GLA_STATE, F32)],
        semantics=("arbitrary",),
    )(proj, w_a2, b_a, norm_g)


def _gla_core_bwd(proj, sprev, d_out, w_a2, b_a, norm_g, cargo=()):
    seq = proj.shape[0]
    nc = seq // STEP

    def body(proj_ref, sprev_ref, do_ref, wa_ref, ba_ref, ng_ref, dproj_ref, dwa_ref, dba_ref, dng_ref, dst_ref):
        @pl.when(pl.program_id(0) == 0)
        def _():
            dst_ref[...] = jnp.zeros_like(dst_ref)
            dwa_ref[...] = jnp.zeros_like(dwa_ref)
            dba_ref[...] = jnp.zeros_like(dba_ref)
            dng_ref[...] = jnp.zeros_like(dng_ref)

        st = tuple(sprev_ref[0, h] for h in range(GLA_HEADS))
        _, vjp = jax.vjp(_gla_step, proj_ref[...], st, wa_ref[...], ba_ref[...], ng_ref[...])
        d_next = tuple(dst_ref[h] for h in range(GLA_HEADS))
        d_proj, d_st, d_wa, d_ba, d_ng = vjp((do_ref[...], d_next))
        dproj_ref[...] = d_proj
        for h in range(GLA_HEADS):
            dst_ref[h] = d_st[h]
        dwa_ref[...] += d_wa
        dba_ref[...] += d_ba
        dng_ref[...] += d_ng

    rev = lambda c: (nc - 1 - c, 0)
    return _cargo_call(
        body, cargo, name="gla_core_bwd", grid=(nc,),
        in_specs=[pl.BlockSpec((STEP,GLA_PROJ), rev), pl.BlockSpec((1,) + _GLA_STATE, lambda c: (nc - 1 - c, 0, 0, 0)),
                  pl.BlockSpec((STEP,GLA_VD), rev), _full_spec(w_a2.shape), _full_spec(b_a.shape), _full_spec(norm_g.shape)],
        out_specs=[pl.BlockSpec((STEP,GLA_PROJ), rev), _full_spec(w_a2.shape), _full_spec(b_a.shape), _full_spec(norm_g.shape)],
        out_shape=[jax.ShapeDtypeStruct((seq, GLA_PROJ), F32), jax.ShapeDtypeStruct(w_a2.shape, F32),
                   jax.ShapeDtypeStruct(b_a.shape, F32), jax.ShapeDtypeStruct(norm_g.shape, F32)],
        scratch_shapes=[pltpu.VMEM(_GLA_STATE, F32)],
        semantics=("arbitrary",),
    )(proj, sprev, d_out, w_a2, b_a, norm_g)


def _ssd_chunk(z, xbc, dt_raw, hs, dt_bias, a_log, d_skip, norm_g):
    t = z.shape[0]
    xs = xbc[:, :SSD_DINNER]
    bm = xbc[:, SSD_DINNER:SSD_DINNER + SSD_GN]
    cm = xbc[:, SSD_DINNER + SSD_GN:]
    dt = _softplus(dt_raw + dt_bias)
    da = dt * (-jnp.exp(a_log))
    tri = _tri(t).astype(F32)
    eye = (lax.broadcasted_iota(jnp.int32, (t, t), 0) == lax.broadcasted_iota(jnp.int32, (t, t), 1)).astype(F32)
    cum = _dot(tri, da, exact=True)
    cum_t = _dot(da, tri, 0, 1, exact=True)
    dt_t = _dot(dt, eye, 0, 0, exact=True)
    cum_end = cum[t - 1:t, :]
    w_state = dt * jnp.exp(cum_end - cum)
    e_cum = jnp.exp(cum)
    g_end = jnp.exp(cum_end)
    head_of = lambda axis: lax.shift_right_logical(lax.broadcasted_iota(jnp.int32, (SSD_GW, SSD_GW), axis),
                                                   jnp.int32(SSD_HEADDIM.bit_length() - 1))
    same_head = head_of(0) == head_of(1)
    ys, new_hs = [], []
    for g in range(SSD_GROUPS):
        heads = range(g * SSD_HPG, (g + 1) * SSD_HPG)
        cols = slice(g * SSD_GW, (g + 1) * SSD_GW)

        def spread(a):
            return jnp.concatenate([jnp.broadcast_to(a[:, h:h + 1], (a.shape[0], SSD_HEADDIM)) for h in heads], axis=1)

        def row(a_t):
            return jnp.concatenate([a_t[h:h + 1, :] for h in heads], axis=1)

        bm_g = bm[:, g * SSD_DSTATE:(g + 1) * SSD_DSTATE]
        cm_g = cm[:, g * SSD_DSTATE:(g + 1) * SSD_DSTATE]
        xs_g = xs[:, cols]
        cb = _dot(cm_g, jnp.concatenate([bm_g] * SSD_HPG, axis=0), 1, 1)
        mix = cb * jnp.exp(-jnp.abs(spread(cum) - row(cum_t))) * row(dt_t)
        x_diag = jnp.where(same_head, jnp.concatenate([xs_g] * SSD_HPG, axis=0), 0.0)
        y = _dot(mix, x_diag)
        y = y + _dot(cm_g, hs[g], 1, 1) * spread(e_cum)
        y = y + spread(d_skip) * xs_g
        states = _dot(xs_g * spread(w_state), bm_g, 0, 0)
        decayed = jnp.concatenate([g_end[:, h:h + 1] * hs[g][j * SSD_HEADDIM:(j + 1) * SSD_HEADDIM, :]
                                   for j, h in enumerate(heads)], axis=0)
        new_hs.append(decayed + states)
        yg = y * _silu(z[:, cols])
        ys.append(yg * lax.rsqrt(jnp.mean(yg * yg, axis=-1, keepdims=True) + EPS) * norm_g[:, cols])
    return jnp.concatenate(ys, axis=1), tuple(new_hs)


_SSD_STATE = (SSD_GROUPS, SSD_GW, SSD_DSTATE)


def _ssd_step(z, xbc, dt_raw, hs, dt_bias, a_log, d_skip, norm_g):
    outs = []
    for s in range(STEP_CHUNKS):
        rows = slice(s * CHUNK, (s + 1) * CHUNK)
        out, hs = _ssd_chunk(z[rows], xbc[rows], dt_raw[rows], hs, dt_bias, a_log, d_skip, norm_g)
        outs.append(out)
    return jnp.concatenate(outs, axis=0), hs
_SSD_DT_BLOCK = (SSD_DINNER + SSD_XBC) // LANES


def _ssd_core_fwd(proj, xbc, dt_bias, a_log, d_skip, norm_g, cargo=()):
    seq = proj.shape[0]
    nc = seq // STEP

    def body(z_ref, xbc_ref, dt_ref, db_ref, al_ref, ds_ref, ng_ref, o_ref, hprev_ref, hs_ref):
        @pl.when(pl.program_id(0) == 0)
        def _():
            hs_ref[...] = jnp.zeros_like(hs_ref)

        hs = tuple(hs_ref[g] for g in range(SSD_GROUPS))
        for g in range(SSD_GROUPS):
            hprev_ref[0, g] = hs[g]
        out, new_hs = _ssd_step(z_ref[...], xbc_ref[...], dt_ref[...], hs, db_ref[...], al_ref[...], ds_ref[...], ng_ref[...])
        o_ref[...] = out
        for g in range(SSD_GROUPS):
            hs_ref[g] = new_hs[g]

    return _cargo_call(
        body, cargo, name="ssd_core_fwd", grid=(nc,),
        in_specs=[pl.BlockSpec((STEP,SSD_DINNER), lambda c: (c, 0)), pl.BlockSpec((STEP,SSD_XBC), lambda c: (c, 0)),
                  pl.BlockSpec((STEP,LANES), lambda c: (c, _SSD_DT_BLOCK)),
                  _full_spec(dt_bias.shape), _full_spec(a_log.shape), _full_spec(d_skip.shape), _full_spec(norm_g.shape)],
        out_specs=[pl.BlockSpec((STEP,SSD_DINNER), lambda c: (c, 0)), pl.BlockSpec((1,) + _SSD_STATE, lambda c: (c, 0, 0, 0))],
        out_shape=[jax.ShapeDtypeStruct((seq, SSD_DINNER), F32), jax.ShapeDtypeStruct((nc,) + _SSD_STATE, F32)],
        scratch_shapes=[pltpu.VMEM(_SSD_STATE, F32)],
        semantics=("arbitrary",),
    )(proj, xbc, proj, dt_bias, a_log, d_skip, norm_g)


def _ssd_core_bwd(proj, xbc, hprev, d_out, dt_bias, a_log, d_skip, norm_g, cargo=()):
    seq = proj.shape[0]
    nc = seq // STEP

    def body(z_ref, xbc_ref, dt_ref, hprev_ref, do_ref, db_ref, al_ref, ds_ref, ng_ref,
             dz_ref, dxbc_ref, ddt_ref, ddb_ref, dal_ref, dds_ref, dng_ref, dhs_ref):
        @pl.when(pl.program_id(0) == 0)
        def _():
            dhs_ref[...] = jnp.zeros_like(dhs_ref)
            ddb_ref[...] = jnp.zeros_like(ddb_ref)
            dal_ref[...] = jnp.zeros_like(dal_ref)
            dds_ref[...] = jnp.zeros_like(dds_ref)
            dng_ref[...] = jnp.zeros_like(dng_ref)

        hs = tuple(hprev_ref[0, g] for g in range(SSD_GROUPS))
        _, vjp = jax.vjp(_ssd_step, z_ref[...], xbc_ref[...], dt_ref[...], hs, db_ref[...], al_ref[...], ds_ref[...], ng_ref[...])
        d_next = tuple(dhs_ref[g] for g in range(SSD_GROUPS))
        d_z, d_xbc, d_dt, d_hs, d_db, d_al, d_ds, d_ng = vjp((do_ref[...], d_next))
        dz_ref[...] = d_z
        dxbc_ref[...] = d_xbc
        ddt_ref[...] = d_dt
        for g in range(SSD_GROUPS):
            dhs_ref[g] = d_hs[g]
        ddb_ref[...] += d_db
        dal_ref[...] += d_al
        dds_ref[...] += d_ds
        dng_ref[...] += d_ng

    rev = lambda c: (nc - 1 - c, 0)
    vec = [_full_spec(dt_bias.shape), _full_spec(a_log.shape), _full_spec(d_skip.shape), _full_spec(norm_g.shape)]
    return _cargo_call(
        body, cargo, name="ssd_core_bwd", grid=(nc,),
        in_specs=[pl.BlockSpec((STEP,SSD_DINNER), rev), pl.BlockSpec((STEP,SSD_XBC), rev),
                  pl.BlockSpec((STEP,LANES), lambda c: (nc - 1 - c, _SSD_DT_BLOCK)),
                  pl.BlockSpec((1,) + _SSD_STATE, lambda c: (nc - 1 - c, 0, 0, 0)),
                  pl.BlockSpec((STEP,SSD_DINNER), rev)] + vec,
        out_specs=[pl.BlockSpec((STEP,SSD_DINNER), rev), pl.BlockSpec((STEP,SSD_XBC), rev),
                   pl.BlockSpec((STEP,LANES), rev)] + vec,
        out_shape=[jax.ShapeDtypeStruct((seq, SSD_DINNER), F32), jax.ShapeDtypeStruct((seq, SSD_XBC), F32),
                   jax.ShapeDtypeStruct((seq, LANES), F32),
                   jax.ShapeDtypeStruct(dt_bias.shape, F32), jax.ShapeDtypeStruct(a_log.shape, F32),
                   jax.ShapeDtypeStruct(d_skip.shape, F32), jax.ShapeDtypeStruct(norm_g.shape, F32)],
        scratch_shapes=[pltpu.VMEM(_SSD_STATE, F32)],
        semantics=("arbitrary",),
    )(proj, xbc, proj, hprev, d_out, dt_bias, a_log, d_skip, norm_g)


CONV_COLS = 2048
CONV_HALO = 8


def _conv_taps(xx, rows):
    last = SSD_CONV - 1
    return [pltpu.roll(xx, last - k, 0)[CONV_HALO:CONV_HALO + rows] if k < last else xx[CONV_HALO:CONV_HALO + rows]
            for k in range(SSD_CONV)]


def _ssd_conv_fwd(proj, conv_w, conv_b, name):
    rows = proj.shape[0]
    tr = _tile(rows, 512, CONV_HALO)
    first_col = SSD_DINNER // CONV_COLS

    def body(x_ref, halo_ref, w_ref, b_ref, o_ref):
        halo = jnp.where(pl.program_id(0) == 0, 0.0, halo_ref[...])
        taps = _conv_taps(jnp.concatenate([halo, x_ref[...]], axis=0), tr)
        out = b_ref[...]
        for k in range(SSD_CONV):
            out = out + taps[k] * w_ref[k:k + 1, :]
        o_ref[...] = _silu(out)

    return pl.pallas_call(
        body, name=name, grid=(rows // tr, SSD_XBC // CONV_COLS),
        in_specs=[pl.BlockSpec((tr, CONV_COLS), lambda i, j: (i, first_col + j)),
                  pl.BlockSpec((CONV_HALO, CONV_COLS), lambda i, j: (jnp.maximum(i * (tr // CONV_HALO) - 1, 0), first_col + j)),
                  pl.BlockSpec((SSD_CONV, CONV_COLS), lambda i, j: (0, j)), pl.BlockSpec((1, CONV_COLS), lambda i, j: (0, j))],
        out_specs=pl.BlockSpec((tr, CONV_COLS), lambda i, j: (i, j)),
        out_shape=jax.ShapeDtypeStruct((rows, SSD_XBC), F32),
        compiler_params=pltpu.CompilerParams(dimension_semantics=("parallel", "parallel"), vmem_limit_bytes=VMEM_LIMIT),
    )(proj, proj, conv_w, conv_b[None])


def _ssd_conv_bwd(proj, d_xbc, conv_w, conv_b, name):
    rows = proj.shape[0]
    tr = _tile(rows, 512, CONV_HALO)
    nb, halos = rows // tr, tr // CONV_HALO
    first_col = SSD_DINNER // CONV_COLS

    def body(x_ref, before_ref, after_ref, d_ref, d_after_ref, w_ref, b_ref, dx_ref, dw_ref, db_ref):
        i = pl.program_id(1)

        @pl.when(i == 0)
        def _():
            dw_ref[...] = jnp.zeros_like(dw_ref)
            db_ref[...] = jnp.zeros_like(db_ref)

        before = jnp.where(i == 0, 0.0, before_ref[...])
        taps = _conv_taps(jnp.concatenate([before, x_ref[...], after_ref[...]], axis=0), tr + CONV_HALO)
        out = b_ref[...]
        for k in range(SSD_CONV):
            out = out + taps[k] * w_ref[k:k + 1, :]
        sig = 1.0 / (1.0 + jnp.exp(-out))
        d_after = jnp.where(i == nb - 1, 0.0, d_after_ref[...])
        d_out = jnp.concatenate([d_ref[...], d_after], axis=0) * sig * (1.0 + out * (1.0 - sig))
        d_x = d_out[:tr] * w_ref[SSD_CONV - 1:SSD_CONV, :]
        for k in range(SSD_CONV - 1):
            ahead = SSD_CONV - 1 - k
            d_x = d_x + pltpu.roll(d_out, tr + CONV_HALO - ahead, 0)[:tr] * w_ref[k:k + 1, :]
        dx_ref[...] = d_x
        for k in range(SSD_CONV):
            dw_ref[k:k + 1, :] += jnp.sum(d_out[:tr] * taps[k][:tr], axis=0, keepdims=True)
        db_ref[...] += jnp.sum(d_out[:tr], axis=0, keepdims=True)

    before = lambda j, i: jnp.maximum(i * halos - 1, 0)
    after = lambda j, i: jnp.minimum((i + 1) * halos, nb * halos - 1)
    d_x, d_w, d_b = pl.pallas_call(
        body, name=name, grid=(SSD_XBC // CONV_COLS, nb),
        in_specs=[pl.BlockSpec((tr, CONV_COLS), lambda j, i: (i, first_col + j)),
                  pl.BlockSpec((CONV_HALO, CONV_COLS), lambda j, i: (before(j, i), first_col + j)),
                  pl.BlockSpec((CONV_HALO, CONV_COLS), lambda j, i: (after(j, i), first_col + j)),
                  pl.BlockSpec((tr, CONV_COLS), lambda j, i: (i, j)),
                  pl.BlockSpec((CONV_HALO, CONV_COLS), lambda j, i: (after(j, i), j)),
                  pl.BlockSpec((SSD_CONV, CONV_COLS), lambda j, i: (0, j)), pl.BlockSpec((1, CONV_COLS), lambda j, i: (0, j))],
        out_specs=[pl.BlockSpec((tr, CONV_COLS), lambda j, i: (i, j)), pl.BlockSpec((SSD_CONV, CONV_COLS), lambda j, i: (0, j)),
                   pl.BlockSpec((1, CONV_COLS), lambda j, i: (0, j))],
        out_shape=[jax.ShapeDtypeStruct((rows, SSD_XBC), F32), jax.ShapeDtypeStruct((SSD_CONV, SSD_XBC), F32),
                   jax.ShapeDtypeStruct((1, SSD_XBC), F32)],
        compiler_params=pltpu.CompilerParams(dimension_semantics=("parallel", "arbitrary"), vmem_limit_bytes=VMEM_LIMIT),
    )(proj, proj, proj, d_xbc, d_xbc, conv_w, conv_b[None])
    return d_x, d_w, d_b[0]


def _s5_boundary_scan(z, lam_re, lam_im, name, reverse=False):
    n_chunks, groups, width = z.shape
    tn = _tile(n_chunks, 128, 1)
    blocks = n_chunks // tn
    lam_a = jnp.concatenate([lam_re, lam_re], axis=1)
    lam_b = jnp.concatenate([-lam_im, lam_im], axis=1)

    def body(z_ref, a_ref, b_ref, x_ref, carry_ref):
        @pl.when(pl.program_id(0) == 0)
        def _():
            carry_ref[...] = jnp.zeros_like(carry_ref)

        a, b = a_ref[...], b_ref[...]

        def step(i, x):
            n = tn - 1 - i if reverse else i
            x_ref[n] = x
            return a * x + b * pltpu.roll(x, width // 2, 1) + z_ref[n]

        carry_ref[...] = lax.fori_loop(0, tn, step, carry_ref[...])

    block = pl.BlockSpec((tn, groups, width), (lambda i: (blocks - 1 - i, 0, 0)) if reverse else (lambda i: (i, 0, 0)))
    return pl.pallas_call(
        body, name=name, grid=(blocks,), in_specs=[block, _full_spec((groups, width)), _full_spec((groups, width))],
        out_specs=block, out_shape=jax.ShapeDtypeStruct(z.shape, F32), scratch_shapes=[pltpu.VMEM((groups, width), F32)],
        compiler_params=pltpu.CompilerParams(dimension_semantics=("arbitrary",), vmem_limit_bytes=VMEM_LIMIT),
    )(z, lam_a, lam_b)


_FLIPS = [(kx, ky, kc) for kx in (0, 1) for ky in (0, 1) for kc in (0, 1)][1:]


def _mesh_position():
    return lax.axis_index("x"), lax.axis_index("y"), lax.axis_index("c")


def _peer(pos, flip):
    return tuple((1 - p) if f else p for p, f in zip(pos, flip))


def _index(pos):
    return 4 * pos[0] + 2 * pos[1] + pos[2]


_ANY = pl.BlockSpec(memory_space=pl.ANY)


def _moved_shape(kind, x):
    return jax.ShapeDtypeStruct(((N_DEV,) + x.shape) if kind == "gather" else x.shape, x.dtype)


def _moves(kind, x_ref, out_ref, send_sems, recv_sems, local_sem):
    me = _mesh_position()
    source = (lambda pos: x_ref) if kind == "gather" else (lambda pos: x_ref.at[_index(pos)])
    local = pltpu.make_async_copy(source(me), out_ref.at[_index(me)], local_sem)
    outgoing, incoming = [], []
    for k, flip in enumerate(_FLIPS):
        peer = _peer(me, flip)
        copy = lambda slot: pltpu.make_async_remote_copy(
            src_ref=source(peer), dst_ref=out_ref.at[_index(slot)], send_sem=send_sems.at[k], recv_sem=recv_sems.at[k],
            device_id=peer, device_id_type=pl.DeviceIdType.MESH)
        outgoing.append(copy(me))
        incoming.append(copy(peer))
    return local, outgoing, incoming


def _start(moves):
    local, outgoing, _ = moves
    local.start()
    for cp in outgoing:
        cp.start()


def _finish(moves):
    local, outgoing, incoming = moves
    for cp in incoming:
        cp.wait_recv()
    for cp in outgoing:
        cp.wait_send()
    local.wait()


def _collective(kind, x, name):
    def body(x_ref, out_ref, send_sems, recv_sems, local_sem):
        moves = _moves(kind, x_ref, out_ref, send_sems, recv_sems, local_sem)
        _start(moves)
        _finish(moves)

    return pl.pallas_call(
        body, name=name, in_specs=[_ANY], out_specs=_ANY, out_shape=_moved_shape(kind, x),
        scratch_shapes=[pltpu.SemaphoreType.DMA((N_DEV - 1,)), pltpu.SemaphoreType.DMA((N_DEV - 1,)), pltpu.SemaphoreType.DMA],
        compiler_params=pltpu.CompilerParams(has_side_effects=True),
    )(x)


def _adamw(parts, w, m, v, name):
    n_parts = parts.shape[0]
    layers, rows, cols = w.shape
    tr = _tile(rows, 256, 8)

    def body(p_ref, w_ref, m_ref, v_ref, g_ref, d_ref, mo_ref, vo_ref):
        g = p_ref[0].astype(F32)
        for s in range(1, n_parts):
            g = g + p_ref[s].astype(F32)
        m_new = ADAM_B1 * m_ref[...] + (1.0 - ADAM_B1) * g
        v_new = ADAM_B2 * v_ref[...] + (1.0 - ADAM_B2) * (g * g)
        m_hat = m_new / (1.0 - ADAM_B1 ** ADAM_STEP)
        v_hat = v_new / (1.0 - ADAM_B2 ** ADAM_STEP)
        g_ref[...] = g
        d_ref[...] = -ADAM_LR * (m_hat / (jnp.sqrt(v_hat) + ADAM_EPS) + ADAM_WD * w_ref[...])
        mo_ref[...] = m_new
        vo_ref[...] = v_new

    blk = pl.BlockSpec((None, tr, cols), lambda l, i: (l, i, 0))
    shape = jax.ShapeDtypeStruct(w.shape, F32)
    return pl.pallas_call(
        body, name=name, grid=(layers, rows // tr),
        in_specs=[pl.BlockSpec((n_parts, None, tr, cols), lambda l, i: (0, l, i, 0)), blk, blk, blk],
        out_specs=[blk, blk, blk, blk], out_shape=[shape, shape, shape, shape],
        compiler_params=pltpu.CompilerParams(dimension_semantics=("parallel", "parallel"), vmem_limit_bytes=VMEM_LIMIT),
    )(parts, w, m, v)


def _sum_parts(parts, name):
    _, rows, cols = parts.shape
    tr = _tile(rows, 256, 8)

    def body(p_ref, o_ref):
        total = p_ref[0]
        for s in range(1, N_DEV):
            total = total + p_ref[s]
        o_ref[...] = total

    return pl.pallas_call(
        body, name=name, grid=(rows // tr,),
        in_specs=[pl.BlockSpec((N_DEV, tr, cols), lambda i: (0, i, 0))], out_specs=pl.BlockSpec((tr, cols), lambda i: (i, 0)),
        out_shape=jax.ShapeDtypeStruct((rows, cols), parts.dtype),
        compiler_params=pltpu.CompilerParams(dimension_semantics=("parallel",), vmem_limit_bytes=VMEM_LIMIT),
    )(parts)


def _row_tile(rows):
    return _tile(rows, 512, 16)


def _row_spec(rows, cols, block=0):
    return pl.BlockSpec((_row_tile(rows), cols), lambda i: (i, block))


def _rows_params(accumulates):
    return pltpu.CompilerParams(dimension_semantics=("arbitrary" if accumulates else "parallel",),
                                vmem_limit_bytes=VMEM_LIMIT)


def _swiglu_fwd(gu, name):
    rows = gu.shape[0]

    def body(g_ref, u_ref, o_ref):
        o_ref[...] = (_silu(g_ref[...].astype(F32)) * u_ref[...].astype(F32)).astype(o_ref.dtype)

    return pl.pallas_call(
        body, name=name, grid=(rows // _row_tile(rows),),
        in_specs=[_row_spec(rows, FFN_HIDDEN, 0), _row_spec(rows, FFN_HIDDEN, 1)], out_specs=_row_spec(rows, FFN_HIDDEN),
        out_shape=jax.ShapeDtypeStruct((rows, FFN_HIDDEN), BF16), compiler_params=_rows_params(False))(gu, gu)


def _swiglu_bwd(gu, d_act, name):
    rows = gu.shape[0]

    def body(g_ref, u_ref, d_ref, o_ref):
        g, u, d = g_ref[...].astype(F32), u_ref[...].astype(F32), d_ref[...].astype(F32)
        sig = 1.0 / (1.0 + jnp.exp(-g))
        o_ref[:, :FFN_HIDDEN] = (d * u * sig * (1.0 + g * (1.0 - sig))).astype(o_ref.dtype)
        o_ref[:, FFN_HIDDEN:] = (d * g * sig).astype(o_ref.dtype)

    return pl.pallas_call(
        body, name=name, grid=(rows // _row_tile(rows),),
        in_specs=[_row_spec(rows, FFN_HIDDEN, 0), _row_spec(rows, FFN_HIDDEN, 1), _row_spec(rows, FFN_HIDDEN)],
        out_specs=_row_spec(rows, 2 * FFN_HIDDEN),
        out_shape=jax.ShapeDtypeStruct((rows, 2 * FFN_HIDDEN), BF16), compiler_params=_rows_params(False))(gu, gu, d_act)


def _add_norm_fwd(h, y, gain, name):
    rows = h.shape[0]

    def body(*refs):
        if y is None:
            h_ref, g_ref, n_ref = refs
            x = h_ref[...]
        else:
            h_ref, y_ref, g_ref, s_ref, n_ref = refs
            x = h_ref[...] + y_ref[...]
            s_ref[...] = x
        n_ref[...] = (x * lax.rsqrt(jnp.mean(x * x, axis=-1, keepdims=True) + EPS) * g_ref[...]).astype(n_ref.dtype)

    row = _row_spec(rows, D_MODEL)
    ins = [h] if y is None else [h, y]
    out_shape = [jax.ShapeDtypeStruct((rows, D_MODEL), BF16)]
    if y is not None:
        out_shape = [jax.ShapeDtypeStruct((rows, D_MODEL), F32)] + out_shape
    res = pl.pallas_call(
        body, name=name, grid=(rows // _row_tile(rows),),
        in_specs=[row] * len(ins) + [_full_spec((1, D_MODEL))], out_specs=[row] * len(out_shape), out_shape=out_shape,
        compiler_params=_rows_params(False))(*ins, gain[None])
    return (h, res[0]) if y is None else (res[0], res[1])


def _norm_bwd(x, gain, d_n, d_skip, name):
    rows = x.shape[0]

    def body(x_ref, g_ref, dn_ref, ds_ref, dx_ref, dg_ref):
        @pl.when(pl.program_id(0) == 0)
        def _():
            dg_ref[...] = jnp.zeros_like(dg_ref)

        x, dn = x_ref[...], dn_ref[...].astype(F32)
        r = lax.rsqrt(jnp.mean(x * x, axis=-1, keepdims=True) + EPS)
        gd = g_ref[...] * dn
        dx_ref[...] = r * gd - x * (r * r * r) * jnp.mean(x * gd, axis=-1, keepdims=True) + ds_ref[...]
        dg_ref[...] += jnp.sum(x * r * dn, axis=0, keepdims=True)

    row = _row_spec(rows, D_MODEL)
    dx, dg = pl.pallas_call(
        body, name=name, grid=(rows // _row_tile(rows),),
        in_specs=[row, _full_spec((1, D_MODEL)), row, row], out_specs=[row, _full_spec((1, D_MODEL))],
        out_shape=[jax.ShapeDtypeStruct((rows, D_MODEL), F32), jax.ShapeDtypeStruct((1, D_MODEL), F32)],
        compiler_params=_rows_params(True))(x, gain[None], d_n, d_skip)
    return dx, dg[0]


def _loss_head(h, gain, target, name):
    rows = h.shape[0]

    def body(x_ref, g_ref, t_ref, loss_ref, dx_ref, dg_ref):
        @pl.when(pl.program_id(0) == 0)
        def _():
            loss_ref[...] = jnp.zeros_like(loss_ref)
            dg_ref[...] = jnp.zeros_like(dg_ref)

        x = x_ref[...]
        r = lax.rsqrt(jnp.mean(x * x, axis=-1, keepdims=True) + EPS)
        err = x * r * g_ref[...] - t_ref[...]
        loss_ref[...] += 0.5 * jnp.sum(jnp.mean(err * err, axis=-1, keepdims=True), axis=0, keepdims=True)
        dy = err * (1.0 / D_MODEL)
        gd = g_ref[...] * dy
        dx_ref[...] = r * gd - x * (r * r * r) * jnp.mean(x * gd, axis=-1, keepdims=True)
        dg_ref[...] += jnp.sum(x * r * dy, axis=0, keepdims=True)

    row = _row_spec(rows, D_MODEL)
    loss, dx, dg = pl.pallas_call(
        body, name=name, grid=(rows // _row_tile(rows),),
        in_specs=[row, _full_spec((1, D_MODEL)), row], out_specs=[_full_spec((1, 1)), row, _full_spec((1, D_MODEL))],
        out_shape=[jax.ShapeDtypeStruct((1, 1), F32), jax.ShapeDtypeStruct((rows, D_MODEL), F32),
                   jax.ShapeDtypeStruct((1, D_MODEL), F32)],
        compiler_params=_rows_params(True))(h, gain[None], target)
    return loss[0, 0], dx, dg[0]


def _join_cols(blocks, n_out, name):
    _, layers, rows, n = blocks.shape
    tr = _tile(rows, 256, 16)

    def body(x_ref, o_ref):
        for d in range(N_DEV):
            o_ref[:, d * n:(d + 1) * n] = x_ref[d]
        if n_out > N_DEV * n:
            o_ref[:, N_DEV * n:] = jnp.zeros((tr, n_out - N_DEV * n), o_ref.dtype)

    return pl.pallas_call(
        body, name=name, grid=(layers, rows // tr),
        in_specs=[pl.BlockSpec((N_DEV, None, tr, n), lambda l, i: (0, l, i, 0))],
        out_specs=pl.BlockSpec((None, tr, n_out), lambda l, i: (l, i, 0)),
        out_shape=jax.ShapeDtypeStruct((layers, rows, n_out), blocks.dtype),
        compiler_params=pltpu.CompilerParams(dimension_semantics=("parallel", "parallel"), vmem_limit_bytes=VMEM_LIMIT),
    )(blocks)


def _split_cols(full, n, name):
    rows = full.shape[0]
    tr = _tile(rows, 256, 16)

    def body(x_ref, o_ref):
        for d in range(N_DEV):
            o_ref[d] = x_ref[:, d * n:(d + 1) * n]

    return pl.pallas_call(
        body, name=name, grid=(rows // tr,),
        in_specs=[pl.BlockSpec((tr, full.shape[1]), lambda i: (i, 0))],
        out_specs=pl.BlockSpec((N_DEV, tr, n), lambda i: (0, i, 0)),
        out_shape=jax.ShapeDtypeStruct((N_DEV, rows, n), full.dtype),
        compiler_params=pltpu.CompilerParams(dimension_semantics=("parallel",), vmem_limit_bytes=VMEM_LIMIT),
    )(full)


def _pack(arrays):
    flat = jnp.concatenate([a.reshape(-1) for a in arrays])
    unit = FLAT_COLS * FLAT_ROWS_ALIGN
    padded = -(-flat.shape[0] // unit) * unit
    return jnp.pad(flat, (0, padded - flat.shape[0])).reshape(-1, FLAT_COLS)


def _unpack(flat, shapes, lead=()):
    flat = flat.reshape(lead + (-1,))
    out, off = [], 0
    for shape in shapes:
        n = math.prod(shape)
        out.append(flat[..., off:off + n].reshape(lead + tuple(shape)))
        off += n
    return out


def _join(blocks, axis):
    moved = jnp.moveaxis(blocks, 0, axis)
    shape = list(moved.shape)
    shape[axis:axis + 2] = [shape[axis] * shape[axis + 1]]
    return moved.reshape(shape)


def _own_shard(full, axis, position):
    n = full.shape[axis] // N_DEV
    return lax.dynamic_slice_in_dim(full, position * n, n, axis)


def _s5_operators(log_dt, a_re, a_im, b_re, b_im, c_re, c_im):
    t = S5_CHUNK
    hi = lax.Precision.HIGHEST
    step = jnp.exp(log_dt)[:, None]
    mag = jnp.exp(step * a_re)
    abar_re = mag * jnp.cos(step * a_im)
    abar_im = mag * jnp.sin(step * a_im)
    den = a_re * a_re + a_im * a_im
    f_re = ((abar_re - 1.0) * a_re + abar_im * a_im) / den
    f_im = (abar_im * a_re - (abar_re - 1.0) * a_im) / den
    bb_re = f_re[..., None] * b_re - f_im[..., None] * b_im
    bb_im = f_re[..., None] * b_im + f_im[..., None] * b_re
    j = jnp.arange(t + 1, dtype=F32)[:, None, None]
    pmag = jnp.exp(j * (step * a_re))
    pw_re = pmag * jnp.cos(j * (step * a_im))
    pw_im = pmag * jnp.sin(j * (step * a_im))
    cl_re = c_re[None] * pw_re[:t, :, None, :] - c_im[None] * pw_im[:t, :, None, :]
    cl_im = c_re[None] * pw_im[:t, :, None, :] + c_im[None] * pw_re[:t, :, None, :]
    kern = (jnp.einsum('jgcp,gpk->jgck', cl_re, bb_re, precision=hi)
            - jnp.einsum('jgcp,gpk->jgck', cl_im, bb_im, precision=hi))
    rp_re, rp_im = pw_re[:t][::-1], pw_im[:t][::-1]
    wz_re = rp_re[:, :, :, None] * bb_re[None] - rp_im[:, :, :, None] * bb_im[None]
    wz_im = rp_re[:, :, :, None] * bb_im[None] + rp_im[:, :, :, None] * bb_re[None]
    w_z = jnp.concatenate([wz_re, wz_im], axis=2).transpose(1, 0, 3, 2).reshape(S5_GROUPS, t * S5_GROUP, 2 * S5_STATE)
    cy_re = c_re[None] * pw_re[1:, :, None, :] - c_im[None] * pw_im[1:, :, None, :]
    cy_im = c_re[None] * pw_im[1:, :, None, :] + c_im[None] * pw_re[1:, :, None, :]
    w_y = jnp.concatenate([cy_re, -cy_im], axis=3).transpose(1, 3, 0, 2).reshape(S5_GROUPS, 2 * S5_STATE, t * S5_GROUP)
    return kern, w_z, w_y, pw_re[t], pw_im[t]


def _s5_lag_selector():
    t = S5_CHUNK
    lag = jnp.arange(t)[:, None] - jnp.arange(t)[None, :]
    return (lag[:, :, None] == jnp.arange(t)[None, None, :]).astype(F32).reshape(t * t, t)


def _s5_toeplitz(kern, tag):
    t = S5_CHUNK
    sel = _s5_lag_selector()
    flat = _matmul(sel, kern.reshape(t, -1), out_dtype=BF16, name=tag + "_toeplitz")
    toep = flat.reshape(t, t, S5_GROUPS, S5_GROUP, S5_GROUP).transpose(2, 1, 4, 0, 3)
    toep = toep.reshape(S5_GROUPS, t * S5_GROUP, t * S5_GROUP)

    def backward(d_toep):
        d_flat = d_toep.reshape(S5_GROUPS, t, S5_GROUP, t, S5_GROUP).transpose(3, 1, 0, 4, 2).reshape(t * t, -1)
        return _matmul(sel, d_flat, ta=True, name=tag + "_toeplitz_dw").reshape(kern.shape)

    return toep, backward


def _s5_gate(y, u, d_skip):
    return jax.nn.gelu(y + d_skip * u)


def _glu(vg):
    return vg[:, :D_MODEL] * jax.nn.sigmoid(vg[:, D_MODEL:])


_BIG = [("gla_w_in", 2), ("gla_w_out", 1), ("ssd_w_in", 2), ("ssd_w_out", 1), ("s5_w_glu", 2), ("ffn_w_gu", 2),
        ("ffn_w_down", 1)]
_PADDED_COLS = {"gla_w_in": GLA_PROJ, "ssd_w_in": SSD_PROJ}


class _Traffic:
    LINK_BYTES_PER_SECOND = 7.0e10
    MATMUL_FLOPS = 7.0e14

    def __init__(self, shards, plan):
        self.shards, self.plan = shards, plan
        self.position = 0
        self.queue = []
        self.weights, self.received = {}, {}
        self.standalone = 0
        for key in plan:
            self._request(key)

    def _request(self, key):
        shard = self.shards[key]
        seconds = (N_DEV - 1) * shard.size * shard.dtype.itemsize / self.LINK_BYTES_PER_SECOND
        self.queue.append(("gather", shard, seconds, key,
                           lambda blocks: self.weights.__setitem__(key, self._assemble(key, blocks))))

    @staticmethod
    def _assemble(key, blocks):
        name, layer = key
        if dict(_BIG)[name] == 1:
            return blocks.reshape((N_DEV * blocks.shape[1], blocks.shape[2]))
        n_out = _PADDED_COLS.get(name, N_DEV * blocks.shape[2])
        return _join_cols(blocks[:, None], n_out, f"join_{name}_{layer}")[0]

    def take(self, key):
        assert key == self.plan[self.position], (key, self.plan[self.position])
        self.position += 1
        while key not in self.weights:
            self._alone(self.queue.pop(0))
        return self.weights[key]

    def run(self, seconds, call, more_carriers_follow=False):
        riders, waiting, left = [], [], seconds
        for item in self.queue:
            if item[2] <= left:
                riders.append(item)
                left -= item[2]
            else:
                waiting.append(item)
        due = self.plan[self.position] if self.position < len(self.plan) else None
        if not riders and not more_carriers_follow and waiting and waiting[0][3] is not None and waiting[0][3] == due:
            riders.append(waiting.pop(0))
        self.queue = waiting
        if not riders:
            return call(())
        results, moved = call([(kind, x) for kind, x, *_ in riders])
        for item, y in zip(riders, moved):
            item[4](y)
        return results

    def matmul(self, a, b, more_carriers_follow=False, **kw):
        m, n = (a.shape[-1] if kw.get("ta") else a.shape[-2]), (b.shape[-2] if kw.get("tb") else b.shape[-1])
        k = a.shape[-2] if kw.get("ta") else a.shape[-1]
        return self.run(2.0 * m * n * k / self.MATMUL_FLOPS, lambda cargo: _matmul(a, b, cargo=cargo, **kw),
                        more_carriers_follow)

    def send_gradient(self, key, dw):
        name, layer = key
        shard = self.shards[key]
        if dict(_BIG)[name] == 1:
            blocks = dw.reshape((N_DEV,) + shard.shape)
        else:
            blocks = _split_cols(dw, shard.shape[1], f"split_{name}_{layer}")
        seconds = (N_DEV - 1) * shard.size * shard.dtype.itemsize / self.LINK_BYTES_PER_SECOND
        self.queue.append(("exchange", blocks, seconds, None, lambda parts: self.received.__setitem__(key, parts)))

    def _alone(self, item):
        kind, x, _, _, deliver = item
        deliver(_collective(kind, x, f"{kind}_alone_{self.standalone}"))
        self.standalone += 1

    def flush(self):
        for item in self.queue:
            self._alone(item)
        self.queue = []


_GLA_FWD_SECONDS, _GLA_BWD_SECONDS, _SSD_FWD_SECONDS, _SSD_BWD_SECONDS = 1.25e-6, 3.4e-6, 3.5e-6, 14e-6


def _linear(x, w, tag, out_dtype=F32, dx_dtype=F32, more_carriers_follow=False):
    traffic, key = w
    weight = traffic.take(key)
    y = traffic.matmul(x, weight, more_carriers_follow, out_dtype=out_dtype, name=tag + "_fwd")

    def backward(dy):
        dx = traffic.matmul(dy, weight, tb=True, out_dtype=dx_dtype, name=tag + "_dx")
        traffic.send_gradient(key, traffic.matmul(x, dy, ta=True, out_dtype=BF16, name=tag + "_dw"))
        return dx

    return y, backward


def _gla_mixer(hn, p, tag):
    traffic, chunks = p["w_in"][0], hn.shape[0] // CHUNK
    w_a2 = jnp.pad(p["w_a2"], ((0, LANES - GLA_RANK), (0, 0)))
    b_a, norm_g = p["b_a"][None], p["norm_g"][None]
    proj, lin_in = _linear(hn, p["w_in"], tag + "_in", more_carriers_follow=True)
    o, sprev = traffic.run(chunks * _GLA_FWD_SECONDS, lambda cargo: _gla_core_fwd(proj, w_a2, b_a, norm_g, cargo))
    y, lin_out = _linear(o, p["w_out"], tag + "_out")

    def backward(dy):
        d_o = lin_out(dy)
        d_proj, d_wa, d_ba, d_ng = traffic.run(chunks * _GLA_BWD_SECONDS,
                                               lambda cargo: _gla_core_bwd(proj, sprev, d_o, w_a2, b_a, norm_g, cargo))
        return lin_in(d_proj), dict(w_a2=d_wa[:GLA_RANK], b_a=d_ba[0], norm_g=d_ng[0])

    return y, backward


def _ssd_mixer(hn, p, tag):
    pad = lambda a: jnp.pad(a[None], ((0, 0), (0, LANES - SSD_HEADS)))
    dt_bias, a_log, d_skip, norm_g = pad(p["dt_bias"]), pad(p["a_log"]), pad(p["d"]), p["norm_g"][None]
    traffic, chunks = p["w_in"][0], hn.shape[0] // CHUNK
    proj, lin_in = _linear(hn, p["w_in"], tag + "_in", more_carriers_follow=True)
    xbc = _ssd_conv_fwd(proj, p["conv_w"], p["conv_b"], tag + "_conv")
    o, hprev = traffic.run(chunks * _SSD_FWD_SECONDS,
                           lambda cargo: _ssd_core_fwd(proj, xbc, dt_bias, a_log, d_skip, norm_g, cargo))
    y, lin_out = _linear(o, p["w_out"], tag + "_out")

    def backward(dy):
        d_o = lin_out(dy)
        d_z, d_xbc, d_dt, d_db, d_al, d_ds, d_ng = traffic.run(
            chunks * _SSD_BWD_SECONDS, lambda cargo: _ssd_core_bwd(proj, xbc, hprev, d_o, dt_bias, a_log, d_skip, norm_g, cargo))
        d_pre, d_cw, d_cb = _ssd_conv_bwd(proj, d_xbc, p["conv_w"], p["conv_b"], tag + "_conv_bwd")
        d_hn = lin_in(jnp.concatenate([d_z, d_pre, d_dt], axis=1))
        return d_hn, dict(conv_w=d_cw, conv_b=d_cb, dt_bias=d_db[0, :SSD_HEADS], a_log=d_al[0, :SSD_HEADS],
                          d=d_ds[0, :SSD_HEADS], norm_g=d_ng[0])

    return y, backward


def _s5_mixer(hn, p, tag):
    seq = hn.shape[0]
    t, n_chunks = S5_CHUNK, hn.shape[0] // S5_CHUNK
    names = ("log_dt", "a_re", "a_im", "b_re", "b_im", "c_re", "c_im")
    (kern, w_z, w_y, lam_re, lam_im), ops_vjp = jax.vjp(_s5_operators, *[p[k] for k in names])
    toep, toep_bwd = _s5_toeplitz(kern, tag)
    to_groups = lambda a: a.reshape(n_chunks, t, S5_GROUPS, S5_GROUP).transpose(2, 0, 1, 3).reshape(S5_GROUPS, n_chunks, t * S5_GROUP)
    from_groups = lambda a: a.reshape(S5_GROUPS, n_chunks, t, S5_GROUP).transpose(1, 2, 0, 3).reshape(seq, D_MODEL)
    ug = to_groups(hn)
    z = _matmul(ug, w_z, name=tag + "_z")
    x_before = _s5_boundary_scan(z.transpose(1, 0, 2), lam_re, lam_im, tag + "_scan")
    xprev = x_before.transpose(1, 0, 2)
    yg = _matmul(ug, toep, name=tag + "_intra") + _matmul(xprev, w_y, name=tag + "_inter")
    act, gate_vjp = jax.vjp(_s5_gate, from_groups(yg), hn, p["d"])
    vg, lin_glu = _linear(act, p["w_glu"], tag + "_glu")
    out, glu_vjp = jax.vjp(_glu, vg)

    def backward(dy):
        d_vg, = glu_vjp(dy)
        d_act = lin_glu(d_vg)
        d_y, d_hn, d_d = gate_vjp(d_act)
        d_yg = to_groups(d_y)
        d_ug = _matmul(d_yg, toep, tb=True, name=tag + "_intra_dx")
        d_toep = _matmul(ug, d_yg, ta=True, out_dtype=BF16, name=tag + "_intra_dw")
        d_xprev = _matmul(d_yg, w_y, tb=True, name=tag + "_inter_dx").transpose(1, 0, 2)
        d_wy = _matmul(xprev, d_yg, ta=True, name=tag + "_inter_dw")
        dz = _s5_boundary_scan(d_xprev, lam_re, -lam_im, tag + "_scan_bwd", reverse=True)
        x_re, x_im, dz_re, dz_im = (x_before[..., :S5_STATE], x_before[..., S5_STATE:], dz[..., :S5_STATE],
                                    dz[..., S5_STATE:])
        d_lam_re = jnp.sum(x_re * dz_re + x_im * dz_im, axis=0)
        d_lam_im = jnp.sum(x_re * dz_im - x_im * dz_re, axis=0)
        d_z = dz.transpose(1, 0, 2)
        d_ug = d_ug + _matmul(d_z, w_z, tb=True, name=tag + "_z_dx")
        d_wz = _matmul(ug, d_z, ta=True, name=tag + "_z_dw")
        grads = dict(zip(names, ops_vjp((toep_bwd(d_toep), d_wz, d_wy, d_lam_re, d_lam_im))))
        grads.update(d=d_d)
        return d_hn + from_groups(d_ug), grads

    return out, backward


_SMALL =[("gla_w_a2", 2), ("gla_b_a", 1), ("gla_norm_g", 1), ("ssd_conv_w", 2), ("s5_d", 1)]
_REPLICATED = ["norm_mix_g", "norm_ffn_g", "ssd_conv_b", "ssd_dt_bias", "ssd_a_log", "ssd_d", "ssd_norm_g", "s5_log_dt",
               "s5_a_re", "s5_a_im", "s5_b_re", "s5_b_im", "s5_c_re", "s5_c_im", "final_norm_g"]
_WEIGHTS = ['norm_mix_g', 'norm_ffn_g', 'gla_w_in', 'gla_w_a2', 'gla_b_a', 'gla_norm_g', 'gla_w_out', 'ssd_w_in',
            'ssd_conv_w', 'ssd_conv_b', 'ssd_dt_bias', 'ssd_a_log', 'ssd_d', 'ssd_norm_g', 'ssd_w_out', 's5_log_dt',
            's5_a_re', 's5_a_im', 's5_b_re', 's5_b_im', 's5_c_re', 's5_c_im', 's5_d', 's5_w_glu', 'ffn_w_gu', 'ffn_w_down',
            'final_norm_g']


def _gather_small(local):
    shapes = [local[n].shape for n, _ in _SMALL]
    blocks = _collective("gather", _pack([local[n] for n, _ in _SMALL]), "gather_vectors")
    parts = _unpack(blocks, shapes, lead=(N_DEV,))
    return {n: _join(part, axis) for (n, axis), part in zip(_SMALL, parts)}


def _forward_plan():
    plan = []
    for i in range(DEPTH):
        j = i // 3
        plan += [[("gla_w_in", j), ("gla_w_out", j)], [("ssd_w_in", j), ("ssd_w_out", j)], [("s5_w_glu", j)]][i % 3]
        plan += [("ffn_w_gu", i), ("ffn_w_down", i)]
    return plan


def _forward_backward(x, target, w, traffic):
    big = lambda name, j: (traffic, (name, j))
    gla = lambda j: dict(w_in=big("gla_w_in", j), w_a2=w["gla_w_a2"][j], b_a=w["gla_b_a"][j], norm_g=w["gla_norm_g"][j],
                         w_out=big("gla_w_out", j))
    ssd = lambda j: dict(w_in=big("ssd_w_in", j), conv_w=w["ssd_conv_w"][j], conv_b=w["ssd_conv_b"][j],
                         dt_bias=w["ssd_dt_bias"][j], a_log=w["ssd_a_log"][j], d=w["ssd_d"][j], norm_g=w["ssd_norm_g"][j],
                         w_out=big("ssd_w_out", j))
    s5 = lambda j: dict(log_dt=w["s5_log_dt"][j], a_re=w["s5_a_re"][j], a_im=w["s5_a_im"][j], b_re=w["s5_b_re"][j],
                        b_im=w["s5_b_im"][j], c_re=w["s5_c_re"][j], c_im=w["s5_c_im"][j], d=w["s5_d"][j],
                        w_glu=big("s5_w_glu", j))
    mixers = [("gla", _gla_mixer, gla), ("ssd", _ssd_mixer, ssd), ("s5", _s5_mixer, s5)]
    base, delta = x, None
    tape = []
    for i in range(DEPTH):
        kind, mixer, params = mixers[i % 3]
        j = i // 3
        h, hn = _add_norm_fwd(base, delta, w["norm_mix_g"][i], f"l{i}_norm_mix")
        y, mixer_bwd = mixer(hn, params(j), f"l{i}_{kind}")
        h_mid, hn2 = _add_norm_fwd(h, y, w["norm_ffn_g"][i], f"l{i}_norm_ffn")
        gu, gu_bwd = _linear(hn2, big("ffn_w_gu", i), f"l{i}_ffn_gu", out_dtype=BF16)
        act = _swiglu_fwd(gu, f"l{i}_swiglu")
        delta, down_bwd = _linear(act, big("ffn_w_down", i), f"l{i}_ffn_down", dx_dtype=BF16)
        base = h_mid
        tape.append((kind, j, h, mixer_bwd, h_mid, gu_bwd, gu, down_bwd))
    loss, d_h, d_final_g = _loss_head(base + delta, w["final_norm_g"], target, "loss_head")

    grads = {n: [None] * w[n].shape[0] for n in w if n != "final_norm_g"}
    grads["final_norm_g"] = d_final_g
    for i in reversed(range(DEPTH)):
        kind, j, h, mixer_bwd, h_mid, gu_bwd, gu, down_bwd = tape[i]
        d_gu = _swiglu_bwd(gu, down_bwd(d_h), f"l{i}_swiglu_bwd")
        d_mid, grads["norm_ffn_g"][i] = _norm_bwd(h_mid, w["norm_ffn_g"][i], gu_bwd(d_gu), d_h, f"l{i}_norm_ffn_bwd")
        d_hn, mixer_grads = mixer_bwd(d_mid)
        for k, g in mixer_grads.items():
            grads[f"{kind}_{k}"][j] = g
        d_h, grads["norm_mix_g"][i] = _norm_bwd(h, w["norm_mix_g"][i], d_hn, d_mid, f"l{i}_norm_mix_bwd")
    return loss, d_h, grads


def kernel(x, norm_mix_g, norm_ffn_g, gla_w_in, gla_w_a2, gla_b_a, gla_norm_g, gla_w_out, ssd_w_in, ssd_conv_w, ssd_conv_b, ssd_dt_bias, ssd_a_log, ssd_d, ssd_norm_g, ssd_w_out, s5_log_dt, s5_a_re, s5_a_im, s5_b_re, s5_b_im, s5_c_re, s5_c_im, s5_d, s5_w_glu, ffn_w_gu, ffn_w_down, final_norm_g, loss_target, m_norm_mix_g, m_norm_ffn_g, m_gla_w_in, m_gla_w_a2, m_gla_b_a, m_gla_norm_g, m_gla_w_out, m_ssd_w_in, m_ssd_conv_w, m_ssd_conv_b, m_ssd_dt_bias, m_ssd_a_log, m_ssd_d, m_ssd_norm_g, m_ssd_w_out, m_s5_log_dt, m_s5_a_re, m_s5_a_im, m_s5_b_re, m_s5_b_im, m_s5_c_re, m_s5_c_im, m_s5_d, m_s5_w_glu, m_ffn_w_gu, m_ffn_w_down, m_final_norm_g, v_norm_mix_g, v_norm_ffn_g, v_gla_w_in, v_gla_w_a2, v_gla_b_a, v_gla_norm_g, v_gla_w_out, v_ssd_w_in, v_ssd_conv_w, v_ssd_conv_b, v_ssd_dt_bias, v_ssd_a_log, v_ssd_d, v_ssd_norm_g, v_ssd_w_out, v_s5_log_dt, v_s5_a_re, v_s5_a_im, v_s5_b_re, v_s5_b_im, v_s5_c_re, v_s5_c_im, v_s5_d, v_s5_w_glu, v_ffn_w_gu, v_ffn_w_down, v_final_norm_g):
    args = locals()
    local = {n: args[n] for n in _WEIGHTS}
    moment_m = {n: args["m_" + n] for n in _WEIGHTS}
    moment_v = {n: args["v_" + n] for n in _WEIGHTS}

    shards = {(n, layer): local[n][layer].astype(BF16) for n, _ in _BIG for layer in range(local[n].shape[0])}
    traffic = _Traffic(shards, _forward_plan())
    full = {n: local[n] for n in _REPLICATED}
    full.update(_gather_small(local))

    loss, d_x, grads = _forward_backward(x[0], loss_target[0], full, traffic)
    traffic.flush()
    loss = lax.psum(loss, ("x", "y", "c"))
    kinds = ("grad", "delta", "new_m", "new_v")
    out = {}

    for n, _ in _BIG:
        parts = jnp.stack([traffic.received[(n, layer)] for layer in range(local[n].shape[0])], axis=1)
        results = _adamw(parts, local[n], moment_m[n], moment_v[n], "adamw_" + n)
        out.update({f"{kind}_{n}": a for kind, a in zip(kinds, results)})

    small = [n for n, _ in _SMALL] + _REPLICATED
    stacked = lambda n: grads[n] if n == "final_norm_g" else jnp.stack(grads[n])
    parts = _collective("gather", _pack([stacked(n) for n in small]), "gather_small_gradients")
    summed = _unpack(_sum_parts(parts, "sum_small_gradients"), [stacked(n).shape for n in small])
    position = _index(_mesh_position())
    mine = [_own_shard(g, axis, position) for g, (_, axis) in zip(summed, _SMALL)] + summed[len(_SMALL):]
    shapes = [local[n].shape for n in small]
    pk = lambda arrays: _pack(arrays)[None]
    results = _adamw(pk(mine)[None], pk([local[n] for n in small]), pk([moment_m[n] for n in small]),
                     pk([moment_v[n] for n in small]), "adamw_small")
    for kind, flat in zip(kinds, results):
        out.update({f"{kind}_{n}": a for n, a in zip(small, _unpack(flat[0], shapes))})

    return (loss, d_x[None], *[out[f"{kind}_{n}"] for kind in ("grad", "delta", "new_m", "new_v") for n in _WEIGHTS])
```

```python
import functools
import math

import jax
import jax.numpy as jnp
import numpy as np
from jax import lax
from jax.experimental import pallas as pl
from jax.experimental.pallas import tpu as pltpu

F32 = jnp.float32
BF16 = jnp.bfloat16
_MXU_DTYPE = jnp.bfloat16

N_DEV = 8
D_MODEL = 1024
DEPTH = 4
CHUNK = 64
STEP_CHUNKS = 1
STEP = CHUNK * STEP_CHUNKS
EPS = 1e-6
GLA_HEADS, GLA_DK, GLA_DV, GLA_RANK, GLA_TAU = 4, 128, 256, 16, 16.0
GLA_QK = GLA_HEADS * GLA_DK
GLA_VD = GLA_HEADS * GLA_DV
LANES = 128
GLA_IN = 2 * GLA_QK + 2 * GLA_VD + GLA_RANK
GLA_PROJ = 2 * GLA_QK + 2 * GLA_VD + LANES
SSD_DINNER, SSD_HEADDIM, SSD_HEADS, SSD_GROUPS, SSD_HPG, SSD_DSTATE, SSD_CONV = 2048, 64, 32, 8, 4, 128, 4
SSD_GN = SSD_GROUPS * SSD_DSTATE
SSD_GW = SSD_HPG * SSD_HEADDIM
SSD_XBC = SSD_DINNER + 2 * SSD_GN
SSD_IN = SSD_DINNER + SSD_XBC + SSD_HEADS
SSD_PROJ = SSD_DINNER + SSD_XBC + LANES
S5_GROUP, S5_GROUPS, S5_STATE = 16, 64, 64
S5_CHUNK = 16
FFN_HIDDEN = 2816
ADAM_LR, ADAM_B1, ADAM_B2, ADAM_EPS, ADAM_WD, ADAM_STEP = 0.001, 0.9, 0.999, 1e-08, 0.01, 10
VMEM_LIMIT = 48 * 1024 * 1024
FLAT_COLS = 1024
FLAT_ROWS_ALIGN = 64


def _tile(n, cap, unit):
    if n <= cap:
        return n
    best = None
    for t in range(unit, cap + 1, unit):
        if n % t == 0:
            best = t
    assert best is not None, (n, cap, unit)
    return best


def _divisors(n, unit):
    return sorted({t for t in range(unit, n + 1, unit) if n % t == 0} | {n})


_MXU_FLOPS, _HBM_BYTES, _ACC_BYTES, _STEP_SECONDS = 1.1e15, 3e12, 1.1e13, 3.5e-7
_MXU_ROWS = 256
_TILE_VMEM_BUDGET = 36 * 1024 * 1024


def _pick_tiles(m, n, k, a_bytes, b_bytes, o_bytes, m_unit):
    best = None
    for tm in _divisors(m, m_unit):
        for tn in _divisors(n, LANES):
            for tk in _divisors(k, LANES):
                nk = k // tk
                vmem = 2 * tm * tk * a_bytes + 2 * tk * tn * b_bytes + 2 * tm * tn * o_bytes + (nk > 1) * tm * tn * 4
                if vmem > _TILE_VMEM_BUDGET or tm > 2048 or tn > 2048:
                    continue
                a_reads = n // tn if nk > 1 else 1
                b_reads = 1 if (nk == 1 and n == tn) else m // tm
                traffic = m * k * a_bytes * a_reads + k * n * b_bytes * b_reads + m * n * o_bytes
                mxu = 2.0 * m * n * k / _MXU_FLOPS * (1.0 + _MXU_ROWS / tm)
                cost = (max(mxu, traffic / _HBM_BYTES) + (nk > 1) * nk * m * n * 8 / _ACC_BYTES
                        + (m // tm) * (n // tn) * nk * _STEP_SECONDS)
                if best is None or cost < best[0]:
                    best = (cost, tm, tn, tk)
    assert best is not None, (m, n, k)
    return best[1:]


def _cargo_call(body, cargo, *, name, grid, in_specs, out_specs, out_shape, scratch_shapes, semantics):
    params = lambda sem: pltpu.CompilerParams(dimension_semantics=sem, vmem_limit_bytes=VMEM_LIMIT)
    if not cargo:
        return pl.pallas_call(body, name=name, grid=grid, in_specs=in_specs, out_specs=out_specs, out_shape=out_shape,
                              scratch_shapes=scratch_shapes, compiler_params=params(semantics))
    n_in, n_out, n_scratch, n_cargo = len(in_specs), len(out_specs), len(scratch_shapes), len(cargo)

    def loaded(*refs):
        ins, cargo_in, rest = refs[:n_in], refs[n_in:n_in + n_cargo], refs[n_in + n_cargo:]
        outs, cargo_out, rest = rest[:n_out], rest[n_out:n_out + n_cargo], rest[n_out + n_cargo:]
        scratch, sems = rest[:n_scratch], rest[n_scratch:]
        ids = [pl.program_id(d) for d in range(len(grid))]
        first = functools.reduce(jnp.logical_and, [i == 0 for i in ids])
        last = functools.reduce(jnp.logical_and, [i == g - 1 for i, g in zip(ids, grid)])
        moves = lambda: [_moves(kind, x_ref, y_ref, *sems[3 * c:3 * c + 3])
                         for c, ((kind, _), x_ref, y_ref) in enumerate(zip(cargo, cargo_in, cargo_out))]

        @pl.when(first)
        def _():
            for mv in moves():
                _start(mv)

        body(*ins, *outs, *scratch)

        @pl.when(last)
        def _():
            for mv in moves():
                _finish(mv)

    sems = [pltpu.SemaphoreType.DMA((N_DEV - 1,)), pltpu.SemaphoreType.DMA((N_DEV - 1,)), pltpu.SemaphoreType.DMA] * n_cargo
    call = pl.pallas_call(
        loaded, name=name, grid=grid, in_specs=list(in_specs) + [_ANY] * n_cargo,
        out_specs=list(out_specs) + [_ANY] * n_cargo,
        out_shape=list(out_shape) + [_moved_shape(kind, x) for kind, x in cargo],
        scratch_shapes=list(scratch_shapes) + sems, compiler_params=params(("arbitrary",) * len(grid)))

    def run(*args):
        results = call(*args, *[x for _, x in cargo])
        return list(results[:n_out]), list(results[n_out:])

    return run


def _matmul(a, b, *, ta=False, tb=False, out_dtype=F32, name, cargo=()):
    batched = a.ndim == 3
    if ta:
        k_dim, m_dim = a.shape[-2:]
    else:
        m_dim, k_dim = a.shape[-2:]
    if tb:
        n_dim, kb = b.shape[-2:]
    else:
        kb, n_dim = b.shape[-2:]
    assert kb == k_dim, (a.shape, b.shape, ta, tb)
    tm, tn, tk = _pick_tiles(m_dim, n_dim, k_dim, a.dtype.itemsize, b.dtype.itemsize, jnp.dtype(out_dtype).itemsize,
                             LANES if ta else 16)
    nk = k_dim // tk
    ca, cb = (0 if ta else 1), (1 if tb else 0)
    grid = (m_dim // tm, n_dim // tn, nk)
    if batched:
        grid = (a.shape[0],) + grid

    def body(a_ref, b_ref, o_ref, *acc):
        part = lax.dot_general(a_ref[...].astype(_MXU_DTYPE), b_ref[...].astype(_MXU_DTYPE),
                               (((ca,), (cb,)), ((), ())), preferred_element_type=F32)
        if nk == 1:
            o_ref[...] = part.astype(o_ref.dtype)
            return
        acc_ref, = acc
        k = pl.program_id(len(grid) - 1)

        @pl.when(k == 0)
        def _():
            acc_ref[...] = part

        @pl.when(k > 0)
        def _():
            acc_ref[...] += part

        @pl.when(k == nk - 1)
        def _():
            o_ref[...] = acc_ref[...].astype(o_ref.dtype)

    lead = (None,) if batched else ()

    def spec(shape, fn):
        if batched:
            return pl.BlockSpec(lead + shape, lambda g, i, j, k: (g,) + fn(i, j, k))
        return pl.BlockSpec(shape, fn)

    a_spec = spec((tk, tm), lambda i, j, k: (k, i)) if ta else spec((tm, tk), lambda i, j, k: (i, k))
    b_spec = spec((tn, tk), lambda i, j, k: (j, k)) if tb else spec((tk, tn), lambda i, j, k: (k, j))
    o_spec = spec((tm, tn), lambda i, j, k: (i, j))
    out_shape = ((a.shape[0],) if batched else ()) + (m_dim, n_dim)
    call = _cargo_call(
        body, cargo, name=name, grid=grid, in_specs=[a_spec, b_spec], out_specs=[o_spec],
        out_shape=[jax.ShapeDtypeStruct(out_shape, out_dtype)], scratch_shapes=[pltpu.VMEM((tm, tn), F32)] if nk > 1 else [],
        semantics=("parallel",) * (len(grid) - 1) + ("arbitrary",))
    if not cargo:
        return call(a, b)[0]
    results, moved = call(a, b)
    return results[0], moved


def _dot(a, b, ca=1, cb=0, exact=False):
    if exact:
        return lax.dot_general(a, b, (((ca,), (cb,)), ((), ())), precision=lax.Precision.HIGHEST,
                               preferred_element_type=F32)
    return lax.dot_general(a.astype(_MXU_DTYPE), b.astype(_MXU_DTYPE), (((ca,), (cb,)), ((), ())),
                           preferred_element_type=F32)


def _tri(n):
    return lax.broadcasted_iota(jnp.int32, (n, n), 0) >= lax.broadcasted_iota(jnp.int32, (n, n), 1)


def _log_sigmoid(x):
    return jnp.minimum(x, 0.0) - jnp.log(1.0 + jnp.exp(-jnp.abs(x)))


def _softplus(x):
    return jnp.maximum(x, 0.0) + jnp.log(1.0 + jnp.exp(-jnp.abs(x)))


def _silu(x):
    return x / (1.0 + jnp.exp(-x))


def _full_spec(shape):
    return pl.BlockSpec(shape, lambda c: (0,) * len(shape))


def _gla_chunk(proj, st, w_a2, b_a, norm_g):
    t = proj.shape[0]
    q = proj[:, 0:GLA_QK] * (GLA_DK ** -0.5)
    k = proj[:, GLA_QK:2 * GLA_QK]
    v = proj[:, 2 * GLA_QK:2 * GLA_QK + GLA_VD]
    r = proj[:, 2 * GLA_QK + GLA_VD:2 * GLA_QK + 2 * GLA_VD]
    a_low = proj[:, 2 * GLA_QK + 2 * GLA_VD:]
    log_a = _log_sigmoid(_dot(a_low, w_a2) + b_a) * (1.0 / GLA_TAU)
    past = _tri(t)
    lc = _dot(past.astype(F32), log_a, exact=True)
    lend = lc[t - 1:t, :]
    e_pos = jnp.exp(lc)
    e_neg = jnp.exp(-lc)
    q_fwd, k_fwd, q_bwd, k_bwd = q * e_pos, k * e_neg, q * e_neg, k * e_pos
    kd = k * jnp.exp(lend - lc)
    g = jnp.exp(lend)
    outs, new_st = [], []
    for h in range(GLA_HEADS):
        sk = slice(h * GLA_DK, (h + 1) * GLA_DK)
        sv = slice(h * GLA_DV, (h + 1) * GLA_DV)
        s_past = _dot(q_fwd[:, sk], k_fwd[:, sk], 1, 1)
        s_future = _dot(q_bwd[:, sk], k_bwd[:, sk], 1, 1)
        scores = jnp.where(past, s_past, s_future)
        o = _dot(scores, v[:, sv]) + _dot(q_fwd[:, sk], st[h], 1, 1)
        new_st.append(st[h] * g[:, sk] + _dot(v[:, sv], kd[:, sk], 0, 0))
        o = o * lax.rsqrt(jnp.mean(o * o, axis=-1, keepdims=True) + EPS) * norm_g[:, sv]
        outs.append(o)
    return jnp.concatenate(outs, axis=1) * _silu(r), tuple(new_st)


_GLA_STATE = (GLA_HEADS, GLA_DV, GLA_DK)


def _gla_step(proj, st, w_a2, b_a, norm_g):
    outs = []
    for s in range(STEP_CHUNKS):
        out, st = _gla_chunk(proj[s * CHUNK:(s + 1) * CHUNK], st, w_a2, b_a, norm_g)
        outs.append(out)
    return jnp.concatenate(outs, axis=0), st


def _gla_core_fwd(proj, w_a2, b_a, norm_g, cargo=()):
    seq = proj.shape[0]
    nc = seq // STEP

    def body(proj_ref, wa_ref, ba_ref, ng_ref, o_ref, sprev_ref, st_ref):
        @pl.when(pl.program_id(0) == 0)
        def _():
            st_ref[...] = jnp.zeros_like(st_ref)

        st = tuple(st_ref[h] for h in range(GLA_HEADS))
        for h in range(GLA_HEADS):
            sprev_ref[0, h] = st[h]
        out, new_st = _gla_step(proj_ref[...], st, wa_ref[...], ba_ref[...], ng_ref[...])
        o_ref[...] = out
        for h in range(GLA_HEADS):
            st_ref[h] = new_st[h]

    return _cargo_call(
        body, cargo, name="gla_core_fwd", grid=(nc,),
        in_specs=[pl.BlockSpec((STEP,GLA_PROJ), lambda c: (c, 0)), _full_spec(w_a2.shape), _full_spec(b_a.shape),
                  _full_spec(norm_g.shape)],
        out_specs=[pl.BlockSpec((STEP,GLA_VD), lambda c: (c, 0)), pl.BlockSpec((1,) + _GLA_STATE, lambda c: (c, 0, 0, 0))],
        out_shape=[jax.ShapeDtypeStruct((seq, GLA_VD), F32), jax.ShapeDtypeStruct((nc,) + _GLA_STATE, F32)],
        scratch_shapes=[pltpu.VMEM(_GLA_STATE, F32)],
        semantics=("arbitrary",),
    )(proj, w_a2, b_a, norm_g)


def _gla_core_bwd(proj, sprev, d_out, w_a2, b_a, norm_g, cargo=()):
    seq = proj.shape[0]
    nc = seq // STEP

    def body(proj_ref, sprev_ref, do_ref, wa_ref, ba_ref, ng_ref, dproj_ref, dwa_ref, dba_ref, dng_ref, dst_ref):
        @pl.when(pl.program_id(0) == 0)
        def _():
            dst_ref[...] = jnp.zeros_like(dst_ref)
            dwa_ref[...] = jnp.zeros_like(dwa_ref)
            dba_ref[...] = jnp.zeros_like(dba_ref)
            dng_ref[...] = jnp.zeros_like(dng_ref)

        st = tuple(sprev_ref[0, h] for h in range(GLA_HEADS))
        _, vjp = jax.vjp(_gla_step, proj_ref[...], st, wa_ref[...], ba_ref[...], ng_ref[...])
        d_next = tuple(dst_ref[h] for h in range(GLA_HEADS))
        d_proj, d_st, d_wa, d_ba, d_ng = vjp((do_ref[...], d_next))
        dproj_ref[...] = d_proj.astype(dproj_ref.dtype)
        for h in range(GLA_HEADS):
            dst_ref[h] = d_st[h]
        dwa_ref[...] += d_wa
        dba_ref[...] += d_ba
        dng_ref[...] += d_ng

    rev = lambda c: (nc - 1 - c, 0)
    return _cargo_call(
        body, cargo, name="gla_core_bwd", grid=(nc,),
        in_specs=[pl.BlockSpec((STEP,GLA_PROJ), rev), pl.BlockSpec((1,) + _GLA_STATE, lambda c: (nc - 1 - c, 0, 0, 0)),
                  pl.BlockSpec((STEP,GLA_VD), rev), _full_spec(w_a2.shape), _full_spec(b_a.shape), _full_spec(norm_g.shape)],
        out_specs=[pl.BlockSpec((STEP,GLA_PROJ), rev), _full_spec(w_a2.shape), _full_spec(b_a.shape), _full_spec(norm_g.shape)],
        out_shape=[jax.ShapeDtypeStruct((seq, GLA_PROJ), BF16), jax.ShapeDtypeStruct(w_a2.shape, F32),
                   jax.ShapeDtypeStruct(b_a.shape, F32), jax.ShapeDtypeStruct(norm_g.shape, F32)],
        scratch_shapes=[pltpu.VMEM(_GLA_STATE, F32)],
        semantics=("arbitrary",),
    )(proj, sprev, d_out, w_a2, b_a, norm_g)


def _ssd_chunk(z, xbc, dt_raw, hs, dt_bias, a_log, d_skip, norm_g):
    t = z.shape[0]
    xs = xbc[:, :SSD_DINNER]
    bm = xbc[:, SSD_DINNER:SSD_DINNER + SSD_GN]
    cm = xbc[:, SSD_DINNER + SSD_GN:]
    dt = _softplus(dt_raw + dt_bias)
    da = dt * (-jnp.exp(a_log))
    tri = _tri(t).astype(F32)
    eye = (lax.broadcasted_iota(jnp.int32, (t, t), 0) == lax.broadcasted_iota(jnp.int32, (t, t), 1)).astype(F32)
    cum = _dot(tri, da, exact=True)
    cum_t = _dot(da, tri, 0, 1, exact=True)
    dt_t = _dot(dt, eye, 0, 0, exact=True)
    cum_end = cum[t - 1:t, :]
    w_state = dt * jnp.exp(cum_end - cum)
    e_cum = jnp.exp(cum)
    g_end = jnp.exp(cum_end)
    head_of = lambda axis: lax.shift_right_logical(lax.broadcasted_iota(jnp.int32, (SSD_GW, SSD_GW), axis),
                                                   jnp.int32(SSD_HEADDIM.bit_length() - 1))
    same_head = head_of(0) == head_of(1)
    ys, new_hs = [], []
    for g in range(SSD_GROUPS):
        heads = range(g * SSD_HPG, (g + 1) * SSD_HPG)
        cols = slice(g * SSD_GW, (g + 1) * SSD_GW)

        def spread(a):
            return jnp.concatenate([jnp.broadcast_to(a[:, h:h + 1], (a.shape[0], SSD_HEADDIM)) for h in heads], axis=1)

        def row(a_t):
            return jnp.concatenate([a_t[h:h + 1, :] for h in heads], axis=1)

        bm_g = bm[:, g * SSD_DSTATE:(g + 1) * SSD_DSTATE]
        cm_g = cm[:, g * SSD_DSTATE:(g + 1) * SSD_DSTATE]
        xs_g = xs[:, cols]
        cb = _dot(cm_g, jnp.concatenate([bm_g] * SSD_HPG, axis=0), 1, 1)
        mix = cb * jnp.exp(-jnp.abs(spread(cum) - row(cum_t))) * row(dt_t)
        x_diag = jnp.where(same_head, jnp.concatenate([xs_g] * SSD_HPG, axis=0), 0.0)
        y = _dot(mix, x_diag)
        y = y + _dot(cm_g, hs[g], 1, 1) * spread(e_cum)
        y = y + spread(d_skip) * xs_g
        states = _dot(xs_g * spread(w_state), bm_g, 0, 0)
        decayed = jnp.concatenate([g_end[:, h:h + 1] * hs[g][j * SSD_HEADDIM:(j + 1) * SSD_HEADDIM, :]
                                   for j, h in enumerate(heads)], axis=0)
        new_hs.append(decayed + states)
        yg = y * _silu(z[:, cols])
        ys.append(yg * lax.rsqrt(jnp.mean(yg * yg, axis=-1, keepdims=True) + EPS) * norm_g[:, cols])
    return jnp.concatenate(ys, axis=1), tuple(new_hs)


_SSD_STATE = (SSD_GROUPS, SSD_GW, SSD_DSTATE)


def _ssd_step(z, xbc, dt_raw, hs, dt_bias, a_log, d_skip, norm_g):
    outs = []
    for s in range(STEP_CHUNKS):
        rows = slice(s * CHUNK, (s + 1) * CHUNK)
        out, hs = _ssd_chunk(z[rows], xbc[rows], dt_raw[rows], hs, dt_bias, a_log, d_skip, norm_g)
        outs.append(out)
    return jnp.concatenate(outs, axis=0), hs
_SSD_DT_BLOCK = (SSD_DINNER + SSD_XBC) // LANES


def _ssd_core_fwd(proj, xbc, dt_bias, a_log, d_skip, norm_g, cargo=()):
    seq = proj.shape[0]
    nc = seq // STEP

    def body(z_ref, xbc_ref, dt_ref, db_ref, al_ref, ds_ref, ng_ref, o_ref, hprev_ref, hs_ref):
        @pl.when(pl.program_id(0) == 0)
        def _():
            hs_ref[...] = jnp.zeros_like(hs_ref)

        hs = tuple(hs_ref[g] for g in range(SSD_GROUPS))
        for g in range(SSD_GROUPS):
            hprev_ref[0, g] = hs[g]
        out, new_hs = _ssd_step(z_ref[...], xbc_ref[...], dt_ref[...], hs, db_ref[...], al_ref[...], ds_ref[...], ng_ref[...])
        o_ref[...] = out
        for g in range(SSD_GROUPS):
            hs_ref[g] = new_hs[g]

    return _cargo_call(
        body, cargo, name="ssd_core_fwd", grid=(nc,),
        in_specs=[pl.BlockSpec((STEP,SSD_DINNER), lambda c: (c, 0)), pl.BlockSpec((STEP,SSD_XBC), lambda c: (c, 0)),
                  pl.BlockSpec((STEP,LANES), lambda c: (c, _SSD_DT_BLOCK)),
                  _full_spec(dt_bias.shape), _full_spec(a_log.shape), _full_spec(d_skip.shape), _full_spec(norm_g.shape)],
        out_specs=[pl.BlockSpec((STEP,SSD_DINNER), lambda c: (c, 0)), pl.BlockSpec((1,) + _SSD_STATE, lambda c: (c, 0, 0, 0))],
        out_shape=[jax.ShapeDtypeStruct((seq, SSD_DINNER), F32), jax.ShapeDtypeStruct((nc,) + _SSD_STATE, F32)],
        scratch_shapes=[pltpu.VMEM(_SSD_STATE, F32)],
        semantics=("arbitrary",),
    )(proj, xbc, proj, dt_bias, a_log, d_skip, norm_g)


def _ssd_core_bwd(proj, xbc, hprev, d_out, dt_bias, a_log, d_skip, norm_g, cargo=()):
    seq = proj.shape[0]
    nc = seq // STEP

    def body(z_ref, xbc_ref, dt_ref, hprev_ref, do_ref, db_ref, al_ref, ds_ref, ng_ref,
             dz_ref, dxbc_ref, ddt_ref, ddb_ref, dal_ref, dds_ref, dng_ref, dhs_ref):
        @pl.when(pl.program_id(0) == 0)
        def _():
            dhs_ref[...] = jnp.zeros_like(dhs_ref)
            ddb_ref[...] = jnp.zeros_like(ddb_ref)
            dal_ref[...] = jnp.zeros_like(dal_ref)
            dds_ref[...] = jnp.zeros_like(dds_ref)
            dng_ref[...] = jnp.zeros_like(dng_ref)

        hs = tuple(hprev_ref[0, g] for g in range(SSD_GROUPS))
        _, vjp = jax.vjp(_ssd_step, z_ref[...], xbc_ref[...], dt_ref[...], hs, db_ref[...], al_ref[...], ds_ref[...], ng_ref[...])
        d_next = tuple(dhs_ref[g] for g in range(SSD_GROUPS))
        d_z, d_xbc, d_dt, d_hs, d_db, d_al, d_ds, d_ng = vjp((do_ref[...], d_next))
        dz_ref[...] = d_z.astype(dz_ref.dtype)
        dxbc_ref[...] = d_xbc
        ddt_ref[...] = d_dt.astype(ddt_ref.dtype)
        for g in range(SSD_GROUPS):
            dhs_ref[g] = d_hs[g]
        ddb_ref[...] += d_db
        dal_ref[...] += d_al
        dds_ref[...] += d_ds
        dng_ref[...] += d_ng

    rev = lambda c: (nc - 1 - c, 0)
    vec = [_full_spec(dt_bias.shape), _full_spec(a_log.shape), _full_spec(d_skip.shape), _full_spec(norm_g.shape)]
    return _cargo_call(
        body, cargo, name="ssd_core_bwd", grid=(nc,),
        in_specs=[pl.BlockSpec((STEP,SSD_DINNER), rev), pl.BlockSpec((STEP,SSD_XBC), rev),
                  pl.BlockSpec((STEP,LANES), lambda c: (nc - 1 - c, _SSD_DT_BLOCK)),
                  pl.BlockSpec((1,) + _SSD_STATE, lambda c: (nc - 1 - c, 0, 0, 0)),
                  pl.BlockSpec((STEP,SSD_DINNER), rev)] + vec,
        out_specs=[pl.BlockSpec((STEP,SSD_DINNER), rev), pl.BlockSpec((STEP,SSD_XBC), rev),
                   pl.BlockSpec((STEP,LANES), rev)] + vec,
        out_shape=[jax.ShapeDtypeStruct((seq, SSD_DINNER), BF16), jax.ShapeDtypeStruct((seq, SSD_XBC), F32),
                   jax.ShapeDtypeStruct((seq, LANES), BF16),
                   jax.ShapeDtypeStruct(dt_bias.shape, F32), jax.ShapeDtypeStruct(a_log.shape, F32),
                   jax.ShapeDtypeStruct(d_skip.shape, F32), jax.ShapeDtypeStruct(norm_g.shape, F32)],
        scratch_shapes=[pltpu.VMEM(_SSD_STATE, F32)],
        semantics=("arbitrary",),
    )(proj, xbc, proj, hprev, d_out, dt_bias, a_log, d_skip, norm_g)


CONV_COLS = 2048
CONV_HALO = 8


def _conv_taps(xx, rows):
    last = SSD_CONV - 1
    return [pltpu.roll(xx, last - k, 0)[CONV_HALO:CONV_HALO + rows] if k < last else xx[CONV_HALO:CONV_HALO + rows]
            for k in range(SSD_CONV)]


def _ssd_conv_fwd(proj, conv_w, conv_b, name):
    rows = proj.shape[0]
    tr = _tile(rows, 512, CONV_HALO)
    first_col = SSD_DINNER // CONV_COLS

    def body(x_ref, halo_ref, w_ref, b_ref, o_ref):
        halo = jnp.where(pl.program_id(0) == 0, 0.0, halo_ref[...])
        taps = _conv_taps(jnp.concatenate([halo, x_ref[...]], axis=0), tr)
        out = b_ref[...]
        for k in range(SSD_CONV):
            out = out + taps[k] * w_ref[k:k + 1, :]
        o_ref[...] = _silu(out)

    return pl.pallas_call(
        body, name=name, grid=(rows // tr, SSD_XBC // CONV_COLS),
        in_specs=[pl.BlockSpec((tr, CONV_COLS), lambda i, j: (i, first_col + j)),
                  pl.BlockSpec((CONV_HALO, CONV_COLS), lambda i, j: (jnp.maximum(i * (tr // CONV_HALO) - 1, 0), first_col + j)),
                  pl.BlockSpec((SSD_CONV, CONV_COLS), lambda i, j: (0, j)), pl.BlockSpec((1, CONV_COLS), lambda i, j: (0, j))],
        out_specs=pl.BlockSpec((tr, CONV_COLS), lambda i, j: (i, j)),
        out_shape=jax.ShapeDtypeStruct((rows, SSD_XBC), F32),
        compiler_params=pltpu.CompilerParams(dimension_semantics=("parallel", "parallel"), vmem_limit_bytes=VMEM_LIMIT),
    )(proj, proj, conv_w, conv_b[None])


def _ssd_conv_bwd(proj, d_xbc, conv_w, conv_b, name):
    rows = proj.shape[0]
    tr = _tile(rows, 512, CONV_HALO)
    nb, halos = rows // tr, tr // CONV_HALO
    first_col = SSD_DINNER // CONV_COLS

    def body(x_ref, before_ref, after_ref, d_ref, d_after_ref, w_ref, b_ref, dx_ref, dw_ref, db_ref):
        i = pl.program_id(1)

        @pl.when(i == 0)
        def _():
            dw_ref[...] = jnp.zeros_like(dw_ref)
            db_ref[...] = jnp.zeros_like(db_ref)

        before = jnp.where(i == 0, 0.0, before_ref[...])
        taps = _conv_taps(jnp.concatenate([before, x_ref[...], after_ref[...]], axis=0), tr + CONV_HALO)
        out = b_ref[...]
        for k in range(SSD_CONV):
            out = out + taps[k] * w_ref[k:k + 1, :]
        sig = 1.0 / (1.0 + jnp.exp(-out))
        d_after = jnp.where(i == nb - 1, 0.0, d_after_ref[...])
        d_out = jnp.concatenate([d_ref[...], d_after], axis=0) * sig * (1.0 + out * (1.0 - sig))
        d_x = d_out[:tr] * w_ref[SSD_CONV - 1:SSD_CONV, :]
        for k in range(SSD_CONV - 1):
            ahead = SSD_CONV - 1 - k
            d_x = d_x + pltpu.roll(d_out, tr + CONV_HALO - ahead, 0)[:tr] * w_ref[k:k + 1, :]
        dx_ref[...] = d_x.astype(dx_ref.dtype)
        for k in range(SSD_CONV):
            dw_ref[k:k + 1, :] += jnp.sum(d_out[:tr] * taps[k][:tr], axis=0, keepdims=True)
        db_ref[...] += jnp.sum(d_out[:tr], axis=0, keepdims=True)

    before = lambda j, i: jnp.maximum(i * halos - 1, 0)
    after = lambda j, i: jnp.minimum((i + 1) * halos, nb * halos - 1)
    d_x, d_w, d_b = pl.pallas_call(
        body, name=name, grid=(SSD_XBC // CONV_COLS, nb),
        in_specs=[pl.BlockSpec((tr, CONV_COLS), lambda j, i: (i, first_col + j)),
                  pl.BlockSpec((CONV_HALO, CONV_COLS), lambda j, i: (before(j, i), first_col + j)),
                  pl.BlockSpec((CONV_HALO, CONV_COLS), lambda j, i: (after(j, i), first_col + j)),
                  pl.BlockSpec((tr, CONV_COLS), lambda j, i: (i, j)),
                  pl.BlockSpec((CONV_HALO, CONV_COLS), lambda j, i: (after(j, i), j)),
                  pl.BlockSpec((SSD_CONV, CONV_COLS), lambda j, i: (0, j)), pl.BlockSpec((1, CONV_COLS), lambda j, i: (0, j))],
        out_specs=[pl.BlockSpec((tr, CONV_COLS), lambda j, i: (i, j)), pl.BlockSpec((SSD_CONV, CONV_COLS), lambda j, i: (0, j)),
                   pl.BlockSpec((1, CONV_COLS), lambda j, i: (0, j))],
        out_shape=[jax.ShapeDtypeStruct((rows, SSD_XBC), BF16), jax.ShapeDtypeStruct((SSD_CONV, SSD_XBC), F32),
                   jax.ShapeDtypeStruct((1, SSD_XBC), F32)],
        compiler_params=pltpu.CompilerParams(dimension_semantics=("parallel", "arbitrary"), vmem_limit_bytes=VMEM_LIMIT),
    )(proj, proj, proj, d_xbc, d_xbc, conv_w, conv_b[None])
    return d_x, d_w, d_b[0]


def _s5_boundary_scan(z, lam_re, lam_im, name, reverse=False):
    n_chunks, groups, width = z.shape
    tn = _tile(n_chunks, 128, 1)
    blocks = n_chunks // tn
    lam_a = jnp.concatenate([lam_re, lam_re], axis=1)
    lam_b = jnp.concatenate([-lam_im, lam_im], axis=1)

    def body(z_ref, a_ref, b_ref, x_ref, carry_ref):
        @pl.when(pl.program_id(0) == 0)
        def _():
            carry_ref[...] = jnp.zeros_like(carry_ref)

        a, b = a_ref[...], b_ref[...]

        def step(i, x):
            n = tn - 1 - i if reverse else i
            x_ref[n] = x
            return a * x + b * pltpu.roll(x, width // 2, 1) + z_ref[n]

        carry_ref[...] = lax.fori_loop(0, tn, step, carry_ref[...])

    block = pl.BlockSpec((tn, groups, width), (lambda i: (blocks - 1 - i, 0, 0)) if reverse else (lambda i: (i, 0, 0)))
    return pl.pallas_call(
        body, name=name, grid=(blocks,), in_specs=[block, _full_spec((groups, width)), _full_spec((groups, width))],
        out_specs=block, out_shape=jax.ShapeDtypeStruct(z.shape, F32), scratch_shapes=[pltpu.VMEM((groups, width), F32)],
        compiler_params=pltpu.CompilerParams(dimension_semantics=("arbitrary",), vmem_limit_bytes=VMEM_LIMIT),
    )(z, lam_a, lam_b)


_FLIPS = [(kx, ky, kc) for kx in (0, 1) for ky in (0, 1) for kc in (0, 1)][1:]


def _mesh_position():
    return lax.axis_index("x"), lax.axis_index("y"), lax.axis_index("c")


def _peer(pos, flip):
    return tuple((1 - p) if f else p for p, f in zip(pos, flip))


def _index(pos):
    return 4 * pos[0] + 2 * pos[1] + pos[2]


_ANY = pl.BlockSpec(memory_space=pl.ANY)


def _moved_shape(kind, x):
    return jax.ShapeDtypeStruct(((N_DEV,) + x.shape) if kind == "gather" else x.shape, x.dtype)


def _moves(kind, x_ref, out_ref, send_sems, recv_sems, local_sem):
    me = _mesh_position()
    source = (lambda pos: x_ref) if kind == "gather" else (lambda pos: x_ref.at[_index(pos)])
    local = pltpu.make_async_copy(source(me), out_ref.at[_index(me)], local_sem)
    outgoing, incoming = [], []
    for k, flip in enumerate(_FLIPS):
        peer = _peer(me, flip)
        copy = lambda slot: pltpu.make_async_remote_copy(
            src_ref=source(peer), dst_ref=out_ref.at[_index(slot)], send_sem=send_sems.at[k], recv_sem=recv_sems.at[k],
            device_id=peer, device_id_type=pl.DeviceIdType.MESH)
        outgoing.append(copy(me))
        incoming.append(copy(peer))
    return local, outgoing, incoming


def _start(moves):
    local, outgoing, _ = moves
    local.start()
    for cp in outgoing:
        cp.start()


def _finish(moves):
    local, outgoing, incoming = moves
    for cp in incoming:
        cp.wait_recv()
    for cp in outgoing:
        cp.wait_send()
    local.wait()


def _collective(kind, x, name):
    def body(x_ref, out_ref, send_sems, recv_sems, local_sem):
        moves = _moves(kind, x_ref, out_ref, send_sems, recv_sems, local_sem)
        _start(moves)
        _finish(moves)

    return pl.pallas_call(
        body, name=name, in_specs=[_ANY], out_specs=_ANY, out_shape=_moved_shape(kind, x),
        scratch_shapes=[pltpu.SemaphoreType.DMA((N_DEV - 1,)), pltpu.SemaphoreType.DMA((N_DEV - 1,)), pltpu.SemaphoreType.DMA],
        compiler_params=pltpu.CompilerParams(has_side_effects=True),
    )(x)


def _adamw(parts, w, m, v, name):
    n_parts = parts.shape[0]
    layers, rows, cols = w.shape
    tr = _tile(rows, 256, 8)

    def body(p_ref, w_ref, m_ref, v_ref, g_ref, d_ref, mo_ref, vo_ref):
        g = p_ref[0].astype(F32)
        for s in range(1, n_parts):
            g = g + p_ref[s].astype(F32)
        m_new = ADAM_B1 * m_ref[...] + (1.0 - ADAM_B1) * g
        v_new = ADAM_B2 * v_ref[...] + (1.0 - ADAM_B2) * (g * g)
        m_hat = m_new / (1.0 - ADAM_B1 ** ADAM_STEP)
        v_hat = v_new / (1.0 - ADAM_B2 ** ADAM_STEP)
        g_ref[...] = g
        d_ref[...] = -ADAM_LR * (m_hat / (jnp.sqrt(v_hat) + ADAM_EPS) + ADAM_WD * w_ref[...])
        mo_ref[...] = m_new
        vo_ref[...] = v_new

    blk = pl.BlockSpec((None, tr, cols), lambda l, i: (l, i, 0))
    shape = jax.ShapeDtypeStruct(w.shape, F32)
    return pl.pallas_call(
        body, name=name, grid=(layers, rows // tr),
        in_specs=[pl.BlockSpec((n_parts, None, tr, cols), lambda l, i: (0, l, i, 0)), blk, blk, blk],
        out_specs=[blk, blk, blk, blk], out_shape=[shape, shape, shape, shape],
        compiler_params=pltpu.CompilerParams(dimension_semantics=("parallel", "parallel"), vmem_limit_bytes=VMEM_LIMIT),
    )(parts, w, m, v)


def _sum_parts(parts, name):
    _, rows, cols = parts.shape
    tr = _tile(rows, 256, 8)

    def body(p_ref, o_ref):
        total = p_ref[0]
        for s in range(1, N_DEV):
            total = total + p_ref[s]
        o_ref[...] = total

    return pl.pallas_call(
        body, name=name, grid=(rows // tr,),
        in_specs=[pl.BlockSpec((N_DEV, tr, cols), lambda i: (0, i, 0))], out_specs=pl.BlockSpec((tr, cols), lambda i: (i, 0)),
        out_shape=jax.ShapeDtypeStruct((rows, cols), parts.dtype),
        compiler_params=pltpu.CompilerParams(dimension_semantics=("parallel",), vmem_limit_bytes=VMEM_LIMIT),
    )(parts)


def _row_tile(rows):
    return _tile(rows, 512, 16)


def _row_spec(rows, cols, block=0):
    return pl.BlockSpec((_row_tile(rows), cols), lambda i: (i, block))


def _rows_params(accumulates):
    return pltpu.CompilerParams(dimension_semantics=("arbitrary" if accumulates else "parallel",),
                                vmem_limit_bytes=VMEM_LIMIT)


def _swiglu_fwd(gu, name):
    rows = gu.shape[0]

    def body(g_ref, u_ref, o_ref):
        o_ref[...] = (_silu(g_ref[...].astype(F32)) * u_ref[...].astype(F32)).astype(o_ref.dtype)

    return pl.pallas_call(
        body, name=name, grid=(rows // _row_tile(rows),),
        in_specs=[_row_spec(rows, FFN_HIDDEN, 0), _row_spec(rows, FFN_HIDDEN, 1)], out_specs=_row_spec(rows, FFN_HIDDEN),
        out_shape=jax.ShapeDtypeStruct((rows, FFN_HIDDEN), BF16), compiler_params=_rows_params(False))(gu, gu)


def _swiglu_bwd(gu, d_act, name):
    rows = gu.shape[0]

    def body(g_ref, u_ref, d_ref, o_ref):
        g, u, d = g_ref[...].astype(F32), u_ref[...].astype(F32), d_ref[...].astype(F32)
        sig = 1.0 / (1.0 + jnp.exp(-g))
        o_ref[:, :FFN_HIDDEN] = (d * u * sig * (1.0 + g * (1.0 - sig))).astype(o_ref.dtype)
        o_ref[:, FFN_HIDDEN:] = (d * g * sig).astype(o_ref.dtype)

    return pl.pallas_call(
        body, name=name, grid=(rows // _row_tile(rows),),
        in_specs=[_row_spec(rows, FFN_HIDDEN, 0), _row_spec(rows, FFN_HIDDEN, 1), _row_spec(rows, FFN_HIDDEN)],
        out_specs=_row_spec(rows, 2 * FFN_HIDDEN),
        out_shape=jax.ShapeDtypeStruct((rows, 2 * FFN_HIDDEN), BF16), compiler_params=_rows_params(False))(gu, gu, d_act)


def _add_norm_fwd(h, y, gain, name):
    rows = h.shape[0]

    def body(*refs):
        if y is None:
            h_ref, g_ref, n_ref = refs
            x = h_ref[...]
        else:
            h_ref, y_ref, g_ref, s_ref, n_ref = refs
            x = h_ref[...] + y_ref[...]
            s_ref[...] = x
        n_ref[...] = (x * lax.rsqrt(jnp.mean(x * x, axis=-1, keepdims=True) + EPS) * g_ref[...]).astype(n_ref.dtype)

    row = _row_spec(rows, D_MODEL)
    ins = [h] if y is None else [h, y]
    out_shape = [jax.ShapeDtypeStruct((rows, D_MODEL), BF16)]
    if y is not None:
        out_shape = [jax.ShapeDtypeStruct((rows, D_MODEL), F32)] + out_shape
    res = pl.pallas_call(
        body, name=name, grid=(rows // _row_tile(rows),),
        in_specs=[row] * len(ins) + [_full_spec((1, D_MODEL))], out_specs=[row] * len(out_shape), out_shape=out_shape,
        compiler_params=_rows_params(False))(*ins, gain[None])
    return (h, res[0]) if y is None else (res[0], res[1])


def _norm_bwd(x, gain, d_n, d_skip, name):
    rows = x.shape[0]

    def body(x_ref, g_ref, dn_ref, ds_ref, dx_ref, dg_ref):
        @pl.when(pl.program_id(0) == 0)
        def _():
            dg_ref[...] = jnp.zeros_like(dg_ref)

        x, dn = x_ref[...], dn_ref[...].astype(F32)
        r = lax.rsqrt(jnp.mean(x * x, axis=-1, keepdims=True) + EPS)
        gd = g_ref[...] * dn
        dx_ref[...] = r * gd - x * (r * r * r) * jnp.mean(x * gd, axis=-1, keepdims=True) + ds_ref[...]
        dg_ref[...] += jnp.sum(x * r * dn, axis=0, keepdims=True)

    row = _row_spec(rows, D_MODEL)
    dx, dg = pl.pallas_call(
        body, name=name, grid=(rows // _row_tile(rows),),
        in_specs=[row, _full_spec((1, D_MODEL)), row, row], out_specs=[row, _full_spec((1, D_MODEL))],
        out_shape=[jax.ShapeDtypeStruct((rows, D_MODEL), F32), jax.ShapeDtypeStruct((1, D_MODEL), F32)],
        compiler_params=_rows_params(True))(x, gain[None], d_n, d_skip)
    return dx, dg[0]


def _loss_head(h, gain, target, name):
    rows = h.shape[0]

    def body(x_ref, g_ref, t_ref, loss_ref, dx_ref, dg_ref):
        @pl.when(pl.program_id(0) == 0)
        def _():
            loss_ref[...] = jnp.zeros_like(loss_ref)
            dg_ref[...] = jnp.zeros_like(dg_ref)

        x = x_ref[...]
        r = lax.rsqrt(jnp.mean(x * x, axis=-1, keepdims=True) + EPS)
        err = x * r * g_ref[...] - t_ref[...]
        loss_ref[...] += 0.5 * jnp.sum(jnp.mean(err * err, axis=-1, keepdims=True), axis=0, keepdims=True)
        dy = err * (1.0 / D_MODEL)
        gd = g_ref[...] * dy
        dx_ref[...] = r * gd - x * (r * r * r) * jnp.mean(x * gd, axis=-1, keepdims=True)
        dg_ref[...] += jnp.sum(x * r * dy, axis=0, keepdims=True)

    row = _row_spec(rows, D_MODEL)
    loss, dx, dg = pl.pallas_call(
        body, name=name, grid=(rows // _row_tile(rows),),
        in_specs=[row, _full_spec((1, D_MODEL)), row], out_specs=[_full_spec((1, 1)), row, _full_spec((1, D_MODEL))],
        out_shape=[jax.ShapeDtypeStruct((1, 1), F32), jax.ShapeDtypeStruct((rows, D_MODEL), F32),
                   jax.ShapeDtypeStruct((1, D_MODEL), F32)],
        compiler_params=_rows_params(True))(h, gain[None], target)
    return loss[0, 0], dx, dg[0]


def _join_cols(blocks, n_out, name):
    _, layers, rows, n = blocks.shape
    tr = _tile(rows, 256, 16)

    def body(x_ref, o_ref):
        for d in range(N_DEV):
            o_ref[:, d * n:(d + 1) * n] = x_ref[d]
        if n_out > N_DEV * n:
            o_ref[:, N_DEV * n:] = jnp.zeros((tr, n_out - N_DEV * n), o_ref.dtype)

    return pl.pallas_call(
        body, name=name, grid=(layers, rows // tr),
        in_specs=[pl.BlockSpec((N_DEV, None, tr, n), lambda l, i: (0, l, i, 0))],
        out_specs=pl.BlockSpec((None, tr, n_out), lambda l, i: (l, i, 0)),
        out_shape=jax.ShapeDtypeStruct((layers, rows, n_out), blocks.dtype),
        compiler_params=pltpu.CompilerParams(dimension_semantics=("parallel", "parallel"), vmem_limit_bytes=VMEM_LIMIT),
    )(blocks)


def _split_cols(full, n, name):
    rows = full.shape[0]
    tr = _tile(rows, 256, 16)

    def body(x_ref, o_ref):
        for d in range(N_DEV):
            o_ref[d] = x_ref[:, d * n:(d + 1) * n]

    return pl.pallas_call(
        body, name=name, grid=(rows // tr,),
        in_specs=[pl.BlockSpec((tr, full.shape[1]), lambda i: (i, 0))],
        out_specs=pl.BlockSpec((N_DEV, tr, n), lambda i: (0, i, 0)),
        out_shape=jax.ShapeDtypeStruct((N_DEV, rows, n), full.dtype),
        compiler_params=pltpu.CompilerParams(dimension_semantics=("parallel",), vmem_limit_bytes=VMEM_LIMIT),
    )(full)


def _pack(arrays):
    flat = jnp.concatenate([a.reshape(-1) for a in arrays])
    unit = FLAT_COLS * FLAT_ROWS_ALIGN
    padded = -(-flat.shape[0] // unit) * unit
    return jnp.pad(flat, (0, padded - flat.shape[0])).reshape(-1, FLAT_COLS)


def _unpack(flat, shapes, lead=()):
    flat = flat.reshape(lead + (-1,))
    out, off = [], 0
    for shape in shapes:
        n = math.prod(shape)
        out.append(flat[..., off:off + n].reshape(lead + tuple(shape)))
        off += n
    return out


def _join(blocks, axis):
    moved = jnp.moveaxis(blocks, 0, axis)
    shape = list(moved.shape)
    shape[axis:axis + 2] = [shape[axis] * shape[axis + 1]]
    return moved.reshape(shape)


def _own_shard(full, axis, position):
    n = full.shape[axis] // N_DEV
    return lax.dynamic_slice_in_dim(full, position * n, n, axis)


def _s5_operators(log_dt, a_re, a_im, b_re, b_im, c_re, c_im):
    t = S5_CHUNK
    hi = lax.Precision.HIGHEST
    step = jnp.exp(log_dt)[:, None]
    mag = jnp.exp(step * a_re)
    abar_re = mag * jnp.cos(step * a_im)
    abar_im = mag * jnp.sin(step * a_im)
    den = a_re * a_re + a_im * a_im
    f_re = ((abar_re - 1.0) * a_re + abar_im * a_im) / den
    f_im = (abar_im * a_re - (abar_re - 1.0) * a_im) / den
    bb_re = f_re[..., None] * b_re - f_im[..., None] * b_im
    bb_im = f_re[..., None] * b_im + f_im[..., None] * b_re
    j = jnp.arange(t + 1, dtype=F32)[:, None, None]
    pmag = jnp.exp(j * (step * a_re))
    pw_re = pmag * jnp.cos(j * (step * a_im))
    pw_im = pmag * jnp.sin(j * (step * a_im))
    cl_re = c_re[None] * pw_re[:t, :, None, :] - c_im[None] * pw_im[:t, :, None, :]
    cl_im = c_re[None] * pw_im[:t, :, None, :] + c_im[None] * pw_re[:t, :, None, :]
    kern = (jnp.einsum('jgcp,gpk->jgck', cl_re, bb_re, precision=hi)
            - jnp.einsum('jgcp,gpk->jgck', cl_im, bb_im, precision=hi))
    rp_re, rp_im = pw_re[:t][::-1], pw_im[:t][::-1]
    wz_re = rp_re[:, :, :, None] * bb_re[None] - rp_im[:, :, :, None] * bb_im[None]
    wz_im = rp_re[:, :, :, None] * bb_im[None] + rp_im[:, :, :, None] * bb_re[None]
    w_z = jnp.concatenate([wz_re, wz_im], axis=2).transpose(1, 0, 3, 2).reshape(S5_GROUPS, t * S5_GROUP, 2 * S5_STATE)
    cy_re = c_re[None] * pw_re[1:, :, None, :] - c_im[None] * pw_im[1:, :, None, :]
    cy_im = c_re[None] * pw_im[1:, :, None, :] + c_im[None] * pw_re[1:, :, None, :]
    w_y = jnp.concatenate([cy_re, -cy_im], axis=3).transpose(1, 3, 0, 2).reshape(S5_GROUPS, 2 * S5_STATE, t * S5_GROUP)
    return kern, w_z, w_y, pw_re[t], pw_im[t]


def _s5_lag_selector():
    t = S5_CHUNK
    lag = jnp.arange(t)[:, None] - jnp.arange(t)[None, :]
    return (lag[:, :, None] == jnp.arange(t)[None, None, :]).astype(F32).reshape(t * t, t)


def _s5_toeplitz(kern, tag):
    t = S5_CHUNK
    sel = _s5_lag_selector()
    flat = _matmul(sel, kern.reshape(t, -1), out_dtype=BF16, name=tag + "_toeplitz")
    toep = flat.reshape(t, t, S5_GROUPS, S5_GROUP, S5_GROUP).transpose(2, 1, 4, 0, 3)
    toep = toep.reshape(S5_GROUPS, t * S5_GROUP, t * S5_GROUP)

    def backward(d_toep):
        d_flat = d_toep.reshape(S5_GROUPS, t, S5_GROUP, t, S5_GROUP).transpose(3, 1, 0, 4, 2).reshape(t * t, -1)
        return _matmul(sel, d_flat, ta=True, name=tag + "_toeplitz_dw").reshape(kern.shape)

    return toep, backward


def _s5_gate(y, u, d_skip):
    return jax.nn.gelu(y + d_skip * u)


def _glu(vg):
    return vg[:, :D_MODEL] * jax.nn.sigmoid(vg[:, D_MODEL:])


_BIG = [("gla_w_in", 2), ("gla_w_out", 1), ("ssd_w_in", 2), ("ssd_w_out", 1), ("s5_w_glu", 2), ("ffn_w_gu", 2),
        ("ffn_w_down", 1)]
_PADDED_COLS = {"gla_w_in": GLA_PROJ, "ssd_w_in": SSD_PROJ}


class _Traffic:
    LINK_BYTES_PER_SECOND = 7.0e10
    MATMUL_FLOPS = 7.0e14

    def __init__(self, shards, plan):
        self.shards, self.plan = shards, plan
        self.position = 0
        self.queue = []
        self.weights, self.received = {}, {}
        self.standalone = self.serial = 0
        for key in plan:
            self._request(key)

    def _request(self, key):
        shard = self.shards[key]
        seconds = (N_DEV - 1) * shard.size * shard.dtype.itemsize / self.LINK_BYTES_PER_SECOND
        self._enqueue("gather", shard, seconds, key, lambda blocks: self.weights.__setitem__(key, self._assemble(key, blocks)))

    def _enqueue(self, kind, x, seconds, key, deliver):
        self.queue.append((kind, x, seconds, key, deliver, self.serial))
        self.serial += 1

    @staticmethod
    def _assemble(key, blocks):
        name, layer = key
        if dict(_BIG)[name] == 1:
            return blocks.reshape((N_DEV * blocks.shape[1], blocks.shape[2]))
        n_out = _PADDED_COLS.get(name, N_DEV * blocks.shape[2])
        return _join_cols(blocks[:, None], n_out, f"join_{name}_{layer}")[0]

    def take(self, key):
        assert key == self.plan[self.position], (key, self.plan[self.position])
        self.position += 1
        while key not in self.weights:
            self._alone(self.queue.pop(0))
        return self.weights[key]

    def run(self, seconds, call, more_carriers_follow=False):
        riders, waiting, left = [], [], seconds
        for item in self.queue:
            if item[2] <= left:
                riders.append(item)
                left -= item[2]
            else:
                waiting.append(item)
        due = [item for item in waiting if item[3] is not None and self.position < len(self.plan)
               and item[3] == self.plan[self.position]]
        if due and not more_carriers_follow:
            left = seconds - due[0][2]
            kept = []
            for item in riders:
                if item[2] <= left:
                    kept.append(item)
                    left -= item[2]
                else:
                    waiting.append(item)
            riders = due + kept
            waiting = [item for item in waiting if item is not due[0]]
            waiting.sort(key=lambda item: item[5])
        self.queue = waiting
        if not riders:
            return call(())
        results, moved = call([(kind, x) for kind, x, *_ in riders])
        for item, y in zip(riders, moved):
            item[4](y)
        return results

    def matmul(self, a, b, more_carriers_follow=False, **kw):
        m, n = (a.shape[-1] if kw.get("ta") else a.shape[-2]), (b.shape[-2] if kw.get("tb") else b.shape[-1])
        k = a.shape[-2] if kw.get("ta") else a.shape[-1]
        return self.run(2.0 * m * n * k / self.MATMUL_FLOPS, lambda cargo: _matmul(a, b, cargo=cargo, **kw),
                        more_carriers_follow)

    def send_gradient(self, key, dw):
        name, layer = key
        shard = self.shards[key]
        if dict(_BIG)[name] == 1:
            blocks = dw.reshape((N_DEV,) + shard.shape)
        else:
            blocks = _split_cols(dw, shard.shape[1], f"split_{name}_{layer}")
        seconds = (N_DEV - 1) * shard.size * shard.dtype.itemsize / self.LINK_BYTES_PER_SECOND
        self._enqueue("exchange", blocks, seconds, None, lambda parts: self.received.__setitem__(key, parts))

    def _alone(self, item):
        kind, x, _, _, deliver, _ = item
        deliver(_collective(kind, x, f"{kind}_alone_{self.standalone}"))
        self.standalone += 1

    def flush(self):
        for item in self.queue:
            self._alone(item)
        self.queue = []


_GLA_FWD_SECONDS, _GLA_BWD_SECONDS, _SSD_FWD_SECONDS, _SSD_BWD_SECONDS = 1.25e-6, 3.4e-6, 3.5e-6, 14e-6


def _linear(x, w, tag, out_dtype=F32, dx_dtype=F32, more_carriers_follow=False):
    traffic, key = w
    weight = traffic.take(key)
    y = traffic.matmul(x, weight, more_carriers_follow, out_dtype=out_dtype, name=tag + "_fwd")

    def backward(dy):
        dx = traffic.matmul(dy, weight, tb=True, out_dtype=dx_dtype, name=tag + "_dx")
        traffic.send_gradient(key, traffic.matmul(x, dy, ta=True, out_dtype=BF16, name=tag + "_dw"))
        return dx

    return y, backward


def _gla_mixer(hn, p, tag):
    traffic, chunks = p["w_in"][0], hn.shape[0] // CHUNK
    w_a2 = jnp.pad(p["w_a2"], ((0, LANES - GLA_RANK), (0, 0)))
    b_a, norm_g = p["b_a"][None], p["norm_g"][None]
    proj, lin_in = _linear(hn, p["w_in"], tag + "_in", more_carriers_follow=True)
    o, sprev = traffic.run(chunks * _GLA_FWD_SECONDS, lambda cargo: _gla_core_fwd(proj, w_a2, b_a, norm_g, cargo))
    y, lin_out = _linear(o, p["w_out"], tag + "_out")

    def backward(dy):
        d_o = lin_out(dy)
        d_proj, d_wa, d_ba, d_ng = traffic.run(chunks * _GLA_BWD_SECONDS,
                                               lambda cargo: _gla_core_bwd(proj, sprev, d_o, w_a2, b_a, norm_g, cargo))
        return lin_in(d_proj), dict(w_a2=d_wa[:GLA_RANK], b_a=d_ba[0], norm_g=d_ng[0])

    return y, backward


def _ssd_mixer(hn, p, tag):
    pad = lambda a: jnp.pad(a[None], ((0, 0), (0, LANES - SSD_HEADS)))
    dt_bias, a_log, d_skip, norm_g = pad(p["dt_bias"]), pad(p["a_log"]), pad(p["d"]), p["norm_g"][None]
    traffic, chunks = p["w_in"][0], hn.shape[0] // CHUNK
    proj, lin_in = _linear(hn, p["w_in"], tag + "_in", more_carriers_follow=True)
    xbc = _ssd_conv_fwd(proj, p["conv_w"], p["conv_b"], tag + "_conv")
    o, hprev = traffic.run(chunks * _SSD_FWD_SECONDS,
                           lambda cargo: _ssd_core_fwd(proj, xbc, dt_bias, a_log, d_skip, norm_g, cargo))
    y, lin_out = _linear(o, p["w_out"], tag + "_out")

    def backward(dy):
        d_o = lin_out(dy)
        d_z, d_xbc, d_dt, d_db, d_al, d_ds, d_ng = traffic.run(
            chunks * _SSD_BWD_SECONDS, lambda cargo: _ssd_core_bwd(proj, xbc, hprev, d_o, dt_bias, a_log, d_skip, norm_g, cargo))
        d_pre, d_cw, d_cb = _ssd_conv_bwd(proj, d_xbc, p["conv_w"], p["conv_b"], tag + "_conv_bwd")
        d_hn = lin_in(jnp.concatenate([d_z, d_pre, d_dt], axis=1))
        return d_hn, dict(conv_w=d_cw, conv_b=d_cb, dt_bias=d_db[0, :SSD_HEADS], a_log=d_al[0, :SSD_HEADS],
                          d=d_ds[0, :SSD_HEADS], norm_g=d_ng[0])

    return y, backward


def _s5_mixer(hn, p, tag):
    seq = hn.shape[0]
    t, n_chunks = S5_CHUNK, hn.shape[0] // S5_CHUNK
    names = ("log_dt", "a_re", "a_im", "b_re", "b_im", "c_re", "c_im")
    (kern, w_z, w_y, lam_re, lam_im), ops_vjp = jax.vjp(_s5_operators, *[p[k] for k in names])
    toep, toep_bwd = _s5_toeplitz(kern, tag)
    to_groups = lambda a: a.reshape(n_chunks, t, S5_GROUPS, S5_GROUP).transpose(2, 0, 1, 3).reshape(S5_GROUPS, n_chunks, t * S5_GROUP)
    from_groups = lambda a: a.reshape(S5_GROUPS, n_chunks, t, S5_GROUP).transpose(1, 2, 0, 3).reshape(seq, D_MODEL)
    ug = to_groups(hn)
    z = _matmul(ug, w_z, name=tag + "_z")
    x_before = _s5_boundary_scan(z.transpose(1, 0, 2), lam_re, lam_im, tag + "_scan")
    xprev = x_before.transpose(1, 0, 2)
    yg = (_matmul(ug, toep, name=tag + "_intra") + _matmul(xprev, w_y, name=tag + "_inter")).astype(BF16)
    act, gate_vjp = jax.vjp(_s5_gate, from_groups(yg), hn, p["d"])
    vg, lin_glu = _linear(act, p["w_glu"], tag + "_glu")
    out, glu_vjp = jax.vjp(_glu, vg)

    def backward(dy):
        d_vg, = glu_vjp(dy)
        d_act = lin_glu(d_vg)
        d_y, d_hn, d_d = gate_vjp(d_act)
        d_yg = to_groups(d_y)
        d_ug = _matmul(d_yg, toep, tb=True, name=tag + "_intra_dx")
        d_toep = _matmul(ug, d_yg, ta=True, out_dtype=BF16, name=tag + "_intra_dw")
        d_xprev = _matmul(d_yg, w_y, tb=True, name=tag + "_inter_dx").transpose(1, 0, 2)
        d_wy = _matmul(xprev, d_yg, ta=True, name=tag + "_inter_dw")
        dz = _s5_boundary_scan(d_xprev, lam_re, -lam_im, tag + "_scan_bwd", reverse=True)
        x_re, x_im, dz_re, dz_im = (x_before[..., :S5_STATE], x_before[..., S5_STATE:], dz[..., :S5_STATE],
                                    dz[..., S5_STATE:])
        d_lam_re = jnp.sum(x_re * dz_re + x_im * dz_im, axis=0)
        d_lam_im = jnp.sum(x_re * dz_im - x_im * dz_re, axis=0)
        d_z = dz.transpose(1, 0, 2)
        d_ug = d_ug + _matmul(d_z, w_z, tb=True, name=tag + "_z_dx")
        d_wz = _matmul(ug, d_z, ta=True, name=tag + "_z_dw")
        grads = dict(zip(names, ops_vjp((toep_bwd(d_toep), d_wz, d_wy, d_lam_re, d_lam_im))))
        grads.update(d=d_d)
        return d_hn + from_groups(d_ug), grads

    return out, backward


_SMALL =[("gla_w_a2", 2), ("gla_b_a", 1), ("gla_norm_g", 1), ("ssd_conv_w", 2), ("s5_d", 1)]
_REPLICATED = ["norm_mix_g", "norm_ffn_g", "ssd_conv_b", "ssd_dt_bias", "ssd_a_log", "ssd_d", "ssd_norm_g", "s5_log_dt",
               "s5_a_re", "s5_a_im", "s5_b_re", "s5_b_im", "s5_c_re", "s5_c_im", "final_norm_g"]
_WEIGHTS = ['norm_mix_g', 'norm_ffn_g', 'gla_w_in', 'gla_w_a2', 'gla_b_a', 'gla_norm_g', 'gla_w_out', 'ssd_w_in',
            'ssd_conv_w', 'ssd_conv_b', 'ssd_dt_bias', 'ssd_a_log', 'ssd_d', 'ssd_norm_g', 'ssd_w_out', 's5_log_dt',
            's5_a_re', 's5_a_im', 's5_b_re', 's5_b_im', 's5_c_re', 's5_c_im', 's5_d', 's5_w_glu', 'ffn_w_gu', 'ffn_w_down',
            'final_norm_g']


def _gather_small(local):
    shapes = [local[n].shape for n, _ in _SMALL]
    blocks = _collective("gather", _pack([local[n] for n, _ in _SMALL]), "gather_vectors")
    parts = _unpack(blocks, shapes, lead=(N_DEV,))
    return {n: _join(part, axis) for (n, axis), part in zip(_SMALL, parts)}


def _forward_plan():
    plan = []
    for i in range(DEPTH):
        j = i // 3
        plan += [[("gla_w_in", j), ("gla_w_out", j)], [("ssd_w_in", j), ("ssd_w_out", j)], [("s5_w_glu", j)]][i % 3]
        plan += [("ffn_w_gu", i), ("ffn_w_down", i)]
    return plan


def _forward_backward(x, target, w, traffic):
    big = lambda name, j: (traffic, (name, j))
    gla = lambda j: dict(w_in=big("gla_w_in", j), w_a2=w["gla_w_a2"][j], b_a=w["gla_b_a"][j], norm_g=w["gla_norm_g"][j],
                         w_out=big("gla_w_out", j))
    ssd = lambda j: dict(w_in=big("ssd_w_in", j), conv_w=w["ssd_conv_w"][j], conv_b=w["ssd_conv_b"][j],
                         dt_bias=w["ssd_dt_bias"][j], a_log=w["ssd_a_log"][j], d=w["ssd_d"][j], norm_g=w["ssd_norm_g"][j],
                         w_out=big("ssd_w_out", j))
    s5 = lambda j: dict(log_dt=w["s5_log_dt"][j], a_re=w["s5_a_re"][j], a_im=w["s5_a_im"][j], b_re=w["s5_b_re"][j],
                        b_im=w["s5_b_im"][j], c_re=w["s5_c_re"][j], c_im=w["s5_c_im"][j], d=w["s5_d"][j],
                        w_glu=big("s5_w_glu", j))
    mixers = [("gla", _gla_mixer, gla), ("ssd", _ssd_mixer, ssd), ("s5", _s5_mixer, s5)]
    base, delta = x, None
    tape = []
    for i in range(DEPTH):
        kind, mixer, params = mixers[i % 3]
        j = i // 3
        h, hn = _add_norm_fwd(base, delta, w["norm_mix_g"][i], f"l{i}_norm_mix")
        y, mixer_bwd = mixer(hn, params(j), f"l{i}_{kind}")
        h_mid, hn2 = _add_norm_fwd(h, y, w["norm_ffn_g"][i], f"l{i}_norm_ffn")
        gu, gu_bwd = _linear(hn2, big("ffn_w_gu", i), f"l{i}_ffn_gu", out_dtype=BF16)
        act = _swiglu_fwd(gu, f"l{i}_swiglu")
        delta, down_bwd = _linear(act, big("ffn_w_down", i), f"l{i}_ffn_down", dx_dtype=BF16)
        base = h_mid
        tape.append((kind, j, h, mixer_bwd, h_mid, gu_bwd, gu, down_bwd))
    loss, d_h, d_final_g = _loss_head(base + delta, w["final_norm_g"], target, "loss_head")

    grads = {n: [None] * w[n].shape[0] for n in w if n != "final_norm_g"}
    grads["final_norm_g"] = d_final_g
    for i in reversed(range(DEPTH)):
        kind, j, h, mixer_bwd, h_mid, gu_bwd, gu, down_bwd = tape[i]
        d_gu = _swiglu_bwd(gu, down_bwd(d_h), f"l{i}_swiglu_bwd")
        d_mid, grads["norm_ffn_g"][i] = _norm_bwd(h_mid, w["norm_ffn_g"][i], gu_bwd(d_gu), d_h, f"l{i}_norm_ffn_bwd")
        d_hn, mixer_grads = mixer_bwd(d_mid)
        for k, g in mixer_grads.items():
            grads[f"{kind}_{k}"][j] = g
        d_h, grads["norm_mix_g"][i] = _norm_bwd(h, w["norm_mix_g"][i], d_hn, d_mid, f"l{i}_norm_mix_bwd")
    return loss, d_h, grads


def kernel(x, norm_mix_g, norm_ffn_g, gla_w_in, gla_w_a2, gla_b_a, gla_norm_g, gla_w_out, ssd_w_in, ssd_conv_w, ssd_conv_b, ssd_dt_bias, ssd_a_log, ssd_d, ssd_norm_g, ssd_w_out, s5_log_dt, s5_a_re, s5_a_im, s5_b_re, s5_b_im, s5_c_re, s5_c_im, s5_d, s5_w_glu, ffn_w_gu, ffn_w_down, final_norm_g, loss_target, m_norm_mix_g, m_norm_ffn_g, m_gla_w_in, m_gla_w_a2, m_gla_b_a, m_gla_norm_g, m_gla_w_out, m_ssd_w_in, m_ssd_conv_w, m_ssd_conv_b, m_ssd_dt_bias, m_ssd_a_log, m_ssd_d, m_ssd_norm_g, m_ssd_w_out, m_s5_log_dt, m_s5_a_re, m_s5_a_im, m_s5_b_re, m_s5_b_im, m_s5_c_re, m_s5_c_im, m_s5_d, m_s5_w_glu, m_ffn_w_gu, m_ffn_w_down, m_final_norm_g, v_norm_mix_g, v_norm_ffn_g, v_gla_w_in, v_gla_w_a2, v_gla_b_a, v_gla_norm_g, v_gla_w_out, v_ssd_w_in, v_ssd_conv_w, v_ssd_conv_b, v_ssd_dt_bias, v_ssd_a_log, v_ssd_d, v_ssd_norm_g, v_ssd_w_out, v_s5_log_dt, v_s5_a_re, v_s5_a_im, v_s5_b_re, v_s5_b_im, v_s5_c_re, v_s5_c_im, v_s5_d, v_s5_w_glu, v_ffn_w_gu, v_ffn_w_down, v_final_norm_g):
    args = locals()
    local = {n: args[n] for n in _WEIGHTS}
    moment_m = {n: args["m_" + n] for n in _WEIGHTS}
    moment_v = {n: args["v_" + n] for n in _WEIGHTS}

    shards = {(n, layer): local[n][layer].astype(BF16) for n, _ in _BIG for layer in range(local[n].shape[0])}
    traffic = _Traffic(shards, _forward_plan())
    full = {n: local[n] for n in _REPLICATED}
    full.update(_gather_small(local))

    loss, d_x, grads = _forward_backward(x[0], loss_target[0], full, traffic)
    traffic.flush()
    loss = lax.psum(loss, ("x", "y", "c"))
    kinds = ("grad", "delta", "new_m", "new_v")
    out = {}

    for n, _ in _BIG:
        parts = jnp.stack([traffic.received[(n, layer)] for layer in range(local[n].shape[0])], axis=1)
        results = _adamw(parts, local[n], moment_m[n], moment_v[n], "adamw_" + n)
        out.update({f"{kind}_{n}": a for kind, a in zip(kinds, results)})

    small = [n for n, _ in _SMALL] + _REPLICATED
    stacked = lambda n: grads[n] if n == "final_norm_g" else jnp.stack(grads[n])
    parts = _collective("gather", _pack([stacked(n) for n in small]), "gather_small_gradients")
    summed = _unpack(_sum_parts(parts, "sum_small_gradients"), [stacked(n).shape for n in small])
    position = _index(_mesh_position())
    mine = [_own_shard(g, axis, position) for g, (_, axis) in zip(summed, _SMALL)] + summed[len(_SMALL):]
    shapes = [local[n].shape for n in small]
    pk = lambda arrays: _pack(arrays)[None]
    results = _adamw(pk(mine)[None], pk([local[n] for n in small]), pk([moment_m[n] for n in small]),
                     pk([moment_v[n] for n in small]), "adamw_small")
    for kind, flat in zip(kinds, results):
        out.update({f"{kind}_{n}": a for n, a in zip(small, _unpack(flat[0], shapes))})

    return (loss, d_x[None], *[out[f"{kind}_{n}"] for kind in ("grad", "delta", "new_m", "new_v") for n in _WEIGHTS])
```

```python
import functools
import math

import jax
import jax.numpy as jnp
import numpy as np
from jax import lax
from jax.experimental import pallas as pl
from jax.experimental.pallas import tpu as pltpu

F32 = jnp.float32
BF16 = jnp.bfloat16
_MXU_DTYPE = jnp.bfloat16

N_DEV = 8
D_MODEL = 1024
DEPTH = 4
CHUNK = 64
STEP_CHUNKS = 1
STEP = CHUNK * STEP_CHUNKS
EPS = 1e-6
GLA_HEADS, GLA_DK, GLA_DV, GLA_RANK, GLA_TAU = 4, 128, 256, 16, 16.0
GLA_QK = GLA_HEADS * GLA_DK
GLA_VD = GLA_HEADS * GLA_DV
LANES = 128
GLA_IN = 2 * GLA_QK + 2 * GLA_VD + GLA_RANK
GLA_PROJ = 2 * GLA_QK + 2 * GLA_VD + LANES
SSD_DINNER, SSD_HEADDIM, SSD_HEADS, SSD_GROUPS, SSD_HPG, SSD_DSTATE, SSD_CONV = 2048, 64, 32, 8, 4, 128, 4
SSD_GN = SSD_GROUPS * SSD_DSTATE
SSD_GW = SSD_HPG * SSD_HEADDIM
SSD_XBC = SSD_DINNER + 2 * SSD_GN
SSD_IN = SSD_DINNER + SSD_XBC + SSD_HEADS
SSD_PROJ = SSD_DINNER + SSD_XBC + LANES
S5_GROUP, S5_GROUPS, S5_STATE = 16, 64, 64
S5_CHUNK = 16
FFN_HIDDEN = 2816
ADAM_LR, ADAM_B1, ADAM_B2, ADAM_EPS, ADAM_WD, ADAM_STEP = 0.001, 0.9, 0.999, 1e-08, 0.01, 10
VMEM_LIMIT = 48 * 1024 * 1024
FLAT_COLS = 1024
FLAT_ROWS_ALIGN = 64


def _tile(n, cap, unit):
    if n <= cap:
        return n
    best = None
    for t in range(unit, cap + 1, unit):
        if n % t == 0:
            best = t
    assert best is not None, (n, cap, unit)
    return best


def _divisors(n, unit):
    return sorted({t for t in range(unit, n + 1, unit) if n % t == 0} | {n})


_MXU_FLOPS, _HBM_BYTES, _ACC_BYTES, _STEP_SECONDS = 1.1e15, 3e12, 1.1e13, 3.5e-7
_MXU_ROWS = 256
_TILE_VMEM_BUDGET = 36 * 1024 * 1024
_BATCH_VMEM_BUDGET = 16 * 1024 * 1024


def _pick_tiles(m, n, k, a_bytes, b_bytes, o_bytes, m_unit):
    best = None
    for tm in _divisors(m, m_unit):
        for tn in _divisors(n, LANES):
            for tk in _divisors(k, LANES):
                nk = k // tk
                vmem = 2 * tm * tk * a_bytes + 2 * tk * tn * b_bytes + 2 * tm * tn * o_bytes + (nk > 1) * tm * tn * 4
                if vmem > _TILE_VMEM_BUDGET or tm > 2048 or tn > 2048:
                    continue
                a_reads = n // tn if nk > 1 else 1
                b_reads = 1 if (nk == 1 and n == tn) else m // tm
                traffic = m * k * a_bytes * a_reads + k * n * b_bytes * b_reads + m * n * o_bytes
                mxu = 2.0 * m * n * k / _MXU_FLOPS * (1.0 + _MXU_ROWS / tm)
                cost = (max(mxu, traffic / _HBM_BYTES) + (nk > 1) * nk * m * n * 8 / _ACC_BYTES
                        + (m // tm) * (n // tn) * nk * _STEP_SECONDS)
                if best is None or cost < best[0]:
                    best = (cost, tm, tn, tk)
    assert best is not None, (m, n, k)
    return best[1:]


def _cargo_call(body, cargo, *, name, grid, in_specs, out_specs, out_shape, scratch_shapes, semantics):
    params = lambda sem: pltpu.CompilerParams(dimension_semantics=sem, vmem_limit_bytes=VMEM_LIMIT)
    if not cargo:
        return pl.pallas_call(body, name=name, grid=grid, in_specs=in_specs, out_specs=out_specs, out_shape=out_shape,
                              scratch_shapes=scratch_shapes, compiler_params=params(semantics))
    n_in, n_out, n_scratch, n_cargo = len(in_specs), len(out_specs), len(scratch_shapes), len(cargo)

    def loaded(*refs):
        ins, cargo_in, rest = refs[:n_in], refs[n_in:n_in + n_cargo], refs[n_in + n_cargo:]
        outs, cargo_out, rest = rest[:n_out], rest[n_out:n_out + n_cargo], rest[n_out + n_cargo:]
        scratch, sems = rest[:n_scratch], rest[n_scratch:]
        ids = [pl.program_id(d) for d in range(len(grid))]
        first = functools.reduce(jnp.logical_and, [i == 0 for i in ids])
        last = functools.reduce(jnp.logical_and, [i == g - 1 for i, g in zip(ids, grid)])
        moves = lambda: [_moves(kind, x_ref, y_ref, *sems[3 * c:3 * c + 3])
                         for c, ((kind, _), x_ref, y_ref) in enumerate(zip(cargo, cargo_in, cargo_out))]

        @pl.when(first)
        def _():
            for mv in moves():
                _start(mv)

        body(*ins, *outs, *scratch)

        @pl.when(last)
        def _():
            for mv in moves():
                _finish(mv)

    sems = [pltpu.SemaphoreType.DMA((N_DEV - 1,)), pltpu.SemaphoreType.DMA((N_DEV - 1,)), pltpu.SemaphoreType.DMA] * n_cargo
    call = pl.pallas_call(
        loaded, name=name, grid=grid, in_specs=list(in_specs) + [_ANY] * n_cargo,
        out_specs=list(out_specs) + [_ANY] * n_cargo,
        out_shape=list(out_shape) + [_moved_shape(kind, x) for kind, x in cargo],
        scratch_shapes=list(scratch_shapes) + sems, compiler_params=params(("arbitrary",) * len(grid)))

    def run(*args):
        results = call(*args, *[x for _, x in cargo])
        return list(results[:n_out]), list(results[n_out:])

    return run


def _matmul(a, b, *, ta=False, tb=False, out_dtype=F32, name, cargo=()):
    batched = a.ndim == 3
    if ta:
        k_dim, m_dim = a.shape[-2:]
    else:
        m_dim, k_dim = a.shape[-2:]
    if tb:
        n_dim, kb = b.shape[-2:]
    else:
        kb, n_dim = b.shape[-2:]
    assert kb == k_dim, (a.shape, b.shape, ta, tb)
    tm, tn, tk = _pick_tiles(m_dim, n_dim, k_dim, a.dtype.itemsize, b.dtype.itemsize, jnp.dtype(out_dtype).itemsize,
                             LANES if ta else 16)
    nk = k_dim // tk
    ca, cb = (0 if ta else 1), (1 if tb else 0)
    grid = (m_dim // tm, n_dim // tn, nk)
    gb = 1
    if batched:
        step_bytes = 2 * (tm * tk * a.dtype.itemsize + tk * tn * b.dtype.itemsize + tm * tn * jnp.dtype(out_dtype).itemsize)
        gb = max(g for g in _divisors(a.shape[0], 1) if g * step_bytes <= _BATCH_VMEM_BUDGET or g == 1)
        grid = (a.shape[0] // gb,) + grid
    dims = (((ca + 1,), (cb + 1,)), ((0,), (0,))) if batched else (((ca,), (cb,)), ((), ()))

    def body(a_ref, b_ref, o_ref, *acc):
        part = lax.dot_general(a_ref[...].astype(_MXU_DTYPE), b_ref[...].astype(_MXU_DTYPE), dims,
                               preferred_element_type=F32)
        if nk == 1:
            o_ref[...] = part.astype(o_ref.dtype)
            return
        acc_ref, = acc
        k = pl.program_id(len(grid) - 1)

        @pl.when(k == 0)
        def _():
            acc_ref[...] = part

        @pl.when(k > 0)
        def _():
            acc_ref[...] += part

        @pl.when(k == nk - 1)
        def _():
            o_ref[...] = acc_ref[...].astype(o_ref.dtype)

    def spec(shape, fn):
        if batched:
            return pl.BlockSpec((gb,) + shape, lambda g, i, j, k: (g,) + fn(i, j, k))
        return pl.BlockSpec(shape, fn)

    a_spec = spec((tk, tm), lambda i, j, k: (k, i)) if ta else spec((tm, tk), lambda i, j, k: (i, k))
    b_spec = spec((tn, tk), lambda i, j, k: (j, k)) if tb else spec((tk, tn), lambda i, j, k: (k, j))
    o_spec = spec((tm, tn), lambda i, j, k: (i, j))
    out_shape = ((a.shape[0],) if batched else ()) + (m_dim, n_dim)
    call = _cargo_call(
        body, cargo, name=name, grid=grid, in_specs=[a_spec, b_spec], out_specs=[o_spec],
        out_shape=[jax.ShapeDtypeStruct(out_shape, out_dtype)],
        scratch_shapes=[pltpu.VMEM(((gb,) if batched else ()) + (tm, tn), F32)] if nk > 1 else [],
        semantics=("parallel",) * (len(grid) - 1) + ("arbitrary",))
    if not cargo:
        return call(a, b)[0]
    results, moved = call(a, b)
    return results[0], moved


def _dot(a, b, ca=1, cb=0, exact=False):
    if exact:
        return lax.dot_general(a, b, (((ca,), (cb,)), ((), ())), precision=lax.Precision.HIGHEST,
                               preferred_element_type=F32)
    return lax.dot_general(a.astype(_MXU_DTYPE), b.astype(_MXU_DTYPE), (((ca,), (cb,)), ((), ())),
                           preferred_element_type=F32)


def _tri(n):
    return lax.broadcasted_iota(jnp.int32, (n, n), 0) >= lax.broadcasted_iota(jnp.int32, (n, n), 1)


def _log_sigmoid(x):
    return jnp.minimum(x, 0.0) - jnp.log(1.0 + jnp.exp(-jnp.abs(x)))


def _softplus(x):
    return jnp.maximum(x, 0.0) + jnp.log(1.0 + jnp.exp(-jnp.abs(x)))


def _silu(x):
    return x / (1.0 + jnp.exp(-x))


def _full_spec(shape):
    return pl.BlockSpec(shape, lambda c: (0,) * len(shape))


def _gla_chunk(proj, st, w_a2, b_a, norm_g):
    t = proj.shape[0]
    q = proj[:, 0:GLA_QK] * (GLA_DK ** -0.5)
    k = proj[:, GLA_QK:2 * GLA_QK]
    v = proj[:, 2 * GLA_QK:2 * GLA_QK + GLA_VD]
    r = proj[:, 2 * GLA_QK + GLA_VD:2 * GLA_QK + 2 * GLA_VD]
    a_low = proj[:, 2 * GLA_QK + 2 * GLA_VD:]
    log_a = _log_sigmoid(_dot(a_low, w_a2) + b_a) * (1.0 / GLA_TAU)
    past = _tri(t)
    lc = _dot(past.astype(F32), log_a, exact=True)
    lend = lc[t - 1:t, :]
    e_pos = jnp.exp(lc)
    e_neg = jnp.exp(-lc)
    q_fwd, k_fwd, q_bwd, k_bwd = q * e_pos, k * e_neg, q * e_neg, k * e_pos
    kd = k * jnp.exp(lend - lc)
    g = jnp.exp(lend)
    outs, new_st = [], []
    for h in range(GLA_HEADS):
        sk = slice(h * GLA_DK, (h + 1) * GLA_DK)
        sv = slice(h * GLA_DV, (h + 1) * GLA_DV)
        s_past = _dot(q_fwd[:, sk], k_fwd[:, sk], 1, 1)
        s_future = _dot(q_bwd[:, sk], k_bwd[:, sk], 1, 1)
        scores = jnp.where(past, s_past, s_future)
        o = _dot(scores, v[:, sv]) + _dot(q_fwd[:, sk], st[h], 1, 1)
        new_st.append(st[h] * g[:, sk] + _dot(v[:, sv], kd[:, sk], 0, 0))
        o = o * lax.rsqrt(jnp.mean(o * o, axis=-1, keepdims=True) + EPS) * norm_g[:, sv]
        outs.append(o)
    return jnp.concatenate(outs, axis=1) * _silu(r), tuple(new_st)


_GLA_STATE = (GLA_HEADS, GLA_DV, GLA_DK)


def _gla_step(proj, st, w_a2, b_a, norm_g):
    outs = []
    for s in range(STEP_CHUNKS):
        out, st = _gla_chunk(proj[s * CHUNK:(s + 1) * CHUNK], st, w_a2, b_a, norm_g)
        outs.append(out)
    return jnp.concatenate(outs, axis=0), st


def _gla_core_fwd(proj, w_a2, b_a, norm_g, cargo=()):
    seq = proj.shape[0]
    nc = seq // STEP

    def body(proj_ref, wa_ref, ba_ref, ng_ref, o_ref, sprev_ref, st_ref):
        @pl.when(pl.program_id(0) == 0)
        def _():
            st_ref[...] = jnp.zeros_like(st_ref)

        st = tuple(st_ref[h] for h in range(GLA_HEADS))
        for h in range(GLA_HEADS):
            sprev_ref[0, h] = st[h]
        out, new_st = _gla_step(proj_ref[...], st, wa_ref[...], ba_ref[...], ng_ref[...])
        o_ref[...] = out
        for h in range(GLA_HEADS):
            st_ref[h] = new_st[h]

    return _cargo_call(
        body, cargo, name="gla_core_fwd", grid=(nc,),
        in_specs=[pl.BlockSpec((STEP,GLA_PROJ), lambda c: (c, 0)), _full_spec(w_a2.shape), _full_spec(b_a.shape),
                  _full_spec(norm_g.shape)],
        out_specs=[pl.BlockSpec((STEP,GLA_VD), lambda c: (c, 0)), pl.BlockSpec((1,) + _GLA_STATE, lambda c: (c, 0, 0, 0))],
        out_shape=[jax.ShapeDtypeStruct((seq, GLA_VD), F32), jax.ShapeDtypeStruct((nc,) + _GLA_STATE, F32)],
        scratch_shapes=[pltpu.VMEM(_GLA_STATE, F32)],
        semantics=("arbitrary",),
    )(proj, w_a2, b_a, norm_g)


def _gla_core_bwd(proj, sprev, d_out, w_a2, b_a, norm_g, cargo=()):
    seq = proj.shape[0]
    nc = seq // STEP

    def body(proj_ref, sprev_ref, do_ref, wa_ref, ba_ref, ng_ref, dproj_ref, dwa_ref, dba_ref, dng_ref, dst_ref):
        @pl.when(pl.program_id(0) == 0)
        def _():
            dst_ref[...] = jnp.zeros_like(dst_ref)
            dwa_ref[...] = jnp.zeros_like(dwa_ref)
            dba_ref[...] = jnp.zeros_like(dba_ref)
            dng_ref[...] = jnp.zeros_like(dng_ref)

        st = tuple(sprev_ref[0, h] for h in range(GLA_HEADS))
        _, vjp = jax.vjp(_gla_step, proj_ref[...], st, wa_ref[...], ba_ref[...], ng_ref[...])
        d_next = tuple(dst_ref[h] for h in range(GLA_HEADS))
        d_proj, d_st, d_wa, d_ba, d_ng = vjp((do_ref[...], d_next))
        dproj_ref[...] = d_proj.astype(dproj_ref.dtype)
        for h in range(GLA_HEADS):
            dst_ref[h] = d_st[h]
        dwa_ref[...] += d_wa
        dba_ref[...] += d_ba
        dng_ref[...] += d_ng

    rev = lambda c: (nc - 1 - c, 0)
    return _cargo_call(
        body, cargo, name="gla_core_bwd", grid=(nc,),
        in_specs=[pl.BlockSpec((STEP,GLA_PROJ), rev), pl.BlockSpec((1,) + _GLA_STATE, lambda c: (nc - 1 - c, 0, 0, 0)),
                  pl.BlockSpec((STEP,GLA_VD), rev), _full_spec(w_a2.shape), _full_spec(b_a.shape), _full_spec(norm_g.shape)],
        out_specs=[pl.BlockSpec((STEP,GLA_PROJ), rev), _full_spec(w_a2.shape), _full_spec(b_a.shape), _full_spec(norm_g.shape)],
        out_shape=[jax.ShapeDtypeStruct((seq, GLA_PROJ), BF16), jax.ShapeDtypeStruct(w_a2.shape, F32),
                   jax.ShapeDtypeStruct(b_a.shape, F32), jax.ShapeDtypeStruct(norm_g.shape, F32)],
        scratch_shapes=[pltpu.VMEM(_GLA_STATE, F32)],
        semantics=("arbitrary",),
    )(proj, sprev, d_out, w_a2, b_a, norm_g)


def _ssd_chunk(z, xbc, dt_raw, hs, dt_bias, a_log, d_skip, norm_g):
    t = z.shape[0]
    xs = xbc[:, :SSD_DINNER]
    bm = xbc[:, SSD_DINNER:SSD_DINNER + SSD_GN]
    cm = xbc[:, SSD_DINNER + SSD_GN:]
    dt = _softplus(dt_raw + dt_bias)
    da = dt * (-jnp.exp(a_log))
    tri = _tri(t).astype(F32)
    eye = (lax.broadcasted_iota(jnp.int32, (t, t), 0) == lax.broadcasted_iota(jnp.int32, (t, t), 1)).astype(F32)
    cum = _dot(tri, da, exact=True)
    cum_t = _dot(da, tri, 0, 1, exact=True)
    dt_t = _dot(dt, eye, 0, 0, exact=True)
    cum_end = cum[t - 1:t, :]
    w_state = dt * jnp.exp(cum_end - cum)
    e_cum = jnp.exp(cum)
    g_end = jnp.exp(cum_end)
    head_of = lambda axis: lax.shift_right_logical(lax.broadcasted_iota(jnp.int32, (SSD_GW, SSD_GW), axis),
                                                   jnp.int32(SSD_HEADDIM.bit_length() - 1))
    same_head = head_of(0) == head_of(1)
    ys, new_hs = [], []
    for g in range(SSD_GROUPS):
        heads = range(g * SSD_HPG, (g + 1) * SSD_HPG)
        cols = slice(g * SSD_GW, (g + 1) * SSD_GW)

        def spread(a):
            return jnp.concatenate([jnp.broadcast_to(a[:, h:h + 1], (a.shape[0], SSD_HEADDIM)) for h in heads], axis=1)

        def row(a_t):
            return jnp.concatenate([a_t[h:h + 1, :] for h in heads], axis=1)

        bm_g = bm[:, g * SSD_DSTATE:(g + 1) * SSD_DSTATE]
        cm_g = cm[:, g * SSD_DSTATE:(g + 1) * SSD_DSTATE]
        xs_g = xs[:, cols]
        cb = _dot(cm_g, jnp.concatenate([bm_g] * SSD_HPG, axis=0), 1, 1)
        mix = cb * jnp.exp(-jnp.abs(spread(cum) - row(cum_t))) * row(dt_t)
        x_diag = jnp.where(same_head, jnp.concatenate([xs_g] * SSD_HPG, axis=0), 0.0)
        y = _dot(mix, x_diag)
        y = y + _dot(cm_g, hs[g], 1, 1) * spread(e_cum)
        y = y + spread(d_skip) * xs_g
        states = _dot(xs_g * spread(w_state), bm_g, 0, 0)
        decayed = jnp.concatenate([g_end[:, h:h + 1] * hs[g][j * SSD_HEADDIM:(j + 1) * SSD_HEADDIM, :]
                                   for j, h in enumerate(heads)], axis=0)
        new_hs.append(decayed + states)
        yg = y * _silu(z[:, cols])
        ys.append(yg * lax.rsqrt(jnp.mean(yg * yg, axis=-1, keepdims=True) + EPS) * norm_g[:, cols])
    return jnp.concatenate(ys, axis=1), tuple(new_hs)


_SSD_STATE = (SSD_GROUPS, SSD_GW, SSD_DSTATE)


def _ssd_step(z, xbc, dt_raw, hs, dt_bias, a_log, d_skip, norm_g):
    outs = []
    for s in range(STEP_CHUNKS):
        rows = slice(s * CHUNK, (s + 1) * CHUNK)
        out, hs = _ssd_chunk(z[rows], xbc[rows], dt_raw[rows], hs, dt_bias, a_log, d_skip, norm_g)
        outs.append(out)
    return jnp.concatenate(outs, axis=0), hs
_SSD_DT_BLOCK = (SSD_DINNER + SSD_XBC) // LANES


def _ssd_core_fwd(proj, xbc, dt_bias, a_log, d_skip, norm_g, cargo=()):
    seq = proj.shape[0]
    nc = seq // STEP

    def body(z_ref, xbc_ref, dt_ref, db_ref, al_ref, ds_ref, ng_ref, o_ref, hprev_ref, hs_ref):
        @pl.when(pl.program_id(0) == 0)
        def _():
            hs_ref[...] = jnp.zeros_like(hs_ref)

        hs = tuple(hs_ref[g] for g in range(SSD_GROUPS))
        for g in range(SSD_GROUPS):
            hprev_ref[0, g] = hs[g]
        out, new_hs = _ssd_step(z_ref[...], xbc_ref[...], dt_ref[...], hs, db_ref[...], al_ref[...], ds_ref[...], ng_ref[...])
        o_ref[...] = out
        for g in range(SSD_GROUPS):
            hs_ref[g] = new_hs[g]

    return _cargo_call(
        body, cargo, name="ssd_core_fwd", grid=(nc,),
        in_specs=[pl.BlockSpec((STEP,SSD_DINNER), lambda c: (c, 0)), pl.BlockSpec((STEP,SSD_XBC), lambda c: (c, 0)),
                  pl.BlockSpec((STEP,LANES), lambda c: (c, _SSD_DT_BLOCK)),
                  _full_spec(dt_bias.shape), _full_spec(a_log.shape), _full_spec(d_skip.shape), _full_spec(norm_g.shape)],
        out_specs=[pl.BlockSpec((STEP,SSD_DINNER), lambda c: (c, 0)), pl.BlockSpec((1,) + _SSD_STATE, lambda c: (c, 0, 0, 0))],
        out_shape=[jax.ShapeDtypeStruct((seq, SSD_DINNER), F32), jax.ShapeDtypeStruct((nc,) + _SSD_STATE, F32)],
        scratch_shapes=[pltpu.VMEM(_SSD_STATE, F32)],
        semantics=("arbitrary",),
    )(proj, xbc, proj, dt_bias, a_log, d_skip, norm_g)


def _ssd_core_bwd(proj, xbc, hprev, d_out, dt_bias, a_log, d_skip, norm_g, cargo=()):
    seq = proj.shape[0]
    nc = seq // STEP

    def body(z_ref, xbc_ref, dt_ref, hprev_ref, do_ref, db_ref, al_ref, ds_ref, ng_ref,
             dz_ref, dxbc_ref, ddt_ref, ddb_ref, dal_ref, dds_ref, dng_ref, dhs_ref):
        @pl.when(pl.program_id(0) == 0)
        def _():
            dhs_ref[...] = jnp.zeros_like(dhs_ref)
            ddb_ref[...] = jnp.zeros_like(ddb_ref)
            dal_ref[...] = jnp.zeros_like(dal_ref)
            dds_ref[...] = jnp.zeros_like(dds_ref)
            dng_ref[...] = jnp.zeros_like(dng_ref)

        hs = tuple(hprev_ref[0, g] for g in range(SSD_GROUPS))
        _, vjp = jax.vjp(_ssd_step, z_ref[...], xbc_ref[...], dt_ref[...], hs, db_ref[...], al_ref[...], ds_ref[...], ng_ref[...])
        d_next = tuple(dhs_ref[g] for g in range(SSD_GROUPS))
        d_z, d_xbc, d_dt, d_hs, d_db, d_al, d_ds, d_ng = vjp((do_ref[...], d_next))
        dz_ref[...] = d_z.astype(dz_ref.dtype)
        dxbc_ref[...] = d_xbc
        ddt_ref[...] = d_dt.astype(ddt_ref.dtype)
        for g in range(SSD_GROUPS):
            dhs_ref[g] = d_hs[g]
        ddb_ref[...] += d_db
        dal_ref[...] += d_al
        dds_ref[...] += d_ds
        dng_ref[...] += d_ng

    rev = lambda c: (nc - 1 - c, 0)
    vec = [_full_spec(dt_bias.shape), _full_spec(a_log.shape), _full_spec(d_skip.shape), _full_spec(norm_g.shape)]
    return _cargo_call(
        body, cargo, name="ssd_core_bwd", grid=(nc,),
        in_specs=[pl.BlockSpec((STEP,SSD_DINNER), rev), pl.BlockSpec((STEP,SSD_XBC), rev),
                  pl.BlockSpec((STEP,LANES), lambda c: (nc - 1 - c, _SSD_DT_BLOCK)),
                  pl.BlockSpec((1,) + _SSD_STATE, lambda c: (nc - 1 - c, 0, 0, 0)),
                  pl.BlockSpec((STEP,SSD_DINNER), rev)] + vec,
        out_specs=[pl.BlockSpec((STEP,SSD_DINNER), rev), pl.BlockSpec((STEP,SSD_XBC), rev),
                   pl.BlockSpec((STEP,LANES), rev)] + vec,
        out_shape=[jax.ShapeDtypeStruct((seq, SSD_DINNER), BF16), jax.ShapeDtypeStruct((seq, SSD_XBC), F32),
                   jax.ShapeDtypeStruct((seq, LANES), BF16),
                   jax.ShapeDtypeStruct(dt_bias.shape, F32), jax.ShapeDtypeStruct(a_log.shape, F32),
                   jax.ShapeDtypeStruct(d_skip.shape, F32), jax.ShapeDtypeStruct(norm_g.shape, F32)],
        scratch_shapes=[pltpu.VMEM(_SSD_STATE, F32)],
        semantics=("arbitrary",),
    )(proj, xbc, proj, hprev, d_out, dt_bias, a_log, d_skip, norm_g)


CONV_COLS = 2048
CONV_HALO = 8


def _conv_taps(xx, rows):
    last = SSD_CONV - 1
    return [pltpu.roll(xx, last - k, 0)[CONV_HALO:CONV_HALO + rows] if k < last else xx[CONV_HALO:CONV_HALO + rows]
            for k in range(SSD_CONV)]


def _ssd_conv_fwd(proj, conv_w, conv_b, name):
    rows = proj.shape[0]
    tr = _tile(rows, 512, CONV_HALO)
    first_col = SSD_DINNER // CONV_COLS

    def body(x_ref, halo_ref, w_ref, b_ref, o_ref):
        halo = jnp.where(pl.program_id(0) == 0, 0.0, halo_ref[...])
        taps = _conv_taps(jnp.concatenate([halo, x_ref[...]], axis=0), tr)
        out = b_ref[...]
        for k in range(SSD_CONV):
            out = out + taps[k] * w_ref[k:k + 1, :]
        o_ref[...] = _silu(out)

    return pl.pallas_call(
        body, name=name, grid=(rows // tr, SSD_XBC // CONV_COLS),
        in_specs=[pl.BlockSpec((tr, CONV_COLS), lambda i, j: (i, first_col + j)),
                  pl.BlockSpec((CONV_HALO, CONV_COLS), lambda i, j: (jnp.maximum(i * (tr // CONV_HALO) - 1, 0), first_col + j)),
                  pl.BlockSpec((SSD_CONV, CONV_COLS), lambda i, j: (0, j)), pl.BlockSpec((1, CONV_COLS), lambda i, j: (0, j))],
        out_specs=pl.BlockSpec((tr, CONV_COLS), lambda i, j: (i, j)),
        out_shape=jax.ShapeDtypeStruct((rows, SSD_XBC), F32),
        compiler_params=pltpu.CompilerParams(dimension_semantics=("parallel", "parallel"), vmem_limit_bytes=VMEM_LIMIT),
    )(proj, proj, conv_w, conv_b[None])


def _ssd_conv_bwd(proj, d_xbc, conv_w, conv_b, name):
    rows = proj.shape[0]
    tr = _tile(rows, 512, CONV_HALO)
    nb, halos = rows // tr, tr // CONV_HALO
    first_col = SSD_DINNER // CONV_COLS

    def body(x_ref, before_ref, after_ref, d_ref, d_after_ref, w_ref, b_ref, dx_ref, dw_ref, db_ref):
        i = pl.program_id(1)

        @pl.when(i == 0)
        def _():
            dw_ref[...] = jnp.zeros_like(dw_ref)
            db_ref[...] = jnp.zeros_like(db_ref)

        before = jnp.where(i == 0, 0.0, before_ref[...])
        taps = _conv_taps(jnp.concatenate([before, x_ref[...], after_ref[...]], axis=0), tr + CONV_HALO)
        out = b_ref[...]
        for k in range(SSD_CONV):
            out = out + taps[k] * w_ref[k:k + 1, :]
        sig = 1.0 / (1.0 + jnp.exp(-out))
        d_after = jnp.where(i == nb - 1, 0.0, d_after_ref[...])
        d_out = jnp.concatenate([d_ref[...], d_after], axis=0) * sig * (1.0 + out * (1.0 - sig))
        d_x = d_out[:tr] * w_ref[SSD_CONV - 1:SSD_CONV, :]
        for k in range(SSD_CONV - 1):
            ahead = SSD_CONV - 1 - k
            d_x = d_x + pltpu.roll(d_out, tr + CONV_HALO - ahead, 0)[:tr] * w_ref[k:k + 1, :]
        dx_ref[...] = d_x.astype(dx_ref.dtype)
        for k in range(SSD_CONV):
            dw_ref[k:k + 1, :] += jnp.sum(d_out[:tr] * taps[k][:tr], axis=0, keepdims=True)
        db_ref[...] += jnp.sum(d_out[:tr], axis=0, keepdims=True)

    before = lambda j, i: jnp.maximum(i * halos - 1, 0)
    after = lambda j, i: jnp.minimum((i + 1) * halos, nb * halos - 1)
    d_x, d_w, d_b = pl.pallas_call(
        body, name=name, grid=(SSD_XBC // CONV_COLS, nb),
        in_specs=[pl.BlockSpec((tr, CONV_COLS), lambda j, i: (i, first_col + j)),
                  pl.BlockSpec((CONV_HALO, CONV_COLS), lambda j, i: (before(j, i), first_col + j)),
                  pl.BlockSpec((CONV_HALO, CONV_COLS), lambda j, i: (after(j, i), first_col + j)),
                  pl.BlockSpec((tr, CONV_COLS), lambda j, i: (i, j)),
                  pl.BlockSpec((CONV_HALO, CONV_COLS), lambda j, i: (after(j, i), j)),
                  pl.BlockSpec((SSD_CONV, CONV_COLS), lambda j, i: (0, j)), pl.BlockSpec((1, CONV_COLS), lambda j, i: (0, j))],
        out_specs=[pl.BlockSpec((tr, CONV_COLS), lambda j, i: (i, j)), pl.BlockSpec((SSD_CONV, CONV_COLS), lambda j, i: (0, j)),
                   pl.BlockSpec((1, CONV_COLS), lambda j, i: (0, j))],
        out_shape=[jax.ShapeDtypeStruct((rows, SSD_XBC), BF16), jax.ShapeDtypeStruct((SSD_CONV, SSD_XBC), F32),
                   jax.ShapeDtypeStruct((1, SSD_XBC), F32)],
        compiler_params=pltpu.CompilerParams(dimension_semantics=("parallel", "arbitrary"), vmem_limit_bytes=VMEM_LIMIT),
    )(proj, proj, proj, d_xbc, d_xbc, conv_w, conv_b[None])
    return d_x, d_w, d_b[0]


def _s5_boundary_scan(z, lam_re, lam_im, name, reverse=False):
    n_chunks, groups, width = z.shape
    tn = _tile(n_chunks, 128, 1)
    blocks = n_chunks // tn
    lam_a = jnp.concatenate([lam_re, lam_re], axis=1)
    lam_b = jnp.concatenate([-lam_im, lam_im], axis=1)

    def body(z_ref, a_ref, b_ref, x_ref, carry_ref):
        @pl.when(pl.program_id(0) == 0)
        def _():
            carry_ref[...] = jnp.zeros_like(carry_ref)

        a, b = a_ref[...], b_ref[...]

        def step(i, x):
            n = tn - 1 - i if reverse else i
            x_ref[n] = x
            return a * x + b * pltpu.roll(x, width // 2, 1) + z_ref[n]

        carry_ref[...] = lax.fori_loop(0, tn, step, carry_ref[...])

    block = pl.BlockSpec((tn, groups, width), (lambda i: (blocks - 1 - i, 0, 0)) if reverse else (lambda i: (i, 0, 0)))
    return pl.pallas_call(
        body, name=name, grid=(blocks,), in_specs=[block, _full_spec((groups, width)), _full_spec((groups, width))],
        out_specs=block, out_shape=jax.ShapeDtypeStruct(z.shape, F32), scratch_shapes=[pltpu.VMEM((groups, width), F32)],
        compiler_params=pltpu.CompilerParams(dimension_semantics=("arbitrary",), vmem_limit_bytes=VMEM_LIMIT),
    )(z, lam_a, lam_b)


_FLIPS = [(kx, ky, kc) for kx in (0, 1) for ky in (0, 1) for kc in (0, 1)][1:]


def _mesh_position():
    return lax.axis_index("x"), lax.axis_index("y"), lax.axis_index("c")


def _peer(pos, flip):
    return tuple((1 - p) if f else p for p, f in zip(pos, flip))


def _index(pos):
    return 4 * pos[0] + 2 * pos[1] + pos[2]


_ANY = pl.BlockSpec(memory_space=pl.ANY)


def _moved_shape(kind, x):
    return jax.ShapeDtypeStruct(((N_DEV,) + x.shape) if kind == "gather" else x.shape, x.dtype)


def _moves(kind, x_ref, out_ref, send_sems, recv_sems, local_sem):
    me = _mesh_position()
    source = (lambda pos: x_ref) if kind == "gather" else (lambda pos: x_ref.at[_index(pos)])
    local = pltpu.make_async_copy(source(me), out_ref.at[_index(me)], local_sem)
    outgoing, incoming = [], []
    for k, flip in enumerate(_FLIPS):
        peer = _peer(me, flip)
        copy = lambda slot: pltpu.make_async_remote_copy(
            src_ref=source(peer), dst_ref=out_ref.at[_index(slot)], send_sem=send_sems.at[k], recv_sem=recv_sems.at[k],
            device_id=peer, device_id_type=pl.DeviceIdType.MESH)
        outgoing.append(copy(me))
        incoming.append(copy(peer))
    return local, outgoing, incoming


def _start(moves):
    local, outgoing, _ = moves
    local.start()
    for cp in outgoing:
        cp.start()


def _finish(moves):
    local, outgoing, incoming = moves
    for cp in incoming:
        cp.wait_recv()
    for cp in outgoing:
        cp.wait_send()
    local.wait()


def _collective(kind, x, name):
    def body(x_ref, out_ref, send_sems, recv_sems, local_sem):
        moves = _moves(kind, x_ref, out_ref, send_sems, recv_sems, local_sem)
        _start(moves)
        _finish(moves)

    return pl.pallas_call(
        body, name=name, in_specs=[_ANY], out_specs=_ANY, out_shape=_moved_shape(kind, x),
        scratch_shapes=[pltpu.SemaphoreType.DMA((N_DEV - 1,)), pltpu.SemaphoreType.DMA((N_DEV - 1,)), pltpu.SemaphoreType.DMA],
        compiler_params=pltpu.CompilerParams(has_side_effects=True),
    )(x)


def _adamw(parts, w, m, v, name):
    n_parts = parts.shape[0]
    layers, rows, cols = w.shape
    tr = _tile(rows, 256, 8)

    def body(p_ref, w_ref, m_ref, v_ref, g_ref, d_ref, mo_ref, vo_ref):
        g = p_ref[0].astype(F32)
        for s in range(1, n_parts):
            g = g + p_ref[s].astype(F32)
        m_new = ADAM_B1 * m_ref[...] + (1.0 - ADAM_B1) * g
        v_new = ADAM_B2 * v_ref[...] + (1.0 - ADAM_B2) * (g * g)
        m_hat = m_new / (1.0 - ADAM_B1 ** ADAM_STEP)
        v_hat = v_new / (1.0 - ADAM_B2 ** ADAM_STEP)
        g_ref[...] = g
        d_ref[...] = -ADAM_LR * (m_hat / (jnp.sqrt(v_hat) + ADAM_EPS) + ADAM_WD * w_ref[...])
        mo_ref[...] = m_new
        vo_ref[...] = v_new

    blk = pl.BlockSpec((None, tr, cols), lambda l, i: (l, i, 0))
    shape = jax.ShapeDtypeStruct(w.shape, F32)
    return pl.pallas_call(
        body, name=name, grid=(layers, rows // tr),
        in_specs=[pl.BlockSpec((n_parts, None, tr, cols), lambda l, i: (0, l, i, 0)), blk, blk, blk],
        out_specs=[blk, blk, blk, blk], out_shape=[shape, shape, shape, shape],
        compiler_params=pltpu.CompilerParams(dimension_semantics=("parallel", "parallel"), vmem_limit_bytes=VMEM_LIMIT),
    )(parts, w, m, v)


def _sum_parts(parts, name):
    _, rows, cols = parts.shape
    tr = _tile(rows, 256, 8)

    def body(p_ref, o_ref):
        total = p_ref[0]
        for s in range(1, N_DEV):
            total = total + p_ref[s]
        o_ref[...] = total

    return pl.pallas_call(
        body, name=name, grid=(rows // tr,),
        in_specs=[pl.BlockSpec((N_DEV, tr, cols), lambda i: (0, i, 0))], out_specs=pl.BlockSpec((tr, cols), lambda i: (i, 0)),
        out_shape=jax.ShapeDtypeStruct((rows, cols), parts.dtype),
        compiler_params=pltpu.CompilerParams(dimension_semantics=("parallel",), vmem_limit_bytes=VMEM_LIMIT),
    )(parts)


def _row_tile(rows):
    return _tile(rows, 512, 16)


def _row_spec(rows, cols, block=0):
    return pl.BlockSpec((_row_tile(rows), cols), lambda i: (i, block))


def _rows_params(accumulates):
    return pltpu.CompilerParams(dimension_semantics=("arbitrary" if accumulates else "parallel",),
                                vmem_limit_bytes=VMEM_LIMIT)


def _swiglu_fwd(gu, name):
    rows = gu.shape[0]

    def body(g_ref, u_ref, o_ref):
        o_ref[...] = (_silu(g_ref[...].astype(F32)) * u_ref[...].astype(F32)).astype(o_ref.dtype)

    return pl.pallas_call(
        body, name=name, grid=(rows // _row_tile(rows),),
        in_specs=[_row_spec(rows, FFN_HIDDEN, 0), _row_spec(rows, FFN_HIDDEN, 1)], out_specs=_row_spec(rows, FFN_HIDDEN),
        out_shape=jax.ShapeDtypeStruct((rows, FFN_HIDDEN), BF16), compiler_params=_rows_params(False))(gu, gu)


def _swiglu_bwd(gu, d_act, name):
    rows = gu.shape[0]

    def body(g_ref, u_ref, d_ref, o_ref):
        g, u, d = g_ref[...].astype(F32), u_ref[...].astype(F32), d_ref[...].astype(F32)
        sig = 1.0 / (1.0 + jnp.exp(-g))
        o_ref[:, :FFN_HIDDEN] = (d * u * sig * (1.0 + g * (1.0 - sig))).astype(o_ref.dtype)
        o_ref[:, FFN_HIDDEN:] = (d * g * sig).astype(o_ref.dtype)

    return pl.pallas_call(
        body, name=name, grid=(rows // _row_tile(rows),),
        in_specs=[_row_spec(rows, FFN_HIDDEN, 0), _row_spec(rows, FFN_HIDDEN, 1), _row_spec(rows, FFN_HIDDEN)],
        out_specs=_row_spec(rows, 2 * FFN_HIDDEN),
        out_shape=jax.ShapeDtypeStruct((rows, 2 * FFN_HIDDEN), BF16), compiler_params=_rows_params(False))(gu, gu, d_act)


def _add_norm_fwd(h, y, gain, name):
    rows = h.shape[0]

    def body(*refs):
        if y is None:
            h_ref, g_ref, n_ref = refs
            x = h_ref[...]
        else:
            h_ref, y_ref, g_ref, s_ref, n_ref = refs
            x = h_ref[...] + y_ref[...]
            s_ref[...] = x
        n_ref[...] = (x * lax.rsqrt(jnp.mean(x * x, axis=-1, keepdims=True) + EPS) * g_ref[...]).astype(n_ref.dtype)

    row = _row_spec(rows, D_MODEL)
    ins = [h] if y is None else [h, y]
    out_shape = [jax.ShapeDtypeStruct((rows, D_MODEL), BF16)]
    if y is not None:
        out_shape = [jax.ShapeDtypeStruct((rows, D_MODEL), F32)] + out_shape
    res = pl.pallas_call(
        body, name=name, grid=(rows // _row_tile(rows),),
        in_specs=[row] * len(ins) + [_full_spec((1, D_MODEL))], out_specs=[row] * len(out_shape), out_shape=out_shape,
        compiler_params=_rows_params(False))(*ins, gain[None])
    return (h, res[0]) if y is None else (res[0], res[1])


def _norm_bwd(x, gain, d_n, d_skip, name):
    rows = x.shape[0]

    def body(x_ref, g_ref, dn_ref, ds_ref, dx_ref, dg_ref):
        @pl.when(pl.program_id(0) == 0)
        def _():
            dg_ref[...] = jnp.zeros_like(dg_ref)

        x, dn = x_ref[...], dn_ref[...].astype(F32)
        r = lax.rsqrt(jnp.mean(x * x, axis=-1, keepdims=True) + EPS)
        gd = g_ref[...] * dn
        dx_ref[...] = r * gd - x * (r * r * r) * jnp.mean(x * gd, axis=-1, keepdims=True) + ds_ref[...]
        dg_ref[...] += jnp.sum(x * r * dn, axis=0, keepdims=True)

    row = _row_spec(rows, D_MODEL)
    dx, dg = pl.pallas_call(
        body, name=name, grid=(rows // _row_tile(rows),),
        in_specs=[row, _full_spec((1, D_MODEL)), row, row], out_specs=[row, _full_spec((1, D_MODEL))],
        out_shape=[jax.ShapeDtypeStruct((rows, D_MODEL), F32), jax.ShapeDtypeStruct((1, D_MODEL), F32)],
        compiler_params=_rows_params(True))(x, gain[None], d_n, d_skip)
    return dx, dg[0]


def _loss_head(h, gain, target, name):
    rows = h.shape[0]

    def body(x_ref, g_ref, t_ref, loss_ref, dx_ref, dg_ref):
        @pl.when(pl.program_id(0) == 0)
        def _():
            loss_ref[...] = jnp.zeros_like(loss_ref)
            dg_ref[...] = jnp.zeros_like(dg_ref)

        x = x_ref[...]
        r = lax.rsqrt(jnp.mean(x * x, axis=-1, keepdims=True) + EPS)
        err = x * r * g_ref[...] - t_ref[...]
        loss_ref[...] += 0.5 * jnp.sum(jnp.mean(err * err, axis=-1, keepdims=True), axis=0, keepdims=True)
        dy = err * (1.0 / D_MODEL)
        gd = g_ref[...] * dy
        dx_ref[...] = r * gd - x * (r * r * r) * jnp.mean(x * gd, axis=-1, keepdims=True)
        dg_ref[...] += jnp.sum(x * r * dy, axis=0, keepdims=True)

    row = _row_spec(rows, D_MODEL)
    loss, dx, dg = pl.pallas_call(
        body, name=name, grid=(rows // _row_tile(rows),),
        in_specs=[row, _full_spec((1, D_MODEL)), row], out_specs=[_full_spec((1, 1)), row, _full_spec((1, D_MODEL))],
        out_shape=[jax.ShapeDtypeStruct((1, 1), F32), jax.ShapeDtypeStruct((rows, D_MODEL), F32),
                   jax.ShapeDtypeStruct((1, D_MODEL), F32)],
        compiler_params=_rows_params(True))(h, gain[None], target)
    return loss[0, 0], dx, dg[0]


def _join_cols(blocks, n_out, name):
    _, layers, rows, n = blocks.shape
    tr = _tile(rows, 256, 16)

    def body(x_ref, o_ref):
        for d in range(N_DEV):
            o_ref[:, d * n:(d + 1) * n] = x_ref[d]
        if n_out > N_DEV * n:
            o_ref[:, N_DEV * n:] = jnp.zeros((tr, n_out - N_DEV * n), o_ref.dtype)

    return pl.pallas_call(
        body, name=name, grid=(layers, rows // tr),
        in_specs=[pl.BlockSpec((N_DEV, None, tr, n), lambda l, i: (0, l, i, 0))],
        out_specs=pl.BlockSpec((None, tr, n_out), lambda l, i: (l, i, 0)),
        out_shape=jax.ShapeDtypeStruct((layers, rows, n_out), blocks.dtype),
        compiler_params=pltpu.CompilerParams(dimension_semantics=("parallel", "parallel"), vmem_limit_bytes=VMEM_LIMIT),
    )(blocks)


def _split_cols(full, n, name):
    rows = full.shape[0]
    tr = _tile(rows, 256, 16)

    def body(x_ref, o_ref):
        for d in range(N_DEV):
            o_ref[d] = x_ref[:, d * n:(d + 1) * n]

    return pl.pallas_call(
        body, name=name, grid=(rows // tr,),
        in_specs=[pl.BlockSpec((tr, full.shape[1]), lambda i: (i, 0))],
        out_specs=pl.BlockSpec((N_DEV, tr, n), lambda i: (0, i, 0)),
        out_shape=jax.ShapeDtypeStruct((N_DEV, rows, n), full.dtype),
        compiler_params=pltpu.CompilerParams(dimension_semantics=("parallel",), vmem_limit_bytes=VMEM_LIMIT),
    )(full)


def _pack(arrays):
    flat = jnp.concatenate([a.reshape(-1) for a in arrays])
    unit = FLAT_COLS * FLAT_ROWS_ALIGN
    padded = -(-flat.shape[0] // unit) * unit
    return jnp.pad(flat, (0, padded - flat.shape[0])).reshape(-1, FLAT_COLS)


def _unpack(flat, shapes, lead=()):
    flat = flat.reshape(lead + (-1,))
    out, off = [], 0
    for shape in shapes:
        n = math.prod(shape)
        out.append(flat[..., off:off + n].reshape(lead + tuple(shape)))
        off += n
    return out


def _join(blocks, axis):
    moved = jnp.moveaxis(blocks, 0, axis)
    shape = list(moved.shape)
    shape[axis:axis + 2] = [shape[axis] * shape[axis + 1]]
    return moved.reshape(shape)


def _own_shard(full, axis, position):
    n = full.shape[axis] // N_DEV
    return lax.dynamic_slice_in_dim(full, position * n, n, axis)


def _s5_operators(log_dt, a_re, a_im, b_re, b_im, c_re, c_im):
    t = S5_CHUNK
    hi = lax.Precision.HIGHEST
    step = jnp.exp(log_dt)[:, None]
    mag = jnp.exp(step * a_re)
    abar_re = mag * jnp.cos(step * a_im)
    abar_im = mag * jnp.sin(step * a_im)
    den = a_re * a_re + a_im * a_im
    f_re = ((abar_re - 1.0) * a_re + abar_im * a_im) / den
    f_im = (abar_im * a_re - (abar_re - 1.0) * a_im) / den
    bb_re = f_re[..., None] * b_re - f_im[..., None] * b_im
    bb_im = f_re[..., None] * b_im + f_im[..., None] * b_re
    j = jnp.arange(t + 1, dtype=F32)[:, None, None]
    pmag = jnp.exp(j * (step * a_re))
    pw_re = pmag * jnp.cos(j * (step * a_im))
    pw_im = pmag * jnp.sin(j * (step * a_im))
    cl_re = c_re[None] * pw_re[:t, :, None, :] - c_im[None] * pw_im[:t, :, None, :]
    cl_im = c_re[None] * pw_im[:t, :, None, :] + c_im[None] * pw_re[:t, :, None, :]
    kern = (jnp.einsum('jgcp,gpk->jgck', cl_re, bb_re, precision=hi)
            - jnp.einsum('jgcp,gpk->jgck', cl_im, bb_im, precision=hi))
    rp_re, rp_im = pw_re[:t][::-1], pw_im[:t][::-1]
    wz_re = rp_re[:, :, :, None] * bb_re[None] - rp_im[:, :, :, None] * bb_im[None]
    wz_im = rp_re[:, :, :, None] * bb_im[None] + rp_im[:, :, :, None] * bb_re[None]
    w_z = jnp.concatenate([wz_re, wz_im], axis=2).transpose(1, 0, 3, 2).reshape(S5_GROUPS, t * S5_GROUP, 2 * S5_STATE)
    cy_re = c_re[None] * pw_re[1:, :, None, :] - c_im[None] * pw_im[1:, :, None, :]
    cy_im = c_re[None] * pw_im[1:, :, None, :] + c_im[None] * pw_re[1:, :, None, :]
    w_y = jnp.concatenate([cy_re, -cy_im], axis=3).transpose(1, 3, 0, 2).reshape(S5_GROUPS, 2 * S5_STATE, t * S5_GROUP)
    return kern, w_z, w_y, pw_re[t], pw_im[t]


def _s5_lag_selector():
    t = S5_CHUNK
    lag = jnp.arange(t)[:, None] - jnp.arange(t)[None, :]
    return (lag[:, :, None] == jnp.arange(t)[None, None, :]).astype(F32).reshape(t * t, t)


def _s5_toeplitz(kern, tag):
    t = S5_CHUNK
    sel = _s5_lag_selector()
    flat = _matmul(sel, kern.reshape(t, -1), out_dtype=BF16, name=tag + "_toeplitz")
    toep = flat.reshape(t, t, S5_GROUPS, S5_GROUP, S5_GROUP).transpose(2, 1, 4, 0, 3)
    toep = toep.reshape(S5_GROUPS, t * S5_GROUP, t * S5_GROUP)

    def backward(d_toep):
        d_flat = d_toep.reshape(S5_GROUPS, t, S5_GROUP, t, S5_GROUP).transpose(3, 1, 0, 4, 2).reshape(t * t, -1)
        return _matmul(sel, d_flat, ta=True, name=tag + "_toeplitz_dw").reshape(kern.shape)

    return toep, backward


def _s5_gate(y, u, d_skip):
    return jax.nn.gelu(y + d_skip * u)


def _glu(vg):
    return vg[:, :D_MODEL] * jax.nn.sigmoid(vg[:, D_MODEL:])


_BIG = [("gla_w_in", 2), ("gla_w_out", 1), ("ssd_w_in", 2), ("ssd_w_out", 1), ("s5_w_glu", 2), ("ffn_w_gu", 2),
        ("ffn_w_down", 1)]
_PADDED_COLS = {"gla_w_in": GLA_PROJ, "ssd_w_in": SSD_PROJ}


class _Traffic:
    LINK_BYTES_PER_SECOND = 7.0e10
    MATMUL_FLOPS = 7.0e14

    def __init__(self, shards, plan):
        self.shards, self.plan = shards, plan
        self.position = 0
        self.queue = []
        self.weights, self.received = {}, {}
        self.early = None
        self.standalone = self.serial = 0
        for key in plan:
            self._request(key)

    def _request(self, key):
        shard = self.shards[key]
        seconds = (N_DEV - 1) * shard.size * shard.dtype.itemsize / self.LINK_BYTES_PER_SECOND
        self._enqueue("gather", shard, seconds, key, lambda blocks: self.weights.__setitem__(key, self._assemble(key, blocks)))

    def _enqueue(self, kind, x, seconds, key, deliver):
        self.queue.append((kind, x, seconds, key, deliver, self.serial))
        self.serial += 1

    @staticmethod
    def _assemble(key, blocks):
        name, layer = key
        if dict(_BIG)[name] == 1:
            return blocks.reshape((N_DEV * blocks.shape[1], blocks.shape[2]))
        n_out = _PADDED_COLS.get(name, N_DEV * blocks.shape[2])
        return _join_cols(blocks[:, None], n_out, f"join_{name}_{layer}")[0]

    def take(self, key):
        assert key == self.plan[self.position], (key, self.plan[self.position])
        self.position += 1
        while key not in self.weights:
            self._alone(self.queue.pop(0))
        return self.weights[key]

    def run(self, seconds, call, more_carriers_follow=False):
        riders, waiting, left = [], [], seconds
        for item in self.queue:
            if item[2] <= left:
                riders.append(item)
                left -= item[2]
            else:
                waiting.append(item)
        due = [item for item in waiting if item[3] is not None and self.position < len(self.plan)
               and item[3] == self.plan[self.position]]
        if due and not more_carriers_follow:
            left = seconds - due[0][2]
            kept = []
            for item in riders:
                if item[2] <= left:
                    kept.append(item)
                    left -= item[2]
                else:
                    waiting.append(item)
            riders = due + kept
            waiting = [item for item in waiting if item is not due[0]]
            waiting.sort(key=lambda item: item[5])
        self.queue = waiting
        if not riders:
            return call(())
        results, moved = call([(kind, x) for kind, x, *_ in riders])
        for item, y in zip(riders, moved):
            item[4](y)
        return results

    def matmul(self, a, b, more_carriers_follow=False, **kw):
        m, n = (a.shape[-1] if kw.get("ta") else a.shape[-2]), (b.shape[-2] if kw.get("tb") else b.shape[-1])
        k = a.shape[-2] if kw.get("ta") else a.shape[-1]
        return self.run(2.0 * m * n * k / self.MATMUL_FLOPS, lambda cargo: _matmul(a, b, cargo=cargo, **kw),
                        more_carriers_follow)

    def send_gradient(self, key, dw):
        name, layer = key
        shard = self.shards[key]
        if dict(_BIG)[name] == 1:
            blocks = dw.reshape((N_DEV,) + shard.shape)
        else:
            blocks = _split_cols(dw, shard.shape[1], f"split_{name}_{layer}")
        seconds = (N_DEV - 1) * shard.size * shard.dtype.itemsize / self.LINK_BYTES_PER_SECOND
        self._enqueue("exchange", blocks, seconds, None, lambda parts: self.received.__setitem__(key, parts))

    def gather_early(self, packed):
        seconds = (N_DEV - 1) * packed.size * packed.dtype.itemsize / self.LINK_BYTES_PER_SECOND
        self._enqueue("gather", packed, seconds, None, lambda parts: setattr(self, "early", parts))

    def _alone(self, item):
        kind, x, _, _, deliver, _ = item
        deliver(_collective(kind, x, f"{kind}_alone_{self.standalone}"))
        self.standalone += 1

    def flush(self):
        for item in self.queue:
            self._alone(item)
        self.queue = []


_GLA_FWD_SECONDS, _GLA_BWD_SECONDS, _SSD_FWD_SECONDS, _SSD_BWD_SECONDS = 1.25e-6, 3.4e-6, 3.5e-6, 14e-6


def _linear(x, w, tag, out_dtype=F32, dx_dtype=F32, more_carriers_follow=False):
    traffic, key = w
    weight = traffic.take(key)
    y = traffic.matmul(x, weight, more_carriers_follow, out_dtype=out_dtype, name=tag + "_fwd")

    def backward(dy):
        dx = traffic.matmul(dy, weight, tb=True, out_dtype=dx_dtype, name=tag + "_dx")
        traffic.send_gradient(key, traffic.matmul(x, dy, ta=True, out_dtype=BF16, name=tag + "_dw"))
        return dx

    return y, backward


def _gla_mixer(hn, p, tag):
    traffic, chunks = p["w_in"][0], hn.shape[0] // CHUNK
    w_a2 = jnp.pad(p["w_a2"], ((0, LANES - GLA_RANK), (0, 0)))
    b_a, norm_g = p["b_a"][None], p["norm_g"][None]
    proj, lin_in = _linear(hn, p["w_in"], tag + "_in", more_carriers_follow=True)
    o, sprev = traffic.run(chunks * _GLA_FWD_SECONDS, lambda cargo: _gla_core_fwd(proj, w_a2, b_a, norm_g, cargo))
    y, lin_out = _linear(o, p["w_out"], tag + "_out")

    def backward(dy):
        d_o = lin_out(dy)
        d_proj, d_wa, d_ba, d_ng = traffic.run(chunks * _GLA_BWD_SECONDS,
                                               lambda cargo: _gla_core_bwd(proj, sprev, d_o, w_a2, b_a, norm_g, cargo))
        return lin_in(d_proj), dict(w_a2=d_wa[:GLA_RANK], b_a=d_ba[0], norm_g=d_ng[0])

    return y, backward


def _ssd_mixer(hn, p, tag):
    pad = lambda a: jnp.pad(a[None], ((0, 0), (0, LANES - SSD_HEADS)))
    dt_bias, a_log, d_skip, norm_g = pad(p["dt_bias"]), pad(p["a_log"]), pad(p["d"]), p["norm_g"][None]
    traffic, chunks = p["w_in"][0], hn.shape[0] // CHUNK
    proj, lin_in = _linear(hn, p["w_in"], tag + "_in", more_carriers_follow=True)
    xbc = _ssd_conv_fwd(proj, p["conv_w"], p["conv_b"], tag + "_conv")
    o, hprev = traffic.run(chunks * _SSD_FWD_SECONDS,
                           lambda cargo: _ssd_core_fwd(proj, xbc, dt_bias, a_log, d_skip, norm_g, cargo))
    y, lin_out = _linear(o, p["w_out"], tag + "_out")

    def backward(dy):
        d_o = lin_out(dy)
        d_z, d_xbc, d_dt, d_db, d_al, d_ds, d_ng = traffic.run(
            chunks * _SSD_BWD_SECONDS, lambda cargo: _ssd_core_bwd(proj, xbc, hprev, d_o, dt_bias, a_log, d_skip, norm_g, cargo))
        d_pre, d_cw, d_cb = _ssd_conv_bwd(proj, d_xbc, p["conv_w"], p["conv_b"], tag + "_conv_bwd")
        d_hn = lin_in(jnp.concatenate([d_z, d_pre, d_dt], axis=1))
        return d_hn, dict(conv_w=d_cw, conv_b=d_cb, dt_bias=d_db[0, :SSD_HEADS], a_log=d_al[0, :SSD_HEADS],
                          d=d_ds[0, :SSD_HEADS], norm_g=d_ng[0])

    return y, backward


def _s5_mixer(hn, p, tag):
    seq = hn.shape[0]
    t, n_chunks = S5_CHUNK, hn.shape[0] // S5_CHUNK
    names = ("log_dt", "a_re", "a_im", "b_re", "b_im", "c_re", "c_im")
    (kern, w_z, w_y, lam_re, lam_im), ops_vjp = jax.vjp(_s5_operators, *[p[k] for k in names])
    toep, toep_bwd = _s5_toeplitz(kern, tag)
    to_groups = lambda a: a.reshape(n_chunks, t, S5_GROUPS, S5_GROUP).transpose(2, 0, 1, 3).reshape(S5_GROUPS, n_chunks, t * S5_GROUP)
    from_groups = lambda a: a.reshape(S5_GROUPS, n_chunks, t, S5_GROUP).transpose(1, 2, 0, 3).reshape(seq, D_MODEL)
    ug = to_groups(hn)
    z = _matmul(ug, w_z, name=tag + "_z")
    x_before = _s5_boundary_scan(z.transpose(1, 0, 2), lam_re, lam_im, tag + "_scan")
    xprev = x_before.transpose(1, 0, 2)
    yg = (_matmul(ug, toep, name=tag + "_intra") + _matmul(xprev, w_y, name=tag + "_inter")).astype(BF16)
    act, gate_vjp = jax.vjp(_s5_gate, from_groups(yg), hn, p["d"])
    vg, lin_glu = _linear(act, p["w_glu"], tag + "_glu")
    out, glu_vjp = jax.vjp(_glu, vg)

    def backward(dy):
        d_vg, = glu_vjp(dy)
        d_act = lin_glu(d_vg)
        d_y, d_hn, d_d = gate_vjp(d_act)
        d_yg = to_groups(d_y)
        d_ug = _matmul(d_yg, toep, tb=True, name=tag + "_intra_dx")
        d_toep = _matmul(ug, d_yg, ta=True, out_dtype=BF16, name=tag + "_intra_dw")
        d_xprev = _matmul(d_yg, w_y, tb=True, name=tag + "_inter_dx").transpose(1, 0, 2)
        d_wy = _matmul(xprev, d_yg, ta=True, name=tag + "_inter_dw")
        dz = _s5_boundary_scan(d_xprev, lam_re, -lam_im, tag + "_scan_bwd", reverse=True)
        x_re, x_im, dz_re, dz_im = (x_before[..., :S5_STATE], x_before[..., S5_STATE:], dz[..., :S5_STATE],
                                    dz[..., S5_STATE:])
        d_lam_re = jnp.sum(x_re * dz_re + x_im * dz_im, axis=0)
        d_lam_im = jnp.sum(x_re * dz_im - x_im * dz_re, axis=0)
        d_z = dz.transpose(1, 0, 2)
        d_ug = d_ug + _matmul(d_z, w_z, tb=True, name=tag + "_z_dx")
        d_wz = _matmul(ug, d_z, ta=True, name=tag + "_z_dw")
        grads = dict(zip(names, ops_vjp((toep_bwd(d_toep), d_wz, d_wy, d_lam_re, d_lam_im))))
        grads.update(d=d_d)
        return d_hn + from_groups(d_ug), grads

    return out, backward


_SMALL =[("gla_w_a2", 2), ("gla_b_a", 1), ("gla_norm_g", 1), ("ssd_conv_w", 2), ("s5_d", 1)]
_REPLICATED = ["norm_mix_g", "norm_ffn_g", "ssd_conv_b", "ssd_dt_bias", "ssd_a_log", "ssd_d", "ssd_norm_g", "s5_log_dt",
               "s5_a_re", "s5_a_im", "s5_b_re", "s5_b_im", "s5_c_re", "s5_c_im", "final_norm_g"]
_EARLY_SMALL = [n for n in [s for s, _ in _SMALL] + _REPLICATED if n.startswith("s5_")]
_WEIGHTS = ['norm_mix_g', 'norm_ffn_g', 'gla_w_in', 'gla_w_a2', 'gla_b_a', 'gla_norm_g', 'gla_w_out', 'ssd_w_in',
            'ssd_conv_w', 'ssd_conv_b', 'ssd_dt_bias', 'ssd_a_log', 'ssd_d', 'ssd_norm_g', 'ssd_w_out', 's5_log_dt',
            's5_a_re', 's5_a_im', 's5_b_re', 's5_b_im', 's5_c_re', 's5_c_im', 's5_d', 's5_w_glu', 'ffn_w_gu', 'ffn_w_down',
            'final_norm_g']


def _gather_small(local):
    shapes = [local[n].shape for n, _ in _SMALL]
    blocks = _collective("gather", _pack([local[n] for n, _ in _SMALL]), "gather_vectors")
    parts = _unpack(blocks, shapes, lead=(N_DEV,))
    return {n: _join(part, axis) for (n, axis), part in zip(_SMALL, parts)}


def _forward_plan():
    plan = []
    for i in range(DEPTH):
        j = i // 3
        plan += [[("gla_w_in", j), ("gla_w_out", j)], [("ssd_w_in", j), ("ssd_w_out", j)], [("s5_w_glu", j)]][i % 3]
        plan += [("ffn_w_gu", i), ("ffn_w_down", i)]
    return plan


def _forward_backward(x, target, w, traffic):
    big = lambda name, j: (traffic, (name, j))
    gla = lambda j: dict(w_in=big("gla_w_in", j), w_a2=w["gla_w_a2"][j], b_a=w["gla_b_a"][j], norm_g=w["gla_norm_g"][j],
                         w_out=big("gla_w_out", j))
    ssd = lambda j: dict(w_in=big("ssd_w_in", j), conv_w=w["ssd_conv_w"][j], conv_b=w["ssd_conv_b"][j],
                         dt_bias=w["ssd_dt_bias"][j], a_log=w["ssd_a_log"][j], d=w["ssd_d"][j], norm_g=w["ssd_norm_g"][j],
                         w_out=big("ssd_w_out", j))
    s5 = lambda j: dict(log_dt=w["s5_log_dt"][j], a_re=w["s5_a_re"][j], a_im=w["s5_a_im"][j], b_re=w["s5_b_re"][j],
                        b_im=w["s5_b_im"][j], c_re=w["s5_c_re"][j], c_im=w["s5_c_im"][j], d=w["s5_d"][j],
                        w_glu=big("s5_w_glu", j))
    mixers = [("gla", _gla_mixer, gla), ("ssd", _ssd_mixer, ssd), ("s5", _s5_mixer, s5)]
    base, delta = x, None
    tape = []
    for i in range(DEPTH):
        kind, mixer, params = mixers[i % 3]
        j = i // 3
        h, hn = _add_norm_fwd(base, delta, w["norm_mix_g"][i], f"l{i}_norm_mix")
        y, mixer_bwd = mixer(hn, params(j), f"l{i}_{kind}")
        h_mid, hn2 = _add_norm_fwd(h, y, w["norm_ffn_g"][i], f"l{i}_norm_ffn")
        gu, gu_bwd = _linear(hn2, big("ffn_w_gu", i), f"l{i}_ffn_gu", out_dtype=BF16)
        act = _swiglu_fwd(gu, f"l{i}_swiglu")
        delta, down_bwd = _linear(act, big("ffn_w_down", i), f"l{i}_ffn_down", dx_dtype=BF16)
        base = h_mid
        tape.append((kind, j, h, mixer_bwd, h_mid, gu_bwd, gu, down_bwd))
    loss, d_h, d_final_g = _loss_head(base + delta, w["final_norm_g"], target, "loss_head")

    grads = {n: [None] * w[n].shape[0] for n in w if n != "final_norm_g"}
    grads["final_norm_g"] = d_final_g
    for i in reversed(range(DEPTH)):
        kind, j, h, mixer_bwd, h_mid, gu_bwd, gu, down_bwd = tape[i]
        d_gu = _swiglu_bwd(gu, down_bwd(d_h), f"l{i}_swiglu_bwd")
        d_mid, grads["norm_ffn_g"][i] = _norm_bwd(h_mid, w["norm_ffn_g"][i], gu_bwd(d_gu), d_h, f"l{i}_norm_ffn_bwd")
        d_hn, mixer_grads = mixer_bwd(d_mid)
        for k, g in mixer_grads.items():
            grads[f"{kind}_{k}"][j] = g
        if kind == "s5" and all(g is not None for n in _EARLY_SMALL for g in grads[n]):
            traffic.gather_early(_pack([jnp.stack(grads[n]) for n in _EARLY_SMALL]))
        d_h, grads["norm_mix_g"][i] = _norm_bwd(h, w["norm_mix_g"][i], d_hn, d_mid, f"l{i}_norm_mix_bwd")
    return loss, d_h, grads


def kernel(x, norm_mix_g, norm_ffn_g, gla_w_in, gla_w_a2, gla_b_a, gla_norm_g, gla_w_out, ssd_w_in, ssd_conv_w, ssd_conv_b, ssd_dt_bias, ssd_a_log, ssd_d, ssd_norm_g, ssd_w_out, s5_log_dt, s5_a_re, s5_a_im, s5_b_re, s5_b_im, s5_c_re, s5_c_im, s5_d, s5_w_glu, ffn_w_gu, ffn_w_down, final_norm_g, loss_target, m_norm_mix_g, m_norm_ffn_g, m_gla_w_in, m_gla_w_a2, m_gla_b_a, m_gla_norm_g, m_gla_w_out, m_ssd_w_in, m_ssd_conv_w, m_ssd_conv_b, m_ssd_dt_bias, m_ssd_a_log, m_ssd_d, m_ssd_norm_g, m_ssd_w_out, m_s5_log_dt, m_s5_a_re, m_s5_a_im, m_s5_b_re, m_s5_b_im, m_s5_c_re, m_s5_c_im, m_s5_d, m_s5_w_glu, m_ffn_w_gu, m_ffn_w_down, m_final_norm_g, v_norm_mix_g, v_norm_ffn_g, v_gla_w_in, v_gla_w_a2, v_gla_b_a, v_gla_norm_g, v_gla_w_out, v_ssd_w_in, v_ssd_conv_w, v_ssd_conv_b, v_ssd_dt_bias, v_ssd_a_log, v_ssd_d, v_ssd_norm_g, v_ssd_w_out, v_s5_log_dt, v_s5_a_re, v_s5_a_im, v_s5_b_re, v_s5_b_im, v_s5_c_re, v_s5_c_im, v_s5_d, v_s5_w_glu, v_ffn_w_gu, v_ffn_w_down, v_final_norm_g):
    args = locals()
    local = {n: args[n] for n in _WEIGHTS}
    moment_m = {n: args["m_" + n] for n in _WEIGHTS}
    moment_v = {n: args["v_" + n] for n in _WEIGHTS}

    shards = {(n, layer): local[n][layer].astype(BF16) for n, _ in _BIG for layer in range(local[n].shape[0])}
    traffic = _Traffic(shards, _forward_plan())
    full = {n: local[n] for n in _REPLICATED}
    full.update(_gather_small(local))

    loss, d_x, grads = _forward_backward(x[0], loss_target[0], full, traffic)
    traffic.flush()
    loss = lax.psum(loss, ("x", "y", "c"))
    kinds = ("grad", "delta", "new_m", "new_v")
    out = {}

    for n, _ in _BIG:
        parts = jnp.stack([traffic.received[(n, layer)] for layer in range(local[n].shape[0])], axis=1)
        results = _adamw(parts, local[n], moment_m[n], moment_v[n], "adamw_" + n)
        out.update({f"{kind}_{n}": a for kind, a in zip(kinds, results)})

    small = [n for n, _ in _SMALL] + _REPLICATED
    stacked = lambda n: grads[n] if n == "final_norm_g" else jnp.stack(grads[n])
    late = [n for n in small if n not in _EARLY_SMALL]
    gathered = [(_EARLY_SMALL, traffic.early, "early"),
                (late, _collective("gather", _pack([stacked(n) for n in late]), "gather_small_gradients"), "late")]
    summed = {}
    for names, parts, tag in gathered:
        sums = _unpack(_sum_parts(parts, "sum_small_gradients_" + tag), [stacked(n).shape for n in names])
        summed.update(zip(names, sums))
    position = _index(_mesh_position())
    mine = [_own_shard(summed[n], axis, position) for n, axis in _SMALL] + [summed[n] for n in _REPLICATED]
    shapes = [local[n].shape for n in small]
    pk = lambda arrays: _pack(arrays)[None]
    results = _adamw(pk(mine)[None], pk([local[n] for n in small]), pk([moment_m[n] for n in small]),
                     pk([moment_v[n] for n in small]), "adamw_small")
    for kind, flat in zip(kinds, results):
        out.update({f"{kind}_{n}": a for n, a in zip(small, _unpack(flat[0], shapes))})

    return (loss, d_x[None], *[out[f"{kind}_{n}"] for kind in ("grad", "delta", "new_m", "new_v") for n in _WEIGHTS])
```

```python
import functools
import math

import jax
import jax.numpy as jnp
import numpy as np
from jax import lax
from jax.experimental import pallas as pl
from jax.experimental.pallas import tpu as pltpu

F32 = jnp.float32
BF16 = jnp.bfloat16
_MXU_DTYPE = jnp.bfloat16

N_DEV = 8
D_MODEL = 1024
DEPTH = 4
CHUNK = 64
STEP_CHUNKS = 1
STEP = CHUNK * STEP_CHUNKS
EPS = 1e-6
GLA_HEADS, GLA_DK, GLA_DV, GLA_RANK, GLA_TAU = 4, 128, 256, 16, 16.0
GLA_QK = GLA_HEADS * GLA_DK
GLA_VD = GLA_HEADS * GLA_DV
LANES = 128
GLA_IN = 2 * GLA_QK + 2 * GLA_VD + GLA_RANK
GLA_PROJ = 2 * GLA_QK + 2 * GLA_VD + LANES
SSD_DINNER, SSD_HEADDIM, SSD_HEADS, SSD_GROUPS, SSD_HPG, SSD_DSTATE, SSD_CONV = 2048, 64, 32, 8, 4, 128, 4
SSD_GN = SSD_GROUPS * SSD_DSTATE
SSD_GW = SSD_HPG * SSD_HEADDIM
SSD_XBC = SSD_DINNER + 2 * SSD_GN
SSD_IN = SSD_DINNER + SSD_XBC + SSD_HEADS
SSD_PROJ = SSD_DINNER + SSD_XBC + LANES
S5_GROUP, S5_GROUPS, S5_STATE = 16, 64, 64
S5_CHUNK = 16
FFN_HIDDEN = 2816
ADAM_LR, ADAM_B1, ADAM_B2, ADAM_EPS, ADAM_WD, ADAM_STEP = 0.001, 0.9, 0.999, 1e-08, 0.01, 10
VMEM_LIMIT = 48 * 1024 * 1024
FLAT_COLS = 1024
FLAT_ROWS_ALIGN = 64


def _tile(n, cap, unit):
    if n <= cap:
        return n
    best = None
    for t in range(unit, cap + 1, unit):
        if n % t == 0:
            best = t
    assert best is not None, (n, cap, unit)
    return best


def _divisors(n, unit):
    return sorted({t for t in range(unit, n + 1, unit) if n % t == 0} | {n})


_MXU_FLOPS, _HBM_BYTES, _ACC_BYTES, _STEP_SECONDS = 1.1e15, 3e12, 1.1e13, 3.5e-7
_MXU_ROWS = 256
_TILE_VMEM_BUDGET = 36 * 1024 * 1024
_BATCH_VMEM_BUDGET = 16 * 1024 * 1024


def _pick_tiles(m, n, k, a_bytes, b_bytes, o_bytes, m_unit):
    best = None
    for tm in _divisors(m, m_unit):
        for tn in _divisors(n, LANES):
            for tk in _divisors(k, LANES):
                nk = k // tk
                vmem = 2 * tm * tk * a_bytes + 2 * tk * tn * b_bytes + 2 * tm * tn * o_bytes + (nk > 1) * tm * tn * 4
                if vmem > _TILE_VMEM_BUDGET or tm > 2048 or tn > 2048:
                    continue
                a_reads = n // tn if nk > 1 else 1
                b_reads = 1 if (nk == 1 and n == tn) else m // tm
                traffic = m * k * a_bytes * a_reads + k * n * b_bytes * b_reads + m * n * o_bytes
                mxu = 2.0 * m * n * k / _MXU_FLOPS * (1.0 + _MXU_ROWS / tm)
                cost = (max(mxu, traffic / _HBM_BYTES) + (nk > 1) * nk * m * n * 8 / _ACC_BYTES
                        + (m // tm) * (n // tn) * nk * _STEP_SECONDS)
                if best is None or cost < best[0]:
                    best = (cost, tm, tn, tk)
    assert best is not None, (m, n, k)
    return best[1:]


def _cargo_call(body, cargo, *, name, grid, in_specs, out_specs, out_shape, scratch_shapes, semantics):
    params = lambda sem: pltpu.CompilerParams(dimension_semantics=sem, vmem_limit_bytes=VMEM_LIMIT)
    if not cargo:
        return pl.pallas_call(body, name=name, grid=grid, in_specs=in_specs, out_specs=out_specs, out_shape=out_shape,
                              scratch_shapes=scratch_shapes, compiler_params=params(semantics))
    n_in, n_out, n_scratch, n_cargo = len(in_specs), len(out_specs), len(scratch_shapes), len(cargo)

    def loaded(*refs):
        ins, cargo_in, rest = refs[:n_in], refs[n_in:n_in + n_cargo], refs[n_in + n_cargo:]
        outs, cargo_out, rest = rest[:n_out], rest[n_out:n_out + n_cargo], rest[n_out + n_cargo:]
        scratch, sems = rest[:n_scratch], rest[n_scratch:]
        ids = [pl.program_id(d) for d in range(len(grid))]
        first = functools.reduce(jnp.logical_and, [i == 0 for i in ids])
        last = functools.reduce(jnp.logical_and, [i == g - 1 for i, g in zip(ids, grid)])
        moves = lambda: [_moves(kind, x_ref, y_ref, *sems[3 * c:3 * c + 3])
                         for c, ((kind, _), x_ref, y_ref) in enumerate(zip(cargo, cargo_in, cargo_out))]

        @pl.when(first)
        def _():
            for mv in moves():
                _start(mv)

        body(*ins, *outs, *scratch)

        @pl.when(last)
        def _():
            for mv in moves():
                _finish(mv)

    sems = [pltpu.SemaphoreType.DMA((N_DEV - 1,)), pltpu.SemaphoreType.DMA((N_DEV - 1,)), pltpu.SemaphoreType.DMA] * n_cargo
    call = pl.pallas_call(
        loaded, name=name, grid=grid, in_specs=list(in_specs) + [_ANY] * n_cargo,
        out_specs=list(out_specs) + [_ANY] * n_cargo,
        out_shape=list(out_shape) + [_moved_shape(kind, x) for kind, x in cargo],
        scratch_shapes=list(scratch_shapes) + sems, compiler_params=params(("arbitrary",) * len(grid)))

    def run(*args):
        results = call(*args, *[x for _, x in cargo])
        return list(results[:n_out]), list(results[n_out:])

    return run


def _matmul(a, b, *, ta=False, tb=False, out_dtype=F32, name, cargo=()):
    batched = a.ndim == 3
    if ta:
        k_dim, m_dim = a.shape[-2:]
    else:
        m_dim, k_dim = a.shape[-2:]
    if tb:
        n_dim, kb = b.shape[-2:]
    else:
        kb, n_dim = b.shape[-2:]
    assert kb == k_dim, (a.shape, b.shape, ta, tb)
    tm, tn, tk = _pick_tiles(m_dim, n_dim, k_dim, a.dtype.itemsize, b.dtype.itemsize, jnp.dtype(out_dtype).itemsize,
                             LANES if ta else 16)
    nk = k_dim // tk
    ca, cb = (0 if ta else 1), (1 if tb else 0)
    grid = (m_dim // tm, n_dim // tn, nk)
    gb = 1
    if batched:
        step_bytes = 2 * (tm * tk * a.dtype.itemsize + tk * tn * b.dtype.itemsize + tm * tn * jnp.dtype(out_dtype).itemsize)
        gb = max(g for g in _divisors(a.shape[0], 1) if g * step_bytes <= _BATCH_VMEM_BUDGET or g == 1)
        grid = (a.shape[0] // gb,) + grid
    dims = (((ca + 1,), (cb + 1,)), ((0,), (0,))) if batched else (((ca,), (cb,)), ((), ()))

    def body(a_ref, b_ref, o_ref, *acc):
        part = lax.dot_general(a_ref[...].astype(_MXU_DTYPE), b_ref[...].astype(_MXU_DTYPE), dims,
                               preferred_element_type=F32)
        if nk == 1:
            o_ref[...] = part.astype(o_ref.dtype)
            return
        acc_ref, = acc
        k = pl.program_id(len(grid) - 1)

        @pl.when(k == 0)
        def _():
            acc_ref[...] = part

        @pl.when(k > 0)
        def _():
            acc_ref[...] += part

        @pl.when(k == nk - 1)
        def _():
            o_ref[...] = acc_ref[...].astype(o_ref.dtype)

    def spec(shape, fn):
        if batched:
            return pl.BlockSpec((gb,) + shape, lambda g, i, j, k: (g,) + fn(i, j, k))
        return pl.BlockSpec(shape, fn)

    a_spec = spec((tk, tm), lambda i, j, k: (k, i)) if ta else spec((tm, tk), lambda i, j, k: (i, k))
    b_spec = spec((tn, tk), lambda i, j, k: (j, k)) if tb else spec((tk, tn), lambda i, j, k: (k, j))
    o_spec = spec((tm, tn), lambda i, j, k: (i, j))
    out_shape = ((a.shape[0],) if batched else ()) + (m_dim, n_dim)
    call = _cargo_call(
        body, cargo, name=name, grid=grid, in_specs=[a_spec, b_spec], out_specs=[o_spec],
        out_shape=[jax.ShapeDtypeStruct(out_shape, out_dtype)],
        scratch_shapes=[pltpu.VMEM(((gb,) if batched else ()) + (tm, tn), F32)] if nk > 1 else [],
        semantics=("parallel",) * (len(grid) - 1) + ("arbitrary",))
    if not cargo:
        return call(a, b)[0]
    results, moved = call(a, b)
    return results[0], moved


def _dot(a, b, ca=1, cb=0, exact=False):
    if exact:
        return lax.dot_general(a, b, (((ca,), (cb,)), ((), ())), precision=lax.Precision.HIGHEST,
                               preferred_element_type=F32)
    return lax.dot_general(a.astype(_MXU_DTYPE), b.astype(_MXU_DTYPE), (((ca,), (cb,)), ((), ())),
                           preferred_element_type=F32)


def _tri(n):
    return lax.broadcasted_iota(jnp.int32, (n, n), 0) >= lax.broadcasted_iota(jnp.int32, (n, n), 1)


def _log_sigmoid(x):
    return jnp.minimum(x, 0.0) - jnp.log(1.0 + jnp.exp(-jnp.abs(x)))


def _softplus(x):
    return jnp.maximum(x, 0.0) + jnp.log(1.0 + jnp.exp(-jnp.abs(x)))


def _silu(x):
    return x / (1.0 + jnp.exp(-x))


def _full_spec(shape):
    return pl.BlockSpec(shape, lambda c: (0,) * len(shape))


def _gla_chunk(proj, st, w_a2, b_a, norm_g):
    t = proj.shape[0]
    q = proj[:, 0:GLA_QK] * (GLA_DK ** -0.5)
    k = proj[:, GLA_QK:2 * GLA_QK]
    v = proj[:, 2 * GLA_QK:2 * GLA_QK + GLA_VD]
    r = proj[:, 2 * GLA_QK + GLA_VD:2 * GLA_QK + 2 * GLA_VD]
    a_low = proj[:, 2 * GLA_QK + 2 * GLA_VD:]
    log_a = _log_sigmoid(_dot(a_low, w_a2) + b_a) * (1.0 / GLA_TAU)
    past = _tri(t)
    lc = _dot(past.astype(F32), log_a, exact=True)
    lend = lc[t - 1:t, :]
    e_pos = jnp.exp(lc)
    e_neg = jnp.exp(-lc)
    q_fwd, k_fwd, q_bwd, k_bwd = q * e_pos, k * e_neg, q * e_neg, k * e_pos
    kd = k * jnp.exp(lend - lc)
    g = jnp.exp(lend)
    outs, new_st = [], []
    for h in range(GLA_HEADS):
        sk = slice(h * GLA_DK, (h + 1) * GLA_DK)
        sv = slice(h * GLA_DV, (h + 1) * GLA_DV)
        s_past = _dot(q_fwd[:, sk], k_fwd[:, sk], 1, 1)
        s_future = _dot(q_bwd[:, sk], k_bwd[:, sk], 1, 1)
        scores = jnp.where(past, s_past, s_future)
        o = _dot(scores, v[:, sv]) + _dot(q_fwd[:, sk], st[h], 1, 1)
        new_st.append(st[h] * g[:, sk] + _dot(v[:, sv], kd[:, sk], 0, 0))
        o = o * lax.rsqrt(jnp.mean(o * o, axis=-1, keepdims=True) + EPS) * norm_g[:, sv]
        outs.append(o)
    return jnp.concatenate(outs, axis=1) * _silu(r), tuple(new_st)


_GLA_STATE = (GLA_HEADS, GLA_DV, GLA_DK)


def _gla_step(proj, st, w_a2, b_a, norm_g):
    outs = []
    for s in range(STEP_CHUNKS):
        out, st = _gla_chunk(proj[s * CHUNK:(s + 1) * CHUNK], st, w_a2, b_a, norm_g)
        outs.append(out)
    return jnp.concatenate(outs, axis=0), st


def _gla_core_fwd(proj, w_a2, b_a, norm_g, cargo=()):
    seq = proj.shape[0]
    nc = seq // STEP

    def body(proj_ref, wa_ref, ba_ref, ng_ref, o_ref, sprev_ref, st_ref):
        @pl.when(pl.program_id(0) == 0)
        def _():
            st_ref[...] = jnp.zeros_like(st_ref)

        st = tuple(st_ref[h] for h in range(GLA_HEADS))
        for h in range(GLA_HEADS):
            sprev_ref[0, h] = st[h]
        out, new_st = _gla_step(proj_ref[...], st, wa_ref[...], ba_ref[...], ng_ref[...])
        o_ref[...] = out
        for h in range(GLA_HEADS):
            st_ref[h] = new_st[h]

    return _cargo_call(
        body, cargo, name="gla_core_fwd", grid=(nc,),
        in_specs=[pl.BlockSpec((STEP,GLA_PROJ), lambda c: (c, 0)), _full_spec(w_a2.shape), _full_spec(b_a.shape),
                  _full_spec(norm_g.shape)],
        out_specs=[pl.BlockSpec((STEP,GLA_VD), lambda c: (c, 0)), pl.BlockSpec((1,) + _GLA_STATE, lambda c: (c, 0, 0, 0))],
        out_shape=[jax.ShapeDtypeStruct((seq, GLA_VD), F32), jax.ShapeDtypeStruct((nc,) + _GLA_STATE, F32)],
        scratch_shapes=[pltpu.VMEM(_GLA_STATE, F32)],
        semantics=("arbitrary",),
    )(proj, w_a2, b_a, norm_g)


def _gla_core_bwd(proj, sprev, d_out, w_a2, b_a, norm_g, cargo=()):
    seq = proj.shape[0]
    nc = seq // STEP

    def body(proj_ref, sprev_ref, do_ref, wa_ref, ba_ref, ng_ref, dproj_ref, dwa_ref, dba_ref, dng_ref, dst_ref):
        @pl.when(pl.program_id(0) == 0)
        def _():
            dst_ref[...] = jnp.zeros_like(dst_ref)
            dwa_ref[...] = jnp.zeros_like(dwa_ref)
            dba_ref[...] = jnp.zeros_like(dba_ref)
            dng_ref[...] = jnp.zeros_like(dng_ref)

        st = tuple(sprev_ref[0, h] for h in range(GLA_HEADS))
        _, vjp = jax.vjp(_gla_step, proj_ref[...], st, wa_ref[...], ba_ref[...], ng_ref[...])
        d_next = tuple(dst_ref[h] for h in range(GLA_HEADS))
        d_proj, d_st, d_wa, d_ba, d_ng = vjp((do_ref[...], d_next))
        dproj_ref[...] = d_proj.astype(dproj_ref.dtype)
        for h in range(GLA_HEADS):
            dst_ref[h] = d_st[h]
        dwa_ref[...] += d_wa
        dba_ref[...] += d_ba
        dng_ref[...] += d_ng

    rev = lambda c: (nc - 1 - c, 0)
    return _cargo_call(
        body, cargo, name="gla_core_bwd", grid=(nc,),
        in_specs=[pl.BlockSpec((STEP,GLA_PROJ), rev), pl.BlockSpec((1,) + _GLA_STATE, lambda c: (nc - 1 - c, 0, 0, 0)),
                  pl.BlockSpec((STEP,GLA_VD), rev), _full_spec(w_a2.shape), _full_spec(b_a.shape), _full_spec(norm_g.shape)],
        out_specs=[pl.BlockSpec((STEP,GLA_PROJ), rev), _full_spec(w_a2.shape), _full_spec(b_a.shape), _full_spec(norm_g.shape)],
        out_shape=[jax.ShapeDtypeStruct((seq, GLA_PROJ), BF16), jax.ShapeDtypeStruct(w_a2.shape, F32),
                   jax.ShapeDtypeStruct(b_a.shape, F32), jax.ShapeDtypeStruct(norm_g.shape, F32)],
        scratch_shapes=[pltpu.VMEM(_GLA_STATE, F32)],
        semantics=("arbitrary",),
    )(proj, sprev, d_out, w_a2, b_a, norm_g)


def _ssd_chunk(z, xbc, dt_raw, hs, dt_bias, a_log, d_skip, norm_g):
    t = z.shape[0]
    xs = xbc[:, :SSD_DINNER]
    bm = xbc[:, SSD_DINNER:SSD_DINNER + SSD_GN]
    cm = xbc[:, SSD_DINNER + SSD_GN:]
    dt = _softplus(dt_raw + dt_bias)
    da = dt * (-jnp.exp(a_log))
    tri = _tri(t).astype(F32)
    eye = (lax.broadcasted_iota(jnp.int32, (t, t), 0) == lax.broadcasted_iota(jnp.int32, (t, t), 1)).astype(F32)
    cum = _dot(tri, da, exact=True)
    cum_t = _dot(da, tri, 0, 1, exact=True)
    dt_t = _dot(dt, eye, 0, 0, exact=True)
    cum_end = cum[t - 1:t, :]
    w_state = dt * jnp.exp(cum_end - cum)
    e_cum = jnp.exp(cum)
    g_end = jnp.exp(cum_end)
    head_of = lambda axis: lax.shift_right_logical(lax.broadcasted_iota(jnp.int32, (SSD_GW, SSD_GW), axis),
                                                   jnp.int32(SSD_HEADDIM.bit_length() - 1))
    same_head = head_of(0) == head_of(1)
    ys, new_hs = [], []
    for g in range(SSD_GROUPS):
        heads = range(g * SSD_HPG, (g + 1) * SSD_HPG)
        cols = slice(g * SSD_GW, (g + 1) * SSD_GW)

        def spread(a):
            return jnp.concatenate([jnp.broadcast_to(a[:, h:h + 1], (a.shape[0], SSD_HEADDIM)) for h in heads], axis=1)

        def row(a_t):
            return jnp.concatenate([a_t[h:h + 1, :] for h in heads], axis=1)

        bm_g = bm[:, g * SSD_DSTATE:(g + 1) * SSD_DSTATE]
        cm_g = cm[:, g * SSD_DSTATE:(g + 1) * SSD_DSTATE]
        xs_g = xs[:, cols]
        cb = _dot(cm_g, jnp.concatenate([bm_g] * SSD_HPG, axis=0), 1, 1)
        mix = cb * jnp.exp(-jnp.abs(spread(cum) - row(cum_t))) * row(dt_t)
        x_diag = jnp.where(same_head, jnp.concatenate([xs_g] * SSD_HPG, axis=0), 0.0)
        y = _dot(mix, x_diag)
        y = y + _dot(cm_g, hs[g], 1, 1) * spread(e_cum)
        y = y + spread(d_skip) * xs_g
        states = _dot(xs_g * spread(w_state), bm_g, 0, 0)
        decayed = jnp.concatenate([g_end[:, h:h + 1] * hs[g][j * SSD_HEADDIM:(j + 1) * SSD_HEADDIM, :]
                                   for j, h in enumerate(heads)], axis=0)
        new_hs.append(decayed + states)
        yg = y * _silu(z[:, cols])
        ys.append(yg * lax.rsqrt(jnp.mean(yg * yg, axis=-1, keepdims=True) + EPS) * norm_g[:, cols])
    return jnp.concatenate(ys, axis=1), tuple(new_hs)


_SSD_STATE = (SSD_GROUPS, SSD_GW, SSD_DSTATE)


def _ssd_step(z, xbc, dt_raw, hs, dt_bias, a_log, d_skip, norm_g):
    outs = []
    for s in range(STEP_CHUNKS):
        rows = slice(s * CHUNK, (s + 1) * CHUNK)
        out, hs = _ssd_chunk(z[rows], xbc[rows], dt_raw[rows], hs, dt_bias, a_log, d_skip, norm_g)
        outs.append(out)
    return jnp.concatenate(outs, axis=0), hs
_SSD_DT_BLOCK = (SSD_DINNER + SSD_XBC) // LANES


def _ssd_core_fwd(proj, xbc, dt_bias, a_log, d_skip, norm_g, cargo=()):
    seq = proj.shape[0]
    nc = seq // STEP

    def body(z_ref, xbc_ref, dt_ref, db_ref, al_ref, ds_ref, ng_ref, o_ref, hprev_ref, hs_ref):
        @pl.when(pl.program_id(0) == 0)
        def _():
            hs_ref[...] = jnp.zeros_like(hs_ref)

        hs = tuple(hs_ref[g] for g in range(SSD_GROUPS))
        for g in range(SSD_GROUPS):
            hprev_ref[0, g] = hs[g]
        out, new_hs = _ssd_step(z_ref[...], xbc_ref[...], dt_ref[...], hs, db_ref[...], al_ref[...], ds_ref[...], ng_ref[...])
        o_ref[...] = out
        for g in range(SSD_GROUPS):
            hs_ref[g] = new_hs[g]

    return _cargo_call(
        body, cargo, name="ssd_core_fwd", grid=(nc,),
        in_specs=[pl.BlockSpec((STEP,SSD_DINNER), lambda c: (c, 0)), pl.BlockSpec((STEP,SSD_XBC), lambda c: (c, 0)),
                  pl.BlockSpec((STEP,LANES), lambda c: (c, _SSD_DT_BLOCK)),
                  _full_spec(dt_bias.shape), _full_spec(a_log.shape), _full_spec(d_skip.shape), _full_spec(norm_g.shape)],
        out_specs=[pl.BlockSpec((STEP,SSD_DINNER), lambda c: (c, 0)), pl.BlockSpec((1,) + _SSD_STATE, lambda c: (c, 0, 0, 0))],
        out_shape=[jax.ShapeDtypeStruct((seq, SSD_DINNER), F32), jax.ShapeDtypeStruct((nc,) + _SSD_STATE, F32)],
        scratch_shapes=[pltpu.VMEM(_SSD_STATE, F32)],
        semantics=("arbitrary",),
    )(proj, xbc, proj, dt_bias, a_log, d_skip, norm_g)


def _ssd_core_bwd(proj, xbc, hprev, d_out, dt_bias, a_log, d_skip, norm_g, cargo=()):
    seq = proj.shape[0]
    nc = seq // STEP

    def body(z_ref, xbc_ref, dt_ref, hprev_ref, do_ref, db_ref, al_ref, ds_ref, ng_ref,
             dz_ref, dxbc_ref, ddt_ref, ddb_ref, dal_ref, dds_ref, dng_ref, dhs_ref):
        @pl.when(pl.program_id(0) == 0)
        def _():
            dhs_ref[...] = jnp.zeros_like(dhs_ref)
            ddb_ref[...] = jnp.zeros_like(ddb_ref)
            dal_ref[...] = jnp.zeros_like(dal_ref)
            dds_ref[...] = jnp.zeros_like(dds_ref)
            dng_ref[...] = jnp.zeros_like(dng_ref)

        hs = tuple(hprev_ref[0, g] for g in range(SSD_GROUPS))
        _, vjp = jax.vjp(_ssd_step, z_ref[...], xbc_ref[...], dt_ref[...], hs, db_ref[...], al_ref[...], ds_ref[...], ng_ref[...])
        d_next = tuple(dhs_ref[g] for g in range(SSD_GROUPS))
        d_z, d_xbc, d_dt, d_hs, d_db, d_al, d_ds, d_ng = vjp((do_ref[...], d_next))
        dz_ref[...] = d_z.astype(dz_ref.dtype)
        dxbc_ref[...] = d_xbc
        ddt_ref[...] = d_dt.astype(ddt_ref.dtype)
        for g in range(SSD_GROUPS):
            dhs_ref[g] = d_hs[g]
        ddb_ref[...] += d_db
        dal_ref[...] += d_al
        dds_ref[...] += d_ds
        dng_ref[...] += d_ng

    rev = lambda c: (nc - 1 - c, 0)
    vec = [_full_spec(dt_bias.shape), _full_spec(a_log.shape), _full_spec(d_skip.shape), _full_spec(norm_g.shape)]
    return _cargo_call(
        body, cargo, name="ssd_core_bwd", grid=(nc,),
        in_specs=[pl.BlockSpec((STEP,SSD_DINNER), rev), pl.BlockSpec((STEP,SSD_XBC), rev),
                  pl.BlockSpec((STEP,LANES), lambda c: (nc - 1 - c, _SSD_DT_BLOCK)),
                  pl.BlockSpec((1,) + _SSD_STATE, lambda c: (nc - 1 - c, 0, 0, 0)),
                  pl.BlockSpec((STEP,SSD_DINNER), rev)] + vec,
        out_specs=[pl.BlockSpec((STEP,SSD_DINNER), rev), pl.BlockSpec((STEP,SSD_XBC), rev),
                   pl.BlockSpec((STEP,LANES), rev)] + vec,
        out_shape=[jax.ShapeDtypeStruct((seq, SSD_DINNER), BF16), jax.ShapeDtypeStruct((seq, SSD_XBC), F32),
                   jax.ShapeDtypeStruct((seq, LANES), BF16),
                   jax.ShapeDtypeStruct(dt_bias.shape, F32), jax.ShapeDtypeStruct(a_log.shape, F32),
                   jax.ShapeDtypeStruct(d_skip.shape, F32), jax.ShapeDtypeStruct(norm_g.shape, F32)],
        scratch_shapes=[pltpu.VMEM(_SSD_STATE, F32)],
        semantics=("arbitrary",),
    )(proj, xbc, proj, hprev, d_out, dt_bias, a_log, d_skip, norm_g)


CONV_COLS = 2048
CONV_HALO = 8


def _conv_taps(xx, rows):
    last = SSD_CONV - 1
    return [pltpu.roll(xx, last - k, 0)[CONV_HALO:CONV_HALO + rows] if k < last else xx[CONV_HALO:CONV_HALO + rows]
            for k in range(SSD_CONV)]


def _ssd_conv_fwd(proj, conv_w, conv_b, name):
    rows = proj.shape[0]
    tr = _tile(rows, 512, CONV_HALO)
    first_col = SSD_DINNER // CONV_COLS

    def body(x_ref, halo_ref, w_ref, b_ref, o_ref):
        halo = jnp.where(pl.program_id(0) == 0, 0.0, halo_ref[...])
        taps = _conv_taps(jnp.concatenate([halo, x_ref[...]], axis=0), tr)
        out = b_ref[...]
        for k in range(SSD_CONV):
            out = out + taps[k] * w_ref[k:k + 1, :]
        o_ref[...] = _silu(out)

    return pl.pallas_call(
        body, name=name, grid=(rows // tr, SSD_XBC // CONV_COLS),
        in_specs=[pl.BlockSpec((tr, CONV_COLS), lambda i, j: (i, first_col + j)),
                  pl.BlockSpec((CONV_HALO, CONV_COLS), lambda i, j: (jnp.maximum(i * (tr // CONV_HALO) - 1, 0), first_col + j)),
                  pl.BlockSpec((SSD_CONV, CONV_COLS), lambda i, j: (0, j)), pl.BlockSpec((1, CONV_COLS), lambda i, j: (0, j))],
        out_specs=pl.BlockSpec((tr, CONV_COLS), lambda i, j: (i, j)),
        out_shape=jax.ShapeDtypeStruct((rows, SSD_XBC), F32),
        compiler_params=pltpu.CompilerParams(dimension_semantics=("parallel", "parallel"), vmem_limit_bytes=VMEM_LIMIT),
    )(proj, proj, conv_w, conv_b[None])


def _ssd_conv_bwd(proj, d_xbc, conv_w, conv_b, name):
    rows = proj.shape[0]
    tr = _tile(rows, 512, CONV_HALO)
    nb, halos = rows // tr, tr // CONV_HALO
    first_col = SSD_DINNER // CONV_COLS

    def body(x_ref, before_ref, after_ref, d_ref, d_after_ref, w_ref, b_ref, dx_ref, dw_ref, db_ref):
        i = pl.program_id(1)

        @pl.when(i == 0)
        def _():
            dw_ref[...] = jnp.zeros_like(dw_ref)
            db_ref[...] = jnp.zeros_like(db_ref)

        before = jnp.where(i == 0, 0.0, before_ref[...])
        taps = _conv_taps(jnp.concatenate([before, x_ref[...], after_ref[...]], axis=0), tr + CONV_HALO)
        out = b_ref[...]
        for k in range(SSD_CONV):
            out = out + taps[k] * w_ref[k:k + 1, :]
        sig = 1.0 / (1.0 + jnp.exp(-out))
        d_after = jnp.where(i == nb - 1, 0.0, d_after_ref[...])
        d_out = jnp.concatenate([d_ref[...], d_after], axis=0) * sig * (1.0 + out * (1.0 - sig))
        d_x = d_out[:tr] * w_ref[SSD_CONV - 1:SSD_CONV, :]
        for k in range(SSD_CONV - 1):
            ahead = SSD_CONV - 1 - k
            d_x = d_x + pltpu.roll(d_out, tr + CONV_HALO - ahead, 0)[:tr] * w_ref[k:k + 1, :]
        dx_ref[...] = d_x.astype(dx_ref.dtype)
        for k in range(SSD_CONV):
            dw_ref[k:k + 1, :] += jnp.sum(d_out[:tr] * taps[k][:tr], axis=0, keepdims=True)
        db_ref[...] += jnp.sum(d_out[:tr], axis=0, keepdims=True)

    before = lambda j, i: jnp.maximum(i * halos - 1, 0)
    after = lambda j, i: jnp.minimum((i + 1) * halos, nb * halos - 1)
    d_x, d_w, d_b = pl.pallas_call(
        body, name=name, grid=(SSD_XBC // CONV_COLS, nb),
        in_specs=[pl.BlockSpec((tr, CONV_COLS), lambda j, i: (i, first_col + j)),
                  pl.BlockSpec((CONV_HALO, CONV_COLS), lambda j, i: (before(j, i), first_col + j)),
                  pl.BlockSpec((CONV_HALO, CONV_COLS), lambda j, i: (after(j, i), first_col + j)),
                  pl.BlockSpec((tr, CONV_COLS), lambda j, i: (i, j)),
                  pl.BlockSpec((CONV_HALO, CONV_COLS), lambda j, i: (after(j, i), j)),
                  pl.BlockSpec((SSD_CONV, CONV_COLS), lambda j, i: (0, j)), pl.BlockSpec((1, CONV_COLS), lambda j, i: (0, j))],
        out_specs=[pl.BlockSpec((tr, CONV_COLS), lambda j, i: (i, j)), pl.BlockSpec((SSD_CONV, CONV_COLS), lambda j, i: (0, j)),
                   pl.BlockSpec((1, CONV_COLS), lambda j, i: (0, j))],
        out_shape=[jax.ShapeDtypeStruct((rows, SSD_XBC), BF16), jax.ShapeDtypeStruct((SSD_CONV, SSD_XBC), F32),
                   jax.ShapeDtypeStruct((1, SSD_XBC), F32)],
        compiler_params=pltpu.CompilerParams(dimension_semantics=("parallel", "arbitrary"), vmem_limit_bytes=VMEM_LIMIT),
    )(proj, proj, proj, d_xbc, d_xbc, conv_w, conv_b[None])
    return d_x, d_w, d_b[0]


def _s5_boundary_scan(z, lam_re, lam_im, name, reverse=False):
    n_chunks, groups, width = z.shape
    tn = _tile(n_chunks, 128, 1)
    blocks = n_chunks // tn
    lam_a = jnp.concatenate([lam_re, lam_re], axis=1)
    lam_b = jnp.concatenate([-lam_im, lam_im], axis=1)

    def body(z_ref, a_ref, b_ref, x_ref, carry_ref):
        @pl.when(pl.program_id(0) == 0)
        def _():
            carry_ref[...] = jnp.zeros_like(carry_ref)

        a, b = a_ref[...], b_ref[...]

        def step(i, x):
            n = tn - 1 - i if reverse else i
            x_ref[n] = x
            return a * x + b * pltpu.roll(x, width // 2, 1) + z_ref[n]

        carry_ref[...] = lax.fori_loop(0, tn, step, carry_ref[...])

    block = pl.BlockSpec((tn, groups, width), (lambda i: (blocks - 1 - i, 0, 0)) if reverse else (lambda i: (i, 0, 0)))
    return pl.pallas_call(
        body, name=name, grid=(blocks,), in_specs=[block, _full_spec((groups, width)), _full_spec((groups, width))],
        out_specs=block, out_shape=jax.ShapeDtypeStruct(z.shape, F32), scratch_shapes=[pltpu.VMEM((groups, width), F32)],
        compiler_params=pltpu.CompilerParams(dimension_semantics=("arbitrary",), vmem_limit_bytes=VMEM_LIMIT),
    )(z, lam_a, lam_b)


_FLIPS = [(kx, ky, kc) for kx in (0, 1) for ky in (0, 1) for kc in (0, 1)][1:]


def _mesh_position():
    return lax.axis_index("x"), lax.axis_index("y"), lax.axis_index("c")


def _peer(pos, flip):
    return tuple((1 - p) if f else p for p, f in zip(pos, flip))


def _index(pos):
    return 4 * pos[0] + 2 * pos[1] + pos[2]


_ANY = pl.BlockSpec(memory_space=pl.ANY)


def _moved_shape(kind, x):
    return jax.ShapeDtypeStruct(((N_DEV,) + x.shape) if kind == "gather" else x.shape, x.dtype)


def _moves(kind, x_ref, out_ref, send_sems, recv_sems, local_sem):
    me = _mesh_position()
    source = (lambda pos: x_ref) if kind == "gather" else (lambda pos: x_ref.at[_index(pos)])
    local = pltpu.make_async_copy(source(me), out_ref.at[_index(me)], local_sem)
    outgoing, incoming = [], []
    for k, flip in enumerate(_FLIPS):
        peer = _peer(me, flip)
        copy = lambda slot: pltpu.make_async_remote_copy(
            src_ref=source(peer), dst_ref=out_ref.at[_index(slot)], send_sem=send_sems.at[k], recv_sem=recv_sems.at[k],
            device_id=peer, device_id_type=pl.DeviceIdType.MESH)
        outgoing.append(copy(me))
        incoming.append(copy(peer))
    return local, outgoing, incoming


def _start(moves):
    local, outgoing, _ = moves
    local.start()
    for cp in outgoing:
        cp.start()


def _finish(moves):
    local, outgoing, incoming = moves
    for cp in incoming:
        cp.wait_recv()
    for cp in outgoing:
        cp.wait_send()
    local.wait()


def _collective(kind, x, name):
    def body(x_ref, out_ref, send_sems, recv_sems, local_sem):
        moves = _moves(kind, x_ref, out_ref, send_sems, recv_sems, local_sem)
        _start(moves)
        _finish(moves)

    return pl.pallas_call(
        body, name=name, in_specs=[_ANY], out_specs=_ANY, out_shape=_moved_shape(kind, x),
        scratch_shapes=[pltpu.SemaphoreType.DMA((N_DEV - 1,)), pltpu.SemaphoreType.DMA((N_DEV - 1,)), pltpu.SemaphoreType.DMA],
        compiler_params=pltpu.CompilerParams(has_side_effects=True),
    )(x)


def _adamw(parts, w, m, v, name):
    n_parts = parts.shape[0]
    layers, rows, cols = w.shape
    tr = _tile(rows, 256, 8)

    def body(p_ref, w_ref, m_ref, v_ref, g_ref, d_ref, mo_ref, vo_ref):
        g = p_ref[0].astype(F32)
        for s in range(1, n_parts):
            g = g + p_ref[s].astype(F32)
        m_new = ADAM_B1 * m_ref[...] + (1.0 - ADAM_B1) * g
        v_new = ADAM_B2 * v_ref[...] + (1.0 - ADAM_B2) * (g * g)
        m_hat = m_new / (1.0 - ADAM_B1 ** ADAM_STEP)
        v_hat = v_new / (1.0 - ADAM_B2 ** ADAM_STEP)
        g_ref[...] = g
        d_ref[...] = -ADAM_LR * (m_hat / (jnp.sqrt(v_hat) + ADAM_EPS) + ADAM_WD * w_ref[...])
        mo_ref[...] = m_new
        vo_ref[...] = v_new

    blk = pl.BlockSpec((None, tr, cols), lambda l, i: (l, i, 0))
    shape = jax.ShapeDtypeStruct(w.shape, F32)
    return pl.pallas_call(
        body, name=name, grid=(layers, rows // tr),
        in_specs=[pl.BlockSpec((n_parts, None, tr, cols), lambda l, i: (0, l, i, 0)), blk, blk, blk],
        out_specs=[blk, blk, blk, blk], out_shape=[shape, shape, shape, shape],
        compiler_params=pltpu.CompilerParams(dimension_semantics=("parallel", "parallel"), vmem_limit_bytes=VMEM_LIMIT),
    )(parts, w, m, v)


def _sum_parts(parts, name):
    _, rows, cols = parts.shape
    tr = _tile(rows, 256, 8)

    def body(p_ref, o_ref):
        total = p_ref[0]
        for s in range(1, N_DEV):
            total = total + p_ref[s]
        o_ref[...] = total

    return pl.pallas_call(
        body, name=name, grid=(rows // tr,),
        in_specs=[pl.BlockSpec((N_DEV, tr, cols), lambda i: (0, i, 0))], out_specs=pl.BlockSpec((tr, cols), lambda i: (i, 0)),
        out_shape=jax.ShapeDtypeStruct((rows, cols), parts.dtype),
        compiler_params=pltpu.CompilerParams(dimension_semantics=("parallel",), vmem_limit_bytes=VMEM_LIMIT),
    )(parts)


def _row_tile(rows):
    return _tile(rows, 512, 16)


def _row_spec(rows, cols, block=0):
    return pl.BlockSpec((_row_tile(rows), cols), lambda i: (i, block))


def _rows_params(accumulates):
    return pltpu.CompilerParams(dimension_semantics=("arbitrary" if accumulates else "parallel",),
                                vmem_limit_bytes=VMEM_LIMIT)


def _swiglu_fwd(gu, name):
    rows = gu.shape[0]

    def body(g_ref, u_ref, o_ref):
        o_ref[...] = (_silu(g_ref[...].astype(F32)) * u_ref[...].astype(F32)).astype(o_ref.dtype)

    return pl.pallas_call(
        body, name=name, grid=(rows // _row_tile(rows),),
        in_specs=[_row_spec(rows, FFN_HIDDEN, 0), _row_spec(rows, FFN_HIDDEN, 1)], out_specs=_row_spec(rows, FFN_HIDDEN),
        out_shape=jax.ShapeDtypeStruct((rows, FFN_HIDDEN), BF16), compiler_params=_rows_params(False))(gu, gu)


def _swiglu_bwd(gu, d_act, name):
    rows = gu.shape[0]

    def body(g_ref, u_ref, d_ref, o_ref):
        g, u, d = g_ref[...].astype(F32), u_ref[...].astype(F32), d_ref[...].astype(F32)
        sig = 1.0 / (1.0 + jnp.exp(-g))
        o_ref[:, :FFN_HIDDEN] = (d * u * sig * (1.0 + g * (1.0 - sig))).astype(o_ref.dtype)
        o_ref[:, FFN_HIDDEN:] = (d * g * sig).astype(o_ref.dtype)

    return pl.pallas_call(
        body, name=name, grid=(rows // _row_tile(rows),),
        in_specs=[_row_spec(rows, FFN_HIDDEN, 0), _row_spec(rows, FFN_HIDDEN, 1), _row_spec(rows, FFN_HIDDEN)],
        out_specs=_row_spec(rows, 2 * FFN_HIDDEN),
        out_shape=jax.ShapeDtypeStruct((rows, 2 * FFN_HIDDEN), BF16), compiler_params=_rows_params(False))(gu, gu, d_act)


def _add_norm_fwd(h, y, gain, name):
    rows = h.shape[0]

    def body(*refs):
        if y is None:
            h_ref, g_ref, n_ref = refs
            x = h_ref[...]
        else:
            h_ref, y_ref, g_ref, s_ref, n_ref = refs
            x = h_ref[...] + y_ref[...]
            s_ref[...] = x
        n_ref[...] = (x * lax.rsqrt(jnp.mean(x * x, axis=-1, keepdims=True) + EPS) * g_ref[...]).astype(n_ref.dtype)

    row = _row_spec(rows, D_MODEL)
    ins = [h] if y is None else [h, y]
    out_shape = [jax.ShapeDtypeStruct((rows, D_MODEL), BF16)]
    if y is not None:
        out_shape = [jax.ShapeDtypeStruct((rows, D_MODEL), F32)] + out_shape
    res = pl.pallas_call(
        body, name=name, grid=(rows // _row_tile(rows),),
        in_specs=[row] * len(ins) + [_full_spec((1, D_MODEL))], out_specs=[row] * len(out_shape), out_shape=out_shape,
        compiler_params=_rows_params(False))(*ins, gain[None])
    return (h, res[0]) if y is None else (res[0], res[1])


def _norm_bwd(x, gain, d_n, d_skip, name):
    rows = x.shape[0]

    def body(x_ref, g_ref, dn_ref, ds_ref, dx_ref, dg_ref):
        @pl.when(pl.program_id(0) == 0)
        def _():
            dg_ref[...] = jnp.zeros_like(dg_ref)

        x, dn = x_ref[...], dn_ref[...].astype(F32)
        r = lax.rsqrt(jnp.mean(x * x, axis=-1, keepdims=True) + EPS)
        gd = g_ref[...] * dn
        dx_ref[...] = r * gd - x * (r * r * r) * jnp.mean(x * gd, axis=-1, keepdims=True) + ds_ref[...]
        dg_ref[...] += jnp.sum(x * r * dn, axis=0, keepdims=True)

    row = _row_spec(rows, D_MODEL)
    dx, dg = pl.pallas_call(
        body, name=name, grid=(rows // _row_tile(rows),),
        in_specs=[row, _full_spec((1, D_MODEL)), row, row], out_specs=[row, _full_spec((1, D_MODEL))],
        out_shape=[jax.ShapeDtypeStruct((rows, D_MODEL), F32), jax.ShapeDtypeStruct((1, D_MODEL), F32)],
        compiler_params=_rows_params(True))(x, gain[None], d_n, d_skip)
    return dx, dg[0]


def _loss_head(h, gain, target, name):
    rows = h.shape[0]

    def body(x_ref, g_ref, t_ref, loss_ref, dx_ref, dg_ref):
        @pl.when(pl.program_id(0) == 0)
        def _():
            loss_ref[...] = jnp.zeros_like(loss_ref)
            dg_ref[...] = jnp.zeros_like(dg_ref)

        x = x_ref[...]
        r = lax.rsqrt(jnp.mean(x * x, axis=-1, keepdims=True) + EPS)
        err = x * r * g_ref[...] - t_ref[...]
        loss_ref[...] += 0.5 * jnp.sum(jnp.mean(err * err, axis=-1, keepdims=True), axis=0, keepdims=True)
        dy = err * (1.0 / D_MODEL)
        gd = g_ref[...] * dy
        dx_ref[...] = r * gd - x * (r * r * r) * jnp.mean(x * gd, axis=-1, keepdims=True)
        dg_ref[...] += jnp.sum(x * r * dy, axis=0, keepdims=True)

    row = _row_spec(rows, D_MODEL)
    loss, dx, dg = pl.pallas_call(
        body, name=name, grid=(rows // _row_tile(rows),),
        in_specs=[row, _full_spec((1, D_MODEL)), row], out_specs=[_full_spec((1, 1)), row, _full_spec((1, D_MODEL))],
        out_shape=[jax.ShapeDtypeStruct((1, 1), F32), jax.ShapeDtypeStruct((rows, D_MODEL), F32),
                   jax.ShapeDtypeStruct((1, D_MODEL), F32)],
        compiler_params=_rows_params(True))(h, gain[None], target)
    return loss[0, 0], dx, dg[0]


def _join_cols(blocks, n_out, name):
    _, layers, rows, n = blocks.shape
    tr = _tile(rows, 256, 16)

    def body(x_ref, o_ref):
        for d in range(N_DEV):
            o_ref[:, d * n:(d + 1) * n] = x_ref[d]
        if n_out > N_DEV * n:
            o_ref[:, N_DEV * n:] = jnp.zeros((tr, n_out - N_DEV * n), o_ref.dtype)

    return pl.pallas_call(
        body, name=name, grid=(layers, rows // tr),
        in_specs=[pl.BlockSpec((N_DEV, None, tr, n), lambda l, i: (0, l, i, 0))],
        out_specs=pl.BlockSpec((None, tr, n_out), lambda l, i: (l, i, 0)),
        out_shape=jax.ShapeDtypeStruct((layers, rows, n_out), blocks.dtype),
        compiler_params=pltpu.CompilerParams(dimension_semantics=("parallel", "parallel"), vmem_limit_bytes=VMEM_LIMIT),
    )(blocks)


def _split_cols(full, n, name):
    rows = full.shape[0]
    tr = _tile(rows, 256, 16)

    def body(x_ref, o_ref):
        for d in range(N_DEV):
            o_ref[d] = x_ref[:, d * n:(d + 1) * n]

    return pl.pallas_call(
        body, name=name, grid=(rows // tr,),
        in_specs=[pl.BlockSpec((tr, full.shape[1]), lambda i: (i, 0))],
        out_specs=pl.BlockSpec((N_DEV, tr, n), lambda i: (0, i, 0)),
        out_shape=jax.ShapeDtypeStruct((N_DEV, rows, n), full.dtype),
        compiler_params=pltpu.CompilerParams(dimension_semantics=("parallel",), vmem_limit_bytes=VMEM_LIMIT),
    )(full)


def _pack(arrays):
    flat = jnp.concatenate([a.reshape(-1) for a in arrays])
    unit = FLAT_COLS * FLAT_ROWS_ALIGN
    padded = -(-flat.shape[0] // unit) * unit
    return jnp.pad(flat, (0, padded - flat.shape[0])).reshape(-1, FLAT_COLS)


def _unpack(flat, shapes, lead=()):
    flat = flat.reshape(lead + (-1,))
    out, off = [], 0
    for shape in shapes:
        n = math.prod(shape)
        out.append(flat[..., off:off + n].reshape(lead + tuple(shape)))
        off += n
    return out


def _join(blocks, axis):
    moved = jnp.moveaxis(blocks, 0, axis)
    shape = list(moved.shape)
    shape[axis:axis + 2] = [shape[axis] * shape[axis + 1]]
    return moved.reshape(shape)


def _own_shard(full, axis, position):
    n = full.shape[axis] // N_DEV
    return lax.dynamic_slice_in_dim(full, position * n, n, axis)


def _s5_operators(log_dt, a_re, a_im, b_re, b_im, c_re, c_im):
    t = S5_CHUNK
    hi = lax.Precision.HIGHEST
    step = jnp.exp(log_dt)[:, None]
    mag = jnp.exp(step * a_re)
    abar_re = mag * jnp.cos(step * a_im)
    abar_im = mag * jnp.sin(step * a_im)
    den = a_re * a_re + a_im * a_im
    f_re = ((abar_re - 1.0) * a_re + abar_im * a_im) / den
    f_im = (abar_im * a_re - (abar_re - 1.0) * a_im) / den
    bb_re = f_re[..., None] * b_re - f_im[..., None] * b_im
    bb_im = f_re[..., None] * b_im + f_im[..., None] * b_re
    j = jnp.arange(t + 1, dtype=F32)[:, None, None]
    pmag = jnp.exp(j * (step * a_re))
    pw_re = pmag * jnp.cos(j * (step * a_im))
    pw_im = pmag * jnp.sin(j * (step * a_im))
    cl_re = c_re[None] * pw_re[:t, :, None, :] - c_im[None] * pw_im[:t, :, None, :]
    cl_im = c_re[None] * pw_im[:t, :, None, :] + c_im[None] * pw_re[:t, :, None, :]
    kern = (jnp.einsum('jgcp,gpk->jgck', cl_re, bb_re, precision=hi)
            - jnp.einsum('jgcp,gpk->jgck', cl_im, bb_im, precision=hi))
    rp_re, rp_im = pw_re[:t][::-1], pw_im[:t][::-1]
    wz_re = rp_re[:, :, :, None] * bb_re[None] - rp_im[:, :, :, None] * bb_im[None]
    wz_im = rp_re[:, :, :, None] * bb_im[None] + rp_im[:, :, :, None] * bb_re[None]
    w_z = jnp.concatenate([wz_re, wz_im], axis=2).transpose(1, 0, 3, 2).reshape(S5_GROUPS, t * S5_GROUP, 2 * S5_STATE)
    cy_re = c_re[None] * pw_re[1:, :, None, :] - c_im[None] * pw_im[1:, :, None, :]
    cy_im = c_re[None] * pw_im[1:, :, None, :] + c_im[None] * pw_re[1:, :, None, :]
    w_y = jnp.concatenate([cy_re, -cy_im], axis=3).transpose(1, 3, 0, 2).reshape(S5_GROUPS, 2 * S5_STATE, t * S5_GROUP)
    return kern, w_z, w_y, pw_re[t], pw_im[t]


def _s5_lag_selector():
    t = S5_CHUNK
    lag = jnp.arange(t)[:, None] - jnp.arange(t)[None, :]
    return (lag[:, :, None] == jnp.arange(t)[None, None, :]).astype(F32).reshape(t * t, t)


def _s5_toeplitz(kern, tag):
    t = S5_CHUNK
    sel = _s5_lag_selector()
    flat = _matmul(sel, kern.reshape(t, -1), out_dtype=BF16, name=tag + "_toeplitz")
    toep = flat.reshape(t, t, S5_GROUPS, S5_GROUP, S5_GROUP).transpose(2, 1, 4, 0, 3)
    toep = toep.reshape(S5_GROUPS, t * S5_GROUP, t * S5_GROUP)

    def backward(d_toep):
        d_flat = d_toep.reshape(S5_GROUPS, t, S5_GROUP, t, S5_GROUP).transpose(3, 1, 0, 4, 2).reshape(t * t, -1)
        return _matmul(sel, d_flat, ta=True, name=tag + "_toeplitz_dw").reshape(kern.shape)

    return toep, backward


def _s5_gate(y, u, d_skip):
    return jax.nn.gelu(y + d_skip * u)


def _glu(vg):
    return vg[:, :D_MODEL] * jax.nn.sigmoid(vg[:, D_MODEL:])


_BIG = [("gla_w_in", 2), ("gla_w_out", 1), ("ssd_w_in", 2), ("ssd_w_out", 1), ("s5_w_glu", 2), ("ffn_w_gu", 2),
        ("ffn_w_down", 1)]
_PADDED_COLS = {"gla_w_in": GLA_PROJ, "ssd_w_in": SSD_PROJ}


class _Traffic:
    LINK_BYTES_PER_SECOND = 7.0e10
    MATMUL_FLOPS = 7.0e14

    def __init__(self, shards, plan):
        self.shards, self.plan = shards, plan
        self.position = 0
        self.queue = []
        self.weights, self.received = {}, {}
        self.early = None
        self.standalone = self.serial = 0
        for key in plan:
            self._request(key)

    def _request(self, key):
        shard = self.shards[key]
        seconds = (N_DEV - 1) * shard.size * shard.dtype.itemsize / self.LINK_BYTES_PER_SECOND
        self._enqueue("gather", shard, seconds, key, lambda blocks: self.weights.__setitem__(key, self._assemble(key, blocks)))

    def _enqueue(self, kind, x, seconds, key, deliver):
        self.queue.append((kind, x, seconds, key, deliver, self.serial))
        self.serial += 1

    @staticmethod
    def _assemble(key, blocks):
        name, layer = key
        if dict(_BIG)[name] == 1:
            return blocks.reshape((N_DEV * blocks.shape[1], blocks.shape[2]))
        n_out = _PADDED_COLS.get(name, N_DEV * blocks.shape[2])
        return _join_cols(blocks[:, None], n_out, f"join_{name}_{layer}")[0]

    def take(self, key):
        assert key == self.plan[self.position], (key, self.plan[self.position])
        self.position += 1
        while key not in self.weights:
            self._alone(self.queue.pop(0))
        return self.weights[key]

    def run(self, seconds, call, more_carriers_follow=False):
        riders, waiting, left = [], [], seconds
        for item in self.queue:
            if item[2] <= left:
                riders.append(item)
                left -= item[2]
            else:
                waiting.append(item)
        due = [item for item in waiting if item[3] is not None and self.position < len(self.plan)
               and item[3] == self.plan[self.position]]
        if due and not more_carriers_follow:
            left = seconds - due[0][2]
            kept = []
            for item in riders:
                if item[2] <= left:
                    kept.append(item)
                    left -= item[2]
                else:
                    waiting.append(item)
            riders = due + kept
            waiting = [item for item in waiting if item is not due[0]]
            waiting.sort(key=lambda item: item[5])
        self.queue = waiting
        if not riders:
            return call(())
        results, moved = call([(kind, x) for kind, x, *_ in riders])
        for item, y in zip(riders, moved):
            item[4](y)
        return results

    def matmul(self, a, b, more_carriers_follow=False, **kw):
        m, n = (a.shape[-1] if kw.get("ta") else a.shape[-2]), (b.shape[-2] if kw.get("tb") else b.shape[-1])
        k = a.shape[-2] if kw.get("ta") else a.shape[-1]
        return self.run(2.0 * m * n * k / self.MATMUL_FLOPS, lambda cargo: _matmul(a, b, cargo=cargo, **kw),
                        more_carriers_follow)

    def send_gradient(self, key, dw):
        name, layer = key
        shard = self.shards[key]
        if dict(_BIG)[name] == 1:
            blocks = dw.reshape((N_DEV,) + shard.shape)
        else:
            blocks = _split_cols(dw, shard.shape[1], f"split_{name}_{layer}")
        seconds = (N_DEV - 1) * shard.size * shard.dtype.itemsize / self.LINK_BYTES_PER_SECOND
        self._enqueue("exchange", blocks, seconds, None, lambda parts: self.received.__setitem__(key, parts))

    def gather_early(self, packed):
        seconds = (N_DEV - 1) * packed.size * packed.dtype.itemsize / self.LINK_BYTES_PER_SECOND
        self._enqueue("gather", packed, seconds, None, lambda parts: setattr(self, "early", parts))

    def _alone(self, item):
        kind, x, _, _, deliver, _ = item
        deliver(_collective(kind, x, f"{kind}_alone_{self.standalone}"))
        self.standalone += 1

    def flush(self):
        for item in self.queue:
            self._alone(item)
        self.queue = []


_GLA_FWD_SECONDS, _GLA_BWD_SECONDS, _SSD_FWD_SECONDS, _SSD_BWD_SECONDS = 1.25e-6, 3.4e-6, 3.5e-6, 14e-6


def _linear(x, w, tag, out_dtype=F32, dx_dtype=F32, more_carriers_follow=False):
    traffic, key = w
    weight = traffic.take(key)
    y = traffic.matmul(x, weight, more_carriers_follow, out_dtype=out_dtype, name=tag + "_fwd")

    def backward(dy):
        dx = traffic.matmul(dy, weight, tb=True, out_dtype=dx_dtype, name=tag + "_dx")
        traffic.send_gradient(key, traffic.matmul(x, dy, ta=True, out_dtype=BF16, name=tag + "_dw"))
        return dx

    return y, backward


def _gla_mixer(hn, p, tag):
    traffic, chunks = p["w_in"][0], hn.shape[0] // CHUNK
    w_a2 = jnp.pad(p["w_a2"], ((0, LANES - GLA_RANK), (0, 0)))
    b_a, norm_g = p["b_a"][None], p["norm_g"][None]
    proj, lin_in = _linear(hn, p["w_in"], tag + "_in", more_carriers_follow=True)
    o, sprev = traffic.run(chunks * _GLA_FWD_SECONDS, lambda cargo: _gla_core_fwd(proj, w_a2, b_a, norm_g, cargo))
    y, lin_out = _linear(o, p["w_out"], tag + "_out")

    def backward(dy):
        d_o = lin_out(dy)
        d_proj, d_wa, d_ba, d_ng = traffic.run(chunks * _GLA_BWD_SECONDS,
                                               lambda cargo: _gla_core_bwd(proj, sprev, d_o, w_a2, b_a, norm_g, cargo))
        return lin_in(d_proj), dict(w_a2=d_wa[:GLA_RANK], b_a=d_ba[0], norm_g=d_ng[0])

    return y, backward


def _ssd_mixer(hn, p, tag):
    pad = lambda a: jnp.pad(a[None], ((0, 0), (0, LANES - SSD_HEADS)))
    dt_bias, a_log, d_skip, norm_g = pad(p["dt_bias"]), pad(p["a_log"]), pad(p["d"]), p["norm_g"][None]
    traffic, chunks = p["w_in"][0], hn.shape[0] // CHUNK
    proj, lin_in = _linear(hn, p["w_in"], tag + "_in", more_carriers_follow=True)
    xbc = _ssd_conv_fwd(proj, p["conv_w"], p["conv_b"], tag + "_conv")
    o, hprev = traffic.run(chunks * _SSD_FWD_SECONDS,
                           lambda cargo: _ssd_core_fwd(proj, xbc, dt_bias, a_log, d_skip, norm_g, cargo))
    y, lin_out = _linear(o, p["w_out"], tag + "_out")

    def backward(dy):
        d_o = lin_out(dy)
        d_z, d_xbc, d_dt, d_db, d_al, d_ds, d_ng = traffic.run(
            chunks * _SSD_BWD_SECONDS, lambda cargo: _ssd_core_bwd(proj, xbc, hprev, d_o, dt_bias, a_log, d_skip, norm_g, cargo))
        d_pre, d_cw, d_cb = _ssd_conv_bwd(proj, d_xbc, p["conv_w"], p["conv_b"], tag + "_conv_bwd")
        d_hn = lin_in(jnp.concatenate([d_z, d_pre, d_dt], axis=1))
        return d_hn, dict(conv_w=d_cw, conv_b=d_cb, dt_bias=d_db[0, :SSD_HEADS], a_log=d_al[0, :SSD_HEADS],
                          d=d_ds[0, :SSD_HEADS], norm_g=d_ng[0])

    return y, backward


def _s5_mixer(hn, p, tag):
    seq = hn.shape[0]
    t, n_chunks = S5_CHUNK, hn.shape[0] // S5_CHUNK
    names = ("log_dt", "a_re", "a_im", "b_re", "b_im", "c_re", "c_im")
    (kern, w_z, w_y, lam_re, lam_im), ops_vjp = jax.vjp(_s5_operators, *[p[k] for k in names])
    toep, toep_bwd = _s5_toeplitz(kern, tag)
    to_groups = lambda a: a.reshape(n_chunks, t, S5_GROUPS, S5_GROUP).transpose(2, 0, 1, 3).reshape(S5_GROUPS, n_chunks, t * S5_GROUP)
    from_groups = lambda a: a.reshape(S5_GROUPS, n_chunks, t, S5_GROUP).transpose(1, 2, 0, 3).reshape(seq, D_MODEL)
    ug = to_groups(hn)
    z = _matmul(ug, w_z, name=tag + "_z")
    x_before = _s5_boundary_scan(z.transpose(1, 0, 2), lam_re, lam_im, tag + "_scan")
    xprev = x_before.transpose(1, 0, 2)
    tw = t * S5_GROUP
    ux = jnp.concatenate([ug, xprev.astype(BF16)], axis=2)
    yg = _matmul(ux, jnp.concatenate([toep, w_y.astype(BF16)], axis=1), out_dtype=BF16, name=tag + "_y")
    act, gate_vjp = jax.vjp(_s5_gate, from_groups(yg), hn, p["d"])
    vg, lin_glu = _linear(act, p["w_glu"], tag + "_glu")
    out, glu_vjp = jax.vjp(_glu, vg)

    def backward(dy):
        d_vg, = glu_vjp(dy)
        d_act = lin_glu(d_vg)
        d_y, d_hn, d_d = gate_vjp(d_act)
        d_yg = to_groups(d_y)
        d_xprev = _matmul(d_yg, w_y, tb=True, name=tag + "_inter_dx").transpose(1, 0, 2)
        d_wy = _matmul(xprev, d_yg, ta=True, name=tag + "_inter_dw")
        dz = _s5_boundary_scan(d_xprev, lam_re, -lam_im, tag + "_scan_bwd", reverse=True)
        x_re, x_im, dz_re, dz_im = (x_before[..., :S5_STATE], x_before[..., S5_STATE:], dz[..., :S5_STATE],
                                    dz[..., S5_STATE:])
        d_lam_re = jnp.sum(x_re * dz_re + x_im * dz_im, axis=0)
        d_lam_im = jnp.sum(x_re * dz_im - x_im * dz_re, axis=0)
        dyz = jnp.concatenate([d_yg, dz.transpose(1, 0, 2).astype(BF16)], axis=2)
        d_ug = _matmul(dyz, jnp.concatenate([toep, w_z.astype(BF16)], axis=2), tb=True, out_dtype=BF16, name=tag + "_du")
        d_ops = _matmul(ug, dyz, ta=True, name=tag + "_dw")
        grads = dict(zip(names, ops_vjp((toep_bwd(d_ops[..., :tw]), d_ops[..., tw:], d_wy, d_lam_re, d_lam_im))))
        grads.update(d=d_d)
        return d_hn.astype(F32) + from_groups(d_ug), grads

    return out, backward


_SMALL =[("gla_w_a2", 2), ("gla_b_a", 1), ("gla_norm_g", 1), ("ssd_conv_w", 2), ("s5_d", 1)]
_REPLICATED = ["norm_mix_g", "norm_ffn_g", "ssd_conv_b", "ssd_dt_bias", "ssd_a_log", "ssd_d", "ssd_norm_g", "s5_log_dt",
               "s5_a_re", "s5_a_im", "s5_b_re", "s5_b_im", "s5_c_re", "s5_c_im", "final_norm_g"]
_EARLY_SMALL = [n for n in [s for s, _ in _SMALL] + _REPLICATED if n.startswith("s5_")]
_WEIGHTS = ['norm_mix_g', 'norm_ffn_g', 'gla_w_in', 'gla_w_a2', 'gla_b_a', 'gla_norm_g', 'gla_w_out', 'ssd_w_in',
            'ssd_conv_w', 'ssd_conv_b', 'ssd_dt_bias', 'ssd_a_log', 'ssd_d', 'ssd_norm_g', 'ssd_w_out', 's5_log_dt',
            's5_a_re', 's5_a_im', 's5_b_re', 's5_b_im', 's5_c_re', 's5_c_im', 's5_d', 's5_w_glu', 'ffn_w_gu', 'ffn_w_down',
            'final_norm_g']


def _gather_small(local):
    shapes = [local[n].shape for n, _ in _SMALL]
    blocks = _collective("gather", _pack([local[n] for n, _ in _SMALL]), "gather_vectors")
    parts = _unpack(blocks, shapes, lead=(N_DEV,))
    return {n: _join(part, axis) for (n, axis), part in zip(_SMALL, parts)}


def _forward_plan():
    plan = []
    for i in range(DEPTH):
        j = i // 3
        plan += [[("gla_w_in", j), ("gla_w_out", j)], [("ssd_w_in", j), ("ssd_w_out", j)], [("s5_w_glu", j)]][i % 3]
        plan += [("ffn_w_gu", i), ("ffn_w_down", i)]
    return plan


def _forward_backward(x, target, w, traffic):
    big = lambda name, j: (traffic, (name, j))
    gla = lambda j: dict(w_in=big("gla_w_in", j), w_a2=w["gla_w_a2"][j], b_a=w["gla_b_a"][j], norm_g=w["gla_norm_g"][j],
                         w_out=big("gla_w_out", j))
    ssd = lambda j: dict(w_in=big("ssd_w_in", j), conv_w=w["ssd_conv_w"][j], conv_b=w["ssd_conv_b"][j],
                         dt_bias=w["ssd_dt_bias"][j], a_log=w["ssd_a_log"][j], d=w["ssd_d"][j], norm_g=w["ssd_norm_g"][j],
                         w_out=big("ssd_w_out", j))
    s5 = lambda j: dict(log_dt=w["s5_log_dt"][j], a_re=w["s5_a_re"][j], a_im=w["s5_a_im"][j], b_re=w["s5_b_re"][j],
                        b_im=w["s5_b_im"][j], c_re=w["s5_c_re"][j], c_im=w["s5_c_im"][j], d=w["s5_d"][j],
                        w_glu=big("s5_w_glu", j))
    mixers = [("gla", _gla_mixer, gla), ("ssd", _ssd_mixer, ssd), ("s5", _s5_mixer, s5)]
    base, delta = x, None
    tape = []
    for i in range(DEPTH):
        kind, mixer, params = mixers[i % 3]
        j = i // 3
        h, hn = _add_norm_fwd(base, delta, w["norm_mix_g"][i], f"l{i}_norm_mix")
        y, mixer_bwd = mixer(hn, params(j), f"l{i}_{kind}")
        h_mid, hn2 = _add_norm_fwd(h, y, w["norm_ffn_g"][i], f"l{i}_norm_ffn")
        gu, gu_bwd = _linear(hn2, big("ffn_w_gu", i), f"l{i}_ffn_gu", out_dtype=BF16)
        act = _swiglu_fwd(gu, f"l{i}_swiglu")
        delta, down_bwd = _linear(act, big("ffn_w_down", i), f"l{i}_ffn_down", dx_dtype=BF16)
        base = h_mid
        tape.append((kind, j, h, mixer_bwd, h_mid, gu_bwd, gu, down_bwd))
    loss, d_h, d_final_g = _loss_head(base + delta, w["final_norm_g"], target, "loss_head")

    grads = {n: [None] * w[n].shape[0] for n in w if n != "final_norm_g"}
    grads["final_norm_g"] = d_final_g
    for i in reversed(range(DEPTH)):
        kind, j, h, mixer_bwd, h_mid, gu_bwd, gu, down_bwd = tape[i]
        d_gu = _swiglu_bwd(gu, down_bwd(d_h), f"l{i}_swiglu_bwd")
        d_mid, grads["norm_ffn_g"][i] = _norm_bwd(h_mid, w["norm_ffn_g"][i], gu_bwd(d_gu), d_h, f"l{i}_norm_ffn_bwd")
        d_hn, mixer_grads = mixer_bwd(d_mid)
        for k, g in mixer_grads.items():
            grads[f"{kind}_{k}"][j] = g
        if kind == "s5" and all(g is not None for n in _EARLY_SMALL for g in grads[n]):
            traffic.gather_early(_pack([jnp.stack(grads[n]) for n in _EARLY_SMALL]))
        d_h, grads["norm_mix_g"][i] = _norm_bwd(h, w["norm_mix_g"][i], d_hn, d_mid, f"l{i}_norm_mix_bwd")
    return loss, d_h, grads


def kernel(x, norm_mix_g, norm_ffn_g, gla_w_in, gla_w_a2, gla_b_a, gla_norm_g, gla_w_out, ssd_w_in, ssd_conv_w, ssd_conv_b, ssd_dt_bias, ssd_a_log, ssd_d, ssd_norm_g, ssd_w_out, s5_log_dt, s5_a_re, s5_a_im, s5_b_re, s5_b_im, s5_c_re, s5_c_im, s5_d, s5_w_glu, ffn_w_gu, ffn_w_down, final_norm_g, loss_target, m_norm_mix_g, m_norm_ffn_g, m_gla_w_in, m_gla_w_a2, m_gla_b_a, m_gla_norm_g, m_gla_w_out, m_ssd_w_in, m_ssd_conv_w, m_ssd_conv_b, m_ssd_dt_bias, m_ssd_a_log, m_ssd_d, m_ssd_norm_g, m_ssd_w_out, m_s5_log_dt, m_s5_a_re, m_s5_a_im, m_s5_b_re, m_s5_b_im, m_s5_c_re, m_s5_c_im, m_s5_d, m_s5_w_glu, m_ffn_w_gu, m_ffn_w_down, m_final_norm_g, v_norm_mix_g, v_norm_ffn_g, v_gla_w_in, v_gla_w_a2, v_gla_b_a, v_gla_norm_g, v_gla_w_out, v_ssd_w_in, v_ssd_conv_w, v_ssd_conv_b, v_ssd_dt_bias, v_ssd_a_log, v_ssd_d, v_ssd_norm_g, v_ssd_w_out, v_s5_log_dt, v_s5_a_re, v_s5_a_im, v_s5_b_re, v_s5_b_im, v_s5_c_re, v_s5_c_im, v_s5_d, v_s5_w_glu, v_ffn_w_gu, v_ffn_w_down, v_final_norm_g):
    args = locals()
    local = {n: args[n] for n in _WEIGHTS}
    moment_m = {n: args["m_" + n] for n in _WEIGHTS}
    moment_v = {n: args["v_" + n] for n in _WEIGHTS}

    shards = {(n, layer): local[n][layer].astype(BF16) for n, _ in _BIG for layer in range(local[n].shape[0])}
    traffic = _Traffic(shards, _forward_plan())
    full = {n: local[n] for n in _REPLICATED}
    full.update(_gather_small(local))

    loss, d_x, grads = _forward_backward(x[0], loss_target[0], full, traffic)
    traffic.flush()
    loss = lax.psum(loss, ("x", "y", "c"))
    kinds = ("grad", "delta", "new_m", "new_v")
    out = {}

    for n, _ in _BIG:
        parts = jnp.stack([traffic.received[(n, layer)] for layer in range(local[n].shape[0])], axis=1)
        results = _adamw(parts, local[n], moment_m[n], moment_v[n], "adamw_" + n)
        out.update({f"{kind}_{n}": a for kind, a in zip(kinds, results)})

    small = [n for n, _ in _SMALL] + _REPLICATED
    stacked = lambda n: grads[n] if n == "final_norm_g" else jnp.stack(grads[n])
    late = [n for n in small if n not in _EARLY_SMALL]
    gathered = [(_EARLY_SMALL, traffic.early, "early"),
                (late, _collective("gather", _pack([stacked(n) for n in late]), "gather_small_gradients"), "late")]
    summed = {}
    for names, parts, tag in gathered:
        sums = _unpack(_sum_parts(parts, "sum_small_gradients_" + tag), [stacked(n).shape for n in names])
        summed.update(zip(names, sums))
    position = _index(_mesh_position())
    mine = [_own_shard(summed[n], axis, position) for n, axis in _SMALL] + [summed[n] for n in _REPLICATED]
    shapes = [local[n].shape for n in small]
    pk = lambda arrays: _pack(arrays)[None]
    results = _adamw(pk(mine)[None], pk([local[n] for n in small]), pk([moment_m[n] for n in small]),
                     pk([moment_v[n] for n in small]), "adamw_small")
    for kind, flat in zip(kinds, results):
        out.update({f"{kind}_{n}": a for n, a in zip(small, _unpack(flat[0], shapes))})

    return (loss, d_x[None], *[out[f"{kind}_{n}"] for kind in ("grad", "delta", "new_m", "new_v") for n in _WEIGHTS])
```

```python
import functools
import math

import jax
import jax.numpy as jnp
import numpy as np
from jax import lax
from jax.experimental import pallas as pl
from jax.experimental.pallas import tpu as pltpu

F32 = jnp.float32
BF16 = jnp.bfloat16
_MXU_DTYPE = jnp.bfloat16

N_DEV = 8
D_MODEL = 1024
DEPTH = 4
CHUNK = 64
STEP_CHUNKS = 1
STEP = CHUNK * STEP_CHUNKS
EPS = 1e-6
GLA_HEADS, GLA_DK, GLA_DV, GLA_RANK, GLA_TAU = 4, 128, 256, 16, 16.0
GLA_QK = GLA_HEADS * GLA_DK
GLA_VD = GLA_HEADS * GLA_DV
LANES = 128
GLA_IN = 2 * GLA_QK + 2 * GLA_VD + GLA_RANK
GLA_PROJ = 2 * GLA_QK + 2 * GLA_VD + LANES
SSD_DINNER, SSD_HEADDIM, SSD_HEADS, SSD_GROUPS, SSD_HPG, SSD_DSTATE, SSD_CONV = 2048, 64, 32, 8, 4, 128, 4
SSD_GN = SSD_GROUPS * SSD_DSTATE
SSD_GW = SSD_HPG * SSD_HEADDIM
SSD_XBC = SSD_DINNER + 2 * SSD_GN
SSD_IN = SSD_DINNER + SSD_XBC + SSD_HEADS
SSD_PROJ = SSD_DINNER + SSD_XBC + LANES
S5_GROUP, S5_GROUPS, S5_STATE = 16, 64, 64
S5_CHUNK = 16
FFN_HIDDEN = 2816
ADAM_LR, ADAM_B1, ADAM_B2, ADAM_EPS, ADAM_WD, ADAM_STEP = 0.001, 0.9, 0.999, 1e-08, 0.01, 10
VMEM_LIMIT = 48 * 1024 * 1024
FLAT_COLS = 1024
FLAT_ROWS_ALIGN = 64


def _tile(n, cap, unit):
    if n <= cap:
        return n
    best = None
    for t in range(unit, cap + 1, unit):
        if n % t == 0:
            best = t
    assert best is not None, (n, cap, unit)
    return best


def _divisors(n, unit):
    return sorted({t for t in range(unit, n + 1, unit) if n % t == 0} | {n})


_MXU_FLOPS, _HBM_BYTES, _ACC_BYTES, _STEP_SECONDS = 1.1e15, 3e12, 1.1e13, 3.5e-7
_MXU_ROWS = 256
_TILE_VMEM_BUDGET = 36 * 1024 * 1024
_BATCH_VMEM_BUDGET = 16 * 1024 * 1024


def _pick_tiles(m, n, k, a_bytes, b_bytes, o_bytes, m_unit):
    best = None
    for tm in _divisors(m, m_unit):
        for tn in _divisors(n, LANES):
            for tk in _divisors(k, LANES):
                nk = k // tk
                vmem = 2 * tm * tk * a_bytes + 2 * tk * tn * b_bytes + 2 * tm * tn * o_bytes + (nk > 1) * tm * tn * 4
                if vmem > _TILE_VMEM_BUDGET or tm > 2048 or tn > 2048:
                    continue
                a_reads = n // tn if nk > 1 else 1
                b_reads = 1 if (nk == 1 and n == tn) else m // tm
                traffic = m * k * a_bytes * a_reads + k * n * b_bytes * b_reads + m * n * o_bytes
                mxu = 2.0 * m * n * k / _MXU_FLOPS * (1.0 + _MXU_ROWS / tm)
                cost = (max(mxu, traffic / _HBM_BYTES) + (nk > 1) * nk * m * n * 8 / _ACC_BYTES
                        + (m // tm) * (n // tn) * nk * _STEP_SECONDS)
                if best is None or cost < best[0]:
                    best = (cost, tm, tn, tk)
    assert best is not None, (m, n, k)
    return best[1:]


def _cargo_call(body, cargo, *, name, grid, in_specs, out_specs, out_shape, scratch_shapes, semantics):
    params = lambda sem: pltpu.CompilerParams(dimension_semantics=sem, vmem_limit_bytes=VMEM_LIMIT)
    if not cargo:
        return pl.pallas_call(body, name=name, grid=grid, in_specs=in_specs, out_specs=out_specs, out_shape=out_shape,
                              scratch_shapes=scratch_shapes, compiler_params=params(semantics))
    n_in, n_out, n_scratch, n_cargo = len(in_specs), len(out_specs), len(scratch_shapes), len(cargo)

    def loaded(*refs):
        ins, cargo_in, rest = refs[:n_in], refs[n_in:n_in + n_cargo], refs[n_in + n_cargo:]
        outs, cargo_out, rest = rest[:n_out], rest[n_out:n_out + n_cargo], rest[n_out + n_cargo:]
        scratch, sems = rest[:n_scratch], rest[n_scratch:]
        ids = [pl.program_id(d) for d in range(len(grid))]
        first = functools.reduce(jnp.logical_and, [i == 0 for i in ids])
        last = functools.reduce(jnp.logical_and, [i == g - 1 for i, g in zip(ids, grid)])
        moves = lambda: [_moves(kind, x_ref, y_ref, *sems[3 * c:3 * c + 3])
                         for c, ((kind, _), x_ref, y_ref) in enumerate(zip(cargo, cargo_in, cargo_out))]

        @pl.when(first)
        def _():
            for mv in moves():
                _start(mv)

        body(*ins, *outs, *scratch)

        @pl.when(last)
        def _():
            for mv in moves():
                _finish(mv)

    sems = [pltpu.SemaphoreType.DMA((N_DEV - 1,)), pltpu.SemaphoreType.DMA((N_DEV - 1,)), pltpu.SemaphoreType.DMA] * n_cargo
    call = pl.pallas_call(
        loaded, name=name, grid=grid, in_specs=list(in_specs) + [_ANY] * n_cargo,
        out_specs=list(out_specs) + [_ANY] * n_cargo,
        out_shape=list(out_shape) + [_moved_shape(kind, x) for kind, x in cargo],
        scratch_shapes=list(scratch_shapes) + sems, compiler_params=params(("arbitrary",) * len(grid)))

    def run(*args):
        results = call(*args, *[x for _, x in cargo])
        return list(results[:n_out]), list(results[n_out:])

    return run


def _matmul(a, b, *, ta=False, tb=False, out_dtype=F32, name, cargo=()):
    batched = a.ndim == 3
    if ta:
        k_dim, m_dim = a.shape[-2:]
    else:
        m_dim, k_dim = a.shape[-2:]
    if tb:
        n_dim, kb = b.shape[-2:]
    else:
        kb, n_dim = b.shape[-2:]
    assert kb == k_dim, (a.shape, b.shape, ta, tb)
    tm, tn, tk = _pick_tiles(m_dim, n_dim, k_dim, a.dtype.itemsize, b.dtype.itemsize, jnp.dtype(out_dtype).itemsize,
                             LANES if ta else 16)
    nk = k_dim // tk
    ca, cb = (0 if ta else 1), (1 if tb else 0)
    grid = (m_dim // tm, n_dim // tn, nk)
    gb = 1
    if batched:
        step_bytes = 2 * (tm * tk * a.dtype.itemsize + tk * tn * b.dtype.itemsize + tm * tn * jnp.dtype(out_dtype).itemsize)
        gb = max(g for g in _divisors(a.shape[0], 1) if g * step_bytes <= _BATCH_VMEM_BUDGET or g == 1)
        grid = (a.shape[0] // gb,) + grid
    dims = (((ca + 1,), (cb + 1,)), ((0,), (0,))) if batched else (((ca,), (cb,)), ((), ()))

    def body(a_ref, b_ref, o_ref, *acc):
        part = lax.dot_general(a_ref[...].astype(_MXU_DTYPE), b_ref[...].astype(_MXU_DTYPE), dims,
                               preferred_element_type=F32)
        if nk == 1:
            o_ref[...] = part.astype(o_ref.dtype)
            return
        acc_ref, = acc
        k = pl.program_id(len(grid) - 1)

        @pl.when(k == 0)
        def _():
            acc_ref[...] = part

        @pl.when(k > 0)
        def _():
            acc_ref[...] += part

        @pl.when(k == nk - 1)
        def _():
            o_ref[...] = acc_ref[...].astype(o_ref.dtype)

    def spec(shape, fn):
        if batched:
            return pl.BlockSpec((gb,) + shape, lambda g, i, j, k: (g,) + fn(i, j, k))
        return pl.BlockSpec(shape, fn)

    a_spec = spec((tk, tm), lambda i, j, k: (k, i)) if ta else spec((tm, tk), lambda i, j, k: (i, k))
    b_spec = spec((tn, tk), lambda i, j, k: (j, k)) if tb else spec((tk, tn), lambda i, j, k: (k, j))
    o_spec = spec((tm, tn), lambda i, j, k: (i, j))
    out_shape = ((a.shape[0],) if batched else ()) + (m_dim, n_dim)
    call = _cargo_call(
        body, cargo, name=name, grid=grid, in_specs=[a_spec, b_spec], out_specs=[o_spec],
        out_shape=[jax.ShapeDtypeStruct(out_shape, out_dtype)],
        scratch_shapes=[pltpu.VMEM(((gb,) if batched else ()) + (tm, tn), F32)] if nk > 1 else [],
        semantics=("parallel",) * (len(grid) - 1) + ("arbitrary",))
    if not cargo:
        return call(a, b)[0]
    results, moved = call(a, b)
    return results[0], moved


def _dot(a, b, ca=1, cb=0, exact=False):
    if exact:
        return lax.dot_general(a, b, (((ca,), (cb,)), ((), ())), precision=lax.Precision.HIGHEST,
                               preferred_element_type=F32)
    return lax.dot_general(a.astype(_MXU_DTYPE), b.astype(_MXU_DTYPE), (((ca,), (cb,)), ((), ())),
                           preferred_element_type=F32)


def _tri(n):
    return lax.broadcasted_iota(jnp.int32, (n, n), 0) >= lax.broadcasted_iota(jnp.int32, (n, n), 1)


def _log_sigmoid(x):
    return jnp.minimum(x, 0.0) - jnp.log(1.0 + jnp.exp(-jnp.abs(x)))


def _softplus(x):
    return jnp.maximum(x, 0.0) + jnp.log(1.0 + jnp.exp(-jnp.abs(x)))


def _silu(x):
    return x / (1.0 + jnp.exp(-x))


def _full_spec(shape):
    return pl.BlockSpec(shape, lambda c: (0,) * len(shape))


def _gla_chunk(proj, st, w_a2, b_a, norm_g):
    t = proj.shape[0]
    q = proj[:, 0:GLA_QK] * (GLA_DK ** -0.5)
    k = proj[:, GLA_QK:2 * GLA_QK]
    v = proj[:, 2 * GLA_QK:2 * GLA_QK + GLA_VD]
    r = proj[:, 2 * GLA_QK + GLA_VD:2 * GLA_QK + 2 * GLA_VD]
    a_low = proj[:, 2 * GLA_QK + 2 * GLA_VD:]
    log_a = _log_sigmoid(_dot(a_low, w_a2) + b_a) * (1.0 / GLA_TAU)
    past = _tri(t)
    lc = _dot(past.astype(F32), log_a, exact=True)
    lend = lc[t - 1:t, :]
    e_pos = jnp.exp(lc)
    e_neg = jnp.exp(-lc)
    q_fwd, k_fwd, q_bwd, k_bwd = q * e_pos, k * e_neg, q * e_neg, k * e_pos
    kd = k * jnp.exp(lend - lc)
    g = jnp.exp(lend)
    outs, new_st = [], []
    for h in range(GLA_HEADS):
        sk = slice(h * GLA_DK, (h + 1) * GLA_DK)
        sv = slice(h * GLA_DV, (h + 1) * GLA_DV)
        s_past = _dot(q_fwd[:, sk], k_fwd[:, sk], 1, 1)
        s_future = _dot(q_bwd[:, sk], k_bwd[:, sk], 1, 1)
        scores = jnp.where(past, s_past, s_future)
        o = _dot(scores, v[:, sv]) + _dot(q_fwd[:, sk], st[h], 1, 1)
        new_st.append(st[h] * g[:, sk] + _dot(v[:, sv], kd[:, sk], 0, 0))
        o = o * lax.rsqrt(jnp.mean(o * o, axis=-1, keepdims=True) + EPS) * norm_g[:, sv]
        outs.append(o)
    return jnp.concatenate(outs, axis=1) * _silu(r), tuple(new_st)


_GLA_STATE = (GLA_HEADS, GLA_DV, GLA_DK)


def _gla_step(proj, st, w_a2, b_a, norm_g):
    outs = []
    for s in range(STEP_CHUNKS):
        out, st = _gla_chunk(proj[s * CHUNK:(s + 1) * CHUNK], st, w_a2, b_a, norm_g)
        outs.append(out)
    return jnp.concatenate(outs, axis=0), st


def _gla_core_fwd(proj, w_a2, b_a, norm_g, cargo=()):
    seq = proj.shape[0]
    nc = seq // STEP

    def body(proj_ref, wa_ref, ba_ref, ng_ref, o_ref, sprev_ref, st_ref):
        @pl.when(pl.program_id(0) == 0)
        def _():
            st_ref[...] = jnp.zeros_like(st_ref)

        st = tuple(st_ref[h] for h in range(GLA_HEADS))
        for h in range(GLA_HEADS):
            sprev_ref[0, h] = st[h]
        out, new_st = _gla_step(proj_ref[...], st, wa_ref[...], ba_ref[...], ng_ref[...])
        o_ref[...] = out
        for h in range(GLA_HEADS):
            st_ref[h] = new_st[h]

    return _cargo_call(
        body, cargo, name="gla_core_fwd", grid=(nc,),
        in_specs=[pl.BlockSpec((STEP,GLA_PROJ), lambda c: (c, 0)), _full_spec(w_a2.shape), _full_spec(b_a.shape),
                  _full_spec(norm_g.shape)],
        out_specs=[pl.BlockSpec((STEP,GLA_VD), lambda c: (c, 0)), pl.BlockSpec((1,) + _GLA_STATE, lambda c: (c, 0, 0, 0))],
        out_shape=[jax.ShapeDtypeStruct((seq, GLA_VD), F32), jax.ShapeDtypeStruct((nc,) + _GLA_STATE, F32)],
        scratch_shapes=[pltpu.VMEM(_GLA_STATE, F32)],
        semantics=("arbitrary",),
    )(proj, w_a2, b_a, norm_g)


def _gla_core_bwd(proj, sprev, d_out, w_a2, b_a, norm_g, cargo=()):
    seq = proj.shape[0]
    nc = seq // STEP

    def body(proj_ref, sprev_ref, do_ref, wa_ref, ba_ref, ng_ref, dproj_ref, dwa_ref, dba_ref, dng_ref, dst_ref):
        @pl.when(pl.program_id(0) == 0)
        def _():
            dst_ref[...] = jnp.zeros_like(dst_ref)
            dwa_ref[...] = jnp.zeros_like(dwa_ref)
            dba_ref[...] = jnp.zeros_like(dba_ref)
            dng_ref[...] = jnp.zeros_like(dng_ref)

        st = tuple(sprev_ref[0, h] for h in range(GLA_HEADS))
        _, vjp = jax.vjp(_gla_step, proj_ref[...], st, wa_ref[...], ba_ref[...], ng_ref[...])
        d_next = tuple(dst_ref[h] for h in range(GLA_HEADS))
        d_proj, d_st, d_wa, d_ba, d_ng = vjp((do_ref[...], d_next))
        dproj_ref[...] = d_proj.astype(dproj_ref.dtype)
        for h in range(GLA_HEADS):
            dst_ref[h] = d_st[h]
        dwa_ref[...] += d_wa
        dba_ref[...] += d_ba
        dng_ref[...] += d_ng

    rev = lambda c: (nc - 1 - c, 0)
    return _cargo_call(
        body, cargo, name="gla_core_bwd", grid=(nc,),
        in_specs=[pl.BlockSpec((STEP,GLA_PROJ), rev), pl.BlockSpec((1,) + _GLA_STATE, lambda c: (nc - 1 - c, 0, 0, 0)),
                  pl.BlockSpec((STEP,GLA_VD), rev), _full_spec(w_a2.shape), _full_spec(b_a.shape), _full_spec(norm_g.shape)],
        out_specs=[pl.BlockSpec((STEP,GLA_PROJ), rev), _full_spec(w_a2.shape), _full_spec(b_a.shape), _full_spec(norm_g.shape)],
        out_shape=[jax.ShapeDtypeStruct((seq, GLA_PROJ), BF16), jax.ShapeDtypeStruct(w_a2.shape, F32),
                   jax.ShapeDtypeStruct(b_a.shape, F32), jax.ShapeDtypeStruct(norm_g.shape, F32)],
        scratch_shapes=[pltpu.VMEM(_GLA_STATE, F32)],
        semantics=("arbitrary",),
    )(proj, sprev, d_out, w_a2, b_a, norm_g)


def _ssd_chunk(z, xbc, dt_raw, hs, dt_bias, a_log, d_skip, norm_g):
    t = z.shape[0]
    xs = xbc[:, :SSD_DINNER]
    bm = xbc[:, SSD_DINNER:SSD_DINNER + SSD_GN]
    cm = xbc[:, SSD_DINNER + SSD_GN:]
    dt = _softplus(dt_raw + dt_bias)
    da = dt * (-jnp.exp(a_log))
    tri = _tri(t).astype(F32)
    eye = (lax.broadcasted_iota(jnp.int32, (t, t), 0) == lax.broadcasted_iota(jnp.int32, (t, t), 1)).astype(F32)
    cum = _dot(tri, da, exact=True)
    cum_t = _dot(da, tri, 0, 1, exact=True)
    dt_t = _dot(dt, eye, 0, 0, exact=True)
    cum_end = cum[t - 1:t, :]
    w_state = dt * jnp.exp(cum_end - cum)
    e_cum = jnp.exp(cum)
    g_end = jnp.exp(cum_end)
    head_of = lambda axis: lax.shift_right_logical(lax.broadcasted_iota(jnp.int32, (SSD_GW, SSD_GW), axis),
                                                   jnp.int32(SSD_HEADDIM.bit_length() - 1))
    same_head = head_of(0) == head_of(1)
    ys, new_hs = [], []
    for g in range(SSD_GROUPS):
        heads = range(g * SSD_HPG, (g + 1) * SSD_HPG)
        cols = slice(g * SSD_GW, (g + 1) * SSD_GW)

        def spread(a):
            return jnp.concatenate([jnp.broadcast_to(a[:, h:h + 1], (a.shape[0], SSD_HEADDIM)) for h in heads], axis=1)

        def row(a_t):
            return jnp.concatenate([a_t[h:h + 1, :] for h in heads], axis=1)

        bm_g = bm[:, g * SSD_DSTATE:(g + 1) * SSD_DSTATE]
        cm_g = cm[:, g * SSD_DSTATE:(g + 1) * SSD_DSTATE]
        xs_g = xs[:, cols]
        cb = _dot(cm_g, jnp.concatenate([bm_g] * SSD_HPG, axis=0), 1, 1)
        mix = cb * jnp.exp(-jnp.abs(spread(cum) - row(cum_t))) * row(dt_t)
        x_diag = jnp.where(same_head, jnp.concatenate([xs_g] * SSD_HPG, axis=0), 0.0)
        y = _dot(mix, x_diag)
        y = y + _dot(cm_g, hs[g], 1, 1) * spread(e_cum)
        y = y + spread(d_skip) * xs_g
        states = _dot(xs_g * spread(w_state), bm_g, 0, 0)
        decayed = jnp.concatenate([g_end[:, h:h + 1] * hs[g][j * SSD_HEADDIM:(j + 1) * SSD_HEADDIM, :]
                                   for j, h in enumerate(heads)], axis=0)
        new_hs.append(decayed + states)
        yg = y * _silu(z[:, cols])
        ys.append(yg * lax.rsqrt(jnp.mean(yg * yg, axis=-1, keepdims=True) + EPS) * norm_g[:, cols])
    return jnp.concatenate(ys, axis=1), tuple(new_hs)


_SSD_STATE = (SSD_GROUPS, SSD_GW, SSD_DSTATE)


def _ssd_step(z, xbc, dt_raw, hs, dt_bias, a_log, d_skip, norm_g):
    outs = []
    for s in range(STEP_CHUNKS):
        rows = slice(s * CHUNK, (s + 1) * CHUNK)
        out, hs = _ssd_chunk(z[rows], xbc[rows], dt_raw[rows], hs, dt_bias, a_log, d_skip, norm_g)
        outs.append(out)
    return jnp.concatenate(outs, axis=0), hs
_SSD_DT_BLOCK = (SSD_DINNER + SSD_XBC) // LANES


def _ssd_core_fwd(proj, xbc, dt_bias, a_log, d_skip, norm_g, cargo=()):
    seq = proj.shape[0]
    nc = seq // STEP

    def body(z_ref, xbc_ref, dt_ref, db_ref, al_ref, ds_ref, ng_ref, o_ref, hprev_ref, hs_ref):
        @pl.when(pl.program_id(0) == 0)
        def _():
            hs_ref[...] = jnp.zeros_like(hs_ref)

        hs = tuple(hs_ref[g] for g in range(SSD_GROUPS))
        for g in range(SSD_GROUPS):
            hprev_ref[0, g] = hs[g]
        out, new_hs = _ssd_step(z_ref[...], xbc_ref[...], dt_ref[...], hs, db_ref[...], al_ref[...], ds_ref[...], ng_ref[...])
        o_ref[...] = out
        for g in range(SSD_GROUPS):
            hs_ref[g] = new_hs[g]

    return _cargo_call(
        body, cargo, name="ssd_core_fwd", grid=(nc,),
        in_specs=[pl.BlockSpec((STEP,SSD_DINNER), lambda c: (c, 0)), pl.BlockSpec((STEP,SSD_XBC), lambda c: (c, 0)),
                  pl.BlockSpec((STEP,LANES), lambda c: (c, _SSD_DT_BLOCK)),
                  _full_spec(dt_bias.shape), _full_spec(a_log.shape), _full_spec(d_skip.shape), _full_spec(norm_g.shape)],
        out_specs=[pl.BlockSpec((STEP,SSD_DINNER), lambda c: (c, 0)), pl.BlockSpec((1,) + _SSD_STATE, lambda c: (c, 0, 0, 0))],
        out_shape=[jax.ShapeDtypeStruct((seq, SSD_DINNER), F32), jax.ShapeDtypeStruct((nc,) + _SSD_STATE, F32)],
        scratch_shapes=[pltpu.VMEM(_SSD_STATE, F32)],
        semantics=("arbitrary",),
    )(proj, xbc, proj, dt_bias, a_log, d_skip, norm_g)


def _ssd_core_bwd(proj, xbc, hprev, d_out, dt_bias, a_log, d_skip, norm_g, cargo=()):
    seq = proj.shape[0]
    nc = seq // STEP

    def body(z_ref, xbc_ref, dt_ref, hprev_ref, do_ref, db_ref, al_ref, ds_ref, ng_ref,
             dz_ref, dxbc_ref, ddt_ref, ddb_ref, dal_ref, dds_ref, dng_ref, dhs_ref):
        @pl.when(pl.program_id(0) == 0)
        def _():
            dhs_ref[...] = jnp.zeros_like(dhs_ref)
            ddb_ref[...] = jnp.zeros_like(ddb_ref)
            dal_ref[...] = jnp.zeros_like(dal_ref)
            dds_ref[...] = jnp.zeros_like(dds_ref)
            dng_ref[...] = jnp.zeros_like(dng_ref)

        hs = tuple(hprev_ref[0, g] for g in range(SSD_GROUPS))
        _, vjp = jax.vjp(_ssd_step, z_ref[...], xbc_ref[...], dt_ref[...], hs, db_ref[...], al_ref[...], ds_ref[...], ng_ref[...])
        d_next = tuple(dhs_ref[g] for g in range(SSD_GROUPS))
        d_z, d_xbc, d_dt, d_hs, d_db, d_al, d_ds, d_ng = vjp((do_ref[...], d_next))
        dz_ref[...] = d_z.astype(dz_ref.dtype)
        dxbc_ref[...] = d_xbc
        ddt_ref[...] = d_dt.astype(ddt_ref.dtype)
        for g in range(SSD_GROUPS):
            dhs_ref[g] = d_hs[g]
        ddb_ref[...] += d_db
        dal_ref[...] += d_al
        dds_ref[...] += d_ds
        dng_ref[...] += d_ng

    rev = lambda c: (nc - 1 - c, 0)
    vec = [_full_spec(dt_bias.shape), _full_spec(a_log.shape), _full_spec(d_skip.shape), _full_spec(norm_g.shape)]
    return _cargo_call(
        body, cargo, name="ssd_core_bwd", grid=(nc,),
        in_specs=[pl.BlockSpec((STEP,SSD_DINNER), rev), pl.BlockSpec((STEP,SSD_XBC), rev),
                  pl.BlockSpec((STEP,LANES), lambda c: (nc - 1 - c, _SSD_DT_BLOCK)),
                  pl.BlockSpec((1,) + _SSD_STATE, lambda c: (nc - 1 - c, 0, 0, 0)),
                  pl.BlockSpec((STEP,SSD_DINNER), rev)] + vec,
        out_specs=[pl.BlockSpec((STEP,SSD_DINNER), rev), pl.BlockSpec((STEP,SSD_XBC), rev),
                   pl.BlockSpec((STEP,LANES), rev)] + vec,
        out_shape=[jax.ShapeDtypeStruct((seq, SSD_DINNER), BF16), jax.ShapeDtypeStruct((seq, SSD_XBC), F32),
                   jax.ShapeDtypeStruct((seq, LANES), BF16),
                   jax.ShapeDtypeStruct(dt_bias.shape, F32), jax.ShapeDtypeStruct(a_log.shape, F32),
                   jax.ShapeDtypeStruct(d_skip.shape, F32), jax.ShapeDtypeStruct(norm_g.shape, F32)],
        scratch_shapes=[pltpu.VMEM(_SSD_STATE, F32)],
        semantics=("arbitrary",),
    )(proj, xbc, proj, hprev, d_out, dt_bias, a_log, d_skip, norm_g)


CONV_COLS = 2048
CONV_HALO = 8


def _conv_taps(xx, rows):
    last = SSD_CONV - 1
    return [pltpu.roll(xx, last - k, 0)[CONV_HALO:CONV_HALO + rows] if k < last else xx[CONV_HALO:CONV_HALO + rows]
            for k in range(SSD_CONV)]


def _ssd_conv_fwd(proj, conv_w, conv_b, name):
    rows = proj.shape[0]
    tr = _tile(rows, 512, CONV_HALO)
    first_col = SSD_DINNER // CONV_COLS

    def body(x_ref, halo_ref, w_ref, b_ref, o_ref):
        halo = jnp.where(pl.program_id(0) == 0, 0.0, halo_ref[...])
        taps = _conv_taps(jnp.concatenate([halo, x_ref[...]], axis=0), tr)
        out = b_ref[...]
        for k in range(SSD_CONV):
            out = out + taps[k] * w_ref[k:k + 1, :]
        o_ref[...] = _silu(out)

    return pl.pallas_call(
        body, name=name, grid=(rows // tr, SSD_XBC // CONV_COLS),
        in_specs=[pl.BlockSpec((tr, CONV_COLS), lambda i, j: (i, first_col + j)),
                  pl.BlockSpec((CONV_HALO, CONV_COLS), lambda i, j: (jnp.maximum(i * (tr // CONV_HALO) - 1, 0), first_col + j)),
                  pl.BlockSpec((SSD_CONV, CONV_COLS), lambda i, j: (0, j)), pl.BlockSpec((1, CONV_COLS), lambda i, j: (0, j))],
        out_specs=pl.BlockSpec((tr, CONV_COLS), lambda i, j: (i, j)),
        out_shape=jax.ShapeDtypeStruct((rows, SSD_XBC), F32),
        compiler_params=pltpu.CompilerParams(dimension_semantics=("parallel", "parallel"), vmem_limit_bytes=VMEM_LIMIT),
    )(proj, proj, conv_w, conv_b[None])


def _ssd_conv_bwd(proj, d_xbc, conv_w, conv_b, name):
    rows = proj.shape[0]
    tr = _tile(rows, 512, CONV_HALO)
    nb, halos = rows // tr, tr // CONV_HALO
    first_col = SSD_DINNER // CONV_COLS

    def body(x_ref, before_ref, after_ref, d_ref, d_after_ref, w_ref, b_ref, dx_ref, dw_ref, db_ref):
        i = pl.program_id(1)

        @pl.when(i == 0)
        def _():
            dw_ref[...] = jnp.zeros_like(dw_ref)
            db_ref[...] = jnp.zeros_like(db_ref)

        before = jnp.where(i == 0, 0.0, before_ref[...])
        taps = _conv_taps(jnp.concatenate([before, x_ref[...], after_ref[...]], axis=0), tr + CONV_HALO)
        out = b_ref[...]
        for k in range(SSD_CONV):
            out = out + taps[k] * w_ref[k:k + 1, :]
        sig = 1.0 / (1.0 + jnp.exp(-out))
        d_after = jnp.where(i == nb - 1, 0.0, d_after_ref[...])
        d_out = jnp.concatenate([d_ref[...], d_after], axis=0) * sig * (1.0 + out * (1.0 - sig))
        d_x = d_out[:tr] * w_ref[SSD_CONV - 1:SSD_CONV, :]
        for k in range(SSD_CONV - 1):
            ahead = SSD_CONV - 1 - k
            d_x = d_x + pltpu.roll(d_out, tr + CONV_HALO - ahead, 0)[:tr] * w_ref[k:k + 1, :]
        dx_ref[...] = d_x.astype(dx_ref.dtype)
        for k in range(SSD_CONV):
            dw_ref[k:k + 1, :] += jnp.sum(d_out[:tr] * taps[k][:tr], axis=0, keepdims=True)
        db_ref[...] += jnp.sum(d_out[:tr], axis=0, keepdims=True)

    before = lambda j, i: jnp.maximum(i * halos - 1, 0)
    after = lambda j, i: jnp.minimum((i + 1) * halos, nb * halos - 1)
    d_x, d_w, d_b = pl.pallas_call(
        body, name=name, grid=(SSD_XBC // CONV_COLS, nb),
        in_specs=[pl.BlockSpec((tr, CONV_COLS), lambda j, i: (i, first_col + j)),
                  pl.BlockSpec((CONV_HALO, CONV_COLS), lambda j, i: (before(j, i), first_col + j)),
                  pl.BlockSpec((CONV_HALO, CONV_COLS), lambda j, i: (after(j, i), first_col + j)),
                  pl.BlockSpec((tr, CONV_COLS), lambda j, i: (i, j)),
                  pl.BlockSpec((CONV_HALO, CONV_COLS), lambda j, i: (after(j, i), j)),
                  pl.BlockSpec((SSD_CONV, CONV_COLS), lambda j, i: (0, j)), pl.BlockSpec((1, CONV_COLS), lambda j, i: (0, j))],
        out_specs=[pl.BlockSpec((tr, CONV_COLS), lambda j, i: (i, j)), pl.BlockSpec((SSD_CONV, CONV_COLS), lambda j, i: (0, j)),
                   pl.BlockSpec((1, CONV_COLS), lambda j, i: (0, j))],
        out_shape=[jax.ShapeDtypeStruct((rows, SSD_XBC), BF16), jax.ShapeDtypeStruct((SSD_CONV, SSD_XBC), F32),
                   jax.ShapeDtypeStruct((1, SSD_XBC), F32)],
        compiler_params=pltpu.CompilerParams(dimension_semantics=("parallel", "arbitrary"), vmem_limit_bytes=VMEM_LIMIT),
    )(proj, proj, proj, d_xbc, d_xbc, conv_w, conv_b[None])
    return d_x, d_w, d_b[0]


def _s5_boundary_scan(z, lam_re, lam_im, name, reverse=False):
    n_chunks, groups, width = z.shape
    tn = _tile(n_chunks, 128, 1)
    blocks = n_chunks // tn
    lam_a = jnp.concatenate([lam_re, lam_re], axis=1)
    lam_b = jnp.concatenate([-lam_im, lam_im], axis=1)

    def body(z_ref, a_ref, b_ref, x_ref, carry_ref):
        @pl.when(pl.program_id(0) == 0)
        def _():
            carry_ref[...] = jnp.zeros_like(carry_ref)

        a, b = a_ref[...], b_ref[...]

        def step(i, x):
            n = tn - 1 - i if reverse else i
            x_ref[n] = x
            return a * x + b * pltpu.roll(x, width // 2, 1) + z_ref[n]

        carry_ref[...] = lax.fori_loop(0, tn, step, carry_ref[...])

    block = pl.BlockSpec((tn, groups, width), (lambda i: (blocks - 1 - i, 0, 0)) if reverse else (lambda i: (i, 0, 0)))
    return pl.pallas_call(
        body, name=name, grid=(blocks,), in_specs=[block, _full_spec((groups, width)), _full_spec((groups, width))],
        out_specs=block, out_shape=jax.ShapeDtypeStruct(z.shape, F32), scratch_shapes=[pltpu.VMEM((groups, width), F32)],
        compiler_params=pltpu.CompilerParams(dimension_semantics=("arbitrary",), vmem_limit_bytes=VMEM_LIMIT),
    )(z, lam_a, lam_b)


_FLIPS = [(kx, ky, kc) for kx in (0, 1) for ky in (0, 1) for kc in (0, 1)][1:]


def _mesh_position():
    return lax.axis_index("x"), lax.axis_index("y"), lax.axis_index("c")


def _peer(pos, flip):
    return tuple((1 - p) if f else p for p, f in zip(pos, flip))


def _index(pos):
    return 4 * pos[0] + 2 * pos[1] + pos[2]


_ANY = pl.BlockSpec(memory_space=pl.ANY)


def _moved_shape(kind, x):
    return jax.ShapeDtypeStruct(((N_DEV,) + x.shape) if kind == "gather" else x.shape, x.dtype)


def _moves(kind, x_ref, out_ref, send_sems, recv_sems, local_sem):
    me = _mesh_position()
    source = (lambda pos: x_ref) if kind == "gather" else (lambda pos: x_ref.at[_index(pos)])
    local = pltpu.make_async_copy(source(me), out_ref.at[_index(me)], local_sem)
    outgoing, incoming = [], []
    for k, flip in enumerate(_FLIPS):
        peer = _peer(me, flip)
        copy = lambda slot: pltpu.make_async_remote_copy(
            src_ref=source(peer), dst_ref=out_ref.at[_index(slot)], send_sem=send_sems.at[k], recv_sem=recv_sems.at[k],
            device_id=peer, device_id_type=pl.DeviceIdType.MESH)
        outgoing.append(copy(me))
        incoming.append(copy(peer))
    return local, outgoing, incoming


def _start(moves):
    local, outgoing, _ = moves
    local.start()
    for cp in outgoing:
        cp.start()


def _finish(moves):
    local, outgoing, incoming = moves
    for cp in incoming:
        cp.wait_recv()
    for cp in outgoing:
        cp.wait_send()
    local.wait()


def _collective(kind, x, name):
    def body(x_ref, out_ref, send_sems, recv_sems, local_sem):
        moves = _moves(kind, x_ref, out_ref, send_sems, recv_sems, local_sem)
        _start(moves)
        _finish(moves)

    return pl.pallas_call(
        body, name=name, in_specs=[_ANY], out_specs=_ANY, out_shape=_moved_shape(kind, x),
        scratch_shapes=[pltpu.SemaphoreType.DMA((N_DEV - 1,)), pltpu.SemaphoreType.DMA((N_DEV - 1,)), pltpu.SemaphoreType.DMA],
        compiler_params=pltpu.CompilerParams(has_side_effects=True),
    )(x)


def _adamw(parts, w, m, v, name):
    n_parts = parts.shape[0]
    layers, rows, cols = w.shape
    tr = _tile(rows, 256, 8)

    def body(p_ref, w_ref, m_ref, v_ref, g_ref, d_ref, mo_ref, vo_ref):
        g = p_ref[0].astype(F32)
        for s in range(1, n_parts):
            g = g + p_ref[s].astype(F32)
        m_new = ADAM_B1 * m_ref[...] + (1.0 - ADAM_B1) * g
        v_new = ADAM_B2 * v_ref[...] + (1.0 - ADAM_B2) * (g * g)
        m_hat = m_new / (1.0 - ADAM_B1 ** ADAM_STEP)
        v_hat = v_new / (1.0 - ADAM_B2 ** ADAM_STEP)
        g_ref[...] = g
        d_ref[...] = -ADAM_LR * (m_hat / (jnp.sqrt(v_hat) + ADAM_EPS) + ADAM_WD * w_ref[...])
        mo_ref[...] = m_new
        vo_ref[...] = v_new

    blk = pl.BlockSpec((None, tr, cols), lambda l, i: (l, i, 0))
    shape = jax.ShapeDtypeStruct(w.shape, F32)
    return pl.pallas_call(
        body, name=name, grid=(layers, rows // tr),
        in_specs=[pl.BlockSpec((n_parts, None, tr, cols), lambda l, i: (0, l, i, 0)), blk, blk, blk],
        out_specs=[blk, blk, blk, blk], out_shape=[shape, shape, shape, shape],
        compiler_params=pltpu.CompilerParams(dimension_semantics=("parallel", "parallel"), vmem_limit_bytes=VMEM_LIMIT),
    )(parts, w, m, v)


def _sum_parts(parts, name):
    _, rows, cols = parts.shape
    tr = _tile(rows, 256, 8)

    def body(p_ref, o_ref):
        total = p_ref[0]
        for s in range(1, N_DEV):
            total = total + p_ref[s]
        o_ref[...] = total

    return pl.pallas_call(
        body, name=name, grid=(rows // tr,),
        in_specs=[pl.BlockSpec((N_DEV, tr, cols), lambda i: (0, i, 0))], out_specs=pl.BlockSpec((tr, cols), lambda i: (i, 0)),
        out_shape=jax.ShapeDtypeStruct((rows, cols), parts.dtype),
        compiler_params=pltpu.CompilerParams(dimension_semantics=("parallel",), vmem_limit_bytes=VMEM_LIMIT),
    )(parts)


def _row_tile(rows):
    return _tile(rows, 512, 16)


def _row_spec(rows, cols, block=0):
    return pl.BlockSpec((_row_tile(rows), cols), lambda i: (i, block))


def _rows_params(accumulates):
    return pltpu.CompilerParams(dimension_semantics=("arbitrary" if accumulates else "parallel",),
                                vmem_limit_bytes=VMEM_LIMIT)


def _swiglu_fwd(gu, name):
    rows = gu.shape[0]

    def body(g_ref, u_ref, o_ref):
        o_ref[...] = (_silu(g_ref[...].astype(F32)) * u_ref[...].astype(F32)).astype(o_ref.dtype)

    return pl.pallas_call(
        body, name=name, grid=(rows // _row_tile(rows),),
        in_specs=[_row_spec(rows, FFN_HIDDEN, 0), _row_spec(rows, FFN_HIDDEN, 1)], out_specs=_row_spec(rows, FFN_HIDDEN),
        out_shape=jax.ShapeDtypeStruct((rows, FFN_HIDDEN), BF16), compiler_params=_rows_params(False))(gu, gu)


def _swiglu_bwd(gu, d_act, name):
    rows = gu.shape[0]

    def body(g_ref, u_ref, d_ref, o_ref):
        g, u, d = g_ref[...].astype(F32), u_ref[...].astype(F32), d_ref[...].astype(F32)
        sig = 1.0 / (1.0 + jnp.exp(-g))
        o_ref[:, :FFN_HIDDEN] = (d * u * sig * (1.0 + g * (1.0 - sig))).astype(o_ref.dtype)
        o_ref[:, FFN_HIDDEN:] = (d * g * sig).astype(o_ref.dtype)

    return pl.pallas_call(
        body, name=name, grid=(rows // _row_tile(rows),),
        in_specs=[_row_spec(rows, FFN_HIDDEN, 0), _row_spec(rows, FFN_HIDDEN, 1), _row_spec(rows, FFN_HIDDEN)],
        out_specs=_row_spec(rows, 2 * FFN_HIDDEN),
        out_shape=jax.ShapeDtypeStruct((rows, 2 * FFN_HIDDEN), BF16), compiler_params=_rows_params(False))(gu, gu, d_act)


def _add_norm_fwd(h, y, gain, name):
    rows = h.shape[0]

    def body(*refs):
        if y is None:
            h_ref, g_ref, n_ref = refs
            x = h_ref[...]
        else:
            h_ref, y_ref, g_ref, s_ref, n_ref = refs
            x = h_ref[...] + y_ref[...]
            s_ref[...] = x
        n_ref[...] = (x * lax.rsqrt(jnp.mean(x * x, axis=-1, keepdims=True) + EPS) * g_ref[...]).astype(n_ref.dtype)

    row = _row_spec(rows, D_MODEL)
    ins = [h] if y is None else [h, y]
    out_shape = [jax.ShapeDtypeStruct((rows, D_MODEL), BF16)]
    if y is not None:
        out_shape = [jax.ShapeDtypeStruct((rows, D_MODEL), F32)] + out_shape
    res = pl.pallas_call(
        body, name=name, grid=(rows // _row_tile(rows),),
        in_specs=[row] * len(ins) + [_full_spec((1, D_MODEL))], out_specs=[row] * len(out_shape), out_shape=out_shape,
        compiler_params=_rows_params(False))(*ins, gain[None])
    return (h, res[0]) if y is None else (res[0], res[1])


def _norm_bwd(x, gain, d_n, d_skip, name):
    rows = x.shape[0]
    d_parts = d_n if isinstance(d_n, tuple) else (d_n,)

    def body(x_ref, g_ref, *refs):
        dn_refs, (ds_ref, dx_ref, dg_ref) = refs[:len(d_parts)], refs[len(d_parts):]

        @pl.when(pl.program_id(0) == 0)
        def _():
            dg_ref[...] = jnp.zeros_like(dg_ref)

        x, dn = x_ref[...], sum(r[...].astype(F32) for r in dn_refs)
        r = lax.rsqrt(jnp.mean(x * x, axis=-1, keepdims=True) + EPS)
        gd = g_ref[...] * dn
        dx_ref[...] = r * gd - x * (r * r * r) * jnp.mean(x * gd, axis=-1, keepdims=True) + ds_ref[...]
        dg_ref[...] += jnp.sum(x * r * dn, axis=0, keepdims=True)

    row = _row_spec(rows, D_MODEL)
    dx, dg = pl.pallas_call(
        body, name=name, grid=(rows // _row_tile(rows),),
        in_specs=[row, _full_spec((1, D_MODEL))] + [row] * (len(d_parts) + 1), out_specs=[row, _full_spec((1, D_MODEL))],
        out_shape=[jax.ShapeDtypeStruct((rows, D_MODEL), F32), jax.ShapeDtypeStruct((1, D_MODEL), F32)],
        compiler_params=_rows_params(True))(x, gain[None], *d_parts, d_skip)
    return dx, dg[0]


_GELU_C, _GELU_A = math.sqrt(2.0 / math.pi), 0.044715


def _s5_gate_fwd(y, u, d_skip, name):
    rows = y.shape[0]

    def body(y_ref, u_ref, d_ref, o_ref):
        x = y_ref[...].astype(F32) + d_ref[...] * u_ref[...].astype(F32)
        o_ref[...] = (0.5 * x * (1.0 + jnp.tanh(_GELU_C * (x + _GELU_A * x * x * x)))).astype(o_ref.dtype)

    row = _row_spec(rows, D_MODEL)
    return pl.pallas_call(
        body, name=name, grid=(rows // _row_tile(rows),), in_specs=[row, row, _full_spec((1, D_MODEL))], out_specs=row,
        out_shape=jax.ShapeDtypeStruct((rows, D_MODEL), BF16), compiler_params=_rows_params(False))(y, u, d_skip[None])


def _s5_gate_bwd(y, u, d_skip, d_act, name):
    rows = y.shape[0]

    def body(y_ref, u_ref, d_ref, da_ref, dy_ref, du_ref, dd_ref):
        @pl.when(pl.program_id(0) == 0)
        def _():
            dd_ref[...] = jnp.zeros_like(dd_ref)

        u = u_ref[...].astype(F32)
        x = y_ref[...].astype(F32) + d_ref[...] * u
        t = jnp.tanh(_GELU_C * (x + _GELU_A * x * x * x))
        slope = 0.5 * (1.0 + t) + 0.5 * x * (1.0 - t * t) * _GELU_C * (1.0 + 3.0 * _GELU_A * x * x)
        dx = da_ref[...].astype(F32) * slope
        dy_ref[...] = dx.astype(dy_ref.dtype)
        du_ref[...] = (dx * d_ref[...]).astype(du_ref.dtype)
        dd_ref[...] += jnp.sum(dx * u, axis=0, keepdims=True)

    row = _row_spec(rows, D_MODEL)
    shape = jax.ShapeDtypeStruct((rows, D_MODEL), BF16)
    d_y, d_u, d_d = pl.pallas_call(
        body, name=name, grid=(rows // _row_tile(rows),), in_specs=[row, row, _full_spec((1, D_MODEL)), row],
        out_specs=[row, row, _full_spec((1, D_MODEL))], out_shape=[shape, shape, jax.ShapeDtypeStruct((1, D_MODEL), F32)],
        compiler_params=_rows_params(True))(y, u, d_skip[None], d_act)
    return d_y, d_u, d_d[0]


def _glu_fwd(vg, name):
    rows = vg.shape[0]

    def body(v_ref, g_ref, o_ref):
        o_ref[...] = v_ref[...] / (1.0 + jnp.exp(-g_ref[...]))

    return pl.pallas_call(
        body, name=name, grid=(rows // _row_tile(rows),),
        in_specs=[_row_spec(rows, D_MODEL, 0), _row_spec(rows, D_MODEL, 1)], out_specs=_row_spec(rows, D_MODEL),
        out_shape=jax.ShapeDtypeStruct((rows, D_MODEL), F32), compiler_params=_rows_params(False))(vg, vg)


def _glu_bwd(vg, d_out, name):
    rows = vg.shape[0]

    def body(v_ref, g_ref, d_ref, o_ref):
        sig = 1.0 / (1.0 + jnp.exp(-g_ref[...]))
        d = d_ref[...]
        o_ref[:, :D_MODEL] = (d * sig).astype(o_ref.dtype)
        o_ref[:, D_MODEL:] = (d * v_ref[...] * sig * (1.0 - sig)).astype(o_ref.dtype)

    return pl.pallas_call(
        body, name=name, grid=(rows // _row_tile(rows),),
        in_specs=[_row_spec(rows, D_MODEL, 0), _row_spec(rows, D_MODEL, 1), _row_spec(rows, D_MODEL)],
        out_specs=_row_spec(rows, 2 * D_MODEL), out_shape=jax.ShapeDtypeStruct((rows, 2 * D_MODEL), BF16),
        compiler_params=_rows_params(False))(vg, vg, d_out)


def _loss_head(h, gain, target, name):
    rows = h.shape[0]

    def body(x_ref, g_ref, t_ref, loss_ref, dx_ref, dg_ref):
        @pl.when(pl.program_id(0) == 0)
        def _():
            loss_ref[...] = jnp.zeros_like(loss_ref)
            dg_ref[...] = jnp.zeros_like(dg_ref)

        x = x_ref[...]
        r = lax.rsqrt(jnp.mean(x * x, axis=-1, keepdims=True) + EPS)
        err = x * r * g_ref[...] - t_ref[...]
        loss_ref[...] += 0.5 * jnp.sum(jnp.mean(err * err, axis=-1, keepdims=True), axis=0, keepdims=True)
        dy = err * (1.0 / D_MODEL)
        gd = g_ref[...] * dy
        dx_ref[...] = r * gd - x * (r * r * r) * jnp.mean(x * gd, axis=-1, keepdims=True)
        dg_ref[...] += jnp.sum(x * r * dy, axis=0, keepdims=True)

    row = _row_spec(rows, D_MODEL)
    loss, dx, dg = pl.pallas_call(
        body, name=name, grid=(rows // _row_tile(rows),),
        in_specs=[row, _full_spec((1, D_MODEL)), row], out_specs=[_full_spec((1, 1)), row, _full_spec((1, D_MODEL))],
        out_shape=[jax.ShapeDtypeStruct((1, 1), F32), jax.ShapeDtypeStruct((rows, D_MODEL), F32),
                   jax.ShapeDtypeStruct((1, D_MODEL), F32)],
        compiler_params=_rows_params(True))(h, gain[None], target)
    return loss[0, 0], dx, dg[0]


def _join_cols(blocks, n_out, name):
    _, layers, rows, n = blocks.shape
    tr = _tile(rows, 256, 16)

    def body(x_ref, o_ref):
        for d in range(N_DEV):
            o_ref[:, d * n:(d + 1) * n] = x_ref[d]
        if n_out > N_DEV * n:
            o_ref[:, N_DEV * n:] = jnp.zeros((tr, n_out - N_DEV * n), o_ref.dtype)

    return pl.pallas_call(
        body, name=name, grid=(layers, rows // tr),
        in_specs=[pl.BlockSpec((N_DEV, None, tr, n), lambda l, i: (0, l, i, 0))],
        out_specs=pl.BlockSpec((None, tr, n_out), lambda l, i: (l, i, 0)),
        out_shape=jax.ShapeDtypeStruct((layers, rows, n_out), blocks.dtype),
        compiler_params=pltpu.CompilerParams(dimension_semantics=("parallel", "parallel"), vmem_limit_bytes=VMEM_LIMIT),
    )(blocks)


def _split_cols(full, n, name):
    rows = full.shape[0]
    tr = _tile(rows, 256, 16)

    def body(x_ref, o_ref):
        for d in range(N_DEV):
            o_ref[d] = x_ref[:, d * n:(d + 1) * n]

    return pl.pallas_call(
        body, name=name, grid=(rows // tr,),
        in_specs=[pl.BlockSpec((tr, full.shape[1]), lambda i: (i, 0))],
        out_specs=pl.BlockSpec((N_DEV, tr, n), lambda i: (0, i, 0)),
        out_shape=jax.ShapeDtypeStruct((N_DEV, rows, n), full.dtype),
        compiler_params=pltpu.CompilerParams(dimension_semantics=("parallel",), vmem_limit_bytes=VMEM_LIMIT),
    )(full)


def _pack(arrays):
    flat = jnp.concatenate([a.reshape(-1) for a in arrays])
    unit = FLAT_COLS * FLAT_ROWS_ALIGN
    padded = -(-flat.shape[0] // unit) * unit
    return jnp.pad(flat, (0, padded - flat.shape[0])).reshape(-1, FLAT_COLS)


def _unpack(flat, shapes, lead=()):
    flat = flat.reshape(lead + (-1,))
    out, off = [], 0
    for shape in shapes:
        n = math.prod(shape)
        out.append(flat[..., off:off + n].reshape(lead + tuple(shape)))
        off += n
    return out


def _join(blocks, axis):
    moved = jnp.moveaxis(blocks, 0, axis)
    shape = list(moved.shape)
    shape[axis:axis + 2] = [shape[axis] * shape[axis + 1]]
    return moved.reshape(shape)


def _own_shard(full, axis, position):
    n = full.shape[axis] // N_DEV
    return lax.dynamic_slice_in_dim(full, position * n, n, axis)


def _s5_operators(log_dt, a_re, a_im, b_re, b_im, c_re, c_im):
    t = S5_CHUNK
    hi = lax.Precision.HIGHEST
    step = jnp.exp(log_dt)[:, None]
    mag = jnp.exp(step * a_re)
    abar_re = mag * jnp.cos(step * a_im)
    abar_im = mag * jnp.sin(step * a_im)
    den = a_re * a_re + a_im * a_im
    f_re = ((abar_re - 1.0) * a_re + abar_im * a_im) / den
    f_im = (abar_im * a_re - (abar_re - 1.0) * a_im) / den
    bb_re = f_re[..., None] * b_re - f_im[..., None] * b_im
    bb_im = f_re[..., None] * b_im + f_im[..., None] * b_re
    j = jnp.arange(t + 1, dtype=F32)[:, None, None]
    pmag = jnp.exp(j * (step * a_re))
    pw_re = pmag * jnp.cos(j * (step * a_im))
    pw_im = pmag * jnp.sin(j * (step * a_im))
    cl_re = c_re[None] * pw_re[:t, :, None, :] - c_im[None] * pw_im[:t, :, None, :]
    cl_im = c_re[None] * pw_im[:t, :, None, :] + c_im[None] * pw_re[:t, :, None, :]
    kern = (jnp.einsum('jgcp,gpk->jgck', cl_re, bb_re, precision=hi)
            - jnp.einsum('jgcp,gpk->jgck', cl_im, bb_im, precision=hi))
    rp_re, rp_im = pw_re[:t][::-1], pw_im[:t][::-1]
    wz_re = rp_re[:, :, :, None] * bb_re[None] - rp_im[:, :, :, None] * bb_im[None]
    wz_im = rp_re[:, :, :, None] * bb_im[None] + rp_im[:, :, :, None] * bb_re[None]
    w_z = jnp.concatenate([wz_re, wz_im], axis=2).transpose(1, 0, 3, 2).reshape(S5_GROUPS, t * S5_GROUP, 2 * S5_STATE)
    cy_re = c_re[None] * pw_re[1:, :, None, :] - c_im[None] * pw_im[1:, :, None, :]
    cy_im = c_re[None] * pw_im[1:, :, None, :] + c_im[None] * pw_re[1:, :, None, :]
    w_y = jnp.concatenate([cy_re, -cy_im], axis=3).transpose(1, 3, 0, 2).reshape(S5_GROUPS, 2 * S5_STATE, t * S5_GROUP)
    return kern, w_z, w_y, pw_re[t], pw_im[t]


def _s5_lag_selector():
    t = S5_CHUNK
    lag = jnp.arange(t)[:, None] - jnp.arange(t)[None, :]
    return (lag[:, :, None] == jnp.arange(t)[None, None, :]).astype(F32).reshape(t * t, t)


def _s5_toeplitz(kern, tag):
    t = S5_CHUNK
    sel = _s5_lag_selector()
    flat = _matmul(sel, kern.reshape(t, -1), out_dtype=BF16, name=tag + "_toeplitz")
    toep = flat.reshape(t, t, S5_GROUPS, S5_GROUP, S5_GROUP).transpose(2, 1, 4, 0, 3)
    toep = toep.reshape(S5_GROUPS, t * S5_GROUP, t * S5_GROUP)

    def backward(d_toep):
        d_flat = d_toep.reshape(S5_GROUPS, t, S5_GROUP, t, S5_GROUP).transpose(3, 1, 0, 4, 2).reshape(t * t, -1)
        return _matmul(sel, d_flat, ta=True, name=tag + "_toeplitz_dw").reshape(kern.shape)

    return toep, backward


_BIG = [("gla_w_in", 2), ("gla_w_out", 1), ("ssd_w_in", 2), ("ssd_w_out", 1), ("s5_w_glu", 2), ("ffn_w_gu", 2),
        ("ffn_w_down", 1)]
_PADDED_COLS = {"gla_w_in": GLA_PROJ, "ssd_w_in": SSD_PROJ}


class _Traffic:
    LINK_BYTES_PER_SECOND = 7.0e10
    MATMUL_FLOPS = 7.0e14

    def __init__(self, shards, plan):
        self.shards, self.plan = shards, plan
        self.position = 0
        self.queue = []
        self.weights, self.received = {}, {}
        self.early = None
        self.standalone = self.serial = 0
        for key in plan:
            self._request(key)

    def _request(self, key):
        shard = self.shards[key]
        seconds = (N_DEV - 1) * shard.size * shard.dtype.itemsize / self.LINK_BYTES_PER_SECOND
        self._enqueue("gather", shard, seconds, key, lambda blocks: self.weights.__setitem__(key, self._assemble(key, blocks)))

    def _enqueue(self, kind, x, seconds, key, deliver):
        self.queue.append((kind, x, seconds, key, deliver, self.serial))
        self.serial += 1

    @staticmethod
    def _assemble(key, blocks):
        name, layer = key
        if dict(_BIG)[name] == 1:
            return blocks.reshape((N_DEV * blocks.shape[1], blocks.shape[2]))
        n_out = _PADDED_COLS.get(name, N_DEV * blocks.shape[2])
        return _join_cols(blocks[:, None], n_out, f"join_{name}_{layer}")[0]

    def take(self, key):
        assert key == self.plan[self.position], (key, self.plan[self.position])
        self.position += 1
        while key not in self.weights:
            self._alone(self.queue.pop(0))
        return self.weights[key]

    def run(self, seconds, call, more_carriers_follow=False):
        riders, waiting, left = [], [], seconds
        for item in self.queue:
            if item[2] <= left:
                riders.append(item)
                left -= item[2]
            else:
                waiting.append(item)
        due = [item for item in waiting if item[3] is not None and self.position < len(self.plan)
               and item[3] == self.plan[self.position]]
        if due and not more_carriers_follow:
            left = seconds - due[0][2]
            kept = []
            for item in riders:
                if item[2] <= left:
                    kept.append(item)
                    left -= item[2]
                else:
                    waiting.append(item)
            riders = due + kept
            waiting = [item for item in waiting if item is not due[0]]
            waiting.sort(key=lambda item: item[5])
        self.queue = waiting
        if not riders:
            return call(())
        results, moved = call([(kind, x) for kind, x, *_ in riders])
        for item, y in zip(riders, moved):
            item[4](y)
        return results

    def matmul(self, a, b, more_carriers_follow=False, **kw):
        m, n = (a.shape[-1] if kw.get("ta") else a.shape[-2]), (b.shape[-2] if kw.get("tb") else b.shape[-1])
        k = a.shape[-2] if kw.get("ta") else a.shape[-1]
        return self.run(2.0 * m * n * k / self.MATMUL_FLOPS, lambda cargo: _matmul(a, b, cargo=cargo, **kw),
                        more_carriers_follow)

    def send_gradient(self, key, dw):
        name, layer = key
        shard = self.shards[key]
        if dict(_BIG)[name] == 1:
            blocks = dw.reshape((N_DEV,) + shard.shape)
        else:
            blocks = _split_cols(dw, shard.shape[1], f"split_{name}_{layer}")
        seconds = (N_DEV - 1) * shard.size * shard.dtype.itemsize / self.LINK_BYTES_PER_SECOND
        self._enqueue("exchange", blocks, seconds, None, lambda parts: self.received.__setitem__(key, parts))

    def gather_early(self, packed):
        seconds = (N_DEV - 1) * packed.size * packed.dtype.itemsize / self.LINK_BYTES_PER_SECOND
        self._enqueue("gather", packed, seconds, None, lambda parts: setattr(self, "early", parts))

    def _alone(self, item):
        kind, x, _, _, deliver, _ = item
        deliver(_collective(kind, x, f"{kind}_alone_{self.standalone}"))
        self.standalone += 1

    def flush(self):
        for item in self.queue:
            self._alone(item)
        self.queue = []


_GLA_FWD_SECONDS, _GLA_BWD_SECONDS, _SSD_FWD_SECONDS, _SSD_BWD_SECONDS = 1.25e-6, 3.4e-6, 3.5e-6, 14e-6


def _linear(x, w, tag, out_dtype=F32, dx_dtype=F32, more_carriers_follow=False):
    traffic, key = w
    weight = traffic.take(key)
    y = traffic.matmul(x, weight, more_carriers_follow, out_dtype=out_dtype, name=tag + "_fwd")

    def backward(dy):
        dx = traffic.matmul(dy, weight, tb=True, out_dtype=dx_dtype, name=tag + "_dx")
        traffic.send_gradient(key, traffic.matmul(x, dy, ta=True, out_dtype=BF16, name=tag + "_dw"))
        return dx

    return y, backward


def _gla_mixer(hn, p, tag):
    traffic, chunks = p["w_in"][0], hn.shape[0] // CHUNK
    w_a2 = jnp.pad(p["w_a2"], ((0, LANES - GLA_RANK), (0, 0)))
    b_a, norm_g = p["b_a"][None], p["norm_g"][None]
    proj, lin_in = _linear(hn, p["w_in"], tag + "_in", more_carriers_follow=True)
    o, sprev = traffic.run(chunks * _GLA_FWD_SECONDS, lambda cargo: _gla_core_fwd(proj, w_a2, b_a, norm_g, cargo))
    y, lin_out = _linear(o, p["w_out"], tag + "_out")

    def backward(dy):
        d_o = lin_out(dy)
        d_proj, d_wa, d_ba, d_ng = traffic.run(chunks * _GLA_BWD_SECONDS,
                                               lambda cargo: _gla_core_bwd(proj, sprev, d_o, w_a2, b_a, norm_g, cargo))
        return lin_in(d_proj), dict(w_a2=d_wa[:GLA_RANK], b_a=d_ba[0], norm_g=d_ng[0])

    return y, backward


def _ssd_mixer(hn, p, tag):
    pad = lambda a: jnp.pad(a[None], ((0, 0), (0, LANES - SSD_HEADS)))
    dt_bias, a_log, d_skip, norm_g = pad(p["dt_bias"]), pad(p["a_log"]), pad(p["d"]), p["norm_g"][None]
    traffic, chunks = p["w_in"][0], hn.shape[0] // CHUNK
    proj, lin_in = _linear(hn, p["w_in"], tag + "_in", more_carriers_follow=True)
    xbc = _ssd_conv_fwd(proj, p["conv_w"], p["conv_b"], tag + "_conv")
    o, hprev = traffic.run(chunks * _SSD_FWD_SECONDS,
                           lambda cargo: _ssd_core_fwd(proj, xbc, dt_bias, a_log, d_skip, norm_g, cargo))
    y, lin_out = _linear(o, p["w_out"], tag + "_out")

    def backward(dy):
        d_o = lin_out(dy)
        d_z, d_xbc, d_dt, d_db, d_al, d_ds, d_ng = traffic.run(
            chunks * _SSD_BWD_SECONDS, lambda cargo: _ssd_core_bwd(proj, xbc, hprev, d_o, dt_bias, a_log, d_skip, norm_g, cargo))
        d_pre, d_cw, d_cb = _ssd_conv_bwd(proj, d_xbc, p["conv_w"], p["conv_b"], tag + "_conv_bwd")
        d_hn = lin_in(jnp.concatenate([d_z, d_pre, d_dt], axis=1))
        return d_hn, dict(conv_w=d_cw, conv_b=d_cb, dt_bias=d_db[0, :SSD_HEADS], a_log=d_al[0, :SSD_HEADS],
                          d=d_ds[0, :SSD_HEADS], norm_g=d_ng[0])

    return y, backward


def _s5_mixer(hn, p, tag):
    seq = hn.shape[0]
    t, n_chunks = S5_CHUNK, hn.shape[0] // S5_CHUNK
    names = ("log_dt", "a_re", "a_im", "b_re", "b_im", "c_re", "c_im")
    (kern, w_z, w_y, lam_re, lam_im), ops_vjp = jax.vjp(_s5_operators, *[p[k] for k in names])
    toep, toep_bwd = _s5_toeplitz(kern, tag)
    to_groups = lambda a: a.reshape(n_chunks, t, S5_GROUPS, S5_GROUP).transpose(2, 0, 1, 3).reshape(S5_GROUPS, n_chunks, t * S5_GROUP)
    from_groups = lambda a: a.reshape(S5_GROUPS, n_chunks, t, S5_GROUP).transpose(1, 2, 0, 3).reshape(seq, D_MODEL)
    ug = to_groups(hn)
    z = _matmul(ug, w_z, name=tag + "_z")
    x_before = _s5_boundary_scan(z.transpose(1, 0, 2), lam_re, lam_im, tag + "_scan")
    xprev = x_before.transpose(1, 0, 2)
    tw = t * S5_GROUP
    ux = jnp.concatenate([ug, xprev.astype(BF16)], axis=2)
    yg = _matmul(ux, jnp.concatenate([toep, w_y.astype(BF16)], axis=1), out_dtype=BF16, name=tag + "_y")
    y = from_groups(yg)
    vg, lin_glu = _linear(_s5_gate_fwd(y, hn, p["d"], tag + "_gate"), p["w_glu"], tag + "_glu", dx_dtype=BF16)
    out = _glu_fwd(vg, tag + "_glu_gate")

    def backward(dy):
        d_act = lin_glu(_glu_bwd(vg, dy, tag + "_glu_gate_bwd"))
        d_y, d_u, d_d = _s5_gate_bwd(y, hn, p["d"], d_act, tag + "_gate_bwd")
        d_yg = to_groups(d_y)
        d_xprev = _matmul(d_yg, w_y, tb=True, name=tag + "_inter_dx").transpose(1, 0, 2)
        d_wy = _matmul(xprev, d_yg, ta=True, name=tag + "_inter_dw")
        dz = _s5_boundary_scan(d_xprev, lam_re, -lam_im, tag + "_scan_bwd", reverse=True)
        x_re, x_im, dz_re, dz_im = (x_before[..., :S5_STATE], x_before[..., S5_STATE:], dz[..., :S5_STATE],
                                    dz[..., S5_STATE:])
        d_lam_re = jnp.sum(x_re * dz_re + x_im * dz_im, axis=0)
        d_lam_im = jnp.sum(x_re * dz_im - x_im * dz_re, axis=0)
        dyz = jnp.concatenate([d_yg, dz.transpose(1, 0, 2).astype(BF16)], axis=2)
        d_ug = _matmul(dyz, jnp.concatenate([toep, w_z.astype(BF16)], axis=2), tb=True, out_dtype=BF16, name=tag + "_du")
        d_ops = _matmul(ug, dyz, ta=True, name=tag + "_dw")
        grads = dict(zip(names, ops_vjp((toep_bwd(d_ops[..., :tw]), d_ops[..., tw:], d_wy, d_lam_re, d_lam_im))))
        grads.update(d=d_d)
        return (d_u, from_groups(d_ug)), grads

    return out, backward


_SMALL =[("gla_w_a2", 2), ("gla_b_a", 1), ("gla_norm_g", 1), ("ssd_conv_w", 2), ("s5_d", 1)]
_REPLICATED = ["norm_mix_g", "norm_ffn_g", "ssd_conv_b", "ssd_dt_bias", "ssd_a_log", "ssd_d", "ssd_norm_g", "s5_log_dt",
               "s5_a_re", "s5_a_im", "s5_b_re", "s5_b_im", "s5_c_re", "s5_c_im", "final_norm_g"]
_EARLY_SMALL = [n for n in [s for s, _ in _SMALL] + _REPLICATED if n.startswith("s5_")]
_WEIGHTS = ['norm_mix_g', 'norm_ffn_g', 'gla_w_in', 'gla_w_a2', 'gla_b_a', 'gla_norm_g', 'gla_w_out', 'ssd_w_in',
            'ssd_conv_w', 'ssd_conv_b', 'ssd_dt_bias', 'ssd_a_log', 'ssd_d', 'ssd_norm_g', 'ssd_w_out', 's5_log_dt',
            's5_a_re', 's5_a_im', 's5_b_re', 's5_b_im', 's5_c_re', 's5_c_im', 's5_d', 's5_w_glu', 'ffn_w_gu', 'ffn_w_down',
            'final_norm_g']


def _gather_small(local):
    shapes = [local[n].shape for n, _ in _SMALL]
    blocks = _collective("gather", _pack([local[n] for n, _ in _SMALL]), "gather_vectors")
    parts = _unpack(blocks, shapes, lead=(N_DEV,))
    return {n: _join(part, axis) for (n, axis), part in zip(_SMALL, parts)}


def _forward_plan():
    plan = []
    for i in range(DEPTH):
        j = i // 3
        plan += [[("gla_w_in", j), ("gla_w_out", j)], [("ssd_w_in", j), ("ssd_w_out", j)], [("s5_w_glu", j)]][i % 3]
        plan += [("ffn_w_gu", i), ("ffn_w_down", i)]
    return plan


def _forward_backward(x, target, w, traffic):
    big = lambda name, j: (traffic, (name, j))
    gla = lambda j: dict(w_in=big("gla_w_in", j), w_a2=w["gla_w_a2"][j], b_a=w["gla_b_a"][j], norm_g=w["gla_norm_g"][j],
                         w_out=big("gla_w_out", j))
    ssd = lambda j: dict(w_in=big("ssd_w_in", j), conv_w=w["ssd_conv_w"][j], conv_b=w["ssd_conv_b"][j],
                         dt_bias=w["ssd_dt_bias"][j], a_log=w["ssd_a_log"][j], d=w["ssd_d"][j], norm_g=w["ssd_norm_g"][j],
                         w_out=big("ssd_w_out", j))
    s5 = lambda j: dict(log_dt=w["s5_log_dt"][j], a_re=w["s5_a_re"][j], a_im=w["s5_a_im"][j], b_re=w["s5_b_re"][j],
                        b_im=w["s5_b_im"][j], c_re=w["s5_c_re"][j], c_im=w["s5_c_im"][j], d=w["s5_d"][j],
                        w_glu=big("s5_w_glu", j))
    mixers = [("gla", _gla_mixer, gla), ("ssd", _ssd_mixer, ssd), ("s5", _s5_mixer, s5)]
    base, delta = x, None
    tape = []
    for i in range(DEPTH):
        kind, mixer, params = mixers[i % 3]
        j = i // 3
        h, hn = _add_norm_fwd(base, delta, w["norm_mix_g"][i], f"l{i}_norm_mix")
        y, mixer_bwd = mixer(hn, params(j), f"l{i}_{kind}")
        h_mid, hn2 = _add_norm_fwd(h, y, w["norm_ffn_g"][i], f"l{i}_norm_ffn")
        gu, gu_bwd = _linear(hn2, big("ffn_w_gu", i), f"l{i}_ffn_gu", out_dtype=BF16)
        act = _swiglu_fwd(gu, f"l{i}_swiglu")
        delta, down_bwd = _linear(act, big("ffn_w_down", i), f"l{i}_ffn_down", dx_dtype=BF16)
        base = h_mid
        tape.append((kind, j, h, mixer_bwd, h_mid, gu_bwd, gu, down_bwd))
    loss, d_h, d_final_g = _loss_head(base + delta, w["final_norm_g"], target, "loss_head")

    grads = {n: [None] * w[n].shape[0] for n in w if n != "final_norm_g"}
    grads["final_norm_g"] = d_final_g
    for i in reversed(range(DEPTH)):
        kind, j, h, mixer_bwd, h_mid, gu_bwd, gu, down_bwd = tape[i]
        d_gu = _swiglu_bwd(gu, down_bwd(d_h), f"l{i}_swiglu_bwd")
        d_mid, grads["norm_ffn_g"][i] = _norm_bwd(h_mid, w["norm_ffn_g"][i], gu_bwd(d_gu), d_h, f"l{i}_norm_ffn_bwd")
        d_hn, mixer_grads = mixer_bwd(d_mid)
        for k, g in mixer_grads.items():
            grads[f"{kind}_{k}"][j] = g
        if kind == "s5" and all(g is not None for n in _EARLY_SMALL for g in grads[n]):
            traffic.gather_early(_pack([jnp.stack(grads[n]) for n in _EARLY_SMALL]))
        d_h, grads["norm_mix_g"][i] = _norm_bwd(h, w["norm_mix_g"][i], d_hn, d_mid, f"l{i}_norm_mix_bwd")
    return loss, d_h, grads


def kernel(x, norm_mix_g, norm_ffn_g, gla_w_in, gla_w_a2, gla_b_a, gla_norm_g, gla_w_out, ssd_w_in, ssd_conv_w, ssd_conv_b, ssd_dt_bias, ssd_a_log, ssd_d, ssd_norm_g, ssd_w_out, s5_log_dt, s5_a_re, s5_a_im, s5_b_re, s5_b_im, s5_c_re, s5_c_im, s5_d, s5_w_glu, ffn_w_gu, ffn_w_down, final_norm_g, loss_target, m_norm_mix_g, m_norm_ffn_g, m_gla_w_in, m_gla_w_a2, m_gla_b_a, m_gla_norm_g, m_gla_w_out, m_ssd_w_in, m_ssd_conv_w, m_ssd_conv_b, m_ssd_dt_bias, m_ssd_a_log, m_ssd_d, m_ssd_norm_g, m_ssd_w_out, m_s5_log_dt, m_s5_a_re, m_s5_a_im, m_s5_b_re, m_s5_b_im, m_s5_c_re, m_s5_c_im, m_s5_d, m_s5_w_glu, m_ffn_w_gu, m_ffn_w_down, m_final_norm_g, v_norm_mix_g, v_norm_ffn_g, v_gla_w_in, v_gla_w_a2, v_gla_b_a, v_gla_norm_g, v_gla_w_out, v_ssd_w_in, v_ssd_conv_w, v_ssd_conv_b, v_ssd_dt_bias, v_ssd_a_log, v_ssd_d, v_ssd_norm_g, v_ssd_w_out, v_s5_log_dt, v_s5_a_re, v_s5_a_im, v_s5_b_re, v_s5_b_im, v_s5_c_re, v_s5_c_im, v_s5_d, v_s5_w_glu, v_ffn_w_gu, v_ffn_w_down, v_final_norm_g):
    args = locals()
    local = {n: args[n] for n in _WEIGHTS}
    moment_m = {n: args["m_" + n] for n in _WEIGHTS}
    moment_v = {n: args["v_" + n] for n in _WEIGHTS}

    shards = {(n, layer): local[n][layer].astype(BF16) for n, _ in _BIG for layer in range(local[n].shape[0])}
    traffic = _Traffic(shards, _forward_plan())
    full = {n: local[n] for n in _REPLICATED}
    full.update(_gather_small(local))

    loss, d_x, grads = _forward_backward(x[0], loss_target[0], full, traffic)
    traffic.flush()
    loss = lax.psum(loss, ("x", "y", "c"))
    kinds = ("grad", "delta", "new_m", "new_v")
    out = {}

    for n, _ in _BIG:
        parts = jnp.stack([traffic.received[(n, layer)] for layer in range(local[n].shape[0])], axis=1)
        results = _adamw(parts, local[n], moment_m[n], moment_v[n], "adamw_" + n)
        out.update({f"{kind}_{n}": a for kind, a in zip(kinds, results)})

    small = [n for n, _ in _SMALL] + _REPLICATED
    stacked = lambda n: grads[n] if n == "final_norm_g" else jnp.stack(grads[n])
    late = [n for n in small if n not in _EARLY_SMALL]
    gathered = [(_EARLY_SMALL, traffic.early, "early"),
                (late, _collective("gather", _pack([stacked(n) for n in late]), "gather_small_gradients"), "late")]
    summed = {}
    for names, parts, tag in gathered:
        sums = _unpack(_sum_parts(parts, "sum_small_gradients_" + tag), [stacked(n).shape for n in names])
        summed.update(zip(names, sums))
    position = _index(_mesh_position())
    mine = [_own_shard(summed[n], axis, position) for n, axis in _SMALL] + [summed[n] for n in _REPLICATED]
    shapes = [local[n].shape for n in small]
    pk = lambda arrays: _pack(arrays)[None]
    results = _adamw(pk(mine)[None], pk([local[n] for n in small]), pk([moment_m[n] for n in small]),
                     pk([moment_v[n] for n in small]), "adamw_small")
    for kind, flat in zip(kinds, results):
        out.update({f"{kind}_{n}": a for n, a in zip(small, _unpack(flat[0], shapes))})

    return (loss, d_x[None], *[out[f"{kind}_{n}"] for kind in ("grad", "delta", "new_m", "new_v") for n in _WEIGHTS])
```

```python
import functools
import math

import jax
import jax.numpy as jnp
import numpy as np
from jax import lax
from jax.experimental import pallas as pl
from jax.experimental.pallas import tpu as pltpu

F32 = jnp.float32
BF16 = jnp.bfloat16
_MXU_DTYPE = jnp.bfloat16

N_DEV = 8
D_MODEL = 1024
DEPTH = 4
CHUNK = 64
STEP_CHUNKS = 2
STEP = CHUNK * STEP_CHUNKS
EPS = 1e-6
GLA_HEADS, GLA_DK, GLA_DV, GLA_RANK, GLA_TAU = 4, 128, 256, 16, 16.0
GLA_QK = GLA_HEADS * GLA_DK
GLA_VD = GLA_HEADS * GLA_DV
LANES = 128
GLA_IN = 2 * GLA_QK + 2 * GLA_VD + GLA_RANK
GLA_PROJ = 2 * GLA_QK + 2 * GLA_VD + LANES
SSD_DINNER, SSD_HEADDIM, SSD_HEADS, SSD_GROUPS, SSD_HPG, SSD_DSTATE, SSD_CONV = 2048, 64, 32, 8, 4, 128, 4
SSD_GN = SSD_GROUPS * SSD_DSTATE
SSD_GW = SSD_HPG * SSD_HEADDIM
SSD_XBC = SSD_DINNER + 2 * SSD_GN
SSD_IN = SSD_DINNER + SSD_XBC + SSD_HEADS
SSD_PROJ = SSD_DINNER + SSD_XBC + LANES
S5_GROUP, S5_GROUPS, S5_STATE = 16, 64, 64
S5_CHUNK = 16
FFN_HIDDEN = 2816
ADAM_LR, ADAM_B1, ADAM_B2, ADAM_EPS, ADAM_WD, ADAM_STEP = 0.001, 0.9, 0.999, 1e-08, 0.01, 10
VMEM_LIMIT = 48 * 1024 * 1024
FLAT_COLS = 1024
FLAT_ROWS_ALIGN = 64


def _tile(n, cap, unit):
    if n <= cap:
        return n
    best = None
    for t in range(unit, cap + 1, unit):
        if n % t == 0:
            best = t
    assert best is not None, (n, cap, unit)
    return best


def _divisors(n, unit):
    return sorted({t for t in range(unit, n + 1, unit) if n % t == 0} | {n})


_MXU_FLOPS, _HBM_BYTES, _ACC_BYTES, _STEP_SECONDS = 1.1e15, 3e12, 1.1e13, 3.5e-7
_MXU_ROWS = 256
_TILE_VMEM_BUDGET = 36 * 1024 * 1024
_BATCH_VMEM_BUDGET = 16 * 1024 * 1024


def _pick_tiles(m, n, k, a_bytes, b_bytes, o_bytes, m_unit):
    best = None
    for tm in _divisors(m, m_unit):
        for tn in _divisors(n, LANES):
            for tk in _divisors(k, LANES):
                nk = k // tk
                vmem = 2 * tm * tk * a_bytes + 2 * tk * tn * b_bytes + 2 * tm * tn * o_bytes + (nk > 1) * tm * tn * 4
                if vmem > _TILE_VMEM_BUDGET or tm > 2048 or tn > 2048:
                    continue
                a_reads = n // tn if nk > 1 else 1
                b_reads = 1 if (nk == 1 and n == tn) else m // tm
                traffic = m * k * a_bytes * a_reads + k * n * b_bytes * b_reads + m * n * o_bytes
                mxu = 2.0 * m * n * k / _MXU_FLOPS * (1.0 + _MXU_ROWS / tm)
                cost = (max(mxu, traffic / _HBM_BYTES) + (nk > 1) * nk * m * n * 8 / _ACC_BYTES
                        + (m // tm) * (n // tn) * nk * _STEP_SECONDS)
                if best is None or cost < best[0]:
                    best = (cost, tm, tn, tk)
    assert best is not None, (m, n, k)
    return best[1:]


def _cargo_call(body, cargo, *, name, grid, in_specs, out_specs, out_shape, scratch_shapes, semantics):
    params = lambda sem: pltpu.CompilerParams(dimension_semantics=sem, vmem_limit_bytes=VMEM_LIMIT)
    if not cargo:
        return pl.pallas_call(body, name=name, grid=grid, in_specs=in_specs, out_specs=out_specs, out_shape=out_shape,
                              scratch_shapes=scratch_shapes, compiler_params=params(semantics))
    n_in, n_out, n_scratch, n_cargo = len(in_specs), len(out_specs), len(scratch_shapes), len(cargo)

    def loaded(*refs):
        ins, cargo_in, rest = refs[:n_in], refs[n_in:n_in + n_cargo], refs[n_in + n_cargo:]
        outs, cargo_out, rest = rest[:n_out], rest[n_out:n_out + n_cargo], rest[n_out + n_cargo:]
        scratch, sems = rest[:n_scratch], rest[n_scratch:]
        ids = [pl.program_id(d) for d in range(len(grid))]
        first = functools.reduce(jnp.logical_and, [i == 0 for i in ids])
        last = functools.reduce(jnp.logical_and, [i == g - 1 for i, g in zip(ids, grid)])
        moves = lambda: [_moves(kind, x_ref, y_ref, *sems[3 * c:3 * c + 3])
                         for c, ((kind, _), x_ref, y_ref) in enumerate(zip(cargo, cargo_in, cargo_out))]

        @pl.when(first)
        def _():
            for mv in moves():
                _start(mv)

        body(*ins, *outs, *scratch)

        @pl.when(last)
        def _():
            for mv in moves():
                _finish(mv)

    sems = [pltpu.SemaphoreType.DMA((N_DEV - 1,)), pltpu.SemaphoreType.DMA((N_DEV - 1,)), pltpu.SemaphoreType.DMA] * n_cargo
    call = pl.pallas_call(
        loaded, name=name, grid=grid, in_specs=list(in_specs) + [_ANY] * n_cargo,
        out_specs=list(out_specs) + [_ANY] * n_cargo,
        out_shape=list(out_shape) + [_moved_shape(kind, x) for kind, x in cargo],
        scratch_shapes=list(scratch_shapes) + sems, compiler_params=params(("arbitrary",) * len(grid)))

    def run(*args):
        results = call(*args, *[x for _, x in cargo])
        return list(results[:n_out]), list(results[n_out:])

    return run


def _matmul(a, b, *, ta=False, tb=False, out_dtype=F32, name, cargo=()):
    batched = a.ndim == 3
    if ta:
        k_dim, m_dim = a.shape[-2:]
    else:
        m_dim, k_dim = a.shape[-2:]
    if tb:
        n_dim, kb = b.shape[-2:]
    else:
        kb, n_dim = b.shape[-2:]
    assert kb == k_dim, (a.shape, b.shape, ta, tb)
    tm, tn, tk = _pick_tiles(m_dim, n_dim, k_dim, a.dtype.itemsize, b.dtype.itemsize, jnp.dtype(out_dtype).itemsize,
                             LANES if ta else 16)
    nk = k_dim // tk
    ca, cb = (0 if ta else 1), (1 if tb else 0)
    grid = (m_dim // tm, n_dim // tn, nk)
    gb = 1
    if batched:
        step_bytes = 2 * (tm * tk * a.dtype.itemsize + tk * tn * b.dtype.itemsize + tm * tn * jnp.dtype(out_dtype).itemsize)
        gb = max(g for g in _divisors(a.shape[0], 1) if g * step_bytes <= _BATCH_VMEM_BUDGET or g == 1)
        grid = (a.shape[0] // gb,) + grid
    dims = (((ca + 1,), (cb + 1,)), ((0,), (0,))) if batched else (((ca,), (cb,)), ((), ()))

    def body(a_ref, b_ref, o_ref, *acc):
        part = lax.dot_general(a_ref[...].astype(_MXU_DTYPE), b_ref[...].astype(_MXU_DTYPE), dims,
                               preferred_element_type=F32)
        if nk == 1:
            o_ref[...] = part.astype(o_ref.dtype)
            return
        acc_ref, = acc
        k = pl.program_id(len(grid) - 1)

        @pl.when(k == 0)
        def _():
            acc_ref[...] = part

        @pl.when(k > 0)
        def _():
            acc_ref[...] += part

        @pl.when(k == nk - 1)
        def _():
            o_ref[...] = acc_ref[...].astype(o_ref.dtype)

    def spec(shape, fn):
        if batched:
            return pl.BlockSpec((gb,) + shape, lambda g, i, j, k: (g,) + fn(i, j, k))
        return pl.BlockSpec(shape, fn)

    a_spec = spec((tk, tm), lambda i, j, k: (k, i)) if ta else spec((tm, tk), lambda i, j, k: (i, k))
    b_spec = spec((tn, tk), lambda i, j, k: (j, k)) if tb else spec((tk, tn), lambda i, j, k: (k, j))
    o_spec = spec((tm, tn), lambda i, j, k: (i, j))
    out_shape = ((a.shape[0],) if batched else ()) + (m_dim, n_dim)
    call = _cargo_call(
        body, cargo, name=name, grid=grid, in_specs=[a_spec, b_spec], out_specs=[o_spec],
        out_shape=[jax.ShapeDtypeStruct(out_shape, out_dtype)],
        scratch_shapes=[pltpu.VMEM(((gb,) if batched else ()) + (tm, tn), F32)] if nk > 1 else [],
        semantics=("parallel",) * (len(grid) - 1) + ("arbitrary",))
    if not cargo:
        return call(a, b)[0]
    results, moved = call(a, b)
    return results[0], moved


def _dot(a, b, ca=1, cb=0, exact=False):
    if exact:
        return lax.dot_general(a, b, (((ca,), (cb,)), ((), ())), precision=lax.Precision.HIGHEST,
                               preferred_element_type=F32)
    return lax.dot_general(a.astype(_MXU_DTYPE), b.astype(_MXU_DTYPE), (((ca,), (cb,)), ((), ())),
                           preferred_element_type=F32)


def _tri(n):
    return lax.broadcasted_iota(jnp.int32, (n, n), 0) >= lax.broadcasted_iota(jnp.int32, (n, n), 1)


def _log_sigmoid(x):
    return jnp.minimum(x, 0.0) - jnp.log(1.0 + jnp.exp(-jnp.abs(x)))


def _softplus(x):
    return jnp.maximum(x, 0.0) + jnp.log(1.0 + jnp.exp(-jnp.abs(x)))


def _silu(x):
    return x / (1.0 + jnp.exp(-x))


def _full_spec(shape):
    return pl.BlockSpec(shape, lambda c: (0,) * len(shape))


def _gla_chunk(proj, st, w_a2, b_a, norm_g):
    t = proj.shape[0]
    q = proj[:, 0:GLA_QK] * (GLA_DK ** -0.5)
    k = proj[:, GLA_QK:2 * GLA_QK]
    v = proj[:, 2 * GLA_QK:2 * GLA_QK + GLA_VD]
    r = proj[:, 2 * GLA_QK + GLA_VD:2 * GLA_QK + 2 * GLA_VD]
    a_low = proj[:, 2 * GLA_QK + 2 * GLA_VD:]
    log_a = _log_sigmoid(_dot(a_low, w_a2) + b_a) * (1.0 / GLA_TAU)
    past = _tri(t)
    lc = _dot(past.astype(F32), log_a, exact=True)
    lend = lc[t - 1:t, :]
    e_pos = jnp.exp(lc)
    e_neg = jnp.exp(-lc)
    q_fwd, k_fwd, q_bwd, k_bwd = q * e_pos, k * e_neg, q * e_neg, k * e_pos
    kd = k * jnp.exp(lend - lc)
    g = jnp.exp(lend)
    outs, new_st = [], []
    for h in range(GLA_HEADS):
        sk = slice(h * GLA_DK, (h + 1) * GLA_DK)
        sv = slice(h * GLA_DV, (h + 1) * GLA_DV)
        s_past = _dot(q_fwd[:, sk], k_fwd[:, sk], 1, 1)
        s_future = _dot(q_bwd[:, sk], k_bwd[:, sk], 1, 1)
        scores = jnp.where(past, s_past, s_future)
        o = _dot(scores, v[:, sv]) + _dot(q_fwd[:, sk], st[h], 1, 1)
        new_st.append(st[h] * g[:, sk] + _dot(v[:, sv], kd[:, sk], 0, 0))
        o = o * lax.rsqrt(jnp.mean(o * o, axis=-1, keepdims=True) + EPS) * norm_g[:, sv]
        outs.append(o)
    return jnp.concatenate(outs, axis=1) * _silu(r), tuple(new_st)


_GLA_STATE = (GLA_HEADS, GLA_DV, GLA_DK)


def _gla_step(proj, st, w_a2, b_a, norm_g):
    outs = []
    for s in range(STEP_CHUNKS):
        out, st = _gla_chunk(proj[s * CHUNK:(s + 1) * CHUNK], st, w_a2, b_a, norm_g)
        outs.append(out)
    return jnp.concatenate(outs, axis=0), st


def _gla_core_fwd(proj, w_a2, b_a, norm_g, cargo=()):
    seq = proj.shape[0]
    nc = seq // STEP

    def body(proj_ref, wa_ref, ba_ref, ng_ref, o_ref, sprev_ref, st_ref):
        @pl.when(pl.program_id(0) == 0)
        def _():
            st_ref[...] = jnp.zeros_like(st_ref)

        st = tuple(st_ref[h] for h in range(GLA_HEADS))
        for h in range(GLA_HEADS):
            sprev_ref[0, h] = st[h]
        out, new_st = _gla_step(proj_ref[...], st, wa_ref[...], ba_ref[...], ng_ref[...])
        o_ref[...] = out
        for h in range(GLA_HEADS):
            st_ref[h] = new_st[h]

    return _cargo_call(
        body, cargo, name="gla_core_fwd", grid=(nc,),
        in_specs=[pl.BlockSpec((STEP,GLA_PROJ), lambda c: (c, 0)), _full_spec(w_a2.shape), _full_spec(b_a.shape),
                  _full_spec(norm_g.shape)],
        out_specs=[pl.BlockSpec((STEP,GLA_VD), lambda c: (c, 0)), pl.BlockSpec((1,) + _GLA_STATE, lambda c: (c, 0, 0, 0))],
        out_shape=[jax.ShapeDtypeStruct((seq, GLA_VD), F32), jax.ShapeDtypeStruct((nc,) + _GLA_STATE, F32)],
        scratch_shapes=[pltpu.VMEM(_GLA_STATE, F32)],
        semantics=("arbitrary",),
    )(proj, w_a2, b_a, norm_g)


def _gla_core_bwd(proj, sprev, d_out, w_a2, b_a, norm_g, cargo=()):
    seq = proj.shape[0]
    nc = seq // STEP

    def body(proj_ref, sprev_ref, do_ref, wa_ref, ba_ref, ng_ref, dproj_ref, dwa_ref, dba_ref, dng_ref, dst_ref):
        @pl.when(pl.program_id(0) == 0)
        def _():
            dst_ref[...] = jnp.zeros_like(dst_ref)
            dwa_ref[...] = jnp.zeros_like(dwa_ref)
            dba_ref[...] = jnp.zeros_like(dba_ref)
            dng_ref[...] = jnp.zeros_like(dng_ref)

        st = tuple(sprev_ref[0, h] for h in range(GLA_HEADS))
        _, vjp = jax.vjp(_gla_step, proj_ref[...], st, wa_ref[...], ba_ref[...], ng_ref[...])
        d_next = tuple(dst_ref[h] for h in range(GLA_HEADS))
        d_proj, d_st, d_wa, d_ba, d_ng = vjp((do_ref[...], d_next))
        dproj_ref[...] = d_proj.astype(dproj_ref.dtype)
        for h in range(GLA_HEADS):
            dst_ref[h] = d_st[h]
        dwa_ref[...] += d_wa
        dba_ref[...] += d_ba
        dng_ref[...] += d_ng

    rev = lambda c: (nc - 1 - c, 0)
    return _cargo_call(
        body, cargo, name="gla_core_bwd", grid=(nc,),
        in_specs=[pl.BlockSpec((STEP,GLA_PROJ), rev), pl.BlockSpec((1,) + _GLA_STATE, lambda c: (nc - 1 - c, 0, 0, 0)),
                  pl.BlockSpec((STEP,GLA_VD), rev), _full_spec(w_a2.shape), _full_spec(b_a.shape), _full_spec(norm_g.shape)],
        out_specs=[pl.BlockSpec((STEP,GLA_PROJ), rev), _full_spec(w_a2.shape), _full_spec(b_a.shape), _full_spec(norm_g.shape)],
        out_shape=[jax.ShapeDtypeStruct((seq, GLA_PROJ), BF16), jax.ShapeDtypeStruct(w_a2.shape, F32),
                   jax.ShapeDtypeStruct(b_a.shape, F32), jax.ShapeDtypeStruct(norm_g.shape, F32)],
        scratch_shapes=[pltpu.VMEM(_GLA_STATE, F32)],
        semantics=("arbitrary",),
    )(proj, sprev, d_out, w_a2, b_a, norm_g)


def _ssd_chunk(z, xbc, dt_raw, hs, dt_bias, a_log, d_skip, norm_g):
    t = z.shape[0]
    xs = xbc[:, :SSD_DINNER]
    bm = xbc[:, SSD_DINNER:SSD_DINNER + SSD_GN]
    cm = xbc[:, SSD_DINNER + SSD_GN:]
    dt = _softplus(dt_raw + dt_bias)
    da = dt * (-jnp.exp(a_log))
    tri = _tri(t).astype(F32)
    eye = (lax.broadcasted_iota(jnp.int32, (t, t), 0) == lax.broadcasted_iota(jnp.int32, (t, t), 1)).astype(F32)
    cum = _dot(tri, da, exact=True)
    cum_t = _dot(da, tri, 0, 1, exact=True)
    dt_t = _dot(dt, eye, 0, 0, exact=True)
    cum_end = cum[t - 1:t, :]
    w_state = dt * jnp.exp(cum_end - cum)
    e_cum = jnp.exp(cum)
    g_end = jnp.exp(cum_end)
    head_of = lambda axis: lax.shift_right_logical(lax.broadcasted_iota(jnp.int32, (SSD_GW, SSD_GW), axis),
                                                   jnp.int32(SSD_HEADDIM.bit_length() - 1))
    same_head = head_of(0) == head_of(1)
    ys, new_hs = [], []
    for g in range(SSD_GROUPS):
        heads = range(g * SSD_HPG, (g + 1) * SSD_HPG)
        cols = slice(g * SSD_GW, (g + 1) * SSD_GW)

        def spread(a):
            return jnp.concatenate([jnp.broadcast_to(a[:, h:h + 1], (a.shape[0], SSD_HEADDIM)) for h in heads], axis=1)

        def row(a_t):
            return jnp.concatenate([a_t[h:h + 1, :] for h in heads], axis=1)

        bm_g = bm[:, g * SSD_DSTATE:(g + 1) * SSD_DSTATE]
        cm_g = cm[:, g * SSD_DSTATE:(g + 1) * SSD_DSTATE]
        xs_g = xs[:, cols]
        cb = _dot(cm_g, jnp.concatenate([bm_g] * SSD_HPG, axis=0), 1, 1)
        mix = cb * jnp.exp(-jnp.abs(spread(cum) - row(cum_t))) * row(dt_t)
        x_diag = jnp.where(same_head, jnp.concatenate([xs_g] * SSD_HPG, axis=0), 0.0)
        y = _dot(mix, x_diag)
        y = y + _dot(cm_g, hs[g], 1, 1) * spread(e_cum)
        y = y + spread(d_skip) * xs_g
        states = _dot(xs_g * spread(w_state), bm_g, 0, 0)
        decayed = jnp.concatenate([g_end[:, h:h + 1] * hs[g][j * SSD_HEADDIM:(j + 1) * SSD_HEADDIM, :]
                                   for j, h in enumerate(heads)], axis=0)
        new_hs.append(decayed + states)
        yg = y * _silu(z[:, cols])
        ys.append(yg * lax.rsqrt(jnp.mean(yg * yg, axis=-1, keepdims=True) + EPS) * norm_g[:, cols])
    return jnp.concatenate(ys, axis=1), tuple(new_hs)


_SSD_STATE = (SSD_GROUPS, SSD_GW, SSD_DSTATE)


def _ssd_step(z, xbc, dt_raw, hs, dt_bias, a_log, d_skip, norm_g):
    outs = []
    for s in range(STEP_CHUNKS):
        rows = slice(s * CHUNK, (s + 1) * CHUNK)
        out, hs = _ssd_chunk(z[rows], xbc[rows], dt_raw[rows], hs, dt_bias, a_log, d_skip, norm_g)
        outs.append(out)
    return jnp.concatenate(outs, axis=0), hs
_SSD_DT_BLOCK = (SSD_DINNER + SSD_XBC) // LANES


def _ssd_core_fwd(proj, xbc, dt_bias, a_log, d_skip, norm_g, cargo=()):
    seq = proj.shape[0]
    nc = seq // STEP

    def body(z_ref, xbc_ref, dt_ref, db_ref, al_ref, ds_ref, ng_ref, o_ref, hprev_ref, hs_ref):
        @pl.when(pl.program_id(0) == 0)
        def _():
            hs_ref[...] = jnp.zeros_like(hs_ref)

        hs = tuple(hs_ref[g] for g in range(SSD_GROUPS))
        for g in range(SSD_GROUPS):
            hprev_ref[0, g] = hs[g]
        out, new_hs = _ssd_step(z_ref[...], xbc_ref[...], dt_ref[...], hs, db_ref[...], al_ref[...], ds_ref[...], ng_ref[...])
        o_ref[...] = out
        for g in range(SSD_GROUPS):
            hs_ref[g] = new_hs[g]

    return _cargo_call(
        body, cargo, name="ssd_core_fwd", grid=(nc,),
        in_specs=[pl.BlockSpec((STEP,SSD_DINNER), lambda c: (c, 0)), pl.BlockSpec((STEP,SSD_XBC), lambda c: (c, 0)),
                  pl.BlockSpec((STEP,LANES), lambda c: (c, _SSD_DT_BLOCK)),
                  _full_spec(dt_bias.shape), _full_spec(a_log.shape), _full_spec(d_skip.shape), _full_spec(norm_g.shape)],
        out_specs=[pl.BlockSpec((STEP,SSD_DINNER), lambda c: (c, 0)), pl.BlockSpec((1,) + _SSD_STATE, lambda c: (c, 0, 0, 0))],
        out_shape=[jax.ShapeDtypeStruct((seq, SSD_DINNER), F32), jax.ShapeDtypeStruct((nc,) + _SSD_STATE, F32)],
        scratch_shapes=[pltpu.VMEM(_SSD_STATE, F32)],
        semantics=("arbitrary",),
    )(proj, xbc, proj, dt_bias, a_log, d_skip, norm_g)


def _ssd_core_bwd(proj, xbc, hprev, d_out, dt_bias, a_log, d_skip, norm_g, cargo=()):
    seq = proj.shape[0]
    nc = seq // STEP

    def body(z_ref, xbc_ref, dt_ref, hprev_ref, do_ref, db_ref, al_ref, ds_ref, ng_ref,
             dz_ref, dxbc_ref, ddt_ref, ddb_ref, dal_ref, dds_ref, dng_ref, dhs_ref):
        @pl.when(pl.program_id(0) == 0)
        def _():
            dhs_ref[...] = jnp.zeros_like(dhs_ref)
            ddb_ref[...] = jnp.zeros_like(ddb_ref)
            dal_ref[...] = jnp.zeros_like(dal_ref)
            dds_ref[...] = jnp.zeros_like(dds_ref)
            dng_ref[...] = jnp.zeros_like(dng_ref)

        hs = tuple(hprev_ref[0, g] for g in range(SSD_GROUPS))
        _, vjp = jax.vjp(_ssd_step, z_ref[...], xbc_ref[...], dt_ref[...], hs, db_ref[...], al_ref[...], ds_ref[...], ng_ref[...])
        d_next = tuple(dhs_ref[g] for g in range(SSD_GROUPS))
        d_z, d_xbc, d_dt, d_hs, d_db, d_al, d_ds, d_ng = vjp((do_ref[...], d_next))
        dz_ref[...] = d_z.astype(dz_ref.dtype)
        dxbc_ref[...] = d_xbc
        ddt_ref[...] = d_dt.astype(ddt_ref.dtype)
        for g in range(SSD_GROUPS):
            dhs_ref[g] = d_hs[g]
        ddb_ref[...] += d_db
        dal_ref[...] += d_al
        dds_ref[...] += d_ds
        dng_ref[...] += d_ng

    rev = lambda c: (nc - 1 - c, 0)
    vec = [_full_spec(dt_bias.shape), _full_spec(a_log.shape), _full_spec(d_skip.shape), _full_spec(norm_g.shape)]
    return _cargo_call(
        body, cargo, name="ssd_core_bwd", grid=(nc,),
        in_specs=[pl.BlockSpec((STEP,SSD_DINNER), rev), pl.BlockSpec((STEP,SSD_XBC), rev),
                  pl.BlockSpec((STEP,LANES), lambda c: (nc - 1 - c, _SSD_DT_BLOCK)),
                  pl.BlockSpec((1,) + _SSD_STATE, lambda c: (nc - 1 - c, 0, 0, 0)),
                  pl.BlockSpec((STEP,SSD_DINNER), rev)] + vec,
        out_specs=[pl.BlockSpec((STEP,SSD_DINNER), rev), pl.BlockSpec((STEP,SSD_XBC), rev),
                   pl.BlockSpec((STEP,LANES), rev)] + vec,
        out_shape=[jax.ShapeDtypeStruct((seq, SSD_DINNER), BF16), jax.ShapeDtypeStruct((seq, SSD_XBC), F32),
                   jax.ShapeDtypeStruct((seq, LANES), BF16),
                   jax.ShapeDtypeStruct(dt_bias.shape, F32), jax.ShapeDtypeStruct(a_log.shape, F32),
                   jax.ShapeDtypeStruct(d_skip.shape, F32), jax.ShapeDtypeStruct(norm_g.shape, F32)],
        scratch_shapes=[pltpu.VMEM(_SSD_STATE, F32)],
        semantics=("arbitrary",),
    )(proj, xbc, proj, hprev, d_out, dt_bias, a_log, d_skip, norm_g)


CONV_COLS = 2048
CONV_HALO = 8


def _conv_taps(xx, rows):
    last = SSD_CONV - 1
    return [pltpu.roll(xx, last - k, 0)[CONV_HALO:CONV_HALO + rows] if k < last else xx[CONV_HALO:CONV_HALO + rows]
            for k in range(SSD_CONV)]


def _ssd_conv_fwd(proj, conv_w, conv_b, name):
    rows = proj.shape[0]
    tr = _tile(rows, 512, CONV_HALO)
    first_col = SSD_DINNER // CONV_COLS

    def body(x_ref, halo_ref, w_ref, b_ref, o_ref):
        halo = jnp.where(pl.program_id(0) == 0, 0.0, halo_ref[...])
        taps = _conv_taps(jnp.concatenate([halo, x_ref[...]], axis=0), tr)
        out = b_ref[...]
        for k in range(SSD_CONV):
            out = out + taps[k] * w_ref[k:k + 1, :]
        o_ref[...] = _silu(out)

    return pl.pallas_call(
        body, name=name, grid=(rows // tr, SSD_XBC // CONV_COLS),
        in_specs=[pl.BlockSpec((tr, CONV_COLS), lambda i, j: (i, first_col + j)),
                  pl.BlockSpec((CONV_HALO, CONV_COLS), lambda i, j: (jnp.maximum(i * (tr // CONV_HALO) - 1, 0), first_col + j)),
                  pl.BlockSpec((SSD_CONV, CONV_COLS), lambda i, j: (0, j)), pl.BlockSpec((1, CONV_COLS), lambda i, j: (0, j))],
        out_specs=pl.BlockSpec((tr, CONV_COLS), lambda i, j: (i, j)),
        out_shape=jax.ShapeDtypeStruct((rows, SSD_XBC), F32),
        compiler_params=pltpu.CompilerParams(dimension_semantics=("parallel", "parallel"), vmem_limit_bytes=VMEM_LIMIT),
    )(proj, proj, conv_w, conv_b[None])


def _ssd_conv_bwd(proj, d_xbc, conv_w, conv_b, name):
    rows = proj.shape[0]
    tr = _tile(rows, 512, CONV_HALO)
    nb, halos = rows // tr, tr // CONV_HALO
    first_col = SSD_DINNER // CONV_COLS

    def body(x_ref, before_ref, after_ref, d_ref, d_after_ref, w_ref, b_ref, dx_ref, dw_ref, db_ref):
        i = pl.program_id(1)

        @pl.when(i == 0)
        def _():
            dw_ref[...] = jnp.zeros_like(dw_ref)
            db_ref[...] = jnp.zeros_like(db_ref)

        before = jnp.where(i == 0, 0.0, before_ref[...])
        taps = _conv_taps(jnp.concatenate([before, x_ref[...], after_ref[...]], axis=0), tr + CONV_HALO)
        out = b_ref[...]
        for k in range(SSD_CONV):
            out = out + taps[k] * w_ref[k:k + 1, :]
        sig = 1.0 / (1.0 + jnp.exp(-out))
        d_after = jnp.where(i == nb - 1, 0.0, d_after_ref[...])
        d_out = jnp.concatenate([d_ref[...], d_after], axis=0) * sig * (1.0 + out * (1.0 - sig))
        d_x = d_out[:tr] * w_ref[SSD_CONV - 1:SSD_CONV, :]
        for k in range(SSD_CONV - 1):
            ahead = SSD_CONV - 1 - k
            d_x = d_x + pltpu.roll(d_out, tr + CONV_HALO - ahead, 0)[:tr] * w_ref[k:k + 1, :]
        dx_ref[...] = d_x.astype(dx_ref.dtype)
        for k in range(SSD_CONV):
            dw_ref[k:k + 1, :] += jnp.sum(d_out[:tr] * taps[k][:tr], axis=0, keepdims=True)
        db_ref[...] += jnp.sum(d_out[:tr], axis=0, keepdims=True)

    before = lambda j, i: jnp.maximum(i * halos - 1, 0)
    after = lambda j, i: jnp.minimum((i + 1) * halos, nb * halos - 1)
    d_x, d_w, d_b = pl.pallas_call(
        body, name=name, grid=(SSD_XBC // CONV_COLS, nb),
        in_specs=[pl.BlockSpec((tr, CONV_COLS), lambda j, i: (i, first_col + j)),
                  pl.BlockSpec((CONV_HALO, CONV_COLS), lambda j, i: (before(j, i), first_col + j)),
                  pl.BlockSpec((CONV_HALO, CONV_COLS), lambda j, i: (after(j, i), first_col + j)),
                  pl.BlockSpec((tr, CONV_COLS), lambda j, i: (i, j)),
                  pl.BlockSpec((CONV_HALO, CONV_COLS), lambda j, i: (after(j, i), j)),
                  pl.BlockSpec((SSD_CONV, CONV_COLS), lambda j, i: (0, j)), pl.BlockSpec((1, CONV_COLS), lambda j, i: (0, j))],
        out_specs=[pl.BlockSpec((tr, CONV_COLS), lambda j, i: (i, j)), pl.BlockSpec((SSD_CONV, CONV_COLS), lambda j, i: (0, j)),
                   pl.BlockSpec((1, CONV_COLS), lambda j, i: (0, j))],
        out_shape=[jax.ShapeDtypeStruct((rows, SSD_XBC), BF16), jax.ShapeDtypeStruct((SSD_CONV, SSD_XBC), F32),
                   jax.ShapeDtypeStruct((1, SSD_XBC), F32)],
        compiler_params=pltpu.CompilerParams(dimension_semantics=("parallel", "arbitrary"), vmem_limit_bytes=VMEM_LIMIT),
    )(proj, proj, proj, d_xbc, d_xbc, conv_w, conv_b[None])
    return d_x, d_w, d_b[0]


def _s5_boundary_scan(z, lam_re, lam_im, name, reverse=False):
    n_chunks, groups, width = z.shape
    tn = _tile(n_chunks, 128, 1)
    blocks = n_chunks // tn
    lam_a = jnp.concatenate([lam_re, lam_re], axis=1)
    lam_b = jnp.concatenate([-lam_im, lam_im], axis=1)

    def body(z_ref, a_ref, b_ref, x_ref, carry_ref):
        @pl.when(pl.program_id(0) == 0)
        def _():
            carry_ref[...] = jnp.zeros_like(carry_ref)

        a, b = a_ref[...], b_ref[...]

        def step(i, x):
            n = tn - 1 - i if reverse else i
            x_ref[n] = x
            return a * x + b * pltpu.roll(x, width // 2, 1) + z_ref[n]

        carry_ref[...] = lax.fori_loop(0, tn, step, carry_ref[...])

    block = pl.BlockSpec((tn, groups, width), (lambda i: (blocks - 1 - i, 0, 0)) if reverse else (lambda i: (i, 0, 0)))
    return pl.pallas_call(
        body, name=name, grid=(blocks,), in_specs=[block, _full_spec((groups, width)), _full_spec((groups, width))],
        out_specs=block, out_shape=jax.ShapeDtypeStruct(z.shape, F32), scratch_shapes=[pltpu.VMEM((groups, width), F32)],
        compiler_params=pltpu.CompilerParams(dimension_semantics=("arbitrary",), vmem_limit_bytes=VMEM_LIMIT),
    )(z, lam_a, lam_b)


_FLIPS = [(kx, ky, kc) for kx in (0, 1) for ky in (0, 1) for kc in (0, 1)][1:]


def _mesh_position():
    return lax.axis_index("x"), lax.axis_index("y"), lax.axis_index("c")


def _peer(pos, flip):
    return tuple((1 - p) if f else p for p, f in zip(pos, flip))


def _index(pos):
    return 4 * pos[0] + 2 * pos[1] + pos[2]


_ANY = pl.BlockSpec(memory_space=pl.ANY)


def _moved_shape(kind, x):
    return jax.ShapeDtypeStruct(((N_DEV,) + x.shape) if kind == "gather" else x.shape, x.dtype)


def _moves(kind, x_ref, out_ref, send_sems, recv_sems, local_sem):
    me = _mesh_position()
    source = (lambda pos: x_ref) if kind == "gather" else (lambda pos: x_ref.at[_index(pos)])
    local = pltpu.make_async_copy(source(me), out_ref.at[_index(me)], local_sem)
    outgoing, incoming = [], []
    for k, flip in enumerate(_FLIPS):
        peer = _peer(me, flip)
        copy = lambda slot: pltpu.make_async_remote_copy(
            src_ref=source(peer), dst_ref=out_ref.at[_index(slot)], send_sem=send_sems.at[k], recv_sem=recv_sems.at[k],
            device_id=peer, device_id_type=pl.DeviceIdType.MESH)
        outgoing.append(copy(me))
        incoming.append(copy(peer))
    return local, outgoing, incoming


def _start(moves):
    local, outgoing, _ = moves
    local.start()
    for cp in outgoing:
        cp.start()


def _finish(moves):
    local, outgoing, incoming = moves
    for cp in incoming:
        cp.wait_recv()
    for cp in outgoing:
        cp.wait_send()
    local.wait()


def _collective(kind, x, name):
    def body(x_ref, out_ref, send_sems, recv_sems, local_sem):
        moves = _moves(kind, x_ref, out_ref, send_sems, recv_sems, local_sem)
        _start(moves)
        _finish(moves)

    return pl.pallas_call(
        body, name=name, in_specs=[_ANY], out_specs=_ANY, out_shape=_moved_shape(kind, x),
        scratch_shapes=[pltpu.SemaphoreType.DMA((N_DEV - 1,)), pltpu.SemaphoreType.DMA((N_DEV - 1,)), pltpu.SemaphoreType.DMA],
        compiler_params=pltpu.CompilerParams(has_side_effects=True),
    )(x)


def _adamw(parts, w, m, v, name):
    n_parts = parts.shape[0]
    layers, rows, cols = w.shape
    tr = _tile(rows, 256, 8)

    def body(p_ref, w_ref, m_ref, v_ref, g_ref, d_ref, mo_ref, vo_ref):
        g = p_ref[0].astype(F32)
        for s in range(1, n_parts):
            g = g + p_ref[s].astype(F32)
        m_new = ADAM_B1 * m_ref[...] + (1.0 - ADAM_B1) * g
        v_new = ADAM_B2 * v_ref[...] + (1.0 - ADAM_B2) * (g * g)
        m_hat = m_new / (1.0 - ADAM_B1 ** ADAM_STEP)
        v_hat = v_new / (1.0 - ADAM_B2 ** ADAM_STEP)
        g_ref[...] = g
        d_ref[...] = -ADAM_LR * (m_hat / (jnp.sqrt(v_hat) + ADAM_EPS) + ADAM_WD * w_ref[...])
        mo_ref[...] = m_new
        vo_ref[...] = v_new

    blk = pl.BlockSpec((None, tr, cols), lambda l, i: (l, i, 0))
    shape = jax.ShapeDtypeStruct(w.shape, F32)
    return pl.pallas_call(
        body, name=name, grid=(layers, rows // tr),
        in_specs=[pl.BlockSpec((n_parts, None, tr, cols), lambda l, i: (0, l, i, 0)), blk, blk, blk],
        out_specs=[blk, blk, blk, blk], out_shape=[shape, shape, shape, shape],
        compiler_params=pltpu.CompilerParams(dimension_semantics=("parallel", "parallel"), vmem_limit_bytes=VMEM_LIMIT),
    )(parts, w, m, v)


def _sum_parts(parts, name):
    _, rows, cols = parts.shape
    tr = _tile(rows, 256, 8)

    def body(p_ref, o_ref):
        total = p_ref[0]
        for s in range(1, N_DEV):
            total = total + p_ref[s]
        o_ref[...] = total

    return pl.pallas_call(
        body, name=name, grid=(rows // tr,),
        in_specs=[pl.BlockSpec((N_DEV, tr, cols), lambda i: (0, i, 0))], out_specs=pl.BlockSpec((tr, cols), lambda i: (i, 0)),
        out_shape=jax.ShapeDtypeStruct((rows, cols), parts.dtype),
        compiler_params=pltpu.CompilerParams(dimension_semantics=("parallel",), vmem_limit_bytes=VMEM_LIMIT),
    )(parts)


def _row_tile(rows):
    return _tile(rows, 512, 16)


def _row_spec(rows, cols, block=0):
    return pl.BlockSpec((_row_tile(rows), cols), lambda i: (i, block))


def _rows_params(accumulates):
    return pltpu.CompilerParams(dimension_semantics=("arbitrary" if accumulates else "parallel",),
                                vmem_limit_bytes=VMEM_LIMIT)


def _gu_matmul(x, w, *, name, cargo=()):
    rows, k_dim = x.shape
    tm, tn = _tile(rows, 512, 16), _tile(FFN_HIDDEN, 1408, LANES)
    nj = FFN_HIDDEN // tn

    def body(x_ref, wg_ref, wu_ref, g_ref, u_ref, a_ref):
        xb = x_ref[...].astype(_MXU_DTYPE)
        g = jnp.dot(xb, wg_ref[...].astype(_MXU_DTYPE), preferred_element_type=F32)
        u = jnp.dot(xb, wu_ref[...].astype(_MXU_DTYPE), preferred_element_type=F32)
        g_ref[...] = g.astype(g_ref.dtype)
        u_ref[...] = u.astype(u_ref.dtype)
        a_ref[...] = (_silu(g) * u).astype(a_ref.dtype)

    out = pl.BlockSpec((tm, tn), lambda i, j: (i, j))
    shape = jax.ShapeDtypeStruct((rows, FFN_HIDDEN), BF16)
    return _cargo_call(
        body, cargo, name=name, grid=(rows // tm, nj),
        in_specs=[pl.BlockSpec((tm, k_dim), lambda i, j: (i, 0)), pl.BlockSpec((k_dim, tn), lambda i, j: (0, j)),
                  pl.BlockSpec((k_dim, tn), lambda i, j: (0, nj + j))],
        out_specs=[out, out, out], out_shape=[shape, shape, shape], scratch_shapes=[], semantics=("parallel", "parallel"),
    )(x, w, w)


def _swiglu_bwd(g, u, d_act, name):
    rows = g.shape[0]

    def body(g_ref, u_ref, d_ref, o_ref):
        g, u, d = g_ref[...].astype(F32), u_ref[...].astype(F32), d_ref[...].astype(F32)
        sig = 1.0 / (1.0 + jnp.exp(-g))
        o_ref[:, :FFN_HIDDEN] = (d * u * sig * (1.0 + g * (1.0 - sig))).astype(o_ref.dtype)
        o_ref[:, FFN_HIDDEN:] = (d * g * sig).astype(o_ref.dtype)

    return pl.pallas_call(
        body, name=name, grid=(rows // _row_tile(rows),),
        in_specs=[_row_spec(rows, FFN_HIDDEN)] * 3, out_specs=_row_spec(rows, 2 * FFN_HIDDEN),
        out_shape=jax.ShapeDtypeStruct((rows, 2 * FFN_HIDDEN), BF16), compiler_params=_rows_params(False))(g, u, d_act)


def _add_norm_fwd(h, y, gain, name):
    rows = h.shape[0]

    def body(*refs):
        if y is None:
            h_ref, g_ref, n_ref = refs
            x = h_ref[...]
        else:
            h_ref, y_ref, g_ref, s_ref, n_ref = refs
            x = h_ref[...] + y_ref[...]
            s_ref[...] = x
        n_ref[...] = (x * lax.rsqrt(jnp.mean(x * x, axis=-1, keepdims=True) + EPS) * g_ref[...]).astype(n_ref.dtype)

    row = _row_spec(rows, D_MODEL)
    ins = [h] if y is None else [h, y]
    out_shape = [jax.ShapeDtypeStruct((rows, D_MODEL), BF16)]
    if y is not None:
        out_shape = [jax.ShapeDtypeStruct((rows, D_MODEL), F32)] + out_shape
    res = pl.pallas_call(
        body, name=name, grid=(rows // _row_tile(rows),),
        in_specs=[row] * len(ins) + [_full_spec((1, D_MODEL))], out_specs=[row] * len(out_shape), out_shape=out_shape,
        compiler_params=_rows_params(False))(*ins, gain[None])
    return (h, res[0]) if y is None else (res[0], res[1])


def _norm_bwd(x, gain, d_n, d_skip, name):
    rows = x.shape[0]
    d_parts = d_n if isinstance(d_n, tuple) else (d_n,)

    def body(x_ref, g_ref, *refs):
        dn_refs, (ds_ref, dx_ref, dg_ref) = refs[:len(d_parts)], refs[len(d_parts):]

        @pl.when(pl.program_id(0) == 0)
        def _():
            dg_ref[...] = jnp.zeros_like(dg_ref)

        x, dn = x_ref[...], sum(r[...].astype(F32) for r in dn_refs)
        r = lax.rsqrt(jnp.mean(x * x, axis=-1, keepdims=True) + EPS)
        gd = g_ref[...] * dn
        dx_ref[...] = r * gd - x * (r * r * r) * jnp.mean(x * gd, axis=-1, keepdims=True) + ds_ref[...]
        dg_ref[...] += jnp.sum(x * r * dn, axis=0, keepdims=True)

    row = _row_spec(rows, D_MODEL)
    dx, dg = pl.pallas_call(
        body, name=name, grid=(rows // _row_tile(rows),),
        in_specs=[row, _full_spec((1, D_MODEL))] + [row] * (len(d_parts) + 1), out_specs=[row, _full_spec((1, D_MODEL))],
        out_shape=[jax.ShapeDtypeStruct((rows, D_MODEL), F32), jax.ShapeDtypeStruct((1, D_MODEL), F32)],
        compiler_params=_rows_params(True))(x, gain[None], *d_parts, d_skip)
    return dx, dg[0]


_GELU_C, _GELU_A = math.sqrt(2.0 / math.pi), 0.044715


def _s5_gate_fwd(y, u, d_skip, name):
    rows = y.shape[0]

    def body(y_ref, u_ref, d_ref, o_ref):
        x = y_ref[...].astype(F32) + d_ref[...] * u_ref[...].astype(F32)
        o_ref[...] = (0.5 * x * (1.0 + jnp.tanh(_GELU_C * (x + _GELU_A * x * x * x)))).astype(o_ref.dtype)

    row = _row_spec(rows, D_MODEL)
    return pl.pallas_call(
        body, name=name, grid=(rows // _row_tile(rows),), in_specs=[row, row, _full_spec((1, D_MODEL))], out_specs=row,
        out_shape=jax.ShapeDtypeStruct((rows, D_MODEL), BF16), compiler_params=_rows_params(False))(y, u, d_skip[None])


def _s5_gate_bwd(y, u, d_skip, d_act, name):
    rows = y.shape[0]

    def body(y_ref, u_ref, d_ref, da_ref, dy_ref, du_ref, dd_ref):
        @pl.when(pl.program_id(0) == 0)
        def _():
            dd_ref[...] = jnp.zeros_like(dd_ref)

        u = u_ref[...].astype(F32)
        x = y_ref[...].astype(F32) + d_ref[...] * u
        t = jnp.tanh(_GELU_C * (x + _GELU_A * x * x * x))
        slope = 0.5 * (1.0 + t) + 0.5 * x * (1.0 - t * t) * _GELU_C * (1.0 + 3.0 * _GELU_A * x * x)
        dx = da_ref[...].astype(F32) * slope
        dy_ref[...] = dx.astype(dy_ref.dtype)
        du_ref[...] = (dx * d_ref[...]).astype(du_ref.dtype)
        dd_ref[...] += jnp.sum(dx * u, axis=0, keepdims=True)

    row = _row_spec(rows, D_MODEL)
    shape = jax.ShapeDtypeStruct((rows, D_MODEL), BF16)
    d_y, d_u, d_d = pl.pallas_call(
        body, name=name, grid=(rows // _row_tile(rows),), in_specs=[row, row, _full_spec((1, D_MODEL)), row],
        out_specs=[row, row, _full_spec((1, D_MODEL))], out_shape=[shape, shape, jax.ShapeDtypeStruct((1, D_MODEL), F32)],
        compiler_params=_rows_params(True))(y, u, d_skip[None], d_act)
    return d_y, d_u, d_d[0]


def _glu_fwd(vg, name):
    rows = vg.shape[0]

    def body(v_ref, g_ref, o_ref):
        o_ref[...] = v_ref[...] / (1.0 + jnp.exp(-g_ref[...]))

    return pl.pallas_call(
        body, name=name, grid=(rows // _row_tile(rows),),
        in_specs=[_row_spec(rows, D_MODEL, 0), _row_spec(rows, D_MODEL, 1)], out_specs=_row_spec(rows, D_MODEL),
        out_shape=jax.ShapeDtypeStruct((rows, D_MODEL), F32), compiler_params=_rows_params(False))(vg, vg)


def _glu_bwd(vg, d_out, name):
    rows = vg.shape[0]

    def body(v_ref, g_ref, d_ref, o_ref):
        sig = 1.0 / (1.0 + jnp.exp(-g_ref[...]))
        d = d_ref[...]
        o_ref[:, :D_MODEL] = (d * sig).astype(o_ref.dtype)
        o_ref[:, D_MODEL:] = (d * v_ref[...] * sig * (1.0 - sig)).astype(o_ref.dtype)

    return pl.pallas_call(
        body, name=name, grid=(rows // _row_tile(rows),),
        in_specs=[_row_spec(rows, D_MODEL, 0), _row_spec(rows, D_MODEL, 1), _row_spec(rows, D_MODEL)],
        out_specs=_row_spec(rows, 2 * D_MODEL), out_shape=jax.ShapeDtypeStruct((rows, 2 * D_MODEL), BF16),
        compiler_params=_rows_params(False))(vg, vg, d_out)


def _loss_head(h, gain, target, name):
    rows = h.shape[0]

    def body(x_ref, g_ref, t_ref, loss_ref, dx_ref, dg_ref):
        @pl.when(pl.program_id(0) == 0)
        def _():
            loss_ref[...] = jnp.zeros_like(loss_ref)
            dg_ref[...] = jnp.zeros_like(dg_ref)

        x = x_ref[...]
        r = lax.rsqrt(jnp.mean(x * x, axis=-1, keepdims=True) + EPS)
        err = x * r * g_ref[...] - t_ref[...]
        loss_ref[...] += 0.5 * jnp.sum(jnp.mean(err * err, axis=-1, keepdims=True), axis=0, keepdims=True)
        dy = err * (1.0 / D_MODEL)
        gd = g_ref[...] * dy
        dx_ref[...] = r * gd - x * (r * r * r) * jnp.mean(x * gd, axis=-1, keepdims=True)
        dg_ref[...] += jnp.sum(x * r * dy, axis=0, keepdims=True)

    row = _row_spec(rows, D_MODEL)
    loss, dx, dg = pl.pallas_call(
        body, name=name, grid=(rows // _row_tile(rows),),
        in_specs=[row, _full_spec((1, D_MODEL)), row], out_specs=[_full_spec((1, 1)), row, _full_spec((1, D_MODEL))],
        out_shape=[jax.ShapeDtypeStruct((1, 1), F32), jax.ShapeDtypeStruct((rows, D_MODEL), F32),
                   jax.ShapeDtypeStruct((1, D_MODEL), F32)],
        compiler_params=_rows_params(True))(h, gain[None], target)
    return loss[0, 0], dx, dg[0]


def _join_cols(blocks, n_out, name):
    _, layers, rows, n = blocks.shape
    tr = _tile(rows, 256, 16)

    def body(x_ref, o_ref):
        for d in range(N_DEV):
            o_ref[:, d * n:(d + 1) * n] = x_ref[d]
        if n_out > N_DEV * n:
            o_ref[:, N_DEV * n:] = jnp.zeros((tr, n_out - N_DEV * n), o_ref.dtype)

    return pl.pallas_call(
        body, name=name, grid=(layers, rows // tr),
        in_specs=[pl.BlockSpec((N_DEV, None, tr, n), lambda l, i: (0, l, i, 0))],
        out_specs=pl.BlockSpec((None, tr, n_out), lambda l, i: (l, i, 0)),
        out_shape=jax.ShapeDtypeStruct((layers, rows, n_out), blocks.dtype),
        compiler_params=pltpu.CompilerParams(dimension_semantics=("parallel", "parallel"), vmem_limit_bytes=VMEM_LIMIT),
    )(blocks)


def _split_cols(full, n, name):
    rows = full.shape[0]
    tr = _tile(rows, 256, 16)

    def body(x_ref, o_ref):
        for d in range(N_DEV):
            o_ref[d] = x_ref[:, d * n:(d + 1) * n]

    return pl.pallas_call(
        body, name=name, grid=(rows // tr,),
        in_specs=[pl.BlockSpec((tr, full.shape[1]), lambda i: (i, 0))],
        out_specs=pl.BlockSpec((N_DEV, tr, n), lambda i: (0, i, 0)),
        out_shape=jax.ShapeDtypeStruct((N_DEV, rows, n), full.dtype),
        compiler_params=pltpu.CompilerParams(dimension_semantics=("parallel",), vmem_limit_bytes=VMEM_LIMIT),
    )(full)


def _pack(arrays):
    flat = jnp.concatenate([a.reshape(-1) for a in arrays])
    unit = FLAT_COLS * FLAT_ROWS_ALIGN
    padded = -(-flat.shape[0] // unit) * unit
    return jnp.pad(flat, (0, padded - flat.shape[0])).reshape(-1, FLAT_COLS)


def _unpack(flat, shapes, lead=()):
    flat = flat.reshape(lead + (-1,))
    out, off = [], 0
    for shape in shapes:
        n = math.prod(shape)
        out.append(flat[..., off:off + n].reshape(lead + tuple(shape)))
        off += n
    return out


def _join(blocks, axis):
    moved = jnp.moveaxis(blocks, 0, axis)
    shape = list(moved.shape)
    shape[axis:axis + 2] = [shape[axis] * shape[axis + 1]]
    return moved.reshape(shape)


def _own_shard(full, axis, position):
    n = full.shape[axis] // N_DEV
    return lax.dynamic_slice_in_dim(full, position * n, n, axis)


def _s5_operators(log_dt, a_re, a_im, b_re, b_im, c_re, c_im):
    t = S5_CHUNK
    hi = lax.Precision.HIGHEST
    step = jnp.exp(log_dt)[:, None]
    mag = jnp.exp(step * a_re)
    abar_re = mag * jnp.cos(step * a_im)
    abar_im = mag * jnp.sin(step * a_im)
    den = a_re * a_re + a_im * a_im
    f_re = ((abar_re - 1.0) * a_re + abar_im * a_im) / den
    f_im = (abar_im * a_re - (abar_re - 1.0) * a_im) / den
    bb_re = f_re[..., None] * b_re - f_im[..., None] * b_im
    bb_im = f_re[..., None] * b_im + f_im[..., None] * b_re
    j = jnp.arange(t + 1, dtype=F32)[:, None, None]
    pmag = jnp.exp(j * (step * a_re))
    pw_re = pmag * jnp.cos(j * (step * a_im))
    pw_im = pmag * jnp.sin(j * (step * a_im))
    cl_re = c_re[None] * pw_re[:t, :, None, :] - c_im[None] * pw_im[:t, :, None, :]
    cl_im = c_re[None] * pw_im[:t, :, None, :] + c_im[None] * pw_re[:t, :, None, :]
    kern = (jnp.einsum('jgcp,gpk->jgck', cl_re, bb_re, precision=hi)
            - jnp.einsum('jgcp,gpk->jgck', cl_im, bb_im, precision=hi))
    rp_re, rp_im = pw_re[:t][::-1], pw_im[:t][::-1]
    wz_re = rp_re[:, :, :, None] * bb_re[None] - rp_im[:, :, :, None] * bb_im[None]
    wz_im = rp_re[:, :, :, None] * bb_im[None] + rp_im[:, :, :, None] * bb_re[None]
    w_z = jnp.concatenate([wz_re, wz_im], axis=2).transpose(1, 0, 3, 2).reshape(S5_GROUPS, t * S5_GROUP, 2 * S5_STATE)
    cy_re = c_re[None] * pw_re[1:, :, None, :] - c_im[None] * pw_im[1:, :, None, :]
    cy_im = c_re[None] * pw_im[1:, :, None, :] + c_im[None] * pw_re[1:, :, None, :]
    w_y = jnp.concatenate([cy_re, -cy_im], axis=3).transpose(1, 3, 0, 2).reshape(S5_GROUPS, 2 * S5_STATE, t * S5_GROUP)
    return kern, w_z, w_y, pw_re[t], pw_im[t]


def _s5_lag_selector():
    t = S5_CHUNK
    lag = jnp.arange(t)[:, None] - jnp.arange(t)[None, :]
    return (lag[:, :, None] == jnp.arange(t)[None, None, :]).astype(F32).reshape(t * t, t)


def _s5_toeplitz(kern, tag):
    t = S5_CHUNK
    sel = _s5_lag_selector()
    flat = _matmul(sel, kern.reshape(t, -1), out_dtype=BF16, name=tag + "_toeplitz")
    toep = flat.reshape(t, t, S5_GROUPS, S5_GROUP, S5_GROUP).transpose(2, 1, 4, 0, 3)
    toep = toep.reshape(S5_GROUPS, t * S5_GROUP, t * S5_GROUP)

    def backward(d_toep):
        d_flat = d_toep.reshape(S5_GROUPS, t, S5_GROUP, t, S5_GROUP).transpose(3, 1, 0, 4, 2).reshape(t * t, -1)
        return _matmul(sel, d_flat, ta=True, name=tag + "_toeplitz_dw").reshape(kern.shape)

    return toep, backward


_BIG = [("gla_w_in", 2), ("gla_w_out", 1), ("ssd_w_in", 2), ("ssd_w_out", 1), ("s5_w_glu", 2), ("ffn_w_gu", 2),
        ("ffn_w_down", 1)]
_PADDED_COLS = {"gla_w_in": GLA_PROJ, "ssd_w_in": SSD_PROJ}


class _Traffic:
    LINK_BYTES_PER_SECOND = 7.0e10
    MATMUL_FLOPS = 7.0e14

    def __init__(self, shards, plan):
        self.shards, self.plan = shards, plan
        self.position = 0
        self.queue = []
        self.weights, self.received = {}, {}
        self.early = None
        self.standalone = self.serial = 0
        for key in plan:
            self._request(key)

    def _request(self, key):
        shard = self.shards[key]
        seconds = (N_DEV - 1) * shard.size * shard.dtype.itemsize / self.LINK_BYTES_PER_SECOND
        self._enqueue("gather", shard, seconds, key, lambda blocks: self.weights.__setitem__(key, self._assemble(key, blocks)))

    def _enqueue(self, kind, x, seconds, key, deliver):
        self.queue.append((kind, x, seconds, key, deliver, self.serial))
        self.serial += 1

    @staticmethod
    def _assemble(key, blocks):
        name, layer = key
        if dict(_BIG)[name] == 1:
            return blocks.reshape((N_DEV * blocks.shape[1], blocks.shape[2]))
        n_out = _PADDED_COLS.get(name, N_DEV * blocks.shape[2])
        return _join_cols(blocks[:, None], n_out, f"join_{name}_{layer}")[0]

    def take(self, key):
        assert key == self.plan[self.position], (key, self.plan[self.position])
        self.position += 1
        while key not in self.weights:
            self._alone(self.queue.pop(0))
        return self.weights[key]

    def run(self, seconds, call, more_carriers_follow=False):
        riders, waiting, left = [], [], seconds
        for item in self.queue:
            if item[2] <= left:
                riders.append(item)
                left -= item[2]
            else:
                waiting.append(item)
        due = [item for item in waiting if item[3] is not None and self.position < len(self.plan)
               and item[3] == self.plan[self.position]]
        if due and not more_carriers_follow:
            left = seconds - due[0][2]
            kept = []
            for item in riders:
                if item[2] <= left:
                    kept.append(item)
                    left -= item[2]
                else:
                    waiting.append(item)
            riders = due + kept
            waiting = [item for item in waiting if item is not due[0]]
            waiting.sort(key=lambda item: item[5])
        self.queue = waiting
        if not riders:
            return call(())
        results, moved = call([(kind, x) for kind, x, *_ in riders])
        for item, y in zip(riders, moved):
            item[4](y)
        return results

    def matmul(self, a, b, more_carriers_follow=False, **kw):
        m, n = (a.shape[-1] if kw.get("ta") else a.shape[-2]), (b.shape[-2] if kw.get("tb") else b.shape[-1])
        k = a.shape[-2] if kw.get("ta") else a.shape[-1]
        return self.run(2.0 * m * n * k / self.MATMUL_FLOPS, lambda cargo: _matmul(a, b, cargo=cargo, **kw),
                        more_carriers_follow)

    def send_gradient(self, key, dw):
        name, layer = key
        shard = self.shards[key]
        if dict(_BIG)[name] == 1:
            blocks = dw.reshape((N_DEV,) + shard.shape)
        else:
            blocks = _split_cols(dw, shard.shape[1], f"split_{name}_{layer}")
        seconds = (N_DEV - 1) * shard.size * shard.dtype.itemsize / self.LINK_BYTES_PER_SECOND
        self._enqueue("exchange", blocks, seconds, None, lambda parts: self.received.__setitem__(key, parts))

    def gather_early(self, packed):
        seconds = (N_DEV - 1) * packed.size * packed.dtype.itemsize / self.LINK_BYTES_PER_SECOND
        self._enqueue("gather", packed, seconds, None, lambda parts: setattr(self, "early", parts))

    def _alone(self, item):
        kind, x, _, _, deliver, _ = item
        deliver(_collective(kind, x, f"{kind}_alone_{self.standalone}"))
        self.standalone += 1

    def flush(self):
        for item in self.queue:
            self._alone(item)
        self.queue = []


_GLA_FWD_SECONDS, _GLA_BWD_SECONDS, _SSD_FWD_SECONDS, _SSD_BWD_SECONDS = 1.25e-6, 3.4e-6, 3.5e-6, 14e-6


def _linear(x, w, tag, out_dtype=F32, dx_dtype=F32, more_carriers_follow=False):
    traffic, key = w
    weight = traffic.take(key)
    y = traffic.matmul(x, weight, more_carriers_follow, out_dtype=out_dtype, name=tag + "_fwd")

    def backward(dy):
        dx = traffic.matmul(dy, weight, tb=True, out_dtype=dx_dtype, name=tag + "_dx")
        traffic.send_gradient(key, traffic.matmul(x, dy, ta=True, out_dtype=BF16, name=tag + "_dw"))
        return dx

    return y, backward


def _gated_linear(x, w, tag):
    traffic, key = w
    weight = traffic.take(key)
    seconds = 2.0 * x.shape[0] * x.shape[1] * weight.shape[1] / traffic.MATMUL_FLOPS
    outs = traffic.run(seconds, lambda cargo: _gu_matmul(x, weight, name=tag + "_fwd", cargo=cargo))

    def backward(d_gu):
        dx = traffic.matmul(d_gu, weight, tb=True, name=tag + "_dx")
        traffic.send_gradient(key, traffic.matmul(x, d_gu, ta=True, out_dtype=BF16, name=tag + "_dw"))
        return dx

    return tuple(outs), backward


def _gla_mixer(hn, p, tag):
    traffic, chunks = p["w_in"][0], hn.shape[0] // CHUNK
    w_a2 = jnp.pad(p["w_a2"], ((0, LANES - GLA_RANK), (0, 0)))
    b_a, norm_g = p["b_a"][None], p["norm_g"][None]
    proj, lin_in = _linear(hn, p["w_in"], tag + "_in", more_carriers_follow=True)
    o, sprev = traffic.run(chunks * _GLA_FWD_SECONDS, lambda cargo: _gla_core_fwd(proj, w_a2, b_a, norm_g, cargo))
    y, lin_out = _linear(o, p["w_out"], tag + "_out")

    def backward(dy):
        d_o = lin_out(dy)
        d_proj, d_wa, d_ba, d_ng = traffic.run(chunks * _GLA_BWD_SECONDS,
                                               lambda cargo: _gla_core_bwd(proj, sprev, d_o, w_a2, b_a, norm_g, cargo))
        return lin_in(d_proj), dict(w_a2=d_wa[:GLA_RANK], b_a=d_ba[0], norm_g=d_ng[0])

    return y, backward


def _ssd_mixer(hn, p, tag):
    pad = lambda a: jnp.pad(a[None], ((0, 0), (0, LANES - SSD_HEADS)))
    dt_bias, a_log, d_skip, norm_g = pad(p["dt_bias"]), pad(p["a_log"]), pad(p["d"]), p["norm_g"][None]
    traffic, chunks = p["w_in"][0], hn.shape[0] // CHUNK
    proj, lin_in = _linear(hn, p["w_in"], tag + "_in", more_carriers_follow=True)
    xbc = _ssd_conv_fwd(proj, p["conv_w"], p["conv_b"], tag + "_conv")
    o, hprev = traffic.run(chunks * _SSD_FWD_SECONDS,
                           lambda cargo: _ssd_core_fwd(proj, xbc, dt_bias, a_log, d_skip, norm_g, cargo))
    y, lin_out = _linear(o, p["w_out"], tag + "_out")

    def backward(dy):
        d_o = lin_out(dy)
        d_z, d_xbc, d_dt, d_db, d_al, d_ds, d_ng = traffic.run(
            chunks * _SSD_BWD_SECONDS, lambda cargo: _ssd_core_bwd(proj, xbc, hprev, d_o, dt_bias, a_log, d_skip, norm_g, cargo))
        d_pre, d_cw, d_cb = _ssd_conv_bwd(proj, d_xbc, p["conv_w"], p["conv_b"], tag + "_conv_bwd")
        d_hn = lin_in(jnp.concatenate([d_z, d_pre, d_dt], axis=1))
        return d_hn, dict(conv_w=d_cw, conv_b=d_cb, dt_bias=d_db[0, :SSD_HEADS], a_log=d_al[0, :SSD_HEADS],
                          d=d_ds[0, :SSD_HEADS], norm_g=d_ng[0])

    return y, backward


def _s5_mixer(hn, p, tag):
    seq = hn.shape[0]
    t, n_chunks = S5_CHUNK, hn.shape[0] // S5_CHUNK
    names = ("log_dt", "a_re", "a_im", "b_re", "b_im", "c_re", "c_im")
    (kern, w_z, w_y, lam_re, lam_im), ops_vjp = jax.vjp(_s5_operators, *[p[k] for k in names])
    toep, toep_bwd = _s5_toeplitz(kern, tag)
    to_groups = lambda a: a.reshape(n_chunks, t, S5_GROUPS, S5_GROUP).transpose(2, 0, 1, 3).reshape(S5_GROUPS, n_chunks, t * S5_GROUP)
    from_groups = lambda a: a.reshape(S5_GROUPS, n_chunks, t, S5_GROUP).transpose(1, 2, 0, 3).reshape(seq, D_MODEL)
    ug = to_groups(hn)
    z = _matmul(ug, w_z, name=tag + "_z")
    x_before = _s5_boundary_scan(z.transpose(1, 0, 2), lam_re, lam_im, tag + "_scan")
    xprev = x_before.transpose(1, 0, 2)
    tw = t * S5_GROUP
    ux = jnp.concatenate([ug, xprev.astype(BF16)], axis=2)
    yg = _matmul(ux, jnp.concatenate([toep, w_y.astype(BF16)], axis=1), out_dtype=BF16, name=tag + "_y")
    y = from_groups(yg)
    vg, lin_glu = _linear(_s5_gate_fwd(y, hn, p["d"], tag + "_gate"), p["w_glu"], tag + "_glu", dx_dtype=BF16)
    out = _glu_fwd(vg, tag + "_glu_gate")

    def backward(dy):
        d_act = lin_glu(_glu_bwd(vg, dy, tag + "_glu_gate_bwd"))
        d_y, d_u, d_d = _s5_gate_bwd(y, hn, p["d"], d_act, tag + "_gate_bwd")
        d_yg = to_groups(d_y)
        d_xprev = _matmul(d_yg, w_y, tb=True, name=tag + "_inter_dx").transpose(1, 0, 2)
        d_wy = _matmul(xprev, d_yg, ta=True, name=tag + "_inter_dw")
        dz = _s5_boundary_scan(d_xprev, lam_re, -lam_im, tag + "_scan_bwd", reverse=True)
        x_re, x_im, dz_re, dz_im = (x_before[..., :S5_STATE], x_before[..., S5_STATE:], dz[..., :S5_STATE],
                                    dz[..., S5_STATE:])
        d_lam_re = jnp.sum(x_re * dz_re + x_im * dz_im, axis=0)
        d_lam_im = jnp.sum(x_re * dz_im - x_im * dz_re, axis=0)
        dyz = jnp.concatenate([d_yg, dz.transpose(1, 0, 2).astype(BF16)], axis=2)
        d_ug = _matmul(dyz, jnp.concatenate([toep, w_z.astype(BF16)], axis=2), tb=True, out_dtype=BF16, name=tag + "_du")
        d_ops = _matmul(ug, dyz, ta=True, name=tag + "_dw")
        grads = dict(zip(names, ops_vjp((toep_bwd(d_ops[..., :tw]), d_ops[..., tw:], d_wy, d_lam_re, d_lam_im))))
        grads.update(d=d_d)
        return (d_u, from_groups(d_ug)), grads

    return out, backward


_SMALL =[("gla_w_a2", 2), ("gla_b_a", 1), ("gla_norm_g", 1), ("ssd_conv_w", 2), ("s5_d", 1)]
_REPLICATED = ["norm_mix_g", "norm_ffn_g", "ssd_conv_b", "ssd_dt_bias", "ssd_a_log", "ssd_d", "ssd_norm_g", "s5_log_dt",
               "s5_a_re", "s5_a_im", "s5_b_re", "s5_b_im", "s5_c_re", "s5_c_im", "final_norm_g"]
_EARLY_SMALL = [n for n in [s for s, _ in _SMALL] + _REPLICATED if n.startswith("s5_")]
_WEIGHTS = ['norm_mix_g', 'norm_ffn_g', 'gla_w_in', 'gla_w_a2', 'gla_b_a', 'gla_norm_g', 'gla_w_out', 'ssd_w_in',
            'ssd_conv_w', 'ssd_conv_b', 'ssd_dt_bias', 'ssd_a_log', 'ssd_d', 'ssd_norm_g', 'ssd_w_out', 's5_log_dt',
            's5_a_re', 's5_a_im', 's5_b_re', 's5_b_im', 's5_c_re', 's5_c_im', 's5_d', 's5_w_glu', 'ffn_w_gu', 'ffn_w_down',
            'final_norm_g']


def _gather_small(local):
    shapes = [local[n].shape for n, _ in _SMALL]
    blocks = _collective("gather", _pack([local[n] for n, _ in _SMALL]), "gather_vectors")
    parts = _unpack(blocks, shapes, lead=(N_DEV,))
    return {n: _join(part, axis) for (n, axis), part in zip(_SMALL, parts)}


def _forward_plan():
    plan = []
    for i in range(DEPTH):
        j = i // 3
        plan += [[("gla_w_in", j), ("gla_w_out", j)], [("ssd_w_in", j), ("ssd_w_out", j)], [("s5_w_glu", j)]][i % 3]
        plan += [("ffn_w_gu", i), ("ffn_w_down", i)]
    return plan


def _forward_backward(x, target, w, traffic):
    big = lambda name, j: (traffic, (name, j))
    gla = lambda j: dict(w_in=big("gla_w_in", j), w_a2=w["gla_w_a2"][j], b_a=w["gla_b_a"][j], norm_g=w["gla_norm_g"][j],
                         w_out=big("gla_w_out", j))
    ssd = lambda j: dict(w_in=big("ssd_w_in", j), conv_w=w["ssd_conv_w"][j], conv_b=w["ssd_conv_b"][j],
                         dt_bias=w["ssd_dt_bias"][j], a_log=w["ssd_a_log"][j], d=w["ssd_d"][j], norm_g=w["ssd_norm_g"][j],
                         w_out=big("ssd_w_out", j))
    s5 = lambda j: dict(log_dt=w["s5_log_dt"][j], a_re=w["s5_a_re"][j], a_im=w["s5_a_im"][j], b_re=w["s5_b_re"][j],
                        b_im=w["s5_b_im"][j], c_re=w["s5_c_re"][j], c_im=w["s5_c_im"][j], d=w["s5_d"][j],
                        w_glu=big("s5_w_glu", j))
    mixers = [("gla", _gla_mixer, gla), ("ssd", _ssd_mixer, ssd), ("s5", _s5_mixer, s5)]
    base, delta = x, None
    tape = []
    for i in range(DEPTH):
        kind, mixer, params = mixers[i % 3]
        j = i // 3
        h, hn = _add_norm_fwd(base, delta, w["norm_mix_g"][i], f"l{i}_norm_mix")
        y, mixer_bwd = mixer(hn, params(j), f"l{i}_{kind}")
        h_mid, hn2 = _add_norm_fwd(h, y, w["norm_ffn_g"][i], f"l{i}_norm_ffn")
        (g, u, act), gu_bwd = _gated_linear(hn2, big("ffn_w_gu", i), f"l{i}_ffn_gu")
        delta, down_bwd = _linear(act, big("ffn_w_down", i), f"l{i}_ffn_down", dx_dtype=BF16)
        base = h_mid
        tape.append((kind, j, h, mixer_bwd, h_mid, gu_bwd, (g, u), down_bwd))
    loss, d_h, d_final_g = _loss_head(base + delta, w["final_norm_g"], target, "loss_head")

    grads = {n: [None] * w[n].shape[0] for n in w if n != "final_norm_g"}
    grads["final_norm_g"] = d_final_g
    for i in reversed(range(DEPTH)):
        kind, j, h, mixer_bwd, h_mid, gu_bwd, (g, u), down_bwd = tape[i]
        d_gu = _swiglu_bwd(g, u, down_bwd(d_h), f"l{i}_swiglu_bwd")
        d_mid, grads["norm_ffn_g"][i] = _norm_bwd(h_mid, w["norm_ffn_g"][i], gu_bwd(d_gu), d_h, f"l{i}_norm_ffn_bwd")
        d_hn, mixer_grads = mixer_bwd(d_mid)
        for k, g in mixer_grads.items():
            grads[f"{kind}_{k}"][j] = g
        if kind == "s5" and all(g is not None for n in _EARLY_SMALL for g in grads[n]):
            traffic.gather_early(_pack([jnp.stack(grads[n]) for n in _EARLY_SMALL]))
        d_h, grads["norm_mix_g"][i] = _norm_bwd(h, w["norm_mix_g"][i], d_hn, d_mid, f"l{i}_norm_mix_bwd")
    return loss, d_h, grads


def kernel(x, norm_mix_g, norm_ffn_g, gla_w_in, gla_w_a2, gla_b_a, gla_norm_g, gla_w_out, ssd_w_in, ssd_conv_w, ssd_conv_b, ssd_dt_bias, ssd_a_log, ssd_d, ssd_norm_g, ssd_w_out, s5_log_dt, s5_a_re, s5_a_im, s5_b_re, s5_b_im, s5_c_re, s5_c_im, s5_d, s5_w_glu, ffn_w_gu, ffn_w_down, final_norm_g, loss_target, m_norm_mix_g, m_norm_ffn_g, m_gla_w_in, m_gla_w_a2, m_gla_b_a, m_gla_norm_g, m_gla_w_out, m_ssd_w_in, m_ssd_conv_w, m_ssd_conv_b, m_ssd_dt_bias, m_ssd_a_log, m_ssd_d, m_ssd_norm_g, m_ssd_w_out, m_s5_log_dt, m_s5_a_re, m_s5_a_im, m_s5_b_re, m_s5_b_im, m_s5_c_re, m_s5_c_im, m_s5_d, m_s5_w_glu, m_ffn_w_gu, m_ffn_w_down, m_final_norm_g, v_norm_mix_g, v_norm_ffn_g, v_gla_w_in, v_gla_w_a2, v_gla_b_a, v_gla_norm_g, v_gla_w_out, v_ssd_w_in, v_ssd_conv_w, v_ssd_conv_b, v_ssd_dt_bias, v_ssd_a_log, v_ssd_d, v_ssd_norm_g, v_ssd_w_out, v_s5_log_dt, v_s5_a_re, v_s5_a_im, v_s5_b_re, v_s5_b_im, v_s5_c_re, v_s5_c_im, v_s5_d, v_s5_w_glu, v_ffn_w_gu, v_ffn_w_down, v_final_norm_g):
    args = locals()
    local = {n: args[n] for n in _WEIGHTS}
    moment_m = {n: args["m_" + n] for n in _WEIGHTS}
    moment_v = {n: args["v_" + n] for n in _WEIGHTS}

    shards = {(n, layer): local[n][layer].astype(BF16) for n, _ in _BIG for layer in range(local[n].shape[0])}
    traffic = _Traffic(shards, _forward_plan())
    full = {n: local[n] for n in _REPLICATED}
    full.update(_gather_small(local))

    loss, d_x, grads = _forward_backward(x[0], loss_target[0], full, traffic)
    traffic.flush()
    loss = lax.psum(loss, ("x", "y", "c"))
    kinds = ("grad", "delta", "new_m", "new_v")
    out = {}

    for n, _ in _BIG:
        parts = jnp.stack([traffic.received[(n, layer)] for layer in range(local[n].shape[0])], axis=1)
        results = _adamw(parts, local[n], moment_m[n], moment_v[n], "adamw_" + n)
        out.update({f"{kind}_{n}": a for kind, a in zip(kinds, results)})

    small = [n for n, _ in _SMALL] + _REPLICATED
    stacked = lambda n: grads[n] if n == "final_norm_g" else jnp.stack(grads[n])
    late = [n for n in small if n not in _EARLY_SMALL]
    gathered = [(_EARLY_SMALL, traffic.early, "early"),
                (late, _collective("gather", _pack([stacked(n) for n in late]), "gather_small_gradients"), "late")]
    summed = {}
    for names, parts, tag in gathered:
        sums = _unpack(_sum_parts(parts, "sum_small_gradients_" + tag), [stacked(n).shape for n in names])
        summed.update(zip(names, sums))
    position = _index(_mesh_position())
    mine = [_own_shard(summed[n], axis, position) for n, axis in _SMALL] + [summed[n] for n in _REPLICATED]
    shapes = [local[n].shape for n in small]
    pk = lambda arrays: _pack(arrays)[None]
    results = _adamw(pk(mine)[None], pk([local[n] for n in small]), pk([moment_m[n] for n in small]),
                     pk([moment_v[n] for n in small]), "adamw_small")
    for kind, flat in zip(kinds, results):
        out.update({f"{kind}_{n}": a for n, a in zip(small, _unpack(flat[0], shapes))})

    return (loss, d_x[None], *[out[f"{kind}_{n}"] for kind in ("grad", "delta", "new_m", "new_v") for n in _WEIGHTS])
```

```python
import functools
import math

import jax
import jax.numpy as jnp
import numpy as np
from jax import lax
from jax.experimental import pallas as pl
from jax.experimental.pallas import tpu as pltpu

F32 = jnp.float32
BF16 = jnp.bfloat16
_MXU_DTYPE = jnp.bfloat16

N_DEV = 8
D_MODEL = 1024
DEPTH = 4
CHUNK = 64
GLA_STEP_CHUNKS, SSD_STEP_CHUNKS = 2, 1
GLA_STEP, SSD_STEP = CHUNK * GLA_STEP_CHUNKS, CHUNK * SSD_STEP_CHUNKS
EPS = 1e-6
GLA_HEADS, GLA_DK, GLA_DV, GLA_RANK, GLA_TAU = 4, 128, 256, 16, 16.0
GLA_QK = GLA_HEADS * GLA_DK
GLA_VD = GLA_HEADS * GLA_DV
LANES = 128
GLA_IN = 2 * GLA_QK + 2 * GLA_VD + GLA_RANK
GLA_PROJ = 2 * GLA_QK + 2 * GLA_VD + LANES
SSD_DINNER, SSD_HEADDIM, SSD_HEADS, SSD_GROUPS, SSD_HPG, SSD_DSTATE, SSD_CONV = 2048, 64, 32, 8, 4, 128, 4
SSD_GN = SSD_GROUPS * SSD_DSTATE
SSD_GW = SSD_HPG * SSD_HEADDIM
SSD_XBC = SSD_DINNER + 2 * SSD_GN
SSD_IN = SSD_DINNER + SSD_XBC + SSD_HEADS
SSD_PROJ = SSD_DINNER + SSD_XBC + LANES
S5_GROUP, S5_GROUPS, S5_STATE = 16, 64, 64
S5_CHUNK = 16
FFN_HIDDEN = 2816
ADAM_LR, ADAM_B1, ADAM_B2, ADAM_EPS, ADAM_WD, ADAM_STEP = 0.001, 0.9, 0.999, 1e-08, 0.01, 10
VMEM_LIMIT = 48 * 1024 * 1024
FLAT_COLS = 1024
FLAT_ROWS_ALIGN = 64


def _tile(n, cap, unit):
    if n <= cap:
        return n
    best = None
    for t in range(unit, cap + 1, unit):
        if n % t == 0:
            best = t
    assert best is not None, (n, cap, unit)
    return best


def _divisors(n, unit):
    return sorted({t for t in range(unit, n + 1, unit) if n % t == 0} | {n})


_MXU_FLOPS, _HBM_BYTES, _ACC_BYTES, _STEP_SECONDS = 1.1e15, 3e12, 1.1e13, 3.5e-7
_MXU_ROWS = 256
_TILE_VMEM_BUDGET = 36 * 1024 * 1024
_BATCH_VMEM_BUDGET = 16 * 1024 * 1024


def _pick_tiles(m, n, k, a_bytes, b_bytes, o_bytes, m_unit):
    best = None
    for tm in _divisors(m, m_unit):
        for tn in _divisors(n, LANES):
            for tk in _divisors(k, LANES):
                nk = k // tk
                vmem = 2 * tm * tk * a_bytes + 2 * tk * tn * b_bytes + 2 * tm * tn * o_bytes + (nk > 1) * tm * tn * 4
                if vmem > _TILE_VMEM_BUDGET or tm > 2048 or tn > 2048:
                    continue
                a_reads = n // tn if nk > 1 else 1
                b_reads = 1 if (nk == 1 and n == tn) else m // tm
                traffic = m * k * a_bytes * a_reads + k * n * b_bytes * b_reads + m * n * o_bytes
                mxu = 2.0 * m * n * k / _MXU_FLOPS * (1.0 + _MXU_ROWS / tm)
                cost = (max(mxu, traffic / _HBM_BYTES) + (nk > 1) * nk * m * n * 8 / _ACC_BYTES
                        + (m // tm) * (n // tn) * nk * _STEP_SECONDS)
                if best is None or cost < best[0]:
                    best = (cost, tm, tn, tk)
    assert best is not None, (m, n, k)
    return best[1:]


def _cargo_call(body, cargo, *, name, grid, in_specs, out_specs, out_shape, scratch_shapes, semantics):
    params = lambda sem: pltpu.CompilerParams(dimension_semantics=sem, vmem_limit_bytes=VMEM_LIMIT)
    if not cargo:
        return pl.pallas_call(body, name=name, grid=grid, in_specs=in_specs, out_specs=out_specs, out_shape=out_shape,
                              scratch_shapes=scratch_shapes, compiler_params=params(semantics))
    n_in, n_out, n_scratch, n_cargo = len(in_specs), len(out_specs), len(scratch_shapes), len(cargo)

    def loaded(*refs):
        ins, cargo_in, rest = refs[:n_in], refs[n_in:n_in + n_cargo], refs[n_in + n_cargo:]
        outs, cargo_out, rest = rest[:n_out], rest[n_out:n_out + n_cargo], rest[n_out + n_cargo:]
        scratch, sems = rest[:n_scratch], rest[n_scratch:]
        ids = [pl.program_id(d) for d in range(len(grid))]
        first = functools.reduce(jnp.logical_and, [i == 0 for i in ids])
        last = functools.reduce(jnp.logical_and, [i == g - 1 for i, g in zip(ids, grid)])
        moves = lambda: [_moves(kind, x_ref, y_ref, *sems[3 * c:3 * c + 3])
                         for c, ((kind, _), x_ref, y_ref) in enumerate(zip(cargo, cargo_in, cargo_out))]

        @pl.when(first)
        def _():
            for mv in moves():
                _start(mv)

        body(*ins, *outs, *scratch)

        @pl.when(last)
        def _():
            for mv in moves():
                _finish(mv)

    sems = [pltpu.SemaphoreType.DMA((N_DEV - 1,)), pltpu.SemaphoreType.DMA((N_DEV - 1,)), pltpu.SemaphoreType.DMA] * n_cargo
    call = pl.pallas_call(
        loaded, name=name, grid=grid, in_specs=list(in_specs) + [_ANY] * n_cargo,
        out_specs=list(out_specs) + [_ANY] * n_cargo,
        out_shape=list(out_shape) + [_moved_shape(kind, x) for kind, x in cargo],
        scratch_shapes=list(scratch_shapes) + sems, compiler_params=params(("arbitrary",) * len(grid)))

    def run(*args):
        results = call(*args, *[x for _, x in cargo])
        return list(results[:n_out]), list(results[n_out:])

    return run


def _matmul(a, b, *, ta=False, tb=False, out_dtype=F32, name, cargo=()):
    batched = a.ndim == 3
    if ta:
        k_dim, m_dim = a.shape[-2:]
    else:
        m_dim, k_dim = a.shape[-2:]
    if tb:
        n_dim, kb = b.shape[-2:]
    else:
        kb, n_dim = b.shape[-2:]
    assert kb == k_dim, (a.shape, b.shape, ta, tb)
    tm, tn, tk = _pick_tiles(m_dim, n_dim, k_dim, a.dtype.itemsize, b.dtype.itemsize, jnp.dtype(out_dtype).itemsize,
                             LANES if ta else 16)
    nk = k_dim // tk
    ca, cb = (0 if ta else 1), (1 if tb else 0)
    grid = (m_dim // tm, n_dim // tn, nk)
    gb = 1
    if batched:
        step_bytes = 2 * (tm * tk * a.dtype.itemsize + tk * tn * b.dtype.itemsize + tm * tn * jnp.dtype(out_dtype).itemsize)
        gb = max(g for g in _divisors(a.shape[0], 1) if g * step_bytes <= _BATCH_VMEM_BUDGET or g == 1)
        grid = (a.shape[0] // gb,) + grid
    dims = (((ca + 1,), (cb + 1,)), ((0,), (0,))) if batched else (((ca,), (cb,)), ((), ()))

    def body(a_ref, b_ref, o_ref, *acc):
        part = lax.dot_general(a_ref[...].astype(_MXU_DTYPE), b_ref[...].astype(_MXU_DTYPE), dims,
                               preferred_element_type=F32)
        if nk == 1:
            o_ref[...] = part.astype(o_ref.dtype)
            return
        acc_ref, = acc
        k = pl.program_id(len(grid) - 1)

        @pl.when(k == 0)
        def _():
            acc_ref[...] = part

        @pl.when(k > 0)
        def _():
            acc_ref[...] += part

        @pl.when(k == nk - 1)
        def _():
            o_ref[...] = acc_ref[...].astype(o_ref.dtype)

    def spec(shape, fn):
        if batched:
            return pl.BlockSpec((gb,) + shape, lambda g, i, j, k: (g,) + fn(i, j, k))
        return pl.BlockSpec(shape, fn)

    a_spec = spec((tk, tm), lambda i, j, k: (k, i)) if ta else spec((tm, tk), lambda i, j, k: (i, k))
    b_spec = spec((tn, tk), lambda i, j, k: (j, k)) if tb else spec((tk, tn), lambda i, j, k: (k, j))
    o_spec = spec((tm, tn), lambda i, j, k: (i, j))
    out_shape = ((a.shape[0],) if batched else ()) + (m_dim, n_dim)
    call = _cargo_call(
        body, cargo, name=name, grid=grid, in_specs=[a_spec, b_spec], out_specs=[o_spec],
        out_shape=[jax.ShapeDtypeStruct(out_shape, out_dtype)],
        scratch_shapes=[pltpu.VMEM(((gb,) if batched else ()) + (tm, tn), F32)] if nk > 1 else [],
        semantics=("parallel",) * (len(grid) - 1) + ("arbitrary",))
    if not cargo:
        return call(a, b)[0]
    results, moved = call(a, b)
    return results[0], moved


def _dot(a, b, ca=1, cb=0, exact=False):
    if exact:
        return lax.dot_general(a, b, (((ca,), (cb,)), ((), ())), precision=lax.Precision.HIGHEST,
                               preferred_element_type=F32)
    return lax.dot_general(a.astype(_MXU_DTYPE), b.astype(_MXU_DTYPE), (((ca,), (cb,)), ((), ())),
                           preferred_element_type=F32)


def _tri(n):
    return lax.broadcasted_iota(jnp.int32, (n, n), 0) >= lax.broadcasted_iota(jnp.int32, (n, n), 1)


def _log_sigmoid(x):
    return jnp.minimum(x, 0.0) - jnp.log(1.0 + jnp.exp(-jnp.abs(x)))


def _softplus(x):
    return jnp.maximum(x, 0.0) + jnp.log(1.0 + jnp.exp(-jnp.abs(x)))


def _silu(x):
    return x / (1.0 + jnp.exp(-x))


def _full_spec(shape):
    return pl.BlockSpec(shape, lambda c: (0,) * len(shape))


def _gla_chunk(proj, st, w_a2, b_a, norm_g):
    t = proj.shape[0]
    q = proj[:, 0:GLA_QK] * (GLA_DK ** -0.5)
    k = proj[:, GLA_QK:2 * GLA_QK]
    v = proj[:, 2 * GLA_QK:2 * GLA_QK + GLA_VD]
    r = proj[:, 2 * GLA_QK + GLA_VD:2 * GLA_QK + 2 * GLA_VD]
    a_low = proj[:, 2 * GLA_QK + 2 * GLA_VD:]
    log_a = _log_sigmoid(_dot(a_low, w_a2) + b_a) * (1.0 / GLA_TAU)
    past = _tri(t)
    lc = _dot(past.astype(F32), log_a, exact=True)
    lend = lc[t - 1:t, :]
    e_pos = jnp.exp(lc)
    e_neg = jnp.exp(-lc)
    q_fwd, k_fwd, q_bwd, k_bwd = q * e_pos, k * e_neg, q * e_neg, k * e_pos
    kd = k * jnp.exp(lend - lc)
    g = jnp.exp(lend)
    outs, new_st = [], []
    for h in range(GLA_HEADS):
        sk = slice(h * GLA_DK, (h + 1) * GLA_DK)
        sv = slice(h * GLA_DV, (h + 1) * GLA_DV)
        s_past = _dot(q_fwd[:, sk], k_fwd[:, sk], 1, 1)
        s_future = _dot(q_bwd[:, sk], k_bwd[:, sk], 1, 1)
        scores = jnp.where(past, s_past, s_future)
        o = _dot(scores, v[:, sv]) + _dot(q_fwd[:, sk], st[h], 1, 1)
        new_st.append(st[h] * g[:, sk] + _dot(v[:, sv], kd[:, sk], 0, 0))
        o = o * lax.rsqrt(jnp.mean(o * o, axis=-1, keepdims=True) + EPS) * norm_g[:, sv]
        outs.append(o)
    return jnp.concatenate(outs, axis=1) * _silu(r), tuple(new_st)


_GLA_STATE = (GLA_HEADS, GLA_DV, GLA_DK)


def _gla_step(proj, st, w_a2, b_a, norm_g):
    outs = []
    for s in range(GLA_STEP_CHUNKS):
        out, st = _gla_chunk(proj[s * CHUNK:(s + 1) * CHUNK], st, w_a2, b_a, norm_g)
        outs.append(out)
    return jnp.concatenate(outs, axis=0), st


def _gla_core_fwd(proj, w_a2, b_a, norm_g, cargo=()):
    seq = proj.shape[0]
    nc = seq // GLA_STEP

    def body(proj_ref, wa_ref, ba_ref, ng_ref, o_ref, sprev_ref, st_ref):
        @pl.when(pl.program_id(0) == 0)
        def _():
            st_ref[...] = jnp.zeros_like(st_ref)

        st = tuple(st_ref[h] for h in range(GLA_HEADS))
        for h in range(GLA_HEADS):
            sprev_ref[0, h] = st[h]
        out, new_st = _gla_step(proj_ref[...], st, wa_ref[...], ba_ref[...], ng_ref[...])
        o_ref[...] = out
        for h in range(GLA_HEADS):
            st_ref[h] = new_st[h]

    return _cargo_call(
        body, cargo, name="gla_core_fwd", grid=(nc,),
        in_specs=[pl.BlockSpec((GLA_STEP, GLA_PROJ), lambda c: (c, 0)), _full_spec(w_a2.shape), _full_spec(b_a.shape),
                  _full_spec(norm_g.shape)],
        out_specs=[pl.BlockSpec((GLA_STEP, GLA_VD), lambda c: (c, 0)), pl.BlockSpec((1,) + _GLA_STATE, lambda c: (c, 0, 0, 0))],
        out_shape=[jax.ShapeDtypeStruct((seq, GLA_VD), F32), jax.ShapeDtypeStruct((nc,) + _GLA_STATE, F32)],
        scratch_shapes=[pltpu.VMEM(_GLA_STATE, F32)],
        semantics=("arbitrary",),
    )(proj, w_a2, b_a, norm_g)


def _gla_core_bwd(proj, sprev, d_out, w_a2, b_a, norm_g, cargo=()):
    seq = proj.shape[0]
    nc = seq // GLA_STEP

    def body(proj_ref, sprev_ref, do_ref, wa_ref, ba_ref, ng_ref, dproj_ref, dwa_ref, dba_ref, dng_ref, dst_ref):
        @pl.when(pl.program_id(0) == 0)
        def _():
            dst_ref[...] = jnp.zeros_like(dst_ref)
            dwa_ref[...] = jnp.zeros_like(dwa_ref)
            dba_ref[...] = jnp.zeros_like(dba_ref)
            dng_ref[...] = jnp.zeros_like(dng_ref)

        st = tuple(sprev_ref[0, h] for h in range(GLA_HEADS))
        _, vjp = jax.vjp(_gla_step, proj_ref[...], st, wa_ref[...], ba_ref[...], ng_ref[...])
        d_next = tuple(dst_ref[h] for h in range(GLA_HEADS))
        d_proj, d_st, d_wa, d_ba, d_ng = vjp((do_ref[...], d_next))
        dproj_ref[...] = d_proj.astype(dproj_ref.dtype)
        for h in range(GLA_HEADS):
            dst_ref[h] = d_st[h]
        dwa_ref[...] += d_wa
        dba_ref[...] += d_ba
        dng_ref[...] += d_ng

    rev = lambda c: (nc - 1 - c, 0)
    return _cargo_call(
        body, cargo, name="gla_core_bwd", grid=(nc,),
        in_specs=[pl.BlockSpec((GLA_STEP, GLA_PROJ), rev), pl.BlockSpec((1,) + _GLA_STATE, lambda c: (nc - 1 - c, 0, 0, 0)),
                  pl.BlockSpec((GLA_STEP, GLA_VD), rev), _full_spec(w_a2.shape), _full_spec(b_a.shape), _full_spec(norm_g.shape)],
        out_specs=[pl.BlockSpec((GLA_STEP, GLA_PROJ), rev), _full_spec(w_a2.shape), _full_spec(b_a.shape), _full_spec(norm_g.shape)],
        out_shape=[jax.ShapeDtypeStruct((seq, GLA_PROJ), BF16), jax.ShapeDtypeStruct(w_a2.shape, F32),
                   jax.ShapeDtypeStruct(b_a.shape, F32), jax.ShapeDtypeStruct(norm_g.shape, F32)],
        scratch_shapes=[pltpu.VMEM(_GLA_STATE, F32)],
        semantics=("arbitrary",),
    )(proj, sprev, d_out, w_a2, b_a, norm_g)


def _ssd_chunk(z, xbc, dt_raw, hs, dt_bias, a_log, d_skip, norm_g):
    t = z.shape[0]
    xs = xbc[:, :SSD_DINNER]
    bm = xbc[:, SSD_DINNER:SSD_DINNER + SSD_GN]
    cm = xbc[:, SSD_DINNER + SSD_GN:]
    dt = _softplus(dt_raw + dt_bias)
    da = dt * (-jnp.exp(a_log))
    tri = _tri(t).astype(F32)
    eye = (lax.broadcasted_iota(jnp.int32, (t, t), 0) == lax.broadcasted_iota(jnp.int32, (t, t), 1)).astype(F32)
    cum = _dot(tri, da, exact=True)
    cum_t = _dot(da, tri, 0, 1, exact=True)
    dt_t = _dot(dt, eye, 0, 0, exact=True)
    cum_end = cum[t - 1:t, :]
    w_state = dt * jnp.exp(cum_end - cum)
    e_cum = jnp.exp(cum)
    g_end = jnp.exp(cum_end)
    head_of = lambda axis: lax.shift_right_logical(lax.broadcasted_iota(jnp.int32, (SSD_GW, SSD_GW), axis),
                                                   jnp.int32(SSD_HEADDIM.bit_length() - 1))
    same_head = head_of(0) == head_of(1)
    ys, new_hs = [], []
    for g in range(SSD_GROUPS):
        heads = range(g * SSD_HPG, (g + 1) * SSD_HPG)
        cols = slice(g * SSD_GW, (g + 1) * SSD_GW)

        def spread(a):
            return jnp.concatenate([jnp.broadcast_to(a[:, h:h + 1], (a.shape[0], SSD_HEADDIM)) for h in heads], axis=1)

        def row(a_t):
            return jnp.concatenate([a_t[h:h + 1, :] for h in heads], axis=1)

        bm_g = bm[:, g * SSD_DSTATE:(g + 1) * SSD_DSTATE]
        cm_g = cm[:, g * SSD_DSTATE:(g + 1) * SSD_DSTATE]
        xs_g = xs[:, cols]
        cb = _dot(cm_g, jnp.concatenate([bm_g] * SSD_HPG, axis=0), 1, 1)
        mix = cb * jnp.exp(-jnp.abs(spread(cum) - row(cum_t))) * row(dt_t)
        x_diag = jnp.where(same_head, jnp.concatenate([xs_g] * SSD_HPG, axis=0), 0.0)
        y = _dot(mix, x_diag)
        y = y + _dot(cm_g, hs[g], 1, 1) * spread(e_cum)
        y = y + spread(d_skip) * xs_g
        states = _dot(xs_g * spread(w_state), bm_g, 0, 0)
        decayed = jnp.concatenate([g_end[:, h:h + 1] * hs[g][j * SSD_HEADDIM:(j + 1) * SSD_HEADDIM, :]
                                   for j, h in enumerate(heads)], axis=0)
        new_hs.append(decayed + states)
        yg = y * _silu(z[:, cols])
        ys.append(yg * lax.rsqrt(jnp.mean(yg * yg, axis=-1, keepdims=True) + EPS) * norm_g[:, cols])
    return jnp.concatenate(ys, axis=1), tuple(new_hs)


_SSD_STATE = (SSD_GROUPS, SSD_GW, SSD_DSTATE)


def _ssd_step(z, xbc, dt_raw, hs, dt_bias, a_log, d_skip, norm_g):
    outs = []
    for s in range(SSD_STEP_CHUNKS):
        rows = slice(s * CHUNK, (s + 1) * CHUNK)
        out, hs = _ssd_chunk(z[rows], xbc[rows], dt_raw[rows], hs, dt_bias, a_log, d_skip, norm_g)
        outs.append(out)
    return jnp.concatenate(outs, axis=0), hs
_SSD_DT_BLOCK = (SSD_DINNER + SSD_XBC) // LANES


def _ssd_core_fwd(proj, xbc, dt_bias, a_log, d_skip, norm_g, cargo=()):
    seq = proj.shape[0]
    nc = seq // SSD_STEP

    def body(z_ref, xbc_ref, dt_ref, db_ref, al_ref, ds_ref, ng_ref, o_ref, hprev_ref, hs_ref):
        @pl.when(pl.program_id(0) == 0)
        def _():
            hs_ref[...] = jnp.zeros_like(hs_ref)

        hs = tuple(hs_ref[g] for g in range(SSD_GROUPS))
        for g in range(SSD_GROUPS):
            hprev_ref[0, g] = hs[g]
        out, new_hs = _ssd_step(z_ref[...], xbc_ref[...], dt_ref[...], hs, db_ref[...], al_ref[...], ds_ref[...], ng_ref[...])
        o_ref[...] = out
        for g in range(SSD_GROUPS):
            hs_ref[g] = new_hs[g]

    return _cargo_call(
        body, cargo, name="ssd_core_fwd", grid=(nc,),
        in_specs=[pl.BlockSpec((SSD_STEP, SSD_DINNER), lambda c: (c, 0)), pl.BlockSpec((SSD_STEP, SSD_XBC), lambda c: (c, 0)),
                  pl.BlockSpec((SSD_STEP, LANES), lambda c: (c, _SSD_DT_BLOCK)),
                  _full_spec(dt_bias.shape), _full_spec(a_log.shape), _full_spec(d_skip.shape), _full_spec(norm_g.shape)],
        out_specs=[pl.BlockSpec((SSD_STEP, SSD_DINNER), lambda c: (c, 0)), pl.BlockSpec((1,) + _SSD_STATE, lambda c: (c, 0, 0, 0))],
        out_shape=[jax.ShapeDtypeStruct((seq, SSD_DINNER), F32), jax.ShapeDtypeStruct((nc,) + _SSD_STATE, F32)],
        scratch_shapes=[pltpu.VMEM(_SSD_STATE, F32)],
        semantics=("arbitrary",),
    )(proj, xbc, proj, dt_bias, a_log, d_skip, norm_g)


def _ssd_core_bwd(proj, xbc, hprev, d_out, dt_bias, a_log, d_skip, norm_g, cargo=()):
    seq = proj.shape[0]
    nc = seq // SSD_STEP

    def body(z_ref, xbc_ref, dt_ref, hprev_ref, do_ref, db_ref, al_ref, ds_ref, ng_ref,
             dz_ref, dxbc_ref, ddt_ref, ddb_ref, dal_ref, dds_ref, dng_ref, dhs_ref):
        @pl.when(pl.program_id(0) == 0)
        def _():
            dhs_ref[...] = jnp.zeros_like(dhs_ref)
            ddb_ref[...] = jnp.zeros_like(ddb_ref)
            dal_ref[...] = jnp.zeros_like(dal_ref)
            dds_ref[...] = jnp.zeros_like(dds_ref)
            dng_ref[...] = jnp.zeros_like(dng_ref)

        hs = tuple(hprev_ref[0, g] for g in range(SSD_GROUPS))
        _, vjp = jax.vjp(_ssd_step, z_ref[...], xbc_ref[...], dt_ref[...], hs, db_ref[...], al_ref[...], ds_ref[...], ng_ref[...])
        d_next = tuple(dhs_ref[g] for g in range(SSD_GROUPS))
        d_z, d_xbc, d_dt, d_hs, d_db, d_al, d_ds, d_ng = vjp((do_ref[...], d_next))
        dz_ref[...] = d_z.astype(dz_ref.dtype)
        dxbc_ref[...] = d_xbc
        ddt_ref[...] = d_dt.astype(ddt_ref.dtype)
        for g in range(SSD_GROUPS):
            dhs_ref[g] = d_hs[g]
        ddb_ref[...] += d_db
        dal_ref[...] += d_al
        dds_ref[...] += d_ds
        dng_ref[...] += d_ng

    rev = lambda c: (nc - 1 - c, 0)
    vec = [_full_spec(dt_bias.shape), _full_spec(a_log.shape), _full_spec(d_skip.shape), _full_spec(norm_g.shape)]
    return _cargo_call(
        body, cargo, name="ssd_core_bwd", grid=(nc,),
        in_specs=[pl.BlockSpec((SSD_STEP, SSD_DINNER), rev), pl.BlockSpec((SSD_STEP, SSD_XBC), rev),
                  pl.BlockSpec((SSD_STEP, LANES), lambda c: (nc - 1 - c, _SSD_DT_BLOCK)),
                  pl.BlockSpec((1,) + _SSD_STATE, lambda c: (nc - 1 - c, 0, 0, 0)),
                  pl.BlockSpec((SSD_STEP, SSD_DINNER), rev)] + vec,
        out_specs=[pl.BlockSpec((SSD_STEP, SSD_DINNER), rev), pl.BlockSpec((SSD_STEP, SSD_XBC), rev),
                   pl.BlockSpec((SSD_STEP, LANES), rev)] + vec,
        out_shape=[jax.ShapeDtypeStruct((seq, SSD_DINNER), BF16), jax.ShapeDtypeStruct((seq, SSD_XBC), F32),
                   jax.ShapeDtypeStruct((seq, LANES), BF16),
                   jax.ShapeDtypeStruct(dt_bias.shape, F32), jax.ShapeDtypeStruct(a_log.shape, F32),
                   jax.ShapeDtypeStruct(d_skip.shape, F32), jax.ShapeDtypeStruct(norm_g.shape, F32)],
        scratch_shapes=[pltpu.VMEM(_SSD_STATE, F32)],
        semantics=("arbitrary",),
    )(proj, xbc, proj, hprev, d_out, dt_bias, a_log, d_skip, norm_g)


CONV_COLS = 2048
CONV_HALO = 8


def _conv_taps(xx, rows):
    last = SSD_CONV - 1
    return [pltpu.roll(xx, last - k, 0)[CONV_HALO:CONV_HALO + rows] if k < last else xx[CONV_HALO:CONV_HALO + rows]
            for k in range(SSD_CONV)]


def _ssd_conv_fwd(proj, conv_w, conv_b, name):
    rows = proj.shape[0]
    tr = _tile(rows, 512, CONV_HALO)
    first_col = SSD_DINNER // CONV_COLS

    def body(x_ref, halo_ref, w_ref, b_ref, o_ref):
        halo = jnp.where(pl.program_id(0) == 0, 0.0, halo_ref[...])
        taps = _conv_taps(jnp.concatenate([halo, x_ref[...]], axis=0), tr)
        out = b_ref[...]
        for k in range(SSD_CONV):
            out = out + taps[k] * w_ref[k:k + 1, :]
        o_ref[...] = _silu(out)

    return pl.pallas_call(
        body, name=name, grid=(rows // tr, SSD_XBC // CONV_COLS),
        in_specs=[pl.BlockSpec((tr, CONV_COLS), lambda i, j: (i, first_col + j)),
                  pl.BlockSpec((CONV_HALO, CONV_COLS), lambda i, j: (jnp.maximum(i * (tr // CONV_HALO) - 1, 0), first_col + j)),
                  pl.BlockSpec((SSD_CONV, CONV_COLS), lambda i, j: (0, j)), pl.BlockSpec((1, CONV_COLS), lambda i, j: (0, j))],
        out_specs=pl.BlockSpec((tr, CONV_COLS), lambda i, j: (i, j)),
        out_shape=jax.ShapeDtypeStruct((rows, SSD_XBC), F32),
        compiler_params=pltpu.CompilerParams(dimension_semantics=("parallel", "parallel"), vmem_limit_bytes=VMEM_LIMIT),
    )(proj, proj, conv_w, conv_b[None])


def _ssd_conv_bwd(proj, d_xbc, conv_w, conv_b, name):
    rows = proj.shape[0]
    tr = _tile(rows, 512, CONV_HALO)
    nb, halos = rows // tr, tr // CONV_HALO
    first_col = SSD_DINNER // CONV_COLS

    def body(x_ref, before_ref, after_ref, d_ref, d_after_ref, w_ref, b_ref, dx_ref, dw_ref, db_ref):
        i = pl.program_id(1)

        @pl.when(i == 0)
        def _():
            dw_ref[...] = jnp.zeros_like(dw_ref)
            db_ref[...] = jnp.zeros_like(db_ref)

        before = jnp.where(i == 0, 0.0, before_ref[...])
        taps = _conv_taps(jnp.concatenate([before, x_ref[...], after_ref[...]], axis=0), tr + CONV_HALO)
        out = b_ref[...]
        for k in range(SSD_CONV):
            out = out + taps[k] * w_ref[k:k + 1, :]
        sig = 1.0 / (1.0 + jnp.exp(-out))
        d_after = jnp.where(i == nb - 1, 0.0, d_after_ref[...])
        d_out = jnp.concatenate([d_ref[...], d_after], axis=0) * sig * (1.0 + out * (1.0 - sig))
        d_x = d_out[:tr] * w_ref[SSD_CONV - 1:SSD_CONV, :]
        for k in range(SSD_CONV - 1):
            ahead = SSD_CONV - 1 - k
            d_x = d_x + pltpu.roll(d_out, tr + CONV_HALO - ahead, 0)[:tr] * w_ref[k:k + 1, :]
        dx_ref[...] = d_x.astype(dx_ref.dtype)
        for k in range(SSD_CONV):
            dw_ref[k:k + 1, :] += jnp.sum(d_out[:tr] * taps[k][:tr], axis=0, keepdims=True)
        db_ref[...] += jnp.sum(d_out[:tr], axis=0, keepdims=True)

    before = lambda j, i: jnp.maximum(i * halos - 1, 0)
    after = lambda j, i: jnp.minimum((i + 1) * halos, nb * halos - 1)
    d_x, d_w, d_b = pl.pallas_call(
        body, name=name, grid=(SSD_XBC // CONV_COLS, nb),
        in_specs=[pl.BlockSpec((tr, CONV_COLS), lambda j, i: (i, first_col + j)),
                  pl.BlockSpec((CONV_HALO, CONV_COLS), lambda j, i: (before(j, i), first_col + j)),
                  pl.BlockSpec((CONV_HALO, CONV_COLS), lambda j, i: (after(j, i), first_col + j)),
                  pl.BlockSpec((tr, CONV_COLS), lambda j, i: (i, j)),
                  pl.BlockSpec((CONV_HALO, CONV_COLS), lambda j, i: (after(j, i), j)),
                  pl.BlockSpec((SSD_CONV, CONV_COLS), lambda j, i: (0, j)), pl.BlockSpec((1, CONV_COLS), lambda j, i: (0, j))],
        out_specs=[pl.BlockSpec((tr, CONV_COLS), lambda j, i: (i, j)), pl.BlockSpec((SSD_CONV, CONV_COLS), lambda j, i: (0, j)),
                   pl.BlockSpec((1, CONV_COLS), lambda j, i: (0, j))],
        out_shape=[jax.ShapeDtypeStruct((rows, SSD_XBC), BF16), jax.ShapeDtypeStruct((SSD_CONV, SSD_XBC), F32),
                   jax.ShapeDtypeStruct((1, SSD_XBC), F32)],
        compiler_params=pltpu.CompilerParams(dimension_semantics=("parallel", "arbitrary"), vmem_limit_bytes=VMEM_LIMIT),
    )(proj, proj, proj, d_xbc, d_xbc, conv_w, conv_b[None])
    return d_x, d_w, d_b[0]


def _s5_boundary_scan(z, lam_re, lam_im, name, reverse=False):
    n_chunks, groups, width = z.shape
    tn = _tile(n_chunks, 128, 1)
    blocks = n_chunks // tn
    lam_a = jnp.concatenate([lam_re, lam_re], axis=1)
    lam_b = jnp.concatenate([-lam_im, lam_im], axis=1)

    def body(z_ref, a_ref, b_ref, x_ref, carry_ref):
        @pl.when(pl.program_id(0) == 0)
        def _():
            carry_ref[...] = jnp.zeros_like(carry_ref)

        a, b = a_ref[...], b_ref[...]

        def step(i, x):
            n = tn - 1 - i if reverse else i
            x_ref[n] = x
            return a * x + b * pltpu.roll(x, width // 2, 1) + z_ref[n]

        carry_ref[...] = lax.fori_loop(0, tn, step, carry_ref[...])

    block = pl.BlockSpec((tn, groups, width), (lambda i: (blocks - 1 - i, 0, 0)) if reverse else (lambda i: (i, 0, 0)))
    return pl.pallas_call(
        body, name=name, grid=(blocks,), in_specs=[block, _full_spec((groups, width)), _full_spec((groups, width))],
        out_specs=block, out_shape=jax.ShapeDtypeStruct(z.shape, F32), scratch_shapes=[pltpu.VMEM((groups, width), F32)],
        compiler_params=pltpu.CompilerParams(dimension_semantics=("arbitrary",), vmem_limit_bytes=VMEM_LIMIT),
    )(z, lam_a, lam_b)


_FLIPS = [(kx, ky, kc) for kx in (0, 1) for ky in (0, 1) for kc in (0, 1)][1:]


def _mesh_position():
    return lax.axis_index("x"), lax.axis_index("y"), lax.axis_index("c")


def _peer(pos, flip):
    return tuple((1 - p) if f else p for p, f in zip(pos, flip))


def _index(pos):
    return 4 * pos[0] + 2 * pos[1] + pos[2]


_ANY = pl.BlockSpec(memory_space=pl.ANY)


def _moved_shape(kind, x):
    return jax.ShapeDtypeStruct(((N_DEV,) + x.shape) if kind == "gather" else x.shape, x.dtype)


def _moves(kind, x_ref, out_ref, send_sems, recv_sems, local_sem):
    me = _mesh_position()
    source = (lambda pos: x_ref) if kind == "gather" else (lambda pos: x_ref.at[_index(pos)])
    local = pltpu.make_async_copy(source(me), out_ref.at[_index(me)], local_sem)
    outgoing, incoming = [], []
    for k, flip in enumerate(_FLIPS):
        peer = _peer(me, flip)
        copy = lambda slot: pltpu.make_async_remote_copy(
            src_ref=source(peer), dst_ref=out_ref.at[_index(slot)], send_sem=send_sems.at[k], recv_sem=recv_sems.at[k],
            device_id=peer, device_id_type=pl.DeviceIdType.MESH)
        outgoing.append(copy(me))
        incoming.append(copy(peer))
    return local, outgoing, incoming


def _start(moves):
    local, outgoing, _ = moves
    local.start()
    for cp in outgoing:
        cp.start()


def _finish(moves):
    local, outgoing, incoming = moves
    for cp in incoming:
        cp.wait_recv()
    for cp in outgoing:
        cp.wait_send()
    local.wait()


def _collective(kind, x, name):
    def body(x_ref, out_ref, send_sems, recv_sems, local_sem):
        moves = _moves(kind, x_ref, out_ref, send_sems, recv_sems, local_sem)
        _start(moves)
        _finish(moves)

    return pl.pallas_call(
        body, name=name, in_specs=[_ANY], out_specs=_ANY, out_shape=_moved_shape(kind, x),
        scratch_shapes=[pltpu.SemaphoreType.DMA((N_DEV - 1,)), pltpu.SemaphoreType.DMA((N_DEV - 1,)), pltpu.SemaphoreType.DMA],
        compiler_params=pltpu.CompilerParams(has_side_effects=True),
    )(x)


def _adamw(parts, w, m, v, name):
    n_parts = parts.shape[0]
    layers, rows, cols = w.shape
    tr = _tile(rows, 256, 8)

    def body(p_ref, w_ref, m_ref, v_ref, g_ref, d_ref, mo_ref, vo_ref):
        g = p_ref[0].astype(F32)
        for s in range(1, n_parts):
            g = g + p_ref[s].astype(F32)
        m_new = ADAM_B1 * m_ref[...] + (1.0 - ADAM_B1) * g
        v_new = ADAM_B2 * v_ref[...] + (1.0 - ADAM_B2) * (g * g)
        m_hat = m_new / (1.0 - ADAM_B1 ** ADAM_STEP)
        v_hat = v_new / (1.0 - ADAM_B2 ** ADAM_STEP)
        g_ref[...] = g
        d_ref[...] = -ADAM_LR * (m_hat / (jnp.sqrt(v_hat) + ADAM_EPS) + ADAM_WD * w_ref[...])
        mo_ref[...] = m_new
        vo_ref[...] = v_new

    blk = pl.BlockSpec((None, tr, cols), lambda l, i: (l, i, 0))
    shape = jax.ShapeDtypeStruct(w.shape, F32)
    return pl.pallas_call(
        body, name=name, grid=(layers, rows // tr),
        in_specs=[pl.BlockSpec((n_parts, None, tr, cols), lambda l, i: (0, l, i, 0)), blk, blk, blk],
        out_specs=[blk, blk, blk, blk], out_shape=[shape, shape, shape, shape],
        compiler_params=pltpu.CompilerParams(dimension_semantics=("parallel", "parallel"), vmem_limit_bytes=VMEM_LIMIT),
    )(parts, w, m, v)


def _sum_parts(parts, name):
    _, rows, cols = parts.shape
    tr = _tile(rows, 256, 8)

    def body(p_ref, o_ref):
        total = p_ref[0]
        for s in range(1, N_DEV):
            total = total + p_ref[s]
        o_ref[...] = total

    return pl.pallas_call(
        body, name=name, grid=(rows // tr,),
        in_specs=[pl.BlockSpec((N_DEV, tr, cols), lambda i: (0, i, 0))], out_specs=pl.BlockSpec((tr, cols), lambda i: (i, 0)),
        out_shape=jax.ShapeDtypeStruct((rows, cols), parts.dtype),
        compiler_params=pltpu.CompilerParams(dimension_semantics=("parallel",), vmem_limit_bytes=VMEM_LIMIT),
    )(parts)


def _row_tile(rows):
    return _tile(rows, 512, 16)


def _row_spec(rows, cols, block=0):
    return pl.BlockSpec((_row_tile(rows), cols), lambda i: (i, block))


def _rows_params(accumulates):
    return pltpu.CompilerParams(dimension_semantics=("arbitrary" if accumulates else "parallel",),
                                vmem_limit_bytes=VMEM_LIMIT)


def _gu_matmul(x, w, *, name, cargo=()):
    rows, k_dim = x.shape
    tm, tn = _tile(rows, 512, 16), _tile(FFN_HIDDEN, 1408, LANES)
    nj = FFN_HIDDEN // tn

    def body(x_ref, wg_ref, wu_ref, g_ref, u_ref, a_ref):
        xb = x_ref[...].astype(_MXU_DTYPE)
        g = jnp.dot(xb, wg_ref[...].astype(_MXU_DTYPE), preferred_element_type=F32)
        u = jnp.dot(xb, wu_ref[...].astype(_MXU_DTYPE), preferred_element_type=F32)
        g_ref[...] = g.astype(g_ref.dtype)
        u_ref[...] = u.astype(u_ref.dtype)
        a_ref[...] = (_silu(g) * u).astype(a_ref.dtype)

    out = pl.BlockSpec((tm, tn), lambda i, j: (i, j))
    shape = jax.ShapeDtypeStruct((rows, FFN_HIDDEN), BF16)
    return _cargo_call(
        body, cargo, name=name, grid=(rows // tm, nj),
        in_specs=[pl.BlockSpec((tm, k_dim), lambda i, j: (i, 0)), pl.BlockSpec((k_dim, tn), lambda i, j: (0, j)),
                  pl.BlockSpec((k_dim, tn), lambda i, j: (0, nj + j))],
        out_specs=[out, out, out], out_shape=[shape, shape, shape], scratch_shapes=[], semantics=("parallel", "parallel"),
    )(x, w, w)


def _swiglu_bwd(g, u, d_act, name):
    rows = g.shape[0]

    def body(g_ref, u_ref, d_ref, o_ref):
        g, u, d = g_ref[...].astype(F32), u_ref[...].astype(F32), d_ref[...].astype(F32)
        sig = 1.0 / (1.0 + jnp.exp(-g))
        o_ref[:, :FFN_HIDDEN] = (d * u * sig * (1.0 + g * (1.0 - sig))).astype(o_ref.dtype)
        o_ref[:, FFN_HIDDEN:] = (d * g * sig).astype(o_ref.dtype)

    return pl.pallas_call(
        body, name=name, grid=(rows // _row_tile(rows),),
        in_specs=[_row_spec(rows, FFN_HIDDEN)] * 3, out_specs=_row_spec(rows, 2 * FFN_HIDDEN),
        out_shape=jax.ShapeDtypeStruct((rows, 2 * FFN_HIDDEN), BF16), compiler_params=_rows_params(False))(g, u, d_act)


def _add_norm_fwd(h, y, gain, name):
    rows = h.shape[0]

    def body(*refs):
        if y is None:
            h_ref, g_ref, n_ref = refs
            x = h_ref[...]
        else:
            h_ref, y_ref, g_ref, s_ref, n_ref = refs
            x = h_ref[...] + y_ref[...]
            s_ref[...] = x
        n_ref[...] = (x * lax.rsqrt(jnp.mean(x * x, axis=-1, keepdims=True) + EPS) * g_ref[...]).astype(n_ref.dtype)

    row = _row_spec(rows, D_MODEL)
    ins = [h] if y is None else [h, y]
    out_shape = [jax.ShapeDtypeStruct((rows, D_MODEL), BF16)]
    if y is not None:
        out_shape = [jax.ShapeDtypeStruct((rows, D_MODEL), F32)] + out_shape
    res = pl.pallas_call(
        body, name=name, grid=(rows // _row_tile(rows),),
        in_specs=[row] * len(ins) + [_full_spec((1, D_MODEL))], out_specs=[row] * len(out_shape), out_shape=out_shape,
        compiler_params=_rows_params(False))(*ins, gain[None])
    return (h, res[0]) if y is None else (res[0], res[1])


def _norm_bwd(x, gain, d_n, d_skip, name):
    rows = x.shape[0]
    d_parts = d_n if isinstance(d_n, tuple) else (d_n,)

    def body(x_ref, g_ref, *refs):
        dn_refs, (ds_ref, dx_ref, dg_ref) = refs[:len(d_parts)], refs[len(d_parts):]

        @pl.when(pl.program_id(0) == 0)
        def _():
            dg_ref[...] = jnp.zeros_like(dg_ref)

        x, dn = x_ref[...], sum(r[...].astype(F32) for r in dn_refs)
        r = lax.rsqrt(jnp.mean(x * x, axis=-1, keepdims=True) + EPS)
        gd = g_ref[...] * dn
        dx_ref[...] = r * gd - x * (r * r * r) * jnp.mean(x * gd, axis=-1, keepdims=True) + ds_ref[...]
        dg_ref[...] += jnp.sum(x * r * dn, axis=0, keepdims=True)

    row = _row_spec(rows, D_MODEL)
    dx, dg = pl.pallas_call(
        body, name=name, grid=(rows // _row_tile(rows),),
        in_specs=[row, _full_spec((1, D_MODEL))] + [row] * (len(d_parts) + 1), out_specs=[row, _full_spec((1, D_MODEL))],
        out_shape=[jax.ShapeDtypeStruct((rows, D_MODEL), F32), jax.ShapeDtypeStruct((1, D_MODEL), F32)],
        compiler_params=_rows_params(True))(x, gain[None], *d_parts, d_skip)
    return dx, dg[0]


_GELU_C, _GELU_A = math.sqrt(2.0 / math.pi), 0.044715


def _s5_gate_fwd(y, u, d_skip, name):
    rows = y.shape[0]

    def body(y_ref, u_ref, d_ref, o_ref):
        x = y_ref[...].astype(F32) + d_ref[...] * u_ref[...].astype(F32)
        o_ref[...] = (0.5 * x * (1.0 + jnp.tanh(_GELU_C * (x + _GELU_A * x * x * x)))).astype(o_ref.dtype)

    row = _row_spec(rows, D_MODEL)
    return pl.pallas_call(
        body, name=name, grid=(rows // _row_tile(rows),), in_specs=[row, row, _full_spec((1, D_MODEL))], out_specs=row,
        out_shape=jax.ShapeDtypeStruct((rows, D_MODEL), BF16), compiler_params=_rows_params(False))(y, u, d_skip[None])


def _s5_gate_bwd(y, u, d_skip, d_act, name):
    rows = y.shape[0]

    def body(y_ref, u_ref, d_ref, da_ref, dy_ref, du_ref, dd_ref):
        @pl.when(pl.program_id(0) == 0)
        def _():
            dd_ref[...] = jnp.zeros_like(dd_ref)

        u = u_ref[...].astype(F32)
        x = y_ref[...].astype(F32) + d_ref[...] * u
        t = jnp.tanh(_GELU_C * (x + _GELU_A * x * x * x))
        slope = 0.5 * (1.0 + t) + 0.5 * x * (1.0 - t * t) * _GELU_C * (1.0 + 3.0 * _GELU_A * x * x)
        dx = da_ref[...].astype(F32) * slope
        dy_ref[...] = dx.astype(dy_ref.dtype)
        du_ref[...] = (dx * d_ref[...]).astype(du_ref.dtype)
        dd_ref[...] += jnp.sum(dx * u, axis=0, keepdims=True)

    row = _row_spec(rows, D_MODEL)
    shape = jax.ShapeDtypeStruct((rows, D_MODEL), BF16)
    d_y, d_u, d_d = pl.pallas_call(
        body, name=name, grid=(rows // _row_tile(rows),), in_specs=[row, row, _full_spec((1, D_MODEL)), row],
        out_specs=[row, row, _full_spec((1, D_MODEL))], out_shape=[shape, shape, jax.ShapeDtypeStruct((1, D_MODEL), F32)],
        compiler_params=_rows_params(True))(y, u, d_skip[None], d_act)
    return d_y, d_u, d_d[0]


def _glu_fwd(vg, name):
    rows = vg.shape[0]

    def body(v_ref, g_ref, o_ref):
        o_ref[...] = v_ref[...] / (1.0 + jnp.exp(-g_ref[...]))

    return pl.pallas_call(
        body, name=name, grid=(rows // _row_tile(rows),),
        in_specs=[_row_spec(rows, D_MODEL, 0), _row_spec(rows, D_MODEL, 1)], out_specs=_row_spec(rows, D_MODEL),
        out_shape=jax.ShapeDtypeStruct((rows, D_MODEL), F32), compiler_params=_rows_params(False))(vg, vg)


def _glu_bwd(vg, d_out, name):
    rows = vg.shape[0]

    def body(v_ref, g_ref, d_ref, o_ref):
        sig = 1.0 / (1.0 + jnp.exp(-g_ref[...]))
        d = d_ref[...]
        o_ref[:, :D_MODEL] = (d * sig).astype(o_ref.dtype)
        o_ref[:, D_MODEL:] = (d * v_ref[...] * sig * (1.0 - sig)).astype(o_ref.dtype)

    return pl.pallas_call(
        body, name=name, grid=(rows // _row_tile(rows),),
        in_specs=[_row_spec(rows, D_MODEL, 0), _row_spec(rows, D_MODEL, 1), _row_spec(rows, D_MODEL)],
        out_specs=_row_spec(rows, 2 * D_MODEL), out_shape=jax.ShapeDtypeStruct((rows, 2 * D_MODEL), BF16),
        compiler_params=_rows_params(False))(vg, vg, d_out)


def _loss_head(h, gain, target, name):
    rows = h.shape[0]

    def body(x_ref, g_ref, t_ref, loss_ref, dx_ref, dg_ref):
        @pl.when(pl.program_id(0) == 0)
        def _():
            loss_ref[...] = jnp.zeros_like(loss_ref)
            dg_ref[...] = jnp.zeros_like(dg_ref)

        x = x_ref[...]
        r = lax.rsqrt(jnp.mean(x * x, axis=-1, keepdims=True) + EPS)
        err = x * r * g_ref[...] - t_ref[...]
        loss_ref[...] += 0.5 * jnp.sum(jnp.mean(err * err, axis=-1, keepdims=True), axis=0, keepdims=True)
        dy = err * (1.0 / D_MODEL)
        gd = g_ref[...] * dy
        dx_ref[...] = r * gd - x * (r * r * r) * jnp.mean(x * gd, axis=-1, keepdims=True)
        dg_ref[...] += jnp.sum(x * r * dy, axis=0, keepdims=True)

    row = _row_spec(rows, D_MODEL)
    loss, dx, dg = pl.pallas_call(
        body, name=name, grid=(rows // _row_tile(rows),),
        in_specs=[row, _full_spec((1, D_MODEL)), row], out_specs=[_full_spec((1, 1)), row, _full_spec((1, D_MODEL))],
        out_shape=[jax.ShapeDtypeStruct((1, 1), F32), jax.ShapeDtypeStruct((rows, D_MODEL), F32),
                   jax.ShapeDtypeStruct((1, D_MODEL), F32)],
        compiler_params=_rows_params(True))(h, gain[None], target)
    return loss[0, 0], dx, dg[0]


def _join_cols(blocks, n_out, name):
    _, layers, rows, n = blocks.shape
    tr = _tile(rows, 256, 16)

    def body(x_ref, o_ref):
        for d in range(N_DEV):
            o_ref[:, d * n:(d + 1) * n] = x_ref[d]
        if n_out > N_DEV * n:
            o_ref[:, N_DEV * n:] = jnp.zeros((tr, n_out - N_DEV * n), o_ref.dtype)

    return pl.pallas_call(
        body, name=name, grid=(layers, rows // tr),
        in_specs=[pl.BlockSpec((N_DEV, None, tr, n), lambda l, i: (0, l, i, 0))],
        out_specs=pl.BlockSpec((None, tr, n_out), lambda l, i: (l, i, 0)),
        out_shape=jax.ShapeDtypeStruct((layers, rows, n_out), blocks.dtype),
        compiler_params=pltpu.CompilerParams(dimension_semantics=("parallel", "parallel"), vmem_limit_bytes=VMEM_LIMIT),
    )(blocks)


def _split_cols(full, n, name):
    rows = full.shape[0]
    tr = _tile(rows, 256, 16)

    def body(x_ref, o_ref):
        for d in range(N_DEV):
            o_ref[d] = x_ref[:, d * n:(d + 1) * n]

    return pl.pallas_call(
        body, name=name, grid=(rows // tr,),
        in_specs=[pl.BlockSpec((tr, full.shape[1]), lambda i: (i, 0))],
        out_specs=pl.BlockSpec((N_DEV, tr, n), lambda i: (0, i, 0)),
        out_shape=jax.ShapeDtypeStruct((N_DEV, rows, n), full.dtype),
        compiler_params=pltpu.CompilerParams(dimension_semantics=("parallel",), vmem_limit_bytes=VMEM_LIMIT),
    )(full)


def _pack(arrays):
    flat = jnp.concatenate([a.reshape(-1) for a in arrays])
    unit = FLAT_COLS * FLAT_ROWS_ALIGN
    padded = -(-flat.shape[0] // unit) * unit
    return jnp.pad(flat, (0, padded - flat.shape[0])).reshape(-1, FLAT_COLS)


def _unpack(flat, shapes, lead=()):
    flat = flat.reshape(lead + (-1,))
    out, off = [], 0
    for shape in shapes:
        n = math.prod(shape)
        out.append(flat[..., off:off + n].reshape(lead + tuple(shape)))
        off += n
    return out


def _join(blocks, axis):
    moved = jnp.moveaxis(blocks, 0, axis)
    shape = list(moved.shape)
    shape[axis:axis + 2] = [shape[axis] * shape[axis + 1]]
    return moved.reshape(shape)


def _own_shard(full, axis, position):
    n = full.shape[axis] // N_DEV
    return lax.dynamic_slice_in_dim(full, position * n, n, axis)


def _s5_operators(log_dt, a_re, a_im, b_re, b_im, c_re, c_im):
    t = S5_CHUNK
    hi = lax.Precision.HIGHEST
    step = jnp.exp(log_dt)[:, None]
    mag = jnp.exp(step * a_re)
    abar_re = mag * jnp.cos(step * a_im)
    abar_im = mag * jnp.sin(step * a_im)
    den = a_re * a_re + a_im * a_im
    f_re = ((abar_re - 1.0) * a_re + abar_im * a_im) / den
    f_im = (abar_im * a_re - (abar_re - 1.0) * a_im) / den
    bb_re = f_re[..., None] * b_re - f_im[..., None] * b_im
    bb_im = f_re[..., None] * b_im + f_im[..., None] * b_re
    j = jnp.arange(t + 1, dtype=F32)[:, None, None]
    pmag = jnp.exp(j * (step * a_re))
    pw_re = pmag * jnp.cos(j * (step * a_im))
    pw_im = pmag * jnp.sin(j * (step * a_im))
    cl_re = c_re[None] * pw_re[:t, :, None, :] - c_im[None] * pw_im[:t, :, None, :]
    cl_im = c_re[None] * pw_im[:t, :, None, :] + c_im[None] * pw_re[:t, :, None, :]
    kern = (jnp.einsum('jgcp,gpk->jgck', cl_re, bb_re, precision=hi)
            - jnp.einsum('jgcp,gpk->jgck', cl_im, bb_im, precision=hi))
    rp_re, rp_im = pw_re[:t][::-1], pw_im[:t][::-1]
    wz_re = rp_re[:, :, :, None] * bb_re[None] - rp_im[:, :, :, None] * bb_im[None]
    wz_im = rp_re[:, :, :, None] * bb_im[None] + rp_im[:, :, :, None] * bb_re[None]
    w_z = jnp.concatenate([wz_re, wz_im], axis=2).transpose(1, 0, 3, 2).reshape(S5_GROUPS, t * S5_GROUP, 2 * S5_STATE)
    cy_re = c_re[None] * pw_re[1:, :, None, :] - c_im[None] * pw_im[1:, :, None, :]
    cy_im = c_re[None] * pw_im[1:, :, None, :] + c_im[None] * pw_re[1:, :, None, :]
    w_y = jnp.concatenate([cy_re, -cy_im], axis=3).transpose(1, 3, 0, 2).reshape(S5_GROUPS, 2 * S5_STATE, t * S5_GROUP)
    return kern, w_z, w_y, pw_re[t], pw_im[t]


def _s5_lag_selector():
    t = S5_CHUNK
    lag = jnp.arange(t)[:, None] - jnp.arange(t)[None, :]
    return (lag[:, :, None] == jnp.arange(t)[None, None, :]).astype(F32).reshape(t * t, t)


def _s5_toeplitz(kern, tag):
    t = S5_CHUNK
    sel = _s5_lag_selector()
    flat = _matmul(sel, kern.reshape(t, -1), out_dtype=BF16, name=tag + "_toeplitz")
    toep = flat.reshape(t, t, S5_GROUPS, S5_GROUP, S5_GROUP).transpose(2, 1, 4, 0, 3)
    toep = toep.reshape(S5_GROUPS, t * S5_GROUP, t * S5_GROUP)

    def backward(d_toep):
        d_flat = d_toep.reshape(S5_GROUPS, t, S5_GROUP, t, S5_GROUP).transpose(3, 1, 0, 4, 2).reshape(t * t, -1)
        return _matmul(sel, d_flat, ta=True, name=tag + "_toeplitz_dw").reshape(kern.shape)

    return toep, backward


_BIG = [("gla_w_in", 2), ("gla_w_out", 1), ("ssd_w_in", 2), ("ssd_w_out", 1), ("s5_w_glu", 2), ("ffn_w_gu", 2),
        ("ffn_w_down", 1)]
_PADDED_COLS = {"gla_w_in": GLA_PROJ, "ssd_w_in": SSD_PROJ}


class _Traffic:
    LINK_BYTES_PER_SECOND = 7.0e10
    MATMUL_FLOPS = 7.0e14

    def __init__(self, shards, plan):
        self.shards, self.plan = shards, plan
        self.position = 0
        self.queue = []
        self.weights, self.received = {}, {}
        self.early = None
        self.standalone = self.serial = 0
        for key in plan:
            self._request(key)

    def _request(self, key):
        shard = self.shards[key]
        seconds = (N_DEV - 1) * shard.size * shard.dtype.itemsize / self.LINK_BYTES_PER_SECOND
        self._enqueue("gather", shard, seconds, key, lambda blocks: self.weights.__setitem__(key, self._assemble(key, blocks)))

    def _enqueue(self, kind, x, seconds, key, deliver):
        self.queue.append((kind, x, seconds, key, deliver, self.serial))
        self.serial += 1

    @staticmethod
    def _assemble(key, blocks):
        name, layer = key
        if dict(_BIG)[name] == 1:
            return blocks.reshape((N_DEV * blocks.shape[1], blocks.shape[2]))
        n_out = _PADDED_COLS.get(name, N_DEV * blocks.shape[2])
        return _join_cols(blocks[:, None], n_out, f"join_{name}_{layer}")[0]

    def take(self, key):
        assert key == self.plan[self.position], (key, self.plan[self.position])
        self.position += 1
        while key not in self.weights:
            self._alone(self.queue.pop(0))
        return self.weights[key]

    def run(self, seconds, call, more_carriers_follow=False):
        riders, waiting, left = [], [], seconds
        for item in self.queue:
            if item[2] <= left:
                riders.append(item)
                left -= item[2]
            else:
                waiting.append(item)
        due = [item for item in waiting if item[3] is not None and self.position < len(self.plan)
               and item[3] == self.plan[self.position]]
        if due and not more_carriers_follow:
            left = seconds - due[0][2]
            kept = []
            for item in riders:
                if item[2] <= left:
                    kept.append(item)
                    left -= item[2]
                else:
                    waiting.append(item)
            riders = due + kept
            waiting = [item for item in waiting if item is not due[0]]
            waiting.sort(key=lambda item: item[5])
        self.queue = waiting
        if not riders:
            return call(())
        results, moved = call([(kind, x) for kind, x, *_ in riders])
        for item, y in zip(riders, moved):
            item[4](y)
        return results

    def matmul(self, a, b, more_carriers_follow=False, **kw):
        m, n = (a.shape[-1] if kw.get("ta") else a.shape[-2]), (b.shape[-2] if kw.get("tb") else b.shape[-1])
        k = a.shape[-2] if kw.get("ta") else a.shape[-1]
        return self.run(2.0 * m * n * k / self.MATMUL_FLOPS, lambda cargo: _matmul(a, b, cargo=cargo, **kw),
                        more_carriers_follow)

    def send_gradient(self, key, dw):
        name, layer = key
        shard = self.shards[key]
        if dict(_BIG)[name] == 1:
            blocks = dw.reshape((N_DEV,) + shard.shape)
        else:
            blocks = _split_cols(dw, shard.shape[1], f"split_{name}_{layer}")
        seconds = (N_DEV - 1) * shard.size * shard.dtype.itemsize / self.LINK_BYTES_PER_SECOND
        self._enqueue("exchange", blocks, seconds, None, lambda parts: self.received.__setitem__(key, parts))

    def gather_early(self, packed):
        seconds = (N_DEV - 1) * packed.size * packed.dtype.itemsize / self.LINK_BYTES_PER_SECOND
        self._enqueue("gather", packed, seconds, None, lambda parts: setattr(self, "early", parts))

    def _alone(self, item):
        kind, x, _, _, deliver, _ = item
        deliver(_collective(kind, x, f"{kind}_alone_{self.standalone}"))
        self.standalone += 1

    def flush(self):
        for item in self.queue:
            self._alone(item)
        self.queue = []


_GLA_FWD_SECONDS, _GLA_BWD_SECONDS, _SSD_FWD_SECONDS, _SSD_BWD_SECONDS = 1.25e-6, 3.4e-6, 3.5e-6, 14e-6


def _linear(x, w, tag, out_dtype=F32, dx_dtype=F32, more_carriers_follow=False):
    traffic, key = w
    weight = traffic.take(key)
    y = traffic.matmul(x, weight, more_carriers_follow, out_dtype=out_dtype, name=tag + "_fwd")

    def backward(dy):
        dx = traffic.matmul(dy, weight, tb=True, out_dtype=dx_dtype, name=tag + "_dx")
        traffic.send_gradient(key, traffic.matmul(x, dy, ta=True, out_dtype=BF16, name=tag + "_dw"))
        return dx

    return y, backward


def _gated_linear(x, w, tag):
    traffic, key = w
    weight = traffic.take(key)
    seconds = 2.0 * x.shape[0] * x.shape[1] * weight.shape[1] / traffic.MATMUL_FLOPS
    outs = traffic.run(seconds, lambda cargo: _gu_matmul(x, weight, name=tag + "_fwd", cargo=cargo))

    def backward(d_gu):
        dx = traffic.matmul(d_gu, weight, tb=True, name=tag + "_dx")
        traffic.send_gradient(key, traffic.matmul(x, d_gu, ta=True, out_dtype=BF16, name=tag + "_dw"))
        return dx

    return tuple(outs), backward


def _gla_mixer(hn, p, tag):
    traffic, chunks = p["w_in"][0], hn.shape[0] // CHUNK
    w_a2 = jnp.pad(p["w_a2"], ((0, LANES - GLA_RANK), (0, 0)))
    b_a, norm_g = p["b_a"][None], p["norm_g"][None]
    proj, lin_in = _linear(hn, p["w_in"], tag + "_in", more_carriers_follow=True)
    o, sprev = traffic.run(chunks * _GLA_FWD_SECONDS, lambda cargo: _gla_core_fwd(proj, w_a2, b_a, norm_g, cargo))
    y, lin_out = _linear(o, p["w_out"], tag + "_out")

    def backward(dy):
        d_o = lin_out(dy)
        d_proj, d_wa, d_ba, d_ng = traffic.run(chunks * _GLA_BWD_SECONDS,
                                               lambda cargo: _gla_core_bwd(proj, sprev, d_o, w_a2, b_a, norm_g, cargo))
        return lin_in(d_proj), dict(w_a2=d_wa[:GLA_RANK], b_a=d_ba[0], norm_g=d_ng[0])

    return y, backward


def _ssd_mixer(hn, p, tag):
    pad = lambda a: jnp.pad(a[None], ((0, 0), (0, LANES - SSD_HEADS)))
    dt_bias, a_log, d_skip, norm_g = pad(p["dt_bias"]), pad(p["a_log"]), pad(p["d"]), p["norm_g"][None]
    traffic, chunks = p["w_in"][0], hn.shape[0] // CHUNK
    proj, lin_in = _linear(hn, p["w_in"], tag + "_in", more_carriers_follow=True)
    xbc = _ssd_conv_fwd(proj, p["conv_w"], p["conv_b"], tag + "_conv")
    o, hprev = traffic.run(chunks * _SSD_FWD_SECONDS,
                           lambda cargo: _ssd_core_fwd(proj, xbc, dt_bias, a_log, d_skip, norm_g, cargo))
    y, lin_out = _linear(o, p["w_out"], tag + "_out")

    def backward(dy):
        d_o = lin_out(dy)
        d_z, d_xbc, d_dt, d_db, d_al, d_ds, d_ng = traffic.run(
            chunks * _SSD_BWD_SECONDS, lambda cargo: _ssd_core_bwd(proj, xbc, hprev, d_o, dt_bias, a_log, d_skip, norm_g, cargo))
        d_pre, d_cw, d_cb = _ssd_conv_bwd(proj, d_xbc, p["conv_w"], p["conv_b"], tag + "_conv_bwd")
        d_hn = lin_in(jnp.concatenate([d_z, d_pre, d_dt], axis=1))
        return d_hn, dict(conv_w=d_cw, conv_b=d_cb, dt_bias=d_db[0, :SSD_HEADS], a_log=d_al[0, :SSD_HEADS],
                          d=d_ds[0, :SSD_HEADS], norm_g=d_ng[0])

    return y, backward


def _s5_mixer(hn, p, tag):
    seq = hn.shape[0]
    t, n_chunks = S5_CHUNK, hn.shape[0] // S5_CHUNK
    names = ("log_dt", "a_re", "a_im", "b_re", "b_im", "c_re", "c_im")
    (kern, w_z, w_y, lam_re, lam_im), ops_vjp = jax.vjp(_s5_operators, *[p[k] for k in names])
    toep, toep_bwd = _s5_toeplitz(kern, tag)
    to_groups = lambda a: a.reshape(n_chunks, t, S5_GROUPS, S5_GROUP).transpose(2, 0, 1, 3).reshape(S5_GROUPS, n_chunks, t * S5_GROUP)
    from_groups = lambda a: a.reshape(S5_GROUPS, n_chunks, t, S5_GROUP).transpose(1, 2, 0, 3).reshape(seq, D_MODEL)
    ug = to_groups(hn)
    z = _matmul(ug, w_z, name=tag + "_z")
    x_before = _s5_boundary_scan(z.transpose(1, 0, 2), lam_re, lam_im, tag + "_scan")
    xprev = x_before.transpose(1, 0, 2)
    tw = t * S5_GROUP
    ux = jnp.concatenate([ug, xprev.astype(BF16)], axis=2)
    yg = _matmul(ux, jnp.concatenate([toep, w_y.astype(BF16)], axis=1), out_dtype=BF16, name=tag + "_y")
    y = from_groups(yg)
    vg, lin_glu = _linear(_s5_gate_fwd(y, hn, p["d"], tag + "_gate"), p["w_glu"], tag + "_glu", dx_dtype=BF16)
    out = _glu_fwd(vg, tag + "_glu_gate")

    def backward(dy):
        d_act = lin_glu(_glu_bwd(vg, dy, tag + "_glu_gate_bwd"))
        d_y, d_u, d_d = _s5_gate_bwd(y, hn, p["d"], d_act, tag + "_gate_bwd")
        d_yg = to_groups(d_y)
        d_xprev = _matmul(d_yg, w_y, tb=True, name=tag + "_inter_dx").transpose(1, 0, 2)
        d_wy = _matmul(xprev, d_yg, ta=True, name=tag + "_inter_dw")
        dz = _s5_boundary_scan(d_xprev, lam_re, -lam_im, tag + "_scan_bwd", reverse=True)
        x_re, x_im, dz_re, dz_im = (x_before[..., :S5_STATE], x_before[..., S5_STATE:], dz[..., :S5_STATE],
                                    dz[..., S5_STATE:])
        d_lam_re = jnp.sum(x_re * dz_re + x_im * dz_im, axis=0)
        d_lam_im = jnp.sum(x_re * dz_im - x_im * dz_re, axis=0)
        dyz = jnp.concatenate([d_yg, dz.transpose(1, 0, 2).astype(BF16)], axis=2)
        d_ug = _matmul(dyz, jnp.concatenate([toep, w_z.astype(BF16)], axis=2), tb=True, out_dtype=BF16, name=tag + "_du")
        d_ops = _matmul(ug, dyz, ta=True, name=tag + "_dw")
        grads = dict(zip(names, ops_vjp((toep_bwd(d_ops[..., :tw]), d_ops[..., tw:], d_wy, d_lam_re, d_lam_im))))
        grads.update(d=d_d)
        return (d_u, from_groups(d_ug)), grads

    return out, backward


_SMALL =[("gla_w_a2", 2), ("gla_b_a", 1), ("gla_norm_g", 1), ("ssd_conv_w", 2), ("s5_d", 1)]
_REPLICATED = ["norm_mix_g", "norm_ffn_g", "ssd_conv_b", "ssd_dt_bias", "ssd_a_log", "ssd_d", "ssd_norm_g", "s5_log_dt",
               "s5_a_re", "s5_a_im", "s5_b_re", "s5_b_im", "s5_c_re", "s5_c_im", "final_norm_g"]
_EARLY_SMALL = [n for n in [s for s, _ in _SMALL] + _REPLICATED if n.startswith("s5_")]
_WEIGHTS = ['norm_mix_g', 'norm_ffn_g', 'gla_w_in', 'gla_w_a2', 'gla_b_a', 'gla_norm_g', 'gla_w_out', 'ssd_w_in',
            'ssd_conv_w', 'ssd_conv_b', 'ssd_dt_bias', 'ssd_a_log', 'ssd_d', 'ssd_norm_g', 'ssd_w_out', 's5_log_dt',
            's5_a_re', 's5_a_im', 's5_b_re', 's5_b_im', 's5_c_re', 's5_c_im', 's5_d', 's5_w_glu', 'ffn_w_gu', 'ffn_w_down',
            'final_norm_g']


def _gather_small(local):
    shapes = [local[n].shape for n, _ in _SMALL]
    blocks = _collective("gather", _pack([local[n] for n, _ in _SMALL]), "gather_vectors")
    parts = _unpack(blocks, shapes, lead=(N_DEV,))
    return {n: _join(part, axis) for (n, axis), part in zip(_SMALL, parts)}


def _forward_plan():
    plan = []
    for i in range(DEPTH):
        j = i // 3
        plan += [[("gla_w_in", j), ("gla_w_out", j)], [("ssd_w_in", j), ("ssd_w_out", j)], [("s5_w_glu", j)]][i % 3]
        plan += [("ffn_w_gu", i), ("ffn_w_down", i)]
    return plan


def _forward_backward(x, target, w, traffic):
    big = lambda name, j: (traffic, (name, j))
    gla = lambda j: dict(w_in=big("gla_w_in", j), w_a2=w["gla_w_a2"][j], b_a=w["gla_b_a"][j], norm_g=w["gla_norm_g"][j],
                         w_out=big("gla_w_out", j))
    ssd = lambda j: dict(w_in=big("ssd_w_in", j), conv_w=w["ssd_conv_w"][j], conv_b=w["ssd_conv_b"][j],
                         dt_bias=w["ssd_dt_bias"][j], a_log=w["ssd_a_log"][j], d=w["ssd_d"][j], norm_g=w["ssd_norm_g"][j],
                         w_out=big("ssd_w_out", j))
    s5 = lambda j: dict(log_dt=w["s5_log_dt"][j], a_re=w["s5_a_re"][j], a_im=w["s5_a_im"][j], b_re=w["s5_b_re"][j],
                        b_im=w["s5_b_im"][j], c_re=w["s5_c_re"][j], c_im=w["s5_c_im"][j], d=w["s5_d"][j],
                        w_glu=big("s5_w_glu", j))
    mixers = [("gla", _gla_mixer, gla), ("ssd", _ssd_mixer, ssd), ("s5", _s5_mixer, s5)]
    base, delta = x, None
    tape = []
    for i in range(DEPTH):
        kind, mixer, params = mixers[i % 3]
        j = i // 3
        h, hn = _add_norm_fwd(base, delta, w["norm_mix_g"][i], f"l{i}_norm_mix")
        y, mixer_bwd = mixer(hn, params(j), f"l{i}_{kind}")
        h_mid, hn2 = _add_norm_fwd(h, y, w["norm_ffn_g"][i], f"l{i}_norm_ffn")
        (g, u, act), gu_bwd = _gated_linear(hn2, big("ffn_w_gu", i), f"l{i}_ffn_gu")
        delta, down_bwd = _linear(act, big("ffn_w_down", i), f"l{i}_ffn_down", dx_dtype=BF16)
        base = h_mid
        tape.append((kind, j, h, mixer_bwd, h_mid, gu_bwd, (g, u), down_bwd))
    loss, d_h, d_final_g = _loss_head(base + delta, w["final_norm_g"], target, "loss_head")

    grads = {n: [None] * w[n].shape[0] for n in w if n != "final_norm_g"}
    grads["final_norm_g"] = d_final_g
    for i in reversed(range(DEPTH)):
        kind, j, h, mixer_bwd, h_mid, gu_bwd, (g, u), down_bwd = tape[i]
        d_gu = _swiglu_bwd(g, u, down_bwd(d_h), f"l{i}_swiglu_bwd")
        d_mid, grads["norm_ffn_g"][i] = _norm_bwd(h_mid, w["norm_ffn_g"][i], gu_bwd(d_gu), d_h, f"l{i}_norm_ffn_bwd")
        d_hn, mixer_grads = mixer_bwd(d_mid)
        for k, g in mixer_grads.items():
            grads[f"{kind}_{k}"][j] = g
        if kind == "s5" and all(g is not None for n in _EARLY_SMALL for g in grads[n]):
            traffic.gather_early(_pack([jnp.stack(grads[n]) for n in _EARLY_SMALL]))
        d_h, grads["norm_mix_g"][i] = _norm_bwd(h, w["norm_mix_g"][i], d_hn, d_mid, f"l{i}_norm_mix_bwd")
    return loss, d_h, grads


def kernel(x, norm_mix_g, norm_ffn_g, gla_w_in, gla_w_a2, gla_b_a, gla_norm_g, gla_w_out, ssd_w_in, ssd_conv_w, ssd_conv_b, ssd_dt_bias, ssd_a_log, ssd_d, ssd_norm_g, ssd_w_out, s5_log_dt, s5_a_re, s5_a_im, s5_b_re, s5_b_im, s5_c_re, s5_c_im, s5_d, s5_w_glu, ffn_w_gu, ffn_w_down, final_norm_g, loss_target, m_norm_mix_g, m_norm_ffn_g, m_gla_w_in, m_gla_w_a2, m_gla_b_a, m_gla_norm_g, m_gla_w_out, m_ssd_w_in, m_ssd_conv_w, m_ssd_conv_b, m_ssd_dt_bias, m_ssd_a_log, m_ssd_d, m_ssd_norm_g, m_ssd_w_out, m_s5_log_dt, m_s5_a_re, m_s5_a_im, m_s5_b_re, m_s5_b_im, m_s5_c_re, m_s5_c_im, m_s5_d, m_s5_w_glu, m_ffn_w_gu, m_ffn_w_down, m_final_norm_g, v_norm_mix_g, v_norm_ffn_g, v_gla_w_in, v_gla_w_a2, v_gla_b_a, v_gla_norm_g, v_gla_w_out, v_ssd_w_in, v_ssd_conv_w, v_ssd_conv_b, v_ssd_dt_bias, v_ssd_a_log, v_ssd_d, v_ssd_norm_g, v_ssd_w_out, v_s5_log_dt, v_s5_a_re, v_s5_a_im, v_s5_b_re, v_s5_b_im, v_s5_c_re, v_s5_c_im, v_s5_d, v_s5_w_glu, v_ffn_w_gu, v_ffn_w_down, v_final_norm_g):
    args = locals()
    local = {n: args[n] for n in _WEIGHTS}
    moment_m = {n: args["m_" + n] for n in _WEIGHTS}
    moment_v = {n: args["v_" + n] for n in _WEIGHTS}

    shards = {(n, layer): local[n][layer].astype(BF16) for n, _ in _BIG for layer in range(local[n].shape[0])}
    traffic = _Traffic(shards, _forward_plan())
    full = {n: local[n] for n in _REPLICATED}
    full.update(_gather_small(local))

    loss, d_x, grads = _forward_backward(x[0], loss_target[0], full, traffic)
    traffic.flush()
    loss = lax.psum(loss, ("x", "y", "c"))
    kinds = ("grad", "delta", "new_m", "new_v")
    out = {}

    for n, _ in _BIG:
        parts = jnp.stack([traffic.received[(n, layer)] for layer in range(local[n].shape[0])], axis=1)
        results = _adamw(parts, local[n], moment_m[n], moment_v[n], "adamw_" + n)
        out.update({f"{kind}_{n}": a for kind, a in zip(kinds, results)})

    small = [n for n, _ in _SMALL] + _REPLICATED
    stacked = lambda n: grads[n] if n == "final_norm_g" else jnp.stack(grads[n])
    late = [n for n in small if n not in _EARLY_SMALL]
    gathered = [(_EARLY_SMALL, traffic.early, "early"),
                (late, _collective("gather", _pack([stacked(n) for n in late]), "gather_small_gradients"), "late")]
    summed = {}
    for names, parts, tag in gathered:
        sums = _unpack(_sum_parts(parts, "sum_small_gradients_" + tag), [stacked(n).shape for n in names])
        summed.update(zip(names, sums))
    position = _index(_mesh_position())
    mine = [_own_shard(summed[n], axis, position) for n, axis in _SMALL] + [summed[n] for n in _REPLICATED]
    shapes = [local[n].shape for n in small]
    pk = lambda arrays: _pack(arrays)[None]
    results = _adamw(pk(mine)[None], pk([local[n] for n in small]), pk([moment_m[n] for n in small]),
                     pk([moment_v[n] for n in small]), "adamw_small")
    for kind, flat in zip(kinds, results):
        out.update({f"{kind}_{n}": a for n, a in zip(small, _unpack(flat[0], shapes))})

    return (loss, d_x[None], *[out[f"{kind}_{n}"] for kind in ("grad", "delta", "new_m", "new_v") for n in _WEIGHTS])
```

```python
import functools
import math

import jax
import jax.numpy as jnp
from jax import lax
from jax.experimental import pallas as pl
from jax.experimental.pallas import tpu as pltpu

F32 = jnp.float32
BF16 = jnp.bfloat16
_MXU_DTYPE = jnp.bfloat16

N_DEV = 8
D_MODEL = 1024
DEPTH = 4
CHUNK = 64
GLA_STEP_CHUNKS, SSD_STEP_CHUNKS = 2, 1
GLA_STEP, SSD_STEP = CHUNK * GLA_STEP_CHUNKS, CHUNK * SSD_STEP_CHUNKS
EPS = 1e-6
GLA_HEADS, GLA_DK, GLA_DV, GLA_RANK, GLA_TAU = 4, 128, 256, 16, 16.0
GLA_QK = GLA_HEADS * GLA_DK
GLA_VD = GLA_HEADS * GLA_DV
LANES = 128
GLA_IN = 2 * GLA_QK + 2 * GLA_VD + GLA_RANK
GLA_PROJ = 2 * GLA_QK + 2 * GLA_VD + LANES
SSD_DINNER, SSD_HEADDIM, SSD_HEADS, SSD_GROUPS, SSD_HPG, SSD_DSTATE, SSD_CONV = 2048, 64, 32, 8, 4, 128, 4
SSD_GN = SSD_GROUPS * SSD_DSTATE
SSD_GW = SSD_HPG * SSD_HEADDIM
SSD_XBC = SSD_DINNER + 2 * SSD_GN
SSD_IN = SSD_DINNER + SSD_XBC + SSD_HEADS
SSD_PROJ = SSD_DINNER + SSD_XBC + LANES
S5_GROUP, S5_GROUPS, S5_STATE = 16, 64, 64
S5_CHUNK = 16
FFN_HIDDEN = 2816
ADAM_LR, ADAM_B1, ADAM_B2, ADAM_EPS, ADAM_WD, ADAM_STEP = 0.001, 0.9, 0.999, 1e-08, 0.01, 10
VMEM_LIMIT = 48 * 1024 * 1024
FLAT_COLS = 1024
FLAT_ROWS_ALIGN = 64


def _tile(n, cap, unit):
    if n <= cap:
        return n
    best = None
    for t in range(unit, cap + 1, unit):
        if n % t == 0:
            best = t
    assert best is not None, (n, cap, unit)
    return best


def _divisors(n, unit):
    return sorted({t for t in range(unit, n + 1, unit) if n % t == 0} | {n})


_MXU_FLOPS, _HBM_BYTES, _ACC_BYTES, _STEP_SECONDS = 1.1e15, 3e12, 1.1e13, 3.5e-7
_MXU_ROWS = 256
_TILE_VMEM_BUDGET = 36 * 1024 * 1024
_BATCH_VMEM_BUDGET = 16 * 1024 * 1024


def _pick_tiles(m, n, k, a_bytes, b_bytes, o_bytes, m_unit):
    best = None
    for tm in _divisors(m, m_unit):
        for tn in _divisors(n, LANES):
            for tk in _divisors(k, LANES):
                nk = k // tk
                vmem = 2 * tm * tk * a_bytes + 2 * tk * tn * b_bytes + 2 * tm * tn * o_bytes + (nk > 1) * tm * tn * 4
                if vmem > _TILE_VMEM_BUDGET or tm > 2048 or tn > 2048:
                    continue
                a_reads = n // tn if nk > 1 else 1
                b_reads = 1 if (nk == 1 and n == tn) else m // tm
                traffic = m * k * a_bytes * a_reads + k * n * b_bytes * b_reads + m * n * o_bytes
                mxu = 2.0 * m * n * k / _MXU_FLOPS * (1.0 + _MXU_ROWS / tm)
                cost = (max(mxu, traffic / _HBM_BYTES) + (nk > 1) * nk * m * n * 8 / _ACC_BYTES
                        + (m // tm) * (n // tn) * nk * _STEP_SECONDS)
                if best is None or cost < best[0]:
                    best = (cost, tm, tn, tk)
    assert best is not None, (m, n, k)
    return best[1:]


def _cargo_call(body, cargo, *, name, grid, in_specs, out_specs, out_shape, scratch_shapes, semantics):
    params = lambda sem: pltpu.CompilerParams(dimension_semantics=sem, vmem_limit_bytes=VMEM_LIMIT)
    if not cargo:
        return pl.pallas_call(body, name=name, grid=grid, in_specs=in_specs, out_specs=out_specs, out_shape=out_shape,
                              scratch_shapes=scratch_shapes, compiler_params=params(semantics))
    n_in, n_out, n_scratch, n_cargo = len(in_specs), len(out_specs), len(scratch_shapes), len(cargo)

    def loaded(*refs):
        ins, cargo_in, rest = refs[:n_in], refs[n_in:n_in + n_cargo], refs[n_in + n_cargo:]
        outs, cargo_out, rest = rest[:n_out], rest[n_out:n_out + n_cargo], rest[n_out + n_cargo:]
        scratch, sems = rest[:n_scratch], rest[n_scratch:]
        ids = [pl.program_id(d) for d in range(len(grid))]
        first = functools.reduce(jnp.logical_and, [i == 0 for i in ids])
        last = functools.reduce(jnp.logical_and, [i == g - 1 for i, g in zip(ids, grid)])
        moves = lambda: [_moves(kind, x_ref, y_ref, *sems[3 * c:3 * c + 3])
                         for c, ((kind, _), x_ref, y_ref) in enumerate(zip(cargo, cargo_in, cargo_out))]

        @pl.when(first)
        def _():
            for mv in moves():
                _start(mv)

        body(*ins, *outs, *scratch)

        @pl.when(last)
        def _():
            for mv in moves():
                _finish(mv)

    sems = [pltpu.SemaphoreType.DMA((N_DEV - 1,)), pltpu.SemaphoreType.DMA((N_DEV - 1,)), pltpu.SemaphoreType.DMA] * n_cargo
    call = pl.pallas_call(
        loaded, name=name, grid=grid, in_specs=list(in_specs) + [_ANY] * n_cargo,
        out_specs=list(out_specs) + [_ANY] * n_cargo,
        out_shape=list(out_shape) + [_moved_shape(kind, x) for kind, x in cargo],
        scratch_shapes=list(scratch_shapes) + sems, compiler_params=params(("arbitrary",) * len(grid)))

    def run(*args):
        results = call(*args, *[x for _, x in cargo])
        return list(results[:n_out]), list(results[n_out:])

    return run


def _matmul(a, b, *, ta=False, tb=False, out_dtype=F32, name, cargo=()):
    batched = a.ndim == 3
    if ta:
        k_dim, m_dim = a.shape[-2:]
    else:
        m_dim, k_dim = a.shape[-2:]
    if tb:
        n_dim, kb = b.shape[-2:]
    else:
        kb, n_dim = b.shape[-2:]
    assert kb == k_dim, (a.shape, b.shape, ta, tb)
    tm, tn, tk = _pick_tiles(m_dim, n_dim, k_dim, a.dtype.itemsize, b.dtype.itemsize, jnp.dtype(out_dtype).itemsize,
                             LANES if ta else 16)
    nk = k_dim // tk
    ca, cb = (0 if ta else 1), (1 if tb else 0)
    grid = (m_dim // tm, n_dim // tn, nk)
    gb = 1
    if batched:
        step_bytes = 2 * (tm * tk * a.dtype.itemsize + tk * tn * b.dtype.itemsize + tm * tn * jnp.dtype(out_dtype).itemsize)
        gb = max(g for g in _divisors(a.shape[0], 1) if g * step_bytes <= _BATCH_VMEM_BUDGET or g == 1)
        grid = (a.shape[0] // gb,) + grid
    dims = (((ca + 1,), (cb + 1,)), ((0,), (0,))) if batched else (((ca,), (cb,)), ((), ()))

    def body(a_ref, b_ref, o_ref, *acc):
        part = lax.dot_general(a_ref[...].astype(_MXU_DTYPE), b_ref[...].astype(_MXU_DTYPE), dims,
                               preferred_element_type=F32)
        if nk == 1:
            o_ref[...] = part.astype(o_ref.dtype)
            return
        acc_ref, = acc
        k = pl.program_id(len(grid) - 1)

        @pl.when(k == 0)
        def _():
            acc_ref[...] = part

        @pl.when(k > 0)
        def _():
            acc_ref[...] += part

        @pl.when(k == nk - 1)
        def _():
            o_ref[...] = acc_ref[...].astype(o_ref.dtype)

    def spec(shape, fn):
        if batched:
            return pl.BlockSpec((gb,) + shape, lambda g, i, j, k: (g,) + fn(i, j, k))
        return pl.BlockSpec(shape, fn)

    a_spec = spec((tk, tm), lambda i, j, k: (k, i)) if ta else spec((tm, tk), lambda i, j, k: (i, k))
    b_spec = spec((tn, tk), lambda i, j, k: (j, k)) if tb else spec((tk, tn), lambda i, j, k: (k, j))
    o_spec = spec((tm, tn), lambda i, j, k: (i, j))
    out_shape = ((a.shape[0],) if batched else ()) + (m_dim, n_dim)
    call = _cargo_call(
        body, cargo, name=name, grid=grid, in_specs=[a_spec, b_spec], out_specs=[o_spec],
        out_shape=[jax.ShapeDtypeStruct(out_shape, out_dtype)],
        scratch_shapes=[pltpu.VMEM(((gb,) if batched else ()) + (tm, tn), F32)] if nk > 1 else [],
        semantics=("parallel",) * (len(grid) - 1) + ("arbitrary",))
    if not cargo:
        return call(a, b)[0]
    results, moved = call(a, b)
    return results[0], moved


def _dot(a, b, ca=1, cb=0, exact=False):
    if exact:
        return lax.dot_general(a, b, (((ca,), (cb,)), ((), ())), precision=lax.Precision.HIGHEST,
                               preferred_element_type=F32)
    return lax.dot_general(a.astype(_MXU_DTYPE), b.astype(_MXU_DTYPE), (((ca,), (cb,)), ((), ())),
                           preferred_element_type=F32)


def _tri(n):
    return lax.broadcasted_iota(jnp.int32, (n, n), 0) >= lax.broadcasted_iota(jnp.int32, (n, n), 1)


def _log_sigmoid(x):
    return jnp.minimum(x, 0.0) - jnp.log(1.0 + jnp.exp(-jnp.abs(x)))


def _softplus(x):
    return jnp.maximum(x, 0.0) + jnp.log(1.0 + jnp.exp(-jnp.abs(x)))


def _silu(x):
    return x / (1.0 + jnp.exp(-x))


def _full_spec(shape):
    return pl.BlockSpec(shape, lambda c: (0,) * len(shape))


def _gla_chunk(proj, st, w_a2, b_a, norm_g):
    t = proj.shape[0]
    q = proj[:, 0:GLA_QK] * (GLA_DK ** -0.5)
    k = proj[:, GLA_QK:2 * GLA_QK]
    v = proj[:, 2 * GLA_QK:2 * GLA_QK + GLA_VD]
    r = proj[:, 2 * GLA_QK + GLA_VD:2 * GLA_QK + 2 * GLA_VD]
    a_low = proj[:, 2 * GLA_QK + 2 * GLA_VD:]
    log_a = _log_sigmoid(_dot(a_low, w_a2) + b_a) * (1.0 / GLA_TAU)
    past = _tri(t)
    lc = _dot(past.astype(F32), log_a, exact=True)
    lend = lc[t - 1:t, :]
    e_pos = jnp.exp(lc)
    e_neg = jnp.exp(-lc)
    q_fwd, k_fwd, q_bwd, k_bwd = q * e_pos, k * e_neg, q * e_neg, k * e_pos
    kd = k * jnp.exp(lend - lc)
    g = jnp.exp(lend)
    outs, new_st = [], []
    for h in range(GLA_HEADS):
        sk = slice(h * GLA_DK, (h + 1) * GLA_DK)
        sv = slice(h * GLA_DV, (h + 1) * GLA_DV)
        s_past = _dot(q_fwd[:, sk], k_fwd[:, sk], 1, 1)
        s_future = _dot(q_bwd[:, sk], k_bwd[:, sk], 1, 1)
        scores = jnp.where(past, s_past, s_future)
        o = _dot(scores, v[:, sv]) + _dot(q_fwd[:, sk], st[h], 1, 1)
        new_st.append(st[h] * g[:, sk] + _dot(v[:, sv], kd[:, sk], 0, 0))
        o = o * lax.rsqrt(jnp.mean(o * o, axis=-1, keepdims=True) + EPS) * norm_g[:, sv]
        outs.append(o)
    return jnp.concatenate(outs, axis=1) * _silu(r), tuple(new_st)


_GLA_STATE = (GLA_HEADS, GLA_DV, GLA_DK)


def _gla_step(proj, st, w_a2, b_a, norm_g):
    outs = []
    for s in range(GLA_STEP_CHUNKS):
        out, st = _gla_chunk(proj[s * CHUNK:(s + 1) * CHUNK], st, w_a2, b_a, norm_g)
        outs.append(out)
    return jnp.concatenate(outs, axis=0), st


def _gla_core_fwd(proj, w_a2, b_a, norm_g, cargo=()):
    seq = proj.shape[0]
    nc = seq // GLA_STEP

    def body(proj_ref, wa_ref, ba_ref, ng_ref, o_ref, sprev_ref, st_ref):
        @pl.when(pl.program_id(0) == 0)
        def _():
            st_ref[...] = jnp.zeros_like(st_ref)

        st = tuple(st_ref[h] for h in range(GLA_HEADS))
        for h in range(GLA_HEADS):
            sprev_ref[0, h] = st[h]
        out, new_st = _gla_step(proj_ref[...], st, wa_ref[...], ba_ref[...], ng_ref[...])
        o_ref[...] = out
        for h in range(GLA_HEADS):
            st_ref[h] = new_st[h]

    return _cargo_call(
        body, cargo, name="gla_core_fwd", grid=(nc,),
        in_specs=[pl.BlockSpec((GLA_STEP, GLA_PROJ), lambda c: (c, 0)), _full_spec(w_a2.shape), _full_spec(b_a.shape),
                  _full_spec(norm_g.shape)],
        out_specs=[pl.BlockSpec((GLA_STEP, GLA_VD), lambda c: (c, 0)), pl.BlockSpec((1,) + _GLA_STATE, lambda c: (c, 0, 0, 0))],
        out_shape=[jax.ShapeDtypeStruct((seq, GLA_VD), F32), jax.ShapeDtypeStruct((nc,) + _GLA_STATE, F32)],
        scratch_shapes=[pltpu.VMEM(_GLA_STATE, F32)],
        semantics=("arbitrary",),
    )(proj, w_a2, b_a, norm_g)


def _gla_core_bwd(proj, sprev, d_out, w_a2, b_a, norm_g, cargo=()):
    seq = proj.shape[0]
    nc = seq // GLA_STEP

    def body(proj_ref, sprev_ref, do_ref, wa_ref, ba_ref, ng_ref, dproj_ref, dwa_ref, dba_ref, dng_ref, dst_ref):
        @pl.when(pl.program_id(0) == 0)
        def _():
            dst_ref[...] = jnp.zeros_like(dst_ref)
            dwa_ref[...] = jnp.zeros_like(dwa_ref)
            dba_ref[...] = jnp.zeros_like(dba_ref)
            dng_ref[...] = jnp.zeros_like(dng_ref)

        st = tuple(sprev_ref[0, h] for h in range(GLA_HEADS))
        _, vjp = jax.vjp(_gla_step, proj_ref[...], st, wa_ref[...], ba_ref[...], ng_ref[...])
        d_next = tuple(dst_ref[h] for h in range(GLA_HEADS))
        d_proj, d_st, d_wa, d_ba, d_ng = vjp((do_ref[...], d_next))
        dproj_ref[...] = d_proj.astype(dproj_ref.dtype)
        for h in range(GLA_HEADS):
            dst_ref[h] = d_st[h]
        dwa_ref[...] += d_wa
        dba_ref[...] += d_ba
        dng_ref[...] += d_ng

    rev = lambda c: (nc - 1 - c, 0)
    return _cargo_call(
        body, cargo, name="gla_core_bwd", grid=(nc,),
        in_specs=[pl.BlockSpec((GLA_STEP, GLA_PROJ), rev), pl.BlockSpec((1,) + _GLA_STATE, lambda c: (nc - 1 - c, 0, 0, 0)),
                  pl.BlockSpec((GLA_STEP, GLA_VD), rev), _full_spec(w_a2.shape), _full_spec(b_a.shape), _full_spec(norm_g.shape)],
        out_specs=[pl.BlockSpec((GLA_STEP, GLA_PROJ), rev), _full_spec(w_a2.shape), _full_spec(b_a.shape), _full_spec(norm_g.shape)],
        out_shape=[jax.ShapeDtypeStruct((seq, GLA_PROJ), BF16), jax.ShapeDtypeStruct(w_a2.shape, F32),
                   jax.ShapeDtypeStruct(b_a.shape, F32), jax.ShapeDtypeStruct(norm_g.shape, F32)],
        scratch_shapes=[pltpu.VMEM(_GLA_STATE, F32)],
        semantics=("arbitrary",),
    )(proj, sprev, d_out, w_a2, b_a, norm_g)


def _ssd_chunk(z, xbc, dt_raw, hs, dt_bias, a_log, d_skip, norm_g):
    t = z.shape[0]
    xs = xbc[:, :SSD_DINNER]
    bm = xbc[:, SSD_DINNER:SSD_DINNER + SSD_GN]
    cm = xbc[:, SSD_DINNER + SSD_GN:]
    dt = _softplus(dt_raw + dt_bias)
    da = dt * (-jnp.exp(a_log))
    tri = _tri(t).astype(F32)
    eye = (lax.broadcasted_iota(jnp.int32, (t, t), 0) == lax.broadcasted_iota(jnp.int32, (t, t), 1)).astype(F32)
    cum = _dot(tri, da, exact=True)
    cum_t = _dot(da, tri, 0, 1, exact=True)
    dt_t = _dot(dt, eye, 0, 0, exact=True)
    cum_end = cum[t - 1:t, :]
    w_state = dt * jnp.exp(cum_end - cum)
    e_cum = jnp.exp(cum)
    g_end = jnp.exp(cum_end)
    head_of = lambda axis: lax.shift_right_logical(lax.broadcasted_iota(jnp.int32, (SSD_GW, SSD_GW), axis),
                                                   jnp.int32(SSD_HEADDIM.bit_length() - 1))
    same_head = head_of(0) == head_of(1)
    ys, new_hs = [], []
    for g in range(SSD_GROUPS):
        heads = range(g * SSD_HPG, (g + 1) * SSD_HPG)
        cols = slice(g * SSD_GW, (g + 1) * SSD_GW)

        def spread(a):
            return jnp.concatenate([jnp.broadcast_to(a[:, h:h + 1], (a.shape[0], SSD_HEADDIM)) for h in heads], axis=1)

        def row(a_t):
            return jnp.concatenate([a_t[h:h + 1, :] for h in heads], axis=1)

        bm_g = bm[:, g * SSD_DSTATE:(g + 1) * SSD_DSTATE]
        cm_g = cm[:, g * SSD_DSTATE:(g + 1) * SSD_DSTATE]
        xs_g = xs[:, cols]
        cb = _dot(cm_g, jnp.concatenate([bm_g] * SSD_HPG, axis=0), 1, 1)
        mix = cb * jnp.exp(-jnp.abs(spread(cum) - row(cum_t))) * row(dt_t)
        x_diag = jnp.where(same_head, jnp.concatenate([xs_g] * SSD_HPG, axis=0), 0.0)
        y = _dot(mix, x_diag)
        y = y + _dot(cm_g, hs[g], 1, 1) * spread(e_cum)
        y = y + spread(d_skip) * xs_g
        states = _dot(xs_g * spread(w_state), bm_g, 0, 0)
        decayed = jnp.concatenate([g_end[:, h:h + 1] * hs[g][j * SSD_HEADDIM:(j + 1) * SSD_HEADDIM, :]
                                   for j, h in enumerate(heads)], axis=0)
        new_hs.append(decayed + states)
        yg = y * _silu(z[:, cols])
        ys.append(yg * lax.rsqrt(jnp.mean(yg * yg, axis=-1, keepdims=True) + EPS) * norm_g[:, cols])
    return jnp.concatenate(ys, axis=1), tuple(new_hs)


_SSD_STATE = (SSD_GROUPS, SSD_GW, SSD_DSTATE)


def _ssd_step(z, xbc, dt_raw, hs, dt_bias, a_log, d_skip, norm_g):
    outs = []
    for s in range(SSD_STEP_CHUNKS):
        rows = slice(s * CHUNK, (s + 1) * CHUNK)
        out, hs = _ssd_chunk(z[rows], xbc[rows], dt_raw[rows], hs, dt_bias, a_log, d_skip, norm_g)
        outs.append(out)
    return jnp.concatenate(outs, axis=0), hs
_SSD_DT_BLOCK = (SSD_DINNER + SSD_XBC) // LANES


def _ssd_core_fwd(proj, xbc, dt_bias, a_log, d_skip, norm_g, cargo=()):
    seq = proj.shape[0]
    nc = seq // SSD_STEP

    def body(z_ref, xbc_ref, dt_ref, db_ref, al_ref, ds_ref, ng_ref, o_ref, hprev_ref, hs_ref):
        @pl.when(pl.program_id(0) == 0)
        def _():
            hs_ref[...] = jnp.zeros_like(hs_ref)

        hs = tuple(hs_ref[g] for g in range(SSD_GROUPS))
        for g in range(SSD_GROUPS):
            hprev_ref[0, g] = hs[g]
        out, new_hs = _ssd_step(z_ref[...], xbc_ref[...], dt_ref[...], hs, db_ref[...], al_ref[...], ds_ref[...], ng_ref[...])
        o_ref[...] = out
        for g in range(SSD_GROUPS):
            hs_ref[g] = new_hs[g]

    return _cargo_call(
        body, cargo, name="ssd_core_fwd", grid=(nc,),
        in_specs=[pl.BlockSpec((SSD_STEP, SSD_DINNER), lambda c: (c, 0)), pl.BlockSpec((SSD_STEP, SSD_XBC), lambda c: (c, 0)),
                  pl.BlockSpec((SSD_STEP, LANES), lambda c: (c, _SSD_DT_BLOCK)),
                  _full_spec(dt_bias.shape), _full_spec(a_log.shape), _full_spec(d_skip.shape), _full_spec(norm_g.shape)],
        out_specs=[pl.BlockSpec((SSD_STEP, SSD_DINNER), lambda c: (c, 0)), pl.BlockSpec((1,) + _SSD_STATE, lambda c: (c, 0, 0, 0))],
        out_shape=[jax.ShapeDtypeStruct((seq, SSD_DINNER), F32), jax.ShapeDtypeStruct((nc,) + _SSD_STATE, F32)],
        scratch_shapes=[pltpu.VMEM(_SSD_STATE, F32)],
        semantics=("arbitrary",),
    )(proj, xbc, proj, dt_bias, a_log, d_skip, norm_g)


def _ssd_core_bwd(proj, xbc, hprev, d_out, dt_bias, a_log, d_skip, norm_g, cargo=()):
    seq = proj.shape[0]
    nc = seq // SSD_STEP

    def body(z_ref, xbc_ref, dt_ref, hprev_ref, do_ref, db_ref, al_ref, ds_ref, ng_ref,
             dz_ref, dxbc_ref, ddt_ref, ddb_ref, dal_ref, dds_ref, dng_ref, dhs_ref):
        @pl.when(pl.program_id(0) == 0)
        def _():
            dhs_ref[...] = jnp.zeros_like(dhs_ref)
            ddb_ref[...] = jnp.zeros_like(ddb_ref)
            dal_ref[...] = jnp.zeros_like(dal_ref)
            dds_ref[...] = jnp.zeros_like(dds_ref)
            dng_ref[...] = jnp.zeros_like(dng_ref)

        hs = tuple(hprev_ref[0, g] for g in range(SSD_GROUPS))
        _, vjp = jax.vjp(_ssd_step, z_ref[...], xbc_ref[...], dt_ref[...], hs, db_ref[...], al_ref[...], ds_ref[...], ng_ref[...])
        d_next = tuple(dhs_ref[g] for g in range(SSD_GROUPS))
        d_z, d_xbc, d_dt, d_hs, d_db, d_al, d_ds, d_ng = vjp((do_ref[...], d_next))
        dz_ref[...] = d_z.astype(dz_ref.dtype)
        dxbc_ref[...] = d_xbc
        ddt_ref[...] = d_dt.astype(ddt_ref.dtype)
        for g in range(SSD_GROUPS):
            dhs_ref[g] = d_hs[g]
        ddb_ref[...] += d_db
        dal_ref[...] += d_al
        dds_ref[...] += d_ds
        dng_ref[...] += d_ng

    rev = lambda c: (nc - 1 - c, 0)
    vec = [_full_spec(dt_bias.shape), _full_spec(a_log.shape), _full_spec(d_skip.shape), _full_spec(norm_g.shape)]
    return _cargo_call(
        body, cargo, name="ssd_core_bwd", grid=(nc,),
        in_specs=[pl.BlockSpec((SSD_STEP, SSD_DINNER), rev), pl.BlockSpec((SSD_STEP, SSD_XBC), rev),
                  pl.BlockSpec((SSD_STEP, LANES), lambda c: (nc - 1 - c, _SSD_DT_BLOCK)),
                  pl.BlockSpec((1,) + _SSD_STATE, lambda c: (nc - 1 - c, 0, 0, 0)),
                  pl.BlockSpec((SSD_STEP, SSD_DINNER), rev)] + vec,
        out_specs=[pl.BlockSpec((SSD_STEP, SSD_DINNER), rev), pl.BlockSpec((SSD_STEP, SSD_XBC), rev),
                   pl.BlockSpec((SSD_STEP, LANES), rev)] + vec,
        out_shape=[jax.ShapeDtypeStruct((seq, SSD_DINNER), BF16), jax.ShapeDtypeStruct((seq, SSD_XBC), F32),
                   jax.ShapeDtypeStruct((seq, LANES), BF16),
                   jax.ShapeDtypeStruct(dt_bias.shape, F32), jax.ShapeDtypeStruct(a_log.shape, F32),
                   jax.ShapeDtypeStruct(d_skip.shape, F32), jax.ShapeDtypeStruct(norm_g.shape, F32)],
        scratch_shapes=[pltpu.VMEM(_SSD_STATE, F32)],
        semantics=("arbitrary",),
    )(proj, xbc, proj, hprev, d_out, dt_bias, a_log, d_skip, norm_g)


CONV_COLS = 2048
CONV_HALO = 8


def _conv_taps(xx, rows):
    last = SSD_CONV - 1
    return [pltpu.roll(xx, last - k, 0)[CONV_HALO:CONV_HALO + rows] if k < last else xx[CONV_HALO:CONV_HALO + rows]
            for k in range(SSD_CONV)]


def _ssd_conv_fwd(proj, conv_w, conv_b, name):
    rows = proj.shape[0]
    tr = _tile(rows, 512, CONV_HALO)
    first_col = SSD_DINNER // CONV_COLS

    def body(x_ref, halo_ref, w_ref, b_ref, o_ref):
        halo = jnp.where(pl.program_id(0) == 0, 0.0, halo_ref[...])
        taps = _conv_taps(jnp.concatenate([halo, x_ref[...]], axis=0), tr)
        out = b_ref[...]
        for k in range(SSD_CONV):
            out = out + taps[k] * w_ref[k:k + 1, :]
        o_ref[...] = _silu(out)

    return pl.pallas_call(
        body, name=name, grid=(rows // tr, SSD_XBC // CONV_COLS),
        in_specs=[pl.BlockSpec((tr, CONV_COLS), lambda i, j: (i, first_col + j)),
                  pl.BlockSpec((CONV_HALO, CONV_COLS), lambda i, j: (jnp.maximum(i * (tr // CONV_HALO) - 1, 0), first_col + j)),
                  pl.BlockSpec((SSD_CONV, CONV_COLS), lambda i, j: (0, j)), pl.BlockSpec((1, CONV_COLS), lambda i, j: (0, j))],
        out_specs=pl.BlockSpec((tr, CONV_COLS), lambda i, j: (i, j)),
        out_shape=jax.ShapeDtypeStruct((rows, SSD_XBC), F32),
        compiler_params=pltpu.CompilerParams(dimension_semantics=("parallel", "parallel"), vmem_limit_bytes=VMEM_LIMIT),
    )(proj, proj, conv_w, conv_b[None])


def _ssd_conv_bwd(proj, d_xbc, conv_w, conv_b, name):
    rows = proj.shape[0]
    tr = _tile(rows, 512, CONV_HALO)
    nb, halos = rows // tr, tr // CONV_HALO
    first_col = SSD_DINNER // CONV_COLS

    def body(x_ref, before_ref, after_ref, d_ref, d_after_ref, w_ref, b_ref, dx_ref, dw_ref, db_ref):
        i = pl.program_id(1)

        @pl.when(i == 0)
        def _():
            dw_ref[...] = jnp.zeros_like(dw_ref)
            db_ref[...] = jnp.zeros_like(db_ref)

        before = jnp.where(i == 0, 0.0, before_ref[...])
        taps = _conv_taps(jnp.concatenate([before, x_ref[...], after_ref[...]], axis=0), tr + CONV_HALO)
        out = b_ref[...]
        for k in range(SSD_CONV):
            out = out + taps[k] * w_ref[k:k + 1, :]
        sig = 1.0 / (1.0 + jnp.exp(-out))
        d_after = jnp.where(i == nb - 1, 0.0, d_after_ref[...])
        d_out = jnp.concatenate([d_ref[...], d_after], axis=0) * sig * (1.0 + out * (1.0 - sig))
        d_x = d_out[:tr] * w_ref[SSD_CONV - 1:SSD_CONV, :]
        for k in range(SSD_CONV - 1):
            ahead = SSD_CONV - 1 - k
            d_x = d_x + pltpu.roll(d_out, tr + CONV_HALO - ahead, 0)[:tr] * w_ref[k:k + 1, :]
        dx_ref[...] = d_x.astype(dx_ref.dtype)
        for k in range(SSD_CONV):
            dw_ref[k:k + 1, :] += jnp.sum(d_out[:tr] * taps[k][:tr], axis=0, keepdims=True)
        db_ref[...] += jnp.sum(d_out[:tr], axis=0, keepdims=True)

    before = lambda j, i: jnp.maximum(i * halos - 1, 0)
    after = lambda j, i: jnp.minimum((i + 1) * halos, nb * halos - 1)
    d_x, d_w, d_b = pl.pallas_call(
        body, name=name, grid=(SSD_XBC // CONV_COLS, nb),
        in_specs=[pl.BlockSpec((tr, CONV_COLS), lambda j, i: (i, first_col + j)),
                  pl.BlockSpec((CONV_HALO, CONV_COLS), lambda j, i: (before(j, i), first_col + j)),
                  pl.BlockSpec((CONV_HALO, CONV_COLS), lambda j, i: (after(j, i), first_col + j)),
                  pl.BlockSpec((tr, CONV_COLS), lambda j, i: (i, j)),
                  pl.BlockSpec((CONV_HALO, CONV_COLS), lambda j, i: (after(j, i), j)),
                  pl.BlockSpec((SSD_CONV, CONV_COLS), lambda j, i: (0, j)), pl.BlockSpec((1, CONV_COLS), lambda j, i: (0, j))],
        out_specs=[pl.BlockSpec((tr, CONV_COLS), lambda j, i: (i, j)), pl.BlockSpec((SSD_CONV, CONV_COLS), lambda j, i: (0, j)),
                   pl.BlockSpec((1, CONV_COLS), lambda j, i: (0, j))],
        out_shape=[jax.ShapeDtypeStruct((rows, SSD_XBC), BF16), jax.ShapeDtypeStruct((SSD_CONV, SSD_XBC), F32),
                   jax.ShapeDtypeStruct((1, SSD_XBC), F32)],
        compiler_params=pltpu.CompilerParams(dimension_semantics=("parallel", "arbitrary"), vmem_limit_bytes=VMEM_LIMIT),
    )(proj, proj, proj, d_xbc, d_xbc, conv_w, conv_b[None])
    return d_x, d_w, d_b[0]


def _s5_boundary_scan(z, lam_re, lam_im, name, reverse=False):
    n_chunks, groups, width = z.shape
    tn = _tile(n_chunks, 128, 1)
    blocks = n_chunks // tn
    lam_a = jnp.concatenate([lam_re, lam_re], axis=1)
    lam_b = jnp.concatenate([-lam_im, lam_im], axis=1)

    def body(z_ref, a_ref, b_ref, x_ref, carry_ref):
        @pl.when(pl.program_id(0) == 0)
        def _():
            carry_ref[...] = jnp.zeros_like(carry_ref)

        a, b = a_ref[...], b_ref[...]

        def step(i, x):
            n = tn - 1 - i if reverse else i
            x_ref[n] = x
            return a * x + b * pltpu.roll(x, width // 2, 1) + z_ref[n]

        carry_ref[...] = lax.fori_loop(0, tn, step, carry_ref[...])

    block = pl.BlockSpec((tn, groups, width), (lambda i: (blocks - 1 - i, 0, 0)) if reverse else (lambda i: (i, 0, 0)))
    return pl.pallas_call(
        body, name=name, grid=(blocks,), in_specs=[block, _full_spec((groups, width)), _full_spec((groups, width))],
        out_specs=block, out_shape=jax.ShapeDtypeStruct(z.shape, F32), scratch_shapes=[pltpu.VMEM((groups, width), F32)],
        compiler_params=pltpu.CompilerParams(dimension_semantics=("arbitrary",), vmem_limit_bytes=VMEM_LIMIT),
    )(z, lam_a, lam_b)


_FLIPS = [(kx, ky, kc) for kx in (0, 1) for ky in (0, 1) for kc in (0, 1)][1:]


def _mesh_position():
    return lax.axis_index("x"), lax.axis_index("y"), lax.axis_index("c")


def _peer(pos, flip):
    return tuple((1 - p) if f else p for p, f in zip(pos, flip))


def _index(pos):
    return 4 * pos[0] + 2 * pos[1] + pos[2]


_ANY = pl.BlockSpec(memory_space=pl.ANY)


def _moved_shape(kind, x):
    return jax.ShapeDtypeStruct(((N_DEV,) + x.shape) if kind == "gather" else x.shape, x.dtype)


def _moves(kind, x_ref, out_ref, send_sems, recv_sems, local_sem):
    me = _mesh_position()
    source = (lambda pos: x_ref) if kind == "gather" else (lambda pos: x_ref.at[_index(pos)])
    local = pltpu.make_async_copy(source(me), out_ref.at[_index(me)], local_sem)
    outgoing, incoming = [], []
    for k, flip in enumerate(_FLIPS):
        peer = _peer(me, flip)
        copy = lambda slot: pltpu.make_async_remote_copy(
            src_ref=source(peer), dst_ref=out_ref.at[_index(slot)], send_sem=send_sems.at[k], recv_sem=recv_sems.at[k],
            device_id=peer, device_id_type=pl.DeviceIdType.MESH)
        outgoing.append(copy(me))
        incoming.append(copy(peer))
    return local, outgoing, incoming


def _start(moves):
    local, outgoing, _ = moves
    local.start()
    for cp in outgoing:
        cp.start()


def _finish(moves):
    local, outgoing, incoming = moves
    for cp in incoming:
        cp.wait_recv()
    for cp in outgoing:
        cp.wait_send()
    local.wait()


def _collective(kind, x, name):
    def body(x_ref, out_ref, send_sems, recv_sems, local_sem):
        moves = _moves(kind, x_ref, out_ref, send_sems, recv_sems, local_sem)
        _start(moves)
        _finish(moves)

    return pl.pallas_call(
        body, name=name, in_specs=[_ANY], out_specs=_ANY, out_shape=_moved_shape(kind, x),
        scratch_shapes=[pltpu.SemaphoreType.DMA((N_DEV - 1,)), pltpu.SemaphoreType.DMA((N_DEV - 1,)), pltpu.SemaphoreType.DMA],
        compiler_params=pltpu.CompilerParams(has_side_effects=True),
    )(x)


def _adamw(parts, w, m, v, name):
    n_parts = parts.shape[0]
    layers, rows, cols = w.shape
    tr = _tile(rows, 256, 8)

    def body(p_ref, w_ref, m_ref, v_ref, g_ref, d_ref, mo_ref, vo_ref):
        g = p_ref[0].astype(F32)
        for s in range(1, n_parts):
            g = g + p_ref[s].astype(F32)
        m_new = ADAM_B1 * m_ref[...] + (1.0 - ADAM_B1) * g
        v_new = ADAM_B2 * v_ref[...] + (1.0 - ADAM_B2) * (g * g)
        m_hat = m_new / (1.0 - ADAM_B1 ** ADAM_STEP)
        v_hat = v_new / (1.0 - ADAM_B2 ** ADAM_STEP)
        g_ref[...] = g
        d_ref[...] = -ADAM_LR * (m_hat / (jnp.sqrt(v_hat) + ADAM_EPS) + ADAM_WD * w_ref[...])
        mo_ref[...] = m_new
        vo_ref[...] = v_new

    blk = pl.BlockSpec((None, tr, cols), lambda l, i: (l, i, 0))
    shape = jax.ShapeDtypeStruct(w.shape, F32)
    return pl.pallas_call(
        body, name=name, grid=(layers, rows // tr),
        in_specs=[pl.BlockSpec((n_parts, None, tr, cols), lambda l, i: (0, l, i, 0)), blk, blk, blk],
        out_specs=[blk, blk, blk, blk], out_shape=[shape, shape, shape, shape],
        compiler_params=pltpu.CompilerParams(dimension_semantics=("parallel", "parallel"), vmem_limit_bytes=VMEM_LIMIT),
    )(parts, w, m, v)


def _sum_parts(parts, name):
    _, rows, cols = parts.shape
    tr = _tile(rows, 256, 8)

    def body(p_ref, o_ref):
        total = p_ref[0]
        for s in range(1, N_DEV):
            total = total + p_ref[s]
        o_ref[...] = total

    return pl.pallas_call(
        body, name=name, grid=(rows // tr,),
        in_specs=[pl.BlockSpec((N_DEV, tr, cols), lambda i: (0, i, 0))], out_specs=pl.BlockSpec((tr, cols), lambda i: (i, 0)),
        out_shape=jax.ShapeDtypeStruct((rows, cols), parts.dtype),
        compiler_params=pltpu.CompilerParams(dimension_semantics=("parallel",), vmem_limit_bytes=VMEM_LIMIT),
    )(parts)


def _row_tile(rows):
    return _tile(rows, 512, 16)


def _row_spec(rows, cols, block=0):
    return pl.BlockSpec((_row_tile(rows), cols), lambda i: (i, block))


def _rows_params(accumulates):
    return pltpu.CompilerParams(dimension_semantics=("arbitrary" if accumulates else "parallel",),
                                vmem_limit_bytes=VMEM_LIMIT)


def _gu_matmul(x, w, *, name, cargo=()):
    rows, k_dim = x.shape
    tm, tn = _tile(rows, 512, 16), _tile(FFN_HIDDEN, 1408, LANES)
    nj = FFN_HIDDEN // tn

    def body(x_ref, wg_ref, wu_ref, g_ref, u_ref, a_ref):
        xb = x_ref[...].astype(_MXU_DTYPE)
        g = jnp.dot(xb, wg_ref[...].astype(_MXU_DTYPE), preferred_element_type=F32)
        u = jnp.dot(xb, wu_ref[...].astype(_MXU_DTYPE), preferred_element_type=F32)
        g_ref[...] = g.astype(g_ref.dtype)
        u_ref[...] = u.astype(u_ref.dtype)
        a_ref[...] = (_silu(g) * u).astype(a_ref.dtype)

    out = pl.BlockSpec((tm, tn), lambda i, j: (i, j))
    shape = jax.ShapeDtypeStruct((rows, FFN_HIDDEN), BF16)
    return _cargo_call(
        body, cargo, name=name, grid=(rows // tm, nj),
        in_specs=[pl.BlockSpec((tm, k_dim), lambda i, j: (i, 0)), pl.BlockSpec((k_dim, tn), lambda i, j: (0, j)),
                  pl.BlockSpec((k_dim, tn), lambda i, j: (0, nj + j))],
        out_specs=[out, out, out], out_shape=[shape, shape, shape], scratch_shapes=[], semantics=("parallel", "parallel"),
    )(x, w, w)


def _swiglu_bwd(g, u, d_act, name):
    rows = g.shape[0]

    def body(g_ref, u_ref, d_ref, o_ref):
        g, u, d = g_ref[...].astype(F32), u_ref[...].astype(F32), d_ref[...].astype(F32)
        sig = 1.0 / (1.0 + jnp.exp(-g))
        o_ref[:, :FFN_HIDDEN] = (d * u * sig * (1.0 + g * (1.0 - sig))).astype(o_ref.dtype)
        o_ref[:, FFN_HIDDEN:] = (d * g * sig).astype(o_ref.dtype)

    return pl.pallas_call(
        body, name=name, grid=(rows // _row_tile(rows),),
        in_specs=[_row_spec(rows, FFN_HIDDEN)] * 3, out_specs=_row_spec(rows, 2 * FFN_HIDDEN),
        out_shape=jax.ShapeDtypeStruct((rows, 2 * FFN_HIDDEN), BF16), compiler_params=_rows_params(False))(g, u, d_act)


def _add_norm_fwd(h, y, gain, name):
    rows = h.shape[0]

    def body(*refs):
        if y is None:
            h_ref, g_ref, n_ref = refs
            x = h_ref[...]
        else:
            h_ref, y_ref, g_ref, s_ref, n_ref = refs
            x = h_ref[...] + y_ref[...]
            s_ref[...] = x
        n_ref[...] = (x * lax.rsqrt(jnp.mean(x * x, axis=-1, keepdims=True) + EPS) * g_ref[...]).astype(n_ref.dtype)

    row = _row_spec(rows, D_MODEL)
    ins = [h] if y is None else [h, y]
    out_shape = [jax.ShapeDtypeStruct((rows, D_MODEL), BF16)]
    if y is not None:
        out_shape = [jax.ShapeDtypeStruct((rows, D_MODEL), F32)] + out_shape
    res = pl.pallas_call(
        body, name=name, grid=(rows // _row_tile(rows),),
        in_specs=[row] * len(ins) + [_full_spec((1, D_MODEL))], out_specs=[row] * len(out_shape), out_shape=out_shape,
        compiler_params=_rows_params(False))(*ins, gain[None])
    return (h, res[0]) if y is None else (res[0], res[1])


def _norm_bwd(x, gain, d_n, d_skip, name, cargo=()):
    rows = x.shape[0]
    d_parts = d_n if isinstance(d_n, tuple) else (d_n,)

    def body(x_ref, g_ref, *refs):
        dn_refs, (ds_ref, dx_ref, dg_ref) = refs[:len(d_parts)], refs[len(d_parts):]

        @pl.when(pl.program_id(0) == 0)
        def _():
            dg_ref[...] = jnp.zeros_like(dg_ref)

        x, dn = x_ref[...], sum(r[...].astype(F32) for r in dn_refs)
        r = lax.rsqrt(jnp.mean(x * x, axis=-1, keepdims=True) + EPS)
        gd = g_ref[...] * dn
        dx_ref[...] = r * gd - x * (r * r * r) * jnp.mean(x * gd, axis=-1, keepdims=True) + ds_ref[...]
        dg_ref[...] += jnp.sum(x * r * dn, axis=0, keepdims=True)

    row = _row_spec(rows, D_MODEL)
    out = _cargo_call(
        body, cargo, name=name, grid=(rows // _row_tile(rows),),
        in_specs=[row, _full_spec((1, D_MODEL))] + [row] * (len(d_parts) + 1), out_specs=[row, _full_spec((1, D_MODEL))],
        out_shape=[jax.ShapeDtypeStruct((rows, D_MODEL), F32), jax.ShapeDtypeStruct((1, D_MODEL), F32)],
        scratch_shapes=[], semantics=("arbitrary",))(x, gain[None], *d_parts, d_skip)
    if cargo:
        (dx, dg), moved = out
        return (dx, dg[0]), moved
    dx, dg = out
    return dx, dg[0]


_GELU_C, _GELU_A = math.sqrt(2.0 / math.pi), 0.044715


def _s5_gate_fwd(y, u, d_skip, name):
    rows = y.shape[0]

    def body(y_ref, u_ref, d_ref, o_ref):
        x = y_ref[...].astype(F32) + d_ref[...] * u_ref[...].astype(F32)
        o_ref[...] = (0.5 * x * (1.0 + jnp.tanh(_GELU_C * (x + _GELU_A * x * x * x)))).astype(o_ref.dtype)

    row = _row_spec(rows, D_MODEL)
    return pl.pallas_call(
        body, name=name, grid=(rows // _row_tile(rows),), in_specs=[row, row, _full_spec((1, D_MODEL))], out_specs=row,
        out_shape=jax.ShapeDtypeStruct((rows, D_MODEL), BF16), compiler_params=_rows_params(False))(y, u, d_skip[None])


def _s5_gate_bwd(y, u, d_skip, d_act, name):
    rows = y.shape[0]

    def body(y_ref, u_ref, d_ref, da_ref, dy_ref, du_ref, dd_ref):
        @pl.when(pl.program_id(0) == 0)
        def _():
            dd_ref[...] = jnp.zeros_like(dd_ref)

        u = u_ref[...].astype(F32)
        x = y_ref[...].astype(F32) + d_ref[...] * u
        t = jnp.tanh(_GELU_C * (x + _GELU_A * x * x * x))
        slope = 0.5 * (1.0 + t) + 0.5 * x * (1.0 - t * t) * _GELU_C * (1.0 + 3.0 * _GELU_A * x * x)
        dx = da_ref[...].astype(F32) * slope
        dy_ref[...] = dx.astype(dy_ref.dtype)
        du_ref[...] = (dx * d_ref[...]).astype(du_ref.dtype)
        dd_ref[...] += jnp.sum(dx * u, axis=0, keepdims=True)

    row = _row_spec(rows, D_MODEL)
    shape = jax.ShapeDtypeStruct((rows, D_MODEL), BF16)
    d_y, d_u, d_d = pl.pallas_call(
        body, name=name, grid=(rows // _row_tile(rows),), in_specs=[row, row, _full_spec((1, D_MODEL)), row],
        out_specs=[row, row, _full_spec((1, D_MODEL))], out_shape=[shape, shape, jax.ShapeDtypeStruct((1, D_MODEL), F32)],
        compiler_params=_rows_params(True))(y, u, d_skip[None], d_act)
    return d_y, d_u, d_d[0]


def _glu_fwd(vg, name):
    rows = vg.shape[0]

    def body(v_ref, g_ref, o_ref):
        o_ref[...] = v_ref[...] / (1.0 + jnp.exp(-g_ref[...]))

    return pl.pallas_call(
        body, name=name, grid=(rows // _row_tile(rows),),
        in_specs=[_row_spec(rows, D_MODEL, 0), _row_spec(rows, D_MODEL, 1)], out_specs=_row_spec(rows, D_MODEL),
        out_shape=jax.ShapeDtypeStruct((rows, D_MODEL), F32), compiler_params=_rows_params(False))(vg, vg)


def _glu_bwd(vg, d_out, name):
    rows = vg.shape[0]

    def body(v_ref, g_ref, d_ref, o_ref):
        sig = 1.0 / (1.0 + jnp.exp(-g_ref[...]))
        d = d_ref[...]
        o_ref[:, :D_MODEL] = (d * sig).astype(o_ref.dtype)
        o_ref[:, D_MODEL:] = (d * v_ref[...] * sig * (1.0 - sig)).astype(o_ref.dtype)

    return pl.pallas_call(
        body, name=name, grid=(rows // _row_tile(rows),),
        in_specs=[_row_spec(rows, D_MODEL, 0), _row_spec(rows, D_MODEL, 1), _row_spec(rows, D_MODEL)],
        out_specs=_row_spec(rows, 2 * D_MODEL), out_shape=jax.ShapeDtypeStruct((rows, 2 * D_MODEL), BF16),
        compiler_params=_rows_params(False))(vg, vg, d_out)


def _loss_head(h, gain, target, name):
    rows = h.shape[0]

    def body(x_ref, g_ref, t_ref, loss_ref, dx_ref, dg_ref):
        @pl.when(pl.program_id(0) == 0)
        def _():
            loss_ref[...] = jnp.zeros_like(loss_ref)
            dg_ref[...] = jnp.zeros_like(dg_ref)

        x = x_ref[...]
        r = lax.rsqrt(jnp.mean(x * x, axis=-1, keepdims=True) + EPS)
        err = x * r * g_ref[...] - t_ref[...]
        loss_ref[...] += 0.5 * jnp.sum(jnp.mean(err * err, axis=-1, keepdims=True), axis=0, keepdims=True)
        dy = err * (1.0 / D_MODEL)
        gd = g_ref[...] * dy
        dx_ref[...] = r * gd - x * (r * r * r) * jnp.mean(x * gd, axis=-1, keepdims=True)
        dg_ref[...] += jnp.sum(x * r * dy, axis=0, keepdims=True)

    row = _row_spec(rows, D_MODEL)
    loss, dx, dg = pl.pallas_call(
        body, name=name, grid=(rows // _row_tile(rows),),
        in_specs=[row, _full_spec((1, D_MODEL)), row], out_specs=[_full_spec((1, 1)), row, _full_spec((1, D_MODEL))],
        out_shape=[jax.ShapeDtypeStruct((1, 1), F32), jax.ShapeDtypeStruct((rows, D_MODEL), F32),
                   jax.ShapeDtypeStruct((1, D_MODEL), F32)],
        compiler_params=_rows_params(True))(h, gain[None], target)
    return loss[0, 0], dx, dg[0]


def _join_cols(blocks, n_out, name):
    _, layers, rows, n = blocks.shape
    tr = _tile(rows, 256, 16)

    def body(x_ref, o_ref):
        for d in range(N_DEV):
            o_ref[:, d * n:(d + 1) * n] = x_ref[d]
        if n_out > N_DEV * n:
            o_ref[:, N_DEV * n:] = jnp.zeros((tr, n_out - N_DEV * n), o_ref.dtype)

    return pl.pallas_call(
        body, name=name, grid=(layers, rows // tr),
        in_specs=[pl.BlockSpec((N_DEV, None, tr, n), lambda l, i: (0, l, i, 0))],
        out_specs=pl.BlockSpec((None, tr, n_out), lambda l, i: (l, i, 0)),
        out_shape=jax.ShapeDtypeStruct((layers, rows, n_out), blocks.dtype),
        compiler_params=pltpu.CompilerParams(dimension_semantics=("parallel", "parallel"), vmem_limit_bytes=VMEM_LIMIT),
    )(blocks)


def _split_cols(full, n, name):
    rows = full.shape[0]
    tr = _tile(rows, 256, 16)

    def body(x_ref, o_ref):
        for d in range(N_DEV):
            o_ref[d] = x_ref[:, d * n:(d + 1) * n]

    return pl.pallas_call(
        body, name=name, grid=(rows // tr,),
        in_specs=[pl.BlockSpec((tr, full.shape[1]), lambda i: (i, 0))],
        out_specs=pl.BlockSpec((N_DEV, tr, n), lambda i: (0, i, 0)),
        out_shape=jax.ShapeDtypeStruct((N_DEV, rows, n), full.dtype),
        compiler_params=pltpu.CompilerParams(dimension_semantics=("parallel",), vmem_limit_bytes=VMEM_LIMIT),
    )(full)


def _pack(arrays):
    flat = jnp.concatenate([a.reshape(-1) for a in arrays])
    unit = FLAT_COLS * FLAT_ROWS_ALIGN
    padded = -(-flat.shape[0] // unit) * unit
    return jnp.pad(flat, (0, padded - flat.shape[0])).reshape(-1, FLAT_COLS)


def _unpack(flat, shapes, lead=()):
    flat = flat.reshape(lead + (-1,))
    out, off = [], 0
    for shape in shapes:
        n = math.prod(shape)
        out.append(flat[..., off:off + n].reshape(lead + tuple(shape)))
        off += n
    return out


def _join(blocks, axis):
    moved = jnp.moveaxis(blocks, 0, axis)
    shape = list(moved.shape)
    shape[axis:axis + 2] = [shape[axis] * shape[axis + 1]]
    return moved.reshape(shape)


def _own_shard(full, axis, position):
    n = full.shape[axis] // N_DEV
    return lax.dynamic_slice_in_dim(full, position * n, n, axis)


def _s5_operators(log_dt, a_re, a_im, b_re, b_im, c_re, c_im):
    t = S5_CHUNK
    hi = lax.Precision.HIGHEST
    step = jnp.exp(log_dt)[:, None]
    mag = jnp.exp(step * a_re)
    abar_re = mag * jnp.cos(step * a_im)
    abar_im = mag * jnp.sin(step * a_im)
    den = a_re * a_re + a_im * a_im
    f_re = ((abar_re - 1.0) * a_re + abar_im * a_im) / den
    f_im = (abar_im * a_re - (abar_re - 1.0) * a_im) / den
    bb_re = f_re[..., None] * b_re - f_im[..., None] * b_im
    bb_im = f_re[..., None] * b_im + f_im[..., None] * b_re
    j = jnp.arange(t + 1, dtype=F32)[:, None, None]
    pmag = jnp.exp(j * (step * a_re))
    pw_re = pmag * jnp.cos(j * (step * a_im))
    pw_im = pmag * jnp.sin(j * (step * a_im))
    cl_re = c_re[None] * pw_re[:t, :, None, :] - c_im[None] * pw_im[:t, :, None, :]
    cl_im = c_re[None] * pw_im[:t, :, None, :] + c_im[None] * pw_re[:t, :, None, :]
    kern = (jnp.einsum('jgcp,gpk->jgck', cl_re, bb_re, precision=hi)
            - jnp.einsum('jgcp,gpk->jgck', cl_im, bb_im, precision=hi))
    rp_re, rp_im = pw_re[:t][::-1], pw_im[:t][::-1]
    wz_re = rp_re[:, :, :, None] * bb_re[None] - rp_im[:, :, :, None] * bb_im[None]
    wz_im = rp_re[:, :, :, None] * bb_im[None] + rp_im[:, :, :, None] * bb_re[None]
    w_z = jnp.concatenate([wz_re, wz_im], axis=2).transpose(1, 0, 3, 2).reshape(S5_GROUPS, t * S5_GROUP, 2 * S5_STATE)
    cy_re = c_re[None] * pw_re[1:, :, None, :] - c_im[None] * pw_im[1:, :, None, :]
    cy_im = c_re[None] * pw_im[1:, :, None, :] + c_im[None] * pw_re[1:, :, None, :]
    w_y = jnp.concatenate([cy_re, -cy_im], axis=3).transpose(1, 3, 0, 2).reshape(S5_GROUPS, 2 * S5_STATE, t * S5_GROUP)
    return kern, w_z, w_y, pw_re[t], pw_im[t]


def _s5_lag_selector():
    t = S5_CHUNK
    lag = jnp.arange(t)[:, None] - jnp.arange(t)[None, :]
    return (lag[:, :, None] == jnp.arange(t)[None, None, :]).astype(F32).reshape(t * t, t)


def _s5_toeplitz(kern, tag):
    t = S5_CHUNK
    sel = _s5_lag_selector()
    flat = _matmul(sel, kern.reshape(t, -1), out_dtype=BF16, name=tag + "_toeplitz")
    toep = flat.reshape(t, t, S5_GROUPS, S5_GROUP, S5_GROUP).transpose(2, 1, 4, 0, 3)
    toep = toep.reshape(S5_GROUPS, t * S5_GROUP, t * S5_GROUP)

    def backward(d_toep):
        d_flat = d_toep.reshape(S5_GROUPS, t, S5_GROUP, t, S5_GROUP).transpose(3, 1, 0, 4, 2).reshape(t * t, -1)
        return _matmul(sel, d_flat, ta=True, name=tag + "_toeplitz_dw").reshape(kern.shape)

    return toep, backward


_BIG = [("gla_w_in", 2), ("gla_w_out", 1), ("ssd_w_in", 2), ("ssd_w_out", 1), ("s5_w_glu", 2), ("ffn_w_gu", 2),
        ("ffn_w_down", 1)]
_PADDED_COLS = {"gla_w_in": GLA_PROJ, "ssd_w_in": SSD_PROJ}


class _Traffic:
    LINK_BYTES_PER_SECOND = 7.0e10
    MATMUL_FLOPS = 7.0e14

    def __init__(self, shards, plan):
        self.shards, self.plan = shards, plan
        self.position = 0
        self.queue = []
        self.weights, self.received = {}, {}
        self.early = None
        self.standalone = self.serial = 0
        for key in plan:
            self._request(key)

    def _request(self, key):
        shard = self.shards[key]
        seconds = (N_DEV - 1) * shard.size * shard.dtype.itemsize / self.LINK_BYTES_PER_SECOND
        self._enqueue("gather", shard, seconds, key, lambda blocks: self.weights.__setitem__(key, self._assemble(key, blocks)))

    def _enqueue(self, kind, x, seconds, key, deliver):
        self.queue.append((kind, x, seconds, key, deliver, self.serial))
        self.serial += 1

    @staticmethod
    def _assemble(key, blocks):
        name, layer = key
        if dict(_BIG)[name] == 1:
            return blocks.reshape((N_DEV * blocks.shape[1], blocks.shape[2]))
        n_out = _PADDED_COLS.get(name, N_DEV * blocks.shape[2])
        return _join_cols(blocks[:, None], n_out, f"join_{name}_{layer}")[0]

    def take(self, key):
        assert key == self.plan[self.position], (key, self.plan[self.position])
        self.position += 1
        while key not in self.weights:
            self._alone(self.queue.pop(0))
        return self.weights[key]

    def run(self, seconds, call, more_carriers_follow=False):
        riders, waiting, left = [], [], seconds
        for item in self.queue:
            if item[2] <= left:
                riders.append(item)
                left -= item[2]
            else:
                waiting.append(item)
        due = [item for item in waiting if item[3] is not None and self.position < len(self.plan)
               and item[3] == self.plan[self.position]]
        if due and not more_carriers_follow:
            left = seconds - due[0][2]
            kept = []
            for item in riders:
                if item[2] <= left:
                    kept.append(item)
                    left -= item[2]
                else:
                    waiting.append(item)
            riders = due + kept
            waiting = [item for item in waiting if item is not due[0]]
            waiting.sort(key=lambda item: item[5])
        self.queue = waiting
        if not riders:
            return call(())
        results, moved = call([(kind, x) for kind, x, *_ in riders])
        for item, y in zip(riders, moved):
            item[4](y)
        return results

    def matmul(self, a, b, more_carriers_follow=False, **kw):
        m, n = (a.shape[-1] if kw.get("ta") else a.shape[-2]), (b.shape[-2] if kw.get("tb") else b.shape[-1])
        k = a.shape[-2] if kw.get("ta") else a.shape[-1]
        return self.run(2.0 * m * n * k / self.MATMUL_FLOPS, lambda cargo: _matmul(a, b, cargo=cargo, **kw),
                        more_carriers_follow)

    def send_gradient(self, key, dw):
        name, layer = key
        shard = self.shards[key]
        if dict(_BIG)[name] == 1:
            blocks = dw.reshape((N_DEV,) + shard.shape)
        else:
            blocks = _split_cols(dw, shard.shape[1], f"split_{name}_{layer}")
        seconds = (N_DEV - 1) * shard.size * shard.dtype.itemsize / self.LINK_BYTES_PER_SECOND
        self._enqueue("exchange", blocks, seconds, None, lambda parts: self.received.__setitem__(key, parts))

    def gather_early(self, packed):
        seconds = (N_DEV - 1) * packed.size * packed.dtype.itemsize / self.LINK_BYTES_PER_SECOND
        self._enqueue("gather", packed, seconds, None, lambda parts: setattr(self, "early", parts))

    def _alone(self, item):
        kind, x, _, _, deliver, _ = item
        deliver(_collective(kind, x, f"{kind}_alone_{self.standalone}"))
        self.standalone += 1

    def flush(self):
        for item in self.queue:
            self._alone(item)
        self.queue = []


_GLA_FWD_SECONDS, _GLA_BWD_SECONDS, _SSD_FWD_SECONDS, _SSD_BWD_SECONDS = 1.25e-6, 3.4e-6, 3.5e-6, 14e-6


def _linear(x, w, tag, out_dtype=F32, dx_dtype=F32, more_carriers_follow=False):
    traffic, key = w
    weight = traffic.take(key)
    y = traffic.matmul(x, weight, more_carriers_follow, out_dtype=out_dtype, name=tag + "_fwd")

    def backward(dy):
        dx = traffic.matmul(dy, weight, tb=True, out_dtype=dx_dtype, name=tag + "_dx")
        traffic.send_gradient(key, traffic.matmul(x, dy, ta=True, out_dtype=BF16, name=tag + "_dw"))
        return dx

    return y, backward


def _gated_linear(x, w, tag):
    traffic, key = w
    weight = traffic.take(key)
    seconds = 2.0 * x.shape[0] * x.shape[1] * weight.shape[1] / traffic.MATMUL_FLOPS
    outs = traffic.run(seconds, lambda cargo: _gu_matmul(x, weight, name=tag + "_fwd", cargo=cargo))

    def backward(d_gu):
        dx = traffic.matmul(d_gu, weight, tb=True, name=tag + "_dx")
        traffic.send_gradient(key, traffic.matmul(x, d_gu, ta=True, out_dtype=BF16, name=tag + "_dw"))
        return dx

    return tuple(outs), backward


def _gla_mixer(hn, p, tag):
    traffic, chunks = p["w_in"][0], hn.shape[0] // CHUNK
    w_a2 = jnp.pad(p["w_a2"], ((0, LANES - GLA_RANK), (0, 0)))
    b_a, norm_g = p["b_a"][None], p["norm_g"][None]
    proj, lin_in = _linear(hn, p["w_in"], tag + "_in", more_carriers_follow=True)
    o, sprev = traffic.run(chunks * _GLA_FWD_SECONDS, lambda cargo: _gla_core_fwd(proj, w_a2, b_a, norm_g, cargo))
    y, lin_out = _linear(o, p["w_out"], tag + "_out")

    def backward(dy):
        d_o = lin_out(dy)
        d_proj, d_wa, d_ba, d_ng = traffic.run(chunks * _GLA_BWD_SECONDS,
                                               lambda cargo: _gla_core_bwd(proj, sprev, d_o, w_a2, b_a, norm_g, cargo))
        return lin_in(d_proj), dict(w_a2=d_wa[:GLA_RANK], b_a=d_ba[0], norm_g=d_ng[0])

    return y, backward


def _ssd_mixer(hn, p, tag):
    pad = lambda a: jnp.pad(a[None], ((0, 0), (0, LANES - SSD_HEADS)))
    dt_bias, a_log, d_skip, norm_g = pad(p["dt_bias"]), pad(p["a_log"]), pad(p["d"]), p["norm_g"][None]
    traffic, chunks = p["w_in"][0], hn.shape[0] // CHUNK
    proj, lin_in = _linear(hn, p["w_in"], tag + "_in", more_carriers_follow=True)
    xbc = _ssd_conv_fwd(proj, p["conv_w"], p["conv_b"], tag + "_conv")
    o, hprev = traffic.run(chunks * _SSD_FWD_SECONDS,
                           lambda cargo: _ssd_core_fwd(proj, xbc, dt_bias, a_log, d_skip, norm_g, cargo))
    y, lin_out = _linear(o, p["w_out"], tag + "_out")

    def backward(dy):
        d_o = lin_out(dy)
        d_z, d_xbc, d_dt, d_db, d_al, d_ds, d_ng = traffic.run(
            chunks * _SSD_BWD_SECONDS, lambda cargo: _ssd_core_bwd(proj, xbc, hprev, d_o, dt_bias, a_log, d_skip, norm_g, cargo))
        d_pre, d_cw, d_cb = _ssd_conv_bwd(proj, d_xbc, p["conv_w"], p["conv_b"], tag + "_conv_bwd")
        d_hn = lin_in(jnp.concatenate([d_z, d_pre, d_dt], axis=1))
        return d_hn, dict(conv_w=d_cw, conv_b=d_cb, dt_bias=d_db[0, :SSD_HEADS], a_log=d_al[0, :SSD_HEADS],
                          d=d_ds[0, :SSD_HEADS], norm_g=d_ng[0])

    return y, backward


def _s5_mixer(hn, p, tag):
    seq = hn.shape[0]
    t, n_chunks = S5_CHUNK, hn.shape[0] // S5_CHUNK
    names = ("log_dt", "a_re", "a_im", "b_re", "b_im", "c_re", "c_im")
    (kern, w_z, w_y, lam_re, lam_im), ops_vjp = jax.vjp(_s5_operators, *[p[k] for k in names])
    toep, toep_bwd = _s5_toeplitz(kern, tag)
    to_groups = lambda a: a.reshape(n_chunks, t, S5_GROUPS, S5_GROUP).transpose(2, 0, 1, 3).reshape(S5_GROUPS, n_chunks, t * S5_GROUP)
    from_groups = lambda a: a.reshape(S5_GROUPS, n_chunks, t, S5_GROUP).transpose(1, 2, 0, 3).reshape(seq, D_MODEL)
    ug = to_groups(hn)
    z = _matmul(ug, w_z, name=tag + "_z")
    x_before = _s5_boundary_scan(z.transpose(1, 0, 2), lam_re, lam_im, tag + "_scan")
    xprev = x_before.transpose(1, 0, 2)
    tw = t * S5_GROUP
    ux = jnp.concatenate([ug, xprev.astype(BF16)], axis=2)
    yg = _matmul(ux, jnp.concatenate([toep, w_y.astype(BF16)], axis=1), out_dtype=BF16, name=tag + "_y")
    y = from_groups(yg)
    vg, lin_glu = _linear(_s5_gate_fwd(y, hn, p["d"], tag + "_gate"), p["w_glu"], tag + "_glu", dx_dtype=BF16)
    out = _glu_fwd(vg, tag + "_glu_gate")

    def backward(dy):
        d_act = lin_glu(_glu_bwd(vg, dy, tag + "_glu_gate_bwd"))
        d_y, d_u, d_d = _s5_gate_bwd(y, hn, p["d"], d_act, tag + "_gate_bwd")
        d_yg = to_groups(d_y)
        d_xprev = _matmul(d_yg, w_y, tb=True, name=tag + "_inter_dx").transpose(1, 0, 2)
        d_wy = _matmul(xprev, d_yg, ta=True, name=tag + "_inter_dw")
        dz = _s5_boundary_scan(d_xprev, lam_re, -lam_im, tag + "_scan_bwd", reverse=True)
        x_re, x_im, dz_re, dz_im = (x_before[..., :S5_STATE], x_before[..., S5_STATE:], dz[..., :S5_STATE],
                                    dz[..., S5_STATE:])
        d_lam_re = jnp.sum(x_re * dz_re + x_im * dz_im, axis=0)
        d_lam_im = jnp.sum(x_re * dz_im - x_im * dz_re, axis=0)
        dyz = jnp.concatenate([d_yg, dz.transpose(1, 0, 2).astype(BF16)], axis=2)
        d_ug = _matmul(dyz, jnp.concatenate([toep, w_z.astype(BF16)], axis=2), tb=True, out_dtype=BF16, name=tag + "_du")
        d_ops = _matmul(ug, dyz, ta=True, name=tag + "_dw")
        grads = dict(zip(names, ops_vjp((toep_bwd(d_ops[..., :tw]), d_ops[..., tw:], d_wy, d_lam_re, d_lam_im))))
        grads.update(d=d_d)
        return (d_u, from_groups(d_ug)), grads

    return out, backward


_SMALL =[("gla_w_a2", 2), ("gla_b_a", 1), ("gla_norm_g", 1), ("ssd_conv_w", 2), ("s5_d", 1)]
_REPLICATED = ["norm_mix_g", "norm_ffn_g", "ssd_conv_b", "ssd_dt_bias", "ssd_a_log", "ssd_d", "ssd_norm_g", "s5_log_dt",
               "s5_a_re", "s5_a_im", "s5_b_re", "s5_b_im", "s5_c_re", "s5_c_im", "final_norm_g"]
_EARLY_SMALL = [n for n in [s for s, _ in _SMALL] + _REPLICATED if n.startswith("s5_")]
_WEIGHTS = ['norm_mix_g', 'norm_ffn_g', 'gla_w_in', 'gla_w_a2', 'gla_b_a', 'gla_norm_g', 'gla_w_out', 'ssd_w_in',
            'ssd_conv_w', 'ssd_conv_b', 'ssd_dt_bias', 'ssd_a_log', 'ssd_d', 'ssd_norm_g', 'ssd_w_out', 's5_log_dt',
            's5_a_re', 's5_a_im', 's5_b_re', 's5_b_im', 's5_c_re', 's5_c_im', 's5_d', 's5_w_glu', 'ffn_w_gu', 'ffn_w_down',
            'final_norm_g']


def _gather_small(local):
    shapes = [local[n].shape for n, _ in _SMALL]
    blocks = _collective("gather", _pack([local[n] for n, _ in _SMALL]), "gather_vectors")
    parts = _unpack(blocks, shapes, lead=(N_DEV,))
    return {n: _join(part, axis) for (n, axis), part in zip(_SMALL, parts)}


def _forward_plan():
    plan = []
    for i in range(DEPTH):
        j = i // 3
        plan += [[("gla_w_in", j), ("gla_w_out", j)], [("ssd_w_in", j), ("ssd_w_out", j)], [("s5_w_glu", j)]][i % 3]
        plan += [("ffn_w_gu", i), ("ffn_w_down", i)]
    return plan


def _forward_backward(x, target, w, traffic):
    big = lambda name, j: (traffic, (name, j))
    gla = lambda j: dict(w_in=big("gla_w_in", j), w_a2=w["gla_w_a2"][j], b_a=w["gla_b_a"][j], norm_g=w["gla_norm_g"][j],
                         w_out=big("gla_w_out", j))
    ssd = lambda j: dict(w_in=big("ssd_w_in", j), conv_w=w["ssd_conv_w"][j], conv_b=w["ssd_conv_b"][j],
                         dt_bias=w["ssd_dt_bias"][j], a_log=w["ssd_a_log"][j], d=w["ssd_d"][j], norm_g=w["ssd_norm_g"][j],
                         w_out=big("ssd_w_out", j))
    s5 = lambda j: dict(log_dt=w["s5_log_dt"][j], a_re=w["s5_a_re"][j], a_im=w["s5_a_im"][j], b_re=w["s5_b_re"][j],
                        b_im=w["s5_b_im"][j], c_re=w["s5_c_re"][j], c_im=w["s5_c_im"][j], d=w["s5_d"][j],
                        w_glu=big("s5_w_glu", j))
    mixers = [("gla", _gla_mixer, gla), ("ssd", _ssd_mixer, ssd), ("s5", _s5_mixer, s5)]
    base, delta = x, None
    tape = []
    for i in range(DEPTH):
        kind, mixer, params = mixers[i % 3]
        j = i // 3
        h, hn = _add_norm_fwd(base, delta, w["norm_mix_g"][i], f"l{i}_norm_mix")
        y, mixer_bwd = mixer(hn, params(j), f"l{i}_{kind}")
        h_mid, hn2 = _add_norm_fwd(h, y, w["norm_ffn_g"][i], f"l{i}_norm_ffn")
        (g, u, act), gu_bwd = _gated_linear(hn2, big("ffn_w_gu", i), f"l{i}_ffn_gu")
        delta, down_bwd = _linear(act, big("ffn_w_down", i), f"l{i}_ffn_down", dx_dtype=BF16)
        base = h_mid
        tape.append((kind, j, h, mixer_bwd, h_mid, gu_bwd, (g, u), down_bwd))
    loss, d_h, d_final_g = _loss_head(base + delta, w["final_norm_g"], target, "loss_head")

    grads = {n: [None] * w[n].shape[0] for n in w if n != "final_norm_g"}
    grads["final_norm_g"] = d_final_g
    for i in reversed(range(DEPTH)):
        kind, j, h, mixer_bwd, h_mid, gu_bwd, (g, u), down_bwd = tape[i]
        d_gu = _swiglu_bwd(g, u, down_bwd(d_h), f"l{i}_swiglu_bwd")
        d_mid, grads["norm_ffn_g"][i] = _norm_bwd(h_mid, w["norm_ffn_g"][i], gu_bwd(d_gu), d_h, f"l{i}_norm_ffn_bwd")
        d_hn, mixer_grads = mixer_bwd(d_mid)
        for k, g in mixer_grads.items():
            grads[f"{kind}_{k}"][j] = g
        if kind == "s5" and all(g is not None for n in _EARLY_SMALL for g in grads[n]):
            traffic.gather_early(_pack([jnp.stack(grads[n]) for n in _EARLY_SMALL]))
        norm_bwd = lambda cargo: _norm_bwd(h, w["norm_mix_g"][i], d_hn, d_mid, f"l{i}_norm_mix_bwd", cargo)
        d_h, grads["norm_mix_g"][i] = traffic.run(1.0, norm_bwd) if i == 0 else norm_bwd(())
    return loss, d_h, grads


def kernel(x, norm_mix_g, norm_ffn_g, gla_w_in, gla_w_a2, gla_b_a, gla_norm_g, gla_w_out, ssd_w_in, ssd_conv_w, ssd_conv_b, ssd_dt_bias, ssd_a_log, ssd_d, ssd_norm_g, ssd_w_out, s5_log_dt, s5_a_re, s5_a_im, s5_b_re, s5_b_im, s5_c_re, s5_c_im, s5_d, s5_w_glu, ffn_w_gu, ffn_w_down, final_norm_g, loss_target, m_norm_mix_g, m_norm_ffn_g, m_gla_w_in, m_gla_w_a2, m_gla_b_a, m_gla_norm_g, m_gla_w_out, m_ssd_w_in, m_ssd_conv_w, m_ssd_conv_b, m_ssd_dt_bias, m_ssd_a_log, m_ssd_d, m_ssd_norm_g, m_ssd_w_out, m_s5_log_dt, m_s5_a_re, m_s5_a_im, m_s5_b_re, m_s5_b_im, m_s5_c_re, m_s5_c_im, m_s5_d, m_s5_w_glu, m_ffn_w_gu, m_ffn_w_down, m_final_norm_g, v_norm_mix_g, v_norm_ffn_g, v_gla_w_in, v_gla_w_a2, v_gla_b_a, v_gla_norm_g, v_gla_w_out, v_ssd_w_in, v_ssd_conv_w, v_ssd_conv_b, v_ssd_dt_bias, v_ssd_a_log, v_ssd_d, v_ssd_norm_g, v_ssd_w_out, v_s5_log_dt, v_s5_a_re, v_s5_a_im, v_s5_b_re, v_s5_b_im, v_s5_c_re, v_s5_c_im, v_s5_d, v_s5_w_glu, v_ffn_w_gu, v_ffn_w_down, v_final_norm_g):
    args = locals()
    local = {n: args[n] for n in _WEIGHTS}
    moment_m = {n: args["m_" + n] for n in _WEIGHTS}
    moment_v = {n: args["v_" + n] for n in _WEIGHTS}

    shards = {(n, layer): local[n][layer].astype(BF16) for n, _ in _BIG for layer in range(local[n].shape[0])}
    traffic = _Traffic(shards, _forward_plan())
    full = {n: local[n] for n in _REPLICATED}
    full.update(_gather_small(local))

    loss, d_x, grads = _forward_backward(x[0], loss_target[0], full, traffic)
    traffic.flush()
    loss = lax.psum(loss, ("x", "y", "c"))
    kinds = ("grad", "delta", "new_m", "new_v")
    out = {}

    for n, _ in _BIG:
        parts = jnp.stack([traffic.received[(n, layer)] for layer in range(local[n].shape[0])], axis=1)
        results = _adamw(parts, local[n], moment_m[n], moment_v[n], "adamw_" + n)
        out.update({f"{kind}_{n}": a for kind, a in zip(kinds, results)})

    small = [n for n, _ in _SMALL] + _REPLICATED
    stacked = lambda n: grads[n] if n == "final_norm_g" else jnp.stack(grads[n])
    late = [n for n in small if n not in _EARLY_SMALL]
    gathered = [(_EARLY_SMALL, traffic.early, "early"),
                (late, _collective("gather", _pack([stacked(n) for n in late]), "gather_small_gradients"), "late")]
    summed = {}
    for names, parts, tag in gathered:
        sums = _unpack(_sum_parts(parts, "sum_small_gradients_" + tag), [stacked(n).shape for n in names])
        summed.update(zip(names, sums))
    position = _index(_mesh_position())
    mine = [_own_shard(summed[n], axis, position) for n, axis in _SMALL] + [summed[n] for n in _REPLICATED]
    shapes = [local[n].shape for n in small]
    pk = lambda arrays: _pack(arrays)[None]
    results = _adamw(pk(mine)[None], pk([local[n] for n in small]), pk([moment_m[n] for n in small]),
                     pk([moment_v[n] for n in small]), "adamw_small")
    for kind, flat in zip(kinds, results):
        out.update({f"{kind}_{n}": a for n, a in zip(small, _unpack(flat[0], shapes))})

    return (loss, d_x[None], *[out[f"{kind}_{n}"] for kind in ("grad", "delta", "new_m", "new_v") for n in _WEIGHTS])
```

```python
import functools
import math

import jax
import jax.numpy as jnp
from jax import lax
from jax.experimental import pallas as pl
from jax.experimental.pallas import tpu as pltpu

F32 = jnp.float32
BF16 = jnp.bfloat16
_MXU_DTYPE = jnp.bfloat16

N_DEV = 8
D_MODEL = 1024
DEPTH = 4
CHUNK = 64
GLA_STEP_CHUNKS, SSD_STEP_CHUNKS = 2, 1
GLA_STEP, SSD_STEP = CHUNK * GLA_STEP_CHUNKS, CHUNK * SSD_STEP_CHUNKS
EPS = 1e-6
GLA_HEADS, GLA_DK, GLA_DV, GLA_RANK, GLA_TAU = 4, 128, 256, 16, 16.0
GLA_QK = GLA_HEADS * GLA_DK
GLA_VD = GLA_HEADS * GLA_DV
LANES = 128
GLA_IN = 2 * GLA_QK + 2 * GLA_VD + GLA_RANK
GLA_PROJ = 2 * GLA_QK + 2 * GLA_VD + LANES
SSD_DINNER, SSD_HEADDIM, SSD_HEADS, SSD_GROUPS, SSD_HPG, SSD_DSTATE, SSD_CONV = 2048, 64, 32, 8, 4, 128, 4
SSD_GN = SSD_GROUPS * SSD_DSTATE
SSD_GW = SSD_HPG * SSD_HEADDIM
SSD_XBC = SSD_DINNER + 2 * SSD_GN
SSD_IN = SSD_DINNER + SSD_XBC + SSD_HEADS
SSD_PROJ = SSD_DINNER + SSD_XBC + LANES
S5_GROUP, S5_GROUPS, S5_STATE = 16, 64, 64
S5_CHUNK = 16
FFN_HIDDEN = 2816
ADAM_LR, ADAM_B1, ADAM_B2, ADAM_EPS, ADAM_WD, ADAM_STEP = 0.001, 0.9, 0.999, 1e-08, 0.01, 10
VMEM_LIMIT = 48 * 1024 * 1024
FLAT_COLS = 1024
FLAT_ROWS_ALIGN = 64


def _tile(n, cap, unit):
    if n <= cap:
        return n
    best = None
    for t in range(unit, cap + 1, unit):
        if n % t == 0:
            best = t
    assert best is not None, (n, cap, unit)
    return best


def _divisors(n, unit):
    return sorted({t for t in range(unit, n + 1, unit) if n % t == 0} | {n})


_MXU_FLOPS, _HBM_BYTES, _ACC_BYTES, _STEP_SECONDS = 1.1e15, 3e12, 1.1e13, 3.5e-7
_MXU_ROWS = 256
_TILE_VMEM_BUDGET = 36 * 1024 * 1024
_BATCH_VMEM_BUDGET = 16 * 1024 * 1024


def _pick_tiles(m, n, k, a_bytes, b_bytes, o_bytes, m_unit):
    best = None
    for tm in _divisors(m, m_unit):
        for tn in _divisors(n, LANES):
            for tk in _divisors(k, LANES):
                nk = k // tk
                vmem = 2 * tm * tk * a_bytes + 2 * tk * tn * b_bytes + 2 * tm * tn * o_bytes + (nk > 1) * tm * tn * 4
                if vmem > _TILE_VMEM_BUDGET or tm > 2048 or tn > 2048:
                    continue
                a_reads = n // tn if nk > 1 else 1
                b_reads = 1 if (nk == 1 and n == tn) else m // tm
                traffic = m * k * a_bytes * a_reads + k * n * b_bytes * b_reads + m * n * o_bytes
                mxu = 2.0 * m * n * k / _MXU_FLOPS * (1.0 + _MXU_ROWS / tm)
                cost = (max(mxu, traffic / _HBM_BYTES) + (nk > 1) * nk * m * n * 8 / _ACC_BYTES
                        + (m // tm) * (n // tn) * nk * _STEP_SECONDS)
                if best is None or cost < best[0]:
                    best = (cost, tm, tn, tk)
    assert best is not None, (m, n, k)
    return best[1:]


def _cargo_call(body, cargo, *, name, grid, in_specs, out_specs, out_shape, scratch_shapes, semantics):
    params = lambda sem: pltpu.CompilerParams(dimension_semantics=sem, vmem_limit_bytes=VMEM_LIMIT)
    if not cargo:
        return pl.pallas_call(body, name=name, grid=grid, in_specs=in_specs, out_specs=out_specs, out_shape=out_shape,
                              scratch_shapes=scratch_shapes, compiler_params=params(semantics))
    n_in, n_out, n_scratch, n_cargo = len(in_specs), len(out_specs), len(scratch_shapes), len(cargo)

    def loaded(*refs):
        ins, cargo_in, rest = refs[:n_in], refs[n_in:n_in + n_cargo], refs[n_in + n_cargo:]
        outs, cargo_out, rest = rest[:n_out], rest[n_out:n_out + n_cargo], rest[n_out + n_cargo:]
        scratch, sems = rest[:n_scratch], rest[n_scratch:]
        ids = [pl.program_id(d) for d in range(len(grid))]
        first = functools.reduce(jnp.logical_and, [i == 0 for i in ids])
        last = functools.reduce(jnp.logical_and, [i == g - 1 for i, g in zip(ids, grid)])
        moves = lambda: [_moves(kind, x_ref, y_ref, *sems[3 * c:3 * c + 3])
                         for c, ((kind, _), x_ref, y_ref) in enumerate(zip(cargo, cargo_in, cargo_out))]

        @pl.when(first)
        def _():
            for mv in moves():
                _start(mv)

        body(*ins, *outs, *scratch)

        @pl.when(last)
        def _():
            for mv in moves():
                _finish(mv)

    sems = [pltpu.SemaphoreType.DMA((N_DEV - 1,)), pltpu.SemaphoreType.DMA((N_DEV - 1,)), pltpu.SemaphoreType.DMA] * n_cargo
    call = pl.pallas_call(
        loaded, name=name, grid=grid, in_specs=list(in_specs) + [_ANY] * n_cargo,
        out_specs=list(out_specs) + [_ANY] * n_cargo,
        out_shape=list(out_shape) + [_moved_shape(kind, x) for kind, x in cargo],
        scratch_shapes=list(scratch_shapes) + sems, compiler_params=params(("arbitrary",) * len(grid)))

    def run(*args):
        results = call(*args, *[x for _, x in cargo])
        return list(results[:n_out]), list(results[n_out:])

    return run


def _matmul(a, b, *, ta=False, tb=False, out_dtype=F32, name, cargo=()):
    batched = a.ndim == 3
    if ta:
        k_dim, m_dim = a.shape[-2:]
    else:
        m_dim, k_dim = a.shape[-2:]
    if tb:
        n_dim, kb = b.shape[-2:]
    else:
        kb, n_dim = b.shape[-2:]
    assert kb == k_dim, (a.shape, b.shape, ta, tb)
    tm, tn, tk = _pick_tiles(m_dim, n_dim, k_dim, a.dtype.itemsize, b.dtype.itemsize, jnp.dtype(out_dtype).itemsize,
                             LANES if ta else 16)
    nk = k_dim // tk
    ca, cb = (0 if ta else 1), (1 if tb else 0)
    grid = (m_dim // tm, n_dim // tn, nk)
    gb = 1
    if batched:
        step_bytes = 2 * (tm * tk * a.dtype.itemsize + tk * tn * b.dtype.itemsize + tm * tn * jnp.dtype(out_dtype).itemsize)
        gb = max(g for g in _divisors(a.shape[0], 1) if g * step_bytes <= _BATCH_VMEM_BUDGET or g == 1)
        grid = (a.shape[0] // gb,) + grid
    dims = (((ca + 1,), (cb + 1,)), ((0,), (0,))) if batched else (((ca,), (cb,)), ((), ()))

    def body(a_ref, b_ref, o_ref, *acc):
        part = lax.dot_general(a_ref[...].astype(_MXU_DTYPE), b_ref[...].astype(_MXU_DTYPE), dims,
                               preferred_element_type=F32)
        if nk == 1:
            o_ref[...] = part.astype(o_ref.dtype)
            return
        acc_ref, = acc
        k = pl.program_id(len(grid) - 1)

        @pl.when(k == 0)
        def _():
            acc_ref[...] = part

        @pl.when(k > 0)
        def _():
            acc_ref[...] += part

        @pl.when(k == nk - 1)
        def _():
            o_ref[...] = acc_ref[...].astype(o_ref.dtype)

    def spec(shape, fn):
        if batched:
            return pl.BlockSpec((gb,) + shape, lambda g, i, j, k: (g,) + fn(i, j, k))
        return pl.BlockSpec(shape, fn)

    a_spec = spec((tk, tm), lambda i, j, k: (k, i)) if ta else spec((tm, tk), lambda i, j, k: (i, k))
    b_spec = spec((tn, tk), lambda i, j, k: (j, k)) if tb else spec((tk, tn), lambda i, j, k: (k, j))
    o_spec = spec((tm, tn), lambda i, j, k: (i, j))
    out_shape = ((a.shape[0],) if batched else ()) + (m_dim, n_dim)
    call = _cargo_call(
        body, cargo, name=name, grid=grid, in_specs=[a_spec, b_spec], out_specs=[o_spec],
        out_shape=[jax.ShapeDtypeStruct(out_shape, out_dtype)],
        scratch_shapes=[pltpu.VMEM(((gb,) if batched else ()) + (tm, tn), F32)] if nk > 1 else [],
        semantics=("parallel",) * (len(grid) - 1) + ("arbitrary",))
    if not cargo:
        return call(a, b)[0]
    results, moved = call(a, b)
    return results[0], moved


def _dot(a, b, ca=1, cb=0, exact=False):
    if exact:
        return lax.dot_general(a, b, (((ca,), (cb,)), ((), ())), precision=lax.Precision.HIGHEST,
                               preferred_element_type=F32)
    return lax.dot_general(a.astype(_MXU_DTYPE), b.astype(_MXU_DTYPE), (((ca,), (cb,)), ((), ())),
                           preferred_element_type=F32)


def _tri(n):
    return lax.broadcasted_iota(jnp.int32, (n, n), 0) >= lax.broadcasted_iota(jnp.int32, (n, n), 1)


def _log_sigmoid(x):
    return jnp.minimum(x, 0.0) - jnp.log(1.0 + jnp.exp(-jnp.abs(x)))


def _softplus(x):
    return jnp.maximum(x, 0.0) + jnp.log(1.0 + jnp.exp(-jnp.abs(x)))


def _silu(x):
    return x / (1.0 + jnp.exp(-x))


def _full_spec(shape):
    return pl.BlockSpec(shape, lambda c: (0,) * len(shape))


def _gla_chunk(proj, st, w_a2, b_a, norm_g):
    t = proj.shape[0]
    q = proj[:, 0:GLA_QK] * (GLA_DK ** -0.5)
    k = proj[:, GLA_QK:2 * GLA_QK]
    v = proj[:, 2 * GLA_QK:2 * GLA_QK + GLA_VD]
    r = proj[:, 2 * GLA_QK + GLA_VD:2 * GLA_QK + 2 * GLA_VD]
    a_low = proj[:, 2 * GLA_QK + 2 * GLA_VD:]
    log_a = _log_sigmoid(_dot(a_low, w_a2) + b_a) * (1.0 / GLA_TAU)
    past = _tri(t)
    lc = _dot(past.astype(F32), log_a, exact=True)
    lend = lc[t - 1:t, :]
    e_pos = jnp.exp(lc)
    e_neg = jnp.exp(-lc)
    q_fwd, k_fwd, q_bwd, k_bwd = q * e_pos, k * e_neg, q * e_neg, k * e_pos
    kd = k * jnp.exp(lend - lc)
    g = jnp.exp(lend)
    outs, new_st = [], []
    for h in range(GLA_HEADS):
        sk = slice(h * GLA_DK, (h + 1) * GLA_DK)
        sv = slice(h * GLA_DV, (h + 1) * GLA_DV)
        s_past = _dot(q_fwd[:, sk], k_fwd[:, sk], 1, 1)
        s_future = _dot(q_bwd[:, sk], k_bwd[:, sk], 1, 1)
        scores = jnp.where(past, s_past, s_future)
        o = _dot(scores, v[:, sv]) + _dot(q_fwd[:, sk], st[h], 1, 1)
        new_st.append(st[h] * g[:, sk] + _dot(v[:, sv], kd[:, sk], 0, 0))
        o = o * lax.rsqrt(jnp.mean(o * o, axis=-1, keepdims=True) + EPS) * norm_g[:, sv]
        outs.append(o)
    return jnp.concatenate(outs, axis=1) * _silu(r), tuple(new_st)


_GLA_STATE = (GLA_HEADS, GLA_DV, GLA_DK)


def _gla_step(proj, st, w_a2, b_a, norm_g):
    outs = []
    for s in range(GLA_STEP_CHUNKS):
        out, st = _gla_chunk(proj[s * CHUNK:(s + 1) * CHUNK], st, w_a2, b_a, norm_g)
        outs.append(out)
    return jnp.concatenate(outs, axis=0), st


def _gla_core_fwd(proj, w_a2, b_a, norm_g, cargo=()):
    seq = proj.shape[0]
    nc = seq // GLA_STEP

    def body(proj_ref, wa_ref, ba_ref, ng_ref, o_ref, sprev_ref, st_ref):
        @pl.when(pl.program_id(0) == 0)
        def _():
            st_ref[...] = jnp.zeros_like(st_ref)

        st = tuple(st_ref[h] for h in range(GLA_HEADS))
        for h in range(GLA_HEADS):
            sprev_ref[0, h] = st[h]
        out, new_st = _gla_step(proj_ref[...], st, wa_ref[...], ba_ref[...], ng_ref[...])
        o_ref[...] = out
        for h in range(GLA_HEADS):
            st_ref[h] = new_st[h]

    return _cargo_call(
        body, cargo, name="gla_core_fwd", grid=(nc,),
        in_specs=[pl.BlockSpec((GLA_STEP, GLA_PROJ), lambda c: (c, 0)), _full_spec(w_a2.shape), _full_spec(b_a.shape),
                  _full_spec(norm_g.shape)],
        out_specs=[pl.BlockSpec((GLA_STEP, GLA_VD), lambda c: (c, 0)), pl.BlockSpec((1,) + _GLA_STATE, lambda c: (c, 0, 0, 0))],
        out_shape=[jax.ShapeDtypeStruct((seq, GLA_VD), F32), jax.ShapeDtypeStruct((nc,) + _GLA_STATE, F32)],
        scratch_shapes=[pltpu.VMEM(_GLA_STATE, F32)],
        semantics=("arbitrary",),
    )(proj, w_a2, b_a, norm_g)


def _gla_core_bwd(proj, sprev, d_out, w_a2, b_a, norm_g, cargo=()):
    seq = proj.shape[0]
    nc = seq // GLA_STEP

    def body(proj_ref, sprev_ref, do_ref, wa_ref, ba_ref, ng_ref, dproj_ref, dwa_ref, dba_ref, dng_ref, dst_ref):
        @pl.when(pl.program_id(0) == 0)
        def _():
            dst_ref[...] = jnp.zeros_like(dst_ref)
            dwa_ref[...] = jnp.zeros_like(dwa_ref)
            dba_ref[...] = jnp.zeros_like(dba_ref)
            dng_ref[...] = jnp.zeros_like(dng_ref)

        st = tuple(sprev_ref[0, h] for h in range(GLA_HEADS))
        _, vjp = jax.vjp(_gla_step, proj_ref[...], st, wa_ref[...], ba_ref[...], ng_ref[...])
        d_next = tuple(dst_ref[h] for h in range(GLA_HEADS))
        d_proj, d_st, d_wa, d_ba, d_ng = vjp((do_ref[...], d_next))
        dproj_ref[...] = d_proj.astype(dproj_ref.dtype)
        for h in range(GLA_HEADS):
            dst_ref[h] = d_st[h]
        dwa_ref[...] += d_wa
        dba_ref[...] += d_ba
        dng_ref[...] += d_ng

    rev = lambda c: (nc - 1 - c, 0)
    return _cargo_call(
        body, cargo, name="gla_core_bwd", grid=(nc,),
        in_specs=[pl.BlockSpec((GLA_STEP, GLA_PROJ), rev), pl.BlockSpec((1,) + _GLA_STATE, lambda c: (nc - 1 - c, 0, 0, 0)),
                  pl.BlockSpec((GLA_STEP, GLA_VD), rev), _full_spec(w_a2.shape), _full_spec(b_a.shape), _full_spec(norm_g.shape)],
        out_specs=[pl.BlockSpec((GLA_STEP, GLA_PROJ), rev), _full_spec(w_a2.shape), _full_spec(b_a.shape), _full_spec(norm_g.shape)],
        out_shape=[jax.ShapeDtypeStruct((seq, GLA_PROJ), BF16), jax.ShapeDtypeStruct(w_a2.shape, F32),
                   jax.ShapeDtypeStruct(b_a.shape, F32), jax.ShapeDtypeStruct(norm_g.shape, F32)],
        scratch_shapes=[pltpu.VMEM(_GLA_STATE, F32)],
        semantics=("arbitrary",),
    )(proj, sprev, d_out, w_a2, b_a, norm_g)


def _ssd_chunk(z, xbc, dt_raw, hs, dt_bias, a_log, d_skip, norm_g):
    t = z.shape[0]
    xs = xbc[:, :SSD_DINNER]
    bm = xbc[:, SSD_DINNER:SSD_DINNER + SSD_GN]
    cm = xbc[:, SSD_DINNER + SSD_GN:]
    dt = _softplus(dt_raw + dt_bias)
    da = dt * (-jnp.exp(a_log))
    tri = _tri(t).astype(F32)
    eye = (lax.broadcasted_iota(jnp.int32, (t, t), 0) == lax.broadcasted_iota(jnp.int32, (t, t), 1)).astype(F32)
    cum = _dot(tri, da, exact=True)
    cum_t = _dot(da, tri, 0, 1, exact=True)
    dt_t = _dot(dt, eye, 0, 0, exact=True)
    cum_end = cum[t - 1:t, :]
    w_state = dt * jnp.exp(cum_end - cum)
    e_cum = jnp.exp(cum)
    g_end = jnp.exp(cum_end)
    head_of = lambda axis: lax.shift_right_logical(lax.broadcasted_iota(jnp.int32, (SSD_GW, SSD_GW), axis),
                                                   jnp.int32(SSD_HEADDIM.bit_length() - 1))
    same_head = head_of(0) == head_of(1)
    ys, new_hs = [], []
    for g in range(SSD_GROUPS):
        heads = range(g * SSD_HPG, (g + 1) * SSD_HPG)
        cols = slice(g * SSD_GW, (g + 1) * SSD_GW)

        def spread(a):
            return jnp.concatenate([jnp.broadcast_to(a[:, h:h + 1], (a.shape[0], SSD_HEADDIM)) for h in heads], axis=1)

        def row(a_t):
            return jnp.concatenate([a_t[h:h + 1, :] for h in heads], axis=1)

        bm_g = bm[:, g * SSD_DSTATE:(g + 1) * SSD_DSTATE]
        cm_g = cm[:, g * SSD_DSTATE:(g + 1) * SSD_DSTATE]
        xs_g = xs[:, cols]
        cb = _dot(cm_g, jnp.concatenate([bm_g] * SSD_HPG, axis=0), 1, 1)
        mix = cb * jnp.exp(-jnp.abs(spread(cum) - row(cum_t))) * row(dt_t)
        x_diag = jnp.where(same_head, jnp.concatenate([xs_g] * SSD_HPG, axis=0), 0.0)
        y = _dot(mix, x_diag)
        y = y + _dot(cm_g, hs[g], 1, 1) * spread(e_cum)
        y = y + spread(d_skip) * xs_g
        states = _dot(xs_g * spread(w_state), bm_g, 0, 0)
        decayed = jnp.concatenate([g_end[:, h:h + 1] * hs[g][j * SSD_HEADDIM:(j + 1) * SSD_HEADDIM, :]
                                   for j, h in enumerate(heads)], axis=0)
        new_hs.append(decayed + states)
        yg = y * _silu(z[:, cols])
        ys.append(yg * lax.rsqrt(jnp.mean(yg * yg, axis=-1, keepdims=True) + EPS) * norm_g[:, cols])
    return jnp.concatenate(ys, axis=1), tuple(new_hs)


_SSD_STATE = (SSD_GROUPS, SSD_GW, SSD_DSTATE)


def _ssd_step(z, xbc, dt_raw, hs, dt_bias, a_log, d_skip, norm_g):
    outs = []
    for s in range(SSD_STEP_CHUNKS):
        rows = slice(s * CHUNK, (s + 1) * CHUNK)
        out, hs = _ssd_chunk(z[rows], xbc[rows], dt_raw[rows], hs, dt_bias, a_log, d_skip, norm_g)
        outs.append(out)
    return jnp.concatenate(outs, axis=0), hs
_SSD_DT_BLOCK = (SSD_DINNER + SSD_XBC) // LANES


def _ssd_core_fwd(proj, xbc, dt_bias, a_log, d_skip, norm_g, cargo=()):
    seq = proj.shape[0]
    nc = seq // SSD_STEP

    def body(z_ref, xbc_ref, dt_ref, db_ref, al_ref, ds_ref, ng_ref, o_ref, hprev_ref, hs_ref):
        @pl.when(pl.program_id(0) == 0)
        def _():
            hs_ref[...] = jnp.zeros_like(hs_ref)

        hs = tuple(hs_ref[g] for g in range(SSD_GROUPS))
        for g in range(SSD_GROUPS):
            hprev_ref[0, g] = hs[g]
        out, new_hs = _ssd_step(z_ref[...], xbc_ref[...], dt_ref[...], hs, db_ref[...], al_ref[...], ds_ref[...], ng_ref[...])
        o_ref[...] = out
        for g in range(SSD_GROUPS):
            hs_ref[g] = new_hs[g]

    return _cargo_call(
        body, cargo, name="ssd_core_fwd", grid=(nc,),
        in_specs=[pl.BlockSpec((SSD_STEP, SSD_DINNER), lambda c: (c, 0)), pl.BlockSpec((SSD_STEP, SSD_XBC), lambda c: (c, 0)),
                  pl.BlockSpec((SSD_STEP, LANES), lambda c: (c, _SSD_DT_BLOCK)),
                  _full_spec(dt_bias.shape), _full_spec(a_log.shape), _full_spec(d_skip.shape), _full_spec(norm_g.shape)],
        out_specs=[pl.BlockSpec((SSD_STEP, SSD_DINNER), lambda c: (c, 0)), pl.BlockSpec((1,) + _SSD_STATE, lambda c: (c, 0, 0, 0))],
        out_shape=[jax.ShapeDtypeStruct((seq, SSD_DINNER), F32), jax.ShapeDtypeStruct((nc,) + _SSD_STATE, F32)],
        scratch_shapes=[pltpu.VMEM(_SSD_STATE, F32)],
        semantics=("arbitrary",),
    )(proj, xbc, proj, dt_bias, a_log, d_skip, norm_g)


def _ssd_core_bwd(proj, xbc, hprev, d_out, dt_bias, a_log, d_skip, norm_g, cargo=()):
    seq = proj.shape[0]
    nc = seq // SSD_STEP

    def body(z_ref, xbc_ref, dt_ref, hprev_ref, do_ref, db_ref, al_ref, ds_ref, ng_ref,
             dz_ref, dxbc_ref, ddt_ref, ddb_ref, dal_ref, dds_ref, dng_ref, dhs_ref):
        @pl.when(pl.program_id(0) == 0)
        def _():
            dhs_ref[...] = jnp.zeros_like(dhs_ref)
            ddb_ref[...] = jnp.zeros_like(ddb_ref)
            dal_ref[...] = jnp.zeros_like(dal_ref)
            dds_ref[...] = jnp.zeros_like(dds_ref)
            dng_ref[...] = jnp.zeros_like(dng_ref)

        hs = tuple(hprev_ref[0, g] for g in range(SSD_GROUPS))
        _, vjp = jax.vjp(_ssd_step, z_ref[...], xbc_ref[...], dt_ref[...], hs, db_ref[...], al_ref[...], ds_ref[...], ng_ref[...])
        d_next = tuple(dhs_ref[g] for g in range(SSD_GROUPS))
        d_z, d_xbc, d_dt, d_hs, d_db, d_al, d_ds, d_ng = vjp((do_ref[...], d_next))
        dz_ref[...] = d_z.astype(dz_ref.dtype)
        dxbc_ref[...] = d_xbc
        ddt_ref[...] = d_dt.astype(ddt_ref.dtype)
        for g in range(SSD_GROUPS):
            dhs_ref[g] = d_hs[g]
        ddb_ref[...] += d_db
        dal_ref[...] += d_al
        dds_ref[...] += d_ds
        dng_ref[...] += d_ng

    rev = lambda c: (nc - 1 - c, 0)
    vec = [_full_spec(dt_bias.shape), _full_spec(a_log.shape), _full_spec(d_skip.shape), _full_spec(norm_g.shape)]
    return _cargo_call(
        body, cargo, name="ssd_core_bwd", grid=(nc,),
        in_specs=[pl.BlockSpec((SSD_STEP, SSD_DINNER), rev), pl.BlockSpec((SSD_STEP, SSD_XBC), rev),
                  pl.BlockSpec((SSD_STEP, LANES), lambda c: (nc - 1 - c, _SSD_DT_BLOCK)),
                  pl.BlockSpec((1,) + _SSD_STATE, lambda c: (nc - 1 - c, 0, 0, 0)),
                  pl.BlockSpec((SSD_STEP, SSD_DINNER), rev)] + vec,
        out_specs=[pl.BlockSpec((SSD_STEP, SSD_DINNER), rev), pl.BlockSpec((SSD_STEP, SSD_XBC), rev),
                   pl.BlockSpec((SSD_STEP, LANES), rev)] + vec,
        out_shape=[jax.ShapeDtypeStruct((seq, SSD_DINNER), BF16), jax.ShapeDtypeStruct((seq, SSD_XBC), F32),
                   jax.ShapeDtypeStruct((seq, LANES), BF16),
                   jax.ShapeDtypeStruct(dt_bias.shape, F32), jax.ShapeDtypeStruct(a_log.shape, F32),
                   jax.ShapeDtypeStruct(d_skip.shape, F32), jax.ShapeDtypeStruct(norm_g.shape, F32)],
        scratch_shapes=[pltpu.VMEM(_SSD_STATE, F32)],
        semantics=("arbitrary",),
    )(proj, xbc, proj, hprev, d_out, dt_bias, a_log, d_skip, norm_g)


CONV_COLS = 2048
CONV_HALO = 8


def _conv_taps(xx, rows):
    last = SSD_CONV - 1
    return [pltpu.roll(xx, last - k, 0)[CONV_HALO:CONV_HALO + rows] if k < last else xx[CONV_HALO:CONV_HALO + rows]
            for k in range(SSD_CONV)]


def _ssd_conv_fwd(proj, conv_w, conv_b, name):
    rows = proj.shape[0]
    tr = _tile(rows, 512, CONV_HALO)
    first_col = SSD_DINNER // CONV_COLS

    def body(x_ref, halo_ref, w_ref, b_ref, o_ref):
        halo = jnp.where(pl.program_id(0) == 0, 0.0, halo_ref[...])
        taps = _conv_taps(jnp.concatenate([halo, x_ref[...]], axis=0), tr)
        out = b_ref[...]
        for k in range(SSD_CONV):
            out = out + taps[k] * w_ref[k:k + 1, :]
        o_ref[...] = _silu(out)

    return pl.pallas_call(
        body, name=name, grid=(rows // tr, SSD_XBC // CONV_COLS),
        in_specs=[pl.BlockSpec((tr, CONV_COLS), lambda i, j: (i, first_col + j)),
                  pl.BlockSpec((CONV_HALO, CONV_COLS), lambda i, j: (jnp.maximum(i * (tr // CONV_HALO) - 1, 0), first_col + j)),
                  pl.BlockSpec((SSD_CONV, CONV_COLS), lambda i, j: (0, j)), pl.BlockSpec((1, CONV_COLS), lambda i, j: (0, j))],
        out_specs=pl.BlockSpec((tr, CONV_COLS), lambda i, j: (i, j)),
        out_shape=jax.ShapeDtypeStruct((rows, SSD_XBC), F32),
        compiler_params=pltpu.CompilerParams(dimension_semantics=("parallel", "parallel"), vmem_limit_bytes=VMEM_LIMIT),
    )(proj, proj, conv_w, conv_b[None])


def _ssd_conv_bwd(proj, d_xbc, conv_w, conv_b, name):
    rows = proj.shape[0]
    tr = _tile(rows, 512, CONV_HALO)
    nb, halos = rows // tr, tr // CONV_HALO
    first_col = SSD_DINNER // CONV_COLS

    def body(x_ref, before_ref, after_ref, d_ref, d_after_ref, w_ref, b_ref, dx_ref, dw_ref, db_ref):
        i = pl.program_id(1)

        @pl.when(i == 0)
        def _():
            dw_ref[...] = jnp.zeros_like(dw_ref)
            db_ref[...] = jnp.zeros_like(db_ref)

        before = jnp.where(i == 0, 0.0, before_ref[...])
        taps = _conv_taps(jnp.concatenate([before, x_ref[...], after_ref[...]], axis=0), tr + CONV_HALO)
        out = b_ref[...]
        for k in range(SSD_CONV):
            out = out + taps[k] * w_ref[k:k + 1, :]
        sig = 1.0 / (1.0 + jnp.exp(-out))
        d_after = jnp.where(i == nb - 1, 0.0, d_after_ref[...])
        d_out = jnp.concatenate([d_ref[...], d_after], axis=0) * sig * (1.0 + out * (1.0 - sig))
        d_x = d_out[:tr] * w_ref[SSD_CONV - 1:SSD_CONV, :]
        for k in range(SSD_CONV - 1):
            ahead = SSD_CONV - 1 - k
            d_x = d_x + pltpu.roll(d_out, tr + CONV_HALO - ahead, 0)[:tr] * w_ref[k:k + 1, :]
        dx_ref[...] = d_x.astype(dx_ref.dtype)
        for k in range(SSD_CONV):
            dw_ref[k:k + 1, :] += jnp.sum(d_out[:tr] * taps[k][:tr], axis=0, keepdims=True)
        db_ref[...] += jnp.sum(d_out[:tr], axis=0, keepdims=True)

    before = lambda j, i: jnp.maximum(i * halos - 1, 0)
    after = lambda j, i: jnp.minimum((i + 1) * halos, nb * halos - 1)
    d_x, d_w, d_b = pl.pallas_call(
        body, name=name, grid=(SSD_XBC // CONV_COLS, nb),
        in_specs=[pl.BlockSpec((tr, CONV_COLS), lambda j, i: (i, first_col + j)),
                  pl.BlockSpec((CONV_HALO, CONV_COLS), lambda j, i: (before(j, i), first_col + j)),
                  pl.BlockSpec((CONV_HALO, CONV_COLS), lambda j, i: (after(j, i), first_col + j)),
                  pl.BlockSpec((tr, CONV_COLS), lambda j, i: (i, j)),
                  pl.BlockSpec((CONV_HALO, CONV_COLS), lambda j, i: (after(j, i), j)),
                  pl.BlockSpec((SSD_CONV, CONV_COLS), lambda j, i: (0, j)), pl.BlockSpec((1, CONV_COLS), lambda j, i: (0, j))],
        out_specs=[pl.BlockSpec((tr, CONV_COLS), lambda j, i: (i, j)), pl.BlockSpec((SSD_CONV, CONV_COLS), lambda j, i: (0, j)),
                   pl.BlockSpec((1, CONV_COLS), lambda j, i: (0, j))],
        out_shape=[jax.ShapeDtypeStruct((rows, SSD_XBC), BF16), jax.ShapeDtypeStruct((SSD_CONV, SSD_XBC), F32),
                   jax.ShapeDtypeStruct((1, SSD_XBC), F32)],
        compiler_params=pltpu.CompilerParams(dimension_semantics=("parallel", "arbitrary"), vmem_limit_bytes=VMEM_LIMIT),
    )(proj, proj, proj, d_xbc, d_xbc, conv_w, conv_b[None])
    return d_x, d_w, d_b[0]


def _s5_boundary_scan(z, lam_re, lam_im, name, reverse=False):
    n_chunks, groups, width = z.shape
    tn = _tile(n_chunks, 128, 1)
    blocks = n_chunks // tn
    lam_a = jnp.concatenate([lam_re, lam_re], axis=1)
    lam_b = jnp.concatenate([-lam_im, lam_im], axis=1)

    def body(z_ref, a_ref, b_ref, x_ref, carry_ref):
        @pl.when(pl.program_id(0) == 0)
        def _():
            carry_ref[...] = jnp.zeros_like(carry_ref)

        a, b = a_ref[...], b_ref[...]

        def step(i, x):
            n = tn - 1 - i if reverse else i
            x_ref[n] = x
            return a * x + b * pltpu.roll(x, width // 2, 1) + z_ref[n]

        carry_ref[...] = lax.fori_loop(0, tn, step, carry_ref[...])

    block = pl.BlockSpec((tn, groups, width), (lambda i: (blocks - 1 - i, 0, 0)) if reverse else (lambda i: (i, 0, 0)))
    return pl.pallas_call(
        body, name=name, grid=(blocks,), in_specs=[block, _full_spec((groups, width)), _full_spec((groups, width))],
        out_specs=block, out_shape=jax.ShapeDtypeStruct(z.shape, F32), scratch_shapes=[pltpu.VMEM((groups, width), F32)],
        compiler_params=pltpu.CompilerParams(dimension_semantics=("arbitrary",), vmem_limit_bytes=VMEM_LIMIT),
    )(z, lam_a, lam_b)


_FLIPS = [(kx, ky, kc) for kx in (0, 1) for ky in (0, 1) for kc in (0, 1)][1:]


def _mesh_position():
    return lax.axis_index("x"), lax.axis_index("y"), lax.axis_index("c")


def _peer(pos, flip):
    return tuple((1 - p) if f else p for p, f in zip(pos, flip))


def _index(pos):
    return 4 * pos[0] + 2 * pos[1] + pos[2]


_ANY = pl.BlockSpec(memory_space=pl.ANY)


def _moved_shape(kind, x):
    return jax.ShapeDtypeStruct(((N_DEV,) + x.shape) if kind == "gather" else x.shape, x.dtype)


def _moves(kind, x_ref, out_ref, send_sems, recv_sems, local_sem):
    me = _mesh_position()
    source = (lambda pos: x_ref) if kind == "gather" else (lambda pos: x_ref.at[_index(pos)])
    local = pltpu.make_async_copy(source(me), out_ref.at[_index(me)], local_sem)
    outgoing, incoming = [], []
    for k, flip in enumerate(_FLIPS):
        peer = _peer(me, flip)
        copy = lambda slot: pltpu.make_async_remote_copy(
            src_ref=source(peer), dst_ref=out_ref.at[_index(slot)], send_sem=send_sems.at[k], recv_sem=recv_sems.at[k],
            device_id=peer, device_id_type=pl.DeviceIdType.MESH)
        outgoing.append(copy(me))
        incoming.append(copy(peer))
    return local, outgoing, incoming


def _start(moves):
    local, outgoing, _ = moves
    local.start()
    for cp in outgoing:
        cp.start()


def _finish(moves):
    local, outgoing, incoming = moves
    for cp in incoming:
        cp.wait_recv()
    for cp in outgoing:
        cp.wait_send()
    local.wait()


def _collective(kind, x, name):
    def body(x_ref, out_ref, send_sems, recv_sems, local_sem):
        moves = _moves(kind, x_ref, out_ref, send_sems, recv_sems, local_sem)
        _start(moves)
        _finish(moves)

    return pl.pallas_call(
        body, name=name, in_specs=[_ANY], out_specs=_ANY, out_shape=_moved_shape(kind, x),
        scratch_shapes=[pltpu.SemaphoreType.DMA((N_DEV - 1,)), pltpu.SemaphoreType.DMA((N_DEV - 1,)), pltpu.SemaphoreType.DMA],
        compiler_params=pltpu.CompilerParams(has_side_effects=True),
    )(x)


def _adamw(parts, w, m, v, name):
    n_parts = parts.shape[0]
    layers, rows, cols = w.shape
    tr = _tile(rows, 256, 8)

    def body(p_ref, w_ref, m_ref, v_ref, g_ref, d_ref, mo_ref, vo_ref):
        g = p_ref[0].astype(F32)
        for s in range(1, n_parts):
            g = g + p_ref[s].astype(F32)
        m_new = ADAM_B1 * m_ref[...] + (1.0 - ADAM_B1) * g
        v_new = ADAM_B2 * v_ref[...] + (1.0 - ADAM_B2) * (g * g)
        m_hat = m_new / (1.0 - ADAM_B1 ** ADAM_STEP)
        v_hat = v_new / (1.0 - ADAM_B2 ** ADAM_STEP)
        g_ref[...] = g
        d_ref[...] = -ADAM_LR * (m_hat / (jnp.sqrt(v_hat) + ADAM_EPS) + ADAM_WD * w_ref[...])
        mo_ref[...] = m_new
        vo_ref[...] = v_new

    blk = pl.BlockSpec((None, tr, cols), lambda l, i: (l, i, 0))
    shape = jax.ShapeDtypeStruct(w.shape, F32)
    return pl.pallas_call(
        body, name=name, grid=(layers, rows // tr),
        in_specs=[pl.BlockSpec((n_parts, None, tr, cols), lambda l, i: (0, l, i, 0)), blk, blk, blk],
        out_specs=[blk, blk, blk, blk], out_shape=[shape, shape, shape, shape],
        compiler_params=pltpu.CompilerParams(dimension_semantics=("parallel", "parallel"), vmem_limit_bytes=VMEM_LIMIT),
    )(parts, w, m, v)


def _sum_parts(parts, name):
    _, rows, cols = parts.shape
    tr = _tile(rows, 256, 8)

    def body(p_ref, o_ref):
        total = p_ref[0]
        for s in range(1, N_DEV):
            total = total + p_ref[s]
        o_ref[...] = total

    return pl.pallas_call(
        body, name=name, grid=(rows // tr,),
        in_specs=[pl.BlockSpec((N_DEV, tr, cols), lambda i: (0, i, 0))], out_specs=pl.BlockSpec((tr, cols), lambda i: (i, 0)),
        out_shape=jax.ShapeDtypeStruct((rows, cols), parts.dtype),
        compiler_params=pltpu.CompilerParams(dimension_semantics=("parallel",), vmem_limit_bytes=VMEM_LIMIT),
    )(parts)


SWIGLU_BUFFERS = 3


def _row_tile(rows):
    return _tile(rows, 512, 16)


def _row_spec(rows, cols, block=0):
    return pl.BlockSpec((_row_tile(rows), cols), lambda i: (i, block))


def _rows_params(accumulates):
    return pltpu.CompilerParams(dimension_semantics=("arbitrary" if accumulates else "parallel",),
                                vmem_limit_bytes=VMEM_LIMIT)


def _gu_matmul(x, w, *, name, cargo=()):
    rows, k_dim = x.shape
    tm, tn = _tile(rows, 512, 16), _tile(FFN_HIDDEN, 1408, LANES)
    nj = FFN_HIDDEN // tn

    def body(x_ref, wg_ref, wu_ref, g_ref, u_ref, a_ref):
        xb = x_ref[...].astype(_MXU_DTYPE)
        g = jnp.dot(xb, wg_ref[...].astype(_MXU_DTYPE), preferred_element_type=F32)
        u = jnp.dot(xb, wu_ref[...].astype(_MXU_DTYPE), preferred_element_type=F32)
        g_ref[...] = g.astype(g_ref.dtype)
        u_ref[...] = u.astype(u_ref.dtype)
        a_ref[...] = (_silu(g) * u).astype(a_ref.dtype)

    out = pl.BlockSpec((tm, tn), lambda i, j: (i, j))
    shape = jax.ShapeDtypeStruct((rows, FFN_HIDDEN), BF16)
    return _cargo_call(
        body, cargo, name=name, grid=(rows // tm, nj),
        in_specs=[pl.BlockSpec((tm, k_dim), lambda i, j: (i, 0)), pl.BlockSpec((k_dim, tn), lambda i, j: (0, j)),
                  pl.BlockSpec((k_dim, tn), lambda i, j: (0, nj + j))],
        out_specs=[out, out, out], out_shape=[shape, shape, shape], scratch_shapes=[], semantics=("parallel", "parallel"),
    )(x, w, w)


def _swiglu_bwd(g, u, d_act, name):
    rows = g.shape[0]
    tr = _tile(rows, 256, 16)
    steps = rows // tr
    ring = min(SWIGLU_BUFFERS, steps)

    def body(g_hbm, u_hbm, d_hbm, o_hbm, g_buf, u_buf, d_buf, o_buf, in_sems, out_sems):
        streams = ((g_hbm, g_buf), (u_hbm, u_buf), (d_hbm, d_buf))

        def reads(i, slot):
            at = pl.ds(pl.multiple_of(i * tr, tr), tr)
            return [pltpu.make_async_copy(src.at[at], buf.at[slot], in_sems.at[k, slot])
                    for k, (src, buf) in enumerate(streams)]

        def write(i, slot):
            return pltpu.make_async_copy(o_buf.at[slot], o_hbm.at[pl.ds(pl.multiple_of(i * tr, tr), tr)], out_sems.at[slot])

        for s in range(ring):
            for cp in reads(s, s):
                cp.start()

        def step(i, carry):
            slot = i % ring
            for cp in reads(i, slot):
                cp.wait()

            @pl.when(i >= ring)
            def _():
                write(i - ring, slot).wait()

            gv, uv, dv = g_buf[slot].astype(F32), u_buf[slot].astype(F32), d_buf[slot].astype(F32)
            sig = 1.0 / (1.0 + jnp.exp(-gv))
            o_buf[slot, :, :FFN_HIDDEN] = (dv * uv * sig * (1.0 + gv * (1.0 - sig))).astype(o_buf.dtype)
            o_buf[slot, :, FFN_HIDDEN:] = (dv * gv * sig).astype(o_buf.dtype)
            write(i, slot).start()

            @pl.when(i + ring < steps)
            def _():
                for cp in reads(i + ring, slot):
                    cp.start()

            return carry

        lax.fori_loop(0, steps, step, 0)
        for s in range(ring):
            last = steps - ring + s
            write(last, last % ring).wait()

    block = lambda cols: pltpu.VMEM((ring, tr, cols), BF16)
    return pl.pallas_call(
        body, name=name, in_specs=[_ANY] * 3, out_specs=_ANY,
        out_shape=jax.ShapeDtypeStruct((rows, 2 * FFN_HIDDEN), BF16),
        scratch_shapes=[block(FFN_HIDDEN)] * 3 + [block(2 * FFN_HIDDEN), pltpu.SemaphoreType.DMA((3, ring)),
                                                  pltpu.SemaphoreType.DMA((ring,))],
        compiler_params=pltpu.CompilerParams(vmem_limit_bytes=VMEM_LIMIT))(g, u, d_act)


def _add_norm_fwd(h, y, gain, name):
    rows = h.shape[0]

    def body(*refs):
        if y is None:
            h_ref, g_ref, n_ref = refs
            x = h_ref[...]
        else:
            h_ref, y_ref, g_ref, s_ref, n_ref = refs
            x = h_ref[...] + y_ref[...]
            s_ref[...] = x
        n_ref[...] = (x * lax.rsqrt(jnp.mean(x * x, axis=-1, keepdims=True) + EPS) * g_ref[...]).astype(n_ref.dtype)

    row = _row_spec(rows, D_MODEL)
    ins = [h] if y is None else [h, y]
    out_shape = [jax.ShapeDtypeStruct((rows, D_MODEL), BF16)]
    if y is not None:
        out_shape = [jax.ShapeDtypeStruct((rows, D_MODEL), F32)] + out_shape
    res = pl.pallas_call(
        body, name=name, grid=(rows // _row_tile(rows),),
        in_specs=[row] * len(ins) + [_full_spec((1, D_MODEL))], out_specs=[row] * len(out_shape), out_shape=out_shape,
        compiler_params=_rows_params(False))(*ins, gain[None])
    return (h, res[0]) if y is None else (res[0], res[1])


def _norm_bwd(x, gain, d_n, d_skip, name):
    rows = x.shape[0]
    d_parts = d_n if isinstance(d_n, tuple) else (d_n,)

    def body(x_ref, g_ref, *refs):
        dn_refs, (ds_ref, dx_ref, dg_ref) = refs[:len(d_parts)], refs[len(d_parts):]

        @pl.when(pl.program_id(0) == 0)
        def _():
            dg_ref[...] = jnp.zeros_like(dg_ref)

        x, dn = x_ref[...], sum(r[...].astype(F32) for r in dn_refs)
        r = lax.rsqrt(jnp.mean(x * x, axis=-1, keepdims=True) + EPS)
        gd = g_ref[...] * dn
        dx_ref[...] = r * gd - x * (r * r * r) * jnp.mean(x * gd, axis=-1, keepdims=True) + ds_ref[...]
        dg_ref[...] += jnp.sum(x * r * dn, axis=0, keepdims=True)

    row = _row_spec(rows, D_MODEL)
    dx, dg = pl.pallas_call(
        body, name=name, grid=(rows // _row_tile(rows),),
        in_specs=[row, _full_spec((1, D_MODEL))] + [row] * (len(d_parts) + 1), out_specs=[row, _full_spec((1, D_MODEL))],
        out_shape=[jax.ShapeDtypeStruct((rows, D_MODEL), F32), jax.ShapeDtypeStruct((1, D_MODEL), F32)],
        compiler_params=_rows_params(True))(x, gain[None], *d_parts, d_skip)
    return dx, dg[0]


_GELU_C, _GELU_A = math.sqrt(2.0 / math.pi), 0.044715


def _s5_gate_fwd(y, u, d_skip, name):
    rows = y.shape[0]

    def body(y_ref, u_ref, d_ref, o_ref):
        x = y_ref[...].astype(F32) + d_ref[...] * u_ref[...].astype(F32)
        o_ref[...] = (0.5 * x * (1.0 + jnp.tanh(_GELU_C * (x + _GELU_A * x * x * x)))).astype(o_ref.dtype)

    row = _row_spec(rows, D_MODEL)
    return pl.pallas_call(
        body, name=name, grid=(rows // _row_tile(rows),), in_specs=[row, row, _full_spec((1, D_MODEL))], out_specs=row,
        out_shape=jax.ShapeDtypeStruct((rows, D_MODEL), BF16), compiler_params=_rows_params(False))(y, u, d_skip[None])


def _s5_gate_bwd(y, u, d_skip, d_act, name):
    rows = y.shape[0]

    def body(y_ref, u_ref, d_ref, da_ref, dy_ref, du_ref, dd_ref):
        @pl.when(pl.program_id(0) == 0)
        def _():
            dd_ref[...] = jnp.zeros_like(dd_ref)

        u = u_ref[...].astype(F32)
        x = y_ref[...].astype(F32) + d_ref[...] * u
        t = jnp.tanh(_GELU_C * (x + _GELU_A * x * x * x))
        slope = 0.5 * (1.0 + t) + 0.5 * x * (1.0 - t * t) * _GELU_C * (1.0 + 3.0 * _GELU_A * x * x)
        dx = da_ref[...].astype(F32) * slope
        dy_ref[...] = dx.astype(dy_ref.dtype)
        du_ref[...] = (dx * d_ref[...]).astype(du_ref.dtype)
        dd_ref[...] += jnp.sum(dx * u, axis=0, keepdims=True)

    row = _row_spec(rows, D_MODEL)
    shape = jax.ShapeDtypeStruct((rows, D_MODEL), BF16)
    d_y, d_u, d_d = pl.pallas_call(
        body, name=name, grid=(rows // _row_tile(rows),), in_specs=[row, row, _full_spec((1, D_MODEL)), row],
        out_specs=[row, row, _full_spec((1, D_MODEL))], out_shape=[shape, shape, jax.ShapeDtypeStruct((1, D_MODEL), F32)],
        compiler_params=_rows_params(True))(y, u, d_skip[None], d_act)
    return d_y, d_u, d_d[0]


def _glu_fwd(vg, name):
    rows = vg.shape[0]

    def body(v_ref, g_ref, o_ref):
        o_ref[...] = v_ref[...] / (1.0 + jnp.exp(-g_ref[...]))

    return pl.pallas_call(
        body, name=name, grid=(rows // _row_tile(rows),),
        in_specs=[_row_spec(rows, D_MODEL, 0), _row_spec(rows, D_MODEL, 1)], out_specs=_row_spec(rows, D_MODEL),
        out_shape=jax.ShapeDtypeStruct((rows, D_MODEL), F32), compiler_params=_rows_params(False))(vg, vg)


def _glu_bwd(vg, d_out, name):
    rows = vg.shape[0]

    def body(v_ref, g_ref, d_ref, o_ref):
        sig = 1.0 / (1.0 + jnp.exp(-g_ref[...]))
        d = d_ref[...]
        o_ref[:, :D_MODEL] = (d * sig).astype(o_ref.dtype)
        o_ref[:, D_MODEL:] = (d * v_ref[...] * sig * (1.0 - sig)).astype(o_ref.dtype)

    return pl.pallas_call(
        body, name=name, grid=(rows // _row_tile(rows),),
        in_specs=[_row_spec(rows, D_MODEL, 0), _row_spec(rows, D_MODEL, 1), _row_spec(rows, D_MODEL)],
        out_specs=_row_spec(rows, 2 * D_MODEL), out_shape=jax.ShapeDtypeStruct((rows, 2 * D_MODEL), BF16),
        compiler_params=_rows_params(False))(vg, vg, d_out)


def _loss_head(h, gain, target, name):
    rows = h.shape[0]

    def body(x_ref, g_ref, t_ref, loss_ref, dx_ref, dg_ref):
        @pl.when(pl.program_id(0) == 0)
        def _():
            loss_ref[...] = jnp.zeros_like(loss_ref)
            dg_ref[...] = jnp.zeros_like(dg_ref)

        x = x_ref[...]
        r = lax.rsqrt(jnp.mean(x * x, axis=-1, keepdims=True) + EPS)
        err = x * r * g_ref[...] - t_ref[...]
        loss_ref[...] += 0.5 * jnp.sum(jnp.mean(err * err, axis=-1, keepdims=True), axis=0, keepdims=True)
        dy = err * (1.0 / D_MODEL)
        gd = g_ref[...] * dy
        dx_ref[...] = r * gd - x * (r * r * r) * jnp.mean(x * gd, axis=-1, keepdims=True)
        dg_ref[...] += jnp.sum(x * r * dy, axis=0, keepdims=True)

    row = _row_spec(rows, D_MODEL)
    loss, dx, dg = pl.pallas_call(
        body, name=name, grid=(rows // _row_tile(rows),),
        in_specs=[row, _full_spec((1, D_MODEL)), row], out_specs=[_full_spec((1, 1)), row, _full_spec((1, D_MODEL))],
        out_shape=[jax.ShapeDtypeStruct((1, 1), F32), jax.ShapeDtypeStruct((rows, D_MODEL), F32),
                   jax.ShapeDtypeStruct((1, D_MODEL), F32)],
        compiler_params=_rows_params(True))(h, gain[None], target)
    return loss[0, 0], dx, dg[0]


def _join_cols(blocks, n_out, name):
    _, layers, rows, n = blocks.shape
    tr = _tile(rows, 256, 16)

    def body(x_ref, o_ref):
        for d in range(N_DEV):
            o_ref[:, d * n:(d + 1) * n] = x_ref[d]
        if n_out > N_DEV * n:
            o_ref[:, N_DEV * n:] = jnp.zeros((tr, n_out - N_DEV * n), o_ref.dtype)

    return pl.pallas_call(
        body, name=name, grid=(layers, rows // tr),
        in_specs=[pl.BlockSpec((N_DEV, None, tr, n), lambda l, i: (0, l, i, 0))],
        out_specs=pl.BlockSpec((None, tr, n_out), lambda l, i: (l, i, 0)),
        out_shape=jax.ShapeDtypeStruct((layers, rows, n_out), blocks.dtype),
        compiler_params=pltpu.CompilerParams(dimension_semantics=("parallel", "parallel"), vmem_limit_bytes=VMEM_LIMIT),
    )(blocks)


def _split_cols(full, n, name):
    rows = full.shape[0]
    tr = _tile(rows, 256, 16)

    def body(x_ref, o_ref):
        for d in range(N_DEV):
            o_ref[d] = x_ref[:, d * n:(d + 1) * n]

    return pl.pallas_call(
        body, name=name, grid=(rows // tr,),
        in_specs=[pl.BlockSpec((tr, full.shape[1]), lambda i: (i, 0))],
        out_specs=pl.BlockSpec((N_DEV, tr, n), lambda i: (0, i, 0)),
        out_shape=jax.ShapeDtypeStruct((N_DEV, rows, n), full.dtype),
        compiler_params=pltpu.CompilerParams(dimension_semantics=("parallel",), vmem_limit_bytes=VMEM_LIMIT),
    )(full)


def _pack(arrays):
    flat = jnp.concatenate([a.reshape(-1) for a in arrays])
    unit = FLAT_COLS * FLAT_ROWS_ALIGN
    padded = -(-flat.shape[0] // unit) * unit
    return jnp.pad(flat, (0, padded - flat.shape[0])).reshape(-1, FLAT_COLS)


def _unpack(flat, shapes, lead=()):
    flat = flat.reshape(lead + (-1,))
    out, off = [], 0
    for shape in shapes:
        n = math.prod(shape)
        out.append(flat[..., off:off + n].reshape(lead + tuple(shape)))
        off += n
    return out


def _join(blocks, axis):
    moved = jnp.moveaxis(blocks, 0, axis)
    shape = list(moved.shape)
    shape[axis:axis + 2] = [shape[axis] * shape[axis + 1]]
    return moved.reshape(shape)


def _own_shard(full, axis, position):
    n = full.shape[axis] // N_DEV
    return lax.dynamic_slice_in_dim(full, position * n, n, axis)


def _s5_operators(log_dt, a_re, a_im, b_re, b_im, c_re, c_im):
    t = S5_CHUNK
    hi = lax.Precision.HIGHEST
    step = jnp.exp(log_dt)[:, None]
    mag = jnp.exp(step * a_re)
    abar_re = mag * jnp.cos(step * a_im)
    abar_im = mag * jnp.sin(step * a_im)
    den = a_re * a_re + a_im * a_im
    f_re = ((abar_re - 1.0) * a_re + abar_im * a_im) / den
    f_im = (abar_im * a_re - (abar_re - 1.0) * a_im) / den
    bb_re = f_re[..., None] * b_re - f_im[..., None] * b_im
    bb_im = f_re[..., None] * b_im + f_im[..., None] * b_re
    j = jnp.arange(t + 1, dtype=F32)[:, None, None]
    pmag = jnp.exp(j * (step * a_re))
    pw_re = pmag * jnp.cos(j * (step * a_im))
    pw_im = pmag * jnp.sin(j * (step * a_im))
    cl_re = c_re[None] * pw_re[:t, :, None, :] - c_im[None] * pw_im[:t, :, None, :]
    cl_im = c_re[None] * pw_im[:t, :, None, :] + c_im[None] * pw_re[:t, :, None, :]
    kern = (jnp.einsum('jgcp,gpk->jgck', cl_re, bb_re, precision=hi)
            - jnp.einsum('jgcp,gpk->jgck', cl_im, bb_im, precision=hi))
    rp_re, rp_im = pw_re[:t][::-1], pw_im[:t][::-1]
    wz_re = rp_re[:, :, :, None] * bb_re[None] - rp_im[:, :, :, None] * bb_im[None]
    wz_im = rp_re[:, :, :, None] * bb_im[None] + rp_im[:, :, :, None] * bb_re[None]
    w_z = jnp.concatenate([wz_re, wz_im], axis=2).transpose(1, 0, 3, 2).reshape(S5_GROUPS, t * S5_GROUP, 2 * S5_STATE)
    cy_re = c_re[None] * pw_re[1:, :, None, :] - c_im[None] * pw_im[1:, :, None, :]
    cy_im = c_re[None] * pw_im[1:, :, None, :] + c_im[None] * pw_re[1:, :, None, :]
    w_y = jnp.concatenate([cy_re, -cy_im], axis=3).transpose(1, 3, 0, 2).reshape(S5_GROUPS, 2 * S5_STATE, t * S5_GROUP)
    return kern, w_z, w_y, pw_re[t], pw_im[t]


def _s5_lag_selector():
    t = S5_CHUNK
    lag = jnp.arange(t)[:, None] - jnp.arange(t)[None, :]
    return (lag[:, :, None] == jnp.arange(t)[None, None, :]).astype(F32).reshape(t * t, t)


def _s5_toeplitz(kern, tag):
    t = S5_CHUNK
    sel = _s5_lag_selector()
    flat = _matmul(sel, kern.reshape(t, -1), out_dtype=BF16, name=tag + "_toeplitz")
    toep = flat.reshape(t, t, S5_GROUPS, S5_GROUP, S5_GROUP).transpose(2, 1, 4, 0, 3)
    toep = toep.reshape(S5_GROUPS, t * S5_GROUP, t * S5_GROUP)

    def backward(d_toep):
        d_flat = d_toep.reshape(S5_GROUPS, t, S5_GROUP, t, S5_GROUP).transpose(3, 1, 0, 4, 2).reshape(t * t, -1)
        return _matmul(sel, d_flat, ta=True, name=tag + "_toeplitz_dw").reshape(kern.shape)

    return toep, backward


_BIG = [("gla_w_in", 2), ("gla_w_out", 1), ("ssd_w_in", 2), ("ssd_w_out", 1), ("s5_w_glu", 2), ("ffn_w_gu", 2),
        ("ffn_w_down", 1)]
_PADDED_COLS = {"gla_w_in": GLA_PROJ, "ssd_w_in": SSD_PROJ}


class _Traffic:
    LINK_BYTES_PER_SECOND = 7.0e10
    MATMUL_FLOPS = 7.0e14

    def __init__(self, shards, plan):
        self.shards, self.plan = shards, plan
        self.position = 0
        self.queue = []
        self.weights, self.received = {}, {}
        self.early = None
        self.standalone = self.serial = 0
        for key in plan:
            self._request(key)

    def _request(self, key):
        shard = self.shards[key]
        seconds = (N_DEV - 1) * shard.size * shard.dtype.itemsize / self.LINK_BYTES_PER_SECOND
        self._enqueue("gather", shard, seconds, key, lambda blocks: self.weights.__setitem__(key, self._assemble(key, blocks)))

    def _enqueue(self, kind, x, seconds, key, deliver):
        self.queue.append((kind, x, seconds, key, deliver, self.serial))
        self.serial += 1

    @staticmethod
    def _assemble(key, blocks):
        name, layer = key
        if dict(_BIG)[name] == 1:
            return blocks.reshape((N_DEV * blocks.shape[1], blocks.shape[2]))
        n_out = _PADDED_COLS.get(name, N_DEV * blocks.shape[2])
        return _join_cols(blocks[:, None], n_out, f"join_{name}_{layer}")[0]

    def take(self, key):
        assert key == self.plan[self.position], (key, self.plan[self.position])
        self.position += 1
        while key not in self.weights:
            self._alone(self.queue.pop(0))
        return self.weights[key]

    def run(self, seconds, call, more_carriers_follow=False):
        riders, waiting, left = [], [], seconds
        for item in self.queue:
            if item[2] <= left:
                riders.append(item)
                left -= item[2]
            else:
                waiting.append(item)
        due = [item for item in waiting if item[3] is not None and self.position < len(self.plan)
               and item[3] == self.plan[self.position]]
        if due and not more_carriers_follow:
            left = seconds - due[0][2]
            kept = []
            for item in riders:
                if item[2] <= left:
                    kept.append(item)
                    left -= item[2]
                else:
                    waiting.append(item)
            riders = due + kept
            waiting = [item for item in waiting if item is not due[0]]
            waiting.sort(key=lambda item: item[5])
        self.queue = waiting
        if not riders:
            return call(())
        results, moved = call([(kind, x) for kind, x, *_ in riders])
        for item, y in zip(riders, moved):
            item[4](y)
        return results

    def matmul(self, a, b, more_carriers_follow=False, **kw):
        m, n = (a.shape[-1] if kw.get("ta") else a.shape[-2]), (b.shape[-2] if kw.get("tb") else b.shape[-1])
        k = a.shape[-2] if kw.get("ta") else a.shape[-1]
        return self.run(2.0 * m * n * k / self.MATMUL_FLOPS, lambda cargo: _matmul(a, b, cargo=cargo, **kw),
                        more_carriers_follow)

    def send_gradient(self, key, dw):
        name, layer = key
        shard = self.shards[key]
        if dict(_BIG)[name] == 1:
            blocks = dw.reshape((N_DEV,) + shard.shape)
        else:
            blocks = _split_cols(dw, shard.shape[1], f"split_{name}_{layer}")
        seconds = (N_DEV - 1) * shard.size * shard.dtype.itemsize / self.LINK_BYTES_PER_SECOND
        self._enqueue("exchange", blocks, seconds, None, lambda parts: self.received.__setitem__(key, parts))

    def gather_early(self, packed):
        seconds = (N_DEV - 1) * packed.size * packed.dtype.itemsize / self.LINK_BYTES_PER_SECOND
        self._enqueue("gather", packed, seconds, None, lambda parts: setattr(self, "early", parts))

    def _alone(self, item):
        kind, x, _, _, deliver, _ = item
        deliver(_collective(kind, x, f"{kind}_alone_{self.standalone}"))
        self.standalone += 1

    def flush(self):
        for item in self.queue:
            self._alone(item)
        self.queue = []


_GLA_FWD_SECONDS, _GLA_BWD_SECONDS, _SSD_FWD_SECONDS, _SSD_BWD_SECONDS = 1.25e-6, 3.4e-6, 3.5e-6, 14e-6


def _linear(x, w, tag, out_dtype=F32, dx_dtype=F32, more_carriers_follow=False):
    traffic, key = w
    weight = traffic.take(key)
    y = traffic.matmul(x, weight, more_carriers_follow, out_dtype=out_dtype, name=tag + "_fwd")

    def backward(dy):
        dx = traffic.matmul(dy, weight, tb=True, out_dtype=dx_dtype, name=tag + "_dx")
        traffic.send_gradient(key, traffic.matmul(x, dy, ta=True, out_dtype=BF16, name=tag + "_dw"))
        return dx

    return y, backward


def _gated_linear(x, w, tag):
    traffic, key = w
    weight = traffic.take(key)
    seconds = 2.0 * x.shape[0] * x.shape[1] * weight.shape[1] / traffic.MATMUL_FLOPS
    outs = traffic.run(seconds, lambda cargo: _gu_matmul(x, weight, name=tag + "_fwd", cargo=cargo))

    def backward(d_gu):
        dx = traffic.matmul(d_gu, weight, tb=True, name=tag + "_dx")
        traffic.send_gradient(key, traffic.matmul(x, d_gu, ta=True, out_dtype=BF16, name=tag + "_dw"))
        return dx

    return tuple(outs), backward


def _gla_mixer(hn, p, tag):
    traffic, chunks = p["w_in"][0], hn.shape[0] // CHUNK
    w_a2 = jnp.pad(p["w_a2"], ((0, LANES - GLA_RANK), (0, 0)))
    b_a, norm_g = p["b_a"][None], p["norm_g"][None]
    proj, lin_in = _linear(hn, p["w_in"], tag + "_in", more_carriers_follow=True)
    o, sprev = traffic.run(chunks * _GLA_FWD_SECONDS, lambda cargo: _gla_core_fwd(proj, w_a2, b_a, norm_g, cargo))
    y, lin_out = _linear(o, p["w_out"], tag + "_out")

    def backward(dy):
        d_o = lin_out(dy)
        d_proj, d_wa, d_ba, d_ng = traffic.run(chunks * _GLA_BWD_SECONDS,
                                               lambda cargo: _gla_core_bwd(proj, sprev, d_o, w_a2, b_a, norm_g, cargo))
        return lin_in(d_proj), dict(w_a2=d_wa[:GLA_RANK], b_a=d_ba[0], norm_g=d_ng[0])

    return y, backward


def _ssd_mixer(hn, p, tag):
    pad = lambda a: jnp.pad(a[None], ((0, 0), (0, LANES - SSD_HEADS)))
    dt_bias, a_log, d_skip, norm_g = pad(p["dt_bias"]), pad(p["a_log"]), pad(p["d"]), p["norm_g"][None]
    traffic, chunks = p["w_in"][0], hn.shape[0] // CHUNK
    proj, lin_in = _linear(hn, p["w_in"], tag + "_in", more_carriers_follow=True)
    xbc = _ssd_conv_fwd(proj, p["conv_w"], p["conv_b"], tag + "_conv")
    o, hprev = traffic.run(chunks * _SSD_FWD_SECONDS,
                           lambda cargo: _ssd_core_fwd(proj, xbc, dt_bias, a_log, d_skip, norm_g, cargo))
    y, lin_out = _linear(o, p["w_out"], tag + "_out")

    def backward(dy):
        d_o = lin_out(dy)
        d_z, d_xbc, d_dt, d_db, d_al, d_ds, d_ng = traffic.run(
            chunks * _SSD_BWD_SECONDS, lambda cargo: _ssd_core_bwd(proj, xbc, hprev, d_o, dt_bias, a_log, d_skip, norm_g, cargo))
        d_pre, d_cw, d_cb = _ssd_conv_bwd(proj, d_xbc, p["conv_w"], p["conv_b"], tag + "_conv_bwd")
        d_hn = lin_in(jnp.concatenate([d_z, d_pre, d_dt], axis=1))
        return d_hn, dict(conv_w=d_cw, conv_b=d_cb, dt_bias=d_db[0, :SSD_HEADS], a_log=d_al[0, :SSD_HEADS],
                          d=d_ds[0, :SSD_HEADS], norm_g=d_ng[0])

    return y, backward


def _s5_mixer(hn, p, tag):
    seq = hn.shape[0]
    t, n_chunks = S5_CHUNK, hn.shape[0] // S5_CHUNK
    names = ("log_dt", "a_re", "a_im", "b_re", "b_im", "c_re", "c_im")
    (kern, w_z, w_y, lam_re, lam_im), ops_vjp = jax.vjp(_s5_operators, *[p[k] for k in names])
    toep, toep_bwd = _s5_toeplitz(kern, tag)
    to_groups = lambda a: a.reshape(n_chunks, t, S5_GROUPS, S5_GROUP).transpose(2, 0, 1, 3).reshape(S5_GROUPS, n_chunks, t * S5_GROUP)
    from_groups = lambda a: a.reshape(S5_GROUPS, n_chunks, t, S5_GROUP).transpose(1, 2, 0, 3).reshape(seq, D_MODEL)
    ug = to_groups(hn)
    z = _matmul(ug, w_z, name=tag + "_z")
    x_before = _s5_boundary_scan(z.transpose(1, 0, 2), lam_re, lam_im, tag + "_scan")
    xprev = x_before.transpose(1, 0, 2)
    tw = t * S5_GROUP
    ux = jnp.concatenate([ug, xprev.astype(BF16)], axis=2)
    yg = _matmul(ux, jnp.concatenate([toep, w_y.astype(BF16)], axis=1), out_dtype=BF16, name=tag + "_y")
    y = from_groups(yg)
    vg, lin_glu = _linear(_s5_gate_fwd(y, hn, p["d"], tag + "_gate"), p["w_glu"], tag + "_glu", dx_dtype=BF16)
    out = _glu_fwd(vg, tag + "_glu_gate")

    def backward(dy):
        d_act = lin_glu(_glu_bwd(vg, dy, tag + "_glu_gate_bwd"))
        d_y, d_u, d_d = _s5_gate_bwd(y, hn, p["d"], d_act, tag + "_gate_bwd")
        d_yg = to_groups(d_y)
        d_xprev = _matmul(d_yg, w_y, tb=True, name=tag + "_inter_dx").transpose(1, 0, 2)
        d_wy = _matmul(xprev, d_yg, ta=True, name=tag + "_inter_dw")
        dz = _s5_boundary_scan(d_xprev, lam_re, -lam_im, tag + "_scan_bwd", reverse=True)
        x_re, x_im, dz_re, dz_im = (x_before[..., :S5_STATE], x_before[..., S5_STATE:], dz[..., :S5_STATE],
                                    dz[..., S5_STATE:])
        d_lam_re = jnp.sum(x_re * dz_re + x_im * dz_im, axis=0)
        d_lam_im = jnp.sum(x_re * dz_im - x_im * dz_re, axis=0)
        dyz = jnp.concatenate([d_yg, dz.transpose(1, 0, 2).astype(BF16)], axis=2)
        d_ug = _matmul(dyz, jnp.concatenate([toep, w_z.astype(BF16)], axis=2), tb=True, out_dtype=BF16, name=tag + "_du")
        d_ops = _matmul(ug, dyz, ta=True, name=tag + "_dw")
        grads = dict(zip(names, ops_vjp((toep_bwd(d_ops[..., :tw]), d_ops[..., tw:], d_wy, d_lam_re, d_lam_im))))
        grads.update(d=d_d)
        return (d_u, from_groups(d_ug)), grads

    return out, backward


_SMALL =[("gla_w_a2", 2), ("gla_b_a", 1), ("gla_norm_g", 1), ("ssd_conv_w", 2), ("s5_d", 1)]
_REPLICATED = ["norm_mix_g", "norm_ffn_g", "ssd_conv_b", "ssd_dt_bias", "ssd_a_log", "ssd_d", "ssd_norm_g", "s5_log_dt",
               "s5_a_re", "s5_a_im", "s5_b_re", "s5_b_im", "s5_c_re", "s5_c_im", "final_norm_g"]
_EARLY_SMALL = [n for n in [s for s, _ in _SMALL] + _REPLICATED if n.startswith("s5_")]
_WEIGHTS = ['norm_mix_g', 'norm_ffn_g', 'gla_w_in', 'gla_w_a2', 'gla_b_a', 'gla_norm_g', 'gla_w_out', 'ssd_w_in',
            'ssd_conv_w', 'ssd_conv_b', 'ssd_dt_bias', 'ssd_a_log', 'ssd_d', 'ssd_norm_g', 'ssd_w_out', 's5_log_dt',
            's5_a_re', 's5_a_im', 's5_b_re', 's5_b_im', 's5_c_re', 's5_c_im', 's5_d', 's5_w_glu', 'ffn_w_gu', 'ffn_w_down',
            'final_norm_g']


def _gather_small(local):
    shapes = [local[n].shape for n, _ in _SMALL]
    blocks = _collective("gather", _pack([local[n] for n, _ in _SMALL]), "gather_vectors")
    parts = _unpack(blocks, shapes, lead=(N_DEV,))
    return {n: _join(part, axis) for (n, axis), part in zip(_SMALL, parts)}


def _forward_plan():
    plan = []
    for i in range(DEPTH):
        j = i // 3
        plan += [[("gla_w_in", j), ("gla_w_out", j)], [("ssd_w_in", j), ("ssd_w_out", j)], [("s5_w_glu", j)]][i % 3]
        plan += [("ffn_w_gu", i), ("ffn_w_down", i)]
    return plan


def _forward_backward(x, target, w, traffic):
    big = lambda name, j: (traffic, (name, j))
    gla = lambda j: dict(w_in=big("gla_w_in", j), w_a2=w["gla_w_a2"][j], b_a=w["gla_b_a"][j], norm_g=w["gla_norm_g"][j],
                         w_out=big("gla_w_out", j))
    ssd = lambda j: dict(w_in=big("ssd_w_in", j), conv_w=w["ssd_conv_w"][j], conv_b=w["ssd_conv_b"][j],
                         dt_bias=w["ssd_dt_bias"][j], a_log=w["ssd_a_log"][j], d=w["ssd_d"][j], norm_g=w["ssd_norm_g"][j],
                         w_out=big("ssd_w_out", j))
    s5 = lambda j: dict(log_dt=w["s5_log_dt"][j], a_re=w["s5_a_re"][j], a_im=w["s5_a_im"][j], b_re=w["s5_b_re"][j],
                        b_im=w["s5_b_im"][j], c_re=w["s5_c_re"][j], c_im=w["s5_c_im"][j], d=w["s5_d"][j],
                        w_glu=big("s5_w_glu", j))
    mixers = [("gla", _gla_mixer, gla), ("ssd", _ssd_mixer, ssd), ("s5", _s5_mixer, s5)]
    base, delta = x, None
    tape = []
    for i in range(DEPTH):
        kind, mixer, params = mixers[i % 3]
        j = i // 3
        h, hn = _add_norm_fwd(base, delta, w["norm_mix_g"][i], f"l{i}_norm_mix")
        y, mixer_bwd = mixer(hn, params(j), f"l{i}_{kind}")
        h_mid, hn2 = _add_norm_fwd(h, y, w["norm_ffn_g"][i], f"l{i}_norm_ffn")
        (g, u, act), gu_bwd = _gated_linear(hn2, big("ffn_w_gu", i), f"l{i}_ffn_gu")
        delta, down_bwd = _linear(act, big("ffn_w_down", i), f"l{i}_ffn_down", dx_dtype=BF16)
        base = h_mid
        tape.append((kind, j, h, mixer_bwd, h_mid, gu_bwd, (g, u), down_bwd))
    loss, d_h, d_final_g = _loss_head(base + delta, w["final_norm_g"], target, "loss_head")

    grads = {n: [None] * w[n].shape[0] for n in w if n != "final_norm_g"}
    grads["final_norm_g"] = d_final_g
    for i in reversed(range(DEPTH)):
        kind, j, h, mixer_bwd, h_mid, gu_bwd, (g, u), down_bwd = tape[i]
        d_gu = _swiglu_bwd(g, u, down_bwd(d_h), f"l{i}_swiglu_bwd")
        d_mid, grads["norm_ffn_g"][i] = _norm_bwd(h_mid, w["norm_ffn_g"][i], gu_bwd(d_gu), d_h, f"l{i}_norm_ffn_bwd")
        d_hn, mixer_grads = mixer_bwd(d_mid)
        for k, g in mixer_grads.items():
            grads[f"{kind}_{k}"][j] = g
        if kind == "s5" and all(g is not None for n in _EARLY_SMALL for g in grads[n]):
            traffic.gather_early(_pack([jnp.stack(grads[n]) for n in _EARLY_SMALL]))
        d_h, grads["norm_mix_g"][i] = _norm_bwd(h, w["norm_mix_g"][i], d_hn, d_mid, f"l{i}_norm_mix_bwd")
    return loss, d_h, grads


def kernel(x, norm_mix_g, norm_ffn_g, gla_w_in, gla_w_a2, gla_b_a, gla_norm_g, gla_w_out, ssd_w_in, ssd_conv_w, ssd_conv_b, ssd_dt_bias, ssd_a_log, ssd_d, ssd_norm_g, ssd_w_out, s5_log_dt, s5_a_re, s5_a_im, s5_b_re, s5_b_im, s5_c_re, s5_c_im, s5_d, s5_w_glu, ffn_w_gu, ffn_w_down, final_norm_g, loss_target, m_norm_mix_g, m_norm_ffn_g, m_gla_w_in, m_gla_w_a2, m_gla_b_a, m_gla_norm_g, m_gla_w_out, m_ssd_w_in, m_ssd_conv_w, m_ssd_conv_b, m_ssd_dt_bias, m_ssd_a_log, m_ssd_d, m_ssd_norm_g, m_ssd_w_out, m_s5_log_dt, m_s5_a_re, m_s5_a_im, m_s5_b_re, m_s5_b_im, m_s5_c_re, m_s5_c_im, m_s5_d, m_s5_w_glu, m_ffn_w_gu, m_ffn_w_down, m_final_norm_g, v_norm_mix_g, v_norm_ffn_g, v_gla_w_in, v_gla_w_a2, v_gla_b_a, v_gla_norm_g, v_gla_w_out, v_ssd_w_in, v_ssd_conv_w, v_ssd_conv_b, v_ssd_dt_bias, v_ssd_a_log, v_ssd_d, v_ssd_norm_g, v_ssd_w_out, v_s5_log_dt, v_s5_a_re, v_s5_a_im, v_s5_b_re, v_s5_b_im, v_s5_c_re, v_s5_c_im, v_s5_d, v_s5_w_glu, v_ffn_w_gu, v_ffn_w_down, v_final_norm_g):
    args = locals()
    local = {n: args[n] for n in _WEIGHTS}
    moment_m = {n: args["m_" + n] for n in _WEIGHTS}
    moment_v = {n: args["v_" + n] for n in _WEIGHTS}

    shards = {(n, layer): local[n][layer].astype(BF16) for n, _ in _BIG for layer in range(local[n].shape[0])}
    traffic = _Traffic(shards, _forward_plan())
    full = {n: local[n] for n in _REPLICATED}
    full.update(_gather_small(local))

    loss, d_x, grads = _forward_backward(x[0], loss_target[0], full, traffic)
    traffic.flush()
    loss = lax.psum(loss, ("x", "y", "c"))
    kinds = ("grad", "delta", "new_m", "new_v")
    out = {}

    for n, _ in _BIG:
        parts = jnp.stack([traffic.received[(n, layer)] for layer in range(local[n].shape[0])], axis=1)
        results = _adamw(parts, local[n], moment_m[n], moment_v[n], "adamw_" + n)
        out.update({f"{kind}_{n}": a for kind, a in zip(kinds, results)})

    small = [n for n, _ in _SMALL] + _REPLICATED
    stacked = lambda n: grads[n] if n == "final_norm_g" else jnp.stack(grads[n])
    late = [n for n in small if n not in _EARLY_SMALL]
    gathered = [(_EARLY_SMALL, traffic.early, "early"),
                (late, _collective("gather", _pack([stacked(n) for n in late]), "gather_small_gradients"), "late")]
    summed = {}
    for names, parts, tag in gathered:
        sums = _unpack(_sum_parts(parts, "sum_small_gradients_" + tag), [stacked(n).shape for n in names])
        summed.update(zip(names, sums))
    position = _index(_mesh_position())
    mine = [_own_shard(summed[n], axis, position) for n, axis in _SMALL] + [summed[n] for n in _REPLICATED]
    shapes = [local[n].shape for n in small]
    pk = lambda arrays: _pack(arrays)[None]
    results = _adamw(pk(mine)[None], pk([local[n] for n in small]), pk([moment_m[n] for n in small]),
                     pk([moment_v[n] for n in small]), "adamw_small")
    for kind, flat in zip(kinds, results):
        out.update({f"{kind}_{n}": a for n, a in zip(small, _unpack(flat[0], shapes))})

    return (loss, d_x[None], *[out[f"{kind}_{n}"] for kind in ("grad", "delta", "new_m", "new_v") for n in _WEIGHTS])
```
